```python
import math
import jax, jax.numpy as jnp
from jax import lax
import numpy as np

D_MODEL = 1024
BATCH = 8
SEQ = 8192
DEPTH = 2

GRID_W = 64
CTX_LEN = 256
N_MIXERS = 2
N_SSM_LAYERS = (DEPTH + 1) // 2
N_ATTN_LAYERS = DEPTH // 2
NORM_EPS = 1e-6

SSM_WIDTH = D_MODEL
SSM_GROUP = 16
SSM_GROUPS = SSM_WIDTH // SSM_GROUP
SSM_STATE = 64
DT_MIN = 1e-3
DT_MAX = 1e-1

HEAD_DIM = 64
N_Q_HEADS = D_MODEL // HEAD_DIM
N_KV_HEADS = 4
KV_REP = N_Q_HEADS // N_KV_HEADS
ATTN_WIDTH = N_Q_HEADS * HEAD_DIM
KV_WIDTH = N_KV_HEADS * HEAD_DIM
ATTN_IN = ATTN_WIDTH + 2 * KV_WIDTH + ATTN_WIDTH
Q_BLOCK = 128
ROPE_THETA = 10000.0
ROPE_AXIS_DIM = HEAD_DIM // 2

kernel_name = "hybrid_s5_gqa_prefix_dit"


def _rmsnorm(x, g):
    xf = x.astype(jnp.float32)
    y = xf * lax.rsqrt(jnp.mean(xf * xf, axis=-1, keepdims=True) + NORM_EPS)
    return (y * g.astype(jnp.float32)).astype(x.dtype)


def _rope_tables(L):
    rows = L // GRID_W
    row = jnp.repeat(jnp.arange(rows), GRID_W).astype(jnp.float32)
    col = jnp.tile(jnp.arange(GRID_W), rows).astype(jnp.float32)
    n_freq = ROPE_AXIS_DIM // 2
    freqs = ROPE_THETA ** (-jnp.arange(n_freq, dtype=jnp.float32) / n_freq)
    ang_r = row[:, None] * freqs[None]
    ang_c = col[:, None] * freqs[None]
    return (jnp.cos(ang_r), jnp.sin(ang_r), jnp.cos(ang_c), jnp.sin(ang_c))


def _rope_half(x, cos, sin):
    h = x.shape[-1] // 2
    x1, x2 = x[..., :h], x[..., h:]
    cs, sn = cos[:, None, :], sin[:, None, :]
    return jnp.concatenate([x1 * cs - x2 * sn, x2 * cs + x1 * sn], axis=-1)


def _rope_2d(x, rope):
    cos_r, sin_r, cos_c, sin_c = rope
    xf = x.astype(jnp.float32)
    out = jnp.concatenate([_rope_half(xf[..., :ROPE_AXIS_DIM], cos_r, sin_r),
                           _rope_half(xf[..., ROPE_AXIS_DIM:], cos_c, sin_c)], axis=-1)
    return out.astype(x.dtype)


def _linear_scan(bu, abar, reverse):
    L = bu.shape[1]
    a = jnp.broadcast_to(abar, (1, L) + abar.shape)

    def combine(e_i, e_j):
        a_i, b_i = e_i
        a_j, b_j = e_j
        return a_j * a_i, a_j * b_i + b_j

    _, h = lax.associative_scan(combine, (a, bu), axis=1, reverse=reverse)
    return h


def _s5_core(u_lat, u_ctx, a_re, a_im, log_dt, b_re, b_im, c_re, c_im, d_skip, with_ctx):
    B, L, E = u_lat.shape
    C = u_ctx.shape[1]
    f32 = jnp.float32
    ul = u_lat.astype(f32).reshape(B, L, SSM_GROUPS, SSM_GROUP).astype(jnp.complex64)
    uc = u_ctx.astype(f32).reshape(B, C, SSM_GROUPS, SSM_GROUP).astype(jnp.complex64)
    dsk = d_skip.astype(f32)
    y_lat = u_lat.astype(f32) * dsk
    y_ctx = u_ctx.astype(f32) * dsk if with_ctx else None
    for d in range(2):
        rev = d == 1
        lam = lax.complex(a_re[d].astype(f32), a_im[d].astype(f32))
        lam_dt = lam * jnp.exp(log_dt[d].astype(f32))[:, None]
        abar = jnp.exp(lam_dt)
        bmat = lax.complex(b_re[d].astype(f32), b_im[d].astype(f32))
        bbar = ((abar - 1.0) / lam)[..., None] * bmat
        cmat = lax.complex(c_re[d].astype(f32), c_im[d].astype(f32))
        h_ctx = _linear_scan(jnp.einsum('bcgh,gph->bcgp', uc, bbar), abar, rev)
        h0 = h_ctx[:, 0] if rev else h_ctx[:, -1]
        steps = jnp.arange(L, 0, -1) if rev else jnp.arange(1, L + 1)
        carry = jnp.exp(lam_dt[None] * steps.astype(f32)[:, None, None])
        h_lat = _linear_scan(jnp.einsum('blgh,gph->blgp', ul, bbar), abar, rev) \
            + carry[None] * h0[:, None]
        y_lat = y_lat + jnp.real(jnp.einsum('blgp,ghp->blgh', h_lat, cmat)).reshape(B, L, E)
        if with_ctx:
            y_ctx = y_ctx + jnp.real(jnp.einsum('bcgp,ghp->bcgh', h_ctx, cmat)).reshape(B, C, E)
    return y_lat.astype(u_lat.dtype), (y_ctx.astype(u_ctx.dtype) if with_ctx else None)


def _s5_post(y, z, w_glu, b_glu, w_out):
    y = jax.nn.gelu(y, approximate=False)
    y = y * jax.nn.sigmoid(y @ w_glu + b_glu)
    return (y * jax.nn.silu(z)) @ w_out


def _ssm_layer(h, hc, w_in, a_re, a_im, log_dt, b_re, b_im, c_re, c_im, d_skip,
               w_glu, b_glu, w_out, with_ctx):
    proj = h @ w_in
    u, z = proj[..., :SSM_WIDTH], proj[..., SSM_WIDTH:]
    if with_ctx:
        proj_c = hc @ w_in
        u_c, z_c = proj_c[..., :SSM_WIDTH], proj_c[..., SSM_WIDTH:]
    else:
        u_c = hc @ w_in[:, :SSM_WIDTH]
    y, y_c = _s5_core(u, u_c, a_re, a_im, log_dt, b_re, b_im, c_re, c_im, d_skip, with_ctx)
    out = _s5_post(y, z, w_glu, b_glu, w_out)
    out_c = _s5_post(y_c, z_c, w_glu, b_glu, w_out) if with_ctx else None
    return out, out_c


def _sdpa(qb, k, v):
    s = jnp.einsum('bqgrd,bkgd->bgrqk', qb, k, preferred_element_type=jnp.float32)
    p = jax.nn.softmax(s * (1.0 / math.sqrt(HEAD_DIM)), axis=-1).astype(v.dtype)
    return jnp.einsum('bgrqk,bkgd->bqgrd', p, v)


def _attn_layer(h, hc, w_in, q_norm, k_norm, w_out, rope, with_ctx):
    B, L, _ = h.shape
    C = hc.shape[1]
    proj = h @ w_in
    q = proj[..., :ATTN_WIDTH].reshape(B, L, N_Q_HEADS, HEAD_DIM)
    k = proj[..., ATTN_WIDTH:ATTN_WIDTH + KV_WIDTH].reshape(B, L, N_KV_HEADS, HEAD_DIM)
    v = proj[..., ATTN_WIDTH + KV_WIDTH:ATTN_WIDTH + 2 * KV_WIDTH].reshape(B, L, N_KV_HEADS, HEAD_DIM)
    z = proj[..., ATTN_WIDTH + 2 * KV_WIDTH:]
    q = _rope_2d(_rmsnorm(q, q_norm), rope)
    k = _rope_2d(_rmsnorm(k, k_norm), rope)
    proj_c = hc @ w_in if with_ctx else hc @ w_in[:, ATTN_WIDTH:ATTN_WIDTH + 2 * KV_WIDTH]
    off = ATTN_WIDTH if with_ctx else 0
    k_c = _rmsnorm(proj_c[..., off:off + KV_WIDTH].reshape(B, C, N_KV_HEADS, HEAD_DIM), k_norm)
    v_c = proj_c[..., off + KV_WIDTH:off + 2 * KV_WIDTH].reshape(B, C, N_KV_HEADS, HEAD_DIM)
    k_all = jnp.concatenate([k, k_c], axis=1)
    v_all = jnp.concatenate([v, v_c], axis=1)
    nb = L // Q_BLOCK
    qb = q.reshape(B, nb, Q_BLOCK, N_KV_HEADS, KV_REP, HEAD_DIM).transpose(1, 0, 2, 3, 4, 5)
    o = lax.map(lambda blk: _sdpa(blk, k_all, v_all), qb)
    o = o.transpose(1, 0, 2, 3, 4, 5).reshape(B, L, ATTN_WIDTH)
    out = (o * jax.nn.silu(z)) @ w_out
    out_c = None
    if with_ctx:
        q_c = _rmsnorm(proj_c[..., :ATTN_WIDTH].reshape(B, C, N_Q_HEADS, HEAD_DIM), q_norm)
        o_c = _sdpa(q_c.reshape(B, C, N_KV_HEADS, KV_REP, HEAD_DIM), k_c, v_c).reshape(B, C, ATTN_WIDTH)
        out_c = (o_c * jax.nn.silu(proj_c[..., ATTN_WIDTH + 2 * KV_WIDTH:])) @ w_out
    return out, out_c


def _fwd_setup_inputs(seed: int = 0) -> dict:
    key = jax.random.key(seed)
    ks = jax.random.split(key, 24)
    f32 = jnp.float32
    nrm = lambda k, shape, s: jax.random.normal(k, shape, f32) * s
    NA, NB, G, P, H = N_SSM_LAYERS, N_ATTN_LAYERS, SSM_GROUPS, SSM_STATE, SSM_GROUP
    a_im0 = jnp.pi * jnp.arange(P, dtype=f32)
    return {
        "x": nrm(ks[0], (BATCH, SEQ, D_MODEL), 1.0),
        "c": nrm(ks[1], (BATCH, D_MODEL), 1.0),
        "ctx": nrm(ks[2], (BATCH, CTX_LEN, D_MODEL), 1.0),
        "c_ctx": nrm(ks[3], (D_MODEL,), 1.0),
        "w_mod": nrm(ks[4], (DEPTH, D_MODEL, 3 * D_MODEL), D_MODEL ** -0.5),
        "b_mod": nrm(ks[5], (DEPTH, 3 * D_MODEL), 0.02),
        "norm_g": 1.0 + nrm(ks[6], (DEPTH, D_MODEL), 0.05),
        "ssm_w_in": nrm(ks[7], (NA, D_MODEL, 2 * SSM_WIDTH), D_MODEL ** -0.5),
        "ssm_a_re": -0.5 + nrm(ks[8], (NA, 2, G, P), 0.01),
        "ssm_a_im": a_im0 + nrm(ks[9], (NA, 2, G, P), 0.01),
        "ssm_log_dt": jax.random.uniform(ks[10], (NA, 2, G), f32,
                                         minval=math.log(DT_MIN), maxval=math.log(DT_MAX)),
        "ssm_b_re": nrm(ks[11], (NA, 2, G, P, H), (2.0 * H) ** -0.5),
        "ssm_b_im": nrm(ks[12], (NA, 2, G, P, H), (2.0 * H) ** -0.5),
        "ssm_c_re": nrm(ks[13], (NA, 2, G, H, P), (2.0 * P) ** -0.5),
        "ssm_c_im": nrm(ks[14], (NA, 2, G, H, P), (2.0 * P) ** -0.5),
        "ssm_d": nrm(ks[15], (NA, SSM_WIDTH), 0.5),
        "ssm_w_glu": nrm(ks[16], (NA, SSM_WIDTH, SSM_WIDTH), SSM_WIDTH ** -0.5),
        "ssm_b_glu": nrm(ks[17], (NA, SSM_WIDTH), 0.02),
        "ssm_w_out": nrm(ks[18], (NA, SSM_WIDTH, D_MODEL), SSM_WIDTH ** -0.5),
        "attn_w_in": nrm(ks[19], (NB, D_MODEL, ATTN_IN), D_MODEL ** -0.5),
        "attn_q_norm": 1.0 + nrm(ks[20], (NB, HEAD_DIM), 0.05),
        "attn_k_norm": 1.0 + nrm(ks[21], (NB, HEAD_DIM), 0.05),
        "attn_w_out": nrm(ks[22], (NB, ATTN_WIDTH, D_MODEL), ATTN_WIDTH ** -0.5),
        "final_norm_g": 1.0 + nrm(ks[23], (D_MODEL,), 0.05),
    }


def _fwd_reference(x, c, ctx, c_ctx, w_mod, b_mod, norm_g, ssm_w_in, ssm_a_re, ssm_a_im,
              ssm_log_dt, ssm_b_re, ssm_b_im, ssm_c_re, ssm_c_im, ssm_d, ssm_w_glu,
              ssm_b_glu, ssm_w_out, attn_w_in, attn_q_norm, attn_k_norm, attn_w_out,
              final_norm_g):
    L = x.shape[1]
    rope = _rope_tables(L)
    s_c = jax.nn.silu(c)
    s_cc = jax.nn.silu(c_ctx)
    for i in range(DEPTH):
        kind, j = i % N_MIXERS, i // N_MIXERS
        with_ctx = i < DEPTH - 1
        mod = s_c @ w_mod[i] + b_mod[i]
        shift, scale, gate = jnp.split(mod, 3, axis=-1)
        mod_c = s_cc @ w_mod[i] + b_mod[i]
        shift_c, scale_c, gate_c = jnp.split(mod_c, 3, axis=-1)
        h = _rmsnorm(x, norm_g[i]) * (1.0 + scale[:, None]) + shift[:, None]
        hc = _rmsnorm(ctx, norm_g[i]) * (1.0 + scale_c) + shift_c
        if kind == 0:
            out, out_c = _ssm_layer(h, hc, ssm_w_in[j], ssm_a_re[j], ssm_a_im[j], ssm_log_dt[j],
                                    ssm_b_re[j], ssm_b_im[j], ssm_c_re[j], ssm_c_im[j], ssm_d[j],
                                    ssm_w_glu[j], ssm_b_glu[j], ssm_w_out[j], with_ctx)
        else:
            out, out_c = _attn_layer(h, hc, attn_w_in[j], attn_q_norm[j], attn_k_norm[j],
                                     attn_w_out[j], rope, with_ctx)
        x = x + gate[:, None] * out
        if with_ctx:
            ctx = ctx + gate_c * out_c
    return _rmsnorm(x, final_norm_g)


import jax as _jax
import jax.numpy as _jnp

TWIN_FORMAT = 'train_step'
FWD_PARAMS = ['x', 'c', 'ctx', 'c_ctx', 'w_mod', 'b_mod', 'norm_g', 'ssm_w_in', 'ssm_a_re', 'ssm_a_im', 'ssm_log_dt', 'ssm_b_re', 'ssm_b_im', 'ssm_c_re', 'ssm_c_im', 'ssm_d', 'ssm_w_glu', 'ssm_b_glu', 'ssm_w_out', 'attn_w_in', 'attn_q_norm', 'attn_k_norm', 'attn_w_out', 'final_norm_g']
TWIN_WEIGHTS = ['c_ctx', 'w_mod', 'b_mod', 'norm_g', 'ssm_w_in', 'ssm_a_re', 'ssm_a_im', 'ssm_log_dt', 'ssm_b_re', 'ssm_b_im', 'ssm_c_re', 'ssm_c_im', 'ssm_d', 'ssm_w_glu', 'ssm_b_glu', 'ssm_w_out', 'attn_w_in', 'attn_q_norm', 'attn_k_norm', 'attn_w_out', 'final_norm_g']
TWIN_DIFF_INPUT = 'x'
TWIN_INPUTS = ['x', 'c', 'ctx', 'c_ctx', 'w_mod', 'b_mod', 'norm_g', 'ssm_w_in', 'ssm_a_re', 'ssm_a_im', 'ssm_log_dt', 'ssm_b_re', 'ssm_b_im', 'ssm_c_re', 'ssm_c_im', 'ssm_d', 'ssm_w_glu', 'ssm_b_glu', 'ssm_w_out', 'attn_w_in', 'attn_q_norm', 'attn_k_norm', 'attn_w_out', 'final_norm_g', 'loss_target', 'm_c_ctx', 'm_w_mod', 'm_b_mod', 'm_norm_g', 'm_ssm_w_in', 'm_ssm_a_re', 'm_ssm_a_im', 'm_ssm_log_dt', 'm_ssm_b_re', 'm_ssm_b_im', 'm_ssm_c_re', 'm_ssm_c_im', 'm_ssm_d', 'm_ssm_w_glu', 'm_ssm_b_glu', 'm_ssm_w_out', 'm_attn_w_in', 'm_attn_q_norm', 'm_attn_k_norm', 'm_attn_w_out', 'm_final_norm_g', 'v_c_ctx', 'v_w_mod', 'v_b_mod', 'v_norm_g', 'v_ssm_w_in', 'v_ssm_a_re', 'v_ssm_a_im', 'v_ssm_log_dt', 'v_ssm_b_re', 'v_ssm_b_im', 'v_ssm_c_re', 'v_ssm_c_im', 'v_ssm_d', 'v_ssm_w_glu', 'v_ssm_b_glu', 'v_ssm_w_out', 'v_attn_w_in', 'v_attn_q_norm', 'v_attn_k_norm', 'v_attn_w_out', 'v_final_norm_g']
TWIN_OUTPUTS = ['loss', 'grad_x', 'grad_c_ctx', 'grad_w_mod', 'grad_b_mod', 'grad_norm_g', 'grad_ssm_w_in', 'grad_ssm_a_re', 'grad_ssm_a_im', 'grad_ssm_log_dt', 'grad_ssm_b_re', 'grad_ssm_b_im', 'grad_ssm_c_re', 'grad_ssm_c_im', 'grad_ssm_d', 'grad_ssm_w_glu', 'grad_ssm_b_glu', 'grad_ssm_w_out', 'grad_attn_w_in', 'grad_attn_q_norm', 'grad_attn_k_norm', 'grad_attn_w_out', 'grad_final_norm_g', 'delta_c_ctx', 'delta_w_mod', 'delta_b_mod', 'delta_norm_g', 'delta_ssm_w_in', 'delta_ssm_a_re', 'delta_ssm_a_im', 'delta_ssm_log_dt', 'delta_ssm_b_re', 'delta_ssm_b_im', 'delta_ssm_c_re', 'delta_ssm_c_im', 'delta_ssm_d', 'delta_ssm_w_glu', 'delta_ssm_b_glu', 'delta_ssm_w_out', 'delta_attn_w_in', 'delta_attn_q_norm', 'delta_attn_k_norm', 'delta_attn_w_out', 'delta_final_norm_g', 'new_m_c_ctx', 'new_m_w_mod', 'new_m_b_mod', 'new_m_norm_g', 'new_m_ssm_w_in', 'new_m_ssm_a_re', 'new_m_ssm_a_im', 'new_m_ssm_log_dt', 'new_m_ssm_b_re', 'new_m_ssm_b_im', 'new_m_ssm_c_re', 'new_m_ssm_c_im', 'new_m_ssm_d', 'new_m_ssm_w_glu', 'new_m_ssm_b_glu', 'new_m_ssm_w_out', 'new_m_attn_w_in', 'new_m_attn_q_norm', 'new_m_attn_k_norm', 'new_m_attn_w_out', 'new_m_final_norm_g', 'new_v_c_ctx', 'new_v_w_mod', 'new_v_b_mod', 'new_v_norm_g', 'new_v_ssm_w_in', 'new_v_ssm_a_re', 'new_v_ssm_a_im', 'new_v_ssm_log_dt', 'new_v_ssm_b_re', 'new_v_ssm_b_im', 'new_v_ssm_c_re', 'new_v_ssm_c_im', 'new_v_ssm_d', 'new_v_ssm_w_glu', 'new_v_ssm_b_glu', 'new_v_ssm_w_out', 'new_v_attn_w_in', 'new_v_attn_q_norm', 'new_v_attn_k_norm', 'new_v_attn_w_out', 'new_v_final_norm_g']
TWIN_LEAF_KINDS = {'loss': 'loss', 'grad_x': 'grad_x', 'grad_c_ctx': 'grad_w', 'grad_w_mod': 'grad_w', 'grad_b_mod': 'grad_w', 'grad_norm_g': 'grad_w', 'grad_ssm_w_in': 'grad_w', 'grad_ssm_a_re': 'grad_w', 'grad_ssm_a_im': 'grad_w', 'grad_ssm_log_dt': 'grad_w', 'grad_ssm_b_re': 'grad_w', 'grad_ssm_b_im': 'grad_w', 'grad_ssm_c_re': 'grad_w', 'grad_ssm_c_im': 'grad_w', 'grad_ssm_d': 'grad_w', 'grad_ssm_w_glu': 'grad_w', 'grad_ssm_b_glu': 'grad_w', 'grad_ssm_w_out': 'grad_w', 'grad_attn_w_in': 'grad_w', 'grad_attn_q_norm': 'grad_w', 'grad_attn_k_norm': 'grad_w', 'grad_attn_w_out': 'grad_w', 'grad_final_norm_g': 'grad_w', 'delta_c_ctx': 'delta_w', 'delta_w_mod': 'delta_w', 'delta_b_mod': 'delta_w', 'delta_norm_g': 'delta_w', 'delta_ssm_w_in': 'delta_w', 'delta_ssm_a_re': 'delta_w', 'delta_ssm_a_im': 'delta_w', 'delta_ssm_log_dt': 'delta_w', 'delta_ssm_b_re': 'delta_w', 'delta_ssm_b_im': 'delta_w', 'delta_ssm_c_re': 'delta_w', 'delta_ssm_c_im': 'delta_w', 'delta_ssm_d': 'delta_w', 'delta_ssm_w_glu': 'delta_w', 'delta_ssm_b_glu': 'delta_w', 'delta_ssm_w_out': 'delta_w', 'delta_attn_w_in': 'delta_w', 'delta_attn_q_norm': 'delta_w', 'delta_attn_k_norm': 'delta_w', 'delta_attn_w_out': 'delta_w', 'delta_final_norm_g': 'delta_w', 'new_m_c_ctx': 'new_m', 'new_m_w_mod': 'new_m', 'new_m_b_mod': 'new_m', 'new_m_norm_g': 'new_m', 'new_m_ssm_w_in': 'new_m', 'new_m_ssm_a_re': 'new_m', 'new_m_ssm_a_im': 'new_m', 'new_m_ssm_log_dt': 'new_m', 'new_m_ssm_b_re': 'new_m', 'new_m_ssm_b_im': 'new_m', 'new_m_ssm_c_re': 'new_m', 'new_m_ssm_c_im': 'new_m', 'new_m_ssm_d': 'new_m', 'new_m_ssm_w_glu': 'new_m', 'new_m_ssm_b_glu': 'new_m', 'new_m_ssm_w_out': 'new_m', 'new_m_attn_w_in': 'new_m', 'new_m_attn_q_norm': 'new_m', 'new_m_attn_k_norm': 'new_m', 'new_m_attn_w_out': 'new_m', 'new_m_final_norm_g': 'new_m', 'new_v_c_ctx': 'new_v', 'new_v_w_mod': 'new_v', 'new_v_b_mod': 'new_v', 'new_v_norm_g': 'new_v', 'new_v_ssm_w_in': 'new_v', 'new_v_ssm_a_re': 'new_v', 'new_v_ssm_a_im': 'new_v', 'new_v_ssm_log_dt': 'new_v', 'new_v_ssm_b_re': 'new_v', 'new_v_ssm_b_im': 'new_v', 'new_v_ssm_c_re': 'new_v', 'new_v_ssm_c_im': 'new_v', 'new_v_ssm_d': 'new_v', 'new_v_ssm_w_glu': 'new_v', 'new_v_ssm_b_glu': 'new_v', 'new_v_ssm_w_out': 'new_v', 'new_v_attn_w_in': 'new_v', 'new_v_attn_q_norm': 'new_v', 'new_v_attn_k_norm': 'new_v', 'new_v_attn_w_out': 'new_v', 'new_v_final_norm_g': 'new_v'}


def _forward(args):
    return _fwd_reference(*[args[k] for k in FWD_PARAMS])


def _output_shape():
    out = _jax.eval_shape(lambda: _forward(_fwd_setup_inputs(0)))
    return out.shape, out.dtype

N_MICROBATCH = 1
ADAM_LR = 0.001
ADAM_B1 = 0.9
ADAM_B2 = 0.999
ADAM_EPS = 1e-08
ADAM_WD = 0.01
ADAM_STEP = 10
PER_EXAMPLE_BATCH_AXIS = {'x': 0, 'c': 0, 'ctx': 0, 'loss_target': 0}
SHARED_INPUTS = []
_WEIGHT_DTYPES = {'c_ctx': _jnp.float32, 'w_mod': _jnp.float32, 'b_mod': _jnp.float32, 'norm_g': _jnp.float32, 'ssm_w_in': _jnp.float32, 'ssm_a_re': _jnp.float32, 'ssm_a_im': _jnp.float32, 'ssm_log_dt': _jnp.float32, 'ssm_b_re': _jnp.float32, 'ssm_b_im': _jnp.float32, 'ssm_c_re': _jnp.float32, 'ssm_c_im': _jnp.float32, 'ssm_d': _jnp.float32, 'ssm_w_glu': _jnp.float32, 'ssm_b_glu': _jnp.float32, 'ssm_w_out': _jnp.float32, 'attn_w_in': _jnp.float32, 'attn_q_norm': _jnp.float32, 'attn_k_norm': _jnp.float32, 'attn_w_out': _jnp.float32, 'final_norm_g': _jnp.float32}
MOMENT_SCALE = {'c_ctx': 3.927267e-02, 'w_mod': 7.845804e-02, 'b_mod': 1.250377e-01, 'norm_g': 5.104293e-02, 'ssm_w_in': 3.204502e-02, 'ssm_a_re': 4.187254e-03, 'ssm_a_im': 4.312499e-03, 'ssm_log_dt': 1.393976e+00, 'ssm_b_re': 2.702275e-03, 'ssm_b_im': 3.199012e-03, 'ssm_c_re': 5.544348e-03, 'ssm_c_im': 5.990552e-03, 'ssm_d': 6.216913e-02, 'ssm_w_glu': 6.503873e-03, 'ssm_b_glu': 1.479606e-02, 'ssm_w_out': 3.180017e-02, 'attn_w_in': 7.919679e-02, 'attn_q_norm': 6.384697e-02, 'attn_k_norm': 6.514118e-02, 'attn_w_out': 9.696996e-02, 'final_norm_g': 6.404913e+01}


def _to_microbatches(a, axis):
    t = _jnp.moveaxis(a, axis, 0)
    t = t.reshape((N_MICROBATCH, t.shape[0] // N_MICROBATCH) + t.shape[1:])
    return _jnp.moveaxis(t, 1, axis + 1)


def setup_inputs(seed: int = 0) -> dict:
    inp = _fwd_setup_inputs(seed)
    key = _jax.random.fold_in(_jax.random.key(seed), 7919)
    shape, _ = _output_shape()
    out = dict(inp)
    out["loss_target"] = _jax.random.normal(_jax.random.fold_in(key, 0), shape, _jnp.float32)
    for i, name in enumerate(TWIN_WEIGHTS):
        w = inp[name].astype(_jnp.float32)
        if MOMENT_SCALE is None:
            s = _jnp.sqrt(_jnp.mean(_jnp.square(w)) + 1e-30)
        else:
            s = MOMENT_SCALE[name]
        km, kv = _jax.random.split(_jax.random.fold_in(key, i + 1))
        out[name] = w
        out["m_" + name] = s * _jax.random.normal(km, w.shape, _jnp.float32)
        out["v_" + name] = (s * s) * _jax.random.uniform(kv, w.shape, _jnp.float32, 0.5, 1.5)
    if N_MICROBATCH > 1:
        for name, axis in PER_EXAMPLE_BATCH_AXIS.items():
            out[name] = _to_microbatches(out[name], axis)
    return {'x': out['x'], 'c': out['c'], 'ctx': out['ctx'], 'c_ctx': out['c_ctx'], 'w_mod': out['w_mod'], 'b_mod': out['b_mod'], 'norm_g': out['norm_g'], 'ssm_w_in': out['ssm_w_in'], 'ssm_a_re': out['ssm_a_re'], 'ssm_a_im': out['ssm_a_im'], 'ssm_log_dt': out['ssm_log_dt'], 'ssm_b_re': out['ssm_b_re'], 'ssm_b_im': out['ssm_b_im'], 'ssm_c_re': out['ssm_c_re'], 'ssm_c_im': out['ssm_c_im'], 'ssm_d': out['ssm_d'], 'ssm_w_glu': out['ssm_w_glu'], 'ssm_b_glu': out['ssm_b_glu'], 'ssm_w_out': out['ssm_w_out'], 'attn_w_in': out['attn_w_in'], 'attn_q_norm': out['attn_q_norm'], 'attn_k_norm': out['attn_k_norm'], 'attn_w_out': out['attn_w_out'], 'final_norm_g': out['final_norm_g'], 'loss_target': out['loss_target'], 'm_c_ctx': out['m_c_ctx'], 'm_w_mod': out['m_w_mod'], 'm_b_mod': out['m_b_mod'], 'm_norm_g': out['m_norm_g'], 'm_ssm_w_in': out['m_ssm_w_in'], 'm_ssm_a_re': out['m_ssm_a_re'], 'm_ssm_a_im': out['m_ssm_a_im'], 'm_ssm_log_dt': out['m_ssm_log_dt'], 'm_ssm_b_re': out['m_ssm_b_re'], 'm_ssm_b_im': out['m_ssm_b_im'], 'm_ssm_c_re': out['m_ssm_c_re'], 'm_ssm_c_im': out['m_ssm_c_im'], 'm_ssm_d': out['m_ssm_d'], 'm_ssm_w_glu': out['m_ssm_w_glu'], 'm_ssm_b_glu': out['m_ssm_b_glu'], 'm_ssm_w_out': out['m_ssm_w_out'], 'm_attn_w_in': out['m_attn_w_in'], 'm_attn_q_norm': out['m_attn_q_norm'], 'm_attn_k_norm': out['m_attn_k_norm'], 'm_attn_w_out': out['m_attn_w_out'], 'm_final_norm_g': out['m_final_norm_g'], 'v_c_ctx': out['v_c_ctx'], 'v_w_mod': out['v_w_mod'], 'v_b_mod': out['v_b_mod'], 'v_norm_g': out['v_norm_g'], 'v_ssm_w_in': out['v_ssm_w_in'], 'v_ssm_a_re': out['v_ssm_a_re'], 'v_ssm_a_im': out['v_ssm_a_im'], 'v_ssm_log_dt': out['v_ssm_log_dt'], 'v_ssm_b_re': out['v_ssm_b_re'], 'v_ssm_b_im': out['v_ssm_b_im'], 'v_ssm_c_re': out['v_ssm_c_re'], 'v_ssm_c_im': out['v_ssm_c_im'], 'v_ssm_d': out['v_ssm_d'], 'v_ssm_w_glu': out['v_ssm_w_glu'], 'v_ssm_b_glu': out['v_ssm_b_glu'], 'v_ssm_w_out': out['v_ssm_w_out'], 'v_attn_w_in': out['v_attn_w_in'], 'v_attn_q_norm': out['v_attn_q_norm'], 'v_attn_k_norm': out['v_attn_k_norm'], 'v_attn_w_out': out['v_attn_w_out'], 'v_final_norm_g': out['v_final_norm_g']}


def _loss(weights, diff, rest, loss_target):
    with _jax.named_scope("forward"):
        args = {**rest, TWIN_DIFF_INPUT: diff, **{k: w.astype(_WEIGHT_DTYPES[k]) for k, w in weights.items()}}
        y = _forward(args)
    with _jax.named_scope("loss_head"):
        err = _jnp.square(y.astype(_jnp.float32) - loss_target)
        return 0.5 * _jnp.sum(_jnp.mean(err, axis=-1)) if err.ndim else 0.5 * err


def _adamw(w, g, m, v):
    m = ADAM_B1 * m + (1.0 - ADAM_B1) * g
    v = ADAM_B2 * v + (1.0 - ADAM_B2) * _jnp.square(g)
    m_hat = m / (1.0 - ADAM_B1 ** ADAM_STEP)
    v_hat = v / (1.0 - ADAM_B2 ** ADAM_STEP)
    delta = -ADAM_LR * (m_hat / (_jnp.sqrt(v_hat) + ADAM_EPS) + ADAM_WD * w)
    return delta, m, v


def reference(x, c, ctx, c_ctx, w_mod, b_mod, norm_g, ssm_w_in, ssm_a_re, ssm_a_im, ssm_log_dt, ssm_b_re, ssm_b_im, ssm_c_re, ssm_c_im, ssm_d, ssm_w_glu, ssm_b_glu, ssm_w_out, attn_w_in, attn_q_norm, attn_k_norm, attn_w_out, final_norm_g, loss_target, m_c_ctx, m_w_mod, m_b_mod, m_norm_g, m_ssm_w_in, m_ssm_a_re, m_ssm_a_im, m_ssm_log_dt, m_ssm_b_re, m_ssm_b_im, m_ssm_c_re, m_ssm_c_im, m_ssm_d, m_ssm_w_glu, m_ssm_b_glu, m_ssm_w_out, m_attn_w_in, m_attn_q_norm, m_attn_k_norm, m_attn_w_out, m_final_norm_g, v_c_ctx, v_w_mod, v_b_mod, v_norm_g, v_ssm_w_in, v_ssm_a_re, v_ssm_a_im, v_ssm_log_dt, v_ssm_b_re, v_ssm_b_im, v_ssm_c_re, v_ssm_c_im, v_ssm_d, v_ssm_w_glu, v_ssm_b_glu, v_ssm_w_out, v_attn_w_in, v_attn_q_norm, v_attn_k_norm, v_attn_w_out, v_final_norm_g):
    given = dict(x=x, c=c, ctx=ctx, c_ctx=c_ctx, w_mod=w_mod, b_mod=b_mod, norm_g=norm_g, ssm_w_in=ssm_w_in, ssm_a_re=ssm_a_re, ssm_a_im=ssm_a_im, ssm_log_dt=ssm_log_dt, ssm_b_re=ssm_b_re, ssm_b_im=ssm_b_im, ssm_c_re=ssm_c_re, ssm_c_im=ssm_c_im, ssm_d=ssm_d, ssm_w_glu=ssm_w_glu, ssm_b_glu=ssm_b_glu, ssm_w_out=ssm_w_out, attn_w_in=attn_w_in, attn_q_norm=attn_q_norm, attn_k_norm=attn_k_norm, attn_w_out=attn_w_out, final_norm_g=final_norm_g, loss_target=loss_target, m_c_ctx=m_c_ctx, m_w_mod=m_w_mod, m_b_mod=m_b_mod, m_norm_g=m_norm_g, m_ssm_w_in=m_ssm_w_in, m_ssm_a_re=m_ssm_a_re, m_ssm_a_im=m_ssm_a_im, m_ssm_log_dt=m_ssm_log_dt, m_ssm_b_re=m_ssm_b_re, m_ssm_b_im=m_ssm_b_im, m_ssm_c_re=m_ssm_c_re, m_ssm_c_im=m_ssm_c_im, m_ssm_d=m_ssm_d, m_ssm_w_glu=m_ssm_w_glu, m_ssm_b_glu=m_ssm_b_glu, m_ssm_w_out=m_ssm_w_out, m_attn_w_in=m_attn_w_in, m_attn_q_norm=m_attn_q_norm, m_attn_k_norm=m_attn_k_norm, m_attn_w_out=m_attn_w_out, m_final_norm_g=m_final_norm_g, v_c_ctx=v_c_ctx, v_w_mod=v_w_mod, v_b_mod=v_b_mod, v_norm_g=v_norm_g, v_ssm_w_in=v_ssm_w_in, v_ssm_a_re=v_ssm_a_re, v_ssm_a_im=v_ssm_a_im, v_ssm_log_dt=v_ssm_log_dt, v_ssm_b_re=v_ssm_b_re, v_ssm_b_im=v_ssm_b_im, v_ssm_c_re=v_ssm_c_re, v_ssm_c_im=v_ssm_c_im, v_ssm_d=v_ssm_d, v_ssm_w_glu=v_ssm_w_glu, v_ssm_b_glu=v_ssm_b_glu, v_ssm_w_out=v_ssm_w_out, v_attn_w_in=v_attn_w_in, v_attn_q_norm=v_attn_q_norm, v_attn_k_norm=v_attn_k_norm, v_attn_w_out=v_attn_w_out, v_final_norm_g=v_final_norm_g)
    weights = {n: given[n] for n in TWIN_WEIGHTS}
    shared = {n: given[n] for n in SHARED_INPUTS}
    per_example = {n: given[n] for n in ['x', 'c', 'ctx']}
    grad_fn = _jax.value_and_grad(_loss, argnums=(0, 1))

    def one_microbatch(ex, loss_target):
        ex = dict(ex)
        diff = ex.pop(TWIN_DIFF_INPUT)
        return grad_fn(weights, diff, {**shared, **ex}, loss_target)

    if N_MICROBATCH == 1:
        loss, (grad_w, grad_x) = one_microbatch(per_example, given["loss_target"])
    else:
        def body(carry, xs):
            loss_sum, grad_sum = carry
            l_k, (gw_k, gx_k) = one_microbatch(xs[0], xs[1])
            with _jax.named_scope("update"):
                return (loss_sum + l_k, _jax.tree.map(_jnp.add, grad_sum, gw_k)), gx_k

        init = (_jnp.zeros((), _jnp.float32), _jax.tree.map(_jnp.zeros_like, weights))
        (loss, grad_w), grad_x = _jax.lax.scan(body, init, (per_example, given["loss_target"]))
    with _jax.named_scope("update"):
        delta_w, new_m, new_v = {}, {}, {}
        for n in TWIN_WEIGHTS:
            delta_w[n], new_m[n], new_v[n] = _adamw(weights[n], grad_w[n], given["m_" + n], given["v_" + n])
    return (loss, grad_x, *[grad_w[n] for n in TWIN_WEIGHTS], *[delta_w[n] for n in TWIN_WEIGHTS],
            *[new_m[n] for n in TWIN_WEIGHTS], *[new_v[n] for n in TWIN_WEIGHTS])
```

```python
import functools
import math

import jax
import jax.numpy as jnp
from jax import lax
from jax.experimental import pallas as pl
from jax.experimental.pallas import tpu as pltpu

F32 = jnp.float32
BF16 = jnp.bfloat16

D_MODEL = 1024
NORM_EPS = 1e-6
SSM_GROUPS = 64
SSM_GROUP = 16
SSM_STATE = 64
SLAB_W = 128
N_SLAB = D_MODEL // SLAB_W
SLAB_GROUPS = SLAB_W // SSM_GROUP
HALF_W = SLAB_GROUPS * SSM_STATE
STATE_W = 2 * HALF_W
N_SEG = 8
HEAD_DIM = 64
N_Q_HEADS = 16
N_KV_HEADS = 4
KV_REP = N_Q_HEADS // N_KV_HEADS
ATTN_W = N_Q_HEADS * HEAD_DIM
KV_W = N_KV_HEADS * HEAD_DIM
GRID_W = 64
ROPE_THETA = 10000.0
N_DEV = 8
N_CHIP = 4
VMEM_LIMIT_BYTES = 56 * 1024 * 1024

ADAM_LR = 0.001
ADAM_B1 = 0.9
ADAM_B2 = 0.999
ADAM_EPS = 1e-08
ADAM_WD = 0.01
ADAM_STEP = 10


def _params(*sem):
    return pltpu.CompilerParams(dimension_semantics=sem, vmem_limit_bytes=VMEM_LIMIT_BYTES)


def _largest_tile(n, cap, unit):
    if n <= cap:
        return n
    t = (cap // unit) * unit
    while t >= unit:
        if n % t == 0:
            return t
        t -= unit
    raise ValueError(f"no tile for {n} (cap {cap}, unit {unit})")


def _rowwise(name, fn, n_rows, tr, row_ins, vec_ins, row_outs, red_outs, n_lat=None, want_flag=False):
    nt = n_rows // tr
    assert nt * tr == n_rows
    nlt = nt if n_lat is None else n_lat // tr

    def sel(i):
        return jnp.where(i >= nlt, 1, 0)

    arrays, in_specs = [], []
    for spec in row_ins:
        arr, cb, w = spec[:3]
        kind = spec[3] if len(spec) > 3 else None
        m = spec[4] if len(spec) > 4 else None
        if kind == "mod":
            imap = functools.partial(lambda i, cb, m: (i % m, cb), cb=cb, m=m)
        elif kind == "clamp":
            imap = functools.partial(lambda i, cb, m: (jnp.minimum(i, m - 1), cb), cb=cb, m=m)
        else:
            imap = functools.partial(lambda i, cb: (i, cb), cb=cb)
        arrays.append(arr)
        in_specs.append(pl.BlockSpec((tr, w), imap))
    for v in vec_ins:
        s, a, w = v.shape
        imap = (lambda i: (sel(i), 0, 0)) if s == 2 else (lambda i: (0, 0, 0))
        arrays.append(v)
        in_specs.append(pl.BlockSpec((1, a, w), imap))
    out_shapes, out_specs = [], []
    for w, dt in row_outs:
        out_shapes.append(jax.ShapeDtypeStruct((n_rows, w), dt))
        out_specs.append(pl.BlockSpec((tr, w), lambda i: (i, 0)))
    for s, w in red_outs:
        out_shapes.append(jax.ShapeDtypeStruct((s, 1, w), F32))
        imap = (lambda i: (sel(i), 0, 0)) if s == 2 else (lambda i: (0, 0, 0))
        out_specs.append(pl.BlockSpec((1, 1, w), imap))
    n_ri, n_vi, n_ro, n_rd = len(row_ins), len(vec_ins), len(row_outs), len(red_outs)

    def body(*refs):
        i = pl.program_id(0)
        rows = [r[...] for r in refs[:n_ri]]
        vecs = [r[0] for r in refs[n_ri:n_ri + n_vi]]
        outs = refs[n_ri + n_vi:]
        lead = [jnp.where(i < nlt, 1.0, 0.0).astype(F32)] if want_flag else []
        res = fn(*lead, *rows, *vecs)
        if not isinstance(res, (tuple, list)):
            res = (res,)
        assert len(res) == n_ro + n_rd
        for k in range(n_ro):
            outs[k][...] = res[k].astype(outs[k].dtype)
        for k in range(n_rd):
            part = jnp.sum(res[n_ro + k].astype(F32), axis=0, keepdims=True)
            first = i == 0
            if red_outs[k][0] == 2:
                first = jnp.logical_or(first, i == nlt)
            o = outs[n_ro + k]

            @pl.when(first)
            def _():
                o[0] = part

            @pl.when(jnp.logical_not(first))
            def _():
                o[0] = o[0] + part

    res = pl.pallas_call(
        body, grid=(nt,), in_specs=in_specs, out_specs=out_specs, out_shape=out_shapes,
        compiler_params=_params("arbitrary"), name=name)(*arrays)
    return res


def _vec(v):
    v = v.astype(F32)
    if v.ndim == 1:
        v = v[None]
    return v[:, None, :]


def _mm(name, a, b, mode, out_dtype=F32):
    if mode in ("nn", "nt"):
        m, k = a.shape
        n = b.shape[1] if mode == "nn" else b.shape[0]
        tm = _largest_tile(m, 512, 8)
        tn = _largest_tile(n, 1024, 128)
        contract = (((1,), (0,)), ((), ())) if mode == "nn" else (((1,), (1,)), ((), ()))

        def body(a_ref, b_ref, o_ref):
            o_ref[...] = lax.dot_general(a_ref[...].astype(BF16), b_ref[...].astype(BF16), contract,
                                         preferred_element_type=F32).astype(o_ref.dtype)

        b_spec = pl.BlockSpec((k, tn), lambda i, j: (0, j)) if mode == "nn" else pl.BlockSpec((tn, k), lambda i, j: (j, 0))
        return pl.pallas_call(
            body, grid=(m // tm, n // tn),
            in_specs=[pl.BlockSpec((tm, k), lambda i, j: (i, 0)), b_spec],
            out_specs=pl.BlockSpec((tm, tn), lambda i, j: (i, j)),
            out_shape=jax.ShapeDtypeStruct((m, n), out_dtype),
            compiler_params=_params("parallel", "arbitrary"), name=name)(a, b)
    assert mode == "tn"
    r, k1 = a.shape
    k2 = b.shape[1]
    tr = _largest_tile(r, 512, 8)
    t2 = _largest_tile(k2, 1024, 128)
    nr = r // tr

    def body(a_ref, b_ref, o_ref):
        part = lax.dot_general(a_ref[...].astype(BF16), b_ref[...].astype(BF16), (((0,), (0,)), ((), ())),
                               preferred_element_type=F32)
        i = pl.program_id(1)

        @pl.when(i == 0)
        def _():
            o_ref[...] = part

        @pl.when(i > 0)
        def _():
            o_ref[...] += part

    return pl.pallas_call(
        body, grid=(k2 // t2, nr),
        in_specs=[pl.BlockSpec((tr, k1), lambda j, i: (i, 0)), pl.BlockSpec((tr, t2), lambda j, i: (i, j))],
        out_specs=pl.BlockSpec((k1, t2), lambda j, i: (0, j)),
        out_shape=jax.ShapeDtypeStruct((k1, k2), F32),
        compiler_params=_params("parallel", "arbitrary"), name=name)(a, b)


def _exchange(name, x, bcast, sibling_only=False):
    rels = [1] if sibling_only else list(range(1, N_DEV))
    n_slot = 2 if sibling_only else N_DEV
    blk = x.shape if bcast else x.shape[1:]

    def body(x_ref, o_ref, send_sems, recv_sems, local_sem):
        mx, my, mc = lax.axis_index("x"), lax.axis_index("y"), lax.axis_index("c")
        me = mc if sibling_only else 4 * mx + 2 * my + mc
        me_dev = 4 * mx + 2 * my + mc
        mine = pltpu.make_async_copy(x_ref if bcast else x_ref.at[me_dev], o_ref.at[me], local_sem)
        mine.start()
        copies = []
        for k, r in enumerate(rels):
            px = 1 - mx if (r >> 2) & 1 else mx
            py = 1 - my if (r >> 1) & 1 else my
            pc = 1 - mc if r & 1 else mc
            src = x_ref if bcast else x_ref.at[4 * px + 2 * py + pc]
            cp = pltpu.make_async_remote_copy(
                src_ref=src, dst_ref=o_ref.at[me], send_sem=send_sems.at[k], recv_sem=recv_sems.at[k],
                device_id=(px, py, pc), device_id_type=pl.DeviceIdType.MESH)
            cp.start()
            copies.append(cp)
        for cp in copies:
            cp.wait()
        mine.wait()

    return pl.pallas_call(
        body, out_shape=jax.ShapeDtypeStruct((n_slot,) + tuple(blk), x.dtype),
        in_specs=[pl.BlockSpec(memory_space=pl.ANY)], out_specs=pl.BlockSpec(memory_space=pl.ANY),
        scratch_shapes=[pltpu.SemaphoreType.DMA((len(rels),)), pltpu.SemaphoreType.DMA((len(rels),)),
                        pltpu.SemaphoreType.DMA],
        name=name)(x)


def _sum_slots(name, x):
    s, r, w = x.shape
    tr = _largest_tile(r, 256, 8)

    def body(x_ref, o_ref):
        acc = x_ref[0]
        for j in range(1, s):
            acc = acc + x_ref[j]
        o_ref[...] = acc

    return pl.pallas_call(
        body, grid=(r // tr,), in_specs=[pl.BlockSpec((s, tr, w), lambda i: (0, i, 0))],
        out_specs=pl.BlockSpec((tr, w), lambda i: (i, 0)), out_shape=jax.ShapeDtypeStruct((r, w), F32),
        compiler_params=_params("parallel"), name=name)(x)


def _sigmoid(x):
    return 1.0 / (1.0 + jnp.exp(-x))


def _silu(x):
    return x * _sigmoid(x)


def _silu_grad(x):
    s = _sigmoid(x)
    return s * (1.0 + x * (1.0 - s))


_INV_SQRT2 = 1.0 / math.sqrt(2.0)
_INV_SQRT2PI = 1.0 / math.sqrt(2.0 * math.pi)


def _gelu(x):
    return 0.5 * x * (1.0 + lax.erf(x * _INV_SQRT2))


def _gelu_grad(x):
    return 0.5 * (1.0 + lax.erf(x * _INV_SQRT2)) + x * jnp.exp(-0.5 * x * x) * _INV_SQRT2PI


def _rms_hat(x):
    r = lax.rsqrt(jnp.mean(x * x, axis=-1, keepdims=True) + NORM_EPS)
    return x * r, r


def _rms_bwd(xh, r, dxh):
    return r * (dxh - xh * jnp.mean(dxh * xh, axis=-1, keepdims=True))


def _norm_mod_fwd(name, x, g, scale, shift, n_rows, tr, n_lat):
    def fn(xt, gv, sc, sh):
        xh, _ = _rms_hat(xt)
        return (xh * gv) * (1.0 + sc) + sh

    return _rowwise(name, fn, n_rows, tr, [(x, 0, D_MODEL)], [g, scale, shift], [(D_MODEL, F32)], [], n_lat=n_lat)[0]


def _norm_mod_bwd(name, x, g, scale, dh, dres, n_rows, tr, n_lat):
    nlt = n_lat // tr

    def fn(flag, xt, dht, drt, gv, sc):
        xh, r = _rms_hat(xt)
        n = xh * gv
        dn = dht * (1.0 + sc)
        dx = _rms_bwd(xh, r, dn * gv) + flag * drt
        return dx, dn * xh, dht * n, dht

    return _rowwise(name, fn, n_rows, tr, [(x, 0, D_MODEL), (dh, 0, D_MODEL), (dres, 0, D_MODEL, "clamp", nlt)],
                    [g, scale], [(D_MODEL, F32)], [(1, D_MODEL), (2, D_MODEL), (2, D_MODEL)], n_lat=n_lat,
                    want_flag=True)


def _s5_prep(a_re, a_im, log_dt, b_re, b_im, seg_lat, seg_ctx):
    def body(ar_ref, ai_ref, ld_ref, br_ref, bi_ref, abr_ref, abi_ref, bbr_ref, bbi_ref, alr_ref, ali_ref, acr_ref,
             aci_ref):
        lr, li = ar_ref[...], ai_ref[...]
        dt = jnp.exp(ld_ref[...])
        ldr, ldi = lr * dt, li * dt
        e = jnp.exp(ldr)
        abr, abi = e * jnp.cos(ldi), e * jnp.sin(ldi)
        abr_ref[...] = abr
        abi_ref[...] = abi
        den = lr * lr + li * li
        nr, ni = abr - 1.0, abi
        qr = (nr * lr + ni * li) / den
        qi = (ni * lr - nr * li) / den
        br, bi = br_ref[...], bi_ref[...]
        bbr_ref[...] = qr[None] * br - qi[None] * bi
        bbi_ref[...] = qr[None] * bi + qi[None] * br
        for seg, r_ref, i_ref in ((seg_lat, alr_ref, ali_ref), (seg_ctx, acr_ref, aci_ref)):
            es = jnp.exp(ldr * float(seg))
            r_ref[...] = es * jnp.cos(ldi * float(seg))
            i_ref[...] = es * jnp.sin(ldi * float(seg))

    sm = jax.ShapeDtypeStruct(a_re.shape, F32)
    big = jax.ShapeDtypeStruct(b_re.shape, F32)
    return pl.pallas_call(body, out_shape=[sm, sm, big, big, sm, sm, sm, sm], name="s5_prep")(
        a_re, a_im, log_dt, b_re, b_im)


def _s5_prep_bwd(a_re, a_im, log_dt, b_re, b_im, dabr, dabi, dbbr, dbbi):
    def body(ar_ref, ai_ref, ld_ref, br_ref, bi_ref, dabr_ref, dabi_ref, dbbr_ref, dbbi_ref,
             dar_ref, dai_ref, dld_ref, dbr_ref, dbi_ref):
        lr, li = ar_ref[...], ai_ref[...]
        dt = jnp.exp(ld_ref[...])
        ldr, ldi = lr * dt, li * dt
        e = jnp.exp(ldr)
        abr, abi = e * jnp.cos(ldi), e * jnp.sin(ldi)
        den = lr * lr + li * li
        nr, ni = abr - 1.0, abi
        qr = (nr * lr + ni * li) / den
        qi = (ni * lr - nr * li) / den
        br, bi = br_ref[...], bi_ref[...]
        gbr, gbi = dbbr_ref[...], dbbi_ref[...]
        dbr_ref[...] = gbr * qr[None] + gbi * qi[None]
        dbi_ref[...] = gbi * qr[None] - gbr * qi[None]
        dqr = jnp.sum(gbr * br + gbi * bi, axis=0)
        dqi = jnp.sum(gbi * br - gbr * bi, axis=0)
        dnr = (dqr * lr - dqi * li) / den
        dni = (dqr * li + dqi * lr) / den
        dlr_q = (dqr * (nr - 2.0 * lr * qr) + dqi * (ni - 2.0 * lr * qi)) / den
        dli_q = (dqr * (ni - 2.0 * li * qr) + dqi * (-nr - 2.0 * li * qi)) / den
        gar = dabr_ref[...] + dnr
        gai = dabi_ref[...] + dni
        dldr = gar * abr + gai * abi
        dldi = gai * abr - gar * abi
        dar_ref[...] = dldr * dt + dlr_q
        dai_ref[...] = dldi * dt + dli_q
        ddt = jnp.sum(dldr * lr + dldi * li, axis=1, keepdims=True)
        dld_ref[...] = ddt * dt

    sm = jax.ShapeDtypeStruct(a_re.shape, F32)
    big = jax.ShapeDtypeStruct(b_re.shape, F32)
    return pl.pallas_call(body, out_shape=[sm, sm, jax.ShapeDtypeStruct(log_dt.shape, F32), big, big],
                          name="s5_prep_bwd")(a_re, a_im, log_dt, b_re, b_im, dabr, dabi, dbbr, dbbi)


def _slab_cols(v):
    return v.reshape(N_SLAB, 1, HALF_W)


def _slab_pair(vr, vi):
    return jnp.concatenate([_slab_cols(vr), _slab_cols(vi)], axis=-1)


def _slab_in_matrix(bbr, bbi):
    eye = jnp.eye(SLAB_GROUPS, dtype=F32)

    def one(b):
        b = b.reshape(N_SLAB, SLAB_GROUPS, SSM_STATE, SSM_GROUP)
        m = jnp.einsum("sgph,gk->sghkp", b, eye)
        return m.reshape(N_SLAB, SLAB_W, HALF_W)

    return jnp.concatenate([one(bbr), one(bbi)], axis=-1)


def _slab_out_matrix(cr, ci):
    eye = jnp.eye(SLAB_GROUPS, dtype=F32)

    def one(c):
        c = c.reshape(N_SLAB, SLAB_GROUPS, SSM_GROUP, SSM_STATE)
        m = jnp.einsum("sghp,gk->skpgh", c, eye)
        return m.reshape(N_SLAB, HALF_W, SLAB_W)

    return jnp.concatenate([one(cr), one(-ci)], axis=1)


def _slab_diag(m):
    m = m.reshape(N_SLAB, SLAB_GROUPS, SSM_GROUP, 2, SLAB_GROUPS, SSM_STATE)
    d = jnp.stack([m[:, g, :, :, g, :] for g in range(SLAB_GROUPS)], axis=1)
    return d.transpose(3, 0, 1, 2, 4).reshape(2, SSM_GROUPS, SSM_GROUP, SSM_STATE)


def _cmul(ar, ai, xr, xi, conj):
    if conj:
        return ar * xr + ai * xi, ar * xi - ai * xr
    return ar * xr - ai * xi, ar * xi + ai * xr


def _s5_carry(name, z, a_seg, init, descending, conj):
    order = list(range(N_SEG - 1, -1, -1)) if descending else list(range(N_SEG))

    def body(z_ref, a_ref, i_ref, e_ref, o_ref):
        ar, ai = a_ref[:, :HALF_W], a_ref[:, HALF_W:]
        cr, ci = i_ref[:, :HALF_W], i_ref[:, HALF_W:]
        for j in order:
            e_ref[j, :, :HALF_W] = cr
            e_ref[j, :, HALF_W:] = ci
            pr, pi = _cmul(ar, ai, cr, ci, conj)
            cr = pr + z_ref[j, :, :HALF_W]
            ci = pi + z_ref[j, :, HALF_W:]
        o_ref[:, :HALF_W] = cr
        o_ref[:, HALF_W:] = ci

    return pl.pallas_call(body, out_shape=[jax.ShapeDtypeStruct(z.shape, F32), jax.ShapeDtypeStruct(init.shape, F32)],
                          name=name)(z, a_seg, init)


def _seg_major(v):
    return jnp.transpose(v, (1, 0, 2))


def _s5_scan(name, u, n_rows, row0, b_mat, c_mat, abar, h_in, descending, full, y_alias=None, y_rows=None):
    seg = n_rows // N_SEG
    ta = min(32, seg)
    nk = seg // ta
    assert seg * N_SEG == n_rows and nk * ta == seg and row0 % n_rows == 0 and ta % 8 == 0
    rb = row0 // n_rows
    tile = ta * N_SEG

    def body(*refs):
        if full:
            if y_alias is not None:
                u_ref, b_ref, c_ref, a_ref, hin_ref, _, hfin_ref, y_ref, hch_ref, st_ref, up_ref, h_ref = refs
            else:
                u_ref, b_ref, c_ref, a_ref, hin_ref, hfin_ref, y_ref, hch_ref, st_ref, up_ref, h_ref = refs
        else:
            u_ref, b_ref, a_ref, hin_ref, hfin_ref, st_ref, up_ref, h_ref = refs
        k = pl.program_id(1)
        kk = nk - 1 - k if descending else k
        a0 = kk * ta

        @pl.when(k == 0)
        def _():
            st_ref[...] = hin_ref[0]

        if full:
            hch_ref[0, 0] = st_ref[...]
        for al in range(ta):
            up_ref[al * N_SEG:(al + 1) * N_SEG, :] = u_ref[pl.ds(a0 + al, N_SEG, stride=seg), :]
        h_ref[...] = jnp.dot(up_ref[...].astype(BF16), b_ref[0], preferred_element_type=F32)
        ar = jnp.broadcast_to(a_ref[0, :, :HALF_W], (N_SEG, HALF_W))
        ai = jnp.broadcast_to(a_ref[0, :, HALF_W:], (N_SEG, HALF_W))

        def step(i, carry):
            hr, hi = carry
            al = ta - 1 - i if descending else i
            row = pl.multiple_of(al * N_SEG, N_SEG)
            pr, pi = _cmul(ar, ai, hr, hi, False)
            hr = pr + h_ref[pl.ds(row, N_SEG), :HALF_W]
            hi = pi + h_ref[pl.ds(row, N_SEG), HALF_W:]
            h_ref[pl.ds(row, N_SEG), :HALF_W] = hr
            h_ref[pl.ds(row, N_SEG), HALF_W:] = hi
            return hr, hi

        hr, hi = lax.fori_loop(0, ta, step, (st_ref[:, :HALF_W], st_ref[:, HALF_W:]))
        st_ref[:, :HALF_W] = hr
        st_ref[:, HALF_W:] = hi
        if full:
            yt = jnp.dot(h_ref[...].astype(BF16), c_ref[0], preferred_element_type=F32)
            for al in range(ta):
                y_ref[pl.ds(a0 + al, N_SEG, stride=seg), :] = yt[al * N_SEG:(al + 1) * N_SEG, :]

        @pl.when(k == nk - 1)
        def _():
            hfin_ref[0] = st_ref[...]

    u_spec = pl.BlockSpec((n_rows, SLAB_W), lambda s, k: (rb, s))
    b_spec = pl.BlockSpec((1, SLAB_W, STATE_W), lambda s, k: (s, 0, 0))
    c_spec = pl.BlockSpec((1, STATE_W, SLAB_W), lambda s, k: (s, 0, 0))
    a_spec = pl.BlockSpec((1, 1, STATE_W), lambda s, k: (s, 0, 0))
    st_spec = pl.BlockSpec((1, N_SEG, STATE_W), lambda s, k: (s, 0, 0))
    st_shape = jax.ShapeDtypeStruct((N_SLAB, N_SEG, STATE_W), F32)
    scratch = [pltpu.VMEM((N_SEG, STATE_W), F32), pltpu.VMEM((tile, SLAB_W), F32), pltpu.VMEM((tile, STATE_W), F32)]
    if not full:
        return pl.pallas_call(
            body, grid=(N_SLAB, nk), in_specs=[u_spec, b_spec, a_spec, st_spec], out_specs=st_spec, out_shape=st_shape,
            scratch_shapes=scratch, compiler_params=_params("parallel", "arbitrary"), name=name)(u, b_mat, abar, h_in)
    kmap = (lambda s, k: (s, nk - 1 - k, 0, 0)) if descending else (lambda s, k: (s, k, 0, 0))
    out_specs = [st_spec, u_spec, pl.BlockSpec((1, 1, N_SEG, STATE_W), kmap)]
    out_shape = [st_shape, jax.ShapeDtypeStruct((y_rows, D_MODEL), F32),
                 jax.ShapeDtypeStruct((N_SLAB, nk, N_SEG, STATE_W), F32)]
    in_specs = [u_spec, b_spec, c_spec, a_spec, st_spec]
    args = [u, b_mat, c_mat, abar, h_in]
    aliases = {}
    if y_alias is not None:
        in_specs.append(pl.BlockSpec(memory_space=pl.ANY))
        args.append(y_alias)
        aliases = {5: 1}
    return pl.pallas_call(
        body, grid=(N_SLAB, nk), in_specs=in_specs, out_specs=out_specs, out_shape=out_shape, scratch_shapes=scratch,
        input_output_aliases=aliases, compiler_params=_params("parallel", "arbitrary"), name=name)(*args)


def _s5_scan_bwd(name, u, dy, n_rows, row0, b_mat, bt_mat, ct_mat, abar, h_chunks, g_in, descending, full,
                 du_alias=None, du_rows=None):
    seg = n_rows // N_SEG
    ta = min(32, seg)
    nk = seg // ta
    rb = row0 // n_rows
    tile = ta * N_SEG
    g_desc = not descending

    def body(*refs):
        if full:
            (u_ref, dy_ref, b_ref, bt_ref, ct_ref, a_ref, hch_ref, gin_ref) = refs[:8]
            rest = refs[9:] if du_alias is not None else refs[8:]
            gfin_ref, du_ref, db_ref, dc_ref, da_ref, st_ref, up_ref, dyp_ref, h_ref, g_ref = rest
        else:
            dy_ref, ct_ref, a_ref, gin_ref, gfin_ref, st_ref, dyp_ref, g_ref = refs
        k = pl.program_id(1)
        kk = nk - 1 - k if g_desc else k
        a0 = kk * ta
        ar = jnp.broadcast_to(a_ref[0, :, :HALF_W], (N_SEG, HALF_W))
        ai = jnp.broadcast_to(a_ref[0, :, HALF_W:], (N_SEG, HALF_W))

        @pl.when(k == 0)
        def _():
            st_ref[...] = gin_ref[0]

        for al in range(ta):
            dyp_ref[al * N_SEG:(al + 1) * N_SEG, :] = dy_ref[pl.ds(a0 + al, N_SEG, stride=seg), :]
        g_ref[...] = jnp.dot(dyp_ref[...].astype(BF16), ct_ref[0], preferred_element_type=F32)

        if full:
            for al in range(ta):
                up_ref[al * N_SEG:(al + 1) * N_SEG, :] = u_ref[pl.ds(a0 + al, N_SEG, stride=seg), :]
            h_ref[...] = jnp.dot(up_ref[...].astype(BF16), b_ref[0], preferred_element_type=F32)
            h0r, h0i = hch_ref[0, 0, :, :HALF_W], hch_ref[0, 0, :, HALF_W:]

            def hstep(i, carry):
                hr, hi = carry
                al = ta - 1 - i if descending else i
                row = pl.multiple_of(al * N_SEG, N_SEG)
                pr, pi = _cmul(ar, ai, hr, hi, False)
                hr = pr + h_ref[pl.ds(row, N_SEG), :HALF_W]
                hi = pi + h_ref[pl.ds(row, N_SEG), HALF_W:]
                h_ref[pl.ds(row, N_SEG), :HALF_W] = hr
                h_ref[pl.ds(row, N_SEG), HALF_W:] = hi
                return hr, hi

            lax.fori_loop(0, ta, hstep, (h0r, h0i))

        def gstep(i, carry):
            gr, gi = carry
            al = ta - 1 - i if g_desc else i
            row = pl.multiple_of(al * N_SEG, N_SEG)
            pr, pi = _cmul(ar, ai, gr, gi, True)
            gr = pr + g_ref[pl.ds(row, N_SEG), :HALF_W]
            gi = pi + g_ref[pl.ds(row, N_SEG), HALF_W:]
            g_ref[pl.ds(row, N_SEG), :HALF_W] = gr
            g_ref[pl.ds(row, N_SEG), HALF_W:] = gi
            return gr, gi

        gr, gi = lax.fori_loop(0, ta, gstep, (st_ref[:, :HALF_W], st_ref[:, HALF_W:]))
        st_ref[:, :HALF_W] = gr
        st_ref[:, HALF_W:] = gi

        @pl.when(k == nk - 1)
        def _():
            gfin_ref[0] = st_ref[...]

        if full:
            gb = g_ref[...].astype(BF16)
            dut = jnp.dot(gb, bt_ref[0], preferred_element_type=F32)
            for al in range(ta):
                du_ref[pl.ds(a0 + al, N_SEG, stride=seg), :] = dut[al * N_SEG:(al + 1) * N_SEG, :]
            tn = (((0,), (0,)), ((), ()))
            dbp = lax.dot_general(up_ref[...].astype(BF16), gb, tn, preferred_element_type=F32)
            dcp = lax.dot_general(dyp_ref[...].astype(BF16), h_ref[...].astype(BF16), tn, preferred_element_type=F32)
            inner = (ta - 1) * N_SEG
            if descending:
                g_in_r, g_in_i = g_ref[0:inner, :HALF_W], g_ref[0:inner, HALF_W:]
                p_in_r, p_in_i = h_ref[N_SEG:tile, :HALF_W], h_ref[N_SEG:tile, HALF_W:]
                g_ed_r, g_ed_i = g_ref[inner:tile, :HALF_W], g_ref[inner:tile, HALF_W:]
            else:
                g_in_r, g_in_i = g_ref[N_SEG:tile, :HALF_W], g_ref[N_SEG:tile, HALF_W:]
                p_in_r, p_in_i = h_ref[0:inner, :HALF_W], h_ref[0:inner, HALF_W:]
                g_ed_r, g_ed_i = g_ref[0:N_SEG, :HALF_W], g_ref[0:N_SEG, HALF_W:]
            dar = g_ed_r * h0r + g_ed_i * h0i
            dai = g_ed_i * h0r - g_ed_r * h0i
            if ta > 1:
                dar = dar + jnp.sum((g_in_r * p_in_r + g_in_i * p_in_i).reshape(ta - 1, N_SEG, HALF_W), axis=0)
                dai = dai + jnp.sum((g_in_i * p_in_r - g_in_r * p_in_i).reshape(ta - 1, N_SEG, HALF_W), axis=0)

            @pl.when(k == 0)
            def _():
                db_ref[0] = dbp
                dc_ref[0] = dcp
                da_ref[0, :, :HALF_W] = dar
                da_ref[0, :, HALF_W:] = dai

            @pl.when(k > 0)
            def _():
                db_ref[0] += dbp
                dc_ref[0] += dcp
                da_ref[0, :, :HALF_W] += dar
                da_ref[0, :, HALF_W:] += dai

    u_spec = pl.BlockSpec((n_rows, SLAB_W), lambda s, k: (rb, s))
    m_spec = pl.BlockSpec((1, SLAB_W, STATE_W), lambda s, k: (s, 0, 0))
    mt_spec = pl.BlockSpec((1, STATE_W, SLAB_W), lambda s, k: (s, 0, 0))
    a_spec = pl.BlockSpec((1, 1, STATE_W), lambda s, k: (s, 0, 0))
    st_spec = pl.BlockSpec((1, N_SEG, STATE_W), lambda s, k: (s, 0, 0))
    st_shape = jax.ShapeDtypeStruct((N_SLAB, N_SEG, STATE_W), F32)
    if not full:
        scratch = [pltpu.VMEM((N_SEG, STATE_W), F32), pltpu.VMEM((tile, SLAB_W), F32), pltpu.VMEM((tile, STATE_W), F32)]
        return pl.pallas_call(
            body, grid=(N_SLAB, nk), in_specs=[u_spec, m_spec, a_spec, st_spec], out_specs=st_spec, out_shape=st_shape,
            scratch_shapes=scratch, compiler_params=_params("parallel", "arbitrary"), name=name)(dy, ct_mat, abar, g_in)
    kmap = (lambda s, k: (s, nk - 1 - k, 0, 0)) if g_desc else (lambda s, k: (s, k, 0, 0))
    in_specs = [u_spec, u_spec, m_spec, mt_spec, m_spec, a_spec, pl.BlockSpec((1, 1, N_SEG, STATE_W), kmap), st_spec]
    args = [u, dy, b_mat, bt_mat, ct_mat, abar, h_chunks, g_in]
    aliases = {}
    if du_alias is not None:
        in_specs.append(pl.BlockSpec(memory_space=pl.ANY))
        args.append(du_alias)
        aliases = {8: 1}
    acc_shape = jax.ShapeDtypeStruct((N_SLAB, SLAB_W, STATE_W), F32)
    out_specs = [st_spec, u_spec, m_spec, m_spec, st_spec]
    out_shape = [st_shape, jax.ShapeDtypeStruct((du_rows, D_MODEL), F32), acc_shape, acc_shape, st_shape]
    scratch = [pltpu.VMEM((N_SEG, STATE_W), F32), pltpu.VMEM((tile, SLAB_W), F32), pltpu.VMEM((tile, SLAB_W), F32),
               pltpu.VMEM((tile, STATE_W), F32), pltpu.VMEM((tile, STATE_W), F32)]
    return pl.pallas_call(
        body, grid=(N_SLAB, nk), in_specs=in_specs, out_specs=out_specs, out_shape=out_shape, scratch_shapes=scratch,
        input_output_aliases=aliases, compiler_params=_params("parallel", "arbitrary"), name=name)(*args)


def _rot_matrix():
    r = [[0.0] * HEAD_DIM for _ in range(HEAD_DIM)]
    q = HEAD_DIM // 4
    for blk in range(2):
        for i in range(q):
            lo, hi = blk * 2 * q + i, blk * 2 * q + q + i
            r[hi][lo] = -1.0
            r[lo][hi] = 1.0
    return jnp.array(r, F32)


def _rope_tables(n_lat, n_ctx):
    rows = n_lat // GRID_W
    row = jnp.repeat(jnp.arange(rows), GRID_W).astype(F32)
    col = jnp.tile(jnp.arange(GRID_W), rows).astype(F32)
    n_freq = HEAD_DIM // 4
    freqs = ROPE_THETA ** (-jnp.arange(n_freq, dtype=F32) / n_freq)
    ang_r = row[:, None] * freqs[None]
    ang_c = col[:, None] * freqs[None]
    cos = jnp.concatenate([jnp.cos(ang_r), jnp.cos(ang_r), jnp.cos(ang_c), jnp.cos(ang_c)], axis=1)
    sin = jnp.concatenate([jnp.sin(ang_r), jnp.sin(ang_r), jnp.sin(ang_c), jnp.sin(ang_c)], axis=1)
    cos = jnp.concatenate([cos, jnp.ones((n_ctx, HEAD_DIM), F32)], axis=0)
    sin = jnp.concatenate([sin, jnp.zeros((n_ctx, HEAD_DIM), F32)], axis=0)
    return cos, sin


def _qk_prep(name, x, w, cos, sin, rot, out_scale, rows_per_head, tr):
    nb = rows_per_head // tr

    def fn(xt, ct, st, wv, rv):
        xh, _ = _rms_hat(xt)
        n = xh * wv
        nr = jnp.dot(n, rv, precision=lax.Precision.HIGHEST, preferred_element_type=F32)
        return (n * ct + nr * st) * out_scale

    return _rowwise(name, fn, x.shape[0], tr, [(x, 0, HEAD_DIM), (cos, 0, HEAD_DIM, "mod", nb), (sin, 0, HEAD_DIM, "mod", nb)],
                    [w, rot[None]], [(HEAD_DIM, BF16)], [])[0]


def _qk_prep_bwd(name, x, w, cos, sin, rot, dy, in_scale, rows_per_head, tr):
    nb = rows_per_head // tr

    def fn(xt, ct, st, dyt, wv, rv):
        xh, r = _rms_hat(xt)
        dyt = dyt.astype(F32) * in_scale
        dyr = jnp.dot(dyt * st, rv, precision=lax.Precision.HIGHEST, preferred_element_type=F32)
        dn = dyt * ct - dyr
        return _rms_bwd(xh, r, dn * wv), dn * xh

    return _rowwise(name, fn, x.shape[0], tr,
                    [(x, 0, HEAD_DIM), (cos, 0, HEAD_DIM, "mod", nb), (sin, 0, HEAD_DIM, "mod", nb), (dy, 0, HEAD_DIM)],
                    [w, rot[None]], [(HEAD_DIM, F32)], [(1, HEAD_DIM)])


def _attn_fwd(q, k, v, tq, tk):
    nh, t, _ = q.shape
    n = k.shape[1]
    nkc = n // tk

    def body(q_ref, k_ref, v_ref, o_ref, lse_ref, m_ref, l_ref, acc_ref):
        qv = q_ref[0]
        m_ref[...] = jnp.full(m_ref.shape, -jnp.inf, F32)
        l_ref[...] = jnp.zeros(l_ref.shape, F32)
        acc_ref[...] = jnp.zeros(acc_ref.shape, F32)

        def step(kc, c):
            off = pl.multiple_of(kc * tk, tk)
            kt = k_ref[0, pl.ds(off, tk), :]
            vt = v_ref[0, pl.ds(off, tk), :]
            s = lax.dot_general(qv, kt, (((1,), (1,)), ((), ())), preferred_element_type=F32)
            m_old = m_ref[...]
            m_new = jnp.maximum(m_old, jnp.max(s, axis=-1, keepdims=True))
            p = jnp.exp(s - m_new)
            alpha = jnp.exp(m_old - m_new)
            l_ref[...] = alpha * l_ref[...] + jnp.sum(p, axis=-1, keepdims=True)
            acc_ref[...] = alpha * acc_ref[...] + jnp.dot(p.astype(BF16), vt, preferred_element_type=F32)
            m_ref[...] = m_new
            return c

        lax.fori_loop(0, nkc, step, 0)
        o_ref[0] = acc_ref[...] / l_ref[...]
        lse_ref[0] = m_ref[...] + jnp.log(l_ref[...])

    return pl.pallas_call(
        body, grid=(nh, t // tq),
        in_specs=[pl.BlockSpec((1, tq, HEAD_DIM), lambda h, i: (h, i, 0)),
                  pl.BlockSpec((1, n, HEAD_DIM), lambda h, i: (h // KV_REP, 0, 0)),
                  pl.BlockSpec((1, n, HEAD_DIM), lambda h, i: (h // KV_REP, 0, 0))],
        out_specs=[pl.BlockSpec((1, tq, HEAD_DIM), lambda h, i: (h, i, 0)), pl.BlockSpec((1, tq, 1), lambda h, i: (h, i, 0))],
        out_shape=[jax.ShapeDtypeStruct((nh, t, HEAD_DIM), F32), jax.ShapeDtypeStruct((nh, t, 1), F32)],
        scratch_shapes=[pltpu.VMEM((tq, 1), F32), pltpu.VMEM((tq, 1), F32), pltpu.VMEM((tq, HEAD_DIM), F32)],
        compiler_params=_params("parallel", "parallel"), name="attn_fwd")(q, k, v)


def _attn_bwd_dq(q, k, v, do, lse, delta, tq, tk):
    nh, t, _ = q.shape
    n = k.shape[1]
    nkc = n // tk

    def body(q_ref, k_ref, v_ref, do_ref, lse_ref, dl_ref, dq_ref):
        qv, dov = q_ref[0], do_ref[0]
        lse_v, dl_v = lse_ref[0], dl_ref[0]
        dq_ref[0] = jnp.zeros((tq, HEAD_DIM), F32)

        def step(kc, c):
            off = pl.multiple_of(kc * tk, tk)
            kt = k_ref[0, pl.ds(off, tk), :]
            vt = v_ref[0, pl.ds(off, tk), :]
            s = lax.dot_general(qv, kt, (((1,), (1,)), ((), ())), preferred_element_type=F32)
            p = jnp.exp(s - lse_v)
            dp = lax.dot_general(dov, vt, (((1,), (1,)), ((), ())), preferred_element_type=F32)
            ds = (p * (dp - dl_v)).astype(BF16)
            dq_ref[0] += jnp.dot(ds, kt, preferred_element_type=F32)
            return c

        lax.fori_loop(0, nkc, step, 0)

    qspec = pl.BlockSpec((1, tq, HEAD_DIM), lambda h, i: (h, i, 0))
    kspec = pl.BlockSpec((1, n, HEAD_DIM), lambda h, i: (h // KV_REP, 0, 0))
    cspec = pl.BlockSpec((1, tq, 1), lambda h, i: (h, i, 0))
    return pl.pallas_call(
        body, grid=(nh, t // tq), in_specs=[qspec, kspec, kspec, qspec, cspec, cspec], out_specs=qspec,
        out_shape=jax.ShapeDtypeStruct((nh, t, HEAD_DIM), F32),
        compiler_params=_params("parallel", "parallel"), name="attn_bwd_dq")(q, k, v, do, lse, delta)


def _attn_bwd_dkv(q, k, v, do, lse_row, delta_row, tq, tk):
    nh, t, _ = q.shape
    nkv, n, _ = k.shape
    nqc = t // tq

    def body(q_ref, k_ref, v_ref, do_ref, lse_ref, dl_ref, dk_ref, dv_ref):
        r = pl.program_id(2)
        kt, vt = k_ref[0], v_ref[0]

        @pl.when(r == 0)
        def _():
            dk_ref[0] = jnp.zeros((tk, HEAD_DIM), F32)
            dv_ref[0] = jnp.zeros((tk, HEAD_DIM), F32)

        def step(qc, c):
            off = pl.multiple_of(qc * tq, tq)
            qt = q_ref[0, pl.ds(off, tq), :]
            dot = do_ref[0, pl.ds(off, tq), :]
            st = lax.dot_general(kt, qt, (((1,), (1,)), ((), ())), preferred_element_type=F32)
            pt = jnp.exp(st - lse_ref[0, :, pl.ds(off, tq)])
            dv_ref[0] += jnp.dot(pt.astype(BF16), dot, preferred_element_type=F32)
            dpt = lax.dot_general(vt, dot, (((1,), (1,)), ((), ())), preferred_element_type=F32)
            dst = (pt * (dpt - dl_ref[0, :, pl.ds(off, tq)])).astype(BF16)
            dk_ref[0] += jnp.dot(dst, qt, preferred_element_type=F32)
            return c

        lax.fori_loop(0, nqc, step, 0)

    qspec = pl.BlockSpec((1, t, HEAD_DIM), lambda g, j, r: (g * KV_REP + r, 0, 0))
    kspec = pl.BlockSpec((1, tk, HEAD_DIM), lambda g, j, r: (g, j, 0))
    rspec = pl.BlockSpec((1, 1, t), lambda g, j, r: (g * KV_REP + r, 0, 0))
    return pl.pallas_call(
        body, grid=(nkv, n // tk, KV_REP), in_specs=[qspec, kspec, kspec, qspec, rspec, rspec], out_specs=[kspec, kspec],
        out_shape=[jax.ShapeDtypeStruct((nkv, n, HEAD_DIM), F32)] * 2,
        compiler_params=_params("parallel", "parallel", "arbitrary"), name="attn_bwd_dkv")(q, k, v, do, lse_row, delta_row)


def _to_heads(x, n_heads):
    r = x.shape[0]
    return x.reshape(r, n_heads, HEAD_DIM).transpose(1, 0, 2)


def _from_heads(x):
    h, r, _ = x.shape
    return x.transpose(1, 0, 2).reshape(r, h * HEAD_DIM)


def _s5_system(p, n_lat, n_ctx):
    two_g = 2 * SSM_GROUPS
    a_re = p["ssm_a_re"].reshape(two_g, SSM_STATE)
    a_im = p["ssm_a_im"].reshape(two_g, SSM_STATE)
    log_dt = p["ssm_log_dt"].reshape(two_g, 1)
    b_re = p["ssm_b_re"].reshape(two_g, SSM_STATE, SSM_GROUP).transpose(2, 0, 1)
    b_im = p["ssm_b_im"].reshape(two_g, SSM_STATE, SSM_GROUP).transpose(2, 0, 1)
    raw = (a_re, a_im, log_dt, b_re, b_im)
    abr, abi, bbr, bbi, alr, ali, acr, aci = _s5_prep(*raw, n_lat // N_SEG, n_ctx // N_SEG)
    dirs = []
    for d in range(2):
        g = slice(d * SSM_GROUPS, (d + 1) * SSM_GROUPS)
        b_mat = _slab_in_matrix(bbr[:, g].transpose(1, 2, 0), bbi[:, g].transpose(1, 2, 0))
        c_mat = _slab_out_matrix(p["ssm_c_re"][0, d], p["ssm_c_im"][0, d])
        dirs.append(dict(
            b=b_mat.astype(BF16), bt=b_mat.transpose(0, 2, 1).astype(BF16),
            c=c_mat.astype(BF16), ct=c_mat.transpose(0, 2, 1).astype(BF16),
            abar=_slab_pair(abr[g], abi[g]),
            a_lat=_slab_pair(alr[g], ali[g])[:, 0], a_ctx=_slab_pair(acr[g], aci[g])[:, 0]))
    return raw, dirs


def _s5_forward(proj, dirs, n_lat, n_ctx):
    n = n_lat + n_ctx
    zero_st = jnp.zeros((N_SLAB, N_SEG, STATE_W), F32)
    zero_c = jnp.zeros((N_SLAB, STATE_W), F32)
    ys, saved = [], []
    for d, s in enumerate(dirs):
        desc = d == 1
        tag = f"s5f{d}"
        zc = _s5_scan(tag + "_ctx_ends", proj, n_ctx, n_lat, s["b"], s["c"], s["abar"], zero_st, desc, False)
        ent_c, h0 = _s5_carry(tag + "_ctx_carry", _seg_major(zc), s["a_ctx"], zero_c, desc, False)
        _, y, hch_c = _s5_scan(tag + "_ctx", proj, n_ctx, n_lat, s["b"], s["c"], s["abar"], _seg_major(ent_c), desc,
                               True, y_rows=n)
        zl = _s5_scan(tag + "_lat_ends", proj, n_lat, 0, s["b"], s["c"], s["abar"], zero_st, desc, False)
        ent_l, _ = _s5_carry(tag + "_lat_carry", _seg_major(zl), s["a_lat"], h0, desc, False)
        _, y, hch_l = _s5_scan(tag + "_lat", proj, n_lat, 0, s["b"], s["c"], s["abar"], _seg_major(ent_l), desc,
                               True, y_alias=y, y_rows=n)
        ys.append(y)
        saved.append((hch_l, hch_c))
    return ys, saved


def _s5_backward(proj, dy, dirs, saved, n_lat, n_ctx):
    n = n_lat + n_ctx
    zero_st = jnp.zeros((N_SLAB, N_SEG, STATE_W), F32)
    zero_c = jnp.zeros((N_SLAB, STATE_W), F32)
    out = []
    for d, s in enumerate(dirs):
        desc = d == 1
        tag = f"s5b{d}"
        hch_l, hch_c = saved[d]
        gl = _s5_scan_bwd(tag + "_lat_ends", proj, dy, n_lat, 0, s["b"], s["bt"], s["ct"], s["abar"], None, zero_st,
                          desc, False)
        ent_l, g0 = _s5_carry(tag + "_lat_carry", _seg_major(gl), s["a_lat"], zero_c, not desc, True)
        _, du, db_l, dc_l, da_l = _s5_scan_bwd(tag + "_lat", proj, dy, n_lat, 0, s["b"], s["bt"], s["ct"], s["abar"],
                                               hch_l, _seg_major(ent_l), desc, True, du_rows=n)
        gc = _s5_scan_bwd(tag + "_ctx_ends", proj, dy, n_ctx, n_lat, s["b"], s["bt"], s["ct"], s["abar"], None, zero_st,
                          desc, False)
        ent_c, _ = _s5_carry(tag + "_ctx_carry", _seg_major(gc), s["a_ctx"], g0, not desc, True)
        _, du, db_c, dc_c, da_c = _s5_scan_bwd(tag + "_ctx", proj, dy, n_ctx, n_lat, s["b"], s["bt"], s["ct"], s["abar"],
                                               hch_c, _seg_major(ent_c), desc, True, du_alias=du, du_rows=n)
        out.append((du, db_l + db_c, dc_l + dc_c, da_l + da_c))
    return out


def _s5_param_grads(raw, bwd):
    dabr, dabi, dbbr, dbbi, dcr, dci = [], [], [], [], [], []
    for _, db, dc, da in bwd:
        da = jnp.sum(da, axis=1)
        dabr.append(da[:, :HALF_W].reshape(SSM_GROUPS, SSM_STATE))
        dabi.append(da[:, HALF_W:].reshape(SSM_GROUPS, SSM_STATE))
        dbd = _slab_diag(db)
        dbbr.append(dbd[0].transpose(1, 0, 2))
        dbbi.append(dbd[1].transpose(1, 0, 2))
        dcd = _slab_diag(dc)
        dcr.append(dcd[0])
        dci.append(-dcd[1])
    cat = lambda xs, ax: jnp.concatenate(xs, axis=ax)
    dar, dai, dld, dbr, dbi = _s5_prep_bwd(*raw, cat(dabr, 0), cat(dabi, 0), cat(dbbr, 1), cat(dbbi, 1))
    shp = (1, 2, SSM_GROUPS, SSM_STATE)
    b_shape = (1, 2, SSM_GROUPS, SSM_STATE, SSM_GROUP)
    return dict(
        ssm_a_re=dar.reshape(shp), ssm_a_im=dai.reshape(shp), ssm_log_dt=dld.reshape(1, 2, SSM_GROUPS),
        ssm_b_re=dbr.transpose(1, 2, 0).reshape(b_shape), ssm_b_im=dbi.transpose(1, 2, 0).reshape(b_shape),
        ssm_c_re=jnp.stack(dcr)[None], ssm_c_im=jnp.stack(dci)[None])


def _example_step(x, ctx, target, mods, w, p):
    t, c = x.shape[0], ctx.shape[0]
    n = t + c
    assert t % c == 0 and c % (8 * N_SEG) == 0 and t % GRID_W == 0
    tr = _largest_tile(c, 256, 8)
    xall = jnp.concatenate([x, ctx], axis=0)
    g0, g1 = _vec(p["norm_g"][0]), _vec(p["norm_g"][1])
    (shift0, scale0, gate0), (shift1, scale1, gate1) = [tuple(_vec(v) for v in m) for m in mods]

    h0 = _norm_mod_fwd("l0_norm", xall, g0, scale0, shift0, n, tr, t)
    proj0 = _mm("l0_in", h0, w["ssm_w_in"], "nn")
    raw, dirs = _s5_system(p, t, c)
    (y_f, y_r), saved = _s5_forward(proj0, dirs, t, c)
    d_skip = _vec(p["ssm_d"][0])

    def post_a(u, yf, yr, dv):
        y = u * dv + yf + yr
        return y, _gelu(y)

    y0, yg = _rowwise("l0_gelu", post_a, n, tr, [(proj0, 0, D_MODEL), (y_f, 0, D_MODEL), (y_r, 0, D_MODEL)], [d_skip],
                      [(D_MODEL, F32), (D_MODEL, F32)], [])
    tg = _mm("l0_glu", yg, w["ssm_w_glu"], "nn")
    b_glu = _vec(p["ssm_b_glu"][0])

    def post_b(ygt, tt, zt, bv):
        return ygt * _sigmoid(tt + bv) * _silu(zt)

    gz0 = _rowwise("l0_gate", post_b, n, tr, [(yg, 0, D_MODEL), (tg, 0, D_MODEL), (proj0, 1, D_MODEL)], [b_glu],
                   [(D_MODEL, F32)], [])[0]
    out0 = _mm("l0_out", gz0, w["ssm_w_out"], "nn")
    x1 = _rowwise("l0_res", lambda xt, ot, gv: xt + gv * ot, n, tr, [(xall, 0, D_MODEL), (out0, 0, D_MODEL)], [gate0],
                  [(D_MODEL, F32)], [], n_lat=t)[0]

    h1 = _norm_mod_fwd("l1_norm", x1, g1, scale1, shift1, n, tr, t)
    proj1 = _mm("l1_in", h1, w["attn_w_in"], "nn")
    q_raw = _to_heads(proj1[:t, :ATTN_W], N_Q_HEADS).reshape(N_Q_HEADS * t, HEAD_DIM)
    k_raw = _to_heads(proj1[:, 2 * ATTN_W:2 * ATTN_W + KV_W], N_KV_HEADS).reshape(N_KV_HEADS * n, HEAD_DIM)
    v_h = _to_heads(proj1[:, 2 * ATTN_W + KV_W:], N_KV_HEADS).astype(BF16)
    cos, sin = _rope_tables(t, c)
    rot = _rot_matrix()
    qn, kn = _vec(p["attn_q_norm"][0]), _vec(p["attn_k_norm"][0])
    q_scale = 1.0 / math.sqrt(HEAD_DIM)
    q_h = _qk_prep("l1_q_prep", q_raw, qn, cos, sin, rot, q_scale, t, tr).reshape(N_Q_HEADS, t, HEAD_DIM)
    k_h = _qk_prep("l1_k_prep", k_raw, kn, cos, sin, rot, 1.0, n, tr).reshape(N_KV_HEADS, n, HEAD_DIM)
    tq = _largest_tile(t, 1024, 128)
    tk = _largest_tile(n, 1024, 64)
    o_h, lse = _attn_fwd(q_h, k_h, v_h, tq, tk)
    o = _from_heads(o_h)
    gz1 = _rowwise("l1_gate", lambda ot, zt: ot * _silu(zt), t, tr, [(o, 0, D_MODEL), (proj1, 1, D_MODEL)], [],
                   [(D_MODEL, F32)], [])[0]
    out1 = _mm("l1_out", gz1, w["attn_w_out"], "nn")

    gf = _vec(p["final_norm_g"])

    def head(x1t, o1t, tgt, g1v, gfv):
        x2 = x1t + g1v * o1t
        xh, r = _rms_hat(x2)
        e = xh * gfv - tgt
        dyf = e * (1.0 / D_MODEL)
        dx2 = _rms_bwd(xh, r, dyf * gfv)
        return dx2, g1v * dx2, dyf * xh, dx2 * o1t, jnp.sum(e * e, axis=1, keepdims=True)

    gate1_lat = gate1[0:1]
    dx2, dout1, d_gf, d_gate1, sq = _rowwise(
        "head", head, t, tr, [(x1, 0, D_MODEL), (out1, 0, D_MODEL), (target, 0, D_MODEL)], [gate1_lat, gf],
        [(D_MODEL, F32), (D_MODEL, F32)], [(1, D_MODEL), (1, D_MODEL), (1, 1)])

    d_w_attn_out = _mm("l1_out_dw", gz1, dout1, "tn")
    dgz1 = _mm("l1_out_dx", dout1, w["attn_w_out"], "nt")
    head_sel = jnp.repeat(jnp.eye(N_Q_HEADS, dtype=F32), HEAD_DIM, axis=0)[None]

    def gate1_bwd(dgt, ot, zt, selm):
        do = dgt * _silu(zt)
        delta = jnp.dot(do * ot, selm, precision=lax.Precision.HIGHEST, preferred_element_type=F32)
        return do, dgt * ot * _silu_grad(zt), delta

    do, dz1, delta = _rowwise("l1_gate_bwd", gate1_bwd, t, tr, [(dgz1, 0, D_MODEL), (o, 0, D_MODEL), (proj1, 1, D_MODEL)],
                              [head_sel], [(D_MODEL, BF16), (D_MODEL, F32), (N_Q_HEADS, F32)], [])
    do_h = _to_heads(do, N_Q_HEADS)
    delta_h = delta.T
    dq_s = _attn_bwd_dq(q_h, k_h, v_h, do_h, lse, delta_h[:, :, None], tq, tk)
    dk_h, dv_h = _attn_bwd_dkv(q_h, k_h, v_h, do_h, lse.reshape(N_Q_HEADS, 1, t), delta_h[:, None, :], tq, tk)
    dq_raw, d_qn = _qk_prep_bwd("l1_q_prep_bwd", q_raw, qn, cos, sin, rot, dq_s.reshape(N_Q_HEADS * t, HEAD_DIM), q_scale,
                                t, tr)
    dk_raw, d_kn = _qk_prep_bwd("l1_k_prep_bwd", k_raw, kn, cos, sin, rot, dk_h.reshape(N_KV_HEADS * n, HEAD_DIM), 1.0,
                                n, tr)
    pad = jnp.zeros((c, D_MODEL), F32)
    dproj1 = jnp.concatenate([
        jnp.concatenate([_from_heads(dq_raw.reshape(N_Q_HEADS, t, HEAD_DIM)), pad], axis=0),
        jnp.concatenate([dz1, pad], axis=0),
        _from_heads(dk_raw.reshape(N_KV_HEADS, n, HEAD_DIM)), _from_heads(dv_h)], axis=1)
    d_w_attn_in = _mm("l1_in_dw", h1, dproj1, "tn")
    dh1 = _mm("l1_in_dx", dproj1, w["attn_w_in"], "nt")
    dx1, d_g1, d_scale1, d_shift1 = _norm_mod_bwd("l1_norm_bwd", x1, g1, scale1, dh1, dx2, n, tr, t)

    dout0, d_gate0 = _rowwise("l0_res_bwd", lambda dxt, ot, gv: (gv * dxt, dxt * ot), n, tr,
                              [(dx1, 0, D_MODEL), (out0, 0, D_MODEL)], [gate0], [(D_MODEL, F32)], [(2, D_MODEL)], n_lat=t)
    d_w_out = _mm("l0_out_dw", gz0, dout0, "tn")
    dgz0 = _mm("l0_out_dx", dout0, w["ssm_w_out"], "nt")

    def post_b_bwd(dgt, ygt, tt, zt, bv):
        s = _sigmoid(tt + bv)
        dy2 = dgt * _silu(zt)
        dt = dy2 * ygt * s * (1.0 - s)
        return dgt * (ygt * s) * _silu_grad(zt), dt, dy2 * s, dt

    dz0, dtg, dyg_a, d_b_glu = _rowwise(
        "l0_gate_bwd", post_b_bwd, n, tr, [(dgz0, 0, D_MODEL), (yg, 0, D_MODEL), (tg, 0, D_MODEL), (proj0, 1, D_MODEL)],
        [b_glu], [(D_MODEL, F32), (D_MODEL, F32), (D_MODEL, F32)], [(1, D_MODEL)])
    d_w_glu = _mm("l0_glu_dw", yg, dtg, "tn")
    dyg_b = _mm("l0_glu_dx", dtg, w["ssm_w_glu"], "nt")

    def post_a_bwd(da, db, yt, ut, dv):
        dy = (da + db) * _gelu_grad(yt)
        return dy, dy * dv, dy * ut

    dy0, du_skip, d_d = _rowwise("l0_gelu_bwd", post_a_bwd, n, tr,
                                 [(dyg_a, 0, D_MODEL), (dyg_b, 0, D_MODEL), (y0, 0, D_MODEL), (proj0, 0, D_MODEL)], [d_skip],
                                 [(D_MODEL, F32), (D_MODEL, F32)], [(1, D_MODEL)])
    s5_bwd = _s5_backward(proj0, dy0, dirs, saved, t, c)
    dproj0 = _rowwise("l0_in_grad", lambda a, b, cc, dz: jnp.concatenate([a + b + cc, dz], axis=1), n, tr,
                      [(du_skip, 0, D_MODEL), (s5_bwd[0][0], 0, D_MODEL), (s5_bwd[1][0], 0, D_MODEL), (dz0, 0, D_MODEL)], [],
                      [(2 * D_MODEL, F32)], [])[0]
    d_w_in = _mm("l0_in_dw", h0, dproj0, "tn")
    dh0 = _mm("l0_in_dx", dproj0, w["ssm_w_in"], "nt")
    dx0, d_g0, d_scale0, d_shift0 = _norm_mod_bwd("l0_norm_bwd", xall, g0, scale0, dh0, dx1, n, tr, t)

    big = dict(ssm_w_in=d_w_in, ssm_w_glu=d_w_glu, ssm_w_out=d_w_out, attn_w_in=d_w_attn_in, attn_w_out=d_w_attn_out)
    small = dict(
        norm_g=jnp.concatenate([d_g0[0], d_g1[0]], axis=0), ssm_d=d_d[0], ssm_b_glu=d_b_glu[0],
        attn_q_norm=d_qn[0], attn_k_norm=d_kn[0], final_norm_g=d_gf[0, 0], **_s5_param_grads(raw, s5_bwd))
    zero_v = jnp.zeros((D_MODEL,), F32)
    d_mod_lat = jnp.stack([jnp.concatenate([d_shift0[0, 0], d_scale0[0, 0], d_gate0[0, 0]]),
                           jnp.concatenate([d_shift1[0, 0], d_scale1[0, 0], d_gate1[0, 0]])])
    d_mod_ctx = jnp.stack([jnp.concatenate([d_shift0[1, 0], d_scale0[1, 0], d_gate0[1, 0]]),
                           jnp.concatenate([d_shift1[1, 0], d_scale1[1, 0], zero_v])])
    return sq[0, 0, 0], dx0[:t], big, small, d_mod_lat, d_mod_ctx


def _adamw(name, w, g, m, v):
    rows, cols = w.shape
    tr = _largest_tile(rows, 256, 8)
    c1 = 1.0 / (1.0 - ADAM_B1 ** ADAM_STEP)
    c2 = 1.0 / (1.0 - ADAM_B2 ** ADAM_STEP)

    def fn(wt, gt, mt, vt):
        mn = ADAM_B1 * mt + (1.0 - ADAM_B1) * gt
        vn = ADAM_B2 * vt + (1.0 - ADAM_B2) * (gt * gt)
        delta = -ADAM_LR * ((mn * c1) / (jnp.sqrt(vn * c2) + ADAM_EPS) + ADAM_WD * wt)
        return delta, mn, vn

    return _rowwise(name, fn, rows, tr, [(a, 0, cols) for a in (w, g, m, v)], [], [(cols, F32)] * 3, [])


BIG = ("ssm_w_in", "ssm_w_glu", "ssm_w_out", "attn_w_in", "attn_w_out")
COL_SHARDED = ("ssm_w_in", "attn_w_in")
WEIGHTS = ("c_ctx", "w_mod", "b_mod", "norm_g", "ssm_w_in", "ssm_a_re", "ssm_a_im", "ssm_log_dt", "ssm_b_re", "ssm_b_im",
           "ssm_c_re", "ssm_c_im", "ssm_d", "ssm_w_glu", "ssm_b_glu", "ssm_w_out", "attn_w_in", "attn_q_norm",
           "attn_k_norm", "attn_w_out", "final_norm_g")
SMALL = tuple(k for k in WEIGHTS if k not in BIG and k != "w_mod")
PACK_W = 1024


def _attn_in_perm(x, inverse):
    a, kv = ATTN_W, 2 * KV_W
    if inverse:
        return jnp.concatenate([x[..., :a], x[..., 2 * a:], x[..., a:2 * a]], axis=-1)
    return jnp.concatenate([x[..., :a], x[..., a + kv:], x[..., a:a + kv]], axis=-1)


def _pack(arrays, dtype, row_unit):
    flat = jnp.concatenate([a.reshape(-1).astype(dtype) for a in arrays])
    rows = -(-flat.shape[0] // PACK_W)
    rows = -(-rows // row_unit) * row_unit
    flat = jnp.concatenate([flat, jnp.zeros((rows * PACK_W - flat.shape[0],), dtype)])
    return flat.reshape(rows, PACK_W)


def _unpack(buf, shapes):
    lead = buf.shape[:-2]
    flat = buf.reshape(lead + (-1,))
    out, off = [], 0
    for shp in shapes:
        size = math.prod(shp)
        out.append(flat[..., off:off + size].reshape(lead + tuple(shp)))
        off += size
    return out


def _half_shape(name, shard_shape):
    r, ccols = shard_shape
    return (r // 2, ccols)


def kernel(x, c, ctx, c_ctx, w_mod, b_mod, norm_g, ssm_w_in, ssm_a_re, ssm_a_im, ssm_log_dt, ssm_b_re, ssm_b_im, ssm_c_re, ssm_c_im, ssm_d, ssm_w_glu, ssm_b_glu, ssm_w_out, attn_w_in, attn_q_norm, attn_k_norm, attn_w_out, final_norm_g, loss_target, m_c_ctx, m_w_mod, m_b_mod, m_norm_g, m_ssm_w_in, m_ssm_a_re, m_ssm_a_im, m_ssm_log_dt, m_ssm_b_re, m_ssm_b_im, m_ssm_c_re, m_ssm_c_im, m_ssm_d, m_ssm_w_glu, m_ssm_b_glu, m_ssm_w_out, m_attn_w_in, m_attn_q_norm, m_attn_k_norm, m_attn_w_out, m_final_norm_g, v_c_ctx, v_w_mod, v_b_mod, v_norm_g, v_ssm_w_in, v_ssm_a_re, v_ssm_a_im, v_ssm_log_dt, v_ssm_b_re, v_ssm_b_im, v_ssm_c_re, v_ssm_c_im, v_ssm_d, v_ssm_w_glu, v_ssm_b_glu, v_ssm_w_out, v_attn_w_in, v_attn_q_norm, v_attn_k_norm, v_attn_w_out, v_final_norm_g):
    args = dict(locals())
    wts = {k: args[k] for k in WEIGHTS}
    mom_m = {k: args["m_" + k] for k in WEIGHTS}
    mom_v = {k: args["v_" + k] for k in WEIGHTS}
    mx, my, mc = lax.axis_index("x"), lax.axis_index("y"), lax.axis_index("c")
    chip = 2 * mx + my
    me = 2 * chip + mc

    halves = []
    for k in BIG:
        sh = wts[k][0]
        hr = sh.shape[0] // 2
        halves.append(lax.dynamic_slice_in_dim(sh, mc * hr, hr, axis=0))
    gathered = _exchange("gather_weights", _pack(halves, BF16, 16), True)
    parts = _unpack(gathered, [h.shape for h in halves])
    w_full = {}
    for k, pc in zip(BIG, parts):
        hr, cols = pc.shape[1:]
        pc = pc.reshape(N_CHIP, 2, hr, cols)
        if k in COL_SHARDED:
            w_full[k] = pc.transpose(1, 2, 0, 3).reshape(2 * hr, N_CHIP * cols)
        else:
            w_full[k] = pc.reshape(N_CHIP * 2 * hr, cols)
    w_full["attn_w_in"] = _attn_in_perm(w_full["attn_w_in"], False)

    c_blk = jnp.concatenate([c, jnp.zeros((7, D_MODEL), F32)], axis=0)
    c_all = _exchange("gather_c", c_blk, True)[:, 0]
    cond = jnp.concatenate([c_all, c_ctx[None], jnp.zeros((7, D_MODEL), F32)], axis=0)
    s_cond, ds_cond = _rowwise("cond_silu", lambda t: (_silu(t), _silu_grad(t)), 16, 16, [(cond, 0, D_MODEL)], [],
                               [(D_MODEL, F32), (D_MODEL, F32)], [])
    w_mod_b = w_mod.astype(BF16)
    mcols = w_mod.shape[2]
    mod_part = jnp.stack([_mm(f"mod{i}", s_cond, w_mod_b[i], "nn") for i in range(2)])
    mod_g = _exchange("gather_mod", mod_part.reshape(32, mcols), True)
    mod_all = mod_g.reshape(N_CHIP, 2, 2, 16, mcols)[:, 0]
    mod_all = mod_all.transpose(1, 2, 0, 3).reshape(2, 16, N_CHIP * mcols) + b_mod[:, None, :]
    mods = []
    for i in range(2):
        lat = lax.dynamic_slice_in_dim(mod_all[i], me, 1, axis=0)[0]
        both = jnp.stack([lat, mod_all[i, 8]])
        mods.append((both[:, :D_MODEL], both[:, D_MODEL:2 * D_MODEL], both[:, 2 * D_MODEL:]))

    small_p = {k: wts[k] for k in SMALL if k != "c_ctx" and k != "b_mod"}
    sq, grad_x, big_g, small_g, d_mod_lat, d_mod_ctx = _example_step(x[0], ctx[0], loss_target[0], mods, w_full, small_p)
    loss = lax.psum(0.5 / D_MODEL * sq, ("x", "y", "c"))
    big_g["attn_w_in"] = _attn_in_perm(big_g["attn_w_in"], True)

    small_names = [k for k in SMALL if k not in ("c_ctx", "b_mod")]
    small_list = [small_g[k] for k in small_names] + [d_mod_lat, d_mod_ctx]
    small_shapes = [wts[k].shape for k in small_names] + [d_mod_lat.shape, d_mod_ctx.shape]
    sg = _exchange("gather_small", _pack(small_list, F32, 8), True)
    sg_sum = _sum_slots("sum_small", sg)
    summed = _unpack(sg_sum, small_shapes)
    grads = dict(zip(small_names, summed[:-2]))
    d_mod_lat_sum, d_mod_ctx_sum = summed[-2], summed[-1]
    grads["b_mod"] = d_mod_lat_sum + d_mod_ctx_sum
    d_mod_lat_all = _unpack(sg, small_shapes)[-2]

    g_w_mod, ds_cc = [], []
    for i in range(2):
        rows9 = jnp.concatenate([d_mod_lat_all[:, i], d_mod_ctx_sum[i][None], jnp.zeros((7, 3 * D_MODEL), F32)], axis=0)
        mine = lax.dynamic_slice_in_dim(rows9, chip * mcols, mcols, axis=1)
        g_w_mod.append(_mm(f"mod{i}_dw", s_cond, mine, "tn"))
        ds_cc.append(_mm(f"mod{i}_dx", mine, w_mod_b[i], "nt")[8])
    grads["w_mod"] = jnp.stack(g_w_mod)
    part = (ds_cc[0] + ds_cc[1]) * jnp.where(mc == 0, 1.0, 0.0)
    part_blk = jnp.concatenate([part[None], jnp.zeros((7, D_MODEL), F32)], axis=0)
    ds_all = _sum_slots("sum_c_ctx", _exchange("gather_c_ctx", part_blk, True))
    grads["c_ctx"] = ds_all[0] * ds_cond[8]

    blocks = []
    for k in BIG:
        g = big_g[k]
        rows, cols = g.shape
        if k in COL_SHARDED:
            blocks.append(g.reshape(2, rows // 2, N_CHIP, cols // N_CHIP).transpose(2, 0, 1, 3).reshape(N_DEV, -1))
        else:
            blocks.append(g.reshape(N_DEV, -1))
    sendbuf = jnp.concatenate(blocks, axis=1)
    sendbuf = sendbuf.reshape(N_DEV, -1, PACK_W)
    recv = _exchange("scatter_big", sendbuf, False)
    mine = _sum_slots("sum_big", recv)
    both = _exchange("swap_halves", mine, True, sibling_only=True)
    half_shapes = [(wts[k].shape[1] // 2, wts[k].shape[2]) for k in BIG]
    for k, pc in zip(BIG, _unpack(both, half_shapes)):
        grads[k] = pc.reshape(wts[k].shape)

    delta, new_m, new_v = {}, {}, {}
    for k in BIG + ("w_mod",):
        shp = wts[k].shape
        two_d = (-1, shp[-1])
        res = _adamw("adamw_" + k, *[a.reshape(two_d) for a in (wts[k], grads[k], mom_m[k], mom_v[k])])
        delta[k], new_m[k], new_v[k] = [r.reshape(shp) for r in res]
    shapes = [wts[k].shape for k in SMALL]
    packed = [_pack([d[k] for k in SMALL], F32, 8) for d in (wts, grads, mom_m, mom_v)]
    res = _adamw("adamw_small", *packed)
    for dst, buf in zip((delta, new_m, new_v), res):
        for k, a in zip(SMALL, _unpack(buf, shapes)):
            dst[k] = a
    grads = {k: grads[k].reshape(wts[k].shape) for k in WEIGHTS}
    return (loss, grad_x[None], *[grads[k] for k in WEIGHTS], *[delta[k] for k in WEIGHTS],
            *[new_m[k] for k in WEIGHTS], *[new_v[k] for k in WEIGHTS])
```

```python
import functools
import math

import jax
import jax.numpy as jnp
from jax import lax
from jax.experimental import pallas as pl
from jax.experimental.pallas import tpu as pltpu

F32 = jnp.float32
BF16 = jnp.bfloat16

D_MODEL = 1024
NORM_EPS = 1e-6
SSM_GROUPS = 64
SSM_GROUP = 16
SSM_STATE = 64
SLAB_W = 128
N_SLAB = D_MODEL // SLAB_W
SLAB_GROUPS = SLAB_W // SSM_GROUP
HALF_W = SLAB_GROUPS * SSM_STATE
STATE_W = 2 * HALF_W
N_SEG = 8
HEAD_DIM = 64
N_Q_HEADS = 16
N_KV_HEADS = 4
KV_REP = N_Q_HEADS // N_KV_HEADS
ATTN_W = N_Q_HEADS * HEAD_DIM
KV_W = N_KV_HEADS * HEAD_DIM
GRID_W = 64
ROPE_THETA = 10000.0
N_DEV = 8
N_CHIP = 4
VMEM_LIMIT_BYTES = 56 * 1024 * 1024

ADAM_LR = 0.001
ADAM_B1 = 0.9
ADAM_B2 = 0.999
ADAM_EPS = 1e-08
ADAM_WD = 0.01
ADAM_STEP = 10


def _params(*sem):
    return pltpu.CompilerParams(dimension_semantics=sem, vmem_limit_bytes=VMEM_LIMIT_BYTES)


def _largest_tile(n, cap, unit):
    if n <= cap:
        return n
    t = (cap // unit) * unit
    while t >= unit:
        if n % t == 0:
            return t
        t -= unit
    raise ValueError(f"no tile for {n} (cap {cap}, unit {unit})")


def _rowwise(name, fn, n_rows, tr, row_ins, vec_ins, row_outs, red_outs, n_lat=None, want_flag=False):
    nt = n_rows // tr
    assert nt * tr == n_rows
    nlt = nt if n_lat is None else n_lat // tr

    def sel(i):
        return jnp.where(i >= nlt, 1, 0)

    arrays, in_specs = [], []
    for spec in row_ins:
        arr, cb, w = spec[:3]
        kind = spec[3] if len(spec) > 3 else None
        m = spec[4] if len(spec) > 4 else None
        if kind == "mod":
            imap = functools.partial(lambda i, cb, m: (i % m, cb), cb=cb, m=m)
        elif kind == "clamp":
            imap = functools.partial(lambda i, cb, m: (jnp.minimum(i, m - 1), cb), cb=cb, m=m)
        else:
            imap = functools.partial(lambda i, cb: (i, cb), cb=cb)
        arrays.append(arr)
        in_specs.append(pl.BlockSpec((tr, w), imap))
    for v in vec_ins:
        s, a, w = v.shape
        imap = (lambda i: (sel(i), 0, 0)) if s == 2 else (lambda i: (0, 0, 0))
        arrays.append(v)
        in_specs.append(pl.BlockSpec((1, a, w), imap))
    out_shapes, out_specs = [], []
    for w, dt in row_outs:
        out_shapes.append(jax.ShapeDtypeStruct((n_rows, w), dt))
        out_specs.append(pl.BlockSpec((tr, w), lambda i: (i, 0)))
    for s, w in red_outs:
        out_shapes.append(jax.ShapeDtypeStruct((s, 1, w), F32))
        imap = (lambda i: (sel(i), 0, 0)) if s == 2 else (lambda i: (0, 0, 0))
        out_specs.append(pl.BlockSpec((1, 1, w), imap))
    n_ri, n_vi, n_ro, n_rd = len(row_ins), len(vec_ins), len(row_outs), len(red_outs)

    def body(*refs):
        i = pl.program_id(0)
        rows = [r[...] for r in refs[:n_ri]]
        vecs = [r[0] for r in refs[n_ri:n_ri + n_vi]]
        outs = refs[n_ri + n_vi:]
        lead = [jnp.where(i < nlt, 1.0, 0.0).astype(F32)] if want_flag else []
        res = fn(*lead, *rows, *vecs)
        if not isinstance(res, (tuple, list)):
            res = (res,)
        assert len(res) == n_ro + n_rd
        for k in range(n_ro):
            outs[k][...] = res[k].astype(outs[k].dtype)
        for k in range(n_rd):
            part = jnp.sum(res[n_ro + k].astype(F32), axis=0, keepdims=True)
            first = i == 0
            if red_outs[k][0] == 2:
                first = jnp.logical_or(first, i == nlt)
            o = outs[n_ro + k]

            @pl.when(first)
            def _():
                o[0] = part

            @pl.when(jnp.logical_not(first))
            def _():
                o[0] = o[0] + part

    res = pl.pallas_call(
        body, grid=(nt,), in_specs=in_specs, out_specs=out_specs, out_shape=out_shapes,
        compiler_params=_params("arbitrary"), name=name)(*arrays)
    return res


def _vec(v):
    v = v.astype(F32)
    if v.ndim == 1:
        v = v[None]
    return v[:, None, :]


def _mm(name, a, b, mode, out_dtype=F32):
    if mode in ("nn", "nt"):
        m, k = a.shape
        n = b.shape[1] if mode == "nn" else b.shape[0]
        tm = _largest_tile(m, 512, 8)
        tn = _largest_tile(n, 1024, 128)
        contract = (((1,), (0,)), ((), ())) if mode == "nn" else (((1,), (1,)), ((), ()))

        def body(a_ref, b_ref, o_ref):
            o_ref[...] = lax.dot_general(a_ref[...].astype(BF16), b_ref[...].astype(BF16), contract,
                                         preferred_element_type=F32).astype(o_ref.dtype)

        b_spec = pl.BlockSpec((k, tn), lambda i, j: (0, j)) if mode == "nn" else pl.BlockSpec((tn, k), lambda i, j: (j, 0))
        return pl.pallas_call(
            body, grid=(m // tm, n // tn),
            in_specs=[pl.BlockSpec((tm, k), lambda i, j: (i, 0)), b_spec],
            out_specs=pl.BlockSpec((tm, tn), lambda i, j: (i, j)),
            out_shape=jax.ShapeDtypeStruct((m, n), out_dtype),
            compiler_params=_params("parallel", "arbitrary"), name=name)(a, b)
    assert mode == "tn"
    r, k1 = a.shape
    k2 = b.shape[1]
    tr = _largest_tile(r, 512, 8)
    t2 = _largest_tile(k2, 1024, 128)
    nr = r // tr

    def body(a_ref, b_ref, o_ref):
        part = lax.dot_general(a_ref[...].astype(BF16), b_ref[...].astype(BF16), (((0,), (0,)), ((), ())),
                               preferred_element_type=F32)
        i = pl.program_id(1)

        @pl.when(i == 0)
        def _():
            o_ref[...] = part

        @pl.when(i > 0)
        def _():
            o_ref[...] += part

    return pl.pallas_call(
        body, grid=(k2 // t2, nr),
        in_specs=[pl.BlockSpec((tr, k1), lambda j, i: (i, 0)), pl.BlockSpec((tr, t2), lambda j, i: (i, j))],
        out_specs=pl.BlockSpec((k1, t2), lambda j, i: (0, j)),
        out_shape=jax.ShapeDtypeStruct((k1, k2), F32),
        compiler_params=_params("parallel", "arbitrary"), name=name)(a, b)


def _exchange(name, x, bcast, sibling_only=False):
    rels = [1] if sibling_only else list(range(1, N_DEV))
    n_slot = 2 if sibling_only else N_DEV
    blk = x.shape if bcast else x.shape[1:]

    def body(x_ref, o_ref, send_sems, recv_sems, local_sem):
        mx, my, mc = lax.axis_index("x"), lax.axis_index("y"), lax.axis_index("c")
        me = mc if sibling_only else 4 * mx + 2 * my + mc
        me_dev = 4 * mx + 2 * my + mc
        mine = pltpu.make_async_copy(x_ref if bcast else x_ref.at[me_dev], o_ref.at[me], local_sem)
        mine.start()
        copies = []
        for k, r in enumerate(rels):
            px = 1 - mx if (r >> 2) & 1 else mx
            py = 1 - my if (r >> 1) & 1 else my
            pc = 1 - mc if r & 1 else mc
            src = x_ref if bcast else x_ref.at[4 * px + 2 * py + pc]
            cp = pltpu.make_async_remote_copy(
                src_ref=src, dst_ref=o_ref.at[me], send_sem=send_sems.at[k], recv_sem=recv_sems.at[k],
                device_id=(px, py, pc), device_id_type=pl.DeviceIdType.MESH)
            cp.start()
            copies.append(cp)
        for cp in copies:
            cp.wait()
        mine.wait()

    return pl.pallas_call(
        body, out_shape=jax.ShapeDtypeStruct((n_slot,) + tuple(blk), x.dtype),
        in_specs=[pl.BlockSpec(memory_space=pl.ANY)], out_specs=pl.BlockSpec(memory_space=pl.ANY),
        scratch_shapes=[pltpu.SemaphoreType.DMA((len(rels),)), pltpu.SemaphoreType.DMA((len(rels),)),
                        pltpu.SemaphoreType.DMA],
        name=name)(x)


def _sum_slots(name, x):
    s, r, w = x.shape
    tr = _largest_tile(r, 256, 8)

    def body(x_ref, o_ref):
        acc = x_ref[0].astype(F32)
        for j in range(1, s):
            acc = acc + x_ref[j].astype(F32)
        o_ref[...] = acc

    return pl.pallas_call(
        body, grid=(r // tr,), in_specs=[pl.BlockSpec((s, tr, w), lambda i: (0, i, 0))],
        out_specs=pl.BlockSpec((tr, w), lambda i: (i, 0)), out_shape=jax.ShapeDtypeStruct((r, w), F32),
        compiler_params=_params("parallel"), name=name)(x)


def _sigmoid(x):
    return 1.0 / (1.0 + jnp.exp(-x))


def _silu(x):
    return x * _sigmoid(x)


def _silu_grad(x):
    s = _sigmoid(x)
    return s * (1.0 + x * (1.0 - s))


_INV_SQRT2 = 1.0 / math.sqrt(2.0)
_INV_SQRT2PI = 1.0 / math.sqrt(2.0 * math.pi)


def _gelu(x):
    return 0.5 * x * (1.0 + lax.erf(x * _INV_SQRT2))


def _gelu_grad(x):
    return 0.5 * (1.0 + lax.erf(x * _INV_SQRT2)) + x * jnp.exp(-0.5 * x * x) * _INV_SQRT2PI


def _rms_hat(x):
    r = lax.rsqrt(jnp.mean(x * x, axis=-1, keepdims=True) + NORM_EPS)
    return x * r, r


def _rms_bwd(xh, r, dxh):
    return r * (dxh - xh * jnp.mean(dxh * xh, axis=-1, keepdims=True))


def _norm_mod_fwd(name, x, g, scale, shift, n_rows, tr, n_lat):
    def fn(xt, gv, sc, sh):
        xh, _ = _rms_hat(xt)
        return (xh * gv) * (1.0 + sc) + sh

    return _rowwise(name, fn, n_rows, tr, [(x, 0, D_MODEL)], [g, scale, shift], [(D_MODEL, F32)], [], n_lat=n_lat)[0]


def _norm_mod_bwd(name, x, g, scale, dh, dres, n_rows, tr, n_lat):
    nlt = n_lat // tr

    def fn(flag, xt, dht, drt, gv, sc):
        xh, r = _rms_hat(xt)
        n = xh * gv
        dn = dht * (1.0 + sc)
        dx = _rms_bwd(xh, r, dn * gv) + flag * drt
        return dx, dn * xh, dht * n, dht

    return _rowwise(name, fn, n_rows, tr, [(x, 0, D_MODEL), (dh, 0, D_MODEL), (dres, 0, D_MODEL, "clamp", nlt)],
                    [g, scale], [(D_MODEL, F32)], [(1, D_MODEL), (2, D_MODEL), (2, D_MODEL)], n_lat=n_lat,
                    want_flag=True)


def _s5_prep(a_re, a_im, log_dt, b_re, b_im, seg_lat, seg_ctx):
    def body(ar_ref, ai_ref, ld_ref, br_ref, bi_ref, abr_ref, abi_ref, bbr_ref, bbi_ref, alr_ref, ali_ref, acr_ref,
             aci_ref):
        lr, li = ar_ref[...], ai_ref[...]
        dt = jnp.exp(ld_ref[...])
        ldr, ldi = lr * dt, li * dt
        e = jnp.exp(ldr)
        abr, abi = e * jnp.cos(ldi), e * jnp.sin(ldi)
        abr_ref[...] = abr
        abi_ref[...] = abi
        den = lr * lr + li * li
        nr, ni = abr - 1.0, abi
        qr = (nr * lr + ni * li) / den
        qi = (ni * lr - nr * li) / den
        br, bi = br_ref[...], bi_ref[...]
        bbr_ref[...] = qr[None] * br - qi[None] * bi
        bbi_ref[...] = qr[None] * bi + qi[None] * br
        for seg, r_ref, i_ref in ((seg_lat, alr_ref, ali_ref), (seg_ctx, acr_ref, aci_ref)):
            es = jnp.exp(ldr * float(seg))
            r_ref[...] = es * jnp.cos(ldi * float(seg))
            i_ref[...] = es * jnp.sin(ldi * float(seg))

    sm = jax.ShapeDtypeStruct(a_re.shape, F32)
    big = jax.ShapeDtypeStruct(b_re.shape, F32)
    return pl.pallas_call(body, out_shape=[sm, sm, big, big, sm, sm, sm, sm], name="s5_prep")(
        a_re, a_im, log_dt, b_re, b_im)


def _s5_prep_bwd(a_re, a_im, log_dt, b_re, b_im, dabr, dabi, dbbr, dbbi):
    def body(ar_ref, ai_ref, ld_ref, br_ref, bi_ref, dabr_ref, dabi_ref, dbbr_ref, dbbi_ref,
             dar_ref, dai_ref, dld_ref, dbr_ref, dbi_ref):
        lr, li = ar_ref[...], ai_ref[...]
        dt = jnp.exp(ld_ref[...])
        ldr, ldi = lr * dt, li * dt
        e = jnp.exp(ldr)
        abr, abi = e * jnp.cos(ldi), e * jnp.sin(ldi)
        den = lr * lr + li * li
        nr, ni = abr - 1.0, abi
        qr = (nr * lr + ni * li) / den
        qi = (ni * lr - nr * li) / den
        br, bi = br_ref[...], bi_ref[...]
        gbr, gbi = dbbr_ref[...], dbbi_ref[...]
        dbr_ref[...] = gbr * qr[None] + gbi * qi[None]
        dbi_ref[...] = gbi * qr[None] - gbr * qi[None]
        dqr = jnp.sum(gbr * br + gbi * bi, axis=0)
        dqi = jnp.sum(gbi * br - gbr * bi, axis=0)
        dnr = (dqr * lr - dqi * li) / den
        dni = (dqr * li + dqi * lr) / den
        dlr_q = (dqr * (nr - 2.0 * lr * qr) + dqi * (ni - 2.0 * lr * qi)) / den
        dli_q = (dqr * (ni - 2.0 * li * qr) + dqi * (-nr - 2.0 * li * qi)) / den
        gar = dabr_ref[...] + dnr
        gai = dabi_ref[...] + dni
        dldr = gar * abr + gai * abi
        dldi = gai * abr - gar * abi
        dar_ref[...] = dldr * dt + dlr_q
        dai_ref[...] = dldi * dt + dli_q
        ddt = jnp.sum(dldr * lr + dldi * li, axis=1, keepdims=True)
        dld_ref[...] = ddt * dt

    sm = jax.ShapeDtypeStruct(a_re.shape, F32)
    big = jax.ShapeDtypeStruct(b_re.shape, F32)
    return pl.pallas_call(body, out_shape=[sm, sm, jax.ShapeDtypeStruct(log_dt.shape, F32), big, big],
                          name="s5_prep_bwd")(a_re, a_im, log_dt, b_re, b_im, dabr, dabi, dbbr, dbbi)


def _slab_cols(v):
    return v.reshape(N_SLAB, 1, HALF_W)


def _slab_pair(vr, vi):
    return jnp.concatenate([_slab_cols(vr), _slab_cols(vi)], axis=-1)


def _slab_in_matrix(bbr, bbi):
    eye = jnp.eye(SLAB_GROUPS, dtype=F32)

    def one(b):
        b = b.reshape(N_SLAB, SLAB_GROUPS, SSM_STATE, SSM_GROUP)
        m = jnp.einsum("sgph,gk->sghkp", b, eye)
        return m.reshape(N_SLAB, SLAB_W, HALF_W)

    return jnp.concatenate([one(bbr), one(bbi)], axis=-1)


def _slab_out_matrix(cr, ci):
    eye = jnp.eye(SLAB_GROUPS, dtype=F32)

    def one(c):
        c = c.reshape(N_SLAB, SLAB_GROUPS, SSM_GROUP, SSM_STATE)
        m = jnp.einsum("sghp,gk->skpgh", c, eye)
        return m.reshape(N_SLAB, HALF_W, SLAB_W)

    return jnp.concatenate([one(cr), one(-ci)], axis=1)


def _slab_diag(m):
    m = m.reshape(N_SLAB, SLAB_GROUPS, SSM_GROUP, 2, SLAB_GROUPS, SSM_STATE)
    d = jnp.stack([m[:, g, :, :, g, :] for g in range(SLAB_GROUPS)], axis=1)
    return d.transpose(3, 0, 1, 2, 4).reshape(2, SSM_GROUPS, SSM_GROUP, SSM_STATE)


def _cmul(ar, ai, xr, xi, conj):
    if conj:
        return ar * xr + ai * xi, ar * xi - ai * xr
    return ar * xr - ai * xi, ar * xi + ai * xr


def _s5_carry(name, z, a_seg, init, descending, conj):
    order = list(range(N_SEG - 1, -1, -1)) if descending else list(range(N_SEG))

    def body(z_ref, a_ref, i_ref, e_ref, o_ref):
        ar, ai = a_ref[:, :HALF_W], a_ref[:, HALF_W:]
        cr, ci = i_ref[:, :HALF_W], i_ref[:, HALF_W:]
        for j in order:
            e_ref[j, :, :HALF_W] = cr
            e_ref[j, :, HALF_W:] = ci
            pr, pi = _cmul(ar, ai, cr, ci, conj)
            cr = pr + z_ref[j, :, :HALF_W]
            ci = pi + z_ref[j, :, HALF_W:]
        o_ref[:, :HALF_W] = cr
        o_ref[:, HALF_W:] = ci

    return pl.pallas_call(body, out_shape=[jax.ShapeDtypeStruct(z.shape, F32), jax.ShapeDtypeStruct(init.shape, F32)],
                          name=name)(z, a_seg, init)


def _seg_major(v):
    return jnp.transpose(v, (1, 0, 2))


def _s5_scan(name, u, n_rows, row0, b_mat, c_mat, abar, h_in, descending, full, y_alias=None, y_rows=None):
    seg = n_rows // N_SEG
    ta = min(32, seg)
    nk = seg // ta
    assert seg * N_SEG == n_rows and nk * ta == seg and row0 % n_rows == 0 and ta % 8 == 0
    rb = row0 // n_rows
    tile = ta * N_SEG

    def body(*refs):
        if full:
            if y_alias is not None:
                u_ref, b_ref, c_ref, a_ref, hin_ref, _, hfin_ref, y_ref, hch_ref, st_ref, up_ref, h_ref = refs
            else:
                u_ref, b_ref, c_ref, a_ref, hin_ref, hfin_ref, y_ref, hch_ref, st_ref, up_ref, h_ref = refs
        else:
            u_ref, b_ref, a_ref, hin_ref, hfin_ref, st_ref, up_ref, h_ref = refs
        k = pl.program_id(1)
        kk = nk - 1 - k if descending else k
        a0 = kk * ta

        @pl.when(k == 0)
        def _():
            st_ref[...] = hin_ref[0]

        if full:
            hch_ref[0, 0] = st_ref[...]
        for al in range(ta):
            up_ref[al * N_SEG:(al + 1) * N_SEG, :] = u_ref[pl.ds(a0 + al, N_SEG, stride=seg), :]
        h_ref[...] = jnp.dot(up_ref[...].astype(BF16), b_ref[0], preferred_element_type=F32)
        ar = jnp.broadcast_to(a_ref[0, :, :HALF_W], (N_SEG, HALF_W))
        ai = jnp.broadcast_to(a_ref[0, :, HALF_W:], (N_SEG, HALF_W))

        def step(i, carry):
            hr, hi = carry
            al = ta - 1 - i if descending else i
            row = pl.multiple_of(al * N_SEG, N_SEG)
            pr, pi = _cmul(ar, ai, hr, hi, False)
            hr = pr + h_ref[pl.ds(row, N_SEG), :HALF_W]
            hi = pi + h_ref[pl.ds(row, N_SEG), HALF_W:]
            h_ref[pl.ds(row, N_SEG), :HALF_W] = hr
            h_ref[pl.ds(row, N_SEG), HALF_W:] = hi
            return hr, hi

        hr, hi = lax.fori_loop(0, ta, step, (st_ref[:, :HALF_W], st_ref[:, HALF_W:]))
        st_ref[:, :HALF_W] = hr
        st_ref[:, HALF_W:] = hi
        if full:
            yt = jnp.dot(h_ref[...].astype(BF16), c_ref[0], preferred_element_type=F32)
            for al in range(ta):
                y_ref[pl.ds(a0 + al, N_SEG, stride=seg), :] = yt[al * N_SEG:(al + 1) * N_SEG, :]

        @pl.when(k == nk - 1)
        def _():
            hfin_ref[0] = st_ref[...]

    u_spec = pl.BlockSpec((n_rows, SLAB_W), lambda s, k: (rb, s))
    b_spec = pl.BlockSpec((1, SLAB_W, STATE_W), lambda s, k: (s, 0, 0))
    c_spec = pl.BlockSpec((1, STATE_W, SLAB_W), lambda s, k: (s, 0, 0))
    a_spec = pl.BlockSpec((1, 1, STATE_W), lambda s, k: (s, 0, 0))
    st_spec = pl.BlockSpec((1, N_SEG, STATE_W), lambda s, k: (s, 0, 0))
    st_shape = jax.ShapeDtypeStruct((N_SLAB, N_SEG, STATE_W), F32)
    scratch = [pltpu.VMEM((N_SEG, STATE_W), F32), pltpu.VMEM((tile, SLAB_W), F32), pltpu.VMEM((tile, STATE_W), F32)]
    if not full:
        return pl.pallas_call(
            body, grid=(N_SLAB, nk), in_specs=[u_spec, b_spec, a_spec, st_spec], out_specs=st_spec, out_shape=st_shape,
            scratch_shapes=scratch, compiler_params=_params("parallel", "arbitrary"), name=name)(u, b_mat, abar, h_in)
    kmap = (lambda s, k: (s, nk - 1 - k, 0, 0)) if descending else (lambda s, k: (s, k, 0, 0))
    out_specs = [st_spec, u_spec, pl.BlockSpec((1, 1, N_SEG, STATE_W), kmap)]
    out_shape = [st_shape, jax.ShapeDtypeStruct((y_rows, D_MODEL), F32),
                 jax.ShapeDtypeStruct((N_SLAB, nk, N_SEG, STATE_W), F32)]
    in_specs = [u_spec, b_spec, c_spec, a_spec, st_spec]
    args = [u, b_mat, c_mat, abar, h_in]
    aliases = {}
    if y_alias is not None:
        in_specs.append(pl.BlockSpec(memory_space=pl.ANY))
        args.append(y_alias)
        aliases = {5: 1}
    return pl.pallas_call(
        body, grid=(N_SLAB, nk), in_specs=in_specs, out_specs=out_specs, out_shape=out_shape, scratch_shapes=scratch,
        input_output_aliases=aliases, compiler_params=_params("parallel", "arbitrary"), name=name)(*args)


def _s5_scan_bwd(name, u, dy, n_rows, row0, b_mat, bt_mat, ct_mat, abar, h_chunks, g_in, descending, full,
                 du_alias=None, du_rows=None):
    seg = n_rows // N_SEG
    ta = min(32, seg)
    nk = seg // ta
    rb = row0 // n_rows
    tile = ta * N_SEG
    g_desc = not descending

    def body(*refs):
        if full:
            (u_ref, dy_ref, b_ref, bt_ref, ct_ref, a_ref, hch_ref, gin_ref) = refs[:8]
            rest = refs[9:] if du_alias is not None else refs[8:]
            gfin_ref, du_ref, db_ref, dc_ref, da_ref, st_ref, up_ref, dyp_ref, h_ref, g_ref = rest
        else:
            dy_ref, ct_ref, a_ref, gin_ref, gfin_ref, st_ref, dyp_ref, g_ref = refs
        k = pl.program_id(1)
        kk = nk - 1 - k if g_desc else k
        a0 = kk * ta
        ar = jnp.broadcast_to(a_ref[0, :, :HALF_W], (N_SEG, HALF_W))
        ai = jnp.broadcast_to(a_ref[0, :, HALF_W:], (N_SEG, HALF_W))

        @pl.when(k == 0)
        def _():
            st_ref[...] = gin_ref[0]

        for al in range(ta):
            dyp_ref[al * N_SEG:(al + 1) * N_SEG, :] = dy_ref[pl.ds(a0 + al, N_SEG, stride=seg), :]
        g_ref[...] = jnp.dot(dyp_ref[...].astype(BF16), ct_ref[0], preferred_element_type=F32)

        if full:
            for al in range(ta):
                up_ref[al * N_SEG:(al + 1) * N_SEG, :] = u_ref[pl.ds(a0 + al, N_SEG, stride=seg), :]
            h_ref[...] = jnp.dot(up_ref[...].astype(BF16), b_ref[0], preferred_element_type=F32)
            h0r, h0i = hch_ref[0, 0, :, :HALF_W], hch_ref[0, 0, :, HALF_W:]

            def hstep(i, carry):
                hr, hi = carry
                al = ta - 1 - i if descending else i
                row = pl.multiple_of(al * N_SEG, N_SEG)
                pr, pi = _cmul(ar, ai, hr, hi, False)
                hr = pr + h_ref[pl.ds(row, N_SEG), :HALF_W]
                hi = pi + h_ref[pl.ds(row, N_SEG), HALF_W:]
                h_ref[pl.ds(row, N_SEG), :HALF_W] = hr
                h_ref[pl.ds(row, N_SEG), HALF_W:] = hi
                return hr, hi

            lax.fori_loop(0, ta, hstep, (h0r, h0i))

        def gstep(i, carry):
            gr, gi = carry
            al = ta - 1 - i if g_desc else i
            row = pl.multiple_of(al * N_SEG, N_SEG)
            pr, pi = _cmul(ar, ai, gr, gi, True)
            gr = pr + g_ref[pl.ds(row, N_SEG), :HALF_W]
            gi = pi + g_ref[pl.ds(row, N_SEG), HALF_W:]
            g_ref[pl.ds(row, N_SEG), :HALF_W] = gr
            g_ref[pl.ds(row, N_SEG), HALF_W:] = gi
            return gr, gi

        gr, gi = lax.fori_loop(0, ta, gstep, (st_ref[:, :HALF_W], st_ref[:, HALF_W:]))
        st_ref[:, :HALF_W] = gr
        st_ref[:, HALF_W:] = gi

        @pl.when(k == nk - 1)
        def _():
            gfin_ref[0] = st_ref[...]

        if full:
            gb = g_ref[...].astype(BF16)
            dut = jnp.dot(gb, bt_ref[0], preferred_element_type=F32)
            for al in range(ta):
                du_ref[pl.ds(a0 + al, N_SEG, stride=seg), :] = dut[al * N_SEG:(al + 1) * N_SEG, :]
            tn = (((0,), (0,)), ((), ()))
            dbp = lax.dot_general(up_ref[...].astype(BF16), gb, tn, preferred_element_type=F32)
            dcp = lax.dot_general(dyp_ref[...].astype(BF16), h_ref[...].astype(BF16), tn, preferred_element_type=F32)
            inner = (ta - 1) * N_SEG
            if descending:
                g_in_r, g_in_i = g_ref[0:inner, :HALF_W], g_ref[0:inner, HALF_W:]
                p_in_r, p_in_i = h_ref[N_SEG:tile, :HALF_W], h_ref[N_SEG:tile, HALF_W:]
                g_ed_r, g_ed_i = g_ref[inner:tile, :HALF_W], g_ref[inner:tile, HALF_W:]
            else:
                g_in_r, g_in_i = g_ref[N_SEG:tile, :HALF_W], g_ref[N_SEG:tile, HALF_W:]
                p_in_r, p_in_i = h_ref[0:inner, :HALF_W], h_ref[0:inner, HALF_W:]
                g_ed_r, g_ed_i = g_ref[0:N_SEG, :HALF_W], g_ref[0:N_SEG, HALF_W:]
            dar = g_ed_r * h0r + g_ed_i * h0i
            dai = g_ed_i * h0r - g_ed_r * h0i
            if ta > 1:
                dar = dar + jnp.sum((g_in_r * p_in_r + g_in_i * p_in_i).reshape(ta - 1, N_SEG, HALF_W), axis=0)
                dai = dai + jnp.sum((g_in_i * p_in_r - g_in_r * p_in_i).reshape(ta - 1, N_SEG, HALF_W), axis=0)

            @pl.when(k == 0)
            def _():
                db_ref[0] = dbp
                dc_ref[0] = dcp
                da_ref[0, :, :HALF_W] = dar
                da_ref[0, :, HALF_W:] = dai

            @pl.when(k > 0)
            def _():
                db_ref[0] += dbp
                dc_ref[0] += dcp
                da_ref[0, :, :HALF_W] += dar
                da_ref[0, :, HALF_W:] += dai

    u_spec = pl.BlockSpec((n_rows, SLAB_W), lambda s, k: (rb, s))
    m_spec = pl.BlockSpec((1, SLAB_W, STATE_W), lambda s, k: (s, 0, 0))
    mt_spec = pl.BlockSpec((1, STATE_W, SLAB_W), lambda s, k: (s, 0, 0))
    a_spec = pl.BlockSpec((1, 1, STATE_W), lambda s, k: (s, 0, 0))
    st_spec = pl.BlockSpec((1, N_SEG, STATE_W), lambda s, k: (s, 0, 0))
    st_shape = jax.ShapeDtypeStruct((N_SLAB, N_SEG, STATE_W), F32)
    if not full:
        scratch = [pltpu.VMEM((N_SEG, STATE_W), F32), pltpu.VMEM((tile, SLAB_W), F32), pltpu.VMEM((tile, STATE_W), F32)]
        return pl.pallas_call(
            body, grid=(N_SLAB, nk), in_specs=[u_spec, m_spec, a_spec, st_spec], out_specs=st_spec, out_shape=st_shape,
            scratch_shapes=scratch, compiler_params=_params("parallel", "arbitrary"), name=name)(dy, ct_mat, abar, g_in)
    kmap = (lambda s, k: (s, nk - 1 - k, 0, 0)) if g_desc else (lambda s, k: (s, k, 0, 0))
    in_specs = [u_spec, u_spec, m_spec, mt_spec, m_spec, a_spec, pl.BlockSpec((1, 1, N_SEG, STATE_W), kmap), st_spec]
    args = [u, dy, b_mat, bt_mat, ct_mat, abar, h_chunks, g_in]
    aliases = {}
    if du_alias is not None:
        in_specs.append(pl.BlockSpec(memory_space=pl.ANY))
        args.append(du_alias)
        aliases = {8: 1}
    acc_shape = jax.ShapeDtypeStruct((N_SLAB, SLAB_W, STATE_W), F32)
    out_specs = [st_spec, u_spec, m_spec, m_spec, st_spec]
    out_shape = [st_shape, jax.ShapeDtypeStruct((du_rows, D_MODEL), F32), acc_shape, acc_shape, st_shape]
    scratch = [pltpu.VMEM((N_SEG, STATE_W), F32), pltpu.VMEM((tile, SLAB_W), F32), pltpu.VMEM((tile, SLAB_W), F32),
               pltpu.VMEM((tile, STATE_W), F32), pltpu.VMEM((tile, STATE_W), F32)]
    return pl.pallas_call(
        body, grid=(N_SLAB, nk), in_specs=in_specs, out_specs=out_specs, out_shape=out_shape, scratch_shapes=scratch,
        input_output_aliases=aliases, compiler_params=_params("parallel", "arbitrary"), name=name)(*args)


ROPE_HALF = HEAD_DIM // 4
TABLE_W = 2 * HEAD_DIM
Q_SCALE = 1.0 / math.sqrt(HEAD_DIM)
HEADS_PER_BLOCK = 2 * KV_REP
Q_BLOCK_W = HEADS_PER_BLOCK * HEAD_DIM


def _rope_tables(n_lat, n_ctx):
    rows = n_lat // GRID_W
    row = jnp.repeat(jnp.arange(rows), GRID_W).astype(F32)
    col = jnp.tile(jnp.arange(GRID_W), rows).astype(F32)
    freqs = ROPE_THETA ** (-jnp.arange(ROPE_HALF, dtype=F32) / ROPE_HALF)
    ang_r = row[:, None] * freqs[None]
    ang_c = col[:, None] * freqs[None]
    cos = jnp.concatenate([jnp.cos(ang_r), jnp.cos(ang_r), jnp.cos(ang_c), jnp.cos(ang_c)] * 2, axis=1)
    sin = jnp.concatenate([jnp.sin(ang_r), jnp.sin(ang_r), jnp.sin(ang_c), jnp.sin(ang_c)] * 2, axis=1)
    cos = jnp.concatenate([cos, jnp.ones((n_ctx, TABLE_W), F32)], axis=0)
    sin = jnp.concatenate([sin, jnp.zeros((n_ctx, TABLE_W), F32)], axis=0)
    return cos, sin


def _rot_half(v):
    w = v.shape[1]
    ahead = pltpu.roll(v, w - ROPE_HALF, axis=1)
    behind = pltpu.roll(v, ROPE_HALF, axis=1)
    lane = lax.broadcasted_iota(jnp.int32, v.shape, 1)
    return jnp.where((lane % (2 * ROPE_HALF)) < ROPE_HALF, -ahead, behind)


def _head_mean(v, sel, selt):
    m = jnp.dot(v, sel, precision=lax.Precision.HIGHEST, preferred_element_type=F32) * (1.0 / HEAD_DIM)
    return jnp.dot(m, selt, precision=lax.Precision.HIGHEST, preferred_element_type=F32)


def _head_selectors(n_heads):
    sel = jnp.repeat(jnp.eye(n_heads, dtype=F32), HEAD_DIM, axis=0)
    return sel[None], sel.T[None]


def _head_norm(x, sel, selt):
    r = lax.rsqrt(_head_mean(x * x, sel, selt) + NORM_EPS)
    return x * r, r


def _qk_prep(proj, qn, kn, cos, sin, n, tr):
    qw, kw = _vec(jnp.tile(qn, N_Q_HEADS)), _vec(jnp.tile(kn, N_KV_HEADS))
    sq, sqt = _head_selectors(N_Q_HEADS)
    sk, skt = _head_selectors(N_KV_HEADS)

    def fn(qr, kvr, ct, st, qwv, kwv, s16, s16t, s4, s4t):
        outs = []
        for x, wv, sel, selt, scale in ((qr, qwv, s16, s16t, Q_SCALE), (kvr[:, :KV_W], kwv, s4, s4t, 1.0)):
            reps = x.shape[1] // TABLE_W
            cw, sw = jnp.tile(ct, (1, reps)), jnp.tile(st, (1, reps))
            xh, _ = _head_norm(x, sel, selt)
            nrm = xh * wv
            outs.append((nrm * cw + _rot_half(nrm) * sw) * scale)
        return outs[0], outs[1], kvr[:, KV_W:]

    return _rowwise("l1_qk_prep", fn, n, tr,
                    [(proj, 0, ATTN_W), (proj, 2 * ATTN_W // (2 * KV_W), 2 * KV_W), (cos, 0, TABLE_W), (sin, 0, TABLE_W)],
                    [qw, kw, sq, sqt, sk, skt], [(ATTN_W, BF16), (KV_W, BF16), (KV_W, BF16)], [])


def _qk_prep_bwd(proj, qn, kn, cos, sin, dq, dz, dk, dv, n, n_lat, tr):
    qw, kw = _vec(jnp.tile(qn, N_Q_HEADS)), _vec(jnp.tile(kn, N_KV_HEADS))
    sq, sqt = _head_selectors(N_Q_HEADS)
    sk, skt = _head_selectors(N_KV_HEADS)
    nlt = n_lat // tr

    def fn(flag, qr, kvr, ct, st, dqt, dzt, dkt, dvt, qwv, kwv, s16, s16t, s4, s4t):
        dxs, dws = [], []
        for x, dy, wv, sel, selt in ((qr, dqt * (flag * Q_SCALE), qwv, s16, s16t), (kvr[:, :KV_W], dkt, kwv, s4, s4t)):
            reps = x.shape[1] // TABLE_W
            cw, sw = jnp.tile(ct, (1, reps)), jnp.tile(st, (1, reps))
            xh, r = _head_norm(x, sel, selt)
            dn = dy * cw - _rot_half(dy * sw)
            dxh = dn * wv
            dxs.append(r * (dxh - xh * _head_mean(dxh * xh, sel, selt)))
            dws.append(dn * xh)
        return jnp.concatenate([dxs[0], dzt * flag, dxs[1], dvt], axis=1), dws[0], dws[1]

    dproj, dqw, dkw = _rowwise(
        "l1_qk_prep_bwd", fn, n, tr,
        [(proj, 0, ATTN_W), (proj, 2 * ATTN_W // (2 * KV_W), 2 * KV_W), (cos, 0, TABLE_W), (sin, 0, TABLE_W),
         (dq, 0, ATTN_W, "clamp", nlt), (dz, 0, ATTN_W, "clamp", nlt), (dk, 0, KV_W), (dv, 0, KV_W)],
        [qw, kw, sq, sqt, sk, skt], [(2 * ATTN_W + 2 * KV_W, F32)], [(1, ATTN_W), (1, KV_W)], n_lat=n_lat, want_flag=True)
    return dproj, dqw.reshape(N_Q_HEADS, HEAD_DIM).sum(0)[None], dkw.reshape(N_KV_HEADS, HEAD_DIM).sum(0)[None]


NT = (((1,), (1,)), ((), ()))
LANES = 128


def _attn_fwd(q, k, v, t, tq, tk):
    n = k.shape[0]
    nkc = n // tk

    def body(q_ref, k_ref, v_ref, o_ref, lse_ref, s_ref, m_ref, l_ref, acc_ref):
        for j in range(HEADS_PER_BLOCK):
            ql = slice(j * HEAD_DIM, (j + 1) * HEAD_DIM)
            kl = slice((j // KV_REP) * HEAD_DIM, (j // KV_REP + 1) * HEAD_DIM)
            qv = q_ref[:, ql]
            m_ref[...] = jnp.full(m_ref.shape, -jnp.inf, F32)

            def scores(kc, c):
                off = pl.multiple_of(kc * tk, tk)
                s = lax.dot_general(qv, k_ref[pl.ds(off, tk), kl], NT, preferred_element_type=F32)
                s_ref[:, pl.ds(off, tk)] = s
                m = m_ref[...]
                for cb in range(tk // LANES):
                    m = jnp.maximum(m, s[:, cb * LANES:(cb + 1) * LANES])
                m_ref[...] = m
                return c

            lax.fori_loop(0, nkc, scores, 0)
            m_row = jnp.max(m_ref[...], axis=-1, keepdims=True)
            l_ref[...] = jnp.zeros(l_ref.shape, F32)
            acc_ref[...] = jnp.zeros(acc_ref.shape, F32)

            def weigh(kc, c):
                off = pl.multiple_of(kc * tk, tk)
                p = jnp.exp(s_ref[:, pl.ds(off, tk)] - m_row)
                lsum = l_ref[...]
                for cb in range(tk // LANES):
                    lsum = lsum + p[:, cb * LANES:(cb + 1) * LANES]
                l_ref[...] = lsum
                acc_ref[...] += jnp.dot(p.astype(BF16), v_ref[pl.ds(off, tk), kl], preferred_element_type=F32)
                return c

            lax.fori_loop(0, nkc, weigh, 0)
            l_row = jnp.sum(l_ref[...], axis=-1, keepdims=True)
            o_ref[:, ql] = acc_ref[...] / l_row
            lse_ref[0, :, j:j + 1] = m_row + jnp.log(l_row)

    nb = ATTN_W // Q_BLOCK_W
    kspec = pl.BlockSpec((n, LANES), lambda b, i: (0, b))
    return pl.pallas_call(
        body, grid=(nb, t // tq),
        in_specs=[pl.BlockSpec((tq, Q_BLOCK_W), lambda b, i: (i, b)), kspec, kspec],
        out_specs=[pl.BlockSpec((tq, Q_BLOCK_W), lambda b, i: (i, b)),
                   pl.BlockSpec((1, tq, HEADS_PER_BLOCK), lambda b, i: (b, i, 0))],
        out_shape=[jax.ShapeDtypeStruct((t, ATTN_W), F32), jax.ShapeDtypeStruct((nb, t, HEADS_PER_BLOCK), F32)],
        scratch_shapes=[pltpu.VMEM((tq, n), F32), pltpu.VMEM((tq, LANES), F32), pltpu.VMEM((tq, LANES), F32),
                        pltpu.VMEM((tq, HEAD_DIM), F32)],
        compiler_params=_params("parallel", "parallel"), name="attn_fwd")(q, k, v)


def _attn_bwd_dq(q, k, v, do, o, lse, t, tq, tk):
    n = k.shape[0]
    nkc = n // tk

    def body(q_ref, k_ref, v_ref, do_ref, o_ref, lse_ref, dq_ref, dl_ref, acc_ref):
        for j in range(HEADS_PER_BLOCK):
            ql = slice(j * HEAD_DIM, (j + 1) * HEAD_DIM)
            kl = slice((j // KV_REP) * HEAD_DIM, (j // KV_REP + 1) * HEAD_DIM)
            qv, dov = q_ref[:, ql], do_ref[:, ql]
            dl_v = jnp.sum(dov.astype(F32) * o_ref[:, ql], axis=-1, keepdims=True)
            dl_ref[0, :, j:j + 1] = dl_v
            lse_v = lse_ref[0, :, j:j + 1]
            acc_ref[...] = jnp.zeros(acc_ref.shape, F32)

            def step(kc, c):
                off = pl.multiple_of(kc * tk, tk)
                kt = k_ref[pl.ds(off, tk), kl]
                s = lax.dot_general(qv, kt, NT, preferred_element_type=F32)
                p = jnp.exp(s - lse_v)
                dp = lax.dot_general(dov, v_ref[pl.ds(off, tk), kl], NT, preferred_element_type=F32)
                ds = (p * (dp - dl_v)).astype(BF16)
                acc_ref[...] += jnp.dot(ds, kt, preferred_element_type=F32)
                return c

            lax.fori_loop(0, nkc, step, 0)
            dq_ref[:, ql] = acc_ref[...]

    nb = ATTN_W // Q_BLOCK_W
    qspec = pl.BlockSpec((tq, Q_BLOCK_W), lambda b, i: (i, b))
    kspec = pl.BlockSpec((n, LANES), lambda b, i: (0, b))
    cspec = pl.BlockSpec((1, tq, HEADS_PER_BLOCK), lambda b, i: (b, i, 0))
    return pl.pallas_call(
        body, grid=(nb, t // tq), in_specs=[qspec, kspec, kspec, qspec, qspec, cspec], out_specs=[qspec, cspec],
        out_shape=[jax.ShapeDtypeStruct((t, ATTN_W), F32), jax.ShapeDtypeStruct((nb, t, HEADS_PER_BLOCK), F32)],
        scratch_shapes=[pltpu.VMEM((tq, HEAD_DIM), F32)],
        compiler_params=_params("parallel", "parallel"), name="attn_bwd_dq")(q, k, v, do, o, lse)


def _attn_bwd_dkv(q, k, v, do, lse_row, delta_row, t, tq, tk):
    n = k.shape[0]
    nqc = t // tq

    def body(q_ref, k_ref, v_ref, do_ref, lse_ref, dl_ref, dk_ref, dv_ref, dka_ref, dva_ref):
        for half in range(2):
            kl = slice(half * HEAD_DIM, (half + 1) * HEAD_DIM)
            kt, vt = k_ref[:, kl], v_ref[:, kl]
            dka_ref[...] = jnp.zeros(dka_ref.shape, F32)
            dva_ref[...] = jnp.zeros(dva_ref.shape, F32)
            for r in range(KV_REP):
                j = half * KV_REP + r
                ql = slice(j * HEAD_DIM, (j + 1) * HEAD_DIM)

                def step(qc, c):
                    off = pl.multiple_of(qc * tq, tq)
                    qt = q_ref[pl.ds(off, tq), ql]
                    dot = do_ref[pl.ds(off, tq), ql]
                    st = lax.dot_general(kt, qt, NT, preferred_element_type=F32)
                    pt = jnp.exp(st - lse_ref[j, :, pl.ds(off, tq)])
                    dva_ref[...] += jnp.dot(pt.astype(BF16), dot, preferred_element_type=F32)
                    dpt = lax.dot_general(vt, dot, NT, preferred_element_type=F32)
                    dst = (pt * (dpt - dl_ref[j, :, pl.ds(off, tq)])).astype(BF16)
                    dka_ref[...] += jnp.dot(dst, qt, preferred_element_type=F32)
                    return c

                lax.fori_loop(0, nqc, step, 0)
            dk_ref[:, kl] = dka_ref[...]
            dv_ref[:, kl] = dva_ref[...]

    nb = ATTN_W // Q_BLOCK_W
    qspec = pl.BlockSpec((t, Q_BLOCK_W), lambda b, i: (0, b))
    kspec = pl.BlockSpec((tk, LANES), lambda b, i: (i, b))
    rspec = pl.BlockSpec((HEADS_PER_BLOCK, 1, t), lambda b, i: (b, 0, 0))
    return pl.pallas_call(
        body, grid=(nb, n // tk), in_specs=[qspec, kspec, kspec, qspec, rspec, rspec], out_specs=[kspec, kspec],
        out_shape=[jax.ShapeDtypeStruct((n, KV_W), F32)] * 2,
        scratch_shapes=[pltpu.VMEM((tk, HEAD_DIM), F32), pltpu.VMEM((tk, HEAD_DIM), F32)],
        compiler_params=_params("parallel", "parallel"), name="attn_bwd_dkv")(q, k, v, do, lse_row, delta_row)


def _s5_system(p, n_lat, n_ctx):
    two_g = 2 * SSM_GROUPS
    a_re = p["ssm_a_re"].reshape(two_g, SSM_STATE)
    a_im = p["ssm_a_im"].reshape(two_g, SSM_STATE)
    log_dt = p["ssm_log_dt"].reshape(two_g, 1)
    b_re = p["ssm_b_re"].reshape(two_g, SSM_STATE, SSM_GROUP).transpose(2, 0, 1)
    b_im = p["ssm_b_im"].reshape(two_g, SSM_STATE, SSM_GROUP).transpose(2, 0, 1)
    raw = (a_re, a_im, log_dt, b_re, b_im)
    abr, abi, bbr, bbi, alr, ali, acr, aci = _s5_prep(*raw, n_lat // N_SEG, n_ctx // N_SEG)
    dirs = []
    for d in range(2):
        g = slice(d * SSM_GROUPS, (d + 1) * SSM_GROUPS)
        b_mat = _slab_in_matrix(bbr[:, g].transpose(1, 2, 0), bbi[:, g].transpose(1, 2, 0))
        c_mat = _slab_out_matrix(p["ssm_c_re"][0, d], p["ssm_c_im"][0, d])
        dirs.append(dict(
            b=b_mat.astype(BF16), bt=b_mat.transpose(0, 2, 1).astype(BF16),
            c=c_mat.astype(BF16), ct=c_mat.transpose(0, 2, 1).astype(BF16),
            abar=_slab_pair(abr[g], abi[g]),
            a_lat=_slab_pair(alr[g], ali[g])[:, 0], a_ctx=_slab_pair(acr[g], aci[g])[:, 0]))
    return raw, dirs


def _s5_forward(proj, dirs, n_lat, n_ctx):
    n = n_lat + n_ctx
    zero_st = jnp.zeros((N_SLAB, N_SEG, STATE_W), F32)
    zero_c = jnp.zeros((N_SLAB, STATE_W), F32)
    ys, saved = [], []
    for d, s in enumerate(dirs):
        desc = d == 1
        tag = f"s5f{d}"
        zc = _s5_scan(tag + "_ctx_ends", proj, n_ctx, n_lat, s["b"], s["c"], s["abar"], zero_st, desc, False)
        ent_c, h0 = _s5_carry(tag + "_ctx_carry", _seg_major(zc), s["a_ctx"], zero_c, desc, False)
        _, y, hch_c = _s5_scan(tag + "_ctx", proj, n_ctx, n_lat, s["b"], s["c"], s["abar"], _seg_major(ent_c), desc,
                               True, y_rows=n)
        zl = _s5_scan(tag + "_lat_ends", proj, n_lat, 0, s["b"], s["c"], s["abar"], zero_st, desc, False)
        ent_l, _ = _s5_carry(tag + "_lat_carry", _seg_major(zl), s["a_lat"], h0, desc, False)
        _, y, hch_l = _s5_scan(tag + "_lat", proj, n_lat, 0, s["b"], s["c"], s["abar"], _seg_major(ent_l), desc,
                               True, y_alias=y, y_rows=n)
        ys.append(y)
        saved.append((hch_l, hch_c))
    return ys, saved


def _s5_backward(proj, dy, dirs, saved, n_lat, n_ctx):
    n = n_lat + n_ctx
    zero_st = jnp.zeros((N_SLAB, N_SEG, STATE_W), F32)
    zero_c = jnp.zeros((N_SLAB, STATE_W), F32)
    out = []
    for d, s in enumerate(dirs):
        desc = d == 1
        tag = f"s5b{d}"
        hch_l, hch_c = saved[d]
        gl = _s5_scan_bwd(tag + "_lat_ends", proj, dy, n_lat, 0, s["b"], s["bt"], s["ct"], s["abar"], None, zero_st,
                          desc, False)
        ent_l, g0 = _s5_carry(tag + "_lat_carry", _seg_major(gl), s["a_lat"], zero_c, not desc, True)
        _, du, db_l, dc_l, da_l = _s5_scan_bwd(tag + "_lat", proj, dy, n_lat, 0, s["b"], s["bt"], s["ct"], s["abar"],
                                               hch_l, _seg_major(ent_l), desc, True, du_rows=n)
        gc = _s5_scan_bwd(tag + "_ctx_ends", proj, dy, n_ctx, n_lat, s["b"], s["bt"], s["ct"], s["abar"], None, zero_st,
                          desc, False)
        ent_c, _ = _s5_carry(tag + "_ctx_carry", _seg_major(gc), s["a_ctx"], g0, not desc, True)
        _, du, db_c, dc_c, da_c = _s5_scan_bwd(tag + "_ctx", proj, dy, n_ctx, n_lat, s["b"], s["bt"], s["ct"], s["abar"],
                                               hch_c, _seg_major(ent_c), desc, True, du_alias=du, du_rows=n)
        out.append((du, db_l + db_c, dc_l + dc_c, da_l + da_c))
    return out


def _s5_param_grads(raw, bwd):
    dabr, dabi, dbbr, dbbi, dcr, dci = [], [], [], [], [], []
    for _, db, dc, da in bwd:
        da = jnp.sum(da, axis=1)
        dabr.append(da[:, :HALF_W].reshape(SSM_GROUPS, SSM_STATE))
        dabi.append(da[:, HALF_W:].reshape(SSM_GROUPS, SSM_STATE))
        dbd = _slab_diag(db)
        dbbr.append(dbd[0].transpose(1, 0, 2))
        dbbi.append(dbd[1].transpose(1, 0, 2))
        dcd = _slab_diag(dc)
        dcr.append(dcd[0])
        dci.append(-dcd[1])
    cat = lambda xs, ax: jnp.concatenate(xs, axis=ax)
    dar, dai, dld, dbr, dbi = _s5_prep_bwd(*raw, cat(dabr, 0), cat(dabi, 0), cat(dbbr, 1), cat(dbbi, 1))
    shp = (1, 2, SSM_GROUPS, SSM_STATE)
    b_shape = (1, 2, SSM_GROUPS, SSM_STATE, SSM_GROUP)
    return dict(
        ssm_a_re=dar.reshape(shp), ssm_a_im=dai.reshape(shp), ssm_log_dt=dld.reshape(1, 2, SSM_GROUPS),
        ssm_b_re=dbr.transpose(1, 2, 0).reshape(b_shape), ssm_b_im=dbi.transpose(1, 2, 0).reshape(b_shape),
        ssm_c_re=jnp.stack(dcr)[None], ssm_c_im=jnp.stack(dci)[None])


def _example_step(x, ctx, target, mods, w, p):
    t, c = x.shape[0], ctx.shape[0]
    n = t + c
    assert t % c == 0 and c % (8 * N_SEG) == 0 and t % GRID_W == 0
    tr = _largest_tile(c, 256, 8)
    xall = jnp.concatenate([x, ctx], axis=0)
    g0, g1 = _vec(p["norm_g"][0]), _vec(p["norm_g"][1])
    (shift0, scale0, gate0), (shift1, scale1, gate1) = [tuple(_vec(v) for v in m) for m in mods]

    h0 = _norm_mod_fwd("l0_norm", xall, g0, scale0, shift0, n, tr, t)
    proj0 = _mm("l0_in", h0, w["ssm_w_in"], "nn")
    raw, dirs = _s5_system(p, t, c)
    (y_f, y_r), saved = _s5_forward(proj0, dirs, t, c)
    d_skip = _vec(p["ssm_d"][0])

    def post_a(u, yf, yr, dv):
        y = u * dv + yf + yr
        return y, _gelu(y)

    y0, yg = _rowwise("l0_gelu", post_a, n, tr, [(proj0, 0, D_MODEL), (y_f, 0, D_MODEL), (y_r, 0, D_MODEL)], [d_skip],
                      [(D_MODEL, F32), (D_MODEL, F32)], [])
    tg = _mm("l0_glu", yg, w["ssm_w_glu"], "nn")
    b_glu = _vec(p["ssm_b_glu"][0])

    def post_b(ygt, tt, zt, bv):
        return ygt * _sigmoid(tt + bv) * _silu(zt)

    gz0 = _rowwise("l0_gate", post_b, n, tr, [(yg, 0, D_MODEL), (tg, 0, D_MODEL), (proj0, 1, D_MODEL)], [b_glu],
                   [(D_MODEL, F32)], [])[0]
    out0 = _mm("l0_out", gz0, w["ssm_w_out"], "nn")
    x1 = _rowwise("l0_res", lambda xt, ot, gv: xt + gv * ot, n, tr, [(xall, 0, D_MODEL), (out0, 0, D_MODEL)], [gate0],
                  [(D_MODEL, F32)], [], n_lat=t)[0]

    h1 = _norm_mod_fwd("l1_norm", x1, g1, scale1, shift1, n, tr, t)
    proj1 = _mm("l1_in", h1, w["attn_w_in"], "nn")
    cos, sin = _rope_tables(t, c)
    qn, kn = p["attn_q_norm"][0], p["attn_k_norm"][0]
    q_h, k_h, v_h = _qk_prep(proj1, qn, kn, cos, sin, n, tr)
    tq = _largest_tile(t, 1024, LANES)
    tk = _largest_tile(n, 1024, LANES)
    o, lse = _attn_fwd(q_h, k_h, v_h, t, _largest_tile(t, 512, LANES), tk)
    gz1 =_rowwise("l1_gate", lambda ot, zt: ot * _silu(zt), t, tr, [(o, 0, D_MODEL), (proj1, 1, D_MODEL)], [],
                   [(D_MODEL, F32)], [])[0]
    out1 = _mm("l1_out", gz1, w["attn_w_out"], "nn")

    gf = _vec(p["final_norm_g"])

    def head(x1t, o1t, tgt, g1v, gfv):
        x2 = x1t + g1v * o1t
        xh, r = _rms_hat(x2)
        e = xh * gfv - tgt
        dyf = e * (1.0 / D_MODEL)
        dx2 = _rms_bwd(xh, r, dyf * gfv)
        return dx2, g1v * dx2, dyf * xh, dx2 * o1t, jnp.sum(e * e, axis=1, keepdims=True)

    gate1_lat = gate1[0:1]
    dx2, dout1, d_gf, d_gate1, sq = _rowwise(
        "head", head, t, tr, [(x1, 0, D_MODEL), (out1, 0, D_MODEL), (target, 0, D_MODEL)], [gate1_lat, gf],
        [(D_MODEL, F32), (D_MODEL, F32)], [(1, D_MODEL), (1, D_MODEL), (1, 1)])

    d_w_attn_out = _mm("l1_out_dw", gz1, dout1, "tn")
    dgz1 = _mm("l1_out_dx", dout1, w["attn_w_out"], "nt")

    def gate1_bwd(dgt, ot, zt):
        return dgt * _silu(zt), dgt * ot * _silu_grad(zt)

    do, dz1 = _rowwise("l1_gate_bwd", gate1_bwd, t, tr, [(dgz1, 0, D_MODEL), (o, 0, D_MODEL), (proj1, 1, D_MODEL)], [],
                       [(D_MODEL, BF16), (D_MODEL, F32)], [])
    dq_s, delta = _attn_bwd_dq(q_h, k_h, v_h, do, o, lse, t, tq, tk)
    by_head = lambda a: a.transpose(0, 2, 1).reshape(N_Q_HEADS, 1, t)
    dk, dv = _attn_bwd_dkv(q_h, k_h, v_h, do, by_head(lse), by_head(delta), t, tq, tk)
    dproj1, d_qn, d_kn = _qk_prep_bwd(proj1, qn, kn, cos, sin, dq_s, dz1, dk, dv, n, t, tr)
    d_w_attn_in = _mm("l1_in_dw", h1, dproj1, "tn")
    dh1 = _mm("l1_in_dx", dproj1, w["attn_w_in"], "nt")
    dx1, d_g1, d_scale1, d_shift1 = _norm_mod_bwd("l1_norm_bwd", x1, g1, scale1, dh1, dx2, n, tr, t)

    dout0, d_gate0 = _rowwise("l0_res_bwd", lambda dxt, ot, gv: (gv * dxt, dxt * ot), n, tr,
                              [(dx1, 0, D_MODEL), (out0, 0, D_MODEL)], [gate0], [(D_MODEL, F32)], [(2, D_MODEL)], n_lat=t)
    d_w_out = _mm("l0_out_dw", gz0, dout0, "tn")
    dgz0 = _mm("l0_out_dx", dout0, w["ssm_w_out"], "nt")

    def post_b_bwd(dgt, ygt, tt, zt, bv):
        s = _sigmoid(tt + bv)
        dy2 = dgt * _silu(zt)
        dt = dy2 * ygt * s * (1.0 - s)
        return dgt * (ygt * s) * _silu_grad(zt), dt, dy2 * s, dt

    dz0, dtg, dyg_a, d_b_glu = _rowwise(
        "l0_gate_bwd", post_b_bwd, n, tr, [(dgz0, 0, D_MODEL), (yg, 0, D_MODEL), (tg, 0, D_MODEL), (proj0, 1, D_MODEL)],
        [b_glu], [(D_MODEL, F32), (D_MODEL, F32), (D_MODEL, F32)], [(1, D_MODEL)])
    d_w_glu = _mm("l0_glu_dw", yg, dtg, "tn")
    dyg_b = _mm("l0_glu_dx", dtg, w["ssm_w_glu"], "nt")

    def post_a_bwd(da, db, yt, ut, dv):
        dy = (da + db) * _gelu_grad(yt)
        return dy, dy * dv, dy * ut

    dy0, du_skip, d_d = _rowwise("l0_gelu_bwd", post_a_bwd, n, tr,
                                 [(dyg_a, 0, D_MODEL), (dyg_b, 0, D_MODEL), (y0, 0, D_MODEL), (proj0, 0, D_MODEL)], [d_skip],
                                 [(D_MODEL, F32), (D_MODEL, F32)], [(1, D_MODEL)])
    s5_bwd = _s5_backward(proj0, dy0, dirs, saved, t, c)
    dproj0 = _rowwise("l0_in_grad", lambda a, b, cc, dz: jnp.concatenate([a + b + cc, dz], axis=1), n, tr,
                      [(du_skip, 0, D_MODEL), (s5_bwd[0][0], 0, D_MODEL), (s5_bwd[1][0], 0, D_MODEL), (dz0, 0, D_MODEL)], [],
                      [(2 * D_MODEL, F32)], [])[0]
    d_w_in = _mm("l0_in_dw", h0, dproj0, "tn")
    dh0 = _mm("l0_in_dx", dproj0, w["ssm_w_in"], "nt")
    dx0, d_g0, d_scale0, d_shift0 = _norm_mod_bwd("l0_norm_bwd", xall, g0, scale0, dh0, dx1, n, tr, t)

    big = dict(ssm_w_in=d_w_in, ssm_w_glu=d_w_glu, ssm_w_out=d_w_out, attn_w_in=d_w_attn_in, attn_w_out=d_w_attn_out)
    small = dict(
        norm_g=jnp.concatenate([d_g0[0], d_g1[0]], axis=0), ssm_d=d_d[0], ssm_b_glu=d_b_glu[0],
        attn_q_norm=d_qn, attn_k_norm=d_kn, final_norm_g=d_gf[0, 0], **_s5_param_grads(raw, s5_bwd))
    zero_v = jnp.zeros((D_MODEL,), F32)
    d_mod_lat = jnp.stack([jnp.concatenate([d_shift0[0, 0], d_scale0[0, 0], d_gate0[0, 0]]),
                           jnp.concatenate([d_shift1[0, 0], d_scale1[0, 0], d_gate1[0, 0]])])
    d_mod_ctx = jnp.stack([jnp.concatenate([d_shift0[1, 0], d_scale0[1, 0], d_gate0[1, 0]]),
                           jnp.concatenate([d_shift1[1, 0], d_scale1[1, 0], zero_v])])
    return sq[0, 0, 0], dx0[:t], big, small, d_mod_lat, d_mod_ctx


def _adamw(name, w, g, m, v):
    rows, cols = w.shape
    tr = _largest_tile(rows, 256, 8)
    c1 = 1.0 / (1.0 - ADAM_B1 ** ADAM_STEP)
    c2 = 1.0 / (1.0 - ADAM_B2 ** ADAM_STEP)

    def fn(wt, gt, mt, vt):
        mn = ADAM_B1 * mt + (1.0 - ADAM_B1) * gt
        vn = ADAM_B2 * vt + (1.0 - ADAM_B2) * (gt * gt)
        delta = -ADAM_LR * ((mn * c1) / (jnp.sqrt(vn * c2) + ADAM_EPS) + ADAM_WD * wt)
        return delta, mn, vn

    return _rowwise(name, fn, rows, tr, [(a, 0, cols) for a in (w, g, m, v)], [], [(cols, F32)] * 3, [])


BIG = ("ssm_w_in", "ssm_w_glu", "ssm_w_out", "attn_w_in", "attn_w_out")
COL_SHARDED = ("ssm_w_in", "attn_w_in")
WEIGHTS = ("c_ctx", "w_mod", "b_mod", "norm_g", "ssm_w_in", "ssm_a_re", "ssm_a_im", "ssm_log_dt", "ssm_b_re", "ssm_b_im",
           "ssm_c_re", "ssm_c_im", "ssm_d", "ssm_w_glu", "ssm_b_glu", "ssm_w_out", "attn_w_in", "attn_q_norm",
           "attn_k_norm", "attn_w_out", "final_norm_g")
SMALL = tuple(k for k in WEIGHTS if k not in BIG and k != "w_mod")
PACK_W = 1024


def _attn_in_perm(x, inverse):
    a, kv = ATTN_W, 2 * KV_W
    if inverse:
        return jnp.concatenate([x[..., :a], x[..., 2 * a:], x[..., a:2 * a]], axis=-1)
    return jnp.concatenate([x[..., :a], x[..., a + kv:], x[..., a:a + kv]], axis=-1)


def _pack(arrays, dtype, row_unit):
    flat = jnp.concatenate([a.reshape(-1).astype(dtype) for a in arrays])
    rows = -(-flat.shape[0] // PACK_W)
    rows = -(-rows // row_unit) * row_unit
    flat = jnp.concatenate([flat, jnp.zeros((rows * PACK_W - flat.shape[0],), dtype)])
    return flat.reshape(rows, PACK_W)


def _unpack(buf, shapes):
    lead = buf.shape[:-2]
    flat = buf.reshape(lead + (-1,))
    out, off = [], 0
    for shp in shapes:
        size = math.prod(shp)
        out.append(flat[..., off:off + size].reshape(lead + tuple(shp)))
        off += size
    return out


def _half_shape(name, shard_shape):
    r, ccols = shard_shape
    return (r // 2, ccols)


def kernel(x, c, ctx, c_ctx, w_mod, b_mod, norm_g, ssm_w_in, ssm_a_re, ssm_a_im, ssm_log_dt, ssm_b_re, ssm_b_im, ssm_c_re, ssm_c_im, ssm_d, ssm_w_glu, ssm_b_glu, ssm_w_out, attn_w_in, attn_q_norm, attn_k_norm, attn_w_out, final_norm_g, loss_target, m_c_ctx, m_w_mod, m_b_mod, m_norm_g, m_ssm_w_in, m_ssm_a_re, m_ssm_a_im, m_ssm_log_dt, m_ssm_b_re, m_ssm_b_im, m_ssm_c_re, m_ssm_c_im, m_ssm_d, m_ssm_w_glu, m_ssm_b_glu, m_ssm_w_out, m_attn_w_in, m_attn_q_norm, m_attn_k_norm, m_attn_w_out, m_final_norm_g, v_c_ctx, v_w_mod, v_b_mod, v_norm_g, v_ssm_w_in, v_ssm_a_re, v_ssm_a_im, v_ssm_log_dt, v_ssm_b_re, v_ssm_b_im, v_ssm_c_re, v_ssm_c_im, v_ssm_d, v_ssm_w_glu, v_ssm_b_glu, v_ssm_w_out, v_attn_w_in, v_attn_q_norm, v_attn_k_norm, v_attn_w_out, v_final_norm_g):
    args = dict(locals())
    wts = {k: args[k] for k in WEIGHTS}
    mom_m = {k: args["m_" + k] for k in WEIGHTS}
    mom_v = {k: args["v_" + k] for k in WEIGHTS}
    mx, my, mc = lax.axis_index("x"), lax.axis_index("y"), lax.axis_index("c")
    chip = 2 * mx + my
    me = 2 * chip + mc

    halves = []
    for k in BIG:
        sh = wts[k][0]
        hr = sh.shape[0] // 2
        halves.append(lax.dynamic_slice_in_dim(sh, mc * hr, hr, axis=0))
    gathered = _exchange("gather_weights", _pack(halves, BF16, 16), True)
    parts = _unpack(gathered, [h.shape for h in halves])
    w_full = {}
    for k, pc in zip(BIG, parts):
        hr, cols = pc.shape[1:]
        pc = pc.reshape(N_CHIP, 2, hr, cols)
        if k in COL_SHARDED:
            w_full[k] = pc.transpose(1, 2, 0, 3).reshape(2 * hr, N_CHIP * cols)
        else:
            w_full[k] = pc.reshape(N_CHIP * 2 * hr, cols)
    w_full["attn_w_in"] = _attn_in_perm(w_full["attn_w_in"], False)

    c_blk = jnp.concatenate([c, jnp.zeros((7, D_MODEL), F32)], axis=0)
    c_all = _exchange("gather_c", c_blk, True)[:, 0]
    cond = jnp.concatenate([c_all, c_ctx[None], jnp.zeros((7, D_MODEL), F32)], axis=0)
    s_cond, ds_cond = _rowwise("cond_silu", lambda t: (_silu(t), _silu_grad(t)), 16, 16, [(cond, 0, D_MODEL)], [],
                               [(D_MODEL, F32), (D_MODEL, F32)], [])
    w_mod_b = w_mod.astype(BF16)
    mcols = w_mod.shape[2]
    mod_part = jnp.stack([_mm(f"mod{i}", s_cond, w_mod_b[i], "nn") for i in range(2)])
    mod_g = _exchange("gather_mod", mod_part.reshape(32, mcols), True)
    mod_all = mod_g.reshape(N_CHIP, 2, 2, 16, mcols)[:, 0]
    mod_all = mod_all.transpose(1, 2, 0, 3).reshape(2, 16, N_CHIP * mcols) + b_mod[:, None, :]
    mods = []
    for i in range(2):
        lat = lax.dynamic_slice_in_dim(mod_all[i], me, 1, axis=0)[0]
        both = jnp.stack([lat, mod_all[i, 8]])
        mods.append((both[:, :D_MODEL], both[:, D_MODEL:2 * D_MODEL], both[:, 2 * D_MODEL:]))

    small_p = {k: wts[k] for k in SMALL if k != "c_ctx" and k != "b_mod"}
    sq, grad_x, big_g, small_g, d_mod_lat, d_mod_ctx = _example_step(x[0], ctx[0], loss_target[0], mods, w_full, small_p)
    loss = lax.psum(0.5 / D_MODEL * sq, ("x", "y", "c"))
    big_g["attn_w_in"] = _attn_in_perm(big_g["attn_w_in"], True)

    small_names = [k for k in SMALL if k not in ("c_ctx", "b_mod")]
    small_list = [small_g[k] for k in small_names] + [d_mod_lat, d_mod_ctx]
    small_shapes = [wts[k].shape for k in small_names] + [d_mod_lat.shape, d_mod_ctx.shape]
    sg = _exchange("gather_small", _pack(small_list, F32, 8), True)
    sg_sum = _sum_slots("sum_small", sg)
    summed = _unpack(sg_sum, small_shapes)
    grads = dict(zip(small_names, summed[:-2]))
    d_mod_lat_sum, d_mod_ctx_sum = summed[-2], summed[-1]
    grads["b_mod"] = d_mod_lat_sum + d_mod_ctx_sum
    d_mod_lat_all = _unpack(sg, small_shapes)[-2]

    g_w_mod, ds_cc = [], []
    for i in range(2):
        rows9 = jnp.concatenate([d_mod_lat_all[:, i], d_mod_ctx_sum[i][None], jnp.zeros((7, 3 * D_MODEL), F32)], axis=0)
        mine = lax.dynamic_slice_in_dim(rows9, chip * mcols, mcols, axis=1)
        g_w_mod.append(_mm(f"mod{i}_dw", s_cond, mine, "tn"))
        ds_cc.append(_mm(f"mod{i}_dx", mine, w_mod_b[i], "nt")[8])
    grads["w_mod"] = jnp.stack(g_w_mod)
    part = (ds_cc[0] + ds_cc[1]) * jnp.where(mc == 0, 1.0, 0.0)
    part_blk = jnp.concatenate([part[None], jnp.zeros((7, D_MODEL), F32)], axis=0)
    ds_all = _sum_slots("sum_c_ctx", _exchange("gather_c_ctx", part_blk, True))
    grads["c_ctx"] = ds_all[0] * ds_cond[8]

    blocks = []
    for k in BIG:
        g = big_g[k]
        rows, cols = g.shape
        if k in COL_SHARDED:
            blocks.append(g.reshape(2, rows // 2, N_CHIP, cols // N_CHIP).transpose(2, 0, 1, 3).reshape(N_DEV, -1))
        else:
            blocks.append(g.reshape(N_DEV, -1))
    sendbuf = jnp.concatenate(blocks, axis=1).astype(BF16)
    sendbuf = sendbuf.reshape(N_DEV, -1, PACK_W)
    recv = _exchange("scatter_big", sendbuf, False)
    mine = _sum_slots("sum_big", recv)
    both = _exchange("swap_halves", mine, True, sibling_only=True)
    half_shapes = [(wts[k].shape[1] // 2, wts[k].shape[2]) for k in BIG]
    for k, pc in zip(BIG, _unpack(both, half_shapes)):
        grads[k] = pc.reshape(wts[k].shape)

    delta, new_m, new_v = {}, {}, {}
    for k in BIG + ("w_mod",):
        shp = wts[k].shape
        two_d = (-1, shp[-1])
        res = _adamw("adamw_" + k, *[a.reshape(two_d) for a in (wts[k], grads[k], mom_m[k], mom_v[k])])
        delta[k], new_m[k], new_v[k] = [r.reshape(shp) for r in res]
    shapes = [wts[k].shape for k in SMALL]
    packed = [_pack([d[k] for k in SMALL], F32, 8) for d in (wts, grads, mom_m, mom_v)]
    res = _adamw("adamw_small", *packed)
    for dst, buf in zip((delta, new_m, new_v), res):
        for k, a in zip(SMALL, _unpack(buf, shapes)):
            dst[k] = a
    grads = {k: grads[k].reshape(wts[k].shape) for k in WEIGHTS}
    return (loss, grad_x[None], *[grads[k] for k in WEIGHTS], *[delta[k] for k in WEIGHTS],
            *[new_m[k] for k in WEIGHTS], *[new_v[k] for k in WEIGHTS])
```

```python
import functools
import math

import jax
import jax.numpy as jnp
from jax import lax
from jax.experimental import pallas as pl
from jax.experimental.pallas import tpu as pltpu

F32 = jnp.float32
BF16 = jnp.bfloat16

D_MODEL = 1024
NORM_EPS = 1e-6
SSM_GROUPS = 64
SSM_GROUP = 16
SSM_STATE = 64
SLAB_W = 128
N_SLAB = D_MODEL // SLAB_W
SLAB_GROUPS = SLAB_W // SSM_GROUP
HALF_W = SLAB_GROUPS * SSM_STATE
STATE_W = 2 * HALF_W
N_SEG = 8
HEAD_DIM = 64
N_Q_HEADS = 16
N_KV_HEADS = 4
KV_REP = N_Q_HEADS // N_KV_HEADS
ATTN_W = N_Q_HEADS * HEAD_DIM
KV_W = N_KV_HEADS * HEAD_DIM
GRID_W = 64
ROPE_THETA = 10000.0
N_DEV = 8
N_CHIP = 4
VMEM_LIMIT_BYTES = 56 * 1024 * 1024

ADAM_LR = 0.001
ADAM_B1 = 0.9
ADAM_B2 = 0.999
ADAM_EPS = 1e-08
ADAM_WD = 0.01
ADAM_STEP = 10


def _params(*sem):
    return pltpu.CompilerParams(dimension_semantics=sem, vmem_limit_bytes=VMEM_LIMIT_BYTES)


def _largest_tile(n, cap, unit):
    if n <= cap:
        return n
    t = (cap // unit) * unit
    while t >= unit:
        if n % t == 0:
            return t
        t -= unit
    raise ValueError(f"no tile for {n} (cap {cap}, unit {unit})")


def _rowwise(name, fn, n_rows, tr, row_ins, vec_ins, row_outs, red_outs, n_lat=None, want_flag=False):
    nt = n_rows // tr
    assert nt * tr == n_rows
    nlt = nt if n_lat is None else n_lat // tr

    def sel(i):
        return jnp.where(i >= nlt, 1, 0)

    arrays, in_specs = [], []
    for spec in row_ins:
        arr, cb, w = spec[:3]
        kind = spec[3] if len(spec) > 3 else None
        m = spec[4] if len(spec) > 4 else None
        if kind == "mod":
            imap = functools.partial(lambda i, cb, m: (i % m, cb), cb=cb, m=m)
        elif kind == "clamp":
            imap = functools.partial(lambda i, cb, m: (jnp.minimum(i, m - 1), cb), cb=cb, m=m)
        else:
            imap = functools.partial(lambda i, cb: (i, cb), cb=cb)
        arrays.append(arr)
        in_specs.append(pl.BlockSpec((tr, w), imap))
    for v in vec_ins:
        s, a, w = v.shape
        imap = (lambda i: (sel(i), 0, 0)) if s == 2 else (lambda i: (0, 0, 0))
        arrays.append(v)
        in_specs.append(pl.BlockSpec((1, a, w), imap))
    out_shapes, out_specs = [], []
    for w, dt in row_outs:
        out_shapes.append(jax.ShapeDtypeStruct((n_rows, w), dt))
        out_specs.append(pl.BlockSpec((tr, w), lambda i: (i, 0)))
    for s, w in red_outs:
        out_shapes.append(jax.ShapeDtypeStruct((s, 1, w), F32))
        imap = (lambda i: (sel(i), 0, 0)) if s == 2 else (lambda i: (0, 0, 0))
        out_specs.append(pl.BlockSpec((1, 1, w), imap))
    n_ri, n_vi, n_ro, n_rd = len(row_ins), len(vec_ins), len(row_outs), len(red_outs)

    def body(*refs):
        i = pl.program_id(0)
        rows = [r[...] for r in refs[:n_ri]]
        vecs = [r[0] for r in refs[n_ri:n_ri + n_vi]]
        outs = refs[n_ri + n_vi:]
        lead = [jnp.where(i < nlt, 1.0, 0.0).astype(F32)] if want_flag else []
        res = fn(*lead, *rows, *vecs)
        if not isinstance(res, (tuple, list)):
            res = (res,)
        assert len(res) == n_ro + n_rd
        for k in range(n_ro):
            outs[k][...] = res[k].astype(outs[k].dtype)
        for k in range(n_rd):
            part = jnp.sum(res[n_ro + k].astype(F32), axis=0, keepdims=True)
            first = i == 0
            if red_outs[k][0] == 2:
                first = jnp.logical_or(first, i == nlt)
            o = outs[n_ro + k]

            @pl.when(first)
            def _():
                o[0] = part

            @pl.when(jnp.logical_not(first))
            def _():
                o[0] = o[0] + part

    res = pl.pallas_call(
        body, grid=(nt,), in_specs=in_specs, out_specs=out_specs, out_shape=out_shapes,
        compiler_params=_params("arbitrary"), name=name)(*arrays)
    return res


def _vec(v):
    v = v.astype(F32)
    if v.ndim == 1:
        v = v[None]
    return v[:, None, :]


def _mm(name, a, b, mode, out_dtype=F32):
    if mode in ("nn", "nt"):
        m, k = a.shape
        n = b.shape[1] if mode == "nn" else b.shape[0]
        tm = _largest_tile(m, 512, 8)
        tn = _largest_tile(n, 1024, 128)
        contract = (((1,), (0,)), ((), ())) if mode == "nn" else (((1,), (1,)), ((), ()))

        def body(a_ref, b_ref, o_ref):
            o_ref[...] = lax.dot_general(a_ref[...].astype(BF16), b_ref[...].astype(BF16), contract,
                                         preferred_element_type=F32).astype(o_ref.dtype)

        b_spec = pl.BlockSpec((k, tn), lambda i, j: (0, j)) if mode == "nn" else pl.BlockSpec((tn, k), lambda i, j: (j, 0))
        return pl.pallas_call(
            body, grid=(m // tm, n // tn),
            in_specs=[pl.BlockSpec((tm, k), lambda i, j: (i, 0)), b_spec],
            out_specs=pl.BlockSpec((tm, tn), lambda i, j: (i, j)),
            out_shape=jax.ShapeDtypeStruct((m, n), out_dtype),
            compiler_params=_params("parallel", "arbitrary"), name=name)(a, b)
    assert mode == "tn"
    r, k1 = a.shape
    k2 = b.shape[1]
    tr = _largest_tile(r, 512, 8)
    t2 = _largest_tile(k2, 1024, 128)
    nr = r // tr

    def body(a_ref, b_ref, o_ref):
        part = lax.dot_general(a_ref[...].astype(BF16), b_ref[...].astype(BF16), (((0,), (0,)), ((), ())),
                               preferred_element_type=F32)
        i = pl.program_id(1)

        @pl.when(i == 0)
        def _():
            o_ref[...] = part

        @pl.when(i > 0)
        def _():
            o_ref[...] += part

    return pl.pallas_call(
        body, grid=(k2 // t2, nr),
        in_specs=[pl.BlockSpec((tr, k1), lambda j, i: (i, 0)), pl.BlockSpec((tr, t2), lambda j, i: (i, j))],
        out_specs=pl.BlockSpec((k1, t2), lambda j, i: (0, j)),
        out_shape=jax.ShapeDtypeStruct((k1, k2), F32),
        compiler_params=_params("parallel", "arbitrary"), name=name)(a, b)


def _exchange(name, x, bcast, sibling_only=False):
    rels = [1] if sibling_only else list(range(1, N_DEV))
    n_slot = 2 if sibling_only else N_DEV
    blk = x.shape if bcast else x.shape[1:]

    def body(x_ref, o_ref, send_sems, recv_sems, local_sem):
        mx, my, mc = lax.axis_index("x"), lax.axis_index("y"), lax.axis_index("c")
        me = mc if sibling_only else 4 * mx + 2 * my + mc
        me_dev = 4 * mx + 2 * my + mc
        mine = pltpu.make_async_copy(x_ref if bcast else x_ref.at[me_dev], o_ref.at[me], local_sem)
        mine.start()
        copies = []
        for k, r in enumerate(rels):
            px = 1 - mx if (r >> 2) & 1 else mx
            py = 1 - my if (r >> 1) & 1 else my
            pc = 1 - mc if r & 1 else mc
            src = x_ref if bcast else x_ref.at[4 * px + 2 * py + pc]
            cp = pltpu.make_async_remote_copy(
                src_ref=src, dst_ref=o_ref.at[me], send_sem=send_sems.at[k], recv_sem=recv_sems.at[k],
                device_id=(px, py, pc), device_id_type=pl.DeviceIdType.MESH)
            cp.start()
            copies.append(cp)
        for cp in copies:
            cp.wait()
        mine.wait()

    return pl.pallas_call(
        body, out_shape=jax.ShapeDtypeStruct((n_slot,) + tuple(blk), x.dtype),
        in_specs=[pl.BlockSpec(memory_space=pl.ANY)], out_specs=pl.BlockSpec(memory_space=pl.ANY),
        scratch_shapes=[pltpu.SemaphoreType.DMA((len(rels),)), pltpu.SemaphoreType.DMA((len(rels),)),
                        pltpu.SemaphoreType.DMA],
        name=name)(x)


def _sum_slots(name, x):
    s, r, w = x.shape
    tr = _largest_tile(r, 256, 8)

    def body(x_ref, o_ref):
        acc = x_ref[0].astype(F32)
        for j in range(1, s):
            acc = acc + x_ref[j].astype(F32)
        o_ref[...] = acc

    return pl.pallas_call(
        body, grid=(r // tr,), in_specs=[pl.BlockSpec((s, tr, w), lambda i: (0, i, 0))],
        out_specs=pl.BlockSpec((tr, w), lambda i: (i, 0)), out_shape=jax.ShapeDtypeStruct((r, w), F32),
        compiler_params=_params("parallel"), name=name)(x)


def _sigmoid(x):
    return 1.0 / (1.0 + jnp.exp(-x))


def _silu(x):
    return x * _sigmoid(x)


def _silu_grad(x):
    s = _sigmoid(x)
    return s * (1.0 + x * (1.0 - s))


_INV_SQRT2 = 1.0 / math.sqrt(2.0)
_INV_SQRT2PI = 1.0 / math.sqrt(2.0 * math.pi)


def _gelu(x):
    return 0.5 * x * (1.0 + lax.erf(x * _INV_SQRT2))


def _gelu_grad(x):
    return 0.5 * (1.0 + lax.erf(x * _INV_SQRT2)) + x * jnp.exp(-0.5 * x * x) * _INV_SQRT2PI


def _rms_hat(x):
    r = lax.rsqrt(jnp.mean(x * x, axis=-1, keepdims=True) + NORM_EPS)
    return x * r, r


def _rms_bwd(xh, r, dxh):
    return r * (dxh - xh * jnp.mean(dxh * xh, axis=-1, keepdims=True))


def _norm_mod_fwd(name, x, g, scale, shift, n_rows, tr, n_lat):
    def fn(xt, gv, sc, sh):
        xh, _ = _rms_hat(xt)
        return (xh * gv) * (1.0 + sc) + sh

    return _rowwise(name, fn, n_rows, tr, [(x, 0, D_MODEL)], [g, scale, shift], [(D_MODEL, F32)], [], n_lat=n_lat)[0]


def _norm_mod_bwd(name, x, g, scale, dh, dres, n_rows, tr, n_lat):
    nlt = n_lat // tr

    def fn(flag, xt, dht, drt, gv, sc):
        xh, r = _rms_hat(xt)
        n = xh * gv
        dn = dht * (1.0 + sc)
        dx = _rms_bwd(xh, r, dn * gv) + flag * drt
        return dx, dn * xh, dht * n, dht

    return _rowwise(name, fn, n_rows, tr, [(x, 0, D_MODEL), (dh, 0, D_MODEL), (dres, 0, D_MODEL, "clamp", nlt)],
                    [g, scale], [(D_MODEL, F32)], [(1, D_MODEL), (2, D_MODEL), (2, D_MODEL)], n_lat=n_lat,
                    want_flag=True)


def _s5_prep(a_re, a_im, log_dt, b_re, b_im, seg_lat, seg_ctx):
    def body(ar_ref, ai_ref, ld_ref, br_ref, bi_ref, abr_ref, abi_ref, bbr_ref, bbi_ref, alr_ref, ali_ref, acr_ref,
             aci_ref):
        lr, li = ar_ref[...], ai_ref[...]
        dt = jnp.exp(ld_ref[...])
        ldr, ldi = lr * dt, li * dt
        e = jnp.exp(ldr)
        abr, abi = e * jnp.cos(ldi), e * jnp.sin(ldi)
        abr_ref[...] = abr
        abi_ref[...] = abi
        den = lr * lr + li * li
        nr, ni = abr - 1.0, abi
        qr = (nr * lr + ni * li) / den
        qi = (ni * lr - nr * li) / den
        br, bi = br_ref[...], bi_ref[...]
        bbr_ref[...] = qr[None] * br - qi[None] * bi
        bbi_ref[...] = qr[None] * bi + qi[None] * br
        for seg, r_ref, i_ref in ((seg_lat, alr_ref, ali_ref), (seg_ctx, acr_ref, aci_ref)):
            es = jnp.exp(ldr * float(seg))
            r_ref[...] = es * jnp.cos(ldi * float(seg))
            i_ref[...] = es * jnp.sin(ldi * float(seg))

    sm = jax.ShapeDtypeStruct(a_re.shape, F32)
    big = jax.ShapeDtypeStruct(b_re.shape, F32)
    return pl.pallas_call(body, out_shape=[sm, sm, big, big, sm, sm, sm, sm], name="s5_prep")(
        a_re, a_im, log_dt, b_re, b_im)


def _s5_prep_bwd(a_re, a_im, log_dt, b_re, b_im, dabr, dabi, dbbr, dbbi):
    def body(ar_ref, ai_ref, ld_ref, br_ref, bi_ref, dabr_ref, dabi_ref, dbbr_ref, dbbi_ref,
             dar_ref, dai_ref, dld_ref, dbr_ref, dbi_ref):
        lr, li = ar_ref[...], ai_ref[...]
        dt = jnp.exp(ld_ref[...])
        ldr, ldi = lr * dt, li * dt
        e = jnp.exp(ldr)
        abr, abi = e * jnp.cos(ldi), e * jnp.sin(ldi)
        den = lr * lr + li * li
        nr, ni = abr - 1.0, abi
        qr = (nr * lr + ni * li) / den
        qi = (ni * lr - nr * li) / den
        br, bi = br_ref[...], bi_ref[...]
        gbr, gbi = dbbr_ref[...], dbbi_ref[...]
        dbr_ref[...] = gbr * qr[None] + gbi * qi[None]
        dbi_ref[...] = gbi * qr[None] - gbr * qi[None]
        dqr = jnp.sum(gbr * br + gbi * bi, axis=0)
        dqi = jnp.sum(gbi * br - gbr * bi, axis=0)
        dnr = (dqr * lr - dqi * li) / den
        dni = (dqr * li + dqi * lr) / den
        dlr_q = (dqr * (nr - 2.0 * lr * qr) + dqi * (ni - 2.0 * lr * qi)) / den
        dli_q = (dqr * (ni - 2.0 * li * qr) + dqi * (-nr - 2.0 * li * qi)) / den
        gar = dabr_ref[...] + dnr
        gai = dabi_ref[...] + dni
        dldr = gar * abr + gai * abi
        dldi = gai * abr - gar * abi
        dar_ref[...] = dldr * dt + dlr_q
        dai_ref[...] = dldi * dt + dli_q
        ddt = jnp.sum(dldr * lr + dldi * li, axis=1, keepdims=True)
        dld_ref[...] = ddt * dt

    sm = jax.ShapeDtypeStruct(a_re.shape, F32)
    big = jax.ShapeDtypeStruct(b_re.shape, F32)
    return pl.pallas_call(body, out_shape=[sm, sm, jax.ShapeDtypeStruct(log_dt.shape, F32), big, big],
                          name="s5_prep_bwd")(a_re, a_im, log_dt, b_re, b_im, dabr, dabi, dbbr, dbbi)


def _slab_cols(v):
    return v.reshape(N_SLAB, 1, HALF_W)


def _slab_pair(vr, vi):
    return jnp.concatenate([_slab_cols(vr), _slab_cols(vi)], axis=-1)


def _slab_in_matrix(bbr, bbi):
    eye = jnp.eye(SLAB_GROUPS, dtype=F32)

    def one(b):
        b = b.reshape(N_SLAB, SLAB_GROUPS, SSM_STATE, SSM_GROUP)
        m = jnp.einsum("sgph,gk->sghkp", b, eye)
        return m.reshape(N_SLAB, SLAB_W, HALF_W)

    return jnp.concatenate([one(bbr), one(bbi)], axis=-1)


def _slab_out_matrix(cr, ci):
    eye = jnp.eye(SLAB_GROUPS, dtype=F32)

    def one(c):
        c = c.reshape(N_SLAB, SLAB_GROUPS, SSM_GROUP, SSM_STATE)
        m = jnp.einsum("sghp,gk->skpgh", c, eye)
        return m.reshape(N_SLAB, HALF_W, SLAB_W)

    return jnp.concatenate([one(cr), one(-ci)], axis=1)


def _slab_diag(m):
    m = m.reshape(N_SLAB, SLAB_GROUPS, SSM_GROUP, 2, SLAB_GROUPS, SSM_STATE)
    d = jnp.stack([m[:, g, :, :, g, :] for g in range(SLAB_GROUPS)], axis=1)
    return d.transpose(3, 0, 1, 2, 4).reshape(2, SSM_GROUPS, SSM_GROUP, SSM_STATE)


def _cmul(ar, ai, xr, xi, conj):
    if conj:
        return ar * xr + ai * xi, ar * xi - ai * xr
    return ar * xr - ai * xi, ar * xi + ai * xr


def _s5_carry(name, z, a_seg, init, descending, conj):
    order = list(range(N_SEG - 1, -1, -1)) if descending else list(range(N_SEG))

    def body(z_ref, a_ref, i_ref, e_ref, o_ref):
        ar, ai = a_ref[:, :HALF_W], a_ref[:, HALF_W:]
        cr, ci = i_ref[:, :HALF_W], i_ref[:, HALF_W:]
        for j in order:
            e_ref[j, :, :HALF_W] = cr
            e_ref[j, :, HALF_W:] = ci
            pr, pi = _cmul(ar, ai, cr, ci, conj)
            cr = pr + z_ref[j, :, :HALF_W]
            ci = pi + z_ref[j, :, HALF_W:]
        o_ref[:, :HALF_W] = cr
        o_ref[:, HALF_W:] = ci

    return pl.pallas_call(body, out_shape=[jax.ShapeDtypeStruct(z.shape, F32), jax.ShapeDtypeStruct(init.shape, F32)],
                          name=name)(z, a_seg, init)


def _seg_major(v):
    return jnp.transpose(v, (1, 0, 2))


def _s5_scan(name, u, n_rows, row0, b_mat, c_mat, abar, h_in, descending, full, y_alias=None, y_rows=None):
    seg = n_rows // N_SEG
    ta = min(32, seg)
    nk = seg // ta
    assert seg * N_SEG == n_rows and nk * ta == seg and row0 % n_rows == 0 and ta % 8 == 0
    rb = row0 // n_rows
    tile = ta * N_SEG

    def body(*refs):
        if full:
            if y_alias is not None:
                u_ref, b_ref, c_ref, a_ref, hin_ref, _, hfin_ref, y_ref, hch_ref, st_ref, up_ref, h_ref = refs
            else:
                u_ref, b_ref, c_ref, a_ref, hin_ref, hfin_ref, y_ref, hch_ref, st_ref, up_ref, h_ref = refs
        else:
            u_ref, b_ref, a_ref, hin_ref, hfin_ref, st_ref, up_ref, h_ref = refs
        k = pl.program_id(1)
        kk = nk - 1 - k if descending else k
        a0 = kk * ta

        @pl.when(k == 0)
        def _():
            st_ref[...] = hin_ref[0]

        if full:
            hch_ref[0, 0] = st_ref[...]
        for al in range(ta):
            up_ref[al * N_SEG:(al + 1) * N_SEG, :] = u_ref[pl.ds(a0 + al, N_SEG, stride=seg), :]
        h_ref[...] = jnp.dot(up_ref[...].astype(BF16), b_ref[0], preferred_element_type=F32)
        ar = jnp.broadcast_to(a_ref[0, :, :HALF_W], (N_SEG, HALF_W))
        ai = jnp.broadcast_to(a_ref[0, :, HALF_W:], (N_SEG, HALF_W))

        def step(i, carry):
            hr, hi = carry
            al = ta - 1 - i if descending else i
            row = pl.multiple_of(al * N_SEG, N_SEG)
            pr, pi = _cmul(ar, ai, hr, hi, False)
            hr = pr + h_ref[pl.ds(row, N_SEG), :HALF_W]
            hi = pi + h_ref[pl.ds(row, N_SEG), HALF_W:]
            h_ref[pl.ds(row, N_SEG), :HALF_W] = hr
            h_ref[pl.ds(row, N_SEG), HALF_W:] = hi
            return hr, hi

        hr, hi = lax.fori_loop(0, ta, step, (st_ref[:, :HALF_W], st_ref[:, HALF_W:]))
        st_ref[:, :HALF_W] = hr
        st_ref[:, HALF_W:] = hi
        if full:
            yt = jnp.dot(h_ref[...].astype(BF16), c_ref[0], preferred_element_type=F32)
            for al in range(ta):
                y_ref[pl.ds(a0 + al, N_SEG, stride=seg), :] = yt[al * N_SEG:(al + 1) * N_SEG, :]

        @pl.when(k == nk - 1)
        def _():
            hfin_ref[0] = st_ref[...]

    u_spec = pl.BlockSpec((n_rows, SLAB_W), lambda s, k: (rb, s))
    b_spec = pl.BlockSpec((1, SLAB_W, STATE_W), lambda s, k: (s, 0, 0))
    c_spec = pl.BlockSpec((1, STATE_W, SLAB_W), lambda s, k: (s, 0, 0))
    a_spec = pl.BlockSpec((1, 1, STATE_W), lambda s, k: (s, 0, 0))
    st_spec = pl.BlockSpec((1, N_SEG, STATE_W), lambda s, k: (s, 0, 0))
    st_shape = jax.ShapeDtypeStruct((N_SLAB, N_SEG, STATE_W), F32)
    scratch = [pltpu.VMEM((N_SEG, STATE_W), F32), pltpu.VMEM((tile, SLAB_W), F32), pltpu.VMEM((tile, STATE_W), F32)]
    if not full:
        return pl.pallas_call(
            body, grid=(N_SLAB, nk), in_specs=[u_spec, b_spec, a_spec, st_spec], out_specs=st_spec, out_shape=st_shape,
            scratch_shapes=scratch, compiler_params=_params("parallel", "arbitrary"), name=name)(u, b_mat, abar, h_in)
    kmap = (lambda s, k: (s, nk - 1 - k, 0, 0)) if descending else (lambda s, k: (s, k, 0, 0))
    out_specs = [st_spec, u_spec, pl.BlockSpec((1, 1, N_SEG, STATE_W), kmap)]
    out_shape = [st_shape, jax.ShapeDtypeStruct((y_rows, D_MODEL), F32),
                 jax.ShapeDtypeStruct((N_SLAB, nk, N_SEG, STATE_W), F32)]
    in_specs = [u_spec, b_spec, c_spec, a_spec, st_spec]
    args = [u, b_mat, c_mat, abar, h_in]
    aliases = {}
    if y_alias is not None:
        in_specs.append(pl.BlockSpec(memory_space=pl.ANY))
        args.append(y_alias)
        aliases = {5: 1}
    return pl.pallas_call(
        body, grid=(N_SLAB, nk), in_specs=in_specs, out_specs=out_specs, out_shape=out_shape, scratch_shapes=scratch,
        input_output_aliases=aliases, compiler_params=_params("parallel", "arbitrary"), name=name)(*args)


def _s5_scan_bwd(name, u, dy, n_rows, row0, b_mat, bt_mat, ct_mat, abar, h_chunks, g_in, descending, full,
                 du_alias=None, du_rows=None):
    seg = n_rows // N_SEG
    ta = min(32, seg)
    nk = seg // ta
    rb = row0 // n_rows
    tile = ta * N_SEG
    g_desc = not descending

    def body(*refs):
        if full:
            (u_ref, dy_ref, b_ref, bt_ref, ct_ref, a_ref, hch_ref, gin_ref) = refs[:8]
            rest = refs[9:] if du_alias is not None else refs[8:]
            gfin_ref, du_ref, db_ref, dc_ref, da_ref, st_ref, up_ref, dyp_ref, h_ref, g_ref = rest
        else:
            dy_ref, ct_ref, a_ref, gin_ref, gfin_ref, st_ref, dyp_ref, g_ref = refs
        k = pl.program_id(1)
        kk = nk - 1 - k if g_desc else k
        a0 = kk * ta
        ar = jnp.broadcast_to(a_ref[0, :, :HALF_W], (N_SEG, HALF_W))
        ai = jnp.broadcast_to(a_ref[0, :, HALF_W:], (N_SEG, HALF_W))

        @pl.when(k == 0)
        def _():
            st_ref[...] = gin_ref[0]

        for al in range(ta):
            dyp_ref[al * N_SEG:(al + 1) * N_SEG, :] = dy_ref[pl.ds(a0 + al, N_SEG, stride=seg), :]
        g_ref[...] = jnp.dot(dyp_ref[...].astype(BF16), ct_ref[0], preferred_element_type=F32)

        if full:
            for al in range(ta):
                up_ref[al * N_SEG:(al + 1) * N_SEG, :] = u_ref[pl.ds(a0 + al, N_SEG, stride=seg), :]
            h_ref[...] = jnp.dot(up_ref[...].astype(BF16), b_ref[0], preferred_element_type=F32)
            h0r, h0i = hch_ref[0, 0, :, :HALF_W], hch_ref[0, 0, :, HALF_W:]

            def hstep(i, carry):
                hr, hi = carry
                al = ta - 1 - i if descending else i
                row = pl.multiple_of(al * N_SEG, N_SEG)
                pr, pi = _cmul(ar, ai, hr, hi, False)
                hr = pr + h_ref[pl.ds(row, N_SEG), :HALF_W]
                hi = pi + h_ref[pl.ds(row, N_SEG), HALF_W:]
                h_ref[pl.ds(row, N_SEG), :HALF_W] = hr
                h_ref[pl.ds(row, N_SEG), HALF_W:] = hi
                return hr, hi

            lax.fori_loop(0, ta, hstep, (h0r, h0i))

        def gstep(i, carry):
            gr, gi = carry
            al = ta - 1 - i if g_desc else i
            row = pl.multiple_of(al * N_SEG, N_SEG)
            pr, pi = _cmul(ar, ai, gr, gi, True)
            gr = pr + g_ref[pl.ds(row, N_SEG), :HALF_W]
            gi = pi + g_ref[pl.ds(row, N_SEG), HALF_W:]
            g_ref[pl.ds(row, N_SEG), :HALF_W] = gr
            g_ref[pl.ds(row, N_SEG), HALF_W:] = gi
            return gr, gi

        gr, gi = lax.fori_loop(0, ta, gstep, (st_ref[:, :HALF_W], st_ref[:, HALF_W:]))
        st_ref[:, :HALF_W] = gr
        st_ref[:, HALF_W:] = gi

        @pl.when(k == nk - 1)
        def _():
            gfin_ref[0] = st_ref[...]

        if full:
            gb = g_ref[...].astype(BF16)
            dut = jnp.dot(gb, bt_ref[0], preferred_element_type=F32)
            for al in range(ta):
                du_ref[pl.ds(a0 + al, N_SEG, stride=seg), :] = dut[al * N_SEG:(al + 1) * N_SEG, :]
            tn = (((0,), (0,)), ((), ()))
            dbp = lax.dot_general(up_ref[...].astype(BF16), gb, tn, preferred_element_type=F32)
            dcp = lax.dot_general(dyp_ref[...].astype(BF16), h_ref[...].astype(BF16), tn, preferred_element_type=F32)
            inner = (ta - 1) * N_SEG
            if descending:
                g_in_r, g_in_i = g_ref[0:inner, :HALF_W], g_ref[0:inner, HALF_W:]
                p_in_r, p_in_i = h_ref[N_SEG:tile, :HALF_W], h_ref[N_SEG:tile, HALF_W:]
                g_ed_r, g_ed_i = g_ref[inner:tile, :HALF_W], g_ref[inner:tile, HALF_W:]
            else:
                g_in_r, g_in_i = g_ref[N_SEG:tile, :HALF_W], g_ref[N_SEG:tile, HALF_W:]
                p_in_r, p_in_i = h_ref[0:inner, :HALF_W], h_ref[0:inner, HALF_W:]
                g_ed_r, g_ed_i = g_ref[0:N_SEG, :HALF_W], g_ref[0:N_SEG, HALF_W:]
            dar = g_ed_r * h0r + g_ed_i * h0i
            dai = g_ed_i * h0r - g_ed_r * h0i
            if ta > 1:
                dar = dar + jnp.sum((g_in_r * p_in_r + g_in_i * p_in_i).reshape(ta - 1, N_SEG, HALF_W), axis=0)
                dai = dai + jnp.sum((g_in_i * p_in_r - g_in_r * p_in_i).reshape(ta - 1, N_SEG, HALF_W), axis=0)

            @pl.when(k == 0)
            def _():
                db_ref[0] = dbp
                dc_ref[0] = dcp
                da_ref[0, :, :HALF_W] = dar
                da_ref[0, :, HALF_W:] = dai

            @pl.when(k > 0)
            def _():
                db_ref[0] += dbp
                dc_ref[0] += dcp
                da_ref[0, :, :HALF_W] += dar
                da_ref[0, :, HALF_W:] += dai

    u_spec = pl.BlockSpec((n_rows, SLAB_W), lambda s, k: (rb, s))
    m_spec = pl.BlockSpec((1, SLAB_W, STATE_W), lambda s, k: (s, 0, 0))
    mt_spec = pl.BlockSpec((1, STATE_W, SLAB_W), lambda s, k: (s, 0, 0))
    a_spec = pl.BlockSpec((1, 1, STATE_W), lambda s, k: (s, 0, 0))
    st_spec = pl.BlockSpec((1, N_SEG, STATE_W), lambda s, k: (s, 0, 0))
    st_shape = jax.ShapeDtypeStruct((N_SLAB, N_SEG, STATE_W), F32)
    if not full:
        scratch = [pltpu.VMEM((N_SEG, STATE_W), F32), pltpu.VMEM((tile, SLAB_W), F32), pltpu.VMEM((tile, STATE_W), F32)]
        return pl.pallas_call(
            body, grid=(N_SLAB, nk), in_specs=[u_spec, m_spec, a_spec, st_spec], out_specs=st_spec, out_shape=st_shape,
            scratch_shapes=scratch, compiler_params=_params("parallel", "arbitrary"), name=name)(dy, ct_mat, abar, g_in)
    kmap = (lambda s, k: (s, nk - 1 - k, 0, 0)) if g_desc else (lambda s, k: (s, k, 0, 0))
    in_specs = [u_spec, u_spec, m_spec, mt_spec, m_spec, a_spec, pl.BlockSpec((1, 1, N_SEG, STATE_W), kmap), st_spec]
    args = [u, dy, b_mat, bt_mat, ct_mat, abar, h_chunks, g_in]
    aliases = {}
    if du_alias is not None:
        in_specs.append(pl.BlockSpec(memory_space=pl.ANY))
        args.append(du_alias)
        aliases = {8: 1}
    acc_shape = jax.ShapeDtypeStruct((N_SLAB, SLAB_W, STATE_W), F32)
    out_specs = [st_spec, u_spec, m_spec, m_spec, st_spec]
    out_shape = [st_shape, jax.ShapeDtypeStruct((du_rows, D_MODEL), F32), acc_shape, acc_shape, st_shape]
    scratch = [pltpu.VMEM((N_SEG, STATE_W), F32), pltpu.VMEM((tile, SLAB_W), F32), pltpu.VMEM((tile, SLAB_W), F32),
               pltpu.VMEM((tile, STATE_W), F32), pltpu.VMEM((tile, STATE_W), F32)]
    return pl.pallas_call(
        body, grid=(N_SLAB, nk), in_specs=in_specs, out_specs=out_specs, out_shape=out_shape, scratch_shapes=scratch,
        input_output_aliases=aliases, compiler_params=_params("parallel", "arbitrary"), name=name)(*args)


ROPE_HALF = HEAD_DIM // 4
TABLE_W = 2 * HEAD_DIM
Q_SCALE = 1.0 / math.sqrt(HEAD_DIM)
HEADS_PER_BLOCK = 2 * KV_REP
Q_BLOCK_W = HEADS_PER_BLOCK * HEAD_DIM


def _rope_tables(n_lat, n_ctx):
    rows = n_lat // GRID_W
    row = jnp.repeat(jnp.arange(rows), GRID_W).astype(F32)
    col = jnp.tile(jnp.arange(GRID_W), rows).astype(F32)
    freqs = ROPE_THETA ** (-jnp.arange(ROPE_HALF, dtype=F32) / ROPE_HALF)
    ang_r = row[:, None] * freqs[None]
    ang_c = col[:, None] * freqs[None]
    cos = jnp.concatenate([jnp.cos(ang_r), jnp.cos(ang_r), jnp.cos(ang_c), jnp.cos(ang_c)] * 2, axis=1)
    sin = jnp.concatenate([jnp.sin(ang_r), jnp.sin(ang_r), jnp.sin(ang_c), jnp.sin(ang_c)] * 2, axis=1)
    cos = jnp.concatenate([cos, jnp.ones((n_ctx, TABLE_W), F32)], axis=0)
    sin = jnp.concatenate([sin, jnp.zeros((n_ctx, TABLE_W), F32)], axis=0)
    return cos, sin


def _rot_half(v):
    w = v.shape[1]
    ahead = pltpu.roll(v, w - ROPE_HALF, axis=1)
    behind = pltpu.roll(v, ROPE_HALF, axis=1)
    lane = lax.broadcasted_iota(jnp.int32, v.shape, 1)
    return jnp.where((lane % (2 * ROPE_HALF)) < ROPE_HALF, -ahead, behind)


def _head_mean(v, sel, selt):
    m = jnp.dot(v, sel, precision=lax.Precision.HIGHEST, preferred_element_type=F32) * (1.0 / HEAD_DIM)
    return jnp.dot(m, selt, precision=lax.Precision.HIGHEST, preferred_element_type=F32)


def _head_selectors(n_heads):
    sel = jnp.repeat(jnp.eye(n_heads, dtype=F32), HEAD_DIM, axis=0)
    return sel[None], sel.T[None]


def _head_norm(x, sel, selt):
    r = lax.rsqrt(_head_mean(x * x, sel, selt) + NORM_EPS)
    return x * r, r


def _qk_prep(proj, qn, kn, cos, sin, n, tr):
    qw, kw = _vec(jnp.tile(qn, N_Q_HEADS)), _vec(jnp.tile(kn, N_KV_HEADS))
    sq, sqt = _head_selectors(N_Q_HEADS)
    sk, skt = _head_selectors(N_KV_HEADS)

    def fn(qr, kvr, ct, st, qwv, kwv, s16, s16t, s4, s4t):
        outs = []
        for x, wv, sel, selt, scale in ((qr, qwv, s16, s16t, Q_SCALE), (kvr[:, :KV_W], kwv, s4, s4t, 1.0)):
            reps = x.shape[1] // TABLE_W
            cw, sw = jnp.tile(ct, (1, reps)), jnp.tile(st, (1, reps))
            xh, _ = _head_norm(x, sel, selt)
            nrm = xh * wv
            outs.append((nrm * cw + _rot_half(nrm) * sw) * scale)
        return outs[0], outs[1], kvr[:, KV_W:]

    return _rowwise("l1_qk_prep", fn, n, tr,
                    [(proj, 0, ATTN_W), (proj, 2 * ATTN_W // (2 * KV_W), 2 * KV_W), (cos, 0, TABLE_W), (sin, 0, TABLE_W)],
                    [qw, kw, sq, sqt, sk, skt], [(ATTN_W, BF16), (KV_W, BF16), (KV_W, BF16)], [])


def _qk_prep_bwd(proj, qn, kn, cos, sin, dq, dz, dk, dv, n, n_lat, tr):
    qw, kw = _vec(jnp.tile(qn, N_Q_HEADS)), _vec(jnp.tile(kn, N_KV_HEADS))
    sq, sqt = _head_selectors(N_Q_HEADS)
    sk, skt = _head_selectors(N_KV_HEADS)
    nlt = n_lat // tr

    def fn(flag, qr, kvr, ct, st, dqt, dzt, dkt, dvt, qwv, kwv, s16, s16t, s4, s4t):
        dxs, dws = [], []
        for x, dy, wv, sel, selt in ((qr, dqt * (flag * Q_SCALE), qwv, s16, s16t), (kvr[:, :KV_W], dkt, kwv, s4, s4t)):
            reps = x.shape[1] // TABLE_W
            cw, sw = jnp.tile(ct, (1, reps)), jnp.tile(st, (1, reps))
            xh, r = _head_norm(x, sel, selt)
            dn = dy * cw - _rot_half(dy * sw)
            dxh = dn * wv
            dxs.append(r * (dxh - xh * _head_mean(dxh * xh, sel, selt)))
            dws.append(dn * xh)
        return jnp.concatenate([dxs[0], dzt * flag, dxs[1], dvt], axis=1), dws[0], dws[1]

    dproj, dqw, dkw = _rowwise(
        "l1_qk_prep_bwd", fn, n, tr,
        [(proj, 0, ATTN_W), (proj, 2 * ATTN_W // (2 * KV_W), 2 * KV_W), (cos, 0, TABLE_W), (sin, 0, TABLE_W),
         (dq, 0, ATTN_W, "clamp", nlt), (dz, 0, ATTN_W, "clamp", nlt), (dk, 0, KV_W), (dv, 0, KV_W)],
        [qw, kw, sq, sqt, sk, skt], [(2 * ATTN_W + 2 * KV_W, F32)], [(1, ATTN_W), (1, KV_W)], n_lat=n_lat, want_flag=True)
    return dproj, dqw.reshape(N_Q_HEADS, HEAD_DIM).sum(0)[None], dkw.reshape(N_KV_HEADS, HEAD_DIM).sum(0)[None]


NT = (((1,), (1,)), ((), ()))
LANES = 128


def _attn_fwd(q, k, v, t, tq, tk):
    n = k.shape[0]
    nkc = n // tk

    ts = _largest_tile(tq, 256, LANES)
    items = [(sub, j) for sub in range(tq // ts) for j in range(HEADS_PER_BLOCK)]

    def body(q_ref, k_ref, v_ref, o_ref, lse_ref, s_ref, m_ref, l_ref, acc_ref):
        def lanes(j):
            g = j // KV_REP
            return slice(j * HEAD_DIM, (j + 1) * HEAD_DIM), slice(g * HEAD_DIM, (g + 1) * HEAD_DIM)

        for idx in range(len(items) + 1):
            nxt = items[idx] if idx < len(items) else None
            cur = items[idx - 1] if idx > 0 else None
            sn, sc = idx % 2, (idx - 1) % 2
            if nxt is not None:
                rows_n = slice(nxt[0] * ts, (nxt[0] + 1) * ts)
                ql_n, kl_n = lanes(nxt[1])
                qv = q_ref[rows_n, ql_n]
                m_ref[sn] = jnp.full((ts, LANES), -jnp.inf, F32)
            if cur is not None:
                rows_c = slice(cur[0] * ts, (cur[0] + 1) * ts)
                ql_c, kl_c = lanes(cur[1])
                m_row = jnp.max(m_ref[sc], axis=-1, keepdims=True)
                l_ref[...] = jnp.zeros(l_ref.shape, F32)
                acc_ref[...] = jnp.zeros(acc_ref.shape, F32)

            def sweep(kc, c):
                off = pl.multiple_of(kc * tk, tk)
                if nxt is not None:
                    s = lax.dot_general(qv, k_ref[pl.ds(off, tk), kl_n], NT, preferred_element_type=F32)
                    s_ref[sn, :, pl.ds(off, tk)] = s
                    m = m_ref[sn]
                    for cb in range(tk // LANES):
                        m = jnp.maximum(m, s[:, cb * LANES:(cb + 1) * LANES])
                    m_ref[sn] = m
                if cur is not None:
                    p = jnp.exp(s_ref[sc, :, pl.ds(off, tk)] - m_row)
                    lsum = l_ref[...]
                    for cb in range(tk // LANES):
                        lsum = lsum + p[:, cb * LANES:(cb + 1) * LANES]
                    l_ref[...] = lsum
                    acc_ref[...] += jnp.dot(p.astype(BF16), v_ref[pl.ds(off, tk), kl_c], preferred_element_type=F32)
                return c

            lax.fori_loop(0, nkc, sweep, 0)
            if cur is not None:
                l_row = jnp.sum(l_ref[...], axis=-1, keepdims=True)
                o_ref[rows_c, ql_c] = acc_ref[...] / l_row
                lse_ref[0, rows_c, cur[1]:cur[1] + 1] = m_row + jnp.log(l_row)

    nb = ATTN_W // Q_BLOCK_W
    kspec = pl.BlockSpec((n, LANES), lambda b, i: (0, b))
    return pl.pallas_call(
        body, grid=(nb, t // tq),
        in_specs=[pl.BlockSpec((tq, Q_BLOCK_W), lambda b, i: (i, b)), kspec, kspec],
        out_specs=[pl.BlockSpec((tq, Q_BLOCK_W), lambda b, i: (i, b)),
                   pl.BlockSpec((1, tq, HEADS_PER_BLOCK), lambda b, i: (b, i, 0))],
        out_shape=[jax.ShapeDtypeStruct((t, ATTN_W), F32), jax.ShapeDtypeStruct((nb, t, HEADS_PER_BLOCK), F32)],
        scratch_shapes=[pltpu.VMEM((2, ts, n), F32), pltpu.VMEM((2, ts, LANES), F32), pltpu.VMEM((ts, LANES), F32),
                        pltpu.VMEM((ts, HEAD_DIM), F32)],
        compiler_params=_params("parallel", "parallel"), name="attn_fwd")(q, k, v)


def _attn_bwd(q, k, v, do, o, lse, t, tq, tk):
    n = k.shape[0]
    nkc = n // tk
    tn = (((0,), (0,)), ((), ()))

    def body(q_ref, k_ref, v_ref, do_ref, o_ref, lse_ref, dq_ref, dk_ref, dv_ref, acc_ref):
        @pl.when(pl.program_id(1) == 0)
        def _():
            dk_ref[...] = jnp.zeros(dk_ref.shape, F32)
            dv_ref[...] = jnp.zeros(dv_ref.shape, F32)

        for j0 in range(0, HEADS_PER_BLOCK, 2):
            kl = slice((j0 // KV_REP) * HEAD_DIM, (j0 // KV_REP + 1) * HEAD_DIM)
            heads = []
            for a in range(2):
                j = j0 + a
                ql = slice(j * HEAD_DIM, (j + 1) * HEAD_DIM)
                qv, dov = q_ref[:, ql], do_ref[:, ql]
                dl_v = jnp.sum(dov.astype(F32) * o_ref[:, ql], axis=-1, keepdims=True)
                heads.append((ql, qv, dov, dl_v, lse_ref[0, :, j:j + 1]))
                acc_ref[a] = jnp.zeros((tq, HEAD_DIM), F32)

            def step(kc, c):
                off = pl.multiple_of(kc * tk, tk)
                kt = k_ref[pl.ds(off, tk), kl]
                vt = v_ref[pl.ds(off, tk), kl]
                dv_part, dk_part = None, None
                for a, (_, qv, dov, dl_v, lse_v) in enumerate(heads):
                    s = lax.dot_general(qv, kt, NT, preferred_element_type=F32)
                    p = jnp.exp(s - lse_v)
                    dp = lax.dot_general(dov, vt, NT, preferred_element_type=F32)
                    ds = (p * (dp - dl_v)).astype(BF16)
                    acc_ref[a] += jnp.dot(ds, kt, preferred_element_type=F32)
                    dvp = lax.dot_general(p.astype(BF16), dov, tn, preferred_element_type=F32)
                    dkp = lax.dot_general(ds, qv, tn, preferred_element_type=F32)
                    dv_part = dvp if dv_part is None else dv_part + dvp
                    dk_part = dkp if dk_part is None else dk_part + dkp
                dv_ref[pl.ds(off, tk), kl] += dv_part
                dk_ref[pl.ds(off, tk), kl] += dk_part
                return c

            lax.fori_loop(0, nkc, step, 0)
            for a, h in enumerate(heads):
                dq_ref[:, h[0]] = acc_ref[a]

    nb = ATTN_W // Q_BLOCK_W
    qspec = pl.BlockSpec((tq, Q_BLOCK_W), lambda b, i: (i, b))
    kspec = pl.BlockSpec((n, LANES), lambda b, i: (0, b))
    cspec = pl.BlockSpec((1, tq, HEADS_PER_BLOCK), lambda b, i: (b, i, 0))
    return pl.pallas_call(
        body, grid=(nb, t // tq), in_specs=[qspec, kspec, kspec, qspec, qspec, cspec], out_specs=[qspec, kspec, kspec],
        out_shape=[jax.ShapeDtypeStruct((t, ATTN_W), F32), jax.ShapeDtypeStruct((n, KV_W), F32),
                   jax.ShapeDtypeStruct((n, KV_W), F32)],
        scratch_shapes=[pltpu.VMEM((2, tq, HEAD_DIM), F32)],
        compiler_params=_params("parallel", "arbitrary"), name="attn_bwd")(q, k, v, do, o, lse)


def _s5_system(p, n_lat, n_ctx):
    two_g = 2 * SSM_GROUPS
    a_re = p["ssm_a_re"].reshape(two_g, SSM_STATE)
    a_im = p["ssm_a_im"].reshape(two_g, SSM_STATE)
    log_dt = p["ssm_log_dt"].reshape(two_g, 1)
    b_re = p["ssm_b_re"].reshape(two_g, SSM_STATE, SSM_GROUP).transpose(2, 0, 1)
    b_im = p["ssm_b_im"].reshape(two_g, SSM_STATE, SSM_GROUP).transpose(2, 0, 1)
    raw = (a_re, a_im, log_dt, b_re, b_im)
    abr, abi, bbr, bbi, alr, ali, acr, aci = _s5_prep(*raw, n_lat // N_SEG, n_ctx // N_SEG)
    dirs = []
    for d in range(2):
        g = slice(d * SSM_GROUPS, (d + 1) * SSM_GROUPS)
        b_mat = _slab_in_matrix(bbr[:, g].transpose(1, 2, 0), bbi[:, g].transpose(1, 2, 0))
        c_mat = _slab_out_matrix(p["ssm_c_re"][0, d], p["ssm_c_im"][0, d])
        dirs.append(dict(
            b=b_mat.astype(BF16), bt=b_mat.transpose(0, 2, 1).astype(BF16),
            c=c_mat.astype(BF16), ct=c_mat.transpose(0, 2, 1).astype(BF16),
            abar=_slab_pair(abr[g], abi[g]),
            a_lat=_slab_pair(alr[g], ali[g])[:, 0], a_ctx=_slab_pair(acr[g], aci[g])[:, 0]))
    return raw, dirs


def _s5_forward(proj, dirs, n_lat, n_ctx):
    n = n_lat + n_ctx
    zero_st = jnp.zeros((N_SLAB, N_SEG, STATE_W), F32)
    zero_c = jnp.zeros((N_SLAB, STATE_W), F32)
    ys, saved = [], []
    for d, s in enumerate(dirs):
        desc = d == 1
        tag = f"s5f{d}"
        zc = _s5_scan(tag + "_ctx_ends", proj, n_ctx, n_lat, s["b"], s["c"], s["abar"], zero_st, desc, False)
        ent_c, h0 = _s5_carry(tag + "_ctx_carry", _seg_major(zc), s["a_ctx"], zero_c, desc, False)
        _, y, hch_c = _s5_scan(tag + "_ctx", proj, n_ctx, n_lat, s["b"], s["c"], s["abar"], _seg_major(ent_c), desc,
                               True, y_rows=n)
        zl = _s5_scan(tag + "_lat_ends", proj, n_lat, 0, s["b"], s["c"], s["abar"], zero_st, desc, False)
        ent_l, _ = _s5_carry(tag + "_lat_carry", _seg_major(zl), s["a_lat"], h0, desc, False)
        _, y, hch_l = _s5_scan(tag + "_lat", proj, n_lat, 0, s["b"], s["c"], s["abar"], _seg_major(ent_l), desc,
                               True, y_alias=y, y_rows=n)
        ys.append(y)
        saved.append((hch_l, hch_c))
    return ys, saved


def _s5_backward(proj, dy, dirs, saved, n_lat, n_ctx):
    n = n_lat + n_ctx
    zero_st = jnp.zeros((N_SLAB, N_SEG, STATE_W), F32)
    zero_c = jnp.zeros((N_SLAB, STATE_W), F32)
    out = []
    for d, s in enumerate(dirs):
        desc = d == 1
        tag = f"s5b{d}"
        hch_l, hch_c = saved[d]
        gl = _s5_scan_bwd(tag + "_lat_ends", proj, dy, n_lat, 0, s["b"], s["bt"], s["ct"], s["abar"], None, zero_st,
                          desc, False)
        ent_l, g0 = _s5_carry(tag + "_lat_carry", _seg_major(gl), s["a_lat"], zero_c, not desc, True)
        _, du, db_l, dc_l, da_l = _s5_scan_bwd(tag + "_lat", proj, dy, n_lat, 0, s["b"], s["bt"], s["ct"], s["abar"],
                                               hch_l, _seg_major(ent_l), desc, True, du_rows=n)
        gc = _s5_scan_bwd(tag + "_ctx_ends", proj, dy, n_ctx, n_lat, s["b"], s["bt"], s["ct"], s["abar"], None, zero_st,
                          desc, False)
        ent_c, _ = _s5_carry(tag + "_ctx_carry", _seg_major(gc), s["a_ctx"], g0, not desc, True)
        _, du, db_c, dc_c, da_c = _s5_scan_bwd(tag + "_ctx", proj, dy, n_ctx, n_lat, s["b"], s["bt"], s["ct"], s["abar"],
                                               hch_c, _seg_major(ent_c), desc, True, du_alias=du, du_rows=n)
        out.append((du, db_l + db_c, dc_l + dc_c, da_l + da_c))
    return out


def _s5_param_grads(raw, bwd):
    dabr, dabi, dbbr, dbbi, dcr, dci = [], [], [], [], [], []
    for _, db, dc, da in bwd:
        da = jnp.sum(da, axis=1)
        dabr.append(da[:, :HALF_W].reshape(SSM_GROUPS, SSM_STATE))
        dabi.append(da[:, HALF_W:].reshape(SSM_GROUPS, SSM_STATE))
        dbd = _slab_diag(db)
        dbbr.append(dbd[0].transpose(1, 0, 2))
        dbbi.append(dbd[1].transpose(1, 0, 2))
        dcd = _slab_diag(dc)
        dcr.append(dcd[0])
        dci.append(-dcd[1])
    cat = lambda xs, ax: jnp.concatenate(xs, axis=ax)
    dar, dai, dld, dbr, dbi = _s5_prep_bwd(*raw, cat(dabr, 0), cat(dabi, 0), cat(dbbr, 1), cat(dbbi, 1))
    shp = (1, 2, SSM_GROUPS, SSM_STATE)
    b_shape = (1, 2, SSM_GROUPS, SSM_STATE, SSM_GROUP)
    return dict(
        ssm_a_re=dar.reshape(shp), ssm_a_im=dai.reshape(shp), ssm_log_dt=dld.reshape(1, 2, SSM_GROUPS),
        ssm_b_re=dbr.transpose(1, 2, 0).reshape(b_shape), ssm_b_im=dbi.transpose(1, 2, 0).reshape(b_shape),
        ssm_c_re=jnp.stack(dcr)[None], ssm_c_im=jnp.stack(dci)[None])


def _example_step(x, ctx, target, mods, w, p):
    t, c = x.shape[0], ctx.shape[0]
    n = t + c
    assert t % c == 0 and c % (8 * N_SEG) == 0 and t % GRID_W == 0
    tr = _largest_tile(c, 256, 8)
    xall = jnp.concatenate([x, ctx], axis=0)
    g0, g1 = _vec(p["norm_g"][0]), _vec(p["norm_g"][1])
    (shift0, scale0, gate0), (shift1, scale1, gate1) = [tuple(_vec(v) for v in m) for m in mods]

    h0 = _norm_mod_fwd("l0_norm", xall, g0, scale0, shift0, n, tr, t)
    proj0 = _mm("l0_in", h0, w["ssm_w_in"], "nn")
    raw, dirs = _s5_system(p, t, c)
    (y_f, y_r), saved = _s5_forward(proj0, dirs, t, c)
    d_skip = _vec(p["ssm_d"][0])

    def post_a(u, yf, yr, dv):
        y = u * dv + yf + yr
        return y, _gelu(y)

    y0, yg = _rowwise("l0_gelu", post_a, n, tr, [(proj0, 0, D_MODEL), (y_f, 0, D_MODEL), (y_r, 0, D_MODEL)], [d_skip],
                      [(D_MODEL, F32), (D_MODEL, F32)], [])
    tg = _mm("l0_glu", yg, w["ssm_w_glu"], "nn")
    b_glu = _vec(p["ssm_b_glu"][0])

    def post_b(ygt, tt, zt, bv):
        return ygt * _sigmoid(tt + bv) * _silu(zt)

    gz0 = _rowwise("l0_gate", post_b, n, tr, [(yg, 0, D_MODEL), (tg, 0, D_MODEL), (proj0, 1, D_MODEL)], [b_glu],
                   [(D_MODEL, F32)], [])[0]
    out0 = _mm("l0_out", gz0, w["ssm_w_out"], "nn")
    x1 = _rowwise("l0_res", lambda xt, ot, gv: xt + gv * ot, n, tr, [(xall, 0, D_MODEL), (out0, 0, D_MODEL)], [gate0],
                  [(D_MODEL, F32)], [], n_lat=t)[0]

    h1 = _norm_mod_fwd("l1_norm", x1, g1, scale1, shift1, n, tr, t)
    proj1 = _mm("l1_in", h1, w["attn_w_in"], "nn")
    cos, sin = _rope_tables(t, c)
    qn, kn = p["attn_q_norm"][0], p["attn_k_norm"][0]
    q_h, k_h, v_h = _qk_prep(proj1, qn, kn, cos, sin, n, tr)
    tq = _largest_tile(t, 512, LANES)
    o, lse = _attn_fwd(q_h, k_h, v_h, t, tq, _largest_tile(n, 2816, LANES))
    gz1 =_rowwise("l1_gate", lambda ot, zt: ot * _silu(zt), t, tr, [(o, 0, D_MODEL), (proj1, 1, D_MODEL)], [],
                   [(D_MODEL, F32)], [])[0]
    out1 = _mm("l1_out", gz1, w["attn_w_out"], "nn")

    gf = _vec(p["final_norm_g"])

    def head(x1t, o1t, tgt, g1v, gfv):
        x2 = x1t + g1v * o1t
        xh, r = _rms_hat(x2)
        e = xh * gfv - tgt
        dyf = e * (1.0 / D_MODEL)
        dx2 = _rms_bwd(xh, r, dyf * gfv)
        return dx2, g1v * dx2, dyf * xh, dx2 * o1t, jnp.sum(e * e, axis=1, keepdims=True)

    gate1_lat = gate1[0:1]
    dx2, dout1, d_gf, d_gate1, sq = _rowwise(
        "head", head, t, tr, [(x1, 0, D_MODEL), (out1, 0, D_MODEL), (target, 0, D_MODEL)], [gate1_lat, gf],
        [(D_MODEL, F32), (D_MODEL, F32)], [(1, D_MODEL), (1, D_MODEL), (1, 1)])

    d_w_attn_out = _mm("l1_out_dw", gz1, dout1, "tn")
    dgz1 = _mm("l1_out_dx", dout1, w["attn_w_out"], "nt")

    def gate1_bwd(dgt, ot, zt):
        return dgt * _silu(zt), dgt * ot * _silu_grad(zt)

    do, dz1 = _rowwise("l1_gate_bwd", gate1_bwd, t, tr, [(dgz1, 0, D_MODEL), (o, 0, D_MODEL), (proj1, 1, D_MODEL)], [],
                       [(D_MODEL, BF16), (D_MODEL, F32)], [])
    dq_s, dk, dv = _attn_bwd(q_h, k_h, v_h, do, o, lse, t, tq, _largest_tile(n, 1024, LANES))
    dproj1, d_qn, d_kn = _qk_prep_bwd(proj1, qn, kn, cos, sin, dq_s, dz1, dk, dv, n, t, tr)
    d_w_attn_in = _mm("l1_in_dw", h1, dproj1, "tn")
    dh1 = _mm("l1_in_dx", dproj1, w["attn_w_in"], "nt")
    dx1, d_g1, d_scale1, d_shift1 = _norm_mod_bwd("l1_norm_bwd", x1, g1, scale1, dh1, dx2, n, tr, t)

    dout0, d_gate0 = _rowwise("l0_res_bwd", lambda dxt, ot, gv: (gv * dxt, dxt * ot), n, tr,
                              [(dx1, 0, D_MODEL), (out0, 0, D_MODEL)], [gate0], [(D_MODEL, F32)], [(2, D_MODEL)], n_lat=t)
    d_w_out = _mm("l0_out_dw", gz0, dout0, "tn")
    dgz0 = _mm("l0_out_dx", dout0, w["ssm_w_out"], "nt")

    def post_b_bwd(dgt, ygt, tt, zt, bv):
        s = _sigmoid(tt + bv)
        dy2 = dgt * _silu(zt)
        dt = dy2 * ygt * s * (1.0 - s)
        return dgt * (ygt * s) * _silu_grad(zt), dt, dy2 * s, dt

    dz0, dtg, dyg_a, d_b_glu = _rowwise(
        "l0_gate_bwd", post_b_bwd, n, tr, [(dgz0, 0, D_MODEL), (yg, 0, D_MODEL), (tg, 0, D_MODEL), (proj0, 1, D_MODEL)],
        [b_glu], [(D_MODEL, F32), (D_MODEL, F32), (D_MODEL, F32)], [(1, D_MODEL)])
    d_w_glu = _mm("l0_glu_dw", yg, dtg, "tn")
    dyg_b = _mm("l0_glu_dx", dtg, w["ssm_w_glu"], "nt")

    def post_a_bwd(da, db, yt, ut, dv):
        dy = (da + db) * _gelu_grad(yt)
        return dy, dy * dv, dy * ut

    dy0, du_skip, d_d = _rowwise("l0_gelu_bwd", post_a_bwd, n, tr,
                                 [(dyg_a, 0, D_MODEL), (dyg_b, 0, D_MODEL), (y0, 0, D_MODEL), (proj0, 0, D_MODEL)], [d_skip],
                                 [(D_MODEL, F32), (D_MODEL, F32)], [(1, D_MODEL)])
    s5_bwd = _s5_backward(proj0, dy0, dirs, saved, t, c)
    dproj0 = _rowwise("l0_in_grad", lambda a, b, cc, dz: jnp.concatenate([a + b + cc, dz], axis=1), n, tr,
                      [(du_skip, 0, D_MODEL), (s5_bwd[0][0], 0, D_MODEL), (s5_bwd[1][0], 0, D_MODEL), (dz0, 0, D_MODEL)], [],
                      [(2 * D_MODEL, F32)], [])[0]
    d_w_in = _mm("l0_in_dw", h0, dproj0, "tn")
    dh0 = _mm("l0_in_dx", dproj0, w["ssm_w_in"], "nt")
    dx0, d_g0, d_scale0, d_shift0 = _norm_mod_bwd("l0_norm_bwd", xall, g0, scale0, dh0, dx1, n, tr, t)

    big = dict(ssm_w_in=d_w_in, ssm_w_glu=d_w_glu, ssm_w_out=d_w_out, attn_w_in=d_w_attn_in, attn_w_out=d_w_attn_out)
    small = dict(
        norm_g=jnp.concatenate([d_g0[0], d_g1[0]], axis=0), ssm_d=d_d[0], ssm_b_glu=d_b_glu[0],
        attn_q_norm=d_qn, attn_k_norm=d_kn, final_norm_g=d_gf[0, 0], **_s5_param_grads(raw, s5_bwd))
    zero_v = jnp.zeros((D_MODEL,), F32)
    d_mod_lat = jnp.stack([jnp.concatenate([d_shift0[0, 0], d_scale0[0, 0], d_gate0[0, 0]]),
                           jnp.concatenate([d_shift1[0, 0], d_scale1[0, 0], d_gate1[0, 0]])])
    d_mod_ctx = jnp.stack([jnp.concatenate([d_shift0[1, 0], d_scale0[1, 0], d_gate0[1, 0]]),
                           jnp.concatenate([d_shift1[1, 0], d_scale1[1, 0], zero_v])])
    return sq[0, 0, 0], dx0[:t], big, small, d_mod_lat, d_mod_ctx


def _adamw(name, w, g, m, v):
    rows, cols = w.shape
    tr = _largest_tile(rows, 256, 8)
    c1 = 1.0 / (1.0 - ADAM_B1 ** ADAM_STEP)
    c2 = 1.0 / (1.0 - ADAM_B2 ** ADAM_STEP)

    def fn(wt, gt, mt, vt):
        mn = ADAM_B1 * mt + (1.0 - ADAM_B1) * gt
        vn = ADAM_B2 * vt + (1.0 - ADAM_B2) * (gt * gt)
        delta = -ADAM_LR * ((mn * c1) / (jnp.sqrt(vn * c2) + ADAM_EPS) + ADAM_WD * wt)
        return delta, mn, vn

    return _rowwise(name, fn, rows, tr, [(a, 0, cols) for a in (w, g, m, v)], [], [(cols, F32)] * 3, [])


BIG = ("ssm_w_in", "ssm_w_glu", "ssm_w_out", "attn_w_in", "attn_w_out")
COL_SHARDED = ("ssm_w_in", "attn_w_in")
WEIGHTS = ("c_ctx", "w_mod", "b_mod", "norm_g", "ssm_w_in", "ssm_a_re", "ssm_a_im", "ssm_log_dt", "ssm_b_re", "ssm_b_im",
           "ssm_c_re", "ssm_c_im", "ssm_d", "ssm_w_glu", "ssm_b_glu", "ssm_w_out", "attn_w_in", "attn_q_norm",
           "attn_k_norm", "attn_w_out", "final_norm_g")
SMALL = tuple(k for k in WEIGHTS if k not in BIG and k != "w_mod")
PACK_W = 1024


def _attn_in_perm(x, inverse):
    a, kv = ATTN_W, 2 * KV_W
    if inverse:
        return jnp.concatenate([x[..., :a], x[..., 2 * a:], x[..., a:2 * a]], axis=-1)
    return jnp.concatenate([x[..., :a], x[..., a + kv:], x[..., a:a + kv]], axis=-1)


def _pack(arrays, dtype, row_unit):
    flat = jnp.concatenate([a.reshape(-1).astype(dtype) for a in arrays])
    rows = -(-flat.shape[0] // PACK_W)
    rows = -(-rows // row_unit) * row_unit
    flat = jnp.concatenate([flat, jnp.zeros((rows * PACK_W - flat.shape[0],), dtype)])
    return flat.reshape(rows, PACK_W)


def _unpack(buf, shapes):
    lead = buf.shape[:-2]
    flat = buf.reshape(lead + (-1,))
    out, off = [], 0
    for shp in shapes:
        size = math.prod(shp)
        out.append(flat[..., off:off + size].reshape(lead + tuple(shp)))
        off += size
    return out


def _half_shape(name, shard_shape):
    r, ccols = shard_shape
    return (r // 2, ccols)


def kernel(x, c, ctx, c_ctx, w_mod, b_mod, norm_g, ssm_w_in, ssm_a_re, ssm_a_im, ssm_log_dt, ssm_b_re, ssm_b_im, ssm_c_re, ssm_c_im, ssm_d, ssm_w_glu, ssm_b_glu, ssm_w_out, attn_w_in, attn_q_norm, attn_k_norm, attn_w_out, final_norm_g, loss_target, m_c_ctx, m_w_mod, m_b_mod, m_norm_g, m_ssm_w_in, m_ssm_a_re, m_ssm_a_im, m_ssm_log_dt, m_ssm_b_re, m_ssm_b_im, m_ssm_c_re, m_ssm_c_im, m_ssm_d, m_ssm_w_glu, m_ssm_b_glu, m_ssm_w_out, m_attn_w_in, m_attn_q_norm, m_attn_k_norm, m_attn_w_out, m_final_norm_g, v_c_ctx, v_w_mod, v_b_mod, v_norm_g, v_ssm_w_in, v_ssm_a_re, v_ssm_a_im, v_ssm_log_dt, v_ssm_b_re, v_ssm_b_im, v_ssm_c_re, v_ssm_c_im, v_ssm_d, v_ssm_w_glu, v_ssm_b_glu, v_ssm_w_out, v_attn_w_in, v_attn_q_norm, v_attn_k_norm, v_attn_w_out, v_final_norm_g):
    args = dict(locals())
    wts = {k: args[k] for k in WEIGHTS}
    mom_m = {k: args["m_" + k] for k in WEIGHTS}
    mom_v = {k: args["v_" + k] for k in WEIGHTS}
    mx, my, mc = lax.axis_index("x"), lax.axis_index("y"), lax.axis_index("c")
    chip = 2 * mx + my
    me = 2 * chip + mc

    halves = []
    for k in BIG:
        sh = wts[k][0]
        hr = sh.shape[0] // 2
        halves.append(lax.dynamic_slice_in_dim(sh, mc * hr, hr, axis=0))
    gathered = _exchange("gather_weights", _pack(halves, BF16, 16), True)
    parts = _unpack(gathered, [h.shape for h in halves])
    w_full = {}
    for k, pc in zip(BIG, parts):
        hr, cols = pc.shape[1:]
        pc = pc.reshape(N_CHIP, 2, hr, cols)
        if k in COL_SHARDED:
            w_full[k] = pc.transpose(1, 2, 0, 3).reshape(2 * hr, N_CHIP * cols)
        else:
            w_full[k] = pc.reshape(N_CHIP * 2 * hr, cols)
    w_full["attn_w_in"] = _attn_in_perm(w_full["attn_w_in"], False)

    c_blk = jnp.concatenate([c, jnp.zeros((7, D_MODEL), F32)], axis=0)
    c_all = _exchange("gather_c", c_blk, True)[:, 0]
    cond = jnp.concatenate([c_all, c_ctx[None], jnp.zeros((7, D_MODEL), F32)], axis=0)
    s_cond, ds_cond = _rowwise("cond_silu", lambda t: (_silu(t), _silu_grad(t)), 16, 16, [(cond, 0, D_MODEL)], [],
                               [(D_MODEL, F32), (D_MODEL, F32)], [])
    w_mod_b = w_mod.astype(BF16)
    mcols = w_mod.shape[2]
    mod_part = jnp.stack([_mm(f"mod{i}", s_cond, w_mod_b[i], "nn") for i in range(2)])
    mod_g = _exchange("gather_mod", mod_part.reshape(32, mcols), True)
    mod_all = mod_g.reshape(N_CHIP, 2, 2, 16, mcols)[:, 0]
    mod_all = mod_all.transpose(1, 2, 0, 3).reshape(2, 16, N_CHIP * mcols) + b_mod[:, None, :]
    mods = []
    for i in range(2):
        lat = lax.dynamic_slice_in_dim(mod_all[i], me, 1, axis=0)[0]
        both = jnp.stack([lat, mod_all[i, 8]])
        mods.append((both[:, :D_MODEL], both[:, D_MODEL:2 * D_MODEL], both[:, 2 * D_MODEL:]))

    small_p = {k: wts[k] for k in SMALL if k != "c_ctx" and k != "b_mod"}
    sq, grad_x, big_g, small_g, d_mod_lat, d_mod_ctx = _example_step(x[0], ctx[0], loss_target[0], mods, w_full, small_p)
    loss = lax.psum(0.5 / D_MODEL * sq, ("x", "y", "c"))
    big_g["attn_w_in"] = _attn_in_perm(big_g["attn_w_in"], True)

    small_names = [k for k in SMALL if k not in ("c_ctx", "b_mod")]
    small_list = [small_g[k] for k in small_names] + [d_mod_lat, d_mod_ctx]
    small_shapes = [wts[k].shape for k in small_names] + [d_mod_lat.shape, d_mod_ctx.shape]
    sg = _exchange("gather_small", _pack(small_list, F32, 8), True)
    sg_sum = _sum_slots("sum_small", sg)
    summed = _unpack(sg_sum, small_shapes)
    grads = dict(zip(small_names, summed[:-2]))
    d_mod_lat_sum, d_mod_ctx_sum = summed[-2], summed[-1]
    grads["b_mod"] = d_mod_lat_sum + d_mod_ctx_sum
    d_mod_lat_all = _unpack(sg, small_shapes)[-2]

    g_w_mod, ds_cc = [], []
    for i in range(2):
        rows9 = jnp.concatenate([d_mod_lat_all[:, i], d_mod_ctx_sum[i][None], jnp.zeros((7, 3 * D_MODEL), F32)], axis=0)
        mine = lax.dynamic_slice_in_dim(rows9, chip * mcols, mcols, axis=1)
        g_w_mod.append(_mm(f"mod{i}_dw", s_cond, mine, "tn"))
        ds_cc.append(_mm(f"mod{i}_dx", mine, w_mod_b[i], "nt")[8])
    grads["w_mod"] = jnp.stack(g_w_mod)
    part = (ds_cc[0] + ds_cc[1]) * jnp.where(mc == 0, 1.0, 0.0)
    part_blk = jnp.concatenate([part[None], jnp.zeros((7, D_MODEL), F32)], axis=0)
    ds_all = _sum_slots("sum_c_ctx", _exchange("gather_c_ctx", part_blk, True))
    grads["c_ctx"] = ds_all[0] * ds_cond[8]

    blocks = []
    for k in BIG:
        g = big_g[k]
        rows, cols = g.shape
        if k in COL_SHARDED:
            blocks.append(g.reshape(2, rows // 2, N_CHIP, cols // N_CHIP).transpose(2, 0, 1, 3).reshape(N_DEV, -1))
        else:
            blocks.append(g.reshape(N_DEV, -1))
    sendbuf = jnp.concatenate(blocks, axis=1).astype(BF16)
    sendbuf = sendbuf.reshape(N_DEV, -1, PACK_W)
    recv = _exchange("scatter_big", sendbuf, False)
    mine = _sum_slots("sum_big", recv)
    both = _exchange("swap_halves", mine, True, sibling_only=True)
    half_shapes = [(wts[k].shape[1] // 2, wts[k].shape[2]) for k in BIG]
    for k, pc in zip(BIG, _unpack(both, half_shapes)):
        grads[k] = pc.reshape(wts[k].shape)

    delta, new_m, new_v = {}, {}, {}
    for k in BIG + ("w_mod",):
        shp = wts[k].shape
        two_d = (-1, shp[-1])
        res = _adamw("adamw_" + k, *[a.reshape(two_d) for a in (wts[k], grads[k], mom_m[k], mom_v[k])])
        delta[k], new_m[k], new_v[k] = [r.reshape(shp) for r in res]
    shapes = [wts[k].shape for k in SMALL]
    packed = [_pack([d[k] for k in SMALL], F32, 8) for d in (wts, grads, mom_m, mom_v)]
    res = _adamw("adamw_small", *packed)
    for dst, buf in zip((delta, new_m, new_v), res):
        for k, a in zip(SMALL, _unpack(buf, shapes)):
            dst[k] = a
    grads = {k: grads[k].reshape(wts[k].shape) for k in WEIGHTS}
    return (loss, grad_x[None], *[grads[k] for k in WEIGHTS], *[delta[k] for k in WEIGHTS],
            *[new_m[k] for k in WEIGHTS], *[new_v[k] for k in WEIGHTS])
```

```python
import functools
import math

import jax
import jax.numpy as jnp
from jax import lax
from jax.experimental import pallas as pl
from jax.experimental.pallas import tpu as pltpu

F32 = jnp.float32
BF16 = jnp.bfloat16

D_MODEL = 1024
NORM_EPS = 1e-6
SSM_GROUPS = 64
SSM_GROUP = 16
SSM_STATE = 64
LANES = 128
SLAB_W = LANES
N_SLAB = D_MODEL // SLAB_W
SLAB_GROUPS = SLAB_W // SSM_GROUP
HALF_W = SLAB_GROUPS * SSM_STATE
STATE_W = 2 * HALF_W
N_SEG = 8
HEAD_DIM = 64
N_Q_HEADS = 16
N_KV_HEADS = 4
KV_REP = N_Q_HEADS // N_KV_HEADS
ATTN_W = N_Q_HEADS * HEAD_DIM
KV_W = N_KV_HEADS * HEAD_DIM
GRID_W = 64
ROPE_THETA = 10000.0
N_DEV = 8
N_CHIP = 4
VMEM_LIMIT_BYTES = 56 * 1024 * 1024

ADAM_LR = 0.001
ADAM_B1 = 0.9
ADAM_B2 = 0.999
ADAM_EPS = 1e-08
ADAM_WD = 0.01
ADAM_STEP = 10


def _params(*sem):
    return pltpu.CompilerParams(dimension_semantics=sem, vmem_limit_bytes=VMEM_LIMIT_BYTES)


def _largest_tile(n, cap, unit):
    if n <= cap:
        return n
    t = (cap // unit) * unit
    while t >= unit:
        if n % t == 0:
            return t
        t -= unit
    raise ValueError(f"no tile for {n} (cap {cap}, unit {unit})")


def _rowwise(name, fn, n_rows, tr, row_ins, vec_ins, row_outs, red_outs, n_lat=None, want_flag=False):
    nt = n_rows // tr
    assert nt * tr == n_rows
    nlt = nt if n_lat is None else n_lat // tr

    def sel(i):
        return jnp.where(i >= nlt, 1, 0)

    arrays, in_specs = [], []
    for spec in row_ins:
        arr, cb, w = spec[:3]
        kind = spec[3] if len(spec) > 3 else None
        m = spec[4] if len(spec) > 4 else None
        if kind == "mod":
            imap = functools.partial(lambda i, cb, m: (i % m, cb), cb=cb, m=m)
        elif kind == "clamp":
            imap = functools.partial(lambda i, cb, m: (jnp.minimum(i, m - 1), cb), cb=cb, m=m)
        else:
            imap = functools.partial(lambda i, cb: (i, cb), cb=cb)
        arrays.append(arr)
        in_specs.append(pl.BlockSpec((tr, w), imap))
    for v in vec_ins:
        s, a, w = v.shape
        imap = (lambda i: (sel(i), 0, 0)) if s == 2 else (lambda i: (0, 0, 0))
        arrays.append(v)
        in_specs.append(pl.BlockSpec((1, a, w), imap))
    out_shapes, out_specs = [], []
    for w, dt in row_outs:
        out_shapes.append(jax.ShapeDtypeStruct((n_rows, w), dt))
        out_specs.append(pl.BlockSpec((tr, w), lambda i: (i, 0)))
    for s, w in red_outs:
        out_shapes.append(jax.ShapeDtypeStruct((s, 1, w), F32))
        imap = (lambda i: (sel(i), 0, 0)) if s == 2 else (lambda i: (0, 0, 0))
        out_specs.append(pl.BlockSpec((1, 1, w), imap))
    n_ri, n_vi, n_ro, n_rd = len(row_ins), len(vec_ins), len(row_outs), len(red_outs)

    def body(*refs):
        i = pl.program_id(0)
        rows = [r[...] for r in refs[:n_ri]]
        vecs = [r[0] for r in refs[n_ri:n_ri + n_vi]]
        outs = refs[n_ri + n_vi:]
        lead = [jnp.where(i < nlt, 1.0, 0.0).astype(F32)] if want_flag else []
        res = fn(*lead, *rows, *vecs)
        if not isinstance(res, (tuple, list)):
            res = (res,)
        assert len(res) == n_ro + n_rd
        for k in range(n_ro):
            outs[k][...] = res[k].astype(outs[k].dtype)
        for k in range(n_rd):
            part = jnp.sum(res[n_ro + k].astype(F32), axis=0, keepdims=True)
            first = i == 0
            if red_outs[k][0] == 2:
                first = jnp.logical_or(first, i == nlt)
            o = outs[n_ro + k]

            @pl.when(first)
            def _():
                o[0] = part

            @pl.when(jnp.logical_not(first))
            def _():
                o[0] = o[0] + part

    res = pl.pallas_call(
        body, grid=(nt,), in_specs=in_specs, out_specs=out_specs, out_shape=out_shapes,
        compiler_params=_params("arbitrary"), name=name)(*arrays)
    return res


def _vec(v):
    v = v.astype(F32)
    if v.ndim == 1:
        v = v[None]
    return v[:, None, :]


def _mm(name, a, b, mode, out_dtype=F32):
    if mode in ("nn", "nt"):
        m, k = a.shape
        n = b.shape[1] if mode == "nn" else b.shape[0]
        tm = _largest_tile(m, 1024, 8)
        tn = _largest_tile(n, 1024, 128)
        contract = (((1,), (0,)), ((), ())) if mode == "nn" else (((1,), (1,)), ((), ()))

        def body(a_ref, b_ref, o_ref):
            o_ref[...] = lax.dot_general(a_ref[...].astype(BF16), b_ref[...].astype(BF16), contract,
                                         preferred_element_type=F32).astype(o_ref.dtype)

        b_spec = pl.BlockSpec((k, tn), lambda i, j: (0, j)) if mode == "nn" else pl.BlockSpec((tn, k), lambda i, j: (j, 0))
        return pl.pallas_call(
            body, grid=(m // tm, n // tn),
            in_specs=[pl.BlockSpec((tm, k), lambda i, j: (i, 0)), b_spec],
            out_specs=pl.BlockSpec((tm, tn), lambda i, j: (i, j)),
            out_shape=jax.ShapeDtypeStruct((m, n), out_dtype),
            compiler_params=_params("parallel", "arbitrary"), name=name)(a, b)
    assert mode == "tn"
    r, k1 = a.shape
    k2 = b.shape[1]
    tr = _largest_tile(r, 1024, 8)
    t2 = _largest_tile(k2, 1024, 128)
    nr = r // tr

    def body(a_ref, b_ref, o_ref):
        part = lax.dot_general(a_ref[...].astype(BF16), b_ref[...].astype(BF16), (((0,), (0,)), ((), ())),
                               preferred_element_type=F32)
        i = pl.program_id(1)

        @pl.when(i == 0)
        def _():
            o_ref[...] = part

        @pl.when(i > 0)
        def _():
            o_ref[...] += part

    return pl.pallas_call(
        body, grid=(k2 // t2, nr),
        in_specs=[pl.BlockSpec((tr, k1), lambda j, i: (i, 0)), pl.BlockSpec((tr, t2), lambda j, i: (i, j))],
        out_specs=pl.BlockSpec((k1, t2), lambda j, i: (0, j)),
        out_shape=jax.ShapeDtypeStruct((k1, k2), F32),
        compiler_params=_params("parallel", "arbitrary"), name=name)(a, b)


def _exchange(name, x, bcast, sibling_only=False):
    rels = [1] if sibling_only else list(range(1, N_DEV))
    n_slot = 2 if sibling_only else N_DEV
    blk = x.shape if bcast else x.shape[1:]

    def body(x_ref, o_ref, send_sems, recv_sems, local_sem):
        mx, my, mc = lax.axis_index("x"), lax.axis_index("y"), lax.axis_index("c")
        me = mc if sibling_only else 4 * mx + 2 * my + mc
        me_dev = 4 * mx + 2 * my + mc
        mine = pltpu.make_async_copy(x_ref if bcast else x_ref.at[me_dev], o_ref.at[me], local_sem)
        mine.start()
        copies = []
        for k, r in enumerate(rels):
            px = 1 - mx if (r >> 2) & 1 else mx
            py = 1 - my if (r >> 1) & 1 else my
            pc = 1 - mc if r & 1 else mc
            src = x_ref if bcast else x_ref.at[4 * px + 2 * py + pc]
            cp = pltpu.make_async_remote_copy(
                src_ref=src, dst_ref=o_ref.at[me], send_sem=send_sems.at[k], recv_sem=recv_sems.at[k],
                device_id=(px, py, pc), device_id_type=pl.DeviceIdType.MESH)
            cp.start()
            copies.append(cp)
        for cp in copies:
            cp.wait()
        mine.wait()

    return pl.pallas_call(
        body, out_shape=jax.ShapeDtypeStruct((n_slot,) + tuple(blk), x.dtype),
        in_specs=[pl.BlockSpec(memory_space=pl.ANY)], out_specs=pl.BlockSpec(memory_space=pl.ANY),
        scratch_shapes=[pltpu.SemaphoreType.DMA((len(rels),)), pltpu.SemaphoreType.DMA((len(rels),)),
                        pltpu.SemaphoreType.DMA],
        name=name)(x)


def _gather_two_level(name, x):
    def body(x_ref, o_ref, send_sems, recv_sems, local_sem):
        mx, my, mc = lax.axis_index("x"), lax.axis_index("y"), lax.axis_index("c")
        me, sibling = (mx, my, mc), (mx, my, 1 - mc)
        chips = [(1 - mx, my), (mx, 1 - my), (1 - mx, 1 - my)]

        def slot(px, py, pc):
            return o_ref.at[4 * px + 2 * py + pc]

        def copy(k, block, to, src=None):
            return pltpu.make_async_remote_copy(
                src_ref=slot(*block) if src is None else src, dst_ref=slot(*block), send_sem=send_sems.at[k],
                recv_sem=recv_sems.at[k], device_id=to, device_id_type=pl.DeviceIdType.MESH)

        mine = pltpu.make_async_copy(x_ref, slot(*me), local_sem)
        mine.start()
        first = [copy(0, me, sibling, src=x_ref)]
        first += [copy(1 + j, me, (*chip, mc), src=x_ref) for j, chip in enumerate(chips)]
        for cp in first:
            cp.start()
        passed = [copy(4 + j, (*chip, mc), sibling) for j, chip in enumerate(chips)]
        for j, chip in enumerate(chips):
            copy(1 + j, (*chip, mc), me).wait_recv()
            passed[j].start()
        copy(0, sibling, me).wait_recv()
        for j, chip in enumerate(chips):
            copy(4 + j, (*chip, 1 - mc), me).wait_recv()
        for cp in first + passed:
            cp.wait_send()
        mine.wait()

    return pl.pallas_call(
        body, out_shape=jax.ShapeDtypeStruct((N_DEV,) + tuple(x.shape), x.dtype),
        in_specs=[pl.BlockSpec(memory_space=pl.ANY)], out_specs=pl.BlockSpec(memory_space=pl.ANY),
        scratch_shapes=[pltpu.SemaphoreType.DMA((N_DEV - 1,)), pltpu.SemaphoreType.DMA((N_DEV - 1,)),
                        pltpu.SemaphoreType.DMA],
        name=name)(x)


def _sum_slots(name, x):
    s, r, w = x.shape
    tr = _largest_tile(r, 256, 8)

    def body(x_ref, o_ref):
        acc = x_ref[0].astype(F32)
        for j in range(1, s):
            acc = acc + x_ref[j].astype(F32)
        o_ref[...] = acc

    return pl.pallas_call(
        body, grid=(r // tr,), in_specs=[pl.BlockSpec((s, tr, w), lambda i: (0, i, 0))],
        out_specs=pl.BlockSpec((tr, w), lambda i: (i, 0)), out_shape=jax.ShapeDtypeStruct((r, w), F32),
        compiler_params=_params("parallel"), name=name)(x)


def _sigmoid(x):
    return 1.0 / (1.0 + jnp.exp(-x))


def _silu(x):
    return x * _sigmoid(x)


def _silu_grad(x):
    s = _sigmoid(x)
    return s * (1.0 + x * (1.0 - s))


_INV_SQRT2 = 1.0 / math.sqrt(2.0)
_INV_SQRT2PI = 1.0 / math.sqrt(2.0 * math.pi)


def _gelu(x):
    return 0.5 * x * (1.0 + lax.erf(x * _INV_SQRT2))


def _gelu_grad(x):
    return 0.5 * (1.0 + lax.erf(x * _INV_SQRT2)) + x * jnp.exp(-0.5 * x * x) * _INV_SQRT2PI


def _rms_hat(x):
    r = lax.rsqrt(jnp.mean(x * x, axis=-1, keepdims=True) + NORM_EPS)
    return x * r, r


def _rms_bwd(xh, r, dxh):
    return r * (dxh - xh * jnp.mean(dxh * xh, axis=-1, keepdims=True))


def _norm_mod_fwd(name, x, g, scale, shift, n_rows, tr, n_lat):
    def fn(xt, gv, sc, sh):
        xh, _ = _rms_hat(xt)
        return (xh * gv) * (1.0 + sc) + sh

    return _rowwise(name, fn, n_rows, tr, [(x, 0, D_MODEL)], [g, scale, shift], [(D_MODEL, F32)], [], n_lat=n_lat)[0]


def _norm_mod_bwd(name, x, g, scale, dh, dres, n_rows, tr, n_lat, prev=None):
    nlt = n_lat // tr

    def fn(flag, xt, dht, drt, *rest):
        gv, sc = rest[-2:] if prev is None else rest[1:3]
        xh, r = _rms_hat(xt)
        n = xh * gv
        dn = dht * (1.0 + sc)
        dx = _rms_bwd(xh, r, dn * gv) + flag * drt
        if prev is None:
            return dx, dn * xh, dht * n, dht
        return dx, rest[3] * dx, dn * xh, dht * n, dht, dx * rest[0]

    rows = [(x, 0, D_MODEL), (dh, 0, D_MODEL), (dres, 0, D_MODEL, "clamp", nlt)]
    vecs, row_outs, reds = [g, scale], [(D_MODEL, F32)], [(1, D_MODEL), (2, D_MODEL), (2, D_MODEL)]
    if prev is not None:
        rows.append((prev[0], 0, D_MODEL))
        vecs.append(prev[1])
        row_outs.append((D_MODEL, F32))
        reds.append((2, D_MODEL))
    return _rowwise(name, fn, n_rows, tr, rows, vecs, row_outs, reds, n_lat=n_lat, want_flag=True)


def _s5_prep(a_re, a_im, log_dt, b_re, b_im, seg_lat, seg_ctx):
    def body(ar_ref, ai_ref, ld_ref, br_ref, bi_ref, abr_ref, abi_ref, bbr_ref, bbi_ref, alr_ref, ali_ref, acr_ref,
             aci_ref):
        lr, li = ar_ref[...], ai_ref[...]
        dt = jnp.exp(ld_ref[...])
        ldr, ldi = lr * dt, li * dt
        e = jnp.exp(ldr)
        abr, abi = e * jnp.cos(ldi), e * jnp.sin(ldi)
        abr_ref[...] = abr
        abi_ref[...] = abi
        den = lr * lr + li * li
        nr, ni = abr - 1.0, abi
        qr = (nr * lr + ni * li) / den
        qi = (ni * lr - nr * li) / den
        br, bi = br_ref[...], bi_ref[...]
        bbr_ref[...] = qr[None] * br - qi[None] * bi
        bbi_ref[...] = qr[None] * bi + qi[None] * br
        for seg, r_ref, i_ref in ((seg_lat, alr_ref, ali_ref), (seg_ctx, acr_ref, aci_ref)):
            es = jnp.exp(ldr * float(seg))
            r_ref[...] = es * jnp.cos(ldi * float(seg))
            i_ref[...] = es * jnp.sin(ldi * float(seg))

    sm = jax.ShapeDtypeStruct(a_re.shape, F32)
    big = jax.ShapeDtypeStruct(b_re.shape, F32)
    return pl.pallas_call(body, out_shape=[sm, sm, big, big, sm, sm, sm, sm], name="s5_prep")(
        a_re, a_im, log_dt, b_re, b_im)


def _s5_prep_bwd(a_re, a_im, log_dt, b_re, b_im, dabr, dabi, dbbr, dbbi):
    def body(ar_ref, ai_ref, ld_ref, br_ref, bi_ref, dabr_ref, dabi_ref, dbbr_ref, dbbi_ref,
             dar_ref, dai_ref, dld_ref, dbr_ref, dbi_ref):
        lr, li = ar_ref[...], ai_ref[...]
        dt = jnp.exp(ld_ref[...])
        ldr, ldi = lr * dt, li * dt
        e = jnp.exp(ldr)
        abr, abi = e * jnp.cos(ldi), e * jnp.sin(ldi)
        den = lr * lr + li * li
        nr, ni = abr - 1.0, abi
        qr = (nr * lr + ni * li) / den
        qi = (ni * lr - nr * li) / den
        br, bi = br_ref[...], bi_ref[...]
        gbr, gbi = dbbr_ref[...], dbbi_ref[...]
        dbr_ref[...] = gbr * qr[None] + gbi * qi[None]
        dbi_ref[...] = gbi * qr[None] - gbr * qi[None]
        dqr = jnp.sum(gbr * br + gbi * bi, axis=0)
        dqi = jnp.sum(gbi * br - gbr * bi, axis=0)
        dnr = (dqr * lr - dqi * li) / den
        dni = (dqr * li + dqi * lr) / den
        dlr_q = (dqr * (nr - 2.0 * lr * qr) + dqi * (ni - 2.0 * lr * qi)) / den
        dli_q = (dqr * (ni - 2.0 * li * qr) + dqi * (-nr - 2.0 * li * qi)) / den
        gar = dabr_ref[...] + dnr
        gai = dabi_ref[...] + dni
        dldr = gar * abr + gai * abi
        dldi = gai * abr - gar * abi
        dar_ref[...] = dldr * dt + dlr_q
        dai_ref[...] = dldi * dt + dli_q
        ddt = jnp.sum(dldr * lr + dldi * li, axis=1, keepdims=True)
        dld_ref[...] = ddt * dt

    sm = jax.ShapeDtypeStruct(a_re.shape, F32)
    big = jax.ShapeDtypeStruct(b_re.shape, F32)
    return pl.pallas_call(body, out_shape=[sm, sm, jax.ShapeDtypeStruct(log_dt.shape, F32), big, big],
                          name="s5_prep_bwd")(a_re, a_im, log_dt, b_re, b_im, dabr, dabi, dbbr, dbbi)


def _slab_cols(v):
    return v.reshape(N_SLAB, 1, HALF_W)


def _slab_pair(vr, vi):
    return jnp.concatenate([_slab_cols(vr), _slab_cols(vi)], axis=-1)


def _slab_in_matrix(bbr, bbi):
    eye = jnp.eye(SLAB_GROUPS, dtype=F32)

    def one(b):
        b = b.reshape(N_SLAB, SLAB_GROUPS, SSM_STATE, SSM_GROUP)
        m = jnp.einsum("sgph,gk->sghkp", b, eye)
        return m.reshape(N_SLAB, SLAB_W, HALF_W)

    return jnp.concatenate([one(bbr), one(bbi)], axis=-1)


def _slab_out_matrix(cr, ci):
    eye = jnp.eye(SLAB_GROUPS, dtype=F32)

    def one(c):
        c = c.reshape(N_SLAB, SLAB_GROUPS, SSM_GROUP, SSM_STATE)
        m = jnp.einsum("sghp,gk->skpgh", c, eye)
        return m.reshape(N_SLAB, HALF_W, SLAB_W)

    return jnp.concatenate([one(cr), one(-ci)], axis=1)


def _slab_diag(m):
    m = m.reshape(N_SLAB, SLAB_GROUPS, SSM_GROUP, 2, SLAB_GROUPS, SSM_STATE)
    d = jnp.stack([m[:, g, :, :, g, :] for g in range(SLAB_GROUPS)], axis=1)
    return d.transpose(3, 0, 1, 2, 4).reshape(2, SSM_GROUPS, SSM_GROUP, SSM_STATE)


def _cmul(ar, ai, xr, xi, conj):
    if conj:
        return ar * xr + ai * xi, ar * xi - ai * xr
    return ar * xr - ai * xi, ar * xi + ai * xr


def _s5_carry(name, z, a_seg, init, descending, conj):
    order = list(range(N_SEG - 1, -1, -1)) if descending else list(range(N_SEG))

    def body(z_ref, a_ref, i_ref, e_ref, o_ref):
        ar, ai = a_ref[:, :HALF_W], a_ref[:, HALF_W:]
        cr, ci = i_ref[:, :HALF_W], i_ref[:, HALF_W:]
        for j in order:
            e_ref[j, :, :HALF_W] = cr
            e_ref[j, :, HALF_W:] = ci
            pr, pi = _cmul(ar, ai, cr, ci, conj)
            cr = pr + z_ref[j, :, :HALF_W]
            ci = pi + z_ref[j, :, HALF_W:]
        o_ref[:, :HALF_W] = cr
        o_ref[:, HALF_W:] = ci

    return pl.pallas_call(body, out_shape=[jax.ShapeDtypeStruct(z.shape, F32), jax.ShapeDtypeStruct(init.shape, F32)],
                          name=name)(z, a_seg, init)


def _seg_major(v):
    return jnp.transpose(v, (1, 0, 2))


def _s5_scan(name, u, n_rows, row0, b_mat, c_mat, abar, h_in, descending, full, y_alias=None, y_rows=None):
    seg = n_rows // N_SEG
    ta = min(32, seg)
    nk = seg // ta
    assert seg * N_SEG == n_rows and nk * ta == seg and row0 % n_rows == 0 and ta % 8 == 0
    rb = row0 // n_rows
    tile = ta * N_SEG

    def body(*refs):
        if full:
            if y_alias is not None:
                u_ref, b_ref, c_ref, a_ref, hin_ref, _, hfin_ref, y_ref, hch_ref, st_ref, up_ref, h_ref = refs
            else:
                u_ref, b_ref, c_ref, a_ref, hin_ref, hfin_ref, y_ref, hch_ref, st_ref, up_ref, h_ref = refs
        else:
            u_ref, b_ref, a_ref, hin_ref, hfin_ref, st_ref, up_ref, h_ref = refs
        k = pl.program_id(1)
        kk = nk - 1 - k if descending else k
        a0 = kk * ta

        @pl.when(k == 0)
        def _():
            st_ref[...] = hin_ref[0]

        if full:
            hch_ref[0, 0] = st_ref[...]
        for al in range(ta):
            up_ref[al * N_SEG:(al + 1) * N_SEG, :] = u_ref[pl.ds(a0 + al, N_SEG, stride=seg), :]
        h_ref[...] = jnp.dot(up_ref[...].astype(BF16), b_ref[0], preferred_element_type=F32)
        ar = jnp.broadcast_to(a_ref[0, :, :HALF_W], (N_SEG, HALF_W))
        ai = jnp.broadcast_to(a_ref[0, :, HALF_W:], (N_SEG, HALF_W))

        def step(i, carry):
            hr, hi = carry
            al = ta - 1 - i if descending else i
            row = pl.multiple_of(al * N_SEG, N_SEG)
            pr, pi = _cmul(ar, ai, hr, hi, False)
            hr = pr + h_ref[pl.ds(row, N_SEG), :HALF_W]
            hi = pi + h_ref[pl.ds(row, N_SEG), HALF_W:]
            if full:
                h_ref[pl.ds(row, N_SEG), :HALF_W] = hr
                h_ref[pl.ds(row, N_SEG), HALF_W:] = hi
            return hr, hi

        hr, hi = lax.fori_loop(0, ta, step, (st_ref[:, :HALF_W], st_ref[:, HALF_W:]))
        st_ref[:, :HALF_W] = hr
        st_ref[:, HALF_W:] = hi
        if full:
            yt = jnp.dot(h_ref[...].astype(BF16), c_ref[0], preferred_element_type=F32)
            for al in range(ta):
                y_ref[pl.ds(a0 + al, N_SEG, stride=seg), :] = yt[al * N_SEG:(al + 1) * N_SEG, :]

        @pl.when(k == nk - 1)
        def _():
            hfin_ref[0] = st_ref[...]

    u_spec = pl.BlockSpec((n_rows, SLAB_W), lambda s, k: (rb, s))
    b_spec = pl.BlockSpec((1, SLAB_W, STATE_W), lambda s, k: (s, 0, 0))
    c_spec = pl.BlockSpec((1, STATE_W, SLAB_W), lambda s, k: (s, 0, 0))
    a_spec = pl.BlockSpec((1, 1, STATE_W), lambda s, k: (s, 0, 0))
    st_spec = pl.BlockSpec((1, N_SEG, STATE_W), lambda s, k: (s, 0, 0))
    st_shape = jax.ShapeDtypeStruct((N_SLAB, N_SEG, STATE_W), F32)
    scratch = [pltpu.VMEM((N_SEG, STATE_W), F32), pltpu.VMEM((tile, SLAB_W), F32), pltpu.VMEM((tile, STATE_W), F32)]
    if not full:
        return pl.pallas_call(
            body, grid=(N_SLAB, nk), in_specs=[u_spec, b_spec, a_spec, st_spec], out_specs=st_spec, out_shape=st_shape,
            scratch_shapes=scratch, compiler_params=_params("parallel", "arbitrary"), name=name)(u, b_mat, abar, h_in)
    kmap = (lambda s, k: (s, nk - 1 - k, 0, 0)) if descending else (lambda s, k: (s, k, 0, 0))
    out_specs = [st_spec, u_spec, pl.BlockSpec((1, 1, N_SEG, STATE_W), kmap)]
    out_shape = [st_shape, jax.ShapeDtypeStruct((y_rows, D_MODEL), F32),
                 jax.ShapeDtypeStruct((N_SLAB, nk, N_SEG, STATE_W), F32)]
    in_specs = [u_spec, b_spec, c_spec, a_spec, st_spec]
    args = [u, b_mat, c_mat, abar, h_in]
    aliases = {}
    if y_alias is not None:
        in_specs.append(pl.BlockSpec(memory_space=pl.ANY))
        args.append(y_alias)
        aliases = {5: 1}
    return pl.pallas_call(
        body, grid=(N_SLAB, nk), in_specs=in_specs, out_specs=out_specs, out_shape=out_shape, scratch_shapes=scratch,
        input_output_aliases=aliases, compiler_params=_params("parallel", "arbitrary"), name=name)(*args)


def _s5_scan_bwd(name, u, dy, n_rows, row0, b_mat, bt_mat, ct_mat, abar, h_chunks, g_in, descending, full,
                 du_alias=None, du_rows=None):
    seg = n_rows // N_SEG
    ta = min(32, seg)
    nk = seg // ta
    rb = row0 // n_rows
    tile = ta * N_SEG
    g_desc = not descending

    def body(*refs):
        if full:
            (u_ref, dy_ref, b_ref, bt_ref, ct_ref, a_ref, hch_ref, gin_ref) = refs[:8]
            rest = refs[9:] if du_alias is not None else refs[8:]
            gfin_ref, du_ref, db_ref, dc_ref, da_ref, st_ref, up_ref, dyp_ref, h_ref, g_ref = rest
        else:
            dy_ref, ct_ref, a_ref, gin_ref, gfin_ref, st_ref, dyp_ref, g_ref = refs
        k = pl.program_id(1)
        kk = nk - 1 - k if g_desc else k
        a0 = kk * ta
        ar = jnp.broadcast_to(a_ref[0, :, :HALF_W], (N_SEG, HALF_W))
        ai = jnp.broadcast_to(a_ref[0, :, HALF_W:], (N_SEG, HALF_W))

        @pl.when(k == 0)
        def _():
            st_ref[...] = gin_ref[0]

        for al in range(ta):
            dyp_ref[al * N_SEG:(al + 1) * N_SEG, :] = dy_ref[pl.ds(a0 + al, N_SEG, stride=seg), :]
        g_ref[...] = jnp.dot(dyp_ref[...].astype(BF16), ct_ref[0], preferred_element_type=F32)

        if full:
            for al in range(ta):
                up_ref[al * N_SEG:(al + 1) * N_SEG, :] = u_ref[pl.ds(a0 + al, N_SEG, stride=seg), :]
            h_ref[...] = jnp.dot(up_ref[...].astype(BF16), b_ref[0], preferred_element_type=F32)
            h0r, h0i = hch_ref[0, 0, :, :HALF_W], hch_ref[0, 0, :, HALF_W:]

            def hstep(i, carry):
                hr, hi = carry
                al = ta - 1 - i if descending else i
                row = pl.multiple_of(al * N_SEG, N_SEG)
                pr, pi = _cmul(ar, ai, hr, hi, False)
                hr = pr + h_ref[pl.ds(row, N_SEG), :HALF_W]
                hi = pi + h_ref[pl.ds(row, N_SEG), HALF_W:]
                h_ref[pl.ds(row, N_SEG), :HALF_W] = hr
                h_ref[pl.ds(row, N_SEG), HALF_W:] = hi
                return hr, hi

            lax.fori_loop(0, ta, hstep, (h0r, h0i))

        def gstep(i, carry):
            gr, gi = carry
            al = ta - 1 - i if g_desc else i
            row = pl.multiple_of(al * N_SEG, N_SEG)
            pr, pi = _cmul(ar, ai, gr, gi, True)
            gr = pr + g_ref[pl.ds(row, N_SEG), :HALF_W]
            gi = pi + g_ref[pl.ds(row, N_SEG), HALF_W:]
            if full:
                g_ref[pl.ds(row, N_SEG), :HALF_W] = gr
                g_ref[pl.ds(row, N_SEG), HALF_W:] = gi
            return gr, gi

        gr, gi = lax.fori_loop(0, ta, gstep, (st_ref[:, :HALF_W], st_ref[:, HALF_W:]))
        st_ref[:, :HALF_W] = gr
        st_ref[:, HALF_W:] = gi

        @pl.when(k == nk - 1)
        def _():
            gfin_ref[0] = st_ref[...]

        if full:
            gb = g_ref[...].astype(BF16)
            dut = jnp.dot(gb, bt_ref[0], preferred_element_type=F32)
            for al in range(ta):
                du_ref[pl.ds(a0 + al, N_SEG, stride=seg), :] = dut[al * N_SEG:(al + 1) * N_SEG, :]
            tn = (((0,), (0,)), ((), ()))
            dbp = lax.dot_general(up_ref[...].astype(BF16), gb, tn, preferred_element_type=F32)
            dcp = lax.dot_general(dyp_ref[...].astype(BF16), h_ref[...].astype(BF16), tn, preferred_element_type=F32)
            inner = (ta - 1) * N_SEG
            if descending:
                g_in_r, g_in_i = g_ref[0:inner, :HALF_W], g_ref[0:inner, HALF_W:]
                p_in_r, p_in_i = h_ref[N_SEG:tile, :HALF_W], h_ref[N_SEG:tile, HALF_W:]
                g_ed_r, g_ed_i = g_ref[inner:tile, :HALF_W], g_ref[inner:tile, HALF_W:]
            else:
                g_in_r, g_in_i = g_ref[N_SEG:tile, :HALF_W], g_ref[N_SEG:tile, HALF_W:]
                p_in_r, p_in_i = h_ref[0:inner, :HALF_W], h_ref[0:inner, HALF_W:]
                g_ed_r, g_ed_i = g_ref[0:N_SEG, :HALF_W], g_ref[0:N_SEG, HALF_W:]
            dar = g_ed_r * h0r + g_ed_i * h0i
            dai = g_ed_i * h0r - g_ed_r * h0i
            if ta > 1:
                dar = dar + jnp.sum((g_in_r * p_in_r + g_in_i * p_in_i).reshape(ta - 1, N_SEG, HALF_W), axis=0)
                dai = dai + jnp.sum((g_in_i * p_in_r - g_in_r * p_in_i).reshape(ta - 1, N_SEG, HALF_W), axis=0)

            @pl.when(k == 0)
            def _():
                db_ref[0] = dbp
                dc_ref[0] = dcp
                da_ref[0, :, :HALF_W] = dar
                da_ref[0, :, HALF_W:] = dai

            @pl.when(k > 0)
            def _():
                db_ref[0] += dbp
                dc_ref[0] += dcp
                da_ref[0, :, :HALF_W] += dar
                da_ref[0, :, HALF_W:] += dai

    u_spec = pl.BlockSpec((n_rows, SLAB_W), lambda s, k: (rb, s))
    m_spec = pl.BlockSpec((1, SLAB_W, STATE_W), lambda s, k: (s, 0, 0))
    mt_spec = pl.BlockSpec((1, STATE_W, SLAB_W), lambda s, k: (s, 0, 0))
    a_spec = pl.BlockSpec((1, 1, STATE_W), lambda s, k: (s, 0, 0))
    st_spec = pl.BlockSpec((1, N_SEG, STATE_W), lambda s, k: (s, 0, 0))
    st_shape = jax.ShapeDtypeStruct((N_SLAB, N_SEG, STATE_W), F32)
    if not full:
        scratch = [pltpu.VMEM((N_SEG, STATE_W), F32), pltpu.VMEM((tile, SLAB_W), F32), pltpu.VMEM((tile, STATE_W), F32)]
        return pl.pallas_call(
            body, grid=(N_SLAB, nk), in_specs=[u_spec, m_spec, a_spec, st_spec], out_specs=st_spec, out_shape=st_shape,
            scratch_shapes=scratch, compiler_params=_params("parallel", "arbitrary"), name=name)(dy, ct_mat, abar, g_in)
    kmap = (lambda s, k: (s, nk - 1 - k, 0, 0)) if g_desc else (lambda s, k: (s, k, 0, 0))
    in_specs = [u_spec, u_spec, m_spec, mt_spec, m_spec, a_spec, pl.BlockSpec((1, 1, N_SEG, STATE_W), kmap), st_spec]
    args = [u, dy, b_mat, bt_mat, ct_mat, abar, h_chunks, g_in]
    aliases = {}
    if du_alias is not None:
        in_specs.append(pl.BlockSpec(memory_space=pl.ANY))
        args.append(du_alias)
        aliases = {8: 1}
    acc_shape = jax.ShapeDtypeStruct((N_SLAB, SLAB_W, STATE_W), F32)
    out_specs = [st_spec, u_spec, m_spec, m_spec, st_spec]
    out_shape = [st_shape, jax.ShapeDtypeStruct((du_rows, D_MODEL), F32), acc_shape, acc_shape, st_shape]
    scratch = [pltpu.VMEM((N_SEG, STATE_W), F32), pltpu.VMEM((tile, SLAB_W), F32), pltpu.VMEM((tile, SLAB_W), F32),
               pltpu.VMEM((tile, STATE_W), F32), pltpu.VMEM((tile, STATE_W), F32)]
    return pl.pallas_call(
        body, grid=(N_SLAB, nk), in_specs=in_specs, out_specs=out_specs, out_shape=out_shape, scratch_shapes=scratch,
        input_output_aliases=aliases, compiler_params=_params("parallel", "arbitrary"), name=name)(*args)


ROPE_HALF = HEAD_DIM // 4
TABLE_W = 2 * HEAD_DIM
Q_SCALE = 1.0 / math.sqrt(HEAD_DIM)
HEADS_PER_BLOCK = 2 * KV_REP
Q_BLOCK_W = HEADS_PER_BLOCK * HEAD_DIM


def _rope_tables(n_lat, n_ctx):
    rows = n_lat // GRID_W
    row = jnp.repeat(jnp.arange(rows), GRID_W).astype(F32)
    col = jnp.tile(jnp.arange(GRID_W), rows).astype(F32)
    freqs = ROPE_THETA ** (-jnp.arange(ROPE_HALF, dtype=F32) / ROPE_HALF)
    ang_r = row[:, None] * freqs[None]
    ang_c = col[:, None] * freqs[None]
    cos = jnp.concatenate([jnp.cos(ang_r), jnp.cos(ang_r), jnp.cos(ang_c), jnp.cos(ang_c)] * 2, axis=1)
    sin = jnp.concatenate([jnp.sin(ang_r), jnp.sin(ang_r), jnp.sin(ang_c), jnp.sin(ang_c)] * 2, axis=1)
    cos = jnp.concatenate([cos, jnp.ones((n_ctx, TABLE_W), F32)], axis=0)
    sin = jnp.concatenate([sin, jnp.zeros((n_ctx, TABLE_W), F32)], axis=0)
    return cos, sin


def _rot_half(v):
    w = v.shape[1]
    ahead = pltpu.roll(v, w - ROPE_HALF, axis=1)
    behind = pltpu.roll(v, ROPE_HALF, axis=1)
    lane = lax.broadcasted_iota(jnp.int32, v.shape, 1)
    return jnp.where((lane % (2 * ROPE_HALF)) < ROPE_HALF, -ahead, behind)


def _head_mean(v, sel, selt):
    m = jnp.dot(v, sel, precision=lax.Precision.HIGH, preferred_element_type=F32) * (1.0 / HEAD_DIM)
    return jnp.dot(m, selt, precision=lax.Precision.HIGH, preferred_element_type=F32)


def _head_selectors(n_heads):
    sel = jnp.repeat(jnp.eye(n_heads, dtype=F32), HEAD_DIM, axis=0)
    return sel[None], sel.T[None]


def _head_norm(x, sel, selt):
    r = lax.rsqrt(_head_mean(x * x, sel, selt) + NORM_EPS)
    return x * r, r


def _qk_prep(proj, qn, kn, cos, sin, n, tr):
    qw, kw = _vec(jnp.tile(qn, N_Q_HEADS)), _vec(jnp.tile(kn, N_KV_HEADS))
    sq, sqt = _head_selectors(N_Q_HEADS)
    sk, skt = _head_selectors(N_KV_HEADS)

    def fn(qr, kvr, ct, st, qwv, kwv, s16, s16t, s4, s4t):
        outs = []
        for x, wv, sel, selt, scale in ((qr, qwv, s16, s16t, Q_SCALE), (kvr[:, :KV_W], kwv, s4, s4t, 1.0)):
            reps = x.shape[1] // TABLE_W
            cw, sw = jnp.tile(ct, (1, reps)), jnp.tile(st, (1, reps))
            xh, _ = _head_norm(x, sel, selt)
            nrm = xh * wv
            outs.append((nrm * cw + _rot_half(nrm) * sw) * scale)
        return outs[0], outs[1], kvr[:, KV_W:]

    return _rowwise("l1_qk_prep", fn, n, tr,
                    [(proj, 0, ATTN_W), (proj, 2 * ATTN_W // (2 * KV_W), 2 * KV_W), (cos, 0, TABLE_W), (sin, 0, TABLE_W)],
                    [qw, kw, sq, sqt, sk, skt], [(ATTN_W, BF16), (KV_W, BF16), (KV_W, BF16)], [])


def _qk_prep_bwd(proj, qn, kn, cos, sin, dq, dz, dk, dv, n, n_lat, tr):
    qw, kw = _vec(jnp.tile(qn, N_Q_HEADS)), _vec(jnp.tile(kn, N_KV_HEADS))
    sq, sqt = _head_selectors(N_Q_HEADS)
    sk, skt = _head_selectors(N_KV_HEADS)
    nlt = n_lat // tr

    def fn(flag, qr, kvr, ct, st, dqt, dzt, dkt, dvt, qwv, kwv, s16, s16t, s4, s4t):
        dxs, dws = [], []
        for x, dy, wv, sel, selt in ((qr, dqt * (flag * Q_SCALE), qwv, s16, s16t), (kvr[:, :KV_W], dkt, kwv, s4, s4t)):
            reps = x.shape[1] // TABLE_W
            cw, sw = jnp.tile(ct, (1, reps)), jnp.tile(st, (1, reps))
            xh, r = _head_norm(x, sel, selt)
            dn = dy * cw - _rot_half(dy * sw)
            dxh = dn * wv
            dxs.append(r * (dxh - xh * _head_mean(dxh * xh, sel, selt)))
            dws.append(dn * xh)
        return jnp.concatenate([dxs[0], dzt * flag, dxs[1], dvt], axis=1), dws[0], dws[1]

    dproj, dqw, dkw = _rowwise(
        "l1_qk_prep_bwd", fn, n, tr,
        [(proj, 0, ATTN_W), (proj, 2 * ATTN_W // (2 * KV_W), 2 * KV_W), (cos, 0, TABLE_W), (sin, 0, TABLE_W),
         (dq, 0, ATTN_W, "clamp", nlt), (dz, 0, ATTN_W, "clamp", nlt), (dk, 0, KV_W), (dv, 0, KV_W)],
        [qw, kw, sq, sqt, sk, skt], [(2 * ATTN_W + 2 * KV_W, F32)], [(1, ATTN_W), (1, KV_W)], n_lat=n_lat, want_flag=True)
    return dproj, dqw.reshape(N_Q_HEADS, HEAD_DIM).sum(0)[None], dkw.reshape(N_KV_HEADS, HEAD_DIM).sum(0)[None]


NT = (((1,), (1,)), ((), ()))


def _attn_fwd(q, k, v, t, tq, tk):
    n = k.shape[0]
    nkc = n // tk

    ts = _largest_tile(tq, 256, LANES)
    items = [(sub, j) for sub in range(tq // ts) for j in range(HEADS_PER_BLOCK)]

    def body(q_ref, k_ref, v_ref, o_ref, lse_ref, s_ref, m_ref, l_ref, acc_ref):
        def lanes(j):
            g = j // KV_REP
            return slice(j * HEAD_DIM, (j + 1) * HEAD_DIM), slice(g * HEAD_DIM, (g + 1) * HEAD_DIM)

        for idx in range(len(items) + 1):
            nxt = items[idx] if idx < len(items) else None
            cur = items[idx - 1] if idx > 0 else None
            sn, sc = idx % 2, (idx - 1) % 2
            if nxt is not None:
                rows_n = slice(nxt[0] * ts, (nxt[0] + 1) * ts)
                ql_n, kl_n = lanes(nxt[1])
                qv = q_ref[rows_n, ql_n]
                m_ref[sn] = jnp.full((ts, LANES), -jnp.inf, F32)
            if cur is not None:
                rows_c = slice(cur[0] * ts, (cur[0] + 1) * ts)
                ql_c, kl_c = lanes(cur[1])
                m_row = jnp.max(m_ref[sc], axis=-1, keepdims=True)
                l_ref[...] = jnp.zeros(l_ref.shape, F32)
                acc_ref[...] = jnp.zeros(acc_ref.shape, F32)

            def sweep(kc, c):
                off = pl.multiple_of(kc * tk, tk)
                if nxt is not None:
                    s = lax.dot_general(qv, k_ref[pl.ds(off, tk), kl_n], NT, preferred_element_type=F32)
                    s_ref[sn, :, pl.ds(off, tk)] = s
                    m = m_ref[sn]
                    for cb in range(tk // LANES):
                        m = jnp.maximum(m, s[:, cb * LANES:(cb + 1) * LANES])
                    m_ref[sn] = m
                if cur is not None:
                    p = jnp.exp(s_ref[sc, :, pl.ds(off, tk)] - m_row)
                    lsum = l_ref[...]
                    for cb in range(tk // LANES):
                        lsum = lsum + p[:, cb * LANES:(cb + 1) * LANES]
                    l_ref[...] = lsum
                    acc_ref[...] += jnp.dot(p.astype(BF16), v_ref[pl.ds(off, tk), kl_c], preferred_element_type=F32)
                return c

            lax.fori_loop(0, nkc, sweep, 0)
            if cur is not None:
                l_row = jnp.sum(l_ref[...], axis=-1, keepdims=True)
                o_ref[rows_c, ql_c] = acc_ref[...] / l_row
                lse_ref[0, rows_c, cur[1]:cur[1] + 1] = m_row + jnp.log(l_row)

    nb = ATTN_W // Q_BLOCK_W
    kspec = pl.BlockSpec((n, LANES), lambda b, i: (0, b))
    return pl.pallas_call(
        body, grid=(nb, t // tq),
        in_specs=[pl.BlockSpec((tq, Q_BLOCK_W), lambda b, i: (i, b)), kspec, kspec],
        out_specs=[pl.BlockSpec((tq, Q_BLOCK_W), lambda b, i: (i, b)),
                   pl.BlockSpec((1, tq, HEADS_PER_BLOCK), lambda b, i: (b, i, 0))],
        out_shape=[jax.ShapeDtypeStruct((t, ATTN_W), F32), jax.ShapeDtypeStruct((nb, t, HEADS_PER_BLOCK), F32)],
        scratch_shapes=[pltpu.VMEM((2, ts, n), F32), pltpu.VMEM((2, ts, LANES), F32), pltpu.VMEM((ts, LANES), F32),
                        pltpu.VMEM((ts, HEAD_DIM), F32)],
        compiler_params=_params("parallel", "parallel"), name="attn_fwd")(q, k, v)


def _attn_bwd(q, k, v, do, o, lse, t, tq, tk):
    n = k.shape[0]
    nkc = n // tk
    tn = (((0,), (0,)), ((), ()))

    def body(q_ref, k_ref, v_ref, do_ref, o_ref, lse_ref, dq_ref, dk_ref, dv_ref, acc_ref):
        @pl.when(pl.program_id(1) == 0)
        def _():
            dk_ref[...] = jnp.zeros(dk_ref.shape, F32)
            dv_ref[...] = jnp.zeros(dv_ref.shape, F32)

        for j0 in range(0, HEADS_PER_BLOCK, 2):
            kl = slice((j0 // KV_REP) * HEAD_DIM, (j0 // KV_REP + 1) * HEAD_DIM)
            heads = []
            for a in range(2):
                j = j0 + a
                ql = slice(j * HEAD_DIM, (j + 1) * HEAD_DIM)
                qv, dov = q_ref[:, ql], do_ref[:, ql]
                dl_v = jnp.sum(dov.astype(F32) * o_ref[:, ql], axis=-1, keepdims=True)
                heads.append((ql, qv, dov, dl_v, lse_ref[0, :, j:j + 1]))
                acc_ref[a] = jnp.zeros((tq, HEAD_DIM), F32)

            def step(kc, c):
                off = pl.multiple_of(kc * tk, tk)
                kt = k_ref[pl.ds(off, tk), kl]
                vt = v_ref[pl.ds(off, tk), kl]
                dv_part, dk_part = None, None
                for a, (_, qv, dov, dl_v, lse_v) in enumerate(heads):
                    s = lax.dot_general(qv, kt, NT, preferred_element_type=F32)
                    p = jnp.exp(s - lse_v)
                    dp = lax.dot_general(dov, vt, NT, preferred_element_type=F32)
                    ds = (p * (dp - dl_v)).astype(BF16)
                    acc_ref[a] += jnp.dot(ds, kt, preferred_element_type=F32)
                    dvp = lax.dot_general(p.astype(BF16), dov, tn, preferred_element_type=F32)
                    dkp = lax.dot_general(ds, qv, tn, preferred_element_type=F32)
                    dv_part = dvp if dv_part is None else dv_part + dvp
                    dk_part = dkp if dk_part is None else dk_part + dkp
                dv_ref[pl.ds(off, tk), kl] += dv_part
                dk_ref[pl.ds(off, tk), kl] += dk_part
                return c

            lax.fori_loop(0, nkc, step, 0)
            for a, h in enumerate(heads):
                dq_ref[:, h[0]] = acc_ref[a]

    nb = ATTN_W // Q_BLOCK_W
    qspec = pl.BlockSpec((tq, Q_BLOCK_W), lambda b, i: (i, b))
    kspec = pl.BlockSpec((n, LANES), lambda b, i: (0, b))
    cspec = pl.BlockSpec((1, tq, HEADS_PER_BLOCK), lambda b, i: (b, i, 0))
    return pl.pallas_call(
        body, grid=(nb, t // tq), in_specs=[qspec, kspec, kspec, qspec, qspec, cspec], out_specs=[qspec, kspec, kspec],
        out_shape=[jax.ShapeDtypeStruct((t, ATTN_W), F32), jax.ShapeDtypeStruct((n, KV_W), F32),
                   jax.ShapeDtypeStruct((n, KV_W), F32)],
        scratch_shapes=[pltpu.VMEM((2, tq, HEAD_DIM), F32)],
        compiler_params=_params("parallel", "arbitrary"), name="attn_bwd")(q, k, v, do, o, lse)


def _s5_system(p, n_lat, n_ctx):
    two_g = 2 * SSM_GROUPS
    a_re = p["ssm_a_re"].reshape(two_g, SSM_STATE)
    a_im = p["ssm_a_im"].reshape(two_g, SSM_STATE)
    log_dt = p["ssm_log_dt"].reshape(two_g, 1)
    b_re = p["ssm_b_re"].reshape(two_g, SSM_STATE, SSM_GROUP).transpose(2, 0, 1)
    b_im = p["ssm_b_im"].reshape(two_g, SSM_STATE, SSM_GROUP).transpose(2, 0, 1)
    raw = (a_re, a_im, log_dt, b_re, b_im)
    abr, abi, bbr, bbi, alr, ali, acr, aci = _s5_prep(*raw, n_lat // N_SEG, n_ctx // N_SEG)
    dirs = []
    for d in range(2):
        g = slice(d * SSM_GROUPS, (d + 1) * SSM_GROUPS)
        b_mat = _slab_in_matrix(bbr[:, g].transpose(1, 2, 0), bbi[:, g].transpose(1, 2, 0))
        c_mat = _slab_out_matrix(p["ssm_c_re"][0, d], p["ssm_c_im"][0, d])
        dirs.append(dict(
            b=b_mat.astype(BF16), bt=b_mat.transpose(0, 2, 1).astype(BF16),
            c=c_mat.astype(BF16), ct=c_mat.transpose(0, 2, 1).astype(BF16),
            abar=_slab_pair(abr[g], abi[g]),
            a_lat=_slab_pair(alr[g], ali[g])[:, 0], a_ctx=_slab_pair(acr[g], aci[g])[:, 0]))
    return raw, dirs


def _s5_forward(proj, dirs, n_lat, n_ctx):
    n = n_lat + n_ctx
    zero_st = jnp.zeros((N_SLAB, N_SEG, STATE_W), F32)
    zero_c = jnp.zeros((N_SLAB, STATE_W), F32)
    ys, saved = [], []
    for d, s in enumerate(dirs):
        desc = d == 1
        tag = f"s5f{d}"
        zc = _s5_scan(tag + "_ctx_ends", proj, n_ctx, n_lat, s["b"], s["c"], s["abar"], zero_st, desc, False)
        ent_c, h0 = _s5_carry(tag + "_ctx_carry", _seg_major(zc), s["a_ctx"], zero_c, desc, False)
        _, y, hch_c = _s5_scan(tag + "_ctx", proj, n_ctx, n_lat, s["b"], s["c"], s["abar"], _seg_major(ent_c), desc,
                               True, y_rows=n)
        zl = _s5_scan(tag + "_lat_ends", proj, n_lat, 0, s["b"], s["c"], s["abar"], zero_st, desc, False)
        ent_l, _ = _s5_carry(tag + "_lat_carry", _seg_major(zl), s["a_lat"], h0, desc, False)
        _, y, hch_l = _s5_scan(tag + "_lat", proj, n_lat, 0, s["b"], s["c"], s["abar"], _seg_major(ent_l), desc,
                               True, y_alias=y, y_rows=n)
        ys.append(y)
        saved.append((hch_l, hch_c))
    return ys, saved


def _s5_backward(proj, dy, dirs, saved, n_lat, n_ctx):
    n = n_lat + n_ctx
    zero_st = jnp.zeros((N_SLAB, N_SEG, STATE_W), F32)
    zero_c = jnp.zeros((N_SLAB, STATE_W), F32)
    out = []
    for d, s in enumerate(dirs):
        desc = d == 1
        tag = f"s5b{d}"
        hch_l, hch_c = saved[d]
        gl = _s5_scan_bwd(tag + "_lat_ends", proj, dy, n_lat, 0, s["b"], s["bt"], s["ct"], s["abar"], None, zero_st,
                          desc, False)
        ent_l, g0 = _s5_carry(tag + "_lat_carry", _seg_major(gl), s["a_lat"], zero_c, not desc, True)
        _, du, db_l, dc_l, da_l = _s5_scan_bwd(tag + "_lat", proj, dy, n_lat, 0, s["b"], s["bt"], s["ct"], s["abar"],
                                               hch_l, _seg_major(ent_l), desc, True, du_rows=n)
        gc = _s5_scan_bwd(tag + "_ctx_ends", proj, dy, n_ctx, n_lat, s["b"], s["bt"], s["ct"], s["abar"], None, zero_st,
                          desc, False)
        ent_c, _ = _s5_carry(tag + "_ctx_carry", _seg_major(gc), s["a_ctx"], g0, not desc, True)
        _, du, db_c, dc_c, da_c = _s5_scan_bwd(tag + "_ctx", proj, dy, n_ctx, n_lat, s["b"], s["bt"], s["ct"], s["abar"],
                                               hch_c, _seg_major(ent_c), desc, True, du_alias=du, du_rows=n)
        out.append((du, db_l + db_c, dc_l + dc_c, da_l + da_c))
    return out


def _s5_param_grads(raw, bwd):
    dabr, dabi, dbbr, dbbi, dcr, dci = [], [], [], [], [], []
    for _, db, dc, da in bwd:
        da = jnp.sum(da, axis=1)
        dabr.append(da[:, :HALF_W].reshape(SSM_GROUPS, SSM_STATE))
        dabi.append(da[:, HALF_W:].reshape(SSM_GROUPS, SSM_STATE))
        dbd = _slab_diag(db)
        dbbr.append(dbd[0].transpose(1, 0, 2))
        dbbi.append(dbd[1].transpose(1, 0, 2))
        dcd = _slab_diag(dc)
        dcr.append(dcd[0])
        dci.append(-dcd[1])
    cat = lambda xs, ax: jnp.concatenate(xs, axis=ax)
    dar, dai, dld, dbr, dbi = _s5_prep_bwd(*raw, cat(dabr, 0), cat(dabi, 0), cat(dbbr, 1), cat(dbbi, 1))
    shp = (1, 2, SSM_GROUPS, SSM_STATE)
    b_shape = (1, 2, SSM_GROUPS, SSM_STATE, SSM_GROUP)
    return dict(
        ssm_a_re=dar.reshape(shp), ssm_a_im=dai.reshape(shp), ssm_log_dt=dld.reshape(1, 2, SSM_GROUPS),
        ssm_b_re=dbr.transpose(1, 2, 0).reshape(b_shape), ssm_b_im=dbi.transpose(1, 2, 0).reshape(b_shape),
        ssm_c_re=jnp.stack(dcr)[None], ssm_c_im=jnp.stack(dci)[None])


def _example_step(x, ctx, target, mods, w, p):
    t, c = x.shape[0], ctx.shape[0]
    n = t + c
    assert t % c == 0 and c % LANES == 0 and c % (8 * N_SEG) == 0 and t % GRID_W == 0
    tr = _largest_tile(c, 256, 8)
    xall = jnp.concatenate([x, ctx], axis=0)
    g0, g1 = _vec(p["norm_g"][0]), _vec(p["norm_g"][1])
    (shift0, scale0, gate0), (shift1, scale1, gate1) = [tuple(_vec(v) for v in m) for m in mods]

    h0 = _norm_mod_fwd("l0_norm", xall, g0, scale0, shift0, n, tr, t)
    proj0 = _mm("l0_in", h0, w["ssm_w_in"], "nn")
    raw, dirs = _s5_system(p, t, c)
    (y_f, y_r), saved = _s5_forward(proj0, dirs, t, c)
    d_skip = _vec(p["ssm_d"][0])

    def post_a(u, yf, yr, dv):
        y = u * dv + yf + yr
        return y, _gelu(y)

    y0, yg = _rowwise("l0_gelu", post_a, n, tr, [(proj0, 0, D_MODEL), (y_f, 0, D_MODEL), (y_r, 0, D_MODEL)], [d_skip],
                      [(D_MODEL, F32), (D_MODEL, F32)], [])
    tg = _mm("l0_glu", yg, w["ssm_w_glu"], "nn")
    b_glu = _vec(p["ssm_b_glu"][0])

    def post_b(ygt, tt, zt, bv):
        return ygt * _sigmoid(tt + bv) * _silu(zt)

    gz0 = _rowwise("l0_gate", post_b, n, tr, [(yg, 0, D_MODEL), (tg, 0, D_MODEL), (proj0, 1, D_MODEL)], [b_glu],
                   [(D_MODEL, F32)], [])[0]
    out0 = _mm("l0_out", gz0, w["ssm_w_out"], "nn")

    def res_norm(xt, ot, gv, g1v, sc, sh):
        x1t = xt + gv * ot
        xh, _ = _rms_hat(x1t)
        return x1t, (xh * g1v) * (1.0 + sc) + sh

    x1, h1 = _rowwise("l0_res_l1_norm", res_norm, n, tr, [(xall, 0, D_MODEL), (out0, 0, D_MODEL)],
                      [gate0, g1, scale1, shift1], [(D_MODEL, F32), (D_MODEL, F32)], [], n_lat=t)
    proj1 = _mm("l1_in", h1, w["attn_w_in"], "nn")
    cos, sin = _rope_tables(t, c)
    qn, kn = p["attn_q_norm"][0], p["attn_k_norm"][0]
    q_h, k_h, v_h = _qk_prep(proj1, qn, kn, cos, sin, n, tr)
    tq = _largest_tile(t, 512, LANES)
    o, lse = _attn_fwd(q_h, k_h, v_h, t, tq, _largest_tile(n, 2816, LANES))
    gz1 =_rowwise("l1_gate", lambda ot, zt: ot * _silu(zt), t, tr, [(o, 0, D_MODEL), (proj1, 1, D_MODEL)], [],
                   [(D_MODEL, F32)], [])[0]
    out1 = _mm("l1_out", gz1, w["attn_w_out"], "nn")

    gf = _vec(p["final_norm_g"])

    def head(x1t, o1t, tgt, g1v, gfv):
        x2 = x1t + g1v * o1t
        xh, r = _rms_hat(x2)
        e = xh * gfv - tgt
        dyf = e * (1.0 / D_MODEL)
        dx2 = _rms_bwd(xh, r, dyf * gfv)
        return dx2, g1v * dx2, dyf * xh, dx2 * o1t, jnp.sum(e * e, axis=1, keepdims=True)

    gate1_lat = gate1[0:1]
    dx2, dout1, d_gf, d_gate1, sq = _rowwise(
        "head", head, t, tr, [(x1, 0, D_MODEL), (out1, 0, D_MODEL), (target, 0, D_MODEL)], [gate1_lat, gf],
        [(D_MODEL, F32), (D_MODEL, F32)], [(1, D_MODEL), (1, D_MODEL), (1, 1)])

    d_w_attn_out = _mm("l1_out_dw", gz1, dout1, "tn")
    dgz1 = _mm("l1_out_dx", dout1, w["attn_w_out"], "nt")

    def gate1_bwd(dgt, ot, zt):
        return dgt * _silu(zt), dgt * ot * _silu_grad(zt)

    do, dz1 = _rowwise("l1_gate_bwd", gate1_bwd, t, tr, [(dgz1, 0, D_MODEL), (o, 0, D_MODEL), (proj1, 1, D_MODEL)], [],
                       [(D_MODEL, BF16), (D_MODEL, F32)], [])
    dq_s, dk, dv = _attn_bwd(q_h, k_h, v_h, do, o, lse, t, tq, _largest_tile(n, 1024, LANES))
    dproj1, d_qn, d_kn = _qk_prep_bwd(proj1, qn, kn, cos, sin, dq_s, dz1, dk, dv, n, t, tr)
    d_w_attn_in = _mm("l1_in_dw", h1, dproj1, "tn")
    dh1 = _mm("l1_in_dx", dproj1, w["attn_w_in"], "nt")
    dx1, dout0, d_g1, d_scale1, d_shift1, d_gate0 = _norm_mod_bwd("l1_norm_bwd", x1, g1, scale1, dh1, dx2, n, tr, t,
                                                                  prev=(out0, gate0))

    d_w_out = _mm("l0_out_dw", gz0, dout0, "tn")
    dgz0 = _mm("l0_out_dx", dout0, w["ssm_w_out"], "nt")

    def post_b_bwd(dgt, ygt, tt, zt, bv):
        s = _sigmoid(tt + bv)
        dy2 = dgt * _silu(zt)
        dt = dy2 * ygt * s * (1.0 - s)
        return dgt * (ygt * s) * _silu_grad(zt), dt, dy2 * s, dt

    dz0, dtg, dyg_a, d_b_glu = _rowwise(
        "l0_gate_bwd", post_b_bwd, n, tr, [(dgz0, 0, D_MODEL), (yg, 0, D_MODEL), (tg, 0, D_MODEL), (proj0, 1, D_MODEL)],
        [b_glu], [(D_MODEL, F32), (D_MODEL, F32), (D_MODEL, F32)], [(1, D_MODEL)])
    d_w_glu = _mm("l0_glu_dw", yg, dtg, "tn")
    dyg_b = _mm("l0_glu_dx", dtg, w["ssm_w_glu"], "nt")

    def post_a_bwd(da, db, yt, ut, dv):
        dy = (da + db) * _gelu_grad(yt)
        return dy, dy * dv, dy * ut

    dy0, du_skip, d_d = _rowwise("l0_gelu_bwd", post_a_bwd, n, tr,
                                 [(dyg_a, 0, D_MODEL), (dyg_b, 0, D_MODEL), (y0, 0, D_MODEL), (proj0, 0, D_MODEL)], [d_skip],
                                 [(D_MODEL, F32), (D_MODEL, F32)], [(1, D_MODEL)])
    s5_bwd = _s5_backward(proj0, dy0, dirs, saved, t, c)
    dproj0 = _rowwise("l0_in_grad", lambda a, b, cc, dz: jnp.concatenate([a + b + cc, dz], axis=1), n, tr,
                      [(du_skip, 0, D_MODEL), (s5_bwd[0][0], 0, D_MODEL), (s5_bwd[1][0], 0, D_MODEL), (dz0, 0, D_MODEL)], [],
                      [(2 * D_MODEL, F32)], [])[0]
    d_w_in = _mm("l0_in_dw", h0, dproj0, "tn")
    dh0 = _mm("l0_in_dx", dproj0, w["ssm_w_in"], "nt")
    dx0, d_g0, d_scale0, d_shift0 = _norm_mod_bwd("l0_norm_bwd", xall, g0, scale0, dh0, dx1, n, tr, t)

    big = dict(ssm_w_in=d_w_in, ssm_w_glu=d_w_glu, ssm_w_out=d_w_out, attn_w_in=d_w_attn_in, attn_w_out=d_w_attn_out)
    small = dict(
        norm_g=jnp.concatenate([d_g0[0], d_g1[0]], axis=0), ssm_d=d_d[0], ssm_b_glu=d_b_glu[0],
        attn_q_norm=d_qn, attn_k_norm=d_kn, final_norm_g=d_gf[0, 0], **_s5_param_grads(raw, s5_bwd))
    zero_v = jnp.zeros((D_MODEL,), F32)
    d_mod_lat = jnp.stack([jnp.concatenate([d_shift0[0, 0], d_scale0[0, 0], d_gate0[0, 0]]),
                           jnp.concatenate([d_shift1[0, 0], d_scale1[0, 0], d_gate1[0, 0]])])
    d_mod_ctx = jnp.stack([jnp.concatenate([d_shift0[1, 0], d_scale0[1, 0], d_gate0[1, 0]]),
                           jnp.concatenate([d_shift1[1, 0], d_scale1[1, 0], zero_v])])
    return sq[0, 0, 0], dx0[:t], big, small, d_mod_lat, d_mod_ctx


def _adamw(name, w, g, m, v):
    rows, cols = w.shape
    tr = _largest_tile(rows, 256, 8)
    c1 = 1.0 / (1.0 - ADAM_B1 ** ADAM_STEP)
    c2 = 1.0 / (1.0 - ADAM_B2 ** ADAM_STEP)

    def fn(wt, gt, mt, vt):
        mn = ADAM_B1 * mt + (1.0 - ADAM_B1) * gt
        vn = ADAM_B2 * vt + (1.0 - ADAM_B2) * (gt * gt)
        delta = -ADAM_LR * ((mn * c1) / (jnp.sqrt(vn * c2) + ADAM_EPS) + ADAM_WD * wt)
        return delta, mn, vn

    return _rowwise(name, fn, rows, tr, [(a, 0, cols) for a in (w, g, m, v)], [], [(cols, F32)] * 3, [])


BIG = ("ssm_w_in", "ssm_w_glu", "ssm_w_out", "attn_w_in", "attn_w_out")
COL_SHARDED = ("ssm_w_in", "attn_w_in")
WEIGHTS = ("c_ctx", "w_mod", "b_mod", "norm_g", "ssm_w_in", "ssm_a_re", "ssm_a_im", "ssm_log_dt", "ssm_b_re", "ssm_b_im",
           "ssm_c_re", "ssm_c_im", "ssm_d", "ssm_w_glu", "ssm_b_glu", "ssm_w_out", "attn_w_in", "attn_q_norm",
           "attn_k_norm", "attn_w_out", "final_norm_g")
SMALL = tuple(k for k in WEIGHTS if k not in BIG and k != "w_mod")
PACK_W = 1024


def _attn_in_perm(x, inverse):
    a, kv = ATTN_W, 2 * KV_W
    if inverse:
        return jnp.concatenate([x[..., :a], x[..., 2 * a:], x[..., a:2 * a]], axis=-1)
    return jnp.concatenate([x[..., :a], x[..., a + kv:], x[..., a:a + kv]], axis=-1)


def _pack(arrays, dtype, row_unit):
    flat = jnp.concatenate([a.reshape(-1).astype(dtype) for a in arrays])
    rows = -(-flat.shape[0] // PACK_W)
    rows = -(-rows // row_unit) * row_unit
    flat = jnp.concatenate([flat, jnp.zeros((rows * PACK_W - flat.shape[0],), dtype)])
    return flat.reshape(rows, PACK_W)


def _unpack(buf, shapes):
    lead = buf.shape[:-2]
    flat = buf.reshape(lead + (-1,))
    out, off = [], 0
    for shp in shapes:
        size = math.prod(shp)
        out.append(flat[..., off:off + size].reshape(lead + tuple(shp)))
        off += size
    return out


def _half_shape(name, shard_shape):
    r, ccols = shard_shape
    return (r // 2, ccols)


def kernel(x, c, ctx, c_ctx, w_mod, b_mod, norm_g, ssm_w_in, ssm_a_re, ssm_a_im, ssm_log_dt, ssm_b_re, ssm_b_im, ssm_c_re, ssm_c_im, ssm_d, ssm_w_glu, ssm_b_glu, ssm_w_out, attn_w_in, attn_q_norm, attn_k_norm, attn_w_out, final_norm_g, loss_target, m_c_ctx, m_w_mod, m_b_mod, m_norm_g, m_ssm_w_in, m_ssm_a_re, m_ssm_a_im, m_ssm_log_dt, m_ssm_b_re, m_ssm_b_im, m_ssm_c_re, m_ssm_c_im, m_ssm_d, m_ssm_w_glu, m_ssm_b_glu, m_ssm_w_out, m_attn_w_in, m_attn_q_norm, m_attn_k_norm, m_attn_w_out, m_final_norm_g, v_c_ctx, v_w_mod, v_b_mod, v_norm_g, v_ssm_w_in, v_ssm_a_re, v_ssm_a_im, v_ssm_log_dt, v_ssm_b_re, v_ssm_b_im, v_ssm_c_re, v_ssm_c_im, v_ssm_d, v_ssm_w_glu, v_ssm_b_glu, v_ssm_w_out, v_attn_w_in, v_attn_q_norm, v_attn_k_norm, v_attn_w_out, v_final_norm_g):
    args = dict(locals())
    wts = {k: args[k] for k in WEIGHTS}
    mom_m = {k: args["m_" + k] for k in WEIGHTS}
    mom_v = {k: args["v_" + k] for k in WEIGHTS}
    mx, my, mc = lax.axis_index("x"), lax.axis_index("y"), lax.axis_index("c")
    chip = 2 * mx + my
    me = 2 * chip + mc

    halves = []
    for k in BIG:
        sh = wts[k][0]
        hr = sh.shape[0] // 2
        halves.append(lax.dynamic_slice_in_dim(sh, mc * hr, hr, axis=0))
    gathered = _gather_two_level("gather_weights", _pack(halves, BF16, 16))
    parts = _unpack(gathered, [h.shape for h in halves])
    w_full = {}
    for k, pc in zip(BIG, parts):
        hr, cols = pc.shape[1:]
        pc = pc.reshape(N_CHIP, 2, hr, cols)
        if k in COL_SHARDED:
            w_full[k] = pc.transpose(1, 2, 0, 3).reshape(2 * hr, N_CHIP * cols)
        else:
            w_full[k] = pc.reshape(N_CHIP * 2 * hr, cols)
    w_full["attn_w_in"] = _attn_in_perm(w_full["attn_w_in"], False)

    c_blk = jnp.concatenate([c, jnp.zeros((7, D_MODEL), F32)], axis=0)
    c_all = _exchange("gather_c", c_blk, True)[:, 0]
    cond = jnp.concatenate([c_all, c_ctx[None], jnp.zeros((7, D_MODEL), F32)], axis=0)
    s_cond, ds_cond = _rowwise("cond_silu", lambda t: (_silu(t), _silu_grad(t)), 16, 16, [(cond, 0, D_MODEL)], [],
                               [(D_MODEL, F32), (D_MODEL, F32)], [])
    w_mod_b = w_mod.astype(BF16)
    mcols = w_mod.shape[2]
    mod_part = jnp.stack([_mm(f"mod{i}", s_cond, w_mod_b[i], "nn") for i in range(2)])
    mod_g = _exchange("gather_mod", mod_part.reshape(32, mcols), True)
    mod_all = mod_g.reshape(N_CHIP, 2, 2, 16, mcols)[:, 0]
    mod_all = mod_all.transpose(1, 2, 0, 3).reshape(2, 16, N_CHIP * mcols) + b_mod[:, None, :]
    mods = []
    for i in range(2):
        lat = lax.dynamic_slice_in_dim(mod_all[i], me, 1, axis=0)[0]
        both = jnp.stack([lat, mod_all[i, 8]])
        mods.append((both[:, :D_MODEL], both[:, D_MODEL:2 * D_MODEL], both[:, 2 * D_MODEL:]))

    small_p = {k: wts[k] for k in SMALL if k != "c_ctx" and k != "b_mod"}
    sq, grad_x, big_g, small_g, d_mod_lat, d_mod_ctx = _example_step(x[0], ctx[0], loss_target[0], mods, w_full, small_p)
    loss = lax.psum(0.5 / D_MODEL * sq, ("x", "y", "c"))
    big_g["attn_w_in"] = _attn_in_perm(big_g["attn_w_in"], True)

    small_names = [k for k in SMALL if k not in ("c_ctx", "b_mod")]
    small_list = [small_g[k] for k in small_names] + [d_mod_lat, d_mod_ctx]
    small_shapes = [wts[k].shape for k in small_names] + [d_mod_lat.shape, d_mod_ctx.shape]
    packed = _pack(small_list, F32, 8 * N_DEV)
    slice_rows = packed.shape[0] // N_DEV
    slices = _exchange("scatter_small", packed.reshape(N_DEV, slice_rows, PACK_W), False)
    my_sum = _sum_slots("sum_small", slices)
    payload = jnp.concatenate([my_sum, _pack([d_mod_lat], F32, 8)], axis=0)
    sg = _exchange("gather_small", payload, True)
    summed = _unpack(sg[:, :slice_rows].reshape(packed.shape), small_shapes)
    grads = dict(zip(small_names, summed[:-2]))
    d_mod_lat_sum, d_mod_ctx_sum = summed[-2], summed[-1]
    grads["b_mod"] = d_mod_lat_sum + d_mod_ctx_sum
    d_mod_lat_all = _unpack(sg[:, slice_rows:], [d_mod_lat.shape])[0]

    g_w_mod, ds_cc = [], []
    for i in range(2):
        rows9 = jnp.concatenate([d_mod_lat_all[:, i], d_mod_ctx_sum[i][None], jnp.zeros((7, 3 * D_MODEL), F32)], axis=0)
        mine = lax.dynamic_slice_in_dim(rows9, chip * mcols, mcols, axis=1)
        g_w_mod.append(_mm(f"mod{i}_dw", s_cond, mine, "tn"))
        ds_cc.append(_mm(f"mod{i}_dx", mine, w_mod_b[i], "nt")[8])
    grads["w_mod"] = jnp.stack(g_w_mod)
    part = (ds_cc[0] + ds_cc[1]) * jnp.where(mc == 0, 1.0, 0.0)
    part_blk = jnp.concatenate([part[None], jnp.zeros((7, D_MODEL), F32)], axis=0)
    ds_all = _sum_slots("sum_c_ctx", _exchange("gather_c_ctx", part_blk, True))
    grads["c_ctx"] = ds_all[0] * ds_cond[8]

    blocks = []
    for k in BIG:
        g = big_g[k]
        rows, cols = g.shape
        if k in COL_SHARDED:
            blocks.append(g.reshape(2, rows // 2, N_CHIP, cols // N_CHIP).transpose(2, 0, 1, 3).reshape(N_DEV, -1))
        else:
            blocks.append(g.reshape(N_DEV, -1))
    sendbuf = jnp.concatenate(blocks, axis=1).astype(BF16)
    sendbuf = sendbuf.reshape(N_DEV, -1, PACK_W)
    recv = _exchange("scatter_big", sendbuf, False)
    mine = _sum_slots("sum_big", recv)
    both = _exchange("swap_halves", mine, True, sibling_only=True)
    half_shapes = [(wts[k].shape[1] // 2, wts[k].shape[2]) for k in BIG]
    for k, pc in zip(BIG, _unpack(both, half_shapes)):
        grads[k] = pc.reshape(wts[k].shape)

    delta, new_m, new_v = {}, {}, {}
    for k in BIG + ("w_mod",):
        shp = wts[k].shape
        two_d = (-1, shp[-1])
        res = _adamw("adamw_" + k, *[a.reshape(two_d) for a in (wts[k], grads[k], mom_m[k], mom_v[k])])
        delta[k], new_m[k], new_v[k] = [r.reshape(shp) for r in res]
    shapes = [wts[k].shape for k in SMALL]
    packed = [_pack([d[k] for k in SMALL], F32, 8) for d in (wts, grads, mom_m, mom_v)]
    res = _adamw("adamw_small", *packed)
    for dst, buf in zip((delta, new_m, new_v), res):
        for k, a in zip(SMALL, _unpack(buf, shapes)):
            dst[k] = a
    grads = {k: grads[k].reshape(wts[k].shape) for k in WEIGHTS}
    return (loss, grad_x[None], *[grads[k] for k in WEIGHTS], *[delta[k] for k in WEIGHTS],
            *[new_m[k] for k in WEIGHTS], *[new_v[k] for k in WEIGHTS])
```

```python
import functools
import math

import jax
import jax.numpy as jnp
from jax import lax
from jax.experimental import pallas as pl
from jax.experimental.pallas import tpu as pltpu

F32 = jnp.float32
BF16 = jnp.bfloat16

D_MODEL = 1024
NORM_EPS = 1e-6
SSM_GROUPS = 64
SSM_GROUP = 16
SSM_STATE = 64
LANES = 128
SLAB_W = LANES
N_SLAB = D_MODEL // SLAB_W
SLAB_GROUPS = SLAB_W // SSM_GROUP
HALF_W = SLAB_GROUPS * SSM_STATE
STATE_W = 2 * HALF_W
N_SEG = 8
HEAD_DIM = 64
N_Q_HEADS = 16
N_KV_HEADS = 4
KV_REP = N_Q_HEADS // N_KV_HEADS
ATTN_W = N_Q_HEADS * HEAD_DIM
KV_W = N_KV_HEADS * HEAD_DIM
GRID_W = 64
ROPE_THETA = 10000.0
N_DEV = 8
N_CHIP = 4
VMEM_LIMIT_BYTES = 56 * 1024 * 1024

ADAM_LR = 0.001
ADAM_B1 = 0.9
ADAM_B2 = 0.999
ADAM_EPS = 1e-08
ADAM_WD = 0.01
ADAM_STEP = 10


def _params(*sem):
    return pltpu.CompilerParams(dimension_semantics=sem, vmem_limit_bytes=VMEM_LIMIT_BYTES)


def _largest_tile(n, cap, unit):
    if n <= cap:
        return n
    t = (cap // unit) * unit
    while t >= unit:
        if n % t == 0:
            return t
        t -= unit
    raise ValueError(f"no tile for {n} (cap {cap}, unit {unit})")


def _rowwise(name, fn, n_rows, tr, row_ins, vec_ins, row_outs, red_outs, n_lat=None, want_flag=False):
    nt = n_rows // tr
    assert nt * tr == n_rows
    nlt = nt if n_lat is None else n_lat // tr

    def sel(i):
        return jnp.where(i >= nlt, 1, 0)

    arrays, in_specs = [], []
    for spec in row_ins:
        arr, cb, w = spec[:3]
        kind = spec[3] if len(spec) > 3 else None
        m = spec[4] if len(spec) > 4 else None
        if kind == "mod":
            imap = functools.partial(lambda i, cb, m: (i % m, cb), cb=cb, m=m)
        elif kind == "clamp":
            imap = functools.partial(lambda i, cb, m: (jnp.minimum(i, m - 1), cb), cb=cb, m=m)
        else:
            imap = functools.partial(lambda i, cb: (i, cb), cb=cb)
        arrays.append(arr)
        in_specs.append(pl.BlockSpec((tr, w), imap))
    for v in vec_ins:
        s, a, w = v.shape
        imap = (lambda i: (sel(i), 0, 0)) if s == 2 else (lambda i: (0, 0, 0))
        arrays.append(v)
        in_specs.append(pl.BlockSpec((1, a, w), imap))
    out_shapes, out_specs = [], []
    for w, dt in row_outs:
        out_shapes.append(jax.ShapeDtypeStruct((n_rows, w), dt))
        out_specs.append(pl.BlockSpec((tr, w), lambda i: (i, 0)))
    for s, w in red_outs:
        out_shapes.append(jax.ShapeDtypeStruct((s, 1, w), F32))
        imap = (lambda i: (sel(i), 0, 0)) if s == 2 else (lambda i: (0, 0, 0))
        out_specs.append(pl.BlockSpec((1, 1, w), imap))
    n_ri, n_vi, n_ro, n_rd = len(row_ins), len(vec_ins), len(row_outs), len(red_outs)

    def body(*refs):
        i = pl.program_id(0)
        rows = [r[...] for r in refs[:n_ri]]
        vecs = [r[0] for r in refs[n_ri:n_ri + n_vi]]
        outs = refs[n_ri + n_vi:]
        lead = [jnp.where(i < nlt, 1.0, 0.0).astype(F32)] if want_flag else []
        res = fn(*lead, *rows, *vecs)
        if not isinstance(res, (tuple, list)):
            res = (res,)
        assert len(res) == n_ro + n_rd
        for k in range(n_ro):
            outs[k][...] = res[k].astype(outs[k].dtype)
        for k in range(n_rd):
            part = jnp.sum(res[n_ro + k].astype(F32), axis=0, keepdims=True)
            first = i == 0
            if red_outs[k][0] == 2:
                first = jnp.logical_or(first, i == nlt)
            o = outs[n_ro + k]

            @pl.when(first)
            def _():
                o[0] = part

            @pl.when(jnp.logical_not(first))
            def _():
                o[0] = o[0] + part

    res = pl.pallas_call(
        body, grid=(nt,), in_specs=in_specs, out_specs=out_specs, out_shape=out_shapes,
        compiler_params=_params("arbitrary"), name=name)(*arrays)
    return res


def _vec(v):
    v = v.astype(F32)
    if v.ndim == 1:
        v = v[None]
    return v[:, None, :]


def _mm(name, a, b, mode, out_dtype=F32):
    if mode in ("nn", "nt"):
        m, k = a.shape
        n = b.shape[1] if mode == "nn" else b.shape[0]
        tm = _largest_tile(m, 1024, 8)
        tn = _largest_tile(n, 1024, 128)
        contract = (((1,), (0,)), ((), ())) if mode == "nn" else (((1,), (1,)), ((), ()))

        def body(a_ref, b_ref, o_ref):
            o_ref[...] = lax.dot_general(a_ref[...].astype(BF16), b_ref[...].astype(BF16), contract,
                                         preferred_element_type=F32).astype(o_ref.dtype)

        b_spec = pl.BlockSpec((k, tn), lambda i, j: (0, j)) if mode == "nn" else pl.BlockSpec((tn, k), lambda i, j: (j, 0))
        return pl.pallas_call(
            body, grid=(m // tm, n // tn),
            in_specs=[pl.BlockSpec((tm, k), lambda i, j: (i, 0)), b_spec],
            out_specs=pl.BlockSpec((tm, tn), lambda i, j: (i, j)),
            out_shape=jax.ShapeDtypeStruct((m, n), out_dtype),
            compiler_params=_params("parallel", "arbitrary"), name=name)(a, b)
    assert mode == "tn"
    r, k1 = a.shape
    k2 = b.shape[1]
    tr = _largest_tile(r, 1024, 8)
    t2 = _largest_tile(k2, 1024, 128)
    nr = r // tr

    def body(a_ref, b_ref, o_ref):
        part = lax.dot_general(a_ref[...].astype(BF16), b_ref[...].astype(BF16), (((0,), (0,)), ((), ())),
                               preferred_element_type=F32)
        i = pl.program_id(1)

        @pl.when(i == 0)
        def _():
            o_ref[...] = part

        @pl.when(i > 0)
        def _():
            o_ref[...] += part

    return pl.pallas_call(
        body, grid=(k2 // t2, nr),
        in_specs=[pl.BlockSpec((tr, k1), lambda j, i: (i, 0)), pl.BlockSpec((tr, t2), lambda j, i: (i, j))],
        out_specs=pl.BlockSpec((k1, t2), lambda j, i: (0, j)),
        out_shape=jax.ShapeDtypeStruct((k1, k2), F32),
        compiler_params=_params("parallel", "arbitrary"), name=name)(a, b)


def _exchange(name, x, bcast, sibling_only=False, chunks=1):
    rels = [1] if sibling_only else list(range(1, N_DEV))
    n_slot = 2 if sibling_only else N_DEV
    blk = x.shape if bcast else x.shape[1:]

    def body(x_ref, o_ref, send_sems, recv_sems, local_sem):
        mx, my, mc = lax.axis_index("x"), lax.axis_index("y"), lax.axis_index("c")
        me = mc if sibling_only else 4 * mx + 2 * my + mc
        me_dev = 4 * mx + 2 * my + mc
        mine = pltpu.make_async_copy(x_ref if bcast else x_ref.at[me_dev], o_ref.at[me], local_sem)
        mine.start()
        copies = []
        rc = blk[0] // chunks
        for k, r in enumerate(rels):
            px = 1 - mx if (r >> 2) & 1 else mx
            py = 1 - my if (r >> 1) & 1 else my
            pc = 1 - mc if r & 1 else mc
            src = x_ref if bcast else x_ref.at[4 * px + 2 * py + pc]
            for ci in range(chunks):
                rows = pl.ds(ci * rc, rc)
                cp = pltpu.make_async_remote_copy(
                    src_ref=src.at[rows], dst_ref=o_ref.at[me, rows], send_sem=send_sems.at[k * chunks + ci],
                    recv_sem=recv_sems.at[k * chunks + ci], device_id=(px, py, pc), device_id_type=pl.DeviceIdType.MESH)
                cp.start()
                copies.append(cp)
        for cp in copies:
            cp.wait()
        mine.wait()

    return pl.pallas_call(
        body, out_shape=jax.ShapeDtypeStruct((n_slot,) + tuple(blk), x.dtype),
        in_specs=[pl.BlockSpec(memory_space=pl.ANY)], out_specs=pl.BlockSpec(memory_space=pl.ANY),
        scratch_shapes=[pltpu.SemaphoreType.DMA((len(rels) * chunks,)), pltpu.SemaphoreType.DMA((len(rels) * chunks,)),
                        pltpu.SemaphoreType.DMA],
        name=name)(x)


def _gather_two_level(name, x):
    def body(x_ref, o_ref, send_sems, recv_sems, local_sem):
        mx, my, mc = lax.axis_index("x"), lax.axis_index("y"), lax.axis_index("c")
        me, sibling = (mx, my, mc), (mx, my, 1 - mc)
        chips = [(1 - mx, my), (mx, 1 - my), (1 - mx, 1 - my)]

        def slot(px, py, pc):
            return o_ref.at[4 * px + 2 * py + pc]

        def copy(k, block, to, src=None):
            return pltpu.make_async_remote_copy(
                src_ref=slot(*block) if src is None else src, dst_ref=slot(*block), send_sem=send_sems.at[k],
                recv_sem=recv_sems.at[k], device_id=to, device_id_type=pl.DeviceIdType.MESH)

        mine = pltpu.make_async_copy(x_ref, slot(*me), local_sem)
        mine.start()
        first = [copy(0, me, sibling, src=x_ref)]
        first += [copy(1 + j, me, (*chip, mc), src=x_ref) for j, chip in enumerate(chips)]
        for cp in first:
            cp.start()
        passed = [copy(4 + j, (*chip, mc), sibling) for j, chip in enumerate(chips)]
        for j, chip in enumerate(chips):
            copy(1 + j, (*chip, mc), me).wait_recv()
            passed[j].start()
        copy(0, sibling, me).wait_recv()
        for j, chip in enumerate(chips):
            copy(4 + j, (*chip, 1 - mc), me).wait_recv()
        for cp in first + passed:
            cp.wait_send()
        mine.wait()

    return pl.pallas_call(
        body, out_shape=jax.ShapeDtypeStruct((N_DEV,) + tuple(x.shape), x.dtype),
        in_specs=[pl.BlockSpec(memory_space=pl.ANY)], out_specs=pl.BlockSpec(memory_space=pl.ANY),
        scratch_shapes=[pltpu.SemaphoreType.DMA((N_DEV - 1,)), pltpu.SemaphoreType.DMA((N_DEV - 1,)),
                        pltpu.SemaphoreType.DMA],
        name=name)(x)


def _sum_slots(name, x):
    s, r, w = x.shape
    tr = _largest_tile(r, 256, 8)

    def body(x_ref, o_ref):
        acc = x_ref[0].astype(F32)
        for j in range(1, s):
            acc = acc + x_ref[j].astype(F32)
        o_ref[...] = acc

    return pl.pallas_call(
        body, grid=(r // tr,), in_specs=[pl.BlockSpec((s, tr, w), lambda i: (0, i, 0))],
        out_specs=pl.BlockSpec((tr, w), lambda i: (i, 0)), out_shape=jax.ShapeDtypeStruct((r, w), F32),
        compiler_params=_params("parallel"), name=name)(x)


def _sigmoid(x):
    return 1.0 / (1.0 + jnp.exp(-x))


def _silu(x):
    return x * _sigmoid(x)


def _silu_grad(x):
    s = _sigmoid(x)
    return s * (1.0 + x * (1.0 - s))


_INV_SQRT2 = 1.0 / math.sqrt(2.0)
_INV_SQRT2PI = 1.0 / math.sqrt(2.0 * math.pi)


def _gelu(x):
    return 0.5 * x * (1.0 + lax.erf(x * _INV_SQRT2))


def _gelu_grad(x):
    return 0.5 * (1.0 + lax.erf(x * _INV_SQRT2)) + x * jnp.exp(-0.5 * x * x) * _INV_SQRT2PI


def _rms_hat(x):
    r = lax.rsqrt(jnp.mean(x * x, axis=-1, keepdims=True) + NORM_EPS)
    return x * r, r


def _rms_bwd(xh, r, dxh):
    return r * (dxh - xh * jnp.mean(dxh * xh, axis=-1, keepdims=True))


def _norm_mod_fwd(name, x, g, scale, shift, n_rows, tr, n_lat):
    def fn(xt, gv, sc, sh):
        xh, _ = _rms_hat(xt)
        return (xh * gv) * (1.0 + sc) + sh

    return _rowwise(name, fn, n_rows, tr, [(x, 0, D_MODEL)], [g, scale, shift], [(D_MODEL, BF16)], [], n_lat=n_lat)[0]


def _norm_mod_bwd(name, x, g, scale, dh, dres, n_rows, tr, n_lat, prev=None):
    nlt = n_lat // tr

    def fn(flag, xt, dht, drt, *rest):
        gv, sc = rest[-2:] if prev is None else rest[1:3]
        xh, r = _rms_hat(xt)
        n = xh * gv
        dn = dht * (1.0 + sc)
        dx = _rms_bwd(xh, r, dn * gv) + flag * drt
        if prev is None:
            return dx, dn * xh, dht * n, dht
        return dx, rest[3] * dx, dn * xh, dht * n, dht, dx * rest[0]

    rows = [(x, 0, D_MODEL), (dh, 0, D_MODEL), (dres, 0, D_MODEL, "clamp", nlt)]
    vecs, row_outs, reds = [g, scale], [(D_MODEL, F32)], [(1, D_MODEL), (2, D_MODEL), (2, D_MODEL)]
    if prev is not None:
        rows.append((prev[0], 0, D_MODEL))
        vecs.append(prev[1])
        row_outs.append((D_MODEL, BF16))
        reds.append((2, D_MODEL))
    return _rowwise(name, fn, n_rows, tr, rows, vecs, row_outs, reds, n_lat=n_lat, want_flag=True)


def _s5_prep(a_re, a_im, log_dt, b_re, b_im, seg_lat, seg_ctx):
    def body(ar_ref, ai_ref, ld_ref, br_ref, bi_ref, abr_ref, abi_ref, bbr_ref, bbi_ref, alr_ref, ali_ref, acr_ref,
             aci_ref):
        lr, li = ar_ref[...], ai_ref[...]
        dt = jnp.exp(ld_ref[...])
        ldr, ldi = lr * dt, li * dt
        e = jnp.exp(ldr)
        abr, abi = e * jnp.cos(ldi), e * jnp.sin(ldi)
        abr_ref[...] = abr
        abi_ref[...] = abi
        den = lr * lr + li * li
        nr, ni = abr - 1.0, abi
        qr = (nr * lr + ni * li) / den
        qi = (ni * lr - nr * li) / den
        br, bi = br_ref[...], bi_ref[...]
        bbr_ref[...] = qr[None] * br - qi[None] * bi
        bbi_ref[...] = qr[None] * bi + qi[None] * br
        for seg, r_ref, i_ref in ((seg_lat, alr_ref, ali_ref), (seg_ctx, acr_ref, aci_ref)):
            es = jnp.exp(ldr * float(seg))
            r_ref[...] = es * jnp.cos(ldi * float(seg))
            i_ref[...] = es * jnp.sin(ldi * float(seg))

    sm = jax.ShapeDtypeStruct(a_re.shape, F32)
    big = jax.ShapeDtypeStruct(b_re.shape, F32)
    return pl.pallas_call(body, out_shape=[sm, sm, big, big, sm, sm, sm, sm], name="s5_prep")(
        a_re, a_im, log_dt, b_re, b_im)


def _s5_prep_bwd(a_re, a_im, log_dt, b_re, b_im, dabr, dabi, dbbr, dbbi):
    def body(ar_ref, ai_ref, ld_ref, br_ref, bi_ref, dabr_ref, dabi_ref, dbbr_ref, dbbi_ref,
             dar_ref, dai_ref, dld_ref, dbr_ref, dbi_ref):
        lr, li = ar_ref[...], ai_ref[...]
        dt = jnp.exp(ld_ref[...])
        ldr, ldi = lr * dt, li * dt
        e = jnp.exp(ldr)
        abr, abi = e * jnp.cos(ldi), e * jnp.sin(ldi)
        den = lr * lr + li * li
        nr, ni = abr - 1.0, abi
        qr = (nr * lr + ni * li) / den
        qi = (ni * lr - nr * li) / den
        br, bi = br_ref[...], bi_ref[...]
        gbr, gbi = dbbr_ref[...], dbbi_ref[...]
        dbr_ref[...] = gbr * qr[None] + gbi * qi[None]
        dbi_ref[...] = gbi * qr[None] - gbr * qi[None]
        dqr = jnp.sum(gbr * br + gbi * bi, axis=0)
        dqi = jnp.sum(gbi * br - gbr * bi, axis=0)
        dnr = (dqr * lr - dqi * li) / den
        dni = (dqr * li + dqi * lr) / den
        dlr_q = (dqr * (nr - 2.0 * lr * qr) + dqi * (ni - 2.0 * lr * qi)) / den
        dli_q = (dqr * (ni - 2.0 * li * qr) + dqi * (-nr - 2.0 * li * qi)) / den
        gar = dabr_ref[...] + dnr
        gai = dabi_ref[...] + dni
        dldr = gar * abr + gai * abi
        dldi = gai * abr - gar * abi
        dar_ref[...] = dldr * dt + dlr_q
        dai_ref[...] = dldi * dt + dli_q
        ddt = jnp.sum(dldr * lr + dldi * li, axis=1, keepdims=True)
        dld_ref[...] = ddt * dt

    sm = jax.ShapeDtypeStruct(a_re.shape, F32)
    big = jax.ShapeDtypeStruct(b_re.shape, F32)
    return pl.pallas_call(body, out_shape=[sm, sm, jax.ShapeDtypeStruct(log_dt.shape, F32), big, big],
                          name="s5_prep_bwd")(a_re, a_im, log_dt, b_re, b_im, dabr, dabi, dbbr, dbbi)


def _slab_cols(v):
    return v.reshape(N_SLAB, 1, HALF_W)


def _slab_pair(vr, vi):
    return jnp.concatenate([_slab_cols(vr), _slab_cols(vi)], axis=-1)


def _slab_in_matrix(bbr, bbi):
    eye = jnp.eye(SLAB_GROUPS, dtype=F32)

    def one(b):
        b = b.reshape(N_SLAB, SLAB_GROUPS, SSM_STATE, SSM_GROUP)
        m = jnp.einsum("sgph,gk->sghkp", b, eye)
        return m.reshape(N_SLAB, SLAB_W, HALF_W)

    return jnp.concatenate([one(bbr), one(bbi)], axis=-1)


def _slab_out_matrix(cr, ci):
    eye = jnp.eye(SLAB_GROUPS, dtype=F32)

    def one(c):
        c = c.reshape(N_SLAB, SLAB_GROUPS, SSM_GROUP, SSM_STATE)
        m = jnp.einsum("sghp,gk->skpgh", c, eye)
        return m.reshape(N_SLAB, HALF_W, SLAB_W)

    return jnp.concatenate([one(cr), one(-ci)], axis=1)


def _slab_diag(m):
    m = m.reshape(N_SLAB, SLAB_GROUPS, SSM_GROUP, 2, SLAB_GROUPS, SSM_STATE)
    d = jnp.stack([m[:, g, :, :, g, :] for g in range(SLAB_GROUPS)], axis=1)
    return d.transpose(3, 0, 1, 2, 4).reshape(2, SSM_GROUPS, SSM_GROUP, SSM_STATE)


def _cmul(ar, ai, xr, xi, conj):
    if conj:
        return ar * xr + ai * xi, ar * xi - ai * xr
    return ar * xr - ai * xi, ar * xi + ai * xr


def _s5_carry(name, z, a_seg, init, descending, conj):
    order = list(range(N_SEG - 1, -1, -1)) if descending else list(range(N_SEG))

    def body(z_ref, a_ref, i_ref, e_ref, o_ref):
        ar, ai = a_ref[:, :HALF_W], a_ref[:, HALF_W:]
        cr, ci = i_ref[:, :HALF_W], i_ref[:, HALF_W:]
        for j in order:
            e_ref[j, :, :HALF_W] = cr
            e_ref[j, :, HALF_W:] = ci
            pr, pi = _cmul(ar, ai, cr, ci, conj)
            cr = pr + z_ref[j, :, :HALF_W]
            ci = pi + z_ref[j, :, HALF_W:]
        o_ref[:, :HALF_W] = cr
        o_ref[:, HALF_W:] = ci

    return pl.pallas_call(body, out_shape=[jax.ShapeDtypeStruct(z.shape, F32), jax.ShapeDtypeStruct(init.shape, F32)],
                          name=name)(z, a_seg, init)


def _seg_major(v):
    return jnp.transpose(v, (1, 0, 2))


def _s5_scan(name, u, n_rows, row0, b_mat, c_mat, abar, h_in, descending, full, y_alias=None, y_rows=None):
    seg = n_rows // N_SEG
    ta = min(32, seg)
    nk = seg // ta
    assert seg * N_SEG == n_rows and nk * ta == seg and row0 % n_rows == 0 and ta % 8 == 0
    rb = row0 // n_rows
    tile = ta * N_SEG

    def body(*refs):
        if full:
            if y_alias is not None:
                u_ref, b_ref, c_ref, a_ref, hin_ref, _, hfin_ref, y_ref, hch_ref, st_ref, up_ref, h_ref = refs
            else:
                u_ref, b_ref, c_ref, a_ref, hin_ref, hfin_ref, y_ref, hch_ref, st_ref, up_ref, h_ref = refs
        else:
            u_ref, b_ref, a_ref, hin_ref, hfin_ref, st_ref, up_ref, h_ref = refs
        k = pl.program_id(1)
        kk = nk - 1 - k if descending else k
        a0 = kk * ta

        @pl.when(k == 0)
        def _():
            st_ref[...] = hin_ref[0]

        if full:
            hch_ref[0, 0] = st_ref[...]
        for al in range(ta):
            up_ref[al * N_SEG:(al + 1) * N_SEG, :] = u_ref[pl.ds(a0 + al, N_SEG, stride=seg), :]
        h_ref[...] = jnp.dot(up_ref[...].astype(BF16), b_ref[0], preferred_element_type=F32)
        ar = jnp.broadcast_to(a_ref[0, :, :HALF_W], (N_SEG, HALF_W))
        ai = jnp.broadcast_to(a_ref[0, :, HALF_W:], (N_SEG, HALF_W))

        def step(i, carry):
            hr, hi = carry
            al = ta - 1 - i if descending else i
            row = pl.multiple_of(al * N_SEG, N_SEG)
            pr, pi = _cmul(ar, ai, hr, hi, False)
            hr = pr + h_ref[pl.ds(row, N_SEG), :HALF_W]
            hi = pi + h_ref[pl.ds(row, N_SEG), HALF_W:]
            if full:
                h_ref[pl.ds(row, N_SEG), :HALF_W] = hr
                h_ref[pl.ds(row, N_SEG), HALF_W:] = hi
            return hr, hi

        hr, hi = lax.fori_loop(0, ta, step, (st_ref[:, :HALF_W], st_ref[:, HALF_W:]))
        st_ref[:, :HALF_W] = hr
        st_ref[:, HALF_W:] = hi
        if full:
            yt = jnp.dot(h_ref[...].astype(BF16), c_ref[0], preferred_element_type=F32)
            for al in range(ta):
                y_ref[pl.ds(a0 + al, N_SEG, stride=seg), :] = yt[al * N_SEG:(al + 1) * N_SEG, :]

        @pl.when(k == nk - 1)
        def _():
            hfin_ref[0] = st_ref[...]

    u_spec = pl.BlockSpec((n_rows, SLAB_W), lambda s, k: (rb, s))
    b_spec = pl.BlockSpec((1, SLAB_W, STATE_W), lambda s, k: (s, 0, 0))
    c_spec = pl.BlockSpec((1, STATE_W, SLAB_W), lambda s, k: (s, 0, 0))
    a_spec = pl.BlockSpec((1, 1, STATE_W), lambda s, k: (s, 0, 0))
    st_spec = pl.BlockSpec((1, N_SEG, STATE_W), lambda s, k: (s, 0, 0))
    st_shape = jax.ShapeDtypeStruct((N_SLAB, N_SEG, STATE_W), F32)
    scratch = [pltpu.VMEM((N_SEG, STATE_W), F32), pltpu.VMEM((tile, SLAB_W), F32), pltpu.VMEM((tile, STATE_W), F32)]
    if not full:
        return pl.pallas_call(
            body, grid=(N_SLAB, nk), in_specs=[u_spec, b_spec, a_spec, st_spec], out_specs=st_spec, out_shape=st_shape,
            scratch_shapes=scratch, compiler_params=_params("parallel", "arbitrary"), name=name)(u, b_mat, abar, h_in)
    kmap = (lambda s, k: (s, nk - 1 - k, 0, 0)) if descending else (lambda s, k: (s, k, 0, 0))
    out_specs = [st_spec, u_spec, pl.BlockSpec((1, 1, N_SEG, STATE_W), kmap)]
    out_shape = [st_shape, jax.ShapeDtypeStruct((y_rows, D_MODEL), F32),
                 jax.ShapeDtypeStruct((N_SLAB, nk, N_SEG, STATE_W), F32)]
    in_specs = [u_spec, b_spec, c_spec, a_spec, st_spec]
    args = [u, b_mat, c_mat, abar, h_in]
    aliases = {}
    if y_alias is not None:
        in_specs.append(pl.BlockSpec(memory_space=pl.ANY))
        args.append(y_alias)
        aliases = {5: 1}
    return pl.pallas_call(
        body, grid=(N_SLAB, nk), in_specs=in_specs, out_specs=out_specs, out_shape=out_shape, scratch_shapes=scratch,
        input_output_aliases=aliases, compiler_params=_params("parallel", "arbitrary"), name=name)(*args)


def _s5_scan_bwd(name, u, dy, n_rows, row0, b_mat, bt_mat, ct_mat, abar, h_chunks, g_in, descending, full,
                 du_alias=None, du_rows=None):
    seg = n_rows // N_SEG
    ta = min(32, seg)
    nk = seg // ta
    rb = row0 // n_rows
    tile = ta * N_SEG
    g_desc = not descending

    def body(*refs):
        if full:
            (u_ref, dy_ref, b_ref, bt_ref, ct_ref, a_ref, hch_ref, gin_ref) = refs[:8]
            rest = refs[9:] if du_alias is not None else refs[8:]
            gfin_ref, du_ref, db_ref, dc_ref, da_ref, st_ref, up_ref, dyp_ref, h_ref, g_ref = rest
        else:
            dy_ref, ct_ref, a_ref, gin_ref, gfin_ref, st_ref, dyp_ref, g_ref = refs
        k = pl.program_id(1)
        kk = nk - 1 - k if g_desc else k
        a0 = kk * ta
        ar = jnp.broadcast_to(a_ref[0, :, :HALF_W], (N_SEG, HALF_W))
        ai = jnp.broadcast_to(a_ref[0, :, HALF_W:], (N_SEG, HALF_W))

        @pl.when(k == 0)
        def _():
            st_ref[...] = gin_ref[0]

        for al in range(ta):
            dyp_ref[al * N_SEG:(al + 1) * N_SEG, :] = dy_ref[pl.ds(a0 + al, N_SEG, stride=seg), :]
        g_ref[...] = jnp.dot(dyp_ref[...].astype(BF16), ct_ref[0], preferred_element_type=F32)

        if full:
            for al in range(ta):
                up_ref[al * N_SEG:(al + 1) * N_SEG, :] = u_ref[pl.ds(a0 + al, N_SEG, stride=seg), :]
            h_ref[...] = jnp.dot(up_ref[...].astype(BF16), b_ref[0], preferred_element_type=F32)
            h0r, h0i = hch_ref[0, 0, :, :HALF_W], hch_ref[0, 0, :, HALF_W:]

            def hstep(i, carry):
                hr, hi = carry
                al = ta - 1 - i if descending else i
                row = pl.multiple_of(al * N_SEG, N_SEG)
                pr, pi = _cmul(ar, ai, hr, hi, False)
                hr = pr + h_ref[pl.ds(row, N_SEG), :HALF_W]
                hi = pi + h_ref[pl.ds(row, N_SEG), HALF_W:]
                h_ref[pl.ds(row, N_SEG), :HALF_W] = hr
                h_ref[pl.ds(row, N_SEG), HALF_W:] = hi
                return hr, hi

            lax.fori_loop(0, ta, hstep, (h0r, h0i))

        def gstep(i, carry):
            gr, gi = carry
            al = ta - 1 - i if g_desc else i
            row = pl.multiple_of(al * N_SEG, N_SEG)
            pr, pi = _cmul(ar, ai, gr, gi, True)
            gr = pr + g_ref[pl.ds(row, N_SEG), :HALF_W]
            gi = pi + g_ref[pl.ds(row, N_SEG), HALF_W:]
            if full:
                g_ref[pl.ds(row, N_SEG), :HALF_W] = gr
                g_ref[pl.ds(row, N_SEG), HALF_W:] = gi
            return gr, gi

        gr, gi = lax.fori_loop(0, ta, gstep, (st_ref[:, :HALF_W], st_ref[:, HALF_W:]))
        st_ref[:, :HALF_W] = gr
        st_ref[:, HALF_W:] = gi

        @pl.when(k == nk - 1)
        def _():
            gfin_ref[0] = st_ref[...]

        if full:
            gb = g_ref[...].astype(BF16)
            dut = jnp.dot(gb, bt_ref[0], preferred_element_type=F32)
            for al in range(ta):
                du_ref[pl.ds(a0 + al, N_SEG, stride=seg), :] = dut[al * N_SEG:(al + 1) * N_SEG, :]
            tn = (((0,), (0,)), ((), ()))
            dbp = lax.dot_general(up_ref[...].astype(BF16), gb, tn, preferred_element_type=F32)
            dcp = lax.dot_general(dyp_ref[...].astype(BF16), h_ref[...].astype(BF16), tn, preferred_element_type=F32)
            inner = (ta - 1) * N_SEG
            if descending:
                g_in_r, g_in_i = g_ref[0:inner, :HALF_W], g_ref[0:inner, HALF_W:]
                p_in_r, p_in_i = h_ref[N_SEG:tile, :HALF_W], h_ref[N_SEG:tile, HALF_W:]
                g_ed_r, g_ed_i = g_ref[inner:tile, :HALF_W], g_ref[inner:tile, HALF_W:]
            else:
                g_in_r, g_in_i = g_ref[N_SEG:tile, :HALF_W], g_ref[N_SEG:tile, HALF_W:]
                p_in_r, p_in_i = h_ref[0:inner, :HALF_W], h_ref[0:inner, HALF_W:]
                g_ed_r, g_ed_i = g_ref[0:N_SEG, :HALF_W], g_ref[0:N_SEG, HALF_W:]
            dar = g_ed_r * h0r + g_ed_i * h0i
            dai = g_ed_i * h0r - g_ed_r * h0i
            if ta > 1:
                dar = dar + jnp.sum((g_in_r * p_in_r + g_in_i * p_in_i).reshape(ta - 1, N_SEG, HALF_W), axis=0)
                dai = dai + jnp.sum((g_in_i * p_in_r - g_in_r * p_in_i).reshape(ta - 1, N_SEG, HALF_W), axis=0)

            @pl.when(k == 0)
            def _():
                db_ref[0] = dbp
                dc_ref[0] = dcp
                da_ref[0, :, :HALF_W] = dar
                da_ref[0, :, HALF_W:] = dai

            @pl.when(k > 0)
            def _():
                db_ref[0] += dbp
                dc_ref[0] += dcp
                da_ref[0, :, :HALF_W] += dar
                da_ref[0, :, HALF_W:] += dai

    u_spec = pl.BlockSpec((n_rows, SLAB_W), lambda s, k: (rb, s))
    m_spec = pl.BlockSpec((1, SLAB_W, STATE_W), lambda s, k: (s, 0, 0))
    mt_spec = pl.BlockSpec((1, STATE_W, SLAB_W), lambda s, k: (s, 0, 0))
    a_spec = pl.BlockSpec((1, 1, STATE_W), lambda s, k: (s, 0, 0))
    st_spec = pl.BlockSpec((1, N_SEG, STATE_W), lambda s, k: (s, 0, 0))
    st_shape = jax.ShapeDtypeStruct((N_SLAB, N_SEG, STATE_W), F32)
    if not full:
        scratch = [pltpu.VMEM((N_SEG, STATE_W), F32), pltpu.VMEM((tile, SLAB_W), F32), pltpu.VMEM((tile, STATE_W), F32)]
        return pl.pallas_call(
            body, grid=(N_SLAB, nk), in_specs=[u_spec, m_spec, a_spec, st_spec], out_specs=st_spec, out_shape=st_shape,
            scratch_shapes=scratch, compiler_params=_params("parallel", "arbitrary"), name=name)(dy, ct_mat, abar, g_in)
    kmap = (lambda s, k: (s, nk - 1 - k, 0, 0)) if g_desc else (lambda s, k: (s, k, 0, 0))
    in_specs = [u_spec, u_spec, m_spec, mt_spec, m_spec, a_spec, pl.BlockSpec((1, 1, N_SEG, STATE_W), kmap), st_spec]
    args = [u, dy, b_mat, bt_mat, ct_mat, abar, h_chunks, g_in]
    aliases = {}
    if du_alias is not None:
        in_specs.append(pl.BlockSpec(memory_space=pl.ANY))
        args.append(du_alias)
        aliases = {8: 1}
    acc_shape = jax.ShapeDtypeStruct((N_SLAB, SLAB_W, STATE_W), F32)
    out_specs = [st_spec, u_spec, m_spec, m_spec, st_spec]
    out_shape = [st_shape, jax.ShapeDtypeStruct((du_rows, D_MODEL), F32), acc_shape, acc_shape, st_shape]
    scratch = [pltpu.VMEM((N_SEG, STATE_W), F32), pltpu.VMEM((tile, SLAB_W), F32), pltpu.VMEM((tile, SLAB_W), F32),
               pltpu.VMEM((tile, STATE_W), F32), pltpu.VMEM((tile, STATE_W), F32)]
    return pl.pallas_call(
        body, grid=(N_SLAB, nk), in_specs=in_specs, out_specs=out_specs, out_shape=out_shape, scratch_shapes=scratch,
        input_output_aliases=aliases, compiler_params=_params("parallel", "arbitrary"), name=name)(*args)


ROPE_HALF = HEAD_DIM // 4
TABLE_W = 2 * HEAD_DIM
Q_SCALE = 1.0 / math.sqrt(HEAD_DIM)
HEADS_PER_BLOCK = 2 * KV_REP
Q_BLOCK_W = HEADS_PER_BLOCK * HEAD_DIM


def _rope_tables(n_lat, n_ctx):
    rows = n_lat // GRID_W
    freqs = ROPE_THETA ** (-jnp.arange(ROPE_HALF, dtype=F32) / ROPE_HALF)
    ang_r = jnp.arange(rows, dtype=F32)[:, None] * freqs[None]
    ang_c = jnp.arange(GRID_W, dtype=F32)[:, None] * freqs[None]
    by_row = lambda v: jnp.repeat(v, GRID_W, axis=0)
    by_col = lambda v: jnp.tile(v, (rows, 1))
    cos = jnp.concatenate([by_row(jnp.cos(ang_r)), by_row(jnp.cos(ang_r)), by_col(jnp.cos(ang_c)), by_col(jnp.cos(ang_c))] * 2,
                          axis=1)
    sin = jnp.concatenate([by_row(jnp.sin(ang_r)), by_row(jnp.sin(ang_r)), by_col(jnp.sin(ang_c)), by_col(jnp.sin(ang_c))] * 2,
                          axis=1)
    cos = jnp.concatenate([cos, jnp.ones((n_ctx, TABLE_W), F32)], axis=0)
    sin = jnp.concatenate([sin, jnp.zeros((n_ctx, TABLE_W), F32)], axis=0)
    return cos, sin


def _rot_half(v):
    w = v.shape[1]
    ahead = pltpu.roll(v, w - ROPE_HALF, axis=1)
    behind = pltpu.roll(v, ROPE_HALF, axis=1)
    lane = lax.broadcasted_iota(jnp.int32, v.shape, 1)
    return jnp.where((lane % (2 * ROPE_HALF)) < ROPE_HALF, -ahead, behind)


def _head_mean(v, sel, selt):
    m = jnp.dot(v, sel, precision=lax.Precision.HIGH, preferred_element_type=F32) * (1.0 / HEAD_DIM)
    return jnp.dot(m, selt, precision=lax.Precision.HIGH, preferred_element_type=F32)


def _head_selectors(n_heads):
    sel = jnp.repeat(jnp.eye(n_heads, dtype=F32), HEAD_DIM, axis=0)
    return sel[None], sel.T[None]


def _head_norm(x, sel, selt):
    r = lax.rsqrt(_head_mean(x * x, sel, selt) + NORM_EPS)
    return x * r, r


def _qk_prep(proj, qn, kn, cos, sin, n, tr):
    qw, kw = _vec(jnp.tile(qn, N_Q_HEADS)), _vec(jnp.tile(kn, N_KV_HEADS))
    sq, sqt = _head_selectors(N_Q_HEADS)
    sk, skt = _head_selectors(N_KV_HEADS)

    def fn(qr, kvr, ct, st, qwv, kwv, s16, s16t, s4, s4t):
        outs = []
        for x, wv, sel, selt, scale in ((qr, qwv, s16, s16t, Q_SCALE), (kvr[:, :KV_W], kwv, s4, s4t, 1.0)):
            reps = x.shape[1] // TABLE_W
            cw, sw = jnp.tile(ct, (1, reps)), jnp.tile(st, (1, reps))
            xh, _ = _head_norm(x, sel, selt)
            nrm = xh * wv
            outs.append((nrm * cw + _rot_half(nrm) * sw) * scale)
        return outs[0], outs[1], kvr[:, KV_W:]

    return _rowwise("l1_qk_prep", fn, n, tr,
                    [(proj, 0, ATTN_W), (proj, 2 * ATTN_W // (2 * KV_W), 2 * KV_W), (cos, 0, TABLE_W), (sin, 0, TABLE_W)],
                    [qw, kw, sq, sqt, sk, skt], [(ATTN_W, BF16), (KV_W, BF16), (KV_W, BF16)], [])


def _qk_prep_bwd(proj, qn, kn, cos, sin, dq, dz, dk, dv, n, n_lat, tr):
    qw, kw = _vec(jnp.tile(qn, N_Q_HEADS)), _vec(jnp.tile(kn, N_KV_HEADS))
    sq, sqt = _head_selectors(N_Q_HEADS)
    sk, skt = _head_selectors(N_KV_HEADS)
    nlt = n_lat // tr

    def fn(flag, qr, kvr, ct, st, dqt, dzt, dkt, dvt, qwv, kwv, s16, s16t, s4, s4t):
        dxs, dws = [], []
        for x, dy, wv, sel, selt in ((qr, dqt * (flag * Q_SCALE), qwv, s16, s16t), (kvr[:, :KV_W], dkt, kwv, s4, s4t)):
            reps = x.shape[1] // TABLE_W
            cw, sw = jnp.tile(ct, (1, reps)), jnp.tile(st, (1, reps))
            xh, r = _head_norm(x, sel, selt)
            dn = dy * cw - _rot_half(dy * sw)
            dxh = dn * wv
            dxs.append(r * (dxh - xh * _head_mean(dxh * xh, sel, selt)))
            dws.append(dn * xh)
        return jnp.concatenate([dxs[0], dzt * flag, dxs[1], dvt], axis=1), dws[0], dws[1]

    dproj, dqw, dkw = _rowwise(
        "l1_qk_prep_bwd", fn, n, tr,
        [(proj, 0, ATTN_W), (proj, 2 * ATTN_W // (2 * KV_W), 2 * KV_W), (cos, 0, TABLE_W), (sin, 0, TABLE_W),
         (dq, 0, ATTN_W, "clamp", nlt), (dz, 0, ATTN_W, "clamp", nlt), (dk, 0, KV_W), (dv, 0, KV_W)],
        [qw, kw, sq, sqt, sk, skt], [(2 * ATTN_W + 2 * KV_W, BF16)], [(1, ATTN_W), (1, KV_W)], n_lat=n_lat, want_flag=True)
    return dproj, dqw.reshape(N_Q_HEADS, HEAD_DIM).sum(0)[None], dkw.reshape(N_KV_HEADS, HEAD_DIM).sum(0)[None]


NT = (((1,), (1,)), ((), ()))


def _attn_fwd(q, k, v, t, tq, tk):
    n = k.shape[0]
    nkc = n // tk

    ts = _largest_tile(tq, 256, LANES)
    items = [(sub, j) for sub in range(tq // ts) for j in range(HEADS_PER_BLOCK)]

    def body(q_ref, k_ref, v_ref, o_ref, lse_ref, s_ref, m_ref, l_ref, acc_ref):
        def lanes(j):
            g = j // KV_REP
            return slice(j * HEAD_DIM, (j + 1) * HEAD_DIM), slice(g * HEAD_DIM, (g + 1) * HEAD_DIM)

        for idx in range(len(items) + 1):
            nxt = items[idx] if idx < len(items) else None
            cur = items[idx - 1] if idx > 0 else None
            sn, sc = idx % 2, (idx - 1) % 2
            if nxt is not None:
                rows_n = slice(nxt[0] * ts, (nxt[0] + 1) * ts)
                ql_n, kl_n = lanes(nxt[1])
                qv = q_ref[rows_n, ql_n]
                m_ref[sn] = jnp.full((ts, LANES), -jnp.inf, F32)
            if cur is not None:
                rows_c = slice(cur[0] * ts, (cur[0] + 1) * ts)
                ql_c, kl_c = lanes(cur[1])
                m_row = jnp.max(m_ref[sc], axis=-1, keepdims=True)
                l_ref[...] = jnp.zeros(l_ref.shape, F32)
                acc_ref[...] = jnp.zeros(acc_ref.shape, F32)

            def sweep(kc, c):
                off = pl.multiple_of(kc * tk, tk)
                if nxt is not None:
                    s = lax.dot_general(qv, k_ref[pl.ds(off, tk), kl_n], NT, preferred_element_type=F32)
                    s_ref[sn, :, pl.ds(off, tk)] = s
                    m = m_ref[sn]
                    for cb in range(tk // LANES):
                        m = jnp.maximum(m, s[:, cb * LANES:(cb + 1) * LANES])
                    m_ref[sn] = m
                if cur is not None:
                    p = jnp.exp(s_ref[sc, :, pl.ds(off, tk)] - m_row)
                    lsum = l_ref[...]
                    for cb in range(tk // LANES):
                        lsum = lsum + p[:, cb * LANES:(cb + 1) * LANES]
                    l_ref[...] = lsum
                    acc_ref[...] += jnp.dot(p.astype(BF16), v_ref[pl.ds(off, tk), kl_c], preferred_element_type=F32)
                return c

            lax.fori_loop(0, nkc, sweep, 0)
            if cur is not None:
                l_row = jnp.sum(l_ref[...], axis=-1, keepdims=True)
                o_ref[rows_c, ql_c] = acc_ref[...] / l_row
                lse_ref[0, rows_c, cur[1]:cur[1] + 1] = m_row + jnp.log(l_row)

    nb = ATTN_W // Q_BLOCK_W
    kspec = pl.BlockSpec((n, LANES), lambda b, i: (0, b))
    return pl.pallas_call(
        body, grid=(nb, t // tq),
        in_specs=[pl.BlockSpec((tq, Q_BLOCK_W), lambda b, i: (i, b)), kspec, kspec],
        out_specs=[pl.BlockSpec((tq, Q_BLOCK_W), lambda b, i: (i, b)),
                   pl.BlockSpec((1, tq, HEADS_PER_BLOCK), lambda b, i: (b, i, 0))],
        out_shape=[jax.ShapeDtypeStruct((t, ATTN_W), F32), jax.ShapeDtypeStruct((nb, t, HEADS_PER_BLOCK), F32)],
        scratch_shapes=[pltpu.VMEM((2, ts, n), F32), pltpu.VMEM((2, ts, LANES), F32), pltpu.VMEM((ts, LANES), F32),
                        pltpu.VMEM((ts, HEAD_DIM), F32)],
        compiler_params=_params("parallel", "parallel"), name="attn_fwd")(q, k, v)


def _attn_bwd(q, k, v, do, o, lse, t, tq, tk):
    n = k.shape[0]
    nkc = n // tk
    tn = (((0,), (0,)), ((), ()))

    def body(q_ref, k_ref, v_ref, do_ref, o_ref, lse_ref, dq_ref, dk_ref, dv_ref, acc_ref):
        @pl.when(pl.program_id(1) == 0)
        def _():
            dk_ref[...] = jnp.zeros(dk_ref.shape, F32)
            dv_ref[...] = jnp.zeros(dv_ref.shape, F32)

        for j0 in range(0, HEADS_PER_BLOCK, 2):
            kl = slice((j0 // KV_REP) * HEAD_DIM, (j0 // KV_REP + 1) * HEAD_DIM)
            heads = []
            for a in range(2):
                j = j0 + a
                ql = slice(j * HEAD_DIM, (j + 1) * HEAD_DIM)
                qv, dov = q_ref[:, ql], do_ref[:, ql]
                dl_v = jnp.sum(dov.astype(F32) * o_ref[:, ql], axis=-1, keepdims=True)
                heads.append((ql, qv, dov, dl_v, lse_ref[0, :, j:j + 1]))
                acc_ref[a] = jnp.zeros((tq, HEAD_DIM), F32)

            def step(kc, c):
                off = pl.multiple_of(kc * tk, tk)
                kt = k_ref[pl.ds(off, tk), kl]
                vt = v_ref[pl.ds(off, tk), kl]
                dv_part, dk_part = None, None
                for a, (_, qv, dov, dl_v, lse_v) in enumerate(heads):
                    s = lax.dot_general(qv, kt, NT, preferred_element_type=F32)
                    p = jnp.exp(s - lse_v)
                    dp = lax.dot_general(dov, vt, NT, preferred_element_type=F32)
                    ds = (p * (dp - dl_v)).astype(BF16)
                    acc_ref[a] += jnp.dot(ds, kt, preferred_element_type=F32)
                    dvp = lax.dot_general(p.astype(BF16), dov, tn, preferred_element_type=F32)
                    dkp = lax.dot_general(ds, qv, tn, preferred_element_type=F32)
                    dv_part = dvp if dv_part is None else dv_part + dvp
                    dk_part = dkp if dk_part is None else dk_part + dkp
                dv_ref[pl.ds(off, tk), kl] += dv_part
                dk_ref[pl.ds(off, tk), kl] += dk_part
                return c

            lax.fori_loop(0, nkc, step, 0)
            for a, h in enumerate(heads):
                dq_ref[:, h[0]] = acc_ref[a]

    nb = ATTN_W // Q_BLOCK_W
    qspec = pl.BlockSpec((tq, Q_BLOCK_W), lambda b, i: (i, b))
    kspec = pl.BlockSpec((n, LANES), lambda b, i: (0, b))
    cspec = pl.BlockSpec((1, tq, HEADS_PER_BLOCK), lambda b, i: (b, i, 0))
    return pl.pallas_call(
        body, grid=(nb, t // tq), in_specs=[qspec, kspec, kspec, qspec, qspec, cspec], out_specs=[qspec, kspec, kspec],
        out_shape=[jax.ShapeDtypeStruct((t, ATTN_W), F32), jax.ShapeDtypeStruct((n, KV_W), F32),
                   jax.ShapeDtypeStruct((n, KV_W), F32)],
        scratch_shapes=[pltpu.VMEM((2, tq, HEAD_DIM), F32)],
        compiler_params=_params("parallel", "arbitrary"), name="attn_bwd")(q, k, v, do, o, lse)


def _s5_system(p, n_lat, n_ctx):
    two_g = 2 * SSM_GROUPS
    a_re = p["ssm_a_re"].reshape(two_g, SSM_STATE)
    a_im = p["ssm_a_im"].reshape(two_g, SSM_STATE)
    log_dt = p["ssm_log_dt"].reshape(two_g, 1)
    b_re = p["ssm_b_re"].reshape(two_g, SSM_STATE, SSM_GROUP).transpose(2, 0, 1)
    b_im = p["ssm_b_im"].reshape(two_g, SSM_STATE, SSM_GROUP).transpose(2, 0, 1)
    raw = (a_re, a_im, log_dt, b_re, b_im)
    abr, abi, bbr, bbi, alr, ali, acr, aci = _s5_prep(*raw, n_lat // N_SEG, n_ctx // N_SEG)
    dirs = []
    for d in range(2):
        g = slice(d * SSM_GROUPS, (d + 1) * SSM_GROUPS)
        b_mat = _slab_in_matrix(bbr[:, g].transpose(1, 2, 0), bbi[:, g].transpose(1, 2, 0))
        c_mat = _slab_out_matrix(p["ssm_c_re"][0, d], p["ssm_c_im"][0, d])
        dirs.append(dict(
            b=b_mat.astype(BF16), bt=b_mat.transpose(0, 2, 1).astype(BF16),
            c=c_mat.astype(BF16), ct=c_mat.transpose(0, 2, 1).astype(BF16),
            abar=_slab_pair(abr[g], abi[g]),
            a_lat=_slab_pair(alr[g], ali[g])[:, 0], a_ctx=_slab_pair(acr[g], aci[g])[:, 0]))
    return raw, dirs


def _s5_forward(proj, dirs, n_lat, n_ctx):
    n = n_lat + n_ctx
    zero_st = jnp.zeros((N_SLAB, N_SEG, STATE_W), F32)
    zero_c = jnp.zeros((N_SLAB, STATE_W), F32)
    ys, saved = [], []
    for d, s in enumerate(dirs):
        desc = d == 1
        tag = f"s5f{d}"
        zc = _s5_scan(tag + "_ctx_ends", proj, n_ctx, n_lat, s["b"], s["c"], s["abar"], zero_st, desc, False)
        ent_c, h0 = _s5_carry(tag + "_ctx_carry", _seg_major(zc), s["a_ctx"], zero_c, desc, False)
        _, y, hch_c = _s5_scan(tag + "_ctx", proj, n_ctx, n_lat, s["b"], s["c"], s["abar"], _seg_major(ent_c), desc,
                               True, y_rows=n)
        zl = _s5_scan(tag + "_lat_ends", proj, n_lat, 0, s["b"], s["c"], s["abar"], zero_st, desc, False)
        ent_l, _ = _s5_carry(tag + "_lat_carry", _seg_major(zl), s["a_lat"], h0, desc, False)
        _, y, hch_l = _s5_scan(tag + "_lat", proj, n_lat, 0, s["b"], s["c"], s["abar"], _seg_major(ent_l), desc,
                               True, y_alias=y, y_rows=n)
        ys.append(y)
        saved.append((hch_l, hch_c))
    return ys, saved


def _s5_backward(proj, dy, dirs, saved, n_lat, n_ctx):
    n = n_lat + n_ctx
    zero_st = jnp.zeros((N_SLAB, N_SEG, STATE_W), F32)
    zero_c = jnp.zeros((N_SLAB, STATE_W), F32)
    out = []
    for d, s in enumerate(dirs):
        desc = d == 1
        tag = f"s5b{d}"
        hch_l, hch_c = saved[d]
        gl = _s5_scan_bwd(tag + "_lat_ends", proj, dy, n_lat, 0, s["b"], s["bt"], s["ct"], s["abar"], None, zero_st,
                          desc, False)
        ent_l, g0 = _s5_carry(tag + "_lat_carry", _seg_major(gl), s["a_lat"], zero_c, not desc, True)
        _, du, db_l, dc_l, da_l = _s5_scan_bwd(tag + "_lat", proj, dy, n_lat, 0, s["b"], s["bt"], s["ct"], s["abar"],
                                               hch_l, _seg_major(ent_l), desc, True, du_rows=n)
        gc = _s5_scan_bwd(tag + "_ctx_ends", proj, dy, n_ctx, n_lat, s["b"], s["bt"], s["ct"], s["abar"], None, zero_st,
                          desc, False)
        ent_c, _ = _s5_carry(tag + "_ctx_carry", _seg_major(gc), s["a_ctx"], g0, not desc, True)
        _, du, db_c, dc_c, da_c = _s5_scan_bwd(tag + "_ctx", proj, dy, n_ctx, n_lat, s["b"], s["bt"], s["ct"], s["abar"],
                                               hch_c, _seg_major(ent_c), desc, True, du_alias=du, du_rows=n)
        out.append((du, db_l + db_c, dc_l + dc_c, da_l + da_c))
    return out


def _s5_param_grads(raw, bwd):
    dabr, dabi, dbbr, dbbi, dcr, dci = [], [], [], [], [], []
    for _, db, dc, da in bwd:
        da = jnp.sum(da, axis=1)
        dabr.append(da[:, :HALF_W].reshape(SSM_GROUPS, SSM_STATE))
        dabi.append(da[:, HALF_W:].reshape(SSM_GROUPS, SSM_STATE))
        dbd = _slab_diag(db)
        dbbr.append(dbd[0].transpose(1, 0, 2))
        dbbi.append(dbd[1].transpose(1, 0, 2))
        dcd = _slab_diag(dc)
        dcr.append(dcd[0])
        dci.append(-dcd[1])
    cat = lambda xs, ax: jnp.concatenate(xs, axis=ax)
    dar, dai, dld, dbr, dbi = _s5_prep_bwd(*raw, cat(dabr, 0), cat(dabi, 0), cat(dbbr, 1), cat(dbbi, 1))
    shp = (1, 2, SSM_GROUPS, SSM_STATE)
    b_shape = (1, 2, SSM_GROUPS, SSM_STATE, SSM_GROUP)
    return dict(
        ssm_a_re=dar.reshape(shp), ssm_a_im=dai.reshape(shp), ssm_log_dt=dld.reshape(1, 2, SSM_GROUPS),
        ssm_b_re=dbr.transpose(1, 2, 0).reshape(b_shape), ssm_b_im=dbi.transpose(1, 2, 0).reshape(b_shape),
        ssm_c_re=jnp.stack(dcr)[None], ssm_c_im=jnp.stack(dci)[None])


def _example_step(x, ctx, target, mods, w, p):
    t, c = x.shape[0], ctx.shape[0]
    n = t + c
    assert t % c == 0 and c % LANES == 0 and c % (8 * N_SEG) == 0 and t % GRID_W == 0
    tr = _largest_tile(c, 256, 8)
    xall = jnp.concatenate([x, ctx], axis=0)
    g0, g1 = _vec(p["norm_g"][0]), _vec(p["norm_g"][1])
    (shift0, scale0, gate0), (shift1, scale1, gate1) = [tuple(_vec(v) for v in m) for m in mods]

    h0 = _norm_mod_fwd("l0_norm", xall, g0, scale0, shift0, n, tr, t)
    proj0 = _mm("l0_in", h0, w["ssm_w_in"], "nn")
    raw, dirs = _s5_system(p, t, c)
    (y_f, y_r), saved = _s5_forward(proj0, dirs, t, c)
    d_skip = _vec(p["ssm_d"][0])

    def post_a(u, yf, yr, dv):
        y = u * dv + yf + yr
        return y, _gelu(y)

    y0, yg = _rowwise("l0_gelu", post_a, n, tr, [(proj0, 0, D_MODEL), (y_f, 0, D_MODEL), (y_r, 0, D_MODEL)], [d_skip],
                      [(D_MODEL, F32), (D_MODEL, F32)], [])
    tg = _mm("l0_glu", yg, w["ssm_w_glu"], "nn")
    b_glu = _vec(p["ssm_b_glu"][0])

    def post_b(ygt, tt, zt, bv):
        return ygt * _sigmoid(tt + bv) * _silu(zt)

    gz0 = _rowwise("l0_gate", post_b, n, tr, [(yg, 0, D_MODEL), (tg, 0, D_MODEL), (proj0, 1, D_MODEL)], [b_glu],
                   [(D_MODEL, BF16)], [])[0]
    out0 = _mm("l0_out", gz0, w["ssm_w_out"], "nn")

    def res_norm(xt, ot, gv, g1v, sc, sh):
        x1t = xt + gv * ot
        xh, _ = _rms_hat(x1t)
        return x1t, (xh * g1v) * (1.0 + sc) + sh

    x1, h1 = _rowwise("l0_res_l1_norm", res_norm, n, tr, [(xall, 0, D_MODEL), (out0, 0, D_MODEL)],
                      [gate0, g1, scale1, shift1], [(D_MODEL, F32), (D_MODEL, BF16)], [], n_lat=t)
    proj1 = _mm("l1_in", h1, w["attn_w_in"], "nn")
    cos, sin = _rope_tables(t, c)
    qn, kn = p["attn_q_norm"][0], p["attn_k_norm"][0]
    q_h, k_h, v_h = _qk_prep(proj1, qn, kn, cos, sin, n, tr)
    tq = _largest_tile(t, 512, LANES)
    o, lse = _attn_fwd(q_h, k_h, v_h, t, tq, _largest_tile(n, 2816, LANES))
    gz1 = _rowwise("l1_gate", lambda ot, zt: ot * _silu(zt), t, tr, [(o, 0, D_MODEL), (proj1, 1, D_MODEL)], [],
                   [(D_MODEL, BF16)], [])[0]
    out1 = _mm("l1_out", gz1, w["attn_w_out"], "nn")

    gf = _vec(p["final_norm_g"])

    def head(x1t, o1t, tgt, g1v, gfv):
        x2 = x1t + g1v * o1t
        xh, r = _rms_hat(x2)
        e = xh * gfv - tgt
        dyf = e * (1.0 / D_MODEL)
        dx2 = _rms_bwd(xh, r, dyf * gfv)
        return dx2, g1v * dx2, dyf * xh, dx2 * o1t, jnp.sum(e * e, axis=1, keepdims=True)

    gate1_lat = gate1[0:1]
    dx2, dout1, d_gf, d_gate1, sq = _rowwise(
        "head", head, t, tr, [(x1, 0, D_MODEL), (out1, 0, D_MODEL), (target, 0, D_MODEL)], [gate1_lat, gf],
        [(D_MODEL, F32), (D_MODEL, BF16)], [(1, D_MODEL), (1, D_MODEL), (1, 1)])

    d_w_attn_out = _mm("l1_out_dw", gz1, dout1, "tn")
    dgz1 = _mm("l1_out_dx", dout1, w["attn_w_out"], "nt")

    def gate1_bwd(dgt, ot, zt):
        return dgt * _silu(zt), dgt * ot * _silu_grad(zt)

    do, dz1 = _rowwise("l1_gate_bwd", gate1_bwd, t, tr, [(dgz1, 0, D_MODEL), (o, 0, D_MODEL), (proj1, 1, D_MODEL)], [],
                       [(D_MODEL, BF16), (D_MODEL, F32)], [])
    dq_s, dk, dv = _attn_bwd(q_h, k_h, v_h, do, o, lse, t, tq, _largest_tile(n, 1024, LANES))
    dproj1, d_qn, d_kn = _qk_prep_bwd(proj1, qn, kn, cos, sin, dq_s, dz1, dk, dv, n, t, tr)
    d_w_attn_in = _mm("l1_in_dw", h1, dproj1, "tn")
    dh1 = _mm("l1_in_dx", dproj1, w["attn_w_in"], "nt")
    dx1, dout0, d_g1, d_scale1, d_shift1, d_gate0 = _norm_mod_bwd("l1_norm_bwd", x1, g1, scale1, dh1, dx2, n, tr, t,
                                                                  prev=(out0, gate0))

    d_w_out = _mm("l0_out_dw", gz0, dout0, "tn")
    dgz0 = _mm("l0_out_dx", dout0, w["ssm_w_out"], "nt")

    def post_b_bwd(dgt, ygt, tt, zt, bv):
        s = _sigmoid(tt + bv)
        dy2 = dgt * _silu(zt)
        dt = dy2 * ygt * s * (1.0 - s)
        return dgt * (ygt * s) * _silu_grad(zt), dt, dy2 * s, dt

    dz0, dtg, dyg_a, d_b_glu = _rowwise(
        "l0_gate_bwd", post_b_bwd, n, tr, [(dgz0, 0, D_MODEL), (yg, 0, D_MODEL), (tg, 0, D_MODEL), (proj0, 1, D_MODEL)],
        [b_glu], [(D_MODEL, BF16), (D_MODEL, BF16), (D_MODEL, F32)], [(1, D_MODEL)])
    d_w_glu = _mm("l0_glu_dw", yg, dtg, "tn")
    dyg_b = _mm("l0_glu_dx", dtg, w["ssm_w_glu"], "nt")

    def post_a_bwd(da, db, yt, ut, dv):
        dy = (da + db) * _gelu_grad(yt)
        return dy, dy * dv, dy * ut

    dy0, du_skip, d_d = _rowwise("l0_gelu_bwd", post_a_bwd, n, tr,
                                 [(dyg_a, 0, D_MODEL), (dyg_b, 0, D_MODEL), (y0, 0, D_MODEL), (proj0, 0, D_MODEL)], [d_skip],
                                 [(D_MODEL, F32), (D_MODEL, F32)], [(1, D_MODEL)])
    s5_bwd = _s5_backward(proj0, dy0, dirs, saved, t, c)
    dproj0 = _rowwise("l0_in_grad", lambda a, b, cc, dz: jnp.concatenate([a + b + cc, dz], axis=1), n, tr,
                      [(du_skip, 0, D_MODEL), (s5_bwd[0][0], 0, D_MODEL), (s5_bwd[1][0], 0, D_MODEL), (dz0, 0, D_MODEL)], [],
                      [(2 * D_MODEL, BF16)], [])[0]
    d_w_in = _mm("l0_in_dw", h0, dproj0, "tn")
    dh0 = _mm("l0_in_dx", dproj0, w["ssm_w_in"], "nt")
    dx0, d_g0, d_scale0, d_shift0 = _norm_mod_bwd("l0_norm_bwd", xall, g0, scale0, dh0, dx1, n, tr, t)

    big = dict(ssm_w_in=d_w_in, ssm_w_glu=d_w_glu, ssm_w_out=d_w_out, attn_w_in=d_w_attn_in, attn_w_out=d_w_attn_out)
    small = dict(
        norm_g=jnp.concatenate([d_g0[0], d_g1[0]], axis=0), ssm_d=d_d[0], ssm_b_glu=d_b_glu[0],
        attn_q_norm=d_qn, attn_k_norm=d_kn, final_norm_g=d_gf[0, 0], **_s5_param_grads(raw, s5_bwd))
    zero_v = jnp.zeros((D_MODEL,), F32)
    d_mod_lat = jnp.stack([jnp.concatenate([d_shift0[0, 0], d_scale0[0, 0], d_gate0[0, 0]]),
                           jnp.concatenate([d_shift1[0, 0], d_scale1[0, 0], d_gate1[0, 0]])])
    d_mod_ctx = jnp.stack([jnp.concatenate([d_shift0[1, 0], d_scale0[1, 0], d_gate0[1, 0]]),
                           jnp.concatenate([d_shift1[1, 0], d_scale1[1, 0], zero_v])])
    return sq[0, 0, 0], dx0[:t], big, small, d_mod_lat, d_mod_ctx


def _adamw(name, w, g, m, v):
    rows, cols = w.shape
    tr = _largest_tile(rows, 256, 8)
    c1 = 1.0 / (1.0 - ADAM_B1 ** ADAM_STEP)
    c2 = 1.0 / (1.0 - ADAM_B2 ** ADAM_STEP)

    def fn(wt, gt, mt, vt):
        mn = ADAM_B1 * mt + (1.0 - ADAM_B1) * gt
        vn = ADAM_B2 * vt + (1.0 - ADAM_B2) * (gt * gt)
        delta = -ADAM_LR * ((mn * c1) / (jnp.sqrt(vn * c2) + ADAM_EPS) + ADAM_WD * wt)
        return delta, mn, vn

    return _rowwise(name, fn, rows, tr, [(a, 0, cols) for a in (w, g, m, v)], [], [(cols, F32)] * 3, [])


BIG = ("ssm_w_in", "ssm_w_glu", "ssm_w_out", "attn_w_in", "attn_w_out")
COL_SHARDED = ("ssm_w_in", "attn_w_in")
WEIGHTS = ("c_ctx", "w_mod", "b_mod", "norm_g", "ssm_w_in", "ssm_a_re", "ssm_a_im", "ssm_log_dt", "ssm_b_re", "ssm_b_im",
           "ssm_c_re", "ssm_c_im", "ssm_d", "ssm_w_glu", "ssm_b_glu", "ssm_w_out", "attn_w_in", "attn_q_norm",
           "attn_k_norm", "attn_w_out", "final_norm_g")
SMALL = tuple(k for k in WEIGHTS if k not in BIG and k != "w_mod")
PACK_W = 1024


def _attn_in_perm(x, inverse):
    a, kv = ATTN_W, 2 * KV_W
    if inverse:
        return jnp.concatenate([x[..., :a], x[..., 2 * a:], x[..., a:2 * a]], axis=-1)
    return jnp.concatenate([x[..., :a], x[..., a + kv:], x[..., a:a + kv]], axis=-1)


def _pack(arrays, dtype, row_unit):
    flat = jnp.concatenate([a.reshape(-1).astype(dtype) for a in arrays])
    rows = -(-flat.shape[0] // PACK_W)
    rows = -(-rows // row_unit) * row_unit
    flat = jnp.concatenate([flat, jnp.zeros((rows * PACK_W - flat.shape[0],), dtype)])
    return flat.reshape(rows, PACK_W)


def _unpack(buf, shapes):
    lead = buf.shape[:-2]
    flat = buf.reshape(lead + (-1,))
    out, off = [], 0
    for shp in shapes:
        size = math.prod(shp)
        out.append(flat[..., off:off + size].reshape(lead + tuple(shp)))
        off += size
    return out


def _half_shape(name, shard_shape):
    r, ccols = shard_shape
    return (r // 2, ccols)


def kernel(x, c, ctx, c_ctx, w_mod, b_mod, norm_g, ssm_w_in, ssm_a_re, ssm_a_im, ssm_log_dt, ssm_b_re, ssm_b_im, ssm_c_re, ssm_c_im, ssm_d, ssm_w_glu, ssm_b_glu, ssm_w_out, attn_w_in, attn_q_norm, attn_k_norm, attn_w_out, final_norm_g, loss_target, m_c_ctx, m_w_mod, m_b_mod, m_norm_g, m_ssm_w_in, m_ssm_a_re, m_ssm_a_im, m_ssm_log_dt, m_ssm_b_re, m_ssm_b_im, m_ssm_c_re, m_ssm_c_im, m_ssm_d, m_ssm_w_glu, m_ssm_b_glu, m_ssm_w_out, m_attn_w_in, m_attn_q_norm, m_attn_k_norm, m_attn_w_out, m_final_norm_g, v_c_ctx, v_w_mod, v_b_mod, v_norm_g, v_ssm_w_in, v_ssm_a_re, v_ssm_a_im, v_ssm_log_dt, v_ssm_b_re, v_ssm_b_im, v_ssm_c_re, v_ssm_c_im, v_ssm_d, v_ssm_w_glu, v_ssm_b_glu, v_ssm_w_out, v_attn_w_in, v_attn_q_norm, v_attn_k_norm, v_attn_w_out, v_final_norm_g):
    args = dict(locals())
    wts = {k: args[k] for k in WEIGHTS}
    mom_m = {k: args["m_" + k] for k in WEIGHTS}
    mom_v = {k: args["v_" + k] for k in WEIGHTS}
    mx, my, mc = lax.axis_index("x"), lax.axis_index("y"), lax.axis_index("c")
    chip = 2 * mx + my
    me = 2 * chip + mc

    halves = []
    for k in BIG:
        sh = wts[k][0]
        hr = sh.shape[0] // 2
        halves.append(lax.dynamic_slice_in_dim(sh, mc * hr, hr, axis=0))
    gathered = _gather_two_level("gather_weights", _pack(halves, BF16, 16))
    parts = _unpack(gathered, [h.shape for h in halves])
    w_full = {}
    for k, pc in zip(BIG, parts):
        hr, cols = pc.shape[1:]
        pc = pc.reshape(N_CHIP, 2, hr, cols)
        if k in COL_SHARDED:
            w_full[k] = pc.transpose(1, 2, 0, 3).reshape(2 * hr, N_CHIP * cols)
        else:
            w_full[k] = pc.reshape(N_CHIP * 2 * hr, cols)
    w_full["attn_w_in"] = _attn_in_perm(w_full["attn_w_in"], False)

    c_blk = jnp.concatenate([c, jnp.zeros((7, D_MODEL), F32)], axis=0)
    c_all = _exchange("gather_c", c_blk, True)[:, 0]
    cond = jnp.concatenate([c_all, c_ctx[None], jnp.zeros((7, D_MODEL), F32)], axis=0)
    s_cond, ds_cond = _rowwise("cond_silu", lambda t: (_silu(t), _silu_grad(t)), 16, 16, [(cond, 0, D_MODEL)], [],
                               [(D_MODEL, F32), (D_MODEL, F32)], [])
    w_mod_b = w_mod.astype(BF16)
    mcols = w_mod.shape[2]
    mod_part = jnp.stack([_mm(f"mod{i}", s_cond, w_mod_b[i], "nn") for i in range(2)])
    mod_g = _exchange("gather_mod", mod_part.reshape(32, mcols), True)
    mod_all = mod_g.reshape(N_CHIP, 2, 2, 16, mcols)[:, 0]
    mod_all = mod_all.transpose(1, 2, 0, 3).reshape(2, 16, N_CHIP * mcols) + b_mod[:, None, :]
    mods = []
    for i in range(2):
        lat = lax.dynamic_slice_in_dim(mod_all[i], me, 1, axis=0)[0]
        both = jnp.stack([lat, mod_all[i, 8]])
        mods.append((both[:, :D_MODEL], both[:, D_MODEL:2 * D_MODEL], both[:, 2 * D_MODEL:]))

    small_p = {k: wts[k] for k in SMALL if k != "c_ctx" and k != "b_mod"}
    sq, grad_x, big_g, small_g, d_mod_lat, d_mod_ctx = _example_step(x[0], ctx[0], loss_target[0], mods, w_full, small_p)
    loss = lax.psum(0.5 / D_MODEL * sq, ("x", "y", "c"))
    big_g["attn_w_in"] = _attn_in_perm(big_g["attn_w_in"], True)

    small_names = [k for k in SMALL if k not in ("c_ctx", "b_mod")]
    small_list = [small_g[k] for k in small_names] + [d_mod_lat, d_mod_ctx]
    small_shapes = [wts[k].shape for k in small_names] + [d_mod_lat.shape, d_mod_ctx.shape]
    packed = _pack(small_list, F32, 8 * N_DEV)
    slice_rows = packed.shape[0] // N_DEV
    slices = _exchange("scatter_small", packed.reshape(N_DEV, slice_rows, PACK_W), False)
    my_sum = _sum_slots("sum_small", slices)
    payload = jnp.concatenate([my_sum, _pack([d_mod_lat], F32, 8)], axis=0)
    sg = _exchange("gather_small", payload, True)
    summed = _unpack(sg[:, :slice_rows].reshape(packed.shape), small_shapes)
    grads = dict(zip(small_names, summed[:-2]))
    d_mod_lat_sum, d_mod_ctx_sum = summed[-2], summed[-1]
    grads["b_mod"] = d_mod_lat_sum + d_mod_ctx_sum
    d_mod_lat_all = _unpack(sg[:, slice_rows:], [d_mod_lat.shape])[0]

    g_w_mod, ds_cc = [], []
    for i in range(2):
        rows9 = jnp.concatenate([d_mod_lat_all[:, i], d_mod_ctx_sum[i][None], jnp.zeros((7, 3 * D_MODEL), F32)], axis=0)
        mine = lax.dynamic_slice_in_dim(rows9, chip * mcols, mcols, axis=1)
        g_w_mod.append(_mm(f"mod{i}_dw", s_cond, mine, "tn"))
        ds_cc.append(_mm(f"mod{i}_dx", mine, w_mod_b[i], "nt")[8])
    grads["w_mod"] = jnp.stack(g_w_mod)
    part = (ds_cc[0] + ds_cc[1]) * jnp.where(mc == 0, 1.0, 0.0)
    part_blk = jnp.concatenate([part[None], jnp.zeros((7, D_MODEL), F32)], axis=0)
    ds_all = _sum_slots("sum_c_ctx", _exchange("gather_c_ctx", part_blk, True))
    grads["c_ctx"] = ds_all[0] * ds_cond[8]

    blocks = []
    for k in BIG:
        g = big_g[k]
        rows, cols = g.shape
        if k in COL_SHARDED:
            blocks.append(g.reshape(2, rows // 2, N_CHIP, cols // N_CHIP).transpose(2, 0, 1, 3).reshape(N_DEV, -1))
        else:
            blocks.append(g.reshape(N_DEV, -1))
    sendbuf = jnp.concatenate(blocks, axis=1).astype(BF16)
    sendbuf = sendbuf.reshape(N_DEV, -1, PACK_W)
    recv = _exchange("scatter_big", sendbuf, False)
    mine = _sum_slots("sum_big", recv)
    both = _exchange("swap_halves", mine, True, sibling_only=True, chunks=8)
    half_shapes = [(wts[k].shape[1] // 2, wts[k].shape[2]) for k in BIG]
    for k, pc in zip(BIG, _unpack(both, half_shapes)):
        grads[k] = pc.reshape(wts[k].shape)

    delta, new_m, new_v = {}, {}, {}
    for k in BIG + ("w_mod",):
        shp = wts[k].shape
        two_d = (-1, shp[-1])
        res = _adamw("adamw_" + k, *[a.reshape(two_d) for a in (wts[k], grads[k], mom_m[k], mom_v[k])])
        delta[k], new_m[k], new_v[k] = [r.reshape(shp) for r in res]
    shapes = [wts[k].shape for k in SMALL]
    packed = [_pack([d[k] for k in SMALL], F32, 8) for d in (wts, grads, mom_m, mom_v)]
    res = _adamw("adamw_small", *packed)
    for dst, buf in zip((delta, new_m, new_v), res):
        for k, a in zip(SMALL, _unpack(buf, shapes)):
            dst[k] = a
    grads = {k: grads[k].reshape(wts[k].shape) for k in WEIGHTS}
    return (loss, grad_x[None], *[grads[k] for k in WEIGHTS], *[delta[k] for k in WEIGHTS],
            *[new_m[k] for k in WEIGHTS], *[new_v[k] for k in WEIGHTS])
```

```python
import functools
import math

import jax
import jax.numpy as jnp
from jax import lax
from jax.experimental import pallas as pl
from jax.experimental.pallas import tpu as pltpu

F32 = jnp.float32
BF16 = jnp.bfloat16

D_MODEL = 1024
NORM_EPS = 1e-6
SSM_GROUPS = 64
SSM_GROUP = 16
SSM_STATE = 64
LANES = 128
SLAB_W = LANES
N_SLAB = D_MODEL // SLAB_W
SLAB_GROUPS = SLAB_W // SSM_GROUP
HALF_W = SLAB_GROUPS * SSM_STATE
STATE_W = 2 * HALF_W
N_SEG = 8
HEAD_DIM = 64
N_Q_HEADS = 16
N_KV_HEADS = 4
KV_REP = N_Q_HEADS // N_KV_HEADS
ATTN_W = N_Q_HEADS * HEAD_DIM
KV_W = N_KV_HEADS * HEAD_DIM
GRID_W = 64
ROPE_THETA = 10000.0
N_DEV = 8
N_CHIP = 4
VMEM_LIMIT_BYTES = 56 * 1024 * 1024

ADAM_LR = 0.001
ADAM_B1 = 0.9
ADAM_B2 = 0.999
ADAM_EPS = 1e-08
ADAM_WD = 0.01
ADAM_STEP = 10


def _params(*sem):
    return pltpu.CompilerParams(dimension_semantics=sem, vmem_limit_bytes=VMEM_LIMIT_BYTES)


def _largest_tile(n, cap, unit):
    if n <= cap:
        return n
    t = (cap // unit) * unit
    while t >= unit:
        if n % t == 0:
            return t
        t -= unit
    raise ValueError(f"no tile for {n} (cap {cap}, unit {unit})")


def _rowwise(name, fn, n_rows, tr, row_ins, vec_ins, row_outs, red_outs, n_lat=None, want_flag=False):
    nt = n_rows // tr
    assert nt * tr == n_rows
    nlt = nt if n_lat is None else n_lat // tr

    def sel(i):
        return jnp.where(i >= nlt, 1, 0)

    arrays, in_specs = [], []
    for spec in row_ins:
        arr, cb, w = spec[:3]
        kind = spec[3] if len(spec) > 3 else None
        m = spec[4] if len(spec) > 4 else None
        if kind == "mod":
            imap = functools.partial(lambda i, cb, m: (i % m, cb), cb=cb, m=m)
        elif kind == "clamp":
            imap = functools.partial(lambda i, cb, m: (jnp.minimum(i, m - 1), cb), cb=cb, m=m)
        else:
            imap = functools.partial(lambda i, cb: (i, cb), cb=cb)
        arrays.append(arr)
        in_specs.append(pl.BlockSpec((tr, w), imap))
    for v in vec_ins:
        s, a, w = v.shape
        imap = (lambda i: (sel(i), 0, 0)) if s == 2 else (lambda i: (0, 0, 0))
        arrays.append(v)
        in_specs.append(pl.BlockSpec((1, a, w), imap))
    out_shapes, out_specs = [], []
    for w, dt in row_outs:
        out_shapes.append(jax.ShapeDtypeStruct((n_rows, w), dt))
        out_specs.append(pl.BlockSpec((tr, w), lambda i: (i, 0)))
    for s, w in red_outs:
        out_shapes.append(jax.ShapeDtypeStruct((s, 1, w), F32))
        imap = (lambda i: (sel(i), 0, 0)) if s == 2 else (lambda i: (0, 0, 0))
        out_specs.append(pl.BlockSpec((1, 1, w), imap))
    n_ri, n_vi, n_ro, n_rd = len(row_ins), len(vec_ins), len(row_outs), len(red_outs)

    def body(*refs):
        i = pl.program_id(0)
        rows = [r[...] for r in refs[:n_ri]]
        vecs = [r[0] for r in refs[n_ri:n_ri + n_vi]]
        outs = refs[n_ri + n_vi:]
        lead = [jnp.where(i < nlt, 1.0, 0.0).astype(F32)] if want_flag else []
        res = fn(*lead, *rows, *vecs)
        if not isinstance(res, (tuple, list)):
            res = (res,)
        assert len(res) == n_ro + n_rd
        for k in range(n_ro):
            outs[k][...] = res[k].astype(outs[k].dtype)
        for k in range(n_rd):
            part = jnp.sum(res[n_ro + k].astype(F32), axis=0, keepdims=True)
            first = i == 0
            if red_outs[k][0] == 2:
                first = jnp.logical_or(first, i == nlt)
            o = outs[n_ro + k]

            @pl.when(first)
            def _():
                o[0] = part

            @pl.when(jnp.logical_not(first))
            def _():
                o[0] = o[0] + part

    res = pl.pallas_call(
        body, grid=(nt,), in_specs=in_specs, out_specs=out_specs, out_shape=out_shapes,
        compiler_params=_params("arbitrary"), name=name)(*arrays)
    return res


def _vec(v):
    v = v.astype(F32)
    if v.ndim == 1:
        v = v[None]
    return v[:, None, :]


def _mm(name, a, b, mode, out_dtype=F32):
    if mode in ("nn", "nt"):
        m, k = a.shape
        n = b.shape[1] if mode == "nn" else b.shape[0]
        tm = _largest_tile(m, 1024, 8)
        tn = _largest_tile(n, 1024, 128)
        contract = (((1,), (0,)), ((), ())) if mode == "nn" else (((1,), (1,)), ((), ()))

        def body(a_ref, b_ref, o_ref):
            o_ref[...] = lax.dot_general(a_ref[...].astype(BF16), b_ref[...].astype(BF16), contract,
                                         preferred_element_type=F32).astype(o_ref.dtype)

        b_spec = pl.BlockSpec((k, tn), lambda i, j: (0, j)) if mode == "nn" else pl.BlockSpec((tn, k), lambda i, j: (j, 0))
        return pl.pallas_call(
            body, grid=(m // tm, n // tn),
            in_specs=[pl.BlockSpec((tm, k), lambda i, j: (i, 0)), b_spec],
            out_specs=pl.BlockSpec((tm, tn), lambda i, j: (i, j)),
            out_shape=jax.ShapeDtypeStruct((m, n), out_dtype),
            compiler_params=_params("parallel", "arbitrary"), name=name)(a, b)
    assert mode == "tn"
    r, k1 = a.shape
    k2 = b.shape[1]
    tr = _largest_tile(r, 1024, 8)
    t2 = _largest_tile(k2, 1024, 128)
    nr = r // tr

    def body(a_ref, b_ref, o_ref, acc_ref):
        part = lax.dot_general(a_ref[...].astype(BF16), b_ref[...].astype(BF16), (((0,), (0,)), ((), ())),
                               preferred_element_type=F32)
        i = pl.program_id(1)

        @pl.when(i == 0)
        def _():
            acc_ref[...] = part

        @pl.when(i > 0)
        def _():
            acc_ref[...] += part

        @pl.when(i == nr - 1)
        def _():
            o_ref[...] = acc_ref[...].astype(o_ref.dtype)

    return pl.pallas_call(
        body, grid=(k2 // t2, nr),
        in_specs=[pl.BlockSpec((tr, k1), lambda j, i: (i, 0)), pl.BlockSpec((tr, t2), lambda j, i: (i, j))],
        out_specs=pl.BlockSpec((k1, t2), lambda j, i: (0, j)),
        out_shape=jax.ShapeDtypeStruct((k1, k2), out_dtype),
        scratch_shapes=[pltpu.VMEM((k1, t2), F32)],
        compiler_params=_params("parallel", "arbitrary"), name=name)(a, b)


def _exchange(name, x, bcast, sibling_only=False, chunks=1):
    rels = [1] if sibling_only else list(range(1, N_DEV))
    n_slot = 2 if sibling_only else N_DEV
    blk = x.shape if bcast else x.shape[1:]

    def body(x_ref, o_ref, send_sems, recv_sems, local_sem):
        mx, my, mc = lax.axis_index("x"), lax.axis_index("y"), lax.axis_index("c")
        me = mc if sibling_only else 4 * mx + 2 * my + mc
        me_dev = 4 * mx + 2 * my + mc
        mine = pltpu.make_async_copy(x_ref if bcast else x_ref.at[me_dev], o_ref.at[me], local_sem)
        mine.start()
        copies = []
        rc = blk[0] // chunks
        for k, r in enumerate(rels):
            px = 1 - mx if (r >> 2) & 1 else mx
            py = 1 - my if (r >> 1) & 1 else my
            pc = 1 - mc if r & 1 else mc
            src = x_ref if bcast else x_ref.at[4 * px + 2 * py + pc]
            for ci in range(chunks):
                rows = pl.ds(ci * rc, rc)
                cp = pltpu.make_async_remote_copy(
                    src_ref=src.at[rows], dst_ref=o_ref.at[me, rows], send_sem=send_sems.at[k * chunks + ci],
                    recv_sem=recv_sems.at[k * chunks + ci], device_id=(px, py, pc), device_id_type=pl.DeviceIdType.MESH)
                cp.start()
                copies.append(cp)
        for cp in copies:
            cp.wait()
        mine.wait()

    return pl.pallas_call(
        body, out_shape=jax.ShapeDtypeStruct((n_slot,) + tuple(blk), x.dtype),
        in_specs=[pl.BlockSpec(memory_space=pl.ANY)], out_specs=pl.BlockSpec(memory_space=pl.ANY),
        scratch_shapes=[pltpu.SemaphoreType.DMA((len(rels) * chunks,)), pltpu.SemaphoreType.DMA((len(rels) * chunks,)),
                        pltpu.SemaphoreType.DMA],
        name=name)(x)


def _gather_two_level(name, x):
    def body(x_ref, o_ref, send_sems, recv_sems, local_sem):
        mx, my, mc = lax.axis_index("x"), lax.axis_index("y"), lax.axis_index("c")
        me, sibling = (mx, my, mc), (mx, my, 1 - mc)
        chips = [(1 - mx, my), (mx, 1 - my), (1 - mx, 1 - my)]

        def slot(px, py, pc):
            return o_ref.at[4 * px + 2 * py + pc]

        def copy(k, block, to, src=None):
            return pltpu.make_async_remote_copy(
                src_ref=slot(*block) if src is None else src, dst_ref=slot(*block), send_sem=send_sems.at[k],
                recv_sem=recv_sems.at[k], device_id=to, device_id_type=pl.DeviceIdType.MESH)

        mine = pltpu.make_async_copy(x_ref, slot(*me), local_sem)
        mine.start()
        first = [copy(0, me, sibling, src=x_ref)]
        first += [copy(1 + j, me, (*chip, mc), src=x_ref) for j, chip in enumerate(chips)]
        for cp in first:
            cp.start()
        passed = [copy(4 + j, (*chip, mc), sibling) for j, chip in enumerate(chips)]
        for j, chip in enumerate(chips):
            copy(1 + j, (*chip, mc), me).wait_recv()
            passed[j].start()
        copy(0, sibling, me).wait_recv()
        for j, chip in enumerate(chips):
            copy(4 + j, (*chip, 1 - mc), me).wait_recv()
        for cp in first + passed:
            cp.wait_send()
        mine.wait()

    return pl.pallas_call(
        body, out_shape=jax.ShapeDtypeStruct((N_DEV,) + tuple(x.shape), x.dtype),
        in_specs=[pl.BlockSpec(memory_space=pl.ANY)], out_specs=pl.BlockSpec(memory_space=pl.ANY),
        scratch_shapes=[pltpu.SemaphoreType.DMA((N_DEV - 1,)), pltpu.SemaphoreType.DMA((N_DEV - 1,)),
                        pltpu.SemaphoreType.DMA],
        name=name)(x)


def _sum_slots(name, x):
    s, r, w = x.shape
    tr = _largest_tile(r, 256, 8)

    def body(x_ref, o_ref):
        acc = x_ref[0].astype(F32)
        for j in range(1, s):
            acc = acc + x_ref[j].astype(F32)
        o_ref[...] = acc

    return pl.pallas_call(
        body, grid=(r // tr,), in_specs=[pl.BlockSpec((s, tr, w), lambda i: (0, i, 0))],
        out_specs=pl.BlockSpec((tr, w), lambda i: (i, 0)), out_shape=jax.ShapeDtypeStruct((r, w), F32),
        compiler_params=_params("parallel"), name=name)(x)


def _sigmoid(x):
    return 1.0 / (1.0 + jnp.exp(-x))


def _silu(x):
    return x * _sigmoid(x)


def _silu_grad(x):
    s = _sigmoid(x)
    return s * (1.0 + x * (1.0 - s))


_INV_SQRT2 = 1.0 / math.sqrt(2.0)
_INV_SQRT2PI = 1.0 / math.sqrt(2.0 * math.pi)


def _gelu(x):
    return 0.5 * x * (1.0 + lax.erf(x * _INV_SQRT2))


def _gelu_grad(x):
    return 0.5 * (1.0 + lax.erf(x * _INV_SQRT2)) + x * jnp.exp(-0.5 * x * x) * _INV_SQRT2PI


def _rms_hat(x):
    r = lax.rsqrt(jnp.mean(x * x, axis=-1, keepdims=True) + NORM_EPS)
    return x * r, r


def _rms_bwd(xh, r, dxh):
    return r * (dxh - xh * jnp.mean(dxh * xh, axis=-1, keepdims=True))


def _norm_mod_fwd(name, x, g, scale, shift, n_rows, tr, n_lat):
    def fn(xt, gv, sc, sh):
        xh, _ = _rms_hat(xt)
        return (xh * gv) * (1.0 + sc) + sh

    return _rowwise(name, fn, n_rows, tr, [(x, 0, D_MODEL)], [g, scale, shift], [(D_MODEL, BF16)], [], n_lat=n_lat)[0]


def _norm_mod_bwd(name, x, g, scale, dh, dres, n_rows, tr, n_lat, prev=None):
    nlt = n_lat // tr

    def fn(flag, xt, dht, drt, *rest):
        gv, sc = rest[-2:] if prev is None else rest[1:3]
        xh, r = _rms_hat(xt)
        n = xh * gv
        dn = dht * (1.0 + sc)
        dx = _rms_bwd(xh, r, dn * gv) + flag * drt
        if prev is None:
            return dx, dn * xh, dht * n, dht
        return dx, rest[3] * dx, dn * xh, dht * n, dht, dx * rest[0]

    rows = [(x, 0, D_MODEL), (dh, 0, D_MODEL), (dres, 0, D_MODEL, "clamp", nlt)]
    vecs, row_outs, reds = [g, scale], [(D_MODEL, F32)], [(1, D_MODEL), (2, D_MODEL), (2, D_MODEL)]
    if prev is not None:
        rows.append((prev[0], 0, D_MODEL))
        vecs.append(prev[1])
        row_outs.append((D_MODEL, BF16))
        reds.append((2, D_MODEL))
    return _rowwise(name, fn, n_rows, tr, rows, vecs, row_outs, reds, n_lat=n_lat, want_flag=True)


def _s5_prep(a_re, a_im, log_dt, b_re, b_im, seg_lat, seg_ctx):
    def body(ar_ref, ai_ref, ld_ref, br_ref, bi_ref, abr_ref, abi_ref, bbr_ref, bbi_ref, alr_ref, ali_ref, acr_ref,
             aci_ref):
        lr, li = ar_ref[...], ai_ref[...]
        dt = jnp.exp(ld_ref[...])
        ldr, ldi = lr * dt, li * dt
        e = jnp.exp(ldr)
        abr, abi = e * jnp.cos(ldi), e * jnp.sin(ldi)
        abr_ref[...] = abr
        abi_ref[...] = abi
        den = lr * lr + li * li
        nr, ni = abr - 1.0, abi
        qr = (nr * lr + ni * li) / den
        qi = (ni * lr - nr * li) / den
        br, bi = br_ref[...], bi_ref[...]
        bbr_ref[...] = qr[None] * br - qi[None] * bi
        bbi_ref[...] = qr[None] * bi + qi[None] * br
        for seg, r_ref, i_ref in ((seg_lat, alr_ref, ali_ref), (seg_ctx, acr_ref, aci_ref)):
            es = jnp.exp(ldr * float(seg))
            r_ref[...] = es * jnp.cos(ldi * float(seg))
            i_ref[...] = es * jnp.sin(ldi * float(seg))

    sm = jax.ShapeDtypeStruct(a_re.shape, F32)
    big = jax.ShapeDtypeStruct(b_re.shape, F32)
    return pl.pallas_call(body, out_shape=[sm, sm, big, big, sm, sm, sm, sm], name="s5_prep")(
        a_re, a_im, log_dt, b_re, b_im)


def _s5_prep_bwd(a_re, a_im, log_dt, b_re, b_im, dabr, dabi, dbbr, dbbi):
    def body(ar_ref, ai_ref, ld_ref, br_ref, bi_ref, dabr_ref, dabi_ref, dbbr_ref, dbbi_ref,
             dar_ref, dai_ref, dld_ref, dbr_ref, dbi_ref):
        lr, li = ar_ref[...], ai_ref[...]
        dt = jnp.exp(ld_ref[...])
        ldr, ldi = lr * dt, li * dt
        e = jnp.exp(ldr)
        abr, abi = e * jnp.cos(ldi), e * jnp.sin(ldi)
        den = lr * lr + li * li
        nr, ni = abr - 1.0, abi
        qr = (nr * lr + ni * li) / den
        qi = (ni * lr - nr * li) / den
        br, bi = br_ref[...], bi_ref[...]
        gbr, gbi = dbbr_ref[...], dbbi_ref[...]
        dbr_ref[...] = gbr * qr[None] + gbi * qi[None]
        dbi_ref[...] = gbi * qr[None] - gbr * qi[None]
        dqr = jnp.sum(gbr * br + gbi * bi, axis=0)
        dqi = jnp.sum(gbi * br - gbr * bi, axis=0)
        dnr = (dqr * lr - dqi * li) / den
        dni = (dqr * li + dqi * lr) / den
        dlr_q = (dqr * (nr - 2.0 * lr * qr) + dqi * (ni - 2.0 * lr * qi)) / den
        dli_q = (dqr * (ni - 2.0 * li * qr) + dqi * (-nr - 2.0 * li * qi)) / den
        gar = dabr_ref[...] + dnr
        gai = dabi_ref[...] + dni
        dldr = gar * abr + gai * abi
        dldi = gai * abr - gar * abi
        dar_ref[...] = dldr * dt + dlr_q
        dai_ref[...] = dldi * dt + dli_q
        ddt = jnp.sum(dldr * lr + dldi * li, axis=1, keepdims=True)
        dld_ref[...] = ddt * dt

    sm = jax.ShapeDtypeStruct(a_re.shape, F32)
    big = jax.ShapeDtypeStruct(b_re.shape, F32)
    return pl.pallas_call(body, out_shape=[sm, sm, jax.ShapeDtypeStruct(log_dt.shape, F32), big, big],
                          name="s5_prep_bwd")(a_re, a_im, log_dt, b_re, b_im, dabr, dabi, dbbr, dbbi)


def _slab_cols(v):
    return v.reshape(N_SLAB, 1, HALF_W)


def _slab_pair(vr, vi):
    return jnp.concatenate([_slab_cols(vr), _slab_cols(vi)], axis=-1)


def _slab_in_matrix(bbr, bbi):
    eye = jnp.eye(SLAB_GROUPS, dtype=F32)

    def one(b):
        b = b.reshape(N_SLAB, SLAB_GROUPS, SSM_STATE, SSM_GROUP)
        m = jnp.einsum("sgph,gk->sghkp", b, eye)
        return m.reshape(N_SLAB, SLAB_W, HALF_W)

    return jnp.concatenate([one(bbr), one(bbi)], axis=-1)


def _slab_out_matrix(cr, ci):
    eye = jnp.eye(SLAB_GROUPS, dtype=F32)

    def one(c):
        c = c.reshape(N_SLAB, SLAB_GROUPS, SSM_GROUP, SSM_STATE)
        m = jnp.einsum("sghp,gk->skpgh", c, eye)
        return m.reshape(N_SLAB, HALF_W, SLAB_W)

    return jnp.concatenate([one(cr), one(-ci)], axis=1)


def _slab_diag(m):
    m = m.reshape(N_SLAB, SLAB_GROUPS, SSM_GROUP, 2, SLAB_GROUPS, SSM_STATE)
    d = jnp.stack([m[:, g, :, :, g, :] for g in range(SLAB_GROUPS)], axis=1)
    return d.transpose(3, 0, 1, 2, 4).reshape(2, SSM_GROUPS, SSM_GROUP, SSM_STATE)


def _cmul(ar, ai, xr, xi, conj):
    if conj:
        return ar * xr + ai * xi, ar * xi - ai * xr
    return ar * xr - ai * xi, ar * xi + ai * xr


def _s5_pow_table(name, abar, seg, falling, conj):
    assert seg >= 8 and seg & (seg - 1) == 0

    def body(a_ref, o_ref, t_ref):
        ar, ai = a_ref[0, :, :HALF_W], a_ref[0, :, HALF_W:]
        if conj:
            ai = -ai
        rr, ri = [jnp.ones_like(ar)], [jnp.zeros_like(ai)]
        for _ in range(7):
            pr, pi = _cmul(ar, ai, rr[-1], ri[-1], False)
            rr.append(pr)
            ri.append(pi)
        sr, si = _cmul(ar, ai, rr[-1], ri[-1], False)
        if falling:
            rr, ri = rr[::-1], ri[::-1]
        first = slice(seg - 8, seg) if falling else slice(0, 8)
        t_ref[first, :HALF_W] = jnp.concatenate(rr, axis=0)
        t_ref[first, HALF_W:] = jnp.concatenate(ri, axis=0)
        size = 8
        while size < seg:
            src = slice(seg - size, seg) if falling else slice(0, size)
            dst = slice(seg - 2 * size, seg - size) if falling else slice(size, 2 * size)
            pr, pi = _cmul(sr, si, t_ref[src, :HALF_W], t_ref[src, HALF_W:], False)
            t_ref[dst, :HALF_W] = pr
            t_ref[dst, HALF_W:] = pi
            sr, si = _cmul(sr, si, sr, si, False)
            size *= 2
        o_ref[0] = t_ref[...].astype(BF16)

    return pl.pallas_call(
        body, grid=(N_SLAB,), in_specs=[pl.BlockSpec((1, 1, STATE_W), lambda s: (s, 0, 0))],
        out_specs=pl.BlockSpec((1, seg, STATE_W), lambda s: (s, 0, 0)),
        out_shape=jax.ShapeDtypeStruct((N_SLAB, seg, STATE_W), BF16),
        scratch_shapes=[pltpu.VMEM((seg, STATE_W), F32)], compiler_params=_params("parallel"), name=name)(abar)


def _s5_ends(name, x, n_rows, row0, table, m_mat):
    seg = n_rows // N_SEG
    rb = row0 // seg
    tn = (((0,), (0,)), ((), ()))

    def body(x_ref, t_ref, m_ref, z_ref):
        j = pl.program_id(1)
        t = lax.dot_general(x_ref[...].astype(BF16), t_ref[0], tn, preferred_element_type=F32)
        tr_, ti_ = t[:, :HALF_W], t[:, HALF_W:]
        mr, mi = m_ref[0, :, :HALF_W], m_ref[0, :, HALF_W:]
        z_ref[0, pl.ds(j, 1), :HALF_W] = jnp.sum(mr * tr_ - mi * ti_, axis=0, keepdims=True)
        z_ref[0, pl.ds(j, 1), HALF_W:] = jnp.sum(mr * ti_ + mi * tr_, axis=0, keepdims=True)

    return pl.pallas_call(
        body, grid=(N_SLAB, N_SEG),
        in_specs=[pl.BlockSpec((seg, SLAB_W), lambda s, j: (rb + j, s)),
                  pl.BlockSpec((1, seg, STATE_W), lambda s, j: (s, 0, 0)),
                  pl.BlockSpec((1, SLAB_W, STATE_W), lambda s, j: (s, 0, 0))],
        out_specs=pl.BlockSpec((1, N_SEG, STATE_W), lambda s, j: (s, 0, 0)),
        out_shape=jax.ShapeDtypeStruct((N_SLAB, N_SEG, STATE_W), F32),
        compiler_params=_params("parallel", "arbitrary"), name=name)(x, table, m_mat)


def _s5_carry(name, z, a_seg, init, descending, conj):
    order = list(range(N_SEG - 1, -1, -1)) if descending else list(range(N_SEG))

    def body(z_ref, a_ref, i_ref, e_ref, o_ref):
        ar, ai = a_ref[:, :HALF_W], a_ref[:, HALF_W:]
        cr, ci = i_ref[:, :HALF_W], i_ref[:, HALF_W:]
        for j in order:
            e_ref[j, :, :HALF_W] = cr
            e_ref[j, :, HALF_W:] = ci
            pr, pi = _cmul(ar, ai, cr, ci, conj)
            cr = pr + z_ref[j, :, :HALF_W]
            ci = pi + z_ref[j, :, HALF_W:]
        o_ref[:, :HALF_W] = cr
        o_ref[:, HALF_W:] = ci

    return pl.pallas_call(body, out_shape=[jax.ShapeDtypeStruct(z.shape, F32), jax.ShapeDtypeStruct(init.shape, F32)],
                          name=name)(z, a_seg, init)


def _seg_major(v):
    return jnp.transpose(v, (1, 0, 2))


def _s5_scan(name, u, n_rows, row0, b_mat, c_mat, abar, h_in, descending, y_alias=None, y_rows=None):
    seg = n_rows // N_SEG
    ta = min(32, seg)
    nk = seg // ta
    assert seg * N_SEG == n_rows and nk * ta == seg and row0 % n_rows == 0 and ta % 8 == 0
    rb = row0 // n_rows
    tile = ta * N_SEG

    def body(*refs):
        u_ref, b_ref, c_ref, a_ref, hin_ref = refs[:5]
        y_ref, hch_ref, st_ref, up_ref, h_ref = refs[-5:]
        k = pl.program_id(1)
        kk = nk - 1 - k if descending else k
        a0 = kk * ta

        @pl.when(k == 0)
        def _():
            st_ref[...] = hin_ref[0]

        hch_ref[0, 0] = st_ref[...]
        for al in range(ta):
            up_ref[al * N_SEG:(al + 1) * N_SEG, :] = u_ref[pl.ds(a0 + al, N_SEG, stride=seg), :]
        h_ref[...] = jnp.dot(up_ref[...].astype(BF16), b_ref[0], preferred_element_type=F32)
        ar = jnp.broadcast_to(a_ref[0, :, :HALF_W], (N_SEG, HALF_W))
        ai = jnp.broadcast_to(a_ref[0, :, HALF_W:], (N_SEG, HALF_W))

        def step(i, carry):
            hr, hi = carry
            al = ta - 1 - i if descending else i
            row = pl.multiple_of(al * N_SEG, N_SEG)
            pr, pi = _cmul(ar, ai, hr, hi, False)
            hr = pr + h_ref[pl.ds(row, N_SEG), :HALF_W]
            hi = pi + h_ref[pl.ds(row, N_SEG), HALF_W:]
            h_ref[pl.ds(row, N_SEG), :HALF_W] = hr
            h_ref[pl.ds(row, N_SEG), HALF_W:] = hi
            return hr, hi

        hr, hi = lax.fori_loop(0, ta, step, (st_ref[:, :HALF_W], st_ref[:, HALF_W:]))
        st_ref[:, :HALF_W] = hr
        st_ref[:, HALF_W:] = hi
        yt = jnp.dot(h_ref[...].astype(BF16), c_ref[0], preferred_element_type=F32)
        for al in range(ta):
            y_ref[pl.ds(a0 + al, N_SEG, stride=seg), :] = yt[al * N_SEG:(al + 1) * N_SEG, :]

    u_spec = pl.BlockSpec((n_rows, SLAB_W), lambda s, k: (rb, s))
    b_spec = pl.BlockSpec((1, SLAB_W, STATE_W), lambda s, k: (s, 0, 0))
    c_spec = pl.BlockSpec((1, STATE_W, SLAB_W), lambda s, k: (s, 0, 0))
    a_spec = pl.BlockSpec((1, 1, STATE_W), lambda s, k: (s, 0, 0))
    st_spec = pl.BlockSpec((1, N_SEG, STATE_W), lambda s, k: (s, 0, 0))
    scratch = [pltpu.VMEM((N_SEG, STATE_W), F32), pltpu.VMEM((tile, SLAB_W), F32), pltpu.VMEM((tile, STATE_W), F32)]
    kmap = (lambda s, k: (s, nk - 1 - k, 0, 0)) if descending else (lambda s, k: (s, k, 0, 0))
    out_specs = [u_spec, pl.BlockSpec((1, 1, N_SEG, STATE_W), kmap)]
    out_shape = [jax.ShapeDtypeStruct((y_rows, D_MODEL), F32), jax.ShapeDtypeStruct((N_SLAB, nk, N_SEG, STATE_W), F32)]
    in_specs = [u_spec, b_spec, c_spec, a_spec, st_spec]
    args = [u, b_mat, c_mat, abar, h_in]
    aliases = {}
    if y_alias is not None:
        in_specs.append(pl.BlockSpec(memory_space=pl.ANY))
        args.append(y_alias)
        aliases = {5: 0}
    return pl.pallas_call(
        body, grid=(N_SLAB, nk), in_specs=in_specs, out_specs=out_specs, out_shape=out_shape, scratch_shapes=scratch,
        input_output_aliases=aliases, compiler_params=_params("parallel", "arbitrary"), name=name)(*args)


def _s5_scan_bwd(name, u, dy, n_rows, row0, b_mat, bt_mat, ct_mat, abar, h_chunks, g_in, descending,
                 du_alias=None, du_rows=None):
    seg = n_rows // N_SEG
    ta = min(32, seg)
    nk = seg // ta
    rb = row0 // n_rows
    tile = ta * N_SEG
    g_desc = not descending

    def body(*refs):
        u_ref, dy_ref, b_ref, bt_ref, ct_ref, a_ref, hch_ref, gin_ref = refs[:8]
        du_ref, db_ref, dc_ref, da_ref, st_ref, up_ref, dyp_ref, h_ref, g_ref = refs[-9:]
        k = pl.program_id(1)
        kk = nk - 1 - k if g_desc else k
        a0 = kk * ta
        ar = jnp.broadcast_to(a_ref[0, :, :HALF_W], (N_SEG, HALF_W))
        ai = jnp.broadcast_to(a_ref[0, :, HALF_W:], (N_SEG, HALF_W))

        @pl.when(k == 0)
        def _():
            st_ref[...] = gin_ref[0]

        for al in range(ta):
            dyp_ref[al * N_SEG:(al + 1) * N_SEG, :] = dy_ref[pl.ds(a0 + al, N_SEG, stride=seg), :]
            up_ref[al * N_SEG:(al + 1) * N_SEG, :] = u_ref[pl.ds(a0 + al, N_SEG, stride=seg), :]
        g_ref[...] = jnp.dot(dyp_ref[...].astype(BF16), ct_ref[0], preferred_element_type=F32)
        h_ref[...] = jnp.dot(up_ref[...].astype(BF16), b_ref[0], preferred_element_type=F32)
        h0r, h0i = hch_ref[0, 0, :, :HALF_W], hch_ref[0, 0, :, HALF_W:]

        def hstep(i, carry):
            hr, hi = carry
            al = ta - 1 - i if descending else i
            row = pl.multiple_of(al * N_SEG, N_SEG)
            pr, pi = _cmul(ar, ai, hr, hi, False)
            hr = pr + h_ref[pl.ds(row, N_SEG), :HALF_W]
            hi = pi + h_ref[pl.ds(row, N_SEG), HALF_W:]
            h_ref[pl.ds(row, N_SEG), :HALF_W] = hr
            h_ref[pl.ds(row, N_SEG), HALF_W:] = hi
            return hr, hi

        lax.fori_loop(0, ta, hstep, (h0r, h0i))

        def gstep(i, carry):
            gr, gi = carry
            al = ta - 1 - i if g_desc else i
            row = pl.multiple_of(al * N_SEG, N_SEG)
            pr, pi = _cmul(ar, ai, gr, gi, True)
            gr = pr + g_ref[pl.ds(row, N_SEG), :HALF_W]
            gi = pi + g_ref[pl.ds(row, N_SEG), HALF_W:]
            g_ref[pl.ds(row, N_SEG), :HALF_W] = gr
            g_ref[pl.ds(row, N_SEG), HALF_W:] = gi
            return gr, gi

        gr, gi = lax.fori_loop(0, ta, gstep, (st_ref[:, :HALF_W], st_ref[:, HALF_W:]))
        st_ref[:, :HALF_W] = gr
        st_ref[:, HALF_W:] = gi

        gb = g_ref[...].astype(BF16)
        dut = jnp.dot(gb, bt_ref[0], preferred_element_type=F32)
        for al in range(ta):
            du_ref[pl.ds(a0 + al, N_SEG, stride=seg), :] = dut[al * N_SEG:(al + 1) * N_SEG, :]
        tn = (((0,), (0,)), ((), ()))
        dbp = lax.dot_general(up_ref[...].astype(BF16), gb, tn, preferred_element_type=F32)
        dcp = lax.dot_general(dyp_ref[...].astype(BF16), h_ref[...].astype(BF16), tn, preferred_element_type=F32)
        inner = (ta - 1) * N_SEG
        if descending:
            g_in_r, g_in_i = g_ref[0:inner, :HALF_W], g_ref[0:inner, HALF_W:]
            p_in_r, p_in_i = h_ref[N_SEG:tile, :HALF_W], h_ref[N_SEG:tile, HALF_W:]
            g_ed_r, g_ed_i = g_ref[inner:tile, :HALF_W], g_ref[inner:tile, HALF_W:]
        else:
            g_in_r, g_in_i = g_ref[N_SEG:tile, :HALF_W], g_ref[N_SEG:tile, HALF_W:]
            p_in_r, p_in_i = h_ref[0:inner, :HALF_W], h_ref[0:inner, HALF_W:]
            g_ed_r, g_ed_i = g_ref[0:N_SEG, :HALF_W], g_ref[0:N_SEG, HALF_W:]
        dar = g_ed_r * h0r + g_ed_i * h0i
        dai = g_ed_i * h0r - g_ed_r * h0i
        if ta > 1:
            dar = dar + jnp.sum((g_in_r * p_in_r + g_in_i * p_in_i).reshape(ta - 1, N_SEG, HALF_W), axis=0)
            dai = dai + jnp.sum((g_in_i * p_in_r - g_in_r * p_in_i).reshape(ta - 1, N_SEG, HALF_W), axis=0)

        @pl.when(k == 0)
        def _():
            db_ref[0] = dbp
            dc_ref[0] = dcp
            da_ref[0, :, :HALF_W] = dar
            da_ref[0, :, HALF_W:] = dai

        @pl.when(k > 0)
        def _():
            db_ref[0] += dbp
            dc_ref[0] += dcp
            da_ref[0, :, :HALF_W] += dar
            da_ref[0, :, HALF_W:] += dai

    u_spec = pl.BlockSpec((n_rows, SLAB_W), lambda s, k: (rb, s))
    m_spec = pl.BlockSpec((1, SLAB_W, STATE_W), lambda s, k: (s, 0, 0))
    mt_spec = pl.BlockSpec((1, STATE_W, SLAB_W), lambda s, k: (s, 0, 0))
    a_spec = pl.BlockSpec((1, 1, STATE_W), lambda s, k: (s, 0, 0))
    st_spec = pl.BlockSpec((1, N_SEG, STATE_W), lambda s, k: (s, 0, 0))
    st_shape = jax.ShapeDtypeStruct((N_SLAB, N_SEG, STATE_W), F32)
    kmap = (lambda s, k: (s, nk - 1 - k, 0, 0)) if g_desc else (lambda s, k: (s, k, 0, 0))
    in_specs = [u_spec, u_spec, m_spec, mt_spec, m_spec, a_spec, pl.BlockSpec((1, 1, N_SEG, STATE_W), kmap), st_spec]
    args = [u, dy, b_mat, bt_mat, ct_mat, abar, h_chunks, g_in]
    aliases = {}
    if du_alias is not None:
        in_specs.append(pl.BlockSpec(memory_space=pl.ANY))
        args.append(du_alias)
        aliases = {8: 0}
    acc_shape = jax.ShapeDtypeStruct((N_SLAB, SLAB_W, STATE_W), F32)
    out_specs = [u_spec, m_spec, m_spec, st_spec]
    out_shape = [jax.ShapeDtypeStruct((du_rows, D_MODEL), F32), acc_shape, acc_shape, st_shape]
    scratch = [pltpu.VMEM((N_SEG, STATE_W), F32), pltpu.VMEM((tile, SLAB_W), F32), pltpu.VMEM((tile, SLAB_W), F32),
               pltpu.VMEM((tile, STATE_W), F32), pltpu.VMEM((tile, STATE_W), F32)]
    return pl.pallas_call(
        body, grid=(N_SLAB, nk), in_specs=in_specs, out_specs=out_specs, out_shape=out_shape, scratch_shapes=scratch,
        input_output_aliases=aliases, compiler_params=_params("parallel", "arbitrary"), name=name)(*args)


ROPE_HALF = HEAD_DIM // 4
TABLE_W = 2 * HEAD_DIM
Q_SCALE = 1.0 / math.sqrt(HEAD_DIM)
HEADS_PER_BLOCK = 2 * KV_REP
Q_BLOCK_W = HEADS_PER_BLOCK * HEAD_DIM


def _rope_tables(n_lat, n_ctx):
    rows = n_lat // GRID_W
    freqs = ROPE_THETA ** (-jnp.arange(ROPE_HALF, dtype=F32) / ROPE_HALF)
    ang_r = jnp.arange(rows, dtype=F32)[:, None] * freqs[None]
    ang_c = jnp.arange(GRID_W, dtype=F32)[:, None] * freqs[None]
    by_row = lambda v: jnp.repeat(v, GRID_W, axis=0)
    by_col = lambda v: jnp.tile(v, (rows, 1))
    cos = jnp.concatenate([by_row(jnp.cos(ang_r)), by_row(jnp.cos(ang_r)), by_col(jnp.cos(ang_c)), by_col(jnp.cos(ang_c))] * 2,
                          axis=1)
    sin = jnp.concatenate([by_row(jnp.sin(ang_r)), by_row(jnp.sin(ang_r)), by_col(jnp.sin(ang_c)), by_col(jnp.sin(ang_c))] * 2,
                          axis=1)
    cos = jnp.concatenate([cos, jnp.ones((n_ctx, TABLE_W), F32)], axis=0)
    sin = jnp.concatenate([sin, jnp.zeros((n_ctx, TABLE_W), F32)], axis=0)
    return cos, sin


def _rot_half(v):
    w = v.shape[1]
    ahead = pltpu.roll(v, w - ROPE_HALF, axis=1)
    behind = pltpu.roll(v, ROPE_HALF, axis=1)
    lane = lax.broadcasted_iota(jnp.int32, v.shape, 1)
    return jnp.where((lane % (2 * ROPE_HALF)) < ROPE_HALF, -ahead, behind)


def _head_mean(v, sel, selt):
    m = jnp.dot(v, sel, precision=lax.Precision.HIGH, preferred_element_type=F32) * (1.0 / HEAD_DIM)
    return jnp.dot(m, selt, precision=lax.Precision.HIGH, preferred_element_type=F32)


def _head_selectors(n_heads):
    sel = jnp.repeat(jnp.eye(n_heads, dtype=F32), HEAD_DIM, axis=0)
    return sel[None], sel.T[None]


def _head_norm(x, sel, selt):
    r = lax.rsqrt(_head_mean(x * x, sel, selt) + NORM_EPS)
    return x * r, r


def _qk_prep(proj, qn, kn, cos, sin, n, tr):
    qw, kw = _vec(jnp.tile(qn, N_Q_HEADS)), _vec(jnp.tile(kn, N_KV_HEADS))
    sq, sqt = _head_selectors(N_Q_HEADS)
    sk, skt = _head_selectors(N_KV_HEADS)

    def fn(qr, kvr, ct, st, qwv, kwv, s16, s16t, s4, s4t):
        outs = []
        for x, wv, sel, selt, scale in ((qr, qwv, s16, s16t, Q_SCALE), (kvr[:, :KV_W], kwv, s4, s4t, 1.0)):
            reps = x.shape[1] // TABLE_W
            cw, sw = jnp.tile(ct, (1, reps)), jnp.tile(st, (1, reps))
            xh, _ = _head_norm(x, sel, selt)
            nrm = xh * wv
            outs.append((nrm * cw + _rot_half(nrm) * sw) * scale)
        return outs[0], outs[1], kvr[:, KV_W:]

    return _rowwise("l1_qk_prep", fn, n, tr,
                    [(proj, 0, ATTN_W), (proj, 2 * ATTN_W // (2 * KV_W), 2 * KV_W), (cos, 0, TABLE_W), (sin, 0, TABLE_W)],
                    [qw, kw, sq, sqt, sk, skt], [(ATTN_W, BF16), (KV_W, BF16), (KV_W, BF16)], [])


def _qk_prep_bwd(proj, qn, kn, cos, sin, dq, dz, dk, dv, n, n_lat, tr):
    qw, kw = _vec(jnp.tile(qn, N_Q_HEADS)), _vec(jnp.tile(kn, N_KV_HEADS))
    sq, sqt = _head_selectors(N_Q_HEADS)
    sk, skt = _head_selectors(N_KV_HEADS)
    nlt = n_lat // tr

    def fn(flag, qr, kvr, ct, st, dqt, dzt, dkt, dvt, qwv, kwv, s16, s16t, s4, s4t):
        dxs, dws = [], []
        for x, dy, wv, sel, selt in ((qr, dqt * (flag * Q_SCALE), qwv, s16, s16t), (kvr[:, :KV_W], dkt, kwv, s4, s4t)):
            reps = x.shape[1] // TABLE_W
            cw, sw = jnp.tile(ct, (1, reps)), jnp.tile(st, (1, reps))
            xh, r = _head_norm(x, sel, selt)
            dn = dy * cw - _rot_half(dy * sw)
            dxh = dn * wv
            dxs.append(r * (dxh - xh * _head_mean(dxh * xh, sel, selt)))
            dws.append(dn * xh)
        return jnp.concatenate([dxs[0], dzt * flag, dxs[1], dvt], axis=1), dws[0], dws[1]

    dproj, dqw, dkw = _rowwise(
        "l1_qk_prep_bwd", fn, n, tr,
        [(proj, 0, ATTN_W), (proj, 2 * ATTN_W // (2 * KV_W), 2 * KV_W), (cos, 0, TABLE_W), (sin, 0, TABLE_W),
         (dq, 0, ATTN_W, "clamp", nlt), (dz, 0, ATTN_W, "clamp", nlt), (dk, 0, KV_W), (dv, 0, KV_W)],
        [qw, kw, sq, sqt, sk, skt], [(2 * ATTN_W + 2 * KV_W, BF16)], [(1, ATTN_W), (1, KV_W)], n_lat=n_lat, want_flag=True)
    return dproj, dqw.reshape(N_Q_HEADS, HEAD_DIM).sum(0)[None], dkw.reshape(N_KV_HEADS, HEAD_DIM).sum(0)[None]


NT = (((1,), (1,)), ((), ()))


def _attn_fwd(q, k, v, t, tq, tk):
    n = k.shape[0]
    nkc = n // tk

    ts = _largest_tile(tq, 256, LANES)
    items = [(sub, j) for sub in range(tq // ts) for j in range(HEADS_PER_BLOCK)]

    def body(q_ref, k_ref, v_ref, o_ref, lse_ref, s_ref, m_ref, l_ref, acc_ref):
        def lanes(j):
            g = j // KV_REP
            return slice(j * HEAD_DIM, (j + 1) * HEAD_DIM), slice(g * HEAD_DIM, (g + 1) * HEAD_DIM)

        for idx in range(len(items) + 1):
            nxt = items[idx] if idx < len(items) else None
            cur = items[idx - 1] if idx > 0 else None
            sn, sc = idx % 2, (idx - 1) % 2
            if nxt is not None:
                rows_n = slice(nxt[0] * ts, (nxt[0] + 1) * ts)
                ql_n, kl_n = lanes(nxt[1])
                qv = q_ref[rows_n, ql_n]
                m_ref[sn] = jnp.full((ts, LANES), -jnp.inf, F32)
            if cur is not None:
                rows_c = slice(cur[0] * ts, (cur[0] + 1) * ts)
                ql_c, kl_c = lanes(cur[1])
                m_row = jnp.max(m_ref[sc], axis=-1, keepdims=True)
                l_ref[...] = jnp.zeros(l_ref.shape, F32)
                acc_ref[...] = jnp.zeros(acc_ref.shape, F32)

            def sweep(kc, c):
                off = pl.multiple_of(kc * tk, tk)
                if nxt is not None:
                    s = lax.dot_general(qv, k_ref[pl.ds(off, tk), kl_n], NT, preferred_element_type=F32)
                    s_ref[sn, :, pl.ds(off, tk)] = s
                    m = m_ref[sn]
                    for cb in range(tk // LANES):
                        m = jnp.maximum(m, s[:, cb * LANES:(cb + 1) * LANES])
                    m_ref[sn] = m
                if cur is not None:
                    p = jnp.exp(s_ref[sc, :, pl.ds(off, tk)] - m_row)
                    lsum = l_ref[...]
                    for cb in range(tk // LANES):
                        lsum = lsum + p[:, cb * LANES:(cb + 1) * LANES]
                    l_ref[...] = lsum
                    acc_ref[...] += jnp.dot(p.astype(BF16), v_ref[pl.ds(off, tk), kl_c], preferred_element_type=F32)
                return c

            lax.fori_loop(0, nkc, sweep, 0)
            if cur is not None:
                l_row = jnp.sum(l_ref[...], axis=-1, keepdims=True)
                o_ref[rows_c, ql_c] = acc_ref[...] / l_row
                lse_ref[0, rows_c, cur[1]:cur[1] + 1] = m_row + jnp.log(l_row)

    nb = ATTN_W // Q_BLOCK_W
    kspec = pl.BlockSpec((n, LANES), lambda b, i: (0, b))
    return pl.pallas_call(
        body, grid=(nb, t // tq),
        in_specs=[pl.BlockSpec((tq, Q_BLOCK_W), lambda b, i: (i, b)), kspec, kspec],
        out_specs=[pl.BlockSpec((tq, Q_BLOCK_W), lambda b, i: (i, b)),
                   pl.BlockSpec((1, tq, HEADS_PER_BLOCK), lambda b, i: (b, i, 0))],
        out_shape=[jax.ShapeDtypeStruct((t, ATTN_W), F32), jax.ShapeDtypeStruct((nb, t, HEADS_PER_BLOCK), F32)],
        scratch_shapes=[pltpu.VMEM((2, ts, n), F32), pltpu.VMEM((2, ts, LANES), F32), pltpu.VMEM((ts, LANES), F32),
                        pltpu.VMEM((ts, HEAD_DIM), F32)],
        compiler_params=_params("parallel", "parallel"), name="attn_fwd")(q, k, v)


def _attn_bwd(q, k, v, do, o, lse, t, tq, tk):
    n = k.shape[0]
    nkc = n // tk
    tn = (((0,), (0,)), ((), ()))

    def body(q_ref, k_ref, v_ref, do_ref, o_ref, lse_ref, dq_ref, dk_ref, dv_ref, acc_ref):
        @pl.when(pl.program_id(1) == 0)
        def _():
            dk_ref[...] = jnp.zeros(dk_ref.shape, F32)
            dv_ref[...] = jnp.zeros(dv_ref.shape, F32)

        for j0 in range(0, HEADS_PER_BLOCK, 2):
            kl = slice((j0 // KV_REP) * HEAD_DIM, (j0 // KV_REP + 1) * HEAD_DIM)
            heads = []
            for a in range(2):
                j = j0 + a
                ql = slice(j * HEAD_DIM, (j + 1) * HEAD_DIM)
                qv, dov = q_ref[:, ql], do_ref[:, ql]
                dl_v = jnp.sum(dov.astype(F32) * o_ref[:, ql], axis=-1, keepdims=True)
                heads.append((ql, qv, dov, dl_v, lse_ref[0, :, j:j + 1]))
                acc_ref[a] = jnp.zeros((tq, HEAD_DIM), F32)

            def step(kc, c):
                off = pl.multiple_of(kc * tk, tk)
                kt = k_ref[pl.ds(off, tk), kl]
                vt = v_ref[pl.ds(off, tk), kl]
                dv_part, dk_part = None, None
                for a, (_, qv, dov, dl_v, lse_v) in enumerate(heads):
                    s = lax.dot_general(qv, kt, NT, preferred_element_type=F32)
                    p = jnp.exp(s - lse_v)
                    dp = lax.dot_general(dov, vt, NT, preferred_element_type=F32)
                    ds = (p * (dp - dl_v)).astype(BF16)
                    acc_ref[a] += jnp.dot(ds, kt, preferred_element_type=F32)
                    dvp = lax.dot_general(p.astype(BF16), dov, tn, preferred_element_type=F32)
                    dkp = lax.dot_general(ds, qv, tn, preferred_element_type=F32)
                    dv_part = dvp if dv_part is None else dv_part + dvp
                    dk_part = dkp if dk_part is None else dk_part + dkp
                dv_ref[pl.ds(off, tk), kl] += dv_part
                dk_ref[pl.ds(off, tk), kl] += dk_part
                return c

            lax.fori_loop(0, nkc, step, 0)
            for a, h in enumerate(heads):
                dq_ref[:, h[0]] = acc_ref[a]

    nb = ATTN_W // Q_BLOCK_W
    qspec = pl.BlockSpec((tq, Q_BLOCK_W), lambda b, i: (i, b))
    kspec = pl.BlockSpec((n, LANES), lambda b, i: (0, b))
    cspec = pl.BlockSpec((1, tq, HEADS_PER_BLOCK), lambda b, i: (b, i, 0))
    return pl.pallas_call(
        body, grid=(nb, t // tq), in_specs=[qspec, kspec, kspec, qspec, qspec, cspec], out_specs=[qspec, kspec, kspec],
        out_shape=[jax.ShapeDtypeStruct((t, ATTN_W), F32), jax.ShapeDtypeStruct((n, KV_W), F32),
                   jax.ShapeDtypeStruct((n, KV_W), F32)],
        scratch_shapes=[pltpu.VMEM((2, tq, HEAD_DIM), F32)],
        compiler_params=_params("parallel", "arbitrary"), name="attn_bwd")(q, k, v, do, o, lse)


def _s5_system(p, n_lat, n_ctx):
    two_g = 2 * SSM_GROUPS
    a_re = p["ssm_a_re"].reshape(two_g, SSM_STATE)
    a_im = p["ssm_a_im"].reshape(two_g, SSM_STATE)
    log_dt = p["ssm_log_dt"].reshape(two_g, 1)
    b_re = p["ssm_b_re"].reshape(two_g, SSM_STATE, SSM_GROUP).transpose(2, 0, 1)
    b_im = p["ssm_b_im"].reshape(two_g, SSM_STATE, SSM_GROUP).transpose(2, 0, 1)
    raw = (a_re, a_im, log_dt, b_re, b_im)
    abr, abi, bbr, bbi, alr, ali, acr, aci = _s5_prep(*raw, n_lat // N_SEG, n_ctx // N_SEG)
    dirs = []
    for d in range(2):
        g = slice(d * SSM_GROUPS, (d + 1) * SSM_GROUPS)
        b_mat = _slab_in_matrix(bbr[:, g].transpose(1, 2, 0), bbi[:, g].transpose(1, 2, 0))
        c_mat = _slab_out_matrix(p["ssm_c_re"][0, d], p["ssm_c_im"][0, d])
        abar = _slab_pair(abr[g], abi[g])
        tables = {}
        for part, seg in (("lat", n_lat // N_SEG), ("ctx", n_ctx // N_SEG)):
            tables["h_" + part] = _s5_pow_table(f"s5_pow_h{d}_{part}", abar, seg, d == 0, False)
            tables["g_" + part] = _s5_pow_table(f"s5_pow_g{d}_{part}", abar, seg, d == 1, True)
        dirs.append(dict(
            b=b_mat.astype(BF16), bt=b_mat.transpose(0, 2, 1).astype(BF16), b32=b_mat,
            c=c_mat.astype(BF16), ct=c_mat.transpose(0, 2, 1).astype(BF16), ct32=c_mat.transpose(0, 2, 1),
            abar=abar, a_lat=_slab_pair(alr[g], ali[g])[:, 0], a_ctx=_slab_pair(acr[g], aci[g])[:, 0], **tables))
    return raw, dirs


def _s5_forward(proj, dirs, n_lat, n_ctx):
    n = n_lat + n_ctx
    zero_c = jnp.zeros((N_SLAB, STATE_W), F32)
    ys, saved = [], []
    for d, s in enumerate(dirs):
        desc = d == 1
        tag = f"s5f{d}"
        zc = _s5_ends(tag + "_ctx_ends", proj, n_ctx, n_lat, s["h_ctx"], s["b32"])
        ent_c, h0 = _s5_carry(tag + "_ctx_carry", _seg_major(zc), s["a_ctx"], zero_c, desc, False)
        y, hch_c = _s5_scan(tag + "_ctx", proj, n_ctx, n_lat, s["b"], s["c"], s["abar"], _seg_major(ent_c), desc,
                            y_rows=n)
        zl = _s5_ends(tag + "_lat_ends", proj, n_lat, 0, s["h_lat"], s["b32"])
        ent_l, _ = _s5_carry(tag + "_lat_carry", _seg_major(zl), s["a_lat"], h0, desc, False)
        y, hch_l = _s5_scan(tag + "_lat", proj, n_lat, 0, s["b"], s["c"], s["abar"], _seg_major(ent_l), desc,
                            y_alias=y, y_rows=n)
        ys.append(y)
        saved.append((hch_l, hch_c))
    return ys, saved


def _s5_backward(proj, dy, dirs, saved, n_lat, n_ctx):
    n = n_lat + n_ctx
    zero_c = jnp.zeros((N_SLAB, STATE_W), F32)
    out = []
    for d, s in enumerate(dirs):
        desc = d == 1
        tag = f"s5b{d}"
        hch_l, hch_c = saved[d]
        gl = _s5_ends(tag + "_lat_ends", dy, n_lat, 0, s["g_lat"], s["ct32"])
        ent_l, g0 = _s5_carry(tag + "_lat_carry", _seg_major(gl), s["a_lat"], zero_c, not desc, True)
        du, db_l, dc_l, da_l = _s5_scan_bwd(tag + "_lat", proj, dy, n_lat, 0, s["b"], s["bt"], s["ct"], s["abar"],
                                            hch_l, _seg_major(ent_l), desc, du_rows=n)
        gc = _s5_ends(tag + "_ctx_ends", dy, n_ctx, n_lat, s["g_ctx"], s["ct32"])
        ent_c, _ = _s5_carry(tag + "_ctx_carry", _seg_major(gc), s["a_ctx"], g0, not desc, True)
        du, db_c, dc_c, da_c = _s5_scan_bwd(tag + "_ctx", proj, dy, n_ctx, n_lat, s["b"], s["bt"], s["ct"], s["abar"],
                                            hch_c, _seg_major(ent_c), desc, du_alias=du, du_rows=n)
        out.append((du, db_l + db_c, dc_l + dc_c, da_l + da_c))
    return out


def _s5_param_grads(raw, bwd):
    dabr, dabi, dbbr, dbbi, dcr, dci = [], [], [], [], [], []
    for _, db, dc, da in bwd:
        da = jnp.sum(da, axis=1)
        dabr.append(da[:, :HALF_W].reshape(SSM_GROUPS, SSM_STATE))
        dabi.append(da[:, HALF_W:].reshape(SSM_GROUPS, SSM_STATE))
        dbd = _slab_diag(db)
        dbbr.append(dbd[0].transpose(1, 0, 2))
        dbbi.append(dbd[1].transpose(1, 0, 2))
        dcd = _slab_diag(dc)
        dcr.append(dcd[0])
        dci.append(-dcd[1])
    cat = lambda xs, ax: jnp.concatenate(xs, axis=ax)
    dar, dai, dld, dbr, dbi = _s5_prep_bwd(*raw, cat(dabr, 0), cat(dabi, 0), cat(dbbr, 1), cat(dbbi, 1))
    shp = (1, 2, SSM_GROUPS, SSM_STATE)
    b_shape = (1, 2, SSM_GROUPS, SSM_STATE, SSM_GROUP)
    return dict(
        ssm_a_re=dar.reshape(shp), ssm_a_im=dai.reshape(shp), ssm_log_dt=dld.reshape(1, 2, SSM_GROUPS),
        ssm_b_re=dbr.transpose(1, 2, 0).reshape(b_shape), ssm_b_im=dbi.transpose(1, 2, 0).reshape(b_shape),
        ssm_c_re=jnp.stack(dcr)[None], ssm_c_im=jnp.stack(dci)[None])


def _example_step(x, ctx, target, mods, w, p):
    t, c = x.shape[0], ctx.shape[0]
    n = t + c
    assert t % c == 0 and c % LANES == 0 and c % (8 * N_SEG) == 0 and t % GRID_W == 0
    tr = _largest_tile(c, 256, 8)
    xall = jnp.concatenate([x, ctx], axis=0)
    g0, g1 = _vec(p["norm_g"][0]), _vec(p["norm_g"][1])
    (shift0, scale0, gate0), (shift1, scale1, gate1) = [tuple(_vec(v) for v in m) for m in mods]

    h0 = _norm_mod_fwd("l0_norm", xall, g0, scale0, shift0, n, tr, t)
    proj0 = _mm("l0_in", h0, w["ssm_w_in"], "nn")
    raw, dirs = _s5_system(p, t, c)
    (y_f, y_r), saved = _s5_forward(proj0, dirs, t, c)
    d_skip = _vec(p["ssm_d"][0])

    def post_a(u, yf, yr, dv):
        y = u * dv + yf + yr
        return y, _gelu(y)

    y0, yg = _rowwise("l0_gelu", post_a, n, tr, [(proj0, 0, D_MODEL), (y_f, 0, D_MODEL), (y_r, 0, D_MODEL)], [d_skip],
                      [(D_MODEL, F32), (D_MODEL, F32)], [])
    tg = _mm("l0_glu", yg, w["ssm_w_glu"], "nn")
    b_glu = _vec(p["ssm_b_glu"][0])

    def post_b(ygt, tt, zt, bv):
        return ygt * _sigmoid(tt + bv) * _silu(zt)

    gz0 = _rowwise("l0_gate", post_b, n, tr, [(yg, 0, D_MODEL), (tg, 0, D_MODEL), (proj0, 1, D_MODEL)], [b_glu],
                   [(D_MODEL, BF16)], [])[0]
    out0 = _mm("l0_out", gz0, w["ssm_w_out"], "nn")

    def res_norm(xt, ot, gv, g1v, sc, sh):
        x1t = xt + gv * ot
        xh, _ = _rms_hat(x1t)
        return x1t, (xh * g1v) * (1.0 + sc) + sh

    x1, h1 = _rowwise("l0_res_l1_norm", res_norm, n, tr, [(xall, 0, D_MODEL), (out0, 0, D_MODEL)],
                      [gate0, g1, scale1, shift1], [(D_MODEL, F32), (D_MODEL, BF16)], [], n_lat=t)
    proj1 = _mm("l1_in", h1, w["attn_w_in"], "nn")
    cos, sin = _rope_tables(t, c)
    qn, kn = p["attn_q_norm"][0], p["attn_k_norm"][0]
    q_h, k_h, v_h = _qk_prep(proj1, qn, kn, cos, sin, n, tr)
    tq = _largest_tile(t, 512, LANES)
    o, lse = _attn_fwd(q_h, k_h, v_h, t, tq, _largest_tile(n, 2816, LANES))
    gz1 = _rowwise("l1_gate", lambda ot, zt: ot * _silu(zt), t, tr, [(o, 0, D_MODEL), (proj1, 1, D_MODEL)], [],
                   [(D_MODEL, BF16)], [])[0]
    out1 = _mm("l1_out", gz1, w["attn_w_out"], "nn")

    gf = _vec(p["final_norm_g"])

    def head(x1t, o1t, tgt, g1v, gfv):
        x2 = x1t + g1v * o1t
        xh, r = _rms_hat(x2)
        e = xh * gfv - tgt
        dyf = e * (1.0 / D_MODEL)
        dx2 = _rms_bwd(xh, r, dyf * gfv)
        return dx2, g1v * dx2, dyf * xh, dx2 * o1t, jnp.sum(e * e, axis=1, keepdims=True)

    gate1_lat = gate1[0:1]
    dx2, dout1, d_gf, d_gate1, sq = _rowwise(
        "head", head, t, tr, [(x1, 0, D_MODEL), (out1, 0, D_MODEL), (target, 0, D_MODEL)], [gate1_lat, gf],
        [(D_MODEL, F32), (D_MODEL, BF16)], [(1, D_MODEL), (1, D_MODEL), (1, 1)])

    d_w_attn_out = _mm("l1_out_dw", gz1, dout1, "tn", out_dtype=BF16)
    dgz1 = _mm("l1_out_dx", dout1, w["attn_w_out"], "nt")

    def gate1_bwd(dgt, ot, zt):
        return dgt * _silu(zt), dgt * ot * _silu_grad(zt)

    do, dz1 = _rowwise("l1_gate_bwd", gate1_bwd, t, tr, [(dgz1, 0, D_MODEL), (o, 0, D_MODEL), (proj1, 1, D_MODEL)], [],
                       [(D_MODEL, BF16), (D_MODEL, F32)], [])
    dq_s, dk, dv = _attn_bwd(q_h, k_h, v_h, do, o, lse, t, tq, _largest_tile(n, 1024, LANES))
    dproj1, d_qn, d_kn = _qk_prep_bwd(proj1, qn, kn, cos, sin, dq_s, dz1, dk, dv, n, t, tr)
    d_w_attn_in = _mm("l1_in_dw", h1, dproj1, "tn", out_dtype=BF16)
    dh1 = _mm("l1_in_dx", dproj1, w["attn_w_in"], "nt")
    dx1, dout0, d_g1, d_scale1, d_shift1, d_gate0 = _norm_mod_bwd("l1_norm_bwd", x1, g1, scale1, dh1, dx2, n, tr, t,
                                                                  prev=(out0, gate0))

    d_w_out = _mm("l0_out_dw", gz0, dout0, "tn", out_dtype=BF16)
    dgz0 = _mm("l0_out_dx", dout0, w["ssm_w_out"], "nt")

    def post_b_bwd(dgt, ygt, tt, zt, bv):
        s = _sigmoid(tt + bv)
        dy2 = dgt * _silu(zt)
        dt = dy2 * ygt * s * (1.0 - s)
        return dgt * (ygt * s) * _silu_grad(zt), dt, dy2 * s, dt

    dz0, dtg, dyg_a, d_b_glu = _rowwise(
        "l0_gate_bwd", post_b_bwd, n, tr, [(dgz0, 0, D_MODEL), (yg, 0, D_MODEL), (tg, 0, D_MODEL), (proj0, 1, D_MODEL)],
        [b_glu], [(D_MODEL, BF16), (D_MODEL, BF16), (D_MODEL, F32)], [(1, D_MODEL)])
    d_w_glu = _mm("l0_glu_dw", yg, dtg, "tn", out_dtype=BF16)
    dyg_b = _mm("l0_glu_dx", dtg, w["ssm_w_glu"], "nt")

    def post_a_bwd(da, db, yt, ut, dv):
        dy = (da + db) * _gelu_grad(yt)
        return dy, dy * dv, dy * ut

    dy0, du_skip, d_d = _rowwise("l0_gelu_bwd", post_a_bwd, n, tr,
                                 [(dyg_a, 0, D_MODEL), (dyg_b, 0, D_MODEL), (y0, 0, D_MODEL), (proj0, 0, D_MODEL)], [d_skip],
                                 [(D_MODEL, F32), (D_MODEL, F32)], [(1, D_MODEL)])
    s5_bwd = _s5_backward(proj0, dy0, dirs, saved, t, c)
    dproj0 = _rowwise("l0_in_grad", lambda a, b, cc, dz: jnp.concatenate([a + b + cc, dz], axis=1), n, tr,
                      [(du_skip, 0, D_MODEL), (s5_bwd[0][0], 0, D_MODEL), (s5_bwd[1][0], 0, D_MODEL), (dz0, 0, D_MODEL)], [],
                      [(2 * D_MODEL, BF16)], [])[0]
    d_w_in = _mm("l0_in_dw", h0, dproj0, "tn", out_dtype=BF16)
    dh0 = _mm("l0_in_dx", dproj0, w["ssm_w_in"], "nt")
    dx0, d_g0, d_scale0, d_shift0 = _norm_mod_bwd("l0_norm_bwd", xall, g0, scale0, dh0, dx1, n, tr, t)

    big = dict(ssm_w_in=d_w_in, ssm_w_glu=d_w_glu, ssm_w_out=d_w_out, attn_w_in=d_w_attn_in, attn_w_out=d_w_attn_out)
    small = dict(
        norm_g=jnp.concatenate([d_g0[0], d_g1[0]], axis=0), ssm_d=d_d[0], ssm_b_glu=d_b_glu[0],
        attn_q_norm=d_qn, attn_k_norm=d_kn, final_norm_g=d_gf[0, 0], **_s5_param_grads(raw, s5_bwd))
    zero_v = jnp.zeros((D_MODEL,), F32)
    d_mod_lat = jnp.stack([jnp.concatenate([d_shift0[0, 0], d_scale0[0, 0], d_gate0[0, 0]]),
                           jnp.concatenate([d_shift1[0, 0], d_scale1[0, 0], d_gate1[0, 0]])])
    d_mod_ctx = jnp.stack([jnp.concatenate([d_shift0[1, 0], d_scale0[1, 0], d_gate0[1, 0]]),
                           jnp.concatenate([d_shift1[1, 0], d_scale1[1, 0], zero_v])])
    return sq[0, 0, 0], dx0[:t], big, small, d_mod_lat, d_mod_ctx


def _adamw(name, w, g, m, v):
    rows, cols = w.shape
    tr = _largest_tile(rows, 256, 8)
    c1 = 1.0 / (1.0 - ADAM_B1 ** ADAM_STEP)
    c2 = 1.0 / (1.0 - ADAM_B2 ** ADAM_STEP)

    def fn(wt, gt, mt, vt):
        mn = ADAM_B1 * mt + (1.0 - ADAM_B1) * gt
        vn = ADAM_B2 * vt + (1.0 - ADAM_B2) * (gt * gt)
        delta = -ADAM_LR * ((mn * c1) / (jnp.sqrt(vn * c2) + ADAM_EPS) + ADAM_WD * wt)
        return delta, mn, vn

    return _rowwise(name, fn, rows, tr, [(a, 0, cols) for a in (w, g, m, v)], [], [(cols, F32)] * 3, [])


BIG = ("ssm_w_in", "ssm_w_glu", "ssm_w_out", "attn_w_in", "attn_w_out")
COL_SHARDED = ("ssm_w_in", "attn_w_in")
WEIGHTS = ("c_ctx", "w_mod", "b_mod", "norm_g", "ssm_w_in", "ssm_a_re", "ssm_a_im", "ssm_log_dt", "ssm_b_re", "ssm_b_im",
           "ssm_c_re", "ssm_c_im", "ssm_d", "ssm_w_glu", "ssm_b_glu", "ssm_w_out", "attn_w_in", "attn_q_norm",
           "attn_k_norm", "attn_w_out", "final_norm_g")
SMALL = tuple(k for k in WEIGHTS if k not in BIG and k != "w_mod")
PACK_W = 1024


def _attn_in_perm(x, inverse):
    a, kv = ATTN_W, 2 * KV_W
    if inverse:
        return jnp.concatenate([x[..., :a], x[..., 2 * a:], x[..., a:2 * a]], axis=-1)
    return jnp.concatenate([x[..., :a], x[..., a + kv:], x[..., a:a + kv]], axis=-1)


def _pack(arrays, dtype, row_unit):
    flat = jnp.concatenate([a.reshape(-1).astype(dtype) for a in arrays])
    rows = -(-flat.shape[0] // PACK_W)
    rows = -(-rows // row_unit) * row_unit
    flat = jnp.concatenate([flat, jnp.zeros((rows * PACK_W - flat.shape[0],), dtype)])
    return flat.reshape(rows, PACK_W)


def _unpack(buf, shapes):
    lead = buf.shape[:-2]
    flat = buf.reshape(lead + (-1,))
    out, off = [], 0
    for shp in shapes:
        size = math.prod(shp)
        out.append(flat[..., off:off + size].reshape(lead + tuple(shp)))
        off += size
    return out


def _half_shape(name, shard_shape):
    r, ccols = shard_shape
    return (r // 2, ccols)


def kernel(x, c, ctx, c_ctx, w_mod, b_mod, norm_g, ssm_w_in, ssm_a_re, ssm_a_im, ssm_log_dt, ssm_b_re, ssm_b_im, ssm_c_re, ssm_c_im, ssm_d, ssm_w_glu, ssm_b_glu, ssm_w_out, attn_w_in, attn_q_norm, attn_k_norm, attn_w_out, final_norm_g, loss_target, m_c_ctx, m_w_mod, m_b_mod, m_norm_g, m_ssm_w_in, m_ssm_a_re, m_ssm_a_im, m_ssm_log_dt, m_ssm_b_re, m_ssm_b_im, m_ssm_c_re, m_ssm_c_im, m_ssm_d, m_ssm_w_glu, m_ssm_b_glu, m_ssm_w_out, m_attn_w_in, m_attn_q_norm, m_attn_k_norm, m_attn_w_out, m_final_norm_g, v_c_ctx, v_w_mod, v_b_mod, v_norm_g, v_ssm_w_in, v_ssm_a_re, v_ssm_a_im, v_ssm_log_dt, v_ssm_b_re, v_ssm_b_im, v_ssm_c_re, v_ssm_c_im, v_ssm_d, v_ssm_w_glu, v_ssm_b_glu, v_ssm_w_out, v_attn_w_in, v_attn_q_norm, v_attn_k_norm, v_attn_w_out, v_final_norm_g):
    args = dict(locals())
    wts = {k: args[k] for k in WEIGHTS}
    mom_m = {k: args["m_" + k] for k in WEIGHTS}
    mom_v = {k: args["v_" + k] for k in WEIGHTS}
    mx, my, mc = lax.axis_index("x"), lax.axis_index("y"), lax.axis_index("c")
    chip = 2 * mx + my
    me = 2 * chip + mc

    halves = []
    for k in BIG:
        sh = wts[k][0]
        hr = sh.shape[0] // 2
        halves.append(lax.dynamic_slice_in_dim(sh, mc * hr, hr, axis=0))
    gathered = _gather_two_level("gather_weights", _pack(halves, BF16, 16))
    parts = _unpack(gathered, [h.shape for h in halves])
    w_full = {}
    for k, pc in zip(BIG, parts):
        hr, cols = pc.shape[1:]
        pc = pc.reshape(N_CHIP, 2, hr, cols)
        if k in COL_SHARDED:
            w_full[k] = pc.transpose(1, 2, 0, 3).reshape(2 * hr, N_CHIP * cols)
        else:
            w_full[k] = pc.reshape(N_CHIP * 2 * hr, cols)
    w_full["attn_w_in"] = _attn_in_perm(w_full["attn_w_in"], False)

    c_blk = jnp.concatenate([c, jnp.zeros((7, D_MODEL), F32)], axis=0)
    c_all = _exchange("gather_c", c_blk, True)[:, 0]
    cond = jnp.concatenate([c_all, c_ctx[None], jnp.zeros((7, D_MODEL), F32)], axis=0)
    s_cond, ds_cond = _rowwise("cond_silu", lambda t: (_silu(t), _silu_grad(t)), 16, 16, [(cond, 0, D_MODEL)], [],
                               [(D_MODEL, F32), (D_MODEL, F32)], [])
    w_mod_b = w_mod.astype(BF16)
    mcols = w_mod.shape[2]
    mod_part = jnp.stack([_mm(f"mod{i}", s_cond, w_mod_b[i], "nn") for i in range(2)])
    mod_g = _exchange("gather_mod", mod_part.reshape(32, mcols), True)
    mod_all = mod_g.reshape(N_CHIP, 2, 2, 16, mcols)[:, 0]
    mod_all = mod_all.transpose(1, 2, 0, 3).reshape(2, 16, N_CHIP * mcols) + b_mod[:, None, :]
    mods = []
    for i in range(2):
        lat = lax.dynamic_slice_in_dim(mod_all[i], me, 1, axis=0)[0]
        both = jnp.stack([lat, mod_all[i, 8]])
        mods.append((both[:, :D_MODEL], both[:, D_MODEL:2 * D_MODEL], both[:, 2 * D_MODEL:]))

    small_p = {k: wts[k] for k in SMALL if k != "c_ctx" and k != "b_mod"}
    sq, grad_x, big_g, small_g, d_mod_lat, d_mod_ctx = _example_step(x[0], ctx[0], loss_target[0], mods, w_full, small_p)
    loss = lax.psum(0.5 / D_MODEL * sq, ("x", "y", "c"))
    big_g["attn_w_in"] = _attn_in_perm(big_g["attn_w_in"], True)

    small_names = [k for k in SMALL if k not in ("c_ctx", "b_mod")]
    small_list = [small_g[k] for k in small_names] + [d_mod_lat, d_mod_ctx]
    small_shapes = [wts[k].shape for k in small_names] + [d_mod_lat.shape, d_mod_ctx.shape]
    packed = _pack(small_list, F32, 8 * N_DEV)
    slice_rows = packed.shape[0] // N_DEV
    slices = _exchange("scatter_small", packed.reshape(N_DEV, slice_rows, PACK_W), False)
    my_sum = _sum_slots("sum_small", slices)
    payload = jnp.concatenate([my_sum, _pack([d_mod_lat], F32, 8)], axis=0)
    sg = _exchange("gather_small", payload, True)
    summed = _unpack(sg[:, :slice_rows].reshape(packed.shape), small_shapes)
    grads = dict(zip(small_names, summed[:-2]))
    d_mod_lat_sum, d_mod_ctx_sum = summed[-2], summed[-1]
    grads["b_mod"] = d_mod_lat_sum + d_mod_ctx_sum
    d_mod_lat_all = _unpack(sg[:, slice_rows:], [d_mod_lat.shape])[0]

    g_w_mod, ds_cc = [], []
    for i in range(2):
        rows9 = jnp.concatenate([d_mod_lat_all[:, i], d_mod_ctx_sum[i][None], jnp.zeros((7, 3 * D_MODEL), F32)], axis=0)
        mine = lax.dynamic_slice_in_dim(rows9, chip * mcols, mcols, axis=1)
        g_w_mod.append(_mm(f"mod{i}_dw", s_cond, mine, "tn"))
        ds_cc.append(_mm(f"mod{i}_dx", mine, w_mod_b[i], "nt")[8])
    grads["w_mod"] = jnp.stack(g_w_mod)
    part = (ds_cc[0] + ds_cc[1]) * jnp.where(mc == 0, 1.0, 0.0)
    part_blk = jnp.concatenate([part[None], jnp.zeros((7, D_MODEL), F32)], axis=0)
    ds_all = _sum_slots("sum_c_ctx", _exchange("gather_c_ctx", part_blk, True))
    grads["c_ctx"] = ds_all[0] * ds_cond[8]

    blocks = []
    for k in BIG:
        g = big_g[k]
        rows, cols = g.shape
        if k in COL_SHARDED:
            blocks.append(g.reshape(2, rows // 2, N_CHIP, cols // N_CHIP).transpose(2, 0, 1, 3).reshape(N_DEV, -1))
        else:
            blocks.append(g.reshape(N_DEV, -1))
    sendbuf = jnp.concatenate(blocks, axis=1).astype(BF16)
    sendbuf = sendbuf.reshape(N_DEV, -1, PACK_W)
    recv = _exchange("scatter_big", sendbuf, False)
    mine = _sum_slots("sum_big", recv)
    both = _exchange("swap_halves", mine, True, sibling_only=True, chunks=8)
    half_shapes = [(wts[k].shape[1] // 2, wts[k].shape[2]) for k in BIG]
    for k, pc in zip(BIG, _unpack(both, half_shapes)):
        grads[k] = pc.reshape(wts[k].shape)

    delta, new_m, new_v = {}, {}, {}
    for k in BIG + ("w_mod",):
        shp = wts[k].shape
        two_d = (-1, shp[-1])
        res = _adamw("adamw_" + k, *[a.reshape(two_d) for a in (wts[k], grads[k], mom_m[k], mom_v[k])])
        delta[k], new_m[k], new_v[k] = [r.reshape(shp) for r in res]
    shapes = [wts[k].shape for k in SMALL]
    packed = [_pack([d[k] for k in SMALL], F32, 8) for d in (wts, grads, mom_m, mom_v)]
    res = _adamw("adamw_small", *packed)
    for dst, buf in zip((delta, new_m, new_v), res):
        for k, a in zip(SMALL, _unpack(buf, shapes)):
            dst[k] = a
    grads = {k: grads[k].reshape(wts[k].shape) for k in WEIGHTS}
    return (loss, grad_x[None], *[grads[k] for k in WEIGHTS], *[delta[k] for k in WEIGHTS],
            *[new_m[k] for k in WEIGHTS], *[new_v[k] for k in WEIGHTS])
```

```python
import functools
import math

import jax
import jax.numpy as jnp
from jax import lax
from jax.experimental import pallas as pl
from jax.experimental.pallas import tpu as pltpu

F32 = jnp.float32
BF16 = jnp.bfloat16

D_MODEL = 1024
NORM_EPS = 1e-6
SSM_GROUPS = 64
SSM_GROUP = 16
SSM_STATE = 64
LANES = 128
SLAB_W = LANES
N_SLAB = D_MODEL // SLAB_W
SLAB_GROUPS = SLAB_W // SSM_GROUP
HALF_W = SLAB_GROUPS * SSM_STATE
STATE_W = 2 * HALF_W
N_SEG = 8
HEAD_DIM = 64
N_Q_HEADS = 16
N_KV_HEADS = 4
KV_REP = N_Q_HEADS // N_KV_HEADS
ATTN_W = N_Q_HEADS * HEAD_DIM
KV_W = N_KV_HEADS * HEAD_DIM
GRID_W = 64
ROPE_THETA = 10000.0
N_DEV = 8
N_CHIP = 4
VMEM_LIMIT_BYTES = 56 * 1024 * 1024

ADAM_LR = 0.001
ADAM_B1 = 0.9
ADAM_B2 = 0.999
ADAM_EPS = 1e-08
ADAM_WD = 0.01
ADAM_STEP = 10


def _params(*sem):
    return pltpu.CompilerParams(dimension_semantics=sem, vmem_limit_bytes=VMEM_LIMIT_BYTES)


def _largest_tile(n, cap, unit):
    if n <= cap:
        return n
    t = (cap // unit) * unit
    while t >= unit:
        if n % t == 0:
            return t
        t -= unit
    raise ValueError(f"no tile for {n} (cap {cap}, unit {unit})")


def _rowwise(name, fn, n_rows, tr, row_ins, vec_ins, row_outs, red_outs, n_lat=None, want_flag=False):
    nt = n_rows // tr
    assert nt * tr == n_rows
    nlt = nt if n_lat is None else n_lat // tr

    def sel(i):
        return jnp.where(i >= nlt, 1, 0)

    arrays, in_specs = [], []
    for spec in row_ins:
        arr, cb, w = spec[:3]
        kind = spec[3] if len(spec) > 3 else None
        m = spec[4] if len(spec) > 4 else None
        if kind == "mod":
            imap = functools.partial(lambda i, cb, m: (i % m, cb), cb=cb, m=m)
        elif kind == "clamp":
            imap = functools.partial(lambda i, cb, m: (jnp.minimum(i, m - 1), cb), cb=cb, m=m)
        else:
            imap = functools.partial(lambda i, cb: (i, cb), cb=cb)
        arrays.append(arr)
        in_specs.append(pl.BlockSpec((tr, w), imap))
    for v in vec_ins:
        s, a, w = v.shape
        imap = (lambda i: (sel(i), 0, 0)) if s == 2 else (lambda i: (0, 0, 0))
        arrays.append(v)
        in_specs.append(pl.BlockSpec((1, a, w), imap))
    out_shapes, out_specs = [], []
    for w, dt in row_outs:
        out_shapes.append(jax.ShapeDtypeStruct((n_rows, w), dt))
        out_specs.append(pl.BlockSpec((tr, w), lambda i: (i, 0)))
    for s, w in red_outs:
        out_shapes.append(jax.ShapeDtypeStruct((s, 1, w), F32))
        imap = (lambda i: (sel(i), 0, 0)) if s == 2 else (lambda i: (0, 0, 0))
        out_specs.append(pl.BlockSpec((1, 1, w), imap))
    n_ri, n_vi, n_ro, n_rd = len(row_ins), len(vec_ins), len(row_outs), len(red_outs)

    def body(*refs):
        i = pl.program_id(0)
        rows = [r[...] for r in refs[:n_ri]]
        vecs = [r[0] for r in refs[n_ri:n_ri + n_vi]]
        outs = refs[n_ri + n_vi:]
        lead = [jnp.where(i < nlt, 1.0, 0.0).astype(F32)] if want_flag else []
        res = fn(*lead, *rows, *vecs)
        if not isinstance(res, (tuple, list)):
            res = (res,)
        assert len(res) == n_ro + n_rd
        for k in range(n_ro):
            outs[k][...] = res[k].astype(outs[k].dtype)
        for k in range(n_rd):
            part = jnp.sum(res[n_ro + k].astype(F32), axis=0, keepdims=True)
            first = i == 0
            if red_outs[k][0] == 2:
                first = jnp.logical_or(first, i == nlt)
            o = outs[n_ro + k]

            @pl.when(first)
            def _():
                o[0] = part

            @pl.when(jnp.logical_not(first))
            def _():
                o[0] = o[0] + part

    res = pl.pallas_call(
        body, grid=(nt,), in_specs=in_specs, out_specs=out_specs, out_shape=out_shapes,
        compiler_params=_params("arbitrary"), name=name)(*arrays)
    return res


def _vec(v):
    v = v.astype(F32)
    if v.ndim == 1:
        v = v[None]
    return v[:, None, :]


def _mm(name, a, b, mode, out_dtype=F32):
    if mode in ("nn", "nt"):
        m, k = a.shape
        n = b.shape[1] if mode == "nn" else b.shape[0]
        tm = _largest_tile(m, 1024, 8)
        tn = _largest_tile(n, 1024, 128)
        contract = (((1,), (0,)), ((), ())) if mode == "nn" else (((1,), (1,)), ((), ()))

        def body(a_ref, b_ref, o_ref):
            o_ref[...] = lax.dot_general(a_ref[...].astype(BF16), b_ref[...].astype(BF16), contract,
                                         preferred_element_type=F32).astype(o_ref.dtype)

        b_spec = pl.BlockSpec((k, tn), lambda i, j: (0, j)) if mode == "nn" else pl.BlockSpec((tn, k), lambda i, j: (j, 0))
        return pl.pallas_call(
            body, grid=(m // tm, n // tn),
            in_specs=[pl.BlockSpec((tm, k), lambda i, j: (i, 0)), b_spec],
            out_specs=pl.BlockSpec((tm, tn), lambda i, j: (i, j)),
            out_shape=jax.ShapeDtypeStruct((m, n), out_dtype),
            compiler_params=_params("parallel", "arbitrary"), name=name)(a, b)
    assert mode == "tn"
    r, k1 = a.shape
    k2 = b.shape[1]
    tr = _largest_tile(r, 1024, 8)
    t2 = _largest_tile(k2, 1024, 128)
    nr = r // tr

    def body(a_ref, b_ref, o_ref, acc_ref):
        part = lax.dot_general(a_ref[...].astype(BF16), b_ref[...].astype(BF16), (((0,), (0,)), ((), ())),
                               preferred_element_type=F32)
        i = pl.program_id(1)

        @pl.when(i == 0)
        def _():
            acc_ref[...] = part

        @pl.when(i > 0)
        def _():
            acc_ref[...] += part

        @pl.when(i == nr - 1)
        def _():
            o_ref[...] = acc_ref[...].astype(o_ref.dtype)

    return pl.pallas_call(
        body, grid=(k2 // t2, nr),
        in_specs=[pl.BlockSpec((tr, k1), lambda j, i: (i, 0)), pl.BlockSpec((tr, t2), lambda j, i: (i, j))],
        out_specs=pl.BlockSpec((k1, t2), lambda j, i: (0, j)),
        out_shape=jax.ShapeDtypeStruct((k1, k2), out_dtype),
        scratch_shapes=[pltpu.VMEM((k1, t2), F32)],
        compiler_params=_params("parallel", "arbitrary"), name=name)(a, b)


def _exchange(name, x, bcast, sibling_only=False, chunks=1):
    rels = [1] if sibling_only else list(range(1, N_DEV))
    n_slot = 2 if sibling_only else N_DEV
    blk = x.shape if bcast else x.shape[1:]

    def body(x_ref, o_ref, send_sems, recv_sems, local_sem):
        mx, my, mc = lax.axis_index("x"), lax.axis_index("y"), lax.axis_index("c")
        me = mc if sibling_only else 4 * mx + 2 * my + mc
        me_dev = 4 * mx + 2 * my + mc
        mine = pltpu.make_async_copy(x_ref if bcast else x_ref.at[me_dev], o_ref.at[me], local_sem)
        mine.start()
        copies = []
        rc = blk[0] // chunks
        for k, r in enumerate(rels):
            px = 1 - mx if (r >> 2) & 1 else mx
            py = 1 - my if (r >> 1) & 1 else my
            pc = 1 - mc if r & 1 else mc
            src = x_ref if bcast else x_ref.at[4 * px + 2 * py + pc]
            for ci in range(chunks):
                rows = pl.ds(ci * rc, rc)
                cp = pltpu.make_async_remote_copy(
                    src_ref=src.at[rows], dst_ref=o_ref.at[me, rows], send_sem=send_sems.at[k * chunks + ci],
                    recv_sem=recv_sems.at[k * chunks + ci], device_id=(px, py, pc), device_id_type=pl.DeviceIdType.MESH)
                cp.start()
                copies.append(cp)
        for cp in copies:
            cp.wait()
        mine.wait()

    return pl.pallas_call(
        body, out_shape=jax.ShapeDtypeStruct((n_slot,) + tuple(blk), x.dtype),
        in_specs=[pl.BlockSpec(memory_space=pl.ANY)], out_specs=pl.BlockSpec(memory_space=pl.ANY),
        scratch_shapes=[pltpu.SemaphoreType.DMA((len(rels) * chunks,)), pltpu.SemaphoreType.DMA((len(rels) * chunks,)),
                        pltpu.SemaphoreType.DMA],
        name=name)(x)


def _gather_two_level(name, x):
    def body(x_ref, o_ref, send_sems, recv_sems, local_sem):
        mx, my, mc = lax.axis_index("x"), lax.axis_index("y"), lax.axis_index("c")
        me, sibling = (mx, my, mc), (mx, my, 1 - mc)
        chips = [(1 - mx, my), (mx, 1 - my), (1 - mx, 1 - my)]

        def slot(px, py, pc):
            return o_ref.at[4 * px + 2 * py + pc]

        def copy(k, block, to, src=None):
            return pltpu.make_async_remote_copy(
                src_ref=slot(*block) if src is None else src, dst_ref=slot(*block), send_sem=send_sems.at[k],
                recv_sem=recv_sems.at[k], device_id=to, device_id_type=pl.DeviceIdType.MESH)

        mine = pltpu.make_async_copy(x_ref, slot(*me), local_sem)
        mine.start()
        first = [copy(0, me, sibling, src=x_ref)]
        first += [copy(1 + j, me, (*chip, mc), src=x_ref) for j, chip in enumerate(chips)]
        for cp in first:
            cp.start()
        passed = [copy(4 + j, (*chip, mc), sibling) for j, chip in enumerate(chips)]
        for j, chip in enumerate(chips):
            copy(1 + j, (*chip, mc), me).wait_recv()
            passed[j].start()
        copy(0, sibling, me).wait_recv()
        for j, chip in enumerate(chips):
            copy(4 + j, (*chip, 1 - mc), me).wait_recv()
        for cp in first + passed:
            cp.wait_send()
        mine.wait()

    return pl.pallas_call(
        body, out_shape=jax.ShapeDtypeStruct((N_DEV,) + tuple(x.shape), x.dtype),
        in_specs=[pl.BlockSpec(memory_space=pl.ANY)], out_specs=pl.BlockSpec(memory_space=pl.ANY),
        scratch_shapes=[pltpu.SemaphoreType.DMA((N_DEV - 1,)), pltpu.SemaphoreType.DMA((N_DEV - 1,)),
                        pltpu.SemaphoreType.DMA],
        name=name)(x)


def _sum_slots(name, x):
    s, r, w = x.shape
    tr = _largest_tile(r, 256, 8)

    def body(x_ref, o_ref):
        acc = x_ref[0].astype(F32)
        for j in range(1, s):
            acc = acc + x_ref[j].astype(F32)
        o_ref[...] = acc

    return pl.pallas_call(
        body, grid=(r // tr,), in_specs=[pl.BlockSpec((s, tr, w), lambda i: (0, i, 0))],
        out_specs=pl.BlockSpec((tr, w), lambda i: (i, 0)), out_shape=jax.ShapeDtypeStruct((r, w), F32),
        compiler_params=_params("parallel"), name=name)(x)


def _sigmoid(x):
    return 1.0 / (1.0 + jnp.exp(-x))


def _silu(x):
    return x * _sigmoid(x)


def _silu_grad(x):
    s = _sigmoid(x)
    return s * (1.0 + x * (1.0 - s))


_INV_SQRT2 = 1.0 / math.sqrt(2.0)
_INV_SQRT2PI = 1.0 / math.sqrt(2.0 * math.pi)


def _gelu(x):
    return 0.5 * x * (1.0 + lax.erf(x * _INV_SQRT2))


def _gelu_grad(x):
    return 0.5 * (1.0 + lax.erf(x * _INV_SQRT2)) + x * jnp.exp(-0.5 * x * x) * _INV_SQRT2PI


def _rms_hat(x):
    r = lax.rsqrt(jnp.mean(x * x, axis=-1, keepdims=True) + NORM_EPS)
    return x * r, r


def _rms_bwd(xh, r, dxh):
    return r * (dxh - xh * jnp.mean(dxh * xh, axis=-1, keepdims=True))


def _norm_mod_fwd(name, x, g, scale, shift, n_rows, tr, n_lat):
    def fn(xt, gv, sc, sh):
        xh, _ = _rms_hat(xt)
        return (xh * gv) * (1.0 + sc) + sh

    return _rowwise(name, fn, n_rows, tr, [(x, 0, D_MODEL)], [g, scale, shift], [(D_MODEL, BF16)], [], n_lat=n_lat)[0]


def _norm_mod_bwd(name, x, g, scale, dh, dres, n_rows, tr, n_lat, prev=None):
    nlt = n_lat // tr

    def fn(flag, xt, dht, drt, *rest):
        gv, sc = rest[-2:] if prev is None else rest[1:3]
        xh, r = _rms_hat(xt)
        n = xh * gv
        dn = dht * (1.0 + sc)
        dx = _rms_bwd(xh, r, dn * gv) + flag * drt
        if prev is None:
            return dx, dn * xh, dht * n, dht
        return dx, rest[3] * dx, dn * xh, dht * n, dht, dx * rest[0]

    rows = [(x, 0, D_MODEL), (dh, 0, D_MODEL), (dres, 0, D_MODEL, "clamp", nlt)]
    vecs, row_outs, reds = [g, scale], [(D_MODEL, F32)], [(1, D_MODEL), (2, D_MODEL), (2, D_MODEL)]
    if prev is not None:
        rows.append((prev[0], 0, D_MODEL))
        vecs.append(prev[1])
        row_outs.append((D_MODEL, BF16))
        reds.append((2, D_MODEL))
    return _rowwise(name, fn, n_rows, tr, rows, vecs, row_outs, reds, n_lat=n_lat, want_flag=True)


def _s5_prep(a_re, a_im, log_dt, b_re, b_im, seg_lat, seg_ctx):
    def body(ar_ref, ai_ref, ld_ref, br_ref, bi_ref, abr_ref, abi_ref, bbr_ref, bbi_ref, alr_ref, ali_ref, acr_ref,
             aci_ref):
        lr, li = ar_ref[...], ai_ref[...]
        dt = jnp.exp(ld_ref[...])
        ldr, ldi = lr * dt, li * dt
        e = jnp.exp(ldr)
        abr, abi = e * jnp.cos(ldi), e * jnp.sin(ldi)
        abr_ref[...] = abr
        abi_ref[...] = abi
        den = lr * lr + li * li
        nr, ni = abr - 1.0, abi
        qr = (nr * lr + ni * li) / den
        qi = (ni * lr - nr * li) / den
        br, bi = br_ref[...], bi_ref[...]
        bbr_ref[...] = qr[None] * br - qi[None] * bi
        bbi_ref[...] = qr[None] * bi + qi[None] * br
        for seg, r_ref, i_ref in ((seg_lat, alr_ref, ali_ref), (seg_ctx, acr_ref, aci_ref)):
            es = jnp.exp(ldr * float(seg))
            r_ref[...] = es * jnp.cos(ldi * float(seg))
            i_ref[...] = es * jnp.sin(ldi * float(seg))

    sm = jax.ShapeDtypeStruct(a_re.shape, F32)
    big = jax.ShapeDtypeStruct(b_re.shape, F32)
    return pl.pallas_call(body, out_shape=[sm, sm, big, big, sm, sm, sm, sm], name="s5_prep")(
        a_re, a_im, log_dt, b_re, b_im)


def _s5_prep_bwd(a_re, a_im, log_dt, b_re, b_im, dabr, dabi, dbbr, dbbi):
    def body(ar_ref, ai_ref, ld_ref, br_ref, bi_ref, dabr_ref, dabi_ref, dbbr_ref, dbbi_ref,
             dar_ref, dai_ref, dld_ref, dbr_ref, dbi_ref):
        lr, li = ar_ref[...], ai_ref[...]
        dt = jnp.exp(ld_ref[...])
        ldr, ldi = lr * dt, li * dt
        e = jnp.exp(ldr)
        abr, abi = e * jnp.cos(ldi), e * jnp.sin(ldi)
        den = lr * lr + li * li
        nr, ni = abr - 1.0, abi
        qr = (nr * lr + ni * li) / den
        qi = (ni * lr - nr * li) / den
        br, bi = br_ref[...], bi_ref[...]
        gbr, gbi = dbbr_ref[...], dbbi_ref[...]
        dbr_ref[...] = gbr * qr[None] + gbi * qi[None]
        dbi_ref[...] = gbi * qr[None] - gbr * qi[None]
        dqr = jnp.sum(gbr * br + gbi * bi, axis=0)
        dqi = jnp.sum(gbi * br - gbr * bi, axis=0)
        dnr = (dqr * lr - dqi * li) / den
        dni = (dqr * li + dqi * lr) / den
        dlr_q = (dqr * (nr - 2.0 * lr * qr) + dqi * (ni - 2.0 * lr * qi)) / den
        dli_q = (dqr * (ni - 2.0 * li * qr) + dqi * (-nr - 2.0 * li * qi)) / den
        gar = dabr_ref[...] + dnr
        gai = dabi_ref[...] + dni
        dldr = gar * abr + gai * abi
        dldi = gai * abr - gar * abi
        dar_ref[...] = dldr * dt + dlr_q
        dai_ref[...] = dldi * dt + dli_q
        ddt = jnp.sum(dldr * lr + dldi * li, axis=1, keepdims=True)
        dld_ref[...] = ddt * dt

    sm = jax.ShapeDtypeStruct(a_re.shape, F32)
    big = jax.ShapeDtypeStruct(b_re.shape, F32)
    return pl.pallas_call(body, out_shape=[sm, sm, jax.ShapeDtypeStruct(log_dt.shape, F32), big, big],
                          name="s5_prep_bwd")(a_re, a_im, log_dt, b_re, b_im, dabr, dabi, dbbr, dbbi)


def _slab_cols(v):
    return v.reshape(N_SLAB, 1, HALF_W)


def _slab_pair(vr, vi):
    return jnp.concatenate([_slab_cols(vr), _slab_cols(vi)], axis=-1)


def _slab_in_matrix(bbr, bbi):
    eye = jnp.eye(SLAB_GROUPS, dtype=F32)

    def one(b):
        b = b.reshape(N_SLAB, SLAB_GROUPS, SSM_STATE, SSM_GROUP)
        m = jnp.einsum("sgph,gk->sghkp", b, eye)
        return m.reshape(N_SLAB, SLAB_W, HALF_W)

    return jnp.concatenate([one(bbr), one(bbi)], axis=-1)


def _slab_out_matrix(cr, ci):
    eye = jnp.eye(SLAB_GROUPS, dtype=F32)

    def one(c):
        c = c.reshape(N_SLAB, SLAB_GROUPS, SSM_GROUP, SSM_STATE)
        m = jnp.einsum("sghp,gk->skpgh", c, eye)
        return m.reshape(N_SLAB, HALF_W, SLAB_W)

    return jnp.concatenate([one(cr), one(-ci)], axis=1)


def _slab_diag(m):
    m = m.reshape(N_SLAB, SLAB_GROUPS, SSM_GROUP, 2, SLAB_GROUPS, SSM_STATE)
    d = jnp.stack([m[:, g, :, :, g, :] for g in range(SLAB_GROUPS)], axis=1)
    return d.transpose(3, 0, 1, 2, 4).reshape(2, SSM_GROUPS, SSM_GROUP, SSM_STATE)


def _cmul(ar, ai, xr, xi, conj):
    if conj:
        return ar * xr + ai * xi, ar * xi - ai * xr
    return ar * xr - ai * xi, ar * xi + ai * xr


def _s5_pow_table(name, abar, seg, falling, conj):
    assert seg >= 8 and seg & (seg - 1) == 0

    def body(a_ref, o_ref, t_ref):
        ar, ai = a_ref[0, :, :HALF_W], a_ref[0, :, HALF_W:]
        if conj:
            ai = -ai
        rr, ri = [jnp.ones_like(ar)], [jnp.zeros_like(ai)]
        for _ in range(7):
            pr, pi = _cmul(ar, ai, rr[-1], ri[-1], False)
            rr.append(pr)
            ri.append(pi)
        sr, si = _cmul(ar, ai, rr[-1], ri[-1], False)
        if falling:
            rr, ri = rr[::-1], ri[::-1]
        first = slice(seg - 8, seg) if falling else slice(0, 8)
        t_ref[first, :HALF_W] = jnp.concatenate(rr, axis=0)
        t_ref[first, HALF_W:] = jnp.concatenate(ri, axis=0)
        size = 8
        while size < seg:
            src = slice(seg - size, seg) if falling else slice(0, size)
            dst = slice(seg - 2 * size, seg - size) if falling else slice(size, 2 * size)
            pr, pi = _cmul(sr, si, t_ref[src, :HALF_W], t_ref[src, HALF_W:], False)
            t_ref[dst, :HALF_W] = pr
            t_ref[dst, HALF_W:] = pi
            sr, si = _cmul(sr, si, sr, si, False)
            size *= 2
        o_ref[0] = t_ref[...].astype(BF16)

    return pl.pallas_call(
        body, grid=(N_SLAB,), in_specs=[pl.BlockSpec((1, 1, STATE_W), lambda s: (s, 0, 0))],
        out_specs=pl.BlockSpec((1, seg, STATE_W), lambda s: (s, 0, 0)),
        out_shape=jax.ShapeDtypeStruct((N_SLAB, seg, STATE_W), BF16),
        scratch_shapes=[pltpu.VMEM((seg, STATE_W), F32)], compiler_params=_params("parallel"), name=name)(abar)


def _s5_ends(name, x, n_rows, row0, table, m_mat):
    seg = n_rows // N_SEG
    rb = row0 // n_rows
    tn = (((0,), (0,)), ((), ()))

    def body(x_ref, t_ref, m_ref, z_ref):
        mr, mi = m_ref[0, :, :HALF_W], m_ref[0, :, HALF_W:]
        for j in range(N_SEG):
            t = lax.dot_general(x_ref[j * seg:(j + 1) * seg, :].astype(BF16), t_ref[0], tn,
                                preferred_element_type=F32)
            tr_, ti_ = t[:, :HALF_W], t[:, HALF_W:]
            z_ref[0, j:j + 1, :HALF_W] = jnp.sum(mr * tr_ - mi * ti_, axis=0, keepdims=True)
            z_ref[0, j:j + 1, HALF_W:] = jnp.sum(mr * ti_ + mi * tr_, axis=0, keepdims=True)

    return pl.pallas_call(
        body, grid=(N_SLAB,),
        in_specs=[pl.BlockSpec((n_rows, SLAB_W), lambda s: (rb, s)),
                  pl.BlockSpec((1, seg, STATE_W), lambda s: (s, 0, 0)),
                  pl.BlockSpec((1, SLAB_W, STATE_W), lambda s: (s, 0, 0))],
        out_specs=pl.BlockSpec((1, N_SEG, STATE_W), lambda s: (s, 0, 0)),
        out_shape=jax.ShapeDtypeStruct((N_SLAB, N_SEG, STATE_W), F32),
        compiler_params=_params("parallel"), name=name)(x, table, m_mat)


def _s5_carry(name, z, a_seg, init, descending, conj):
    order = list(range(N_SEG - 1, -1, -1)) if descending else list(range(N_SEG))

    def body(z_ref, a_ref, i_ref, e_ref, o_ref):
        ar, ai = a_ref[:, :HALF_W], a_ref[:, HALF_W:]
        cr, ci = i_ref[:, :HALF_W], i_ref[:, HALF_W:]
        for j in order:
            e_ref[j, :, :HALF_W] = cr
            e_ref[j, :, HALF_W:] = ci
            pr, pi = _cmul(ar, ai, cr, ci, conj)
            cr = pr + z_ref[j, :, :HALF_W]
            ci = pi + z_ref[j, :, HALF_W:]
        o_ref[:, :HALF_W] = cr
        o_ref[:, HALF_W:] = ci

    return pl.pallas_call(body, out_shape=[jax.ShapeDtypeStruct(z.shape, F32), jax.ShapeDtypeStruct(init.shape, F32)],
                          name=name)(z, a_seg, init)


def _seg_major(v):
    return jnp.transpose(v, (1, 0, 2))


def _s5_scan(name, u, n_rows, row0, b_mat, c_mat, abar, h_in, descending, y_alias=None, y_rows=None):
    seg = n_rows // N_SEG
    ta = min(32, seg)
    nk = seg // ta
    assert seg * N_SEG == n_rows and nk * ta == seg and row0 % n_rows == 0 and ta % 8 == 0
    rb = row0 // n_rows
    tile = ta * N_SEG

    def body(*refs):
        u_ref, b_ref, c_ref, a_ref, hin_ref = refs[:5]
        y_ref, hch_ref, st_ref, up_ref, h_ref = refs[-5:]
        k = pl.program_id(1)
        kk = nk - 1 - k if descending else k
        a0 = kk * ta

        @pl.when(k == 0)
        def _():
            st_ref[...] = hin_ref[0]

        hch_ref[0, 0] = st_ref[...]
        for al in range(ta):
            up_ref[al * N_SEG:(al + 1) * N_SEG, :] = u_ref[pl.ds(a0 + al, N_SEG, stride=seg), :]
        h_ref[...] = jnp.dot(up_ref[...].astype(BF16), b_ref[0], preferred_element_type=F32)
        ar = jnp.broadcast_to(a_ref[0, :, :HALF_W], (N_SEG, HALF_W))
        ai = jnp.broadcast_to(a_ref[0, :, HALF_W:], (N_SEG, HALF_W))

        def step(i, carry):
            hr, hi = carry
            al = ta - 1 - i if descending else i
            row = pl.multiple_of(al * N_SEG, N_SEG)
            pr, pi = _cmul(ar, ai, hr, hi, False)
            hr = pr + h_ref[pl.ds(row, N_SEG), :HALF_W]
            hi = pi + h_ref[pl.ds(row, N_SEG), HALF_W:]
            h_ref[pl.ds(row, N_SEG), :HALF_W] = hr
            h_ref[pl.ds(row, N_SEG), HALF_W:] = hi
            return hr, hi

        hr, hi = lax.fori_loop(0, ta, step, (st_ref[:, :HALF_W], st_ref[:, HALF_W:]), unroll=True)
        st_ref[:, :HALF_W] = hr
        st_ref[:, HALF_W:] = hi
        yt = jnp.dot(h_ref[...].astype(BF16), c_ref[0], preferred_element_type=F32)
        for al in range(ta):
            y_ref[pl.ds(a0 + al, N_SEG, stride=seg), :] = yt[al * N_SEG:(al + 1) * N_SEG, :]

    u_spec = pl.BlockSpec((n_rows, SLAB_W), lambda s, k: (rb, s))
    b_spec = pl.BlockSpec((1, SLAB_W, STATE_W), lambda s, k: (s, 0, 0))
    c_spec = pl.BlockSpec((1, STATE_W, SLAB_W), lambda s, k: (s, 0, 0))
    a_spec = pl.BlockSpec((1, 1, STATE_W), lambda s, k: (s, 0, 0))
    st_spec = pl.BlockSpec((1, N_SEG, STATE_W), lambda s, k: (s, 0, 0))
    scratch = [pltpu.VMEM((N_SEG, STATE_W), F32), pltpu.VMEM((tile, SLAB_W), F32), pltpu.VMEM((tile, STATE_W), F32)]
    kmap = (lambda s, k: (s, nk - 1 - k, 0, 0)) if descending else (lambda s, k: (s, k, 0, 0))
    out_specs = [u_spec, pl.BlockSpec((1, 1, N_SEG, STATE_W), kmap)]
    out_shape = [jax.ShapeDtypeStruct((y_rows, D_MODEL), F32), jax.ShapeDtypeStruct((N_SLAB, nk, N_SEG, STATE_W), F32)]
    in_specs = [u_spec, b_spec, c_spec, a_spec, st_spec]
    args = [u, b_mat, c_mat, abar, h_in]
    aliases = {}
    if y_alias is not None:
        in_specs.append(pl.BlockSpec(memory_space=pl.ANY))
        args.append(y_alias)
        aliases = {5: 0}
    return pl.pallas_call(
        body, grid=(N_SLAB, nk), in_specs=in_specs, out_specs=out_specs, out_shape=out_shape, scratch_shapes=scratch,
        input_output_aliases=aliases, compiler_params=_params("parallel", "arbitrary"), name=name)(*args)


def _s5_scan_bwd(name, u, dy, n_rows, row0, b_mat, bt_mat, ct_mat, abar, h_chunks, g_in, descending,
                 du_alias=None, du_rows=None):
    seg = n_rows // N_SEG
    ta = min(32, seg)
    nk = seg // ta
    rb = row0 // n_rows
    tile = ta * N_SEG
    g_desc = not descending

    def body(*refs):
        u_ref, dy_ref, b_ref, bt_ref, ct_ref, a_ref, hch_ref, gin_ref = refs[:8]
        du_ref, db_ref, dc_ref, da_ref, st_ref, up_ref, dyp_ref, h_ref, g_ref = refs[-9:]
        k = pl.program_id(1)
        kk = nk - 1 - k if g_desc else k
        a0 = kk * ta
        ar = jnp.broadcast_to(a_ref[0, :, :HALF_W], (N_SEG, HALF_W))
        ai = jnp.broadcast_to(a_ref[0, :, HALF_W:], (N_SEG, HALF_W))

        @pl.when(k == 0)
        def _():
            st_ref[...] = gin_ref[0]

        for al in range(ta):
            dyp_ref[al * N_SEG:(al + 1) * N_SEG, :] = dy_ref[pl.ds(a0 + al, N_SEG, stride=seg), :]
            up_ref[al * N_SEG:(al + 1) * N_SEG, :] = u_ref[pl.ds(a0 + al, N_SEG, stride=seg), :]
        g_ref[...] = jnp.dot(dyp_ref[...].astype(BF16), ct_ref[0], preferred_element_type=F32)
        h_ref[...] = jnp.dot(up_ref[...].astype(BF16), b_ref[0], preferred_element_type=F32)
        h0r, h0i = hch_ref[0, 0, :, :HALF_W], hch_ref[0, 0, :, HALF_W:]

        def hstep(i, carry):
            hr, hi = carry
            al = ta - 1 - i if descending else i
            row = pl.multiple_of(al * N_SEG, N_SEG)
            pr, pi = _cmul(ar, ai, hr, hi, False)
            hr = pr + h_ref[pl.ds(row, N_SEG), :HALF_W]
            hi = pi + h_ref[pl.ds(row, N_SEG), HALF_W:]
            h_ref[pl.ds(row, N_SEG), :HALF_W] = hr
            h_ref[pl.ds(row, N_SEG), HALF_W:] = hi
            return hr, hi

        lax.fori_loop(0, ta, hstep, (h0r, h0i), unroll=True)

        def gstep(i, carry):
            gr, gi = carry
            al = ta - 1 - i if g_desc else i
            row = pl.multiple_of(al * N_SEG, N_SEG)
            pr, pi = _cmul(ar, ai, gr, gi, True)
            gr = pr + g_ref[pl.ds(row, N_SEG), :HALF_W]
            gi = pi + g_ref[pl.ds(row, N_SEG), HALF_W:]
            g_ref[pl.ds(row, N_SEG), :HALF_W] = gr
            g_ref[pl.ds(row, N_SEG), HALF_W:] = gi
            return gr, gi

        gr, gi = lax.fori_loop(0, ta, gstep, (st_ref[:, :HALF_W], st_ref[:, HALF_W:]), unroll=True)
        st_ref[:, :HALF_W] = gr
        st_ref[:, HALF_W:] = gi

        gb = g_ref[...].astype(BF16)
        dut = jnp.dot(gb, bt_ref[0], preferred_element_type=F32)
        for al in range(ta):
            du_ref[pl.ds(a0 + al, N_SEG, stride=seg), :] = dut[al * N_SEG:(al + 1) * N_SEG, :]
        tn = (((0,), (0,)), ((), ()))
        dbp = lax.dot_general(up_ref[...].astype(BF16), gb, tn, preferred_element_type=F32)
        dcp = lax.dot_general(dyp_ref[...].astype(BF16), h_ref[...].astype(BF16), tn, preferred_element_type=F32)
        inner = (ta - 1) * N_SEG
        if descending:
            g_in_r, g_in_i = g_ref[0:inner, :HALF_W], g_ref[0:inner, HALF_W:]
            p_in_r, p_in_i = h_ref[N_SEG:tile, :HALF_W], h_ref[N_SEG:tile, HALF_W:]
            g_ed_r, g_ed_i = g_ref[inner:tile, :HALF_W], g_ref[inner:tile, HALF_W:]
        else:
            g_in_r, g_in_i = g_ref[N_SEG:tile, :HALF_W], g_ref[N_SEG:tile, HALF_W:]
            p_in_r, p_in_i = h_ref[0:inner, :HALF_W], h_ref[0:inner, HALF_W:]
            g_ed_r, g_ed_i = g_ref[0:N_SEG, :HALF_W], g_ref[0:N_SEG, HALF_W:]
        dar = g_ed_r * h0r + g_ed_i * h0i
        dai = g_ed_i * h0r - g_ed_r * h0i
        if ta > 1:
            dar = dar + jnp.sum((g_in_r * p_in_r + g_in_i * p_in_i).reshape(ta - 1, N_SEG, HALF_W), axis=0)
            dai = dai + jnp.sum((g_in_i * p_in_r - g_in_r * p_in_i).reshape(ta - 1, N_SEG, HALF_W), axis=0)

        @pl.when(k == 0)
        def _():
            db_ref[0] = dbp
            dc_ref[0] = dcp
            da_ref[0, :, :HALF_W] = dar
            da_ref[0, :, HALF_W:] = dai

        @pl.when(k > 0)
        def _():
            db_ref[0] += dbp
            dc_ref[0] += dcp
            da_ref[0, :, :HALF_W] += dar
            da_ref[0, :, HALF_W:] += dai

    u_spec = pl.BlockSpec((n_rows, SLAB_W), lambda s, k: (rb, s))
    m_spec = pl.BlockSpec((1, SLAB_W, STATE_W), lambda s, k: (s, 0, 0))
    mt_spec = pl.BlockSpec((1, STATE_W, SLAB_W), lambda s, k: (s, 0, 0))
    a_spec = pl.BlockSpec((1, 1, STATE_W), lambda s, k: (s, 0, 0))
    st_spec = pl.BlockSpec((1, N_SEG, STATE_W), lambda s, k: (s, 0, 0))
    st_shape = jax.ShapeDtypeStruct((N_SLAB, N_SEG, STATE_W), F32)
    kmap = (lambda s, k: (s, nk - 1 - k, 0, 0)) if g_desc else (lambda s, k: (s, k, 0, 0))
    in_specs = [u_spec, u_spec, m_spec, mt_spec, m_spec, a_spec, pl.BlockSpec((1, 1, N_SEG, STATE_W), kmap), st_spec]
    args = [u, dy, b_mat, bt_mat, ct_mat, abar, h_chunks, g_in]
    aliases = {}
    if du_alias is not None:
        in_specs.append(pl.BlockSpec(memory_space=pl.ANY))
        args.append(du_alias)
        aliases = {8: 0}
    acc_shape = jax.ShapeDtypeStruct((N_SLAB, SLAB_W, STATE_W), F32)
    out_specs = [u_spec, m_spec, m_spec, st_spec]
    out_shape = [jax.ShapeDtypeStruct((du_rows, D_MODEL), F32), acc_shape, acc_shape, st_shape]
    scratch = [pltpu.VMEM((N_SEG, STATE_W), F32), pltpu.VMEM((tile, SLAB_W), F32), pltpu.VMEM((tile, SLAB_W), F32),
               pltpu.VMEM((tile, STATE_W), F32), pltpu.VMEM((tile, STATE_W), F32)]
    return pl.pallas_call(
        body, grid=(N_SLAB, nk), in_specs=in_specs, out_specs=out_specs, out_shape=out_shape, scratch_shapes=scratch,
        input_output_aliases=aliases, compiler_params=_params("parallel", "arbitrary"), name=name)(*args)


ROPE_HALF = HEAD_DIM // 4
TABLE_W = 2 * HEAD_DIM
Q_SCALE = 1.0 / math.sqrt(HEAD_DIM)
HEADS_PER_BLOCK = 2 * KV_REP
Q_BLOCK_W = HEADS_PER_BLOCK * HEAD_DIM


def _rope_tables(n_lat, n_ctx):
    rows = n_lat // GRID_W
    freqs = ROPE_THETA ** (-jnp.arange(ROPE_HALF, dtype=F32) / ROPE_HALF)
    ang_r = jnp.arange(rows, dtype=F32)[:, None] * freqs[None]
    ang_c = jnp.arange(GRID_W, dtype=F32)[:, None] * freqs[None]
    by_row = lambda v: jnp.repeat(v, GRID_W, axis=0)
    by_col = lambda v: jnp.tile(v, (rows, 1))
    cos = jnp.concatenate([by_row(jnp.cos(ang_r)), by_row(jnp.cos(ang_r)), by_col(jnp.cos(ang_c)), by_col(jnp.cos(ang_c))] * 2,
                          axis=1)
    sin = jnp.concatenate([by_row(jnp.sin(ang_r)), by_row(jnp.sin(ang_r)), by_col(jnp.sin(ang_c)), by_col(jnp.sin(ang_c))] * 2,
                          axis=1)
    cos = jnp.concatenate([cos, jnp.ones((n_ctx, TABLE_W), F32)], axis=0)
    sin = jnp.concatenate([sin, jnp.zeros((n_ctx, TABLE_W), F32)], axis=0)
    return cos, sin


def _rot_half(v):
    w = v.shape[1]
    ahead = pltpu.roll(v, w - ROPE_HALF, axis=1)
    behind = pltpu.roll(v, ROPE_HALF, axis=1)
    lane = lax.broadcasted_iota(jnp.int32, v.shape, 1)
    return jnp.where((lane % (2 * ROPE_HALF)) < ROPE_HALF, -ahead, behind)


def _head_mean(v, sel, selt):
    m = jnp.dot(v, sel, precision=lax.Precision.HIGH, preferred_element_type=F32) * (1.0 / HEAD_DIM)
    return jnp.dot(m, selt, precision=lax.Precision.HIGH, preferred_element_type=F32)


def _head_selectors(n_heads):
    sel = jnp.repeat(jnp.eye(n_heads, dtype=F32), HEAD_DIM, axis=0)
    return sel[None], sel.T[None]


def _head_norm(x, sel, selt):
    r = lax.rsqrt(_head_mean(x * x, sel, selt) + NORM_EPS)
    return x * r, r


def _qk_prep(proj, qn, kn, cos, sin, n, tr):
    qw, kw = _vec(jnp.tile(qn, N_Q_HEADS)), _vec(jnp.tile(kn, N_KV_HEADS))
    sq, sqt = _head_selectors(N_Q_HEADS)
    sk, skt = _head_selectors(N_KV_HEADS)

    def fn(qr, kvr, ct, st, qwv, kwv, s16, s16t, s4, s4t):
        outs = []
        for x, wv, sel, selt, scale in ((qr, qwv, s16, s16t, Q_SCALE), (kvr[:, :KV_W], kwv, s4, s4t, 1.0)):
            reps = x.shape[1] // TABLE_W
            cw, sw = jnp.tile(ct, (1, reps)), jnp.tile(st, (1, reps))
            xh, _ = _head_norm(x, sel, selt)
            nrm = xh * wv
            outs.append((nrm * cw + _rot_half(nrm) * sw) * scale)
        return outs[0], outs[1], kvr[:, KV_W:]

    return _rowwise("l1_qk_prep", fn, n, tr,
                    [(proj, 0, ATTN_W), (proj, 2 * ATTN_W // (2 * KV_W), 2 * KV_W), (cos, 0, TABLE_W), (sin, 0, TABLE_W)],
                    [qw, kw, sq, sqt, sk, skt], [(ATTN_W, BF16), (KV_W, BF16), (KV_W, BF16)], [])


def _qk_prep_bwd(proj, qn, kn, cos, sin, dq, dz, dk, dv, n, n_lat, tr):
    qw, kw = _vec(jnp.tile(qn, N_Q_HEADS)), _vec(jnp.tile(kn, N_KV_HEADS))
    sq, sqt = _head_selectors(N_Q_HEADS)
    sk, skt = _head_selectors(N_KV_HEADS)
    nlt = n_lat // tr

    def fn(flag, qr, kvr, ct, st, dqt, dzt, dkt, dvt, qwv, kwv, s16, s16t, s4, s4t):
        dxs, dws = [], []
        for x, dy, wv, sel, selt in ((qr, dqt * (flag * Q_SCALE), qwv, s16, s16t), (kvr[:, :KV_W], dkt, kwv, s4, s4t)):
            reps = x.shape[1] // TABLE_W
            cw, sw = jnp.tile(ct, (1, reps)), jnp.tile(st, (1, reps))
            xh, r = _head_norm(x, sel, selt)
            dn = dy * cw - _rot_half(dy * sw)
            dxh = dn * wv
            dxs.append(r * (dxh - xh * _head_mean(dxh * xh, sel, selt)))
            dws.append(dn * xh)
        return jnp.concatenate([dxs[0], dzt * flag, dxs[1], dvt], axis=1), dws[0], dws[1]

    dproj, dqw, dkw = _rowwise(
        "l1_qk_prep_bwd", fn, n, tr,
        [(proj, 0, ATTN_W), (proj, 2 * ATTN_W // (2 * KV_W), 2 * KV_W), (cos, 0, TABLE_W), (sin, 0, TABLE_W),
         (dq, 0, ATTN_W, "clamp", nlt), (dz, 0, ATTN_W, "clamp", nlt), (dk, 0, KV_W), (dv, 0, KV_W)],
        [qw, kw, sq, sqt, sk, skt], [(2 * ATTN_W + 2 * KV_W, BF16)], [(1, ATTN_W), (1, KV_W)], n_lat=n_lat, want_flag=True)
    return dproj, dqw.reshape(N_Q_HEADS, HEAD_DIM).sum(0)[None], dkw.reshape(N_KV_HEADS, HEAD_DIM).sum(0)[None]


NT = (((1,), (1,)), ((), ()))


def _attn_fwd(q, k, v, t, tq, tk):
    n = k.shape[0]
    nkc = n // tk

    ts = _largest_tile(tq, 256, LANES)
    items = [(sub, j) for sub in range(tq // ts) for j in range(HEADS_PER_BLOCK)]

    def body(q_ref, k_ref, v_ref, o_ref, lse_ref, s_ref, m_ref, l_ref, acc_ref):
        def lanes(j):
            g = j // KV_REP
            return slice(j * HEAD_DIM, (j + 1) * HEAD_DIM), slice(g * HEAD_DIM, (g + 1) * HEAD_DIM)

        for idx in range(len(items) + 1):
            nxt = items[idx] if idx < len(items) else None
            cur = items[idx - 1] if idx > 0 else None
            sn, sc = idx % 2, (idx - 1) % 2
            if nxt is not None:
                rows_n = slice(nxt[0] * ts, (nxt[0] + 1) * ts)
                ql_n, kl_n = lanes(nxt[1])
                qv = q_ref[rows_n, ql_n]
                m_ref[sn] = jnp.full((ts, LANES), -jnp.inf, F32)
            if cur is not None:
                rows_c = slice(cur[0] * ts, (cur[0] + 1) * ts)
                ql_c, kl_c = lanes(cur[1])
                m_row = jnp.max(m_ref[sc], axis=-1, keepdims=True)
                l_ref[...] = jnp.zeros(l_ref.shape, F32)
                acc_ref[...] = jnp.zeros(acc_ref.shape, F32)

            def sweep(kc, c):
                off = pl.multiple_of(kc * tk, tk)
                if nxt is not None:
                    s = lax.dot_general(qv, k_ref[pl.ds(off, tk), kl_n], NT, preferred_element_type=F32)
                    s_ref[sn, :, pl.ds(off, tk)] = s
                    m = m_ref[sn]
                    for cb in range(tk // LANES):
                        m = jnp.maximum(m, s[:, cb * LANES:(cb + 1) * LANES])
                    m_ref[sn] = m
                if cur is not None:
                    p = jnp.exp(s_ref[sc, :, pl.ds(off, tk)] - m_row)
                    lsum = l_ref[...]
                    for cb in range(tk // LANES):
                        lsum = lsum + p[:, cb * LANES:(cb + 1) * LANES]
                    l_ref[...] = lsum
                    acc_ref[...] += jnp.dot(p.astype(BF16), v_ref[pl.ds(off, tk), kl_c], preferred_element_type=F32)
                return c

            lax.fori_loop(0, nkc, sweep, 0)
            if cur is not None:
                l_row = jnp.sum(l_ref[...], axis=-1, keepdims=True)
                o_ref[rows_c, ql_c] = acc_ref[...] / l_row
                lse_ref[0, rows_c, cur[1]:cur[1] + 1] = m_row + jnp.log(l_row)

    nb = ATTN_W // Q_BLOCK_W
    kspec = pl.BlockSpec((n, LANES), lambda b, i: (0, b))
    return pl.pallas_call(
        body, grid=(nb, t // tq),
        in_specs=[pl.BlockSpec((tq, Q_BLOCK_W), lambda b, i: (i, b)), kspec, kspec],
        out_specs=[pl.BlockSpec((tq, Q_BLOCK_W), lambda b, i: (i, b)),
                   pl.BlockSpec((1, tq, HEADS_PER_BLOCK), lambda b, i: (b, i, 0))],
        out_shape=[jax.ShapeDtypeStruct((t, ATTN_W), F32), jax.ShapeDtypeStruct((nb, t, HEADS_PER_BLOCK), F32)],
        scratch_shapes=[pltpu.VMEM((2, ts, n), F32), pltpu.VMEM((2, ts, LANES), F32), pltpu.VMEM((ts, LANES), F32),
                        pltpu.VMEM((ts, HEAD_DIM), F32)],
        compiler_params=_params("parallel", "parallel"), name="attn_fwd")(q, k, v)


def _attn_bwd(q, k, v, do, o, lse, t, tq, tk):
    n = k.shape[0]
    nkc = n // tk
    tn = (((0,), (0,)), ((), ()))

    def body(q_ref, k_ref, v_ref, do_ref, o_ref, lse_ref, dq_ref, dk_ref, dv_ref, acc_ref):
        @pl.when(pl.program_id(1) == 0)
        def _():
            dk_ref[...] = jnp.zeros(dk_ref.shape, F32)
            dv_ref[...] = jnp.zeros(dv_ref.shape, F32)

        for j0 in range(0, HEADS_PER_BLOCK, 2):
            kl = slice((j0 // KV_REP) * HEAD_DIM, (j0 // KV_REP + 1) * HEAD_DIM)
            heads = []
            for a in range(2):
                j = j0 + a
                ql = slice(j * HEAD_DIM, (j + 1) * HEAD_DIM)
                qv, dov = q_ref[:, ql], do_ref[:, ql]
                dl_v = jnp.sum(dov.astype(F32) * o_ref[:, ql], axis=-1, keepdims=True)
                heads.append((ql, qv, dov, dl_v, lse_ref[0, :, j:j + 1]))
                acc_ref[a] = jnp.zeros((tq, HEAD_DIM), F32)

            def step(kc, c):
                off = pl.multiple_of(kc * tk, tk)
                kt = k_ref[pl.ds(off, tk), kl]
                vt = v_ref[pl.ds(off, tk), kl]
                dv_part, dk_part = None, None
                for a, (_, qv, dov, dl_v, lse_v) in enumerate(heads):
                    s = lax.dot_general(qv, kt, NT, preferred_element_type=F32)
                    p = jnp.exp(s - lse_v)
                    dp = lax.dot_general(dov, vt, NT, preferred_element_type=F32)
                    ds = (p * (dp - dl_v)).astype(BF16)
                    acc_ref[a] += jnp.dot(ds, kt, preferred_element_type=F32)
                    dvp = lax.dot_general(p.astype(BF16), dov, tn, preferred_element_type=F32)
                    dkp = lax.dot_general(ds, qv, tn, preferred_element_type=F32)
                    dv_part = dvp if dv_part is None else dv_part + dvp
                    dk_part = dkp if dk_part is None else dk_part + dkp
                dv_ref[pl.ds(off, tk), kl] += dv_part
                dk_ref[pl.ds(off, tk), kl] += dk_part
                return c

            lax.fori_loop(0, nkc, step, 0)
            for a, h in enumerate(heads):
                dq_ref[:, h[0]] = acc_ref[a]

    nb = ATTN_W // Q_BLOCK_W
    qspec = pl.BlockSpec((tq, Q_BLOCK_W), lambda b, i: (i, b))
    kspec = pl.BlockSpec((n, LANES), lambda b, i: (0, b))
    cspec = pl.BlockSpec((1, tq, HEADS_PER_BLOCK), lambda b, i: (b, i, 0))
    return pl.pallas_call(
        body, grid=(nb, t // tq), in_specs=[qspec, kspec, kspec, qspec, qspec, cspec], out_specs=[qspec, kspec, kspec],
        out_shape=[jax.ShapeDtypeStruct((t, ATTN_W), F32), jax.ShapeDtypeStruct((n, KV_W), F32),
                   jax.ShapeDtypeStruct((n, KV_W), F32)],
        scratch_shapes=[pltpu.VMEM((2, tq, HEAD_DIM), F32)],
        compiler_params=_params("parallel", "arbitrary"), name="attn_bwd")(q, k, v, do, o, lse)


def _s5_system(p, n_lat, n_ctx):
    two_g = 2 * SSM_GROUPS
    a_re = p["ssm_a_re"].reshape(two_g, SSM_STATE)
    a_im = p["ssm_a_im"].reshape(two_g, SSM_STATE)
    log_dt = p["ssm_log_dt"].reshape(two_g, 1)
    b_re = p["ssm_b_re"].reshape(two_g, SSM_STATE, SSM_GROUP).transpose(2, 0, 1)
    b_im = p["ssm_b_im"].reshape(two_g, SSM_STATE, SSM_GROUP).transpose(2, 0, 1)
    raw = (a_re, a_im, log_dt, b_re, b_im)
    abr, abi, bbr, bbi, alr, ali, acr, aci = _s5_prep(*raw, n_lat // N_SEG, n_ctx // N_SEG)
    dirs = []
    for d in range(2):
        g = slice(d * SSM_GROUPS, (d + 1) * SSM_GROUPS)
        b_mat = _slab_in_matrix(bbr[:, g].transpose(1, 2, 0), bbi[:, g].transpose(1, 2, 0))
        c_mat = _slab_out_matrix(p["ssm_c_re"][0, d], p["ssm_c_im"][0, d])
        abar = _slab_pair(abr[g], abi[g])
        tables = {}
        for part, seg in (("lat", n_lat // N_SEG), ("ctx", n_ctx // N_SEG)):
            tables["h_" + part] = _s5_pow_table(f"s5_pow_h{d}_{part}", abar, seg, d == 0, False)
            tables["g_" + part] = _s5_pow_table(f"s5_pow_g{d}_{part}", abar, seg, d == 1, True)
        dirs.append(dict(
            b=b_mat.astype(BF16), bt=b_mat.transpose(0, 2, 1).astype(BF16), b32=b_mat,
            c=c_mat.astype(BF16), ct=c_mat.transpose(0, 2, 1).astype(BF16), ct32=c_mat.transpose(0, 2, 1),
            abar=abar, a_lat=_slab_pair(alr[g], ali[g])[:, 0], a_ctx=_slab_pair(acr[g], aci[g])[:, 0], **tables))
    return raw, dirs


def _s5_forward(proj, dirs, n_lat, n_ctx):
    n = n_lat + n_ctx
    zero_c = jnp.zeros((N_SLAB, STATE_W), F32)
    ys, saved = [], []
    for d, s in enumerate(dirs):
        desc = d == 1
        tag = f"s5f{d}"
        zc = _s5_ends(tag + "_ctx_ends", proj, n_ctx, n_lat, s["h_ctx"], s["b32"])
        ent_c, h0 = _s5_carry(tag + "_ctx_carry", _seg_major(zc), s["a_ctx"], zero_c, desc, False)
        y, hch_c = _s5_scan(tag + "_ctx", proj, n_ctx, n_lat, s["b"], s["c"], s["abar"], _seg_major(ent_c), desc,
                            y_rows=n)
        zl = _s5_ends(tag + "_lat_ends", proj, n_lat, 0, s["h_lat"], s["b32"])
        ent_l, _ = _s5_carry(tag + "_lat_carry", _seg_major(zl), s["a_lat"], h0, desc, False)
        y, hch_l = _s5_scan(tag + "_lat", proj, n_lat, 0, s["b"], s["c"], s["abar"], _seg_major(ent_l), desc,
                            y_alias=y, y_rows=n)
        ys.append(y)
        saved.append((hch_l, hch_c))
    return ys, saved


def _s5_backward(proj, dy, dirs, saved, n_lat, n_ctx):
    n = n_lat + n_ctx
    zero_c = jnp.zeros((N_SLAB, STATE_W), F32)
    out = []
    for d, s in enumerate(dirs):
        desc = d == 1
        tag = f"s5b{d}"
        hch_l, hch_c = saved[d]
        gl = _s5_ends(tag + "_lat_ends", dy, n_lat, 0, s["g_lat"], s["ct32"])
        ent_l, g0 = _s5_carry(tag + "_lat_carry", _seg_major(gl), s["a_lat"], zero_c, not desc, True)
        du, db_l, dc_l, da_l = _s5_scan_bwd(tag + "_lat", proj, dy, n_lat, 0, s["b"], s["bt"], s["ct"], s["abar"],
                                            hch_l, _seg_major(ent_l), desc, du_rows=n)
        gc = _s5_ends(tag + "_ctx_ends", dy, n_ctx, n_lat, s["g_ctx"], s["ct32"])
        ent_c, _ = _s5_carry(tag + "_ctx_carry", _seg_major(gc), s["a_ctx"], g0, not desc, True)
        du, db_c, dc_c, da_c = _s5_scan_bwd(tag + "_ctx", proj, dy, n_ctx, n_lat, s["b"], s["bt"], s["ct"], s["abar"],
                                            hch_c, _seg_major(ent_c), desc, du_alias=du, du_rows=n)
        out.append((du, db_l + db_c, dc_l + dc_c, da_l + da_c))
    return out


def _s5_param_grads(raw, bwd):
    dabr, dabi, dbbr, dbbi, dcr, dci = [], [], [], [], [], []
    for _, db, dc, da in bwd:
        da = jnp.sum(da, axis=1)
        dabr.append(da[:, :HALF_W].reshape(SSM_GROUPS, SSM_STATE))
        dabi.append(da[:, HALF_W:].reshape(SSM_GROUPS, SSM_STATE))
        dbd = _slab_diag(db)
        dbbr.append(dbd[0].transpose(1, 0, 2))
        dbbi.append(dbd[1].transpose(1, 0, 2))
        dcd = _slab_diag(dc)
        dcr.append(dcd[0])
        dci.append(-dcd[1])
    cat = lambda xs, ax: jnp.concatenate(xs, axis=ax)
    dar, dai, dld, dbr, dbi = _s5_prep_bwd(*raw, cat(dabr, 0), cat(dabi, 0), cat(dbbr, 1), cat(dbbi, 1))
    shp = (1, 2, SSM_GROUPS, SSM_STATE)
    b_shape = (1, 2, SSM_GROUPS, SSM_STATE, SSM_GROUP)
    return dict(
        ssm_a_re=dar.reshape(shp), ssm_a_im=dai.reshape(shp), ssm_log_dt=dld.reshape(1, 2, SSM_GROUPS),
        ssm_b_re=dbr.transpose(1, 2, 0).reshape(b_shape), ssm_b_im=dbi.transpose(1, 2, 0).reshape(b_shape),
        ssm_c_re=jnp.stack(dcr)[None], ssm_c_im=jnp.stack(dci)[None])


def _example_step(x, ctx, target, mods, w, p):
    t, c = x.shape[0], ctx.shape[0]
    n = t + c
    assert t % c == 0 and c % LANES == 0 and c % (8 * N_SEG) == 0 and t % GRID_W == 0
    tr = _largest_tile(c, 256, 8)
    xall = jnp.concatenate([x, ctx], axis=0)
    g0, g1 = _vec(p["norm_g"][0]), _vec(p["norm_g"][1])
    (shift0, scale0, gate0), (shift1, scale1, gate1) = [tuple(_vec(v) for v in m) for m in mods]

    h0 = _norm_mod_fwd("l0_norm", xall, g0, scale0, shift0, n, tr, t)
    proj0 = _mm("l0_in", h0, w["ssm_w_in"], "nn")
    raw, dirs = _s5_system(p, t, c)
    (y_f, y_r), saved = _s5_forward(proj0, dirs, t, c)
    d_skip = _vec(p["ssm_d"][0])

    def post_a(u, yf, yr, dv):
        y = u * dv + yf + yr
        return y, _gelu(y)

    y0, yg = _rowwise("l0_gelu", post_a, n, tr, [(proj0, 0, D_MODEL), (y_f, 0, D_MODEL), (y_r, 0, D_MODEL)], [d_skip],
                      [(D_MODEL, F32), (D_MODEL, F32)], [])
    tg = _mm("l0_glu", yg, w["ssm_w_glu"], "nn")
    b_glu = _vec(p["ssm_b_glu"][0])

    def post_b(ygt, tt, zt, bv):
        return ygt * _sigmoid(tt + bv) * _silu(zt)

    gz0 = _rowwise("l0_gate", post_b, n, tr, [(yg, 0, D_MODEL), (tg, 0, D_MODEL), (proj0, 1, D_MODEL)], [b_glu],
                   [(D_MODEL, BF16)], [])[0]
    out0 = _mm("l0_out", gz0, w["ssm_w_out"], "nn")

    def res_norm(xt, ot, gv, g1v, sc, sh):
        x1t = xt + gv * ot
        xh, _ = _rms_hat(x1t)
        return x1t, (xh * g1v) * (1.0 + sc) + sh

    x1, h1 = _rowwise("l0_res_l1_norm", res_norm, n, tr, [(xall, 0, D_MODEL), (out0, 0, D_MODEL)],
                      [gate0, g1, scale1, shift1], [(D_MODEL, F32), (D_MODEL, BF16)], [], n_lat=t)
    proj1 = _mm("l1_in", h1, w["attn_w_in"], "nn")
    cos, sin = _rope_tables(t, c)
    qn, kn = p["attn_q_norm"][0], p["attn_k_norm"][0]
    q_h, k_h, v_h = _qk_prep(proj1, qn, kn, cos, sin, n, tr)
    tq = _largest_tile(t, 512, LANES)
    o, lse = _attn_fwd(q_h, k_h, v_h, t, tq, _largest_tile(n, 2816, LANES))
    gz1 = _rowwise("l1_gate", lambda ot, zt: ot * _silu(zt), t, tr, [(o, 0, D_MODEL), (proj1, 1, D_MODEL)], [],
                   [(D_MODEL, BF16)], [])[0]
    out1 = _mm("l1_out", gz1, w["attn_w_out"], "nn")

    gf = _vec(p["final_norm_g"])

    def head(x1t, o1t, tgt, g1v, gfv):
        x2 = x1t + g1v * o1t
        xh, r = _rms_hat(x2)
        e = xh * gfv - tgt
        dyf = e * (1.0 / D_MODEL)
        dx2 = _rms_bwd(xh, r, dyf * gfv)
        return dx2, g1v * dx2, dyf * xh, dx2 * o1t, jnp.sum(e * e, axis=1, keepdims=True)

    gate1_lat = gate1[0:1]
    dx2, dout1, d_gf, d_gate1, sq = _rowwise(
        "head", head, t, tr, [(x1, 0, D_MODEL), (out1, 0, D_MODEL), (target, 0, D_MODEL)], [gate1_lat, gf],
        [(D_MODEL, F32), (D_MODEL, BF16)], [(1, D_MODEL), (1, D_MODEL), (1, 1)])

    d_w_attn_out = _mm("l1_out_dw", gz1, dout1, "tn", out_dtype=BF16)
    dgz1 = _mm("l1_out_dx", dout1, w["attn_w_out"], "nt")

    def gate1_bwd(dgt, ot, zt):
        return dgt * _silu(zt), dgt * ot * _silu_grad(zt)

    do, dz1 = _rowwise("l1_gate_bwd", gate1_bwd, t, tr, [(dgz1, 0, D_MODEL), (o, 0, D_MODEL), (proj1, 1, D_MODEL)], [],
                       [(D_MODEL, BF16), (D_MODEL, F32)], [])
    dq_s, dk, dv = _attn_bwd(q_h, k_h, v_h, do, o, lse, t, tq, _largest_tile(n, 1024, LANES))
    dproj1, d_qn, d_kn = _qk_prep_bwd(proj1, qn, kn, cos, sin, dq_s, dz1, dk, dv, n, t, tr)
    d_w_attn_in = _mm("l1_in_dw", h1, dproj1, "tn", out_dtype=BF16)
    dh1 = _mm("l1_in_dx", dproj1, w["attn_w_in"], "nt")
    dx1, dout0, d_g1, d_scale1, d_shift1, d_gate0 = _norm_mod_bwd("l1_norm_bwd", x1, g1, scale1, dh1, dx2, n, tr, t,
                                                                  prev=(out0, gate0))

    d_w_out = _mm("l0_out_dw", gz0, dout0, "tn", out_dtype=BF16)
    dgz0 = _mm("l0_out_dx", dout0, w["ssm_w_out"], "nt")

    def post_b_bwd(dgt, ygt, tt, zt, bv):
        s = _sigmoid(tt + bv)
        dy2 = dgt * _silu(zt)
        dt = dy2 * ygt * s * (1.0 - s)
        return dgt * (ygt * s) * _silu_grad(zt), dt, dy2 * s, dt

    dz0, dtg, dyg_a, d_b_glu = _rowwise(
        "l0_gate_bwd", post_b_bwd, n, tr, [(dgz0, 0, D_MODEL), (yg, 0, D_MODEL), (tg, 0, D_MODEL), (proj0, 1, D_MODEL)],
        [b_glu], [(D_MODEL, BF16), (D_MODEL, BF16), (D_MODEL, F32)], [(1, D_MODEL)])
    d_w_glu = _mm("l0_glu_dw", yg, dtg, "tn", out_dtype=BF16)
    dyg_b = _mm("l0_glu_dx", dtg, w["ssm_w_glu"], "nt")

    def post_a_bwd(da, db, yt, ut, dv):
        dy = (da + db) * _gelu_grad(yt)
        return dy, dy * dv, dy * ut

    dy0, du_skip, d_d = _rowwise("l0_gelu_bwd", post_a_bwd, n, tr,
                                 [(dyg_a, 0, D_MODEL), (dyg_b, 0, D_MODEL), (y0, 0, D_MODEL), (proj0, 0, D_MODEL)], [d_skip],
                                 [(D_MODEL, F32), (D_MODEL, F32)], [(1, D_MODEL)])
    s5_bwd = _s5_backward(proj0, dy0, dirs, saved, t, c)
    dproj0 = _rowwise("l0_in_grad", lambda a, b, cc, dz: jnp.concatenate([a + b + cc, dz], axis=1), n, tr,
                      [(du_skip, 0, D_MODEL), (s5_bwd[0][0], 0, D_MODEL), (s5_bwd[1][0], 0, D_MODEL), (dz0, 0, D_MODEL)], [],
                      [(2 * D_MODEL, BF16)], [])[0]
    d_w_in = _mm("l0_in_dw", h0, dproj0, "tn", out_dtype=BF16)
    dh0 = _mm("l0_in_dx", dproj0, w["ssm_w_in"], "nt")
    dx0, d_g0, d_scale0, d_shift0 = _norm_mod_bwd("l0_norm_bwd", xall, g0, scale0, dh0, dx1, n, tr, t)

    big = dict(ssm_w_in=d_w_in, ssm_w_glu=d_w_glu, ssm_w_out=d_w_out, attn_w_in=d_w_attn_in, attn_w_out=d_w_attn_out)
    small = dict(
        norm_g=jnp.concatenate([d_g0[0], d_g1[0]], axis=0), ssm_d=d_d[0], ssm_b_glu=d_b_glu[0],
        attn_q_norm=d_qn, attn_k_norm=d_kn, final_norm_g=d_gf[0, 0], **_s5_param_grads(raw, s5_bwd))
    zero_v = jnp.zeros((D_MODEL,), F32)
    d_mod_lat = jnp.stack([jnp.concatenate([d_shift0[0, 0], d_scale0[0, 0], d_gate0[0, 0]]),
                           jnp.concatenate([d_shift1[0, 0], d_scale1[0, 0], d_gate1[0, 0]])])
    d_mod_ctx = jnp.stack([jnp.concatenate([d_shift0[1, 0], d_scale0[1, 0], d_gate0[1, 0]]),
                           jnp.concatenate([d_shift1[1, 0], d_scale1[1, 0], zero_v])])
    return sq[0, 0, 0], dx0[:t], big, small, d_mod_lat, d_mod_ctx


def _adamw(name, w, g, m, v):
    rows, cols = w.shape
    tr = _largest_tile(rows, 256, 8)
    c1 = 1.0 / (1.0 - ADAM_B1 ** ADAM_STEP)
    c2 = 1.0 / (1.0 - ADAM_B2 ** ADAM_STEP)

    def fn(wt, gt, mt, vt):
        mn = ADAM_B1 * mt + (1.0 - ADAM_B1) * gt
        vn = ADAM_B2 * vt + (1.0 - ADAM_B2) * (gt * gt)
        delta = -ADAM_LR * ((mn * c1) / (jnp.sqrt(vn * c2) + ADAM_EPS) + ADAM_WD * wt)
        return delta, mn, vn

    return _rowwise(name, fn, rows, tr, [(a, 0, cols) for a in (w, g, m, v)], [], [(cols, F32)] * 3, [])


BIG = ("ssm_w_in", "ssm_w_glu", "ssm_w_out", "attn_w_in", "attn_w_out")
COL_SHARDED = ("ssm_w_in", "attn_w_in")
WEIGHTS = ("c_ctx", "w_mod", "b_mod", "norm_g", "ssm_w_in", "ssm_a_re", "ssm_a_im", "ssm_log_dt", "ssm_b_re", "ssm_b_im",
           "ssm_c_re", "ssm_c_im", "ssm_d", "ssm_w_glu", "ssm_b_glu", "ssm_w_out", "attn_w_in", "attn_q_norm",
           "attn_k_norm", "attn_w_out", "final_norm_g")
SMALL = tuple(k for k in WEIGHTS if k not in BIG and k != "w_mod")
PACK_W = 1024


def _attn_in_perm(x, inverse):
    a, kv = ATTN_W, 2 * KV_W
    if inverse:
        return jnp.concatenate([x[..., :a], x[..., 2 * a:], x[..., a:2 * a]], axis=-1)
    return jnp.concatenate([x[..., :a], x[..., a + kv:], x[..., a:a + kv]], axis=-1)


def _pack(arrays, dtype, row_unit):
    flat = jnp.concatenate([a.reshape(-1).astype(dtype) for a in arrays])
    rows = -(-flat.shape[0] // PACK_W)
    rows = -(-rows // row_unit) * row_unit
    flat = jnp.concatenate([flat, jnp.zeros((rows * PACK_W - flat.shape[0],), dtype)])
    return flat.reshape(rows, PACK_W)


def _unpack(buf, shapes):
    lead = buf.shape[:-2]
    flat = buf.reshape(lead + (-1,))
    out, off = [], 0
    for shp in shapes:
        size = math.prod(shp)
        out.append(flat[..., off:off + size].reshape(lead + tuple(shp)))
        off += size
    return out


def _half_shape(name, shard_shape):
    r, ccols = shard_shape
    return (r // 2, ccols)


def kernel(x, c, ctx, c_ctx, w_mod, b_mod, norm_g, ssm_w_in, ssm_a_re, ssm_a_im, ssm_log_dt, ssm_b_re, ssm_b_im, ssm_c_re, ssm_c_im, ssm_d, ssm_w_glu, ssm_b_glu, ssm_w_out, attn_w_in, attn_q_norm, attn_k_norm, attn_w_out, final_norm_g, loss_target, m_c_ctx, m_w_mod, m_b_mod, m_norm_g, m_ssm_w_in, m_ssm_a_re, m_ssm_a_im, m_ssm_log_dt, m_ssm_b_re, m_ssm_b_im, m_ssm_c_re, m_ssm_c_im, m_ssm_d, m_ssm_w_glu, m_ssm_b_glu, m_ssm_w_out, m_attn_w_in, m_attn_q_norm, m_attn_k_norm, m_attn_w_out, m_final_norm_g, v_c_ctx, v_w_mod, v_b_mod, v_norm_g, v_ssm_w_in, v_ssm_a_re, v_ssm_a_im, v_ssm_log_dt, v_ssm_b_re, v_ssm_b_im, v_ssm_c_re, v_ssm_c_im, v_ssm_d, v_ssm_w_glu, v_ssm_b_glu, v_ssm_w_out, v_attn_w_in, v_attn_q_norm, v_attn_k_norm, v_attn_w_out, v_final_norm_g):
    args = dict(locals())
    wts = {k: args[k] for k in WEIGHTS}
    mom_m = {k: args["m_" + k] for k in WEIGHTS}
    mom_v = {k: args["v_" + k] for k in WEIGHTS}
    mx, my, mc = lax.axis_index("x"), lax.axis_index("y"), lax.axis_index("c")
    chip = 2 * mx + my
    me = 2 * chip + mc

    halves = []
    for k in BIG:
        sh = wts[k][0]
        hr = sh.shape[0] // 2
        halves.append(lax.dynamic_slice_in_dim(sh, mc * hr, hr, axis=0))
    gathered = _gather_two_level("gather_weights", _pack(halves, BF16, 16))
    parts = _unpack(gathered, [h.shape for h in halves])
    w_full = {}
    for k, pc in zip(BIG, parts):
        hr, cols = pc.shape[1:]
        pc = pc.reshape(N_CHIP, 2, hr, cols)
        if k in COL_SHARDED:
            w_full[k] = pc.transpose(1, 2, 0, 3).reshape(2 * hr, N_CHIP * cols)
        else:
            w_full[k] = pc.reshape(N_CHIP * 2 * hr, cols)
    w_full["attn_w_in"] = _attn_in_perm(w_full["attn_w_in"], False)

    c_blk = jnp.concatenate([c, jnp.zeros((7, D_MODEL), F32)], axis=0)
    c_all = _exchange("gather_c", c_blk, True)[:, 0]
    cond = jnp.concatenate([c_all, c_ctx[None], jnp.zeros((7, D_MODEL), F32)], axis=0)
    s_cond, ds_cond = _rowwise("cond_silu", lambda t: (_silu(t), _silu_grad(t)), 16, 16, [(cond, 0, D_MODEL)], [],
                               [(D_MODEL, F32), (D_MODEL, F32)], [])
    w_mod_b = w_mod.astype(BF16)
    mcols = w_mod.shape[2]
    mod_part = jnp.stack([_mm(f"mod{i}", s_cond, w_mod_b[i], "nn") for i in range(2)])
    mod_g = _exchange("gather_mod", mod_part.reshape(32, mcols), True)
    mod_all = mod_g.reshape(N_CHIP, 2, 2, 16, mcols)[:, 0]
    mod_all = mod_all.transpose(1, 2, 0, 3).reshape(2, 16, N_CHIP * mcols) + b_mod[:, None, :]
    mods = []
    for i in range(2):
        lat = lax.dynamic_slice_in_dim(mod_all[i], me, 1, axis=0)[0]
        both = jnp.stack([lat, mod_all[i, 8]])
        mods.append((both[:, :D_MODEL], both[:, D_MODEL:2 * D_MODEL], both[:, 2 * D_MODEL:]))

    small_p = {k: wts[k] for k in SMALL if k != "c_ctx" and k != "b_mod"}
    sq, grad_x, big_g, small_g, d_mod_lat, d_mod_ctx = _example_step(x[0], ctx[0], loss_target[0], mods, w_full, small_p)
    loss = lax.psum(0.5 / D_MODEL * sq, ("x", "y", "c"))
    big_g["attn_w_in"] = _attn_in_perm(big_g["attn_w_in"], True)

    small_names = [k for k in SMALL if k not in ("c_ctx", "b_mod")]
    small_list = [small_g[k] for k in small_names] + [d_mod_lat, d_mod_ctx]
    small_shapes = [wts[k].shape for k in small_names] + [d_mod_lat.shape, d_mod_ctx.shape]
    packed = _pack(small_list, F32, 8 * N_DEV)
    slice_rows = packed.shape[0] // N_DEV
    slices = _exchange("scatter_small", packed.reshape(N_DEV, slice_rows, PACK_W), False)
    my_sum = _sum_slots("sum_small", slices)
    payload = jnp.concatenate([my_sum, _pack([d_mod_lat], F32, 8)], axis=0)
    sg = _exchange("gather_small", payload, True)
    summed = _unpack(sg[:, :slice_rows].reshape(packed.shape), small_shapes)
    grads = dict(zip(small_names, summed[:-2]))
    d_mod_lat_sum, d_mod_ctx_sum = summed[-2], summed[-1]
    grads["b_mod"] = d_mod_lat_sum + d_mod_ctx_sum
    d_mod_lat_all = _unpack(sg[:, slice_rows:], [d_mod_lat.shape])[0]

    g_w_mod, ds_cc = [], []
    for i in range(2):
        rows9 = jnp.concatenate([d_mod_lat_all[:, i], d_mod_ctx_sum[i][None], jnp.zeros((7, 3 * D_MODEL), F32)], axis=0)
        mine = lax.dynamic_slice_in_dim(rows9, chip * mcols, mcols, axis=1)
        g_w_mod.append(_mm(f"mod{i}_dw", s_cond, mine, "tn"))
        ds_cc.append(_mm(f"mod{i}_dx", mine, w_mod_b[i], "nt")[8])
    grads["w_mod"] = jnp.stack(g_w_mod)
    part = (ds_cc[0] + ds_cc[1]) * jnp.where(mc == 0, 1.0, 0.0)
    part_blk = jnp.concatenate([part[None], jnp.zeros((7, D_MODEL), F32)], axis=0)
    ds_all = _sum_slots("sum_c_ctx", _exchange("gather_c_ctx", part_blk, True))
    grads["c_ctx"] = ds_all[0] * ds_cond[8]

    blocks = []
    for k in BIG:
        g = big_g[k]
        rows, cols = g.shape
        if k in COL_SHARDED:
            blocks.append(g.reshape(2, rows // 2, N_CHIP, cols // N_CHIP).transpose(2, 0, 1, 3).reshape(N_DEV, -1))
        else:
            blocks.append(g.reshape(N_DEV, -1))
    sendbuf = jnp.concatenate(blocks, axis=1).astype(BF16)
    sendbuf = sendbuf.reshape(N_DEV, -1, PACK_W)
    recv = _exchange("scatter_big", sendbuf, False)
    mine = _sum_slots("sum_big", recv)
    both = _exchange("swap_halves", mine, True, sibling_only=True, chunks=8)
    half_shapes = [(wts[k].shape[1] // 2, wts[k].shape[2]) for k in BIG]
    for k, pc in zip(BIG, _unpack(both, half_shapes)):
        grads[k] = pc.reshape(wts[k].shape)

    delta, new_m, new_v = {}, {}, {}
    for k in BIG + ("w_mod",):
        shp = wts[k].shape
        two_d = (-1, shp[-1])
        res = _adamw("adamw_" + k, *[a.reshape(two_d) for a in (wts[k], grads[k], mom_m[k], mom_v[k])])
        delta[k], new_m[k], new_v[k] = [r.reshape(shp) for r in res]
    shapes = [wts[k].shape for k in SMALL]
    packed = [_pack([d[k] for k in SMALL], F32, 8) for d in (wts, grads, mom_m, mom_v)]
    res = _adamw("adamw_small", *packed)
    for dst, buf in zip((delta, new_m, new_v), res):
        for k, a in zip(SMALL, _unpack(buf, shapes)):
            dst[k] = a
    grads = {k: grads[k].reshape(wts[k].shape) for k in WEIGHTS}
    return (loss, grad_x[None], *[grads[k] for k in WEIGHTS], *[delta[k] for k in WEIGHTS],
            *[new_m[k] for k in WEIGHTS], *[new_v[k] for k in WEIGHTS])
```

```python
import functools
import math

import jax
import jax.numpy as jnp
from jax import lax
from jax.experimental import pallas as pl
from jax.experimental.pallas import tpu as pltpu

F32 = jnp.float32
BF16 = jnp.bfloat16

D_MODEL = 1024
NORM_EPS = 1e-6
SSM_GROUPS = 64
SSM_GROUP = 16
SSM_STATE = 64
LANES = 128
SLAB_W = LANES
N_SLAB = D_MODEL // SLAB_W
SLAB_GROUPS = SLAB_W // SSM_GROUP
HALF_W = SLAB_GROUPS * SSM_STATE
STATE_W = 2 * HALF_W
N_SEG = 8
HEAD_DIM = 64
N_Q_HEADS = 16
N_KV_HEADS = 4
KV_REP = N_Q_HEADS // N_KV_HEADS
ATTN_W = N_Q_HEADS * HEAD_DIM
KV_W = N_KV_HEADS * HEAD_DIM
GRID_W = 64
ROPE_THETA = 10000.0
N_DEV = 8
N_CHIP = 4
VMEM_LIMIT_BYTES = 56 * 1024 * 1024

ADAM_LR = 0.001
ADAM_B1 = 0.9
ADAM_B2 = 0.999
ADAM_EPS = 1e-08
ADAM_WD = 0.01
ADAM_STEP = 10


def _params(*sem):
    return pltpu.CompilerParams(dimension_semantics=sem, vmem_limit_bytes=VMEM_LIMIT_BYTES)


def _largest_tile(n, cap, unit):
    if n <= cap:
        return n
    t = (cap // unit) * unit
    while t >= unit:
        if n % t == 0:
            return t
        t -= unit
    raise ValueError(f"no tile for {n} (cap {cap}, unit {unit})")


def _rowwise(name, fn, n_rows, tr, row_ins, vec_ins, row_outs, red_outs, n_lat=None, want_flag=False):
    nt = n_rows // tr
    assert nt * tr == n_rows
    nlt = nt if n_lat is None else n_lat // tr

    def sel(i):
        return jnp.where(i >= nlt, 1, 0)

    arrays, in_specs = [], []
    for spec in row_ins:
        arr, cb, w = spec[:3]
        kind = spec[3] if len(spec) > 3 else None
        m = spec[4] if len(spec) > 4 else None
        if kind == "mod":
            imap = functools.partial(lambda i, cb, m: (i % m, cb), cb=cb, m=m)
        elif kind == "clamp":
            imap = functools.partial(lambda i, cb, m: (jnp.minimum(i, m - 1), cb), cb=cb, m=m)
        else:
            imap = functools.partial(lambda i, cb: (i, cb), cb=cb)
        arrays.append(arr)
        in_specs.append(pl.BlockSpec((tr, w), imap))
    for v in vec_ins:
        s, a, w = v.shape
        imap = (lambda i: (sel(i), 0, 0)) if s == 2 else (lambda i: (0, 0, 0))
        arrays.append(v)
        in_specs.append(pl.BlockSpec((1, a, w), imap))
    out_shapes, out_specs = [], []
    for w, dt in row_outs:
        out_shapes.append(jax.ShapeDtypeStruct((n_rows, w), dt))
        out_specs.append(pl.BlockSpec((tr, w), lambda i: (i, 0)))
    for s, w in red_outs:
        out_shapes.append(jax.ShapeDtypeStruct((s, 1, w), F32))
        imap = (lambda i: (sel(i), 0, 0)) if s == 2 else (lambda i: (0, 0, 0))
        out_specs.append(pl.BlockSpec((1, 1, w), imap))
    n_ri, n_vi, n_ro, n_rd = len(row_ins), len(vec_ins), len(row_outs), len(red_outs)

    def body(*refs):
        i = pl.program_id(0)
        rows = [r[...] for r in refs[:n_ri]]
        vecs = [r[0] for r in refs[n_ri:n_ri + n_vi]]
        outs = refs[n_ri + n_vi:]
        lead = [jnp.where(i < nlt, 1.0, 0.0).astype(F32)] if want_flag else []
        res = fn(*lead, *rows, *vecs)
        if not isinstance(res, (tuple, list)):
            res = (res,)
        assert len(res) == n_ro + n_rd
        for k in range(n_ro):
            outs[k][...] = res[k].astype(outs[k].dtype)
        for k in range(n_rd):
            part = jnp.sum(res[n_ro + k].astype(F32), axis=0, keepdims=True)
            first = i == 0
            if red_outs[k][0] == 2:
                first = jnp.logical_or(first, i == nlt)
            o = outs[n_ro + k]

            @pl.when(first)
            def _():
                o[0] = part

            @pl.when(jnp.logical_not(first))
            def _():
                o[0] = o[0] + part

    res = pl.pallas_call(
        body, grid=(nt,), in_specs=in_specs, out_specs=out_specs, out_shape=out_shapes,
        compiler_params=_params("arbitrary"), name=name)(*arrays)
    return res


def _vec(v):
    v = v.astype(F32)
    if v.ndim == 1:
        v = v[None]
    return v[:, None, :]


def _mm(name, a, b, mode, out_dtype=F32):
    if mode in ("nn", "nt"):
        m, k = a.shape
        n = b.shape[1] if mode == "nn" else b.shape[0]
        tm = _largest_tile(m, 1024, 8)
        tn = _largest_tile(n, 1024, 128)
        contract = (((1,), (0,)), ((), ())) if mode == "nn" else (((1,), (1,)), ((), ()))

        def body(a_ref, b_ref, o_ref):
            o_ref[...] = lax.dot_general(a_ref[...].astype(BF16), b_ref[...].astype(BF16), contract,
                                         preferred_element_type=F32).astype(o_ref.dtype)

        b_spec = pl.BlockSpec((k, tn), lambda i, j: (0, j)) if mode == "nn" else pl.BlockSpec((tn, k), lambda i, j: (j, 0))
        return pl.pallas_call(
            body, grid=(m // tm, n // tn),
            in_specs=[pl.BlockSpec((tm, k), lambda i, j: (i, 0)), b_spec],
            out_specs=pl.BlockSpec((tm, tn), lambda i, j: (i, j)),
            out_shape=jax.ShapeDtypeStruct((m, n), out_dtype),
            compiler_params=_params("parallel", "arbitrary"), name=name)(a, b)
    assert mode == "tn"
    r, k1 = a.shape
    k2 = b.shape[1]
    tr = _largest_tile(r, 1024, 8)
    t2 = _largest_tile(k2, 1024, 128)
    nr = r // tr

    def body(a_ref, b_ref, o_ref, acc_ref):
        part = lax.dot_general(a_ref[...].astype(BF16), b_ref[...].astype(BF16), (((0,), (0,)), ((), ())),
                               preferred_element_type=F32)
        i = pl.program_id(1)

        @pl.when(i == 0)
        def _():
            acc_ref[...] = part

        @pl.when(i > 0)
        def _():
            acc_ref[...] += part

        @pl.when(i == nr - 1)
        def _():
            o_ref[...] = acc_ref[...].astype(o_ref.dtype)

    return pl.pallas_call(
        body, grid=(k2 // t2, nr),
        in_specs=[pl.BlockSpec((tr, k1), lambda j, i: (i, 0)), pl.BlockSpec((tr, t2), lambda j, i: (i, j))],
        out_specs=pl.BlockSpec((k1, t2), lambda j, i: (0, j)),
        out_shape=jax.ShapeDtypeStruct((k1, k2), out_dtype),
        scratch_shapes=[pltpu.VMEM((k1, t2), F32)],
        compiler_params=_params("parallel", "arbitrary"), name=name)(a, b)


def _exchange(name, x, bcast, sibling_only=False, chunks=1):
    rels = [1] if sibling_only else list(range(1, N_DEV))
    n_slot = 2 if sibling_only else N_DEV
    blk = x.shape if bcast else x.shape[1:]

    def body(x_ref, o_ref, send_sems, recv_sems, local_sem):
        mx, my, mc = lax.axis_index("x"), lax.axis_index("y"), lax.axis_index("c")
        me = mc if sibling_only else 4 * mx + 2 * my + mc
        me_dev = 4 * mx + 2 * my + mc
        mine = pltpu.make_async_copy(x_ref if bcast else x_ref.at[me_dev], o_ref.at[me], local_sem)
        mine.start()
        copies = []
        rc = blk[0] // chunks
        for k, r in enumerate(rels):
            px = 1 - mx if (r >> 2) & 1 else mx
            py = 1 - my if (r >> 1) & 1 else my
            pc = 1 - mc if r & 1 else mc
            src = x_ref if bcast else x_ref.at[4 * px + 2 * py + pc]
            for ci in range(chunks):
                rows = pl.ds(ci * rc, rc)
                cp = pltpu.make_async_remote_copy(
                    src_ref=src.at[rows], dst_ref=o_ref.at[me, rows], send_sem=send_sems.at[k * chunks + ci],
                    recv_sem=recv_sems.at[k * chunks + ci], device_id=(px, py, pc), device_id_type=pl.DeviceIdType.MESH)
                cp.start()
                copies.append(cp)
        for cp in copies:
            cp.wait()
        mine.wait()

    return pl.pallas_call(
        body, out_shape=jax.ShapeDtypeStruct((n_slot,) + tuple(blk), x.dtype),
        in_specs=[pl.BlockSpec(memory_space=pl.ANY)], out_specs=pl.BlockSpec(memory_space=pl.ANY),
        scratch_shapes=[pltpu.SemaphoreType.DMA((len(rels) * chunks,)), pltpu.SemaphoreType.DMA((len(rels) * chunks,)),
                        pltpu.SemaphoreType.DMA],
        name=name)(x)


def _gather_two_level(name, x):
    def body(x_ref, o_ref, send_sems, recv_sems, local_sem):
        mx, my, mc = lax.axis_index("x"), lax.axis_index("y"), lax.axis_index("c")
        me, sibling = (mx, my, mc), (mx, my, 1 - mc)
        chips = [(1 - mx, my), (mx, 1 - my), (1 - mx, 1 - my)]

        def slot(px, py, pc):
            return o_ref.at[4 * px + 2 * py + pc]

        def copy(k, block, to, src=None):
            return pltpu.make_async_remote_copy(
                src_ref=slot(*block) if src is None else src, dst_ref=slot(*block), send_sem=send_sems.at[k],
                recv_sem=recv_sems.at[k], device_id=to, device_id_type=pl.DeviceIdType.MESH)

        mine = pltpu.make_async_copy(x_ref, slot(*me), local_sem)
        mine.start()
        first = [copy(0, me, sibling, src=x_ref)]
        first += [copy(1 + j, me, (*chip, mc), src=x_ref) for j, chip in enumerate(chips)]
        for cp in first:
            cp.start()
        passed = [copy(4 + j, (*chip, mc), sibling) for j, chip in enumerate(chips)]
        for j, chip in enumerate(chips):
            copy(1 + j, (*chip, mc), me).wait_recv()
            passed[j].start()
        copy(0, sibling, me).wait_recv()
        for j, chip in enumerate(chips):
            copy(4 + j, (*chip, 1 - mc), me).wait_recv()
        for cp in first + passed:
            cp.wait_send()
        mine.wait()

    return pl.pallas_call(
        body, out_shape=jax.ShapeDtypeStruct((N_DEV,) + tuple(x.shape), x.dtype),
        in_specs=[pl.BlockSpec(memory_space=pl.ANY)], out_specs=pl.BlockSpec(memory_space=pl.ANY),
        scratch_shapes=[pltpu.SemaphoreType.DMA((N_DEV - 1,)), pltpu.SemaphoreType.DMA((N_DEV - 1,)),
                        pltpu.SemaphoreType.DMA],
        name=name)(x)


def _sum_slots(name, x):
    s, r, w = x.shape
    tr = _largest_tile(r, 256, 8)

    def body(x_ref, o_ref):
        acc = x_ref[0].astype(F32)
        for j in range(1, s):
            acc = acc + x_ref[j].astype(F32)
        o_ref[...] = acc

    return pl.pallas_call(
        body, grid=(r // tr,), in_specs=[pl.BlockSpec((s, tr, w), lambda i: (0, i, 0))],
        out_specs=pl.BlockSpec((tr, w), lambda i: (i, 0)), out_shape=jax.ShapeDtypeStruct((r, w), F32),
        compiler_params=_params("parallel"), name=name)(x)


def _sigmoid(x):
    return 1.0 / (1.0 + jnp.exp(-x))


def _silu(x):
    return x * _sigmoid(x)


def _silu_grad(x):
    s = _sigmoid(x)
    return s * (1.0 + x * (1.0 - s))


_INV_SQRT2 = 1.0 / math.sqrt(2.0)
_INV_SQRT2PI = 1.0 / math.sqrt(2.0 * math.pi)


def _gelu(x):
    return 0.5 * x * (1.0 + lax.erf(x * _INV_SQRT2))


def _gelu_grad(x):
    return 0.5 * (1.0 + lax.erf(x * _INV_SQRT2)) + x * jnp.exp(-0.5 * x * x) * _INV_SQRT2PI


def _rms_hat(x):
    r = lax.rsqrt(jnp.mean(x * x, axis=-1, keepdims=True) + NORM_EPS)
    return x * r, r


def _rms_bwd(xh, r, dxh):
    return r * (dxh - xh * jnp.mean(dxh * xh, axis=-1, keepdims=True))


def _norm_mod_fwd(name, x, g, scale, shift, n_rows, tr, n_lat):
    def fn(xt, gv, sc, sh):
        xh, _ = _rms_hat(xt)
        return (xh * gv) * (1.0 + sc) + sh

    return _rowwise(name, fn, n_rows, tr, [(x, 0, D_MODEL)], [g, scale, shift], [(D_MODEL, BF16)], [], n_lat=n_lat)[0]


def _norm_mod_bwd(name, x, g, scale, dh, dres, n_rows, tr, n_lat, prev=None):
    nlt = n_lat // tr

    def fn(flag, xt, dht, drt, *rest):
        gv, sc = rest[-2:] if prev is None else rest[1:3]
        xh, r = _rms_hat(xt)
        n = xh * gv
        dn = dht * (1.0 + sc)
        dx = _rms_bwd(xh, r, dn * gv) + flag * drt
        if prev is None:
            return dx, dn * xh, dht * n, dht
        return dx, rest[3] * dx, dn * xh, dht * n, dht, dx * rest[0]

    rows = [(x, 0, D_MODEL), (dh, 0, D_MODEL), (dres, 0, D_MODEL, "clamp", nlt)]
    vecs, row_outs, reds = [g, scale], [(D_MODEL, F32)], [(1, D_MODEL), (2, D_MODEL), (2, D_MODEL)]
    if prev is not None:
        rows.append((prev[0], 0, D_MODEL))
        vecs.append(prev[1])
        row_outs.append((D_MODEL, BF16))
        reds.append((2, D_MODEL))
    return _rowwise(name, fn, n_rows, tr, rows, vecs, row_outs, reds, n_lat=n_lat, want_flag=True)


def _s5_prep(a_re, a_im, log_dt, b_re, b_im, seg_lat, seg_ctx):
    def body(ar_ref, ai_ref, ld_ref, br_ref, bi_ref, abr_ref, abi_ref, bbr_ref, bbi_ref, alr_ref, ali_ref, acr_ref,
             aci_ref):
        lr, li = ar_ref[...], ai_ref[...]
        dt = jnp.exp(ld_ref[...])
        ldr, ldi = lr * dt, li * dt
        e = jnp.exp(ldr)
        abr, abi = e * jnp.cos(ldi), e * jnp.sin(ldi)
        abr_ref[...] = abr
        abi_ref[...] = abi
        den = lr * lr + li * li
        nr, ni = abr - 1.0, abi
        qr = (nr * lr + ni * li) / den
        qi = (ni * lr - nr * li) / den
        br, bi = br_ref[...], bi_ref[...]
        bbr_ref[...] = qr[None] * br - qi[None] * bi
        bbi_ref[...] = qr[None] * bi + qi[None] * br
        for seg, r_ref, i_ref in ((seg_lat, alr_ref, ali_ref), (seg_ctx, acr_ref, aci_ref)):
            es = jnp.exp(ldr * float(seg))
            r_ref[...] = es * jnp.cos(ldi * float(seg))
            i_ref[...] = es * jnp.sin(ldi * float(seg))

    sm = jax.ShapeDtypeStruct(a_re.shape, F32)
    big = jax.ShapeDtypeStruct(b_re.shape, F32)
    return pl.pallas_call(body, out_shape=[sm, sm, big, big, sm, sm, sm, sm], name="s5_prep")(
        a_re, a_im, log_dt, b_re, b_im)


def _s5_prep_bwd(a_re, a_im, log_dt, b_re, b_im, dabr, dabi, dbbr, dbbi):
    def body(ar_ref, ai_ref, ld_ref, br_ref, bi_ref, dabr_ref, dabi_ref, dbbr_ref, dbbi_ref,
             dar_ref, dai_ref, dld_ref, dbr_ref, dbi_ref):
        lr, li = ar_ref[...], ai_ref[...]
        dt = jnp.exp(ld_ref[...])
        ldr, ldi = lr * dt, li * dt
        e = jnp.exp(ldr)
        abr, abi = e * jnp.cos(ldi), e * jnp.sin(ldi)
        den = lr * lr + li * li
        nr, ni = abr - 1.0, abi
        qr = (nr * lr + ni * li) / den
        qi = (ni * lr - nr * li) / den
        br, bi = br_ref[...], bi_ref[...]
        gbr, gbi = dbbr_ref[...], dbbi_ref[...]
        dbr_ref[...] = gbr * qr[None] + gbi * qi[None]
        dbi_ref[...] = gbi * qr[None] - gbr * qi[None]
        dqr = jnp.sum(gbr * br + gbi * bi, axis=0)
        dqi = jnp.sum(gbi * br - gbr * bi, axis=0)
        dnr = (dqr * lr - dqi * li) / den
        dni = (dqr * li + dqi * lr) / den
        dlr_q = (dqr * (nr - 2.0 * lr * qr) + dqi * (ni - 2.0 * lr * qi)) / den
        dli_q = (dqr * (ni - 2.0 * li * qr) + dqi * (-nr - 2.0 * li * qi)) / den
        gar = dabr_ref[...] + dnr
        gai = dabi_ref[...] + dni
        dldr = gar * abr + gai * abi
        dldi = gai * abr - gar * abi
        dar_ref[...] = dldr * dt + dlr_q
        dai_ref[...] = dldi * dt + dli_q
        ddt = jnp.sum(dldr * lr + dldi * li, axis=1, keepdims=True)
        dld_ref[...] = ddt * dt

    sm = jax.ShapeDtypeStruct(a_re.shape, F32)
    big = jax.ShapeDtypeStruct(b_re.shape, F32)
    return pl.pallas_call(body, out_shape=[sm, sm, jax.ShapeDtypeStruct(log_dt.shape, F32), big, big],
                          name="s5_prep_bwd")(a_re, a_im, log_dt, b_re, b_im, dabr, dabi, dbbr, dbbi)


def _slab_cols(v):
    return v.reshape(N_SLAB, 1, HALF_W)


def _slab_pair(vr, vi):
    return jnp.concatenate([_slab_cols(vr), _slab_cols(vi)], axis=-1)


def _slab_in_matrix(bbr, bbi):
    eye = jnp.eye(SLAB_GROUPS, dtype=F32)

    def one(b):
        b = b.reshape(N_SLAB, SLAB_GROUPS, SSM_STATE, SSM_GROUP)
        m = jnp.einsum("sgph,gk->sghkp", b, eye)
        return m.reshape(N_SLAB, SLAB_W, HALF_W)

    return jnp.concatenate([one(bbr), one(bbi)], axis=-1)


def _slab_out_matrix(cr, ci):
    eye = jnp.eye(SLAB_GROUPS, dtype=F32)

    def one(c):
        c = c.reshape(N_SLAB, SLAB_GROUPS, SSM_GROUP, SSM_STATE)
        m = jnp.einsum("sghp,gk->skpgh", c, eye)
        return m.reshape(N_SLAB, HALF_W, SLAB_W)

    return jnp.concatenate([one(cr), one(-ci)], axis=1)


def _slab_diag(m):
    m = m.reshape(N_SLAB, SLAB_GROUPS, SSM_GROUP, 2, SLAB_GROUPS, SSM_STATE)
    d = jnp.stack([m[:, g, :, :, g, :] for g in range(SLAB_GROUPS)], axis=1)
    return d.transpose(3, 0, 1, 2, 4).reshape(2, SSM_GROUPS, SSM_GROUP, SSM_STATE)


def _cmul(ar, ai, xr, xi, conj):
    if conj:
        return ar * xr + ai * xi, ar * xi - ai * xr
    return ar * xr - ai * xi, ar * xi + ai * xr


def _s5_pow_table(name, abar, seg, falling, conj):
    assert seg >= 8 and seg & (seg - 1) == 0

    def body(a_ref, o_ref, t_ref):
        ar, ai = a_ref[0, :, :HALF_W], a_ref[0, :, HALF_W:]
        if conj:
            ai = -ai
        rr, ri = [jnp.ones_like(ar)], [jnp.zeros_like(ai)]
        for _ in range(7):
            pr, pi = _cmul(ar, ai, rr[-1], ri[-1], False)
            rr.append(pr)
            ri.append(pi)
        sr, si = _cmul(ar, ai, rr[-1], ri[-1], False)
        if falling:
            rr, ri = rr[::-1], ri[::-1]
        first = slice(seg - 8, seg) if falling else slice(0, 8)
        t_ref[first, :HALF_W] = jnp.concatenate(rr, axis=0)
        t_ref[first, HALF_W:] = jnp.concatenate(ri, axis=0)
        size = 8
        while size < seg:
            src = slice(seg - size, seg) if falling else slice(0, size)
            dst = slice(seg - 2 * size, seg - size) if falling else slice(size, 2 * size)
            pr, pi = _cmul(sr, si, t_ref[src, :HALF_W], t_ref[src, HALF_W:], False)
            t_ref[dst, :HALF_W] = pr
            t_ref[dst, HALF_W:] = pi
            sr, si = _cmul(sr, si, sr, si, False)
            size *= 2
        o_ref[0] = t_ref[...].astype(BF16)

    return pl.pallas_call(
        body, grid=(N_SLAB,), in_specs=[pl.BlockSpec((1, 1, STATE_W), lambda s: (s, 0, 0))],
        out_specs=pl.BlockSpec((1, seg, STATE_W), lambda s: (s, 0, 0)),
        out_shape=jax.ShapeDtypeStruct((N_SLAB, seg, STATE_W), BF16),
        scratch_shapes=[pltpu.VMEM((seg, STATE_W), F32)], compiler_params=_params("parallel"), name=name)(abar)


def _s5_ends(name, x, n_rows, row0, table, m_mat):
    seg = n_rows // N_SEG
    rb = row0 // n_rows
    tn = (((0,), (0,)), ((), ()))

    def body(x_ref, t_ref, m_ref, z_ref):
        mr, mi = m_ref[0, :, :HALF_W], m_ref[0, :, HALF_W:]
        for j in range(N_SEG):
            t = lax.dot_general(x_ref[j * seg:(j + 1) * seg, :].astype(BF16), t_ref[0], tn,
                                preferred_element_type=F32)
            tr_, ti_ = t[:, :HALF_W], t[:, HALF_W:]
            z_ref[0, j:j + 1, :HALF_W] = jnp.sum(mr * tr_ - mi * ti_, axis=0, keepdims=True)
            z_ref[0, j:j + 1, HALF_W:] = jnp.sum(mr * ti_ + mi * tr_, axis=0, keepdims=True)

    return pl.pallas_call(
        body, grid=(N_SLAB,),
        in_specs=[pl.BlockSpec((n_rows, SLAB_W), lambda s: (rb, s)),
                  pl.BlockSpec((1, seg, STATE_W), lambda s: (s, 0, 0)),
                  pl.BlockSpec((1, SLAB_W, STATE_W), lambda s: (s, 0, 0))],
        out_specs=pl.BlockSpec((1, N_SEG, STATE_W), lambda s: (s, 0, 0)),
        out_shape=jax.ShapeDtypeStruct((N_SLAB, N_SEG, STATE_W), F32),
        compiler_params=_params("parallel"), name=name)(x, table, m_mat)


def _s5_carry(name, z, a_seg, init, descending, conj):
    order = list(range(N_SEG - 1, -1, -1)) if descending else list(range(N_SEG))

    def body(z_ref, a_ref, i_ref, e_ref, o_ref):
        ar, ai = a_ref[:, :HALF_W], a_ref[:, HALF_W:]
        cr, ci = i_ref[:, :HALF_W], i_ref[:, HALF_W:]
        for j in order:
            e_ref[j, :, :HALF_W] = cr
            e_ref[j, :, HALF_W:] = ci
            pr, pi = _cmul(ar, ai, cr, ci, conj)
            cr = pr + z_ref[j, :, :HALF_W]
            ci = pi + z_ref[j, :, HALF_W:]
        o_ref[:, :HALF_W] = cr
        o_ref[:, HALF_W:] = ci

    return pl.pallas_call(body, out_shape=[jax.ShapeDtypeStruct(z.shape, F32), jax.ShapeDtypeStruct(init.shape, F32)],
                          name=name)(z, a_seg, init)


def _seg_major(v):
    return jnp.transpose(v, (1, 0, 2))


def _s5_scan(name, u, n_rows, row0, b_mat, c_mat, abar, h_in, descending, y_alias=None, y_rows=None):
    seg = n_rows // N_SEG
    ta = min(32, seg)
    nk = seg // ta
    assert seg * N_SEG == n_rows and nk * ta == seg and row0 % n_rows == 0 and ta % 8 == 0
    rb = row0 // n_rows
    tile = ta * N_SEG

    def body(*refs):
        u_ref, b_ref, c_ref, a_ref, hin_ref = refs[:5]
        y_ref, hch_ref, st_ref, up_ref, h_ref = refs[-5:]
        k = pl.program_id(1)
        kk = nk - 1 - k if descending else k
        a0 = kk * ta

        @pl.when(k == 0)
        def _():
            st_ref[...] = hin_ref[0]

        hch_ref[0, 0] = st_ref[...]
        for al in range(ta):
            up_ref[al * N_SEG:(al + 1) * N_SEG, :] = u_ref[pl.ds(a0 + al, N_SEG, stride=seg), :]
        h_ref[...] = jnp.dot(up_ref[...].astype(BF16), b_ref[0], preferred_element_type=F32)
        ar = jnp.broadcast_to(a_ref[0, :, :HALF_W], (N_SEG, HALF_W))
        ai = jnp.broadcast_to(a_ref[0, :, HALF_W:], (N_SEG, HALF_W))

        def step(i, carry):
            hr, hi = carry
            al = ta - 1 - i if descending else i
            row = pl.multiple_of(al * N_SEG, N_SEG)
            pr, pi = _cmul(ar, ai, hr, hi, False)
            hr = pr + h_ref[pl.ds(row, N_SEG), :HALF_W]
            hi = pi + h_ref[pl.ds(row, N_SEG), HALF_W:]
            h_ref[pl.ds(row, N_SEG), :HALF_W] = hr
            h_ref[pl.ds(row, N_SEG), HALF_W:] = hi
            return hr, hi

        hr, hi = lax.fori_loop(0, ta, step, (st_ref[:, :HALF_W], st_ref[:, HALF_W:]), unroll=True)
        st_ref[:, :HALF_W] = hr
        st_ref[:, HALF_W:] = hi
        yt = jnp.dot(h_ref[...].astype(BF16), c_ref[0], preferred_element_type=F32)
        for al in range(ta):
            y_ref[pl.ds(a0 + al, N_SEG, stride=seg), :] = yt[al * N_SEG:(al + 1) * N_SEG, :]

    u_spec = pl.BlockSpec((n_rows, SLAB_W), lambda s, k: (rb, s))
    b_spec = pl.BlockSpec((1, SLAB_W, STATE_W), lambda s, k: (s, 0, 0))
    c_spec = pl.BlockSpec((1, STATE_W, SLAB_W), lambda s, k: (s, 0, 0))
    a_spec = pl.BlockSpec((1, 1, STATE_W), lambda s, k: (s, 0, 0))
    st_spec = pl.BlockSpec((1, N_SEG, STATE_W), lambda s, k: (s, 0, 0))
    scratch = [pltpu.VMEM((N_SEG, STATE_W), F32), pltpu.VMEM((tile, SLAB_W), F32), pltpu.VMEM((tile, STATE_W), F32)]
    kmap = (lambda s, k: (s, nk - 1 - k, 0, 0)) if descending else (lambda s, k: (s, k, 0, 0))
    out_specs = [u_spec, pl.BlockSpec((1, 1, N_SEG, STATE_W), kmap)]
    out_shape = [jax.ShapeDtypeStruct((y_rows, D_MODEL), F32), jax.ShapeDtypeStruct((N_SLAB, nk, N_SEG, STATE_W), F32)]
    in_specs = [u_spec, b_spec, c_spec, a_spec, st_spec]
    args = [u, b_mat, c_mat, abar, h_in]
    aliases = {}
    if y_alias is not None:
        in_specs.append(pl.BlockSpec(memory_space=pl.ANY))
        args.append(y_alias)
        aliases = {5: 0}
    return pl.pallas_call(
        body, grid=(N_SLAB, nk), in_specs=in_specs, out_specs=out_specs, out_shape=out_shape, scratch_shapes=scratch,
        input_output_aliases=aliases, compiler_params=_params("parallel", "arbitrary"), name=name)(*args)


def _s5_scan_bwd(name, u, dy, n_rows, row0, b_mat, bt_mat, ct_mat, abar, h_chunks, g_in, descending,
                 du_alias=None, du_rows=None):
    seg = n_rows // N_SEG
    ta = min(32, seg)
    nk = seg // ta
    rb = row0 // n_rows
    tile = ta * N_SEG
    g_desc = not descending

    def body(*refs):
        u_ref, dy_ref, b_ref, bt_ref, ct_ref, a_ref, hch_ref, gin_ref = refs[:8]
        du_ref, db_ref, dc_ref, da_ref, st_ref, up_ref, dyp_ref, h_ref, g_ref = refs[-9:]
        k = pl.program_id(1)
        kk = nk - 1 - k if g_desc else k
        a0 = kk * ta
        ar = jnp.broadcast_to(a_ref[0, :, :HALF_W], (N_SEG, HALF_W))
        ai = jnp.broadcast_to(a_ref[0, :, HALF_W:], (N_SEG, HALF_W))

        @pl.when(k == 0)
        def _():
            st_ref[...] = gin_ref[0]

        for al in range(ta):
            dyp_ref[al * N_SEG:(al + 1) * N_SEG, :] = dy_ref[pl.ds(a0 + al, N_SEG, stride=seg), :]
            up_ref[al * N_SEG:(al + 1) * N_SEG, :] = u_ref[pl.ds(a0 + al, N_SEG, stride=seg), :]
        g_ref[...] = jnp.dot(dyp_ref[...].astype(BF16), ct_ref[0], preferred_element_type=F32)
        h_ref[...] = jnp.dot(up_ref[...].astype(BF16), b_ref[0], preferred_element_type=F32)
        h0r, h0i = hch_ref[0, 0, :, :HALF_W], hch_ref[0, 0, :, HALF_W:]

        def hstep(i, carry):
            hr, hi = carry
            al = ta - 1 - i if descending else i
            row = pl.multiple_of(al * N_SEG, N_SEG)
            pr, pi = _cmul(ar, ai, hr, hi, False)
            hr = pr + h_ref[pl.ds(row, N_SEG), :HALF_W]
            hi = pi + h_ref[pl.ds(row, N_SEG), HALF_W:]
            h_ref[pl.ds(row, N_SEG), :HALF_W] = hr
            h_ref[pl.ds(row, N_SEG), HALF_W:] = hi
            return hr, hi

        lax.fori_loop(0, ta, hstep, (h0r, h0i), unroll=True)

        def gstep(i, carry):
            gr, gi = carry
            al = ta - 1 - i if g_desc else i
            row = pl.multiple_of(al * N_SEG, N_SEG)
            pr, pi = _cmul(ar, ai, gr, gi, True)
            gr = pr + g_ref[pl.ds(row, N_SEG), :HALF_W]
            gi = pi + g_ref[pl.ds(row, N_SEG), HALF_W:]
            g_ref[pl.ds(row, N_SEG), :HALF_W] = gr
            g_ref[pl.ds(row, N_SEG), HALF_W:] = gi
            return gr, gi

        gr, gi = lax.fori_loop(0, ta, gstep, (st_ref[:, :HALF_W], st_ref[:, HALF_W:]), unroll=True)
        st_ref[:, :HALF_W] = gr
        st_ref[:, HALF_W:] = gi

        gb = g_ref[...].astype(BF16)
        dut = jnp.dot(gb, bt_ref[0], preferred_element_type=F32)
        for al in range(ta):
            du_ref[pl.ds(a0 + al, N_SEG, stride=seg), :] = dut[al * N_SEG:(al + 1) * N_SEG, :]
        tn = (((0,), (0,)), ((), ()))
        dbp = lax.dot_general(up_ref[...].astype(BF16), gb, tn, preferred_element_type=F32)
        dcp = lax.dot_general(dyp_ref[...].astype(BF16), h_ref[...].astype(BF16), tn, preferred_element_type=F32)
        inner = (ta - 1) * N_SEG
        if descending:
            g_in_r, g_in_i = g_ref[0:inner, :HALF_W], g_ref[0:inner, HALF_W:]
            p_in_r, p_in_i = h_ref[N_SEG:tile, :HALF_W], h_ref[N_SEG:tile, HALF_W:]
            g_ed_r, g_ed_i = g_ref[inner:tile, :HALF_W], g_ref[inner:tile, HALF_W:]
        else:
            g_in_r, g_in_i = g_ref[N_SEG:tile, :HALF_W], g_ref[N_SEG:tile, HALF_W:]
            p_in_r, p_in_i = h_ref[0:inner, :HALF_W], h_ref[0:inner, HALF_W:]
            g_ed_r, g_ed_i = g_ref[0:N_SEG, :HALF_W], g_ref[0:N_SEG, HALF_W:]
        dar = g_ed_r * h0r + g_ed_i * h0i
        dai = g_ed_i * h0r - g_ed_r * h0i
        if ta > 1:
            dar = dar + jnp.sum((g_in_r * p_in_r + g_in_i * p_in_i).reshape(ta - 1, N_SEG, HALF_W), axis=0)
            dai = dai + jnp.sum((g_in_i * p_in_r - g_in_r * p_in_i).reshape(ta - 1, N_SEG, HALF_W), axis=0)

        @pl.when(k == 0)
        def _():
            db_ref[0] = dbp
            dc_ref[0] = dcp
            da_ref[0, :, :HALF_W] = dar
            da_ref[0, :, HALF_W:] = dai

        @pl.when(k > 0)
        def _():
            db_ref[0] += dbp
            dc_ref[0] += dcp
            da_ref[0, :, :HALF_W] += dar
            da_ref[0, :, HALF_W:] += dai

    u_spec = pl.BlockSpec((n_rows, SLAB_W), lambda s, k: (rb, s))
    m_spec = pl.BlockSpec((1, SLAB_W, STATE_W), lambda s, k: (s, 0, 0))
    mt_spec = pl.BlockSpec((1, STATE_W, SLAB_W), lambda s, k: (s, 0, 0))
    a_spec = pl.BlockSpec((1, 1, STATE_W), lambda s, k: (s, 0, 0))
    st_spec = pl.BlockSpec((1, N_SEG, STATE_W), lambda s, k: (s, 0, 0))
    st_shape = jax.ShapeDtypeStruct((N_SLAB, N_SEG, STATE_W), F32)
    kmap = (lambda s, k: (s, nk - 1 - k, 0, 0)) if g_desc else (lambda s, k: (s, k, 0, 0))
    in_specs = [u_spec, u_spec, m_spec, mt_spec, m_spec, a_spec, pl.BlockSpec((1, 1, N_SEG, STATE_W), kmap), st_spec]
    args = [u, dy, b_mat, bt_mat, ct_mat, abar, h_chunks, g_in]
    aliases = {}
    if du_alias is not None:
        in_specs.append(pl.BlockSpec(memory_space=pl.ANY))
        args.append(du_alias)
        aliases = {8: 0}
    acc_shape = jax.ShapeDtypeStruct((N_SLAB, SLAB_W, STATE_W), F32)
    out_specs = [u_spec, m_spec, m_spec, st_spec]
    out_shape = [jax.ShapeDtypeStruct((du_rows, D_MODEL), F32), acc_shape, acc_shape, st_shape]
    scratch = [pltpu.VMEM((N_SEG, STATE_W), F32), pltpu.VMEM((tile, SLAB_W), F32), pltpu.VMEM((tile, SLAB_W), F32),
               pltpu.VMEM((tile, STATE_W), F32), pltpu.VMEM((tile, STATE_W), F32)]
    return pl.pallas_call(
        body, grid=(N_SLAB, nk), in_specs=in_specs, out_specs=out_specs, out_shape=out_shape, scratch_shapes=scratch,
        input_output_aliases=aliases, compiler_params=_params("parallel", "arbitrary"), name=name)(*args)


ROPE_HALF = HEAD_DIM // 4
TABLE_W = 2 * HEAD_DIM
Q_SCALE = 1.0 / math.sqrt(HEAD_DIM)
HEADS_PER_BLOCK = 2 * KV_REP
Q_BLOCK_W = HEADS_PER_BLOCK * HEAD_DIM


def _rope_tables(n_lat, n_ctx):
    rows = n_lat // GRID_W
    freqs = ROPE_THETA ** (-jnp.arange(ROPE_HALF, dtype=F32) / ROPE_HALF)
    ang_r = jnp.arange(rows, dtype=F32)[:, None] * freqs[None]
    ang_c = jnp.arange(GRID_W, dtype=F32)[:, None] * freqs[None]
    by_row = lambda v: jnp.repeat(v, GRID_W, axis=0)
    by_col = lambda v: jnp.tile(v, (rows, 1))
    cos = jnp.concatenate([by_row(jnp.cos(ang_r)), by_row(jnp.cos(ang_r)), by_col(jnp.cos(ang_c)), by_col(jnp.cos(ang_c))] * 2,
                          axis=1)
    sin = jnp.concatenate([by_row(jnp.sin(ang_r)), by_row(jnp.sin(ang_r)), by_col(jnp.sin(ang_c)), by_col(jnp.sin(ang_c))] * 2,
                          axis=1)
    cos = jnp.concatenate([cos, jnp.ones((n_ctx, TABLE_W), F32)], axis=0)
    sin = jnp.concatenate([sin, jnp.zeros((n_ctx, TABLE_W), F32)], axis=0)
    return cos, sin


def _rot_half(v):
    w = v.shape[1]
    ahead = pltpu.roll(v, w - ROPE_HALF, axis=1)
    behind = pltpu.roll(v, ROPE_HALF, axis=1)
    lane = lax.broadcasted_iota(jnp.int32, v.shape, 1)
    return jnp.where((lane % (2 * ROPE_HALF)) < ROPE_HALF, -ahead, behind)


def _head_mean(v, sel, selt):
    m = jnp.dot(v, sel, precision=lax.Precision.HIGH, preferred_element_type=F32) * (1.0 / HEAD_DIM)
    return jnp.dot(m, selt, precision=lax.Precision.HIGH, preferred_element_type=F32)


def _head_selectors(n_heads):
    sel = jnp.repeat(jnp.eye(n_heads, dtype=F32), HEAD_DIM, axis=0)
    return sel[None], sel.T[None]


def _head_norm(x, sel, selt):
    r = lax.rsqrt(_head_mean(x * x, sel, selt) + NORM_EPS)
    return x * r, r


def _qk_prep(proj, qn, kn, cos, sin, n, tr):
    qw, kw = _vec(jnp.tile(qn, N_Q_HEADS)), _vec(jnp.tile(kn, N_KV_HEADS))
    sq, sqt = _head_selectors(N_Q_HEADS)
    sk, skt = _head_selectors(N_KV_HEADS)

    def fn(qr, kvr, ct, st, qwv, kwv, s16, s16t, s4, s4t):
        outs = []
        for x, wv, sel, selt, scale in ((qr, qwv, s16, s16t, Q_SCALE), (kvr[:, :KV_W], kwv, s4, s4t, 1.0)):
            reps = x.shape[1] // TABLE_W
            cw, sw = jnp.tile(ct, (1, reps)), jnp.tile(st, (1, reps))
            xh, _ = _head_norm(x, sel, selt)
            nrm = xh * wv
            outs.append((nrm * cw + _rot_half(nrm) * sw) * scale)
        return outs[0], outs[1], kvr[:, KV_W:]

    return _rowwise("l1_qk_prep", fn, n, tr,
                    [(proj, 0, ATTN_W), (proj, 2 * ATTN_W // (2 * KV_W), 2 * KV_W), (cos, 0, TABLE_W), (sin, 0, TABLE_W)],
                    [qw, kw, sq, sqt, sk, skt], [(ATTN_W, BF16), (KV_W, BF16), (KV_W, BF16)], [])


def _qk_prep_bwd(proj, qn, kn, cos, sin, dq, dz, dk, dv, n, n_lat, tr):
    qw, kw = _vec(jnp.tile(qn, N_Q_HEADS)), _vec(jnp.tile(kn, N_KV_HEADS))
    sq, sqt = _head_selectors(N_Q_HEADS)
    sk, skt = _head_selectors(N_KV_HEADS)
    nlt = n_lat // tr

    def fn(flag, qr, kvr, ct, st, dqt, dzt, dkt, dvt, qwv, kwv, s16, s16t, s4, s4t):
        dxs, dws = [], []
        for x, dy, wv, sel, selt in ((qr, dqt * (flag * Q_SCALE), qwv, s16, s16t), (kvr[:, :KV_W], dkt, kwv, s4, s4t)):
            reps = x.shape[1] // TABLE_W
            cw, sw = jnp.tile(ct, (1, reps)), jnp.tile(st, (1, reps))
            xh, r = _head_norm(x, sel, selt)
            dn = dy * cw - _rot_half(dy * sw)
            dxh = dn * wv
            dxs.append(r * (dxh - xh * _head_mean(dxh * xh, sel, selt)))
            dws.append(dn * xh)
        return jnp.concatenate([dxs[0], dzt * flag, dxs[1], dvt], axis=1), dws[0], dws[1]

    dproj, dqw, dkw = _rowwise(
        "l1_qk_prep_bwd", fn, n, tr,
        [(proj, 0, ATTN_W), (proj, 2 * ATTN_W // (2 * KV_W), 2 * KV_W), (cos, 0, TABLE_W), (sin, 0, TABLE_W),
         (dq, 0, ATTN_W, "clamp", nlt), (dz, 0, ATTN_W, "clamp", nlt), (dk, 0, KV_W), (dv, 0, KV_W)],
        [qw, kw, sq, sqt, sk, skt], [(2 * ATTN_W + 2 * KV_W, BF16)], [(1, ATTN_W), (1, KV_W)], n_lat=n_lat, want_flag=True)
    return dproj, dqw.reshape(N_Q_HEADS, HEAD_DIM).sum(0)[None], dkw.reshape(N_KV_HEADS, HEAD_DIM).sum(0)[None]


NT = (((1,), (1,)), ((), ()))


def _attn_fwd(q, k, v, t, tq, tk):
    n = k.shape[0]
    nkc = n // tk

    ts = _largest_tile(tq, 256, LANES)
    items = [(sub, j) for sub in range(tq // ts) for j in range(HEADS_PER_BLOCK)]

    def body(q_ref, k_ref, v_ref, o_ref, lse_ref, s_ref, m_ref, l_ref, acc_ref):
        def lanes(j):
            g = j // KV_REP
            return slice(j * HEAD_DIM, (j + 1) * HEAD_DIM), slice(g * HEAD_DIM, (g + 1) * HEAD_DIM)

        for idx in range(len(items) + 1):
            nxt = items[idx] if idx < len(items) else None
            cur = items[idx - 1] if idx > 0 else None
            sn, sc = idx % 2, (idx - 1) % 2
            if nxt is not None:
                rows_n = slice(nxt[0] * ts, (nxt[0] + 1) * ts)
                ql_n, kl_n = lanes(nxt[1])
                qv = q_ref[rows_n, ql_n]
                m_ref[sn] = jnp.full((ts, LANES), -jnp.inf, F32)
            if cur is not None:
                rows_c = slice(cur[0] * ts, (cur[0] + 1) * ts)
                ql_c, kl_c = lanes(cur[1])
                m_row = jnp.max(m_ref[sc], axis=-1, keepdims=True)
                l_ref[...] = jnp.zeros(l_ref.shape, F32)
                acc_ref[...] = jnp.zeros(acc_ref.shape, F32)

            def sweep(kc, c):
                off = pl.multiple_of(kc * tk, tk)
                if nxt is not None:
                    s = lax.dot_general(qv, k_ref[pl.ds(off, tk), kl_n], NT, preferred_element_type=F32)
                    s_ref[sn, :, pl.ds(off, tk)] = s
                    m = m_ref[sn]
                    for cb in range(tk // LANES):
                        m = jnp.maximum(m, s[:, cb * LANES:(cb + 1) * LANES])
                    m_ref[sn] = m
                if cur is not None:
                    p = jnp.exp(s_ref[sc, :, pl.ds(off, tk)] - m_row)
                    lsum = l_ref[...]
                    for cb in range(tk // LANES):
                        lsum = lsum + p[:, cb * LANES:(cb + 1) * LANES]
                    l_ref[...] = lsum
                    acc_ref[...] += jnp.dot(p.astype(BF16), v_ref[pl.ds(off, tk), kl_c], preferred_element_type=F32)
                return c

            lax.fori_loop(0, nkc, sweep, 0, unroll=True)
            if cur is not None:
                l_row = jnp.sum(l_ref[...], axis=-1, keepdims=True)
                o_ref[rows_c, ql_c] = acc_ref[...] / l_row
                lse_ref[0, rows_c, cur[1]:cur[1] + 1] = m_row + jnp.log(l_row)

    nb = ATTN_W // Q_BLOCK_W
    kspec = pl.BlockSpec((n, LANES), lambda b, i: (0, b))
    return pl.pallas_call(
        body, grid=(nb, t // tq),
        in_specs=[pl.BlockSpec((tq, Q_BLOCK_W), lambda b, i: (i, b)), kspec, kspec],
        out_specs=[pl.BlockSpec((tq, Q_BLOCK_W), lambda b, i: (i, b)),
                   pl.BlockSpec((1, tq, HEADS_PER_BLOCK), lambda b, i: (b, i, 0))],
        out_shape=[jax.ShapeDtypeStruct((t, ATTN_W), F32), jax.ShapeDtypeStruct((nb, t, HEADS_PER_BLOCK), F32)],
        scratch_shapes=[pltpu.VMEM((2, ts, n), F32), pltpu.VMEM((2, ts, LANES), F32), pltpu.VMEM((ts, LANES), F32),
                        pltpu.VMEM((ts, HEAD_DIM), F32)],
        compiler_params=_params("parallel", "parallel"), name="attn_fwd")(q, k, v)


def _attn_bwd(q, k, v, do, o, lse, t, tq, tk):
    n = k.shape[0]
    nkc = n // tk
    tn = (((0,), (0,)), ((), ()))

    def body(q_ref, k_ref, v_ref, do_ref, o_ref, lse_ref, dq_ref, dk_ref, dv_ref, acc_ref):
        @pl.when(pl.program_id(1) == 0)
        def _():
            dk_ref[...] = jnp.zeros(dk_ref.shape, F32)
            dv_ref[...] = jnp.zeros(dv_ref.shape, F32)

        for j0 in range(0, HEADS_PER_BLOCK, 2):
            kl = slice((j0 // KV_REP) * HEAD_DIM, (j0 // KV_REP + 1) * HEAD_DIM)
            heads = []
            for a in range(2):
                j = j0 + a
                ql = slice(j * HEAD_DIM, (j + 1) * HEAD_DIM)
                qv, dov = q_ref[:, ql], do_ref[:, ql]
                dl_v = jnp.sum(dov.astype(F32) * o_ref[:, ql], axis=-1, keepdims=True)
                heads.append((ql, qv, dov, dl_v, lse_ref[0, :, j:j + 1]))
                acc_ref[a] = jnp.zeros((tq, HEAD_DIM), F32)

            def step(kc, c):
                off = pl.multiple_of(kc * tk, tk)
                kt = k_ref[pl.ds(off, tk), kl]
                vt = v_ref[pl.ds(off, tk), kl]
                dv_part, dk_part = None, None
                for a, (_, qv, dov, dl_v, lse_v) in enumerate(heads):
                    s = lax.dot_general(qv, kt, NT, preferred_element_type=F32)
                    p = jnp.exp(s - lse_v)
                    dp = lax.dot_general(dov, vt, NT, preferred_element_type=F32)
                    ds = (p * (dp - dl_v)).astype(BF16)
                    acc_ref[a] += jnp.dot(ds, kt, preferred_element_type=F32)
                    dvp = lax.dot_general(p.astype(BF16), dov, tn, preferred_element_type=F32)
                    dkp = lax.dot_general(ds, qv, tn, preferred_element_type=F32)
                    dv_part = dvp if dv_part is None else dv_part + dvp
                    dk_part = dkp if dk_part is None else dk_part + dkp
                dv_ref[pl.ds(off, tk), kl] += dv_part
                dk_ref[pl.ds(off, tk), kl] += dk_part
                return c

            lax.fori_loop(0, nkc, step, 0, unroll=2)
            for a, h in enumerate(heads):
                dq_ref[:, h[0]] = acc_ref[a]

    nb = ATTN_W // Q_BLOCK_W
    qspec = pl.BlockSpec((tq, Q_BLOCK_W), lambda b, i: (i, b))
    kspec = pl.BlockSpec((n, LANES), lambda b, i: (0, b))
    cspec = pl.BlockSpec((1, tq, HEADS_PER_BLOCK), lambda b, i: (b, i, 0))
    return pl.pallas_call(
        body, grid=(nb, t // tq), in_specs=[qspec, kspec, kspec, qspec, qspec, cspec], out_specs=[qspec, kspec, kspec],
        out_shape=[jax.ShapeDtypeStruct((t, ATTN_W), F32), jax.ShapeDtypeStruct((n, KV_W), F32),
                   jax.ShapeDtypeStruct((n, KV_W), F32)],
        scratch_shapes=[pltpu.VMEM((2, tq, HEAD_DIM), F32)],
        compiler_params=_params("parallel", "arbitrary"), name="attn_bwd")(q, k, v, do, o, lse)


def _s5_system(p, n_lat, n_ctx):
    two_g = 2 * SSM_GROUPS
    a_re = p["ssm_a_re"].reshape(two_g, SSM_STATE)
    a_im = p["ssm_a_im"].reshape(two_g, SSM_STATE)
    log_dt = p["ssm_log_dt"].reshape(two_g, 1)
    b_re = p["ssm_b_re"].reshape(two_g, SSM_STATE, SSM_GROUP).transpose(2, 0, 1)
    b_im = p["ssm_b_im"].reshape(two_g, SSM_STATE, SSM_GROUP).transpose(2, 0, 1)
    raw = (a_re, a_im, log_dt, b_re, b_im)
    abr, abi, bbr, bbi, alr, ali, acr, aci = _s5_prep(*raw, n_lat // N_SEG, n_ctx // N_SEG)
    dirs = []
    for d in range(2):
        g = slice(d * SSM_GROUPS, (d + 1) * SSM_GROUPS)
        b_mat = _slab_in_matrix(bbr[:, g].transpose(1, 2, 0), bbi[:, g].transpose(1, 2, 0))
        c_mat = _slab_out_matrix(p["ssm_c_re"][0, d], p["ssm_c_im"][0, d])
        abar = _slab_pair(abr[g], abi[g])
        tables = {}
        for part, seg in (("lat", n_lat // N_SEG), ("ctx", n_ctx // N_SEG)):
            tables["h_" + part] = _s5_pow_table(f"s5_pow_h{d}_{part}", abar, seg, d == 0, False)
            tables["g_" + part] = _s5_pow_table(f"s5_pow_g{d}_{part}", abar, seg, d == 1, True)
        dirs.append(dict(
            b=b_mat.astype(BF16), bt=b_mat.transpose(0, 2, 1).astype(BF16), b32=b_mat,
            c=c_mat.astype(BF16), ct=c_mat.transpose(0, 2, 1).astype(BF16), ct32=c_mat.transpose(0, 2, 1),
            abar=abar, a_lat=_slab_pair(alr[g], ali[g])[:, 0], a_ctx=_slab_pair(acr[g], aci[g])[:, 0], **tables))
    return raw, dirs


def _s5_forward(proj, dirs, n_lat, n_ctx):
    n = n_lat + n_ctx
    zero_c = jnp.zeros((N_SLAB, STATE_W), F32)
    ys, saved = [], []
    for d, s in enumerate(dirs):
        desc = d == 1
        tag = f"s5f{d}"
        zc = _s5_ends(tag + "_ctx_ends", proj, n_ctx, n_lat, s["h_ctx"], s["b32"])
        ent_c, h0 = _s5_carry(tag + "_ctx_carry", _seg_major(zc), s["a_ctx"], zero_c, desc, False)
        y, hch_c = _s5_scan(tag + "_ctx", proj, n_ctx, n_lat, s["b"], s["c"], s["abar"], _seg_major(ent_c), desc,
                            y_rows=n)
        zl = _s5_ends(tag + "_lat_ends", proj, n_lat, 0, s["h_lat"], s["b32"])
        ent_l, _ = _s5_carry(tag + "_lat_carry", _seg_major(zl), s["a_lat"], h0, desc, False)
        y, hch_l = _s5_scan(tag + "_lat", proj, n_lat, 0, s["b"], s["c"], s["abar"], _seg_major(ent_l), desc,
                            y_alias=y, y_rows=n)
        ys.append(y)
        saved.append((hch_l, hch_c))
    return ys, saved


def _s5_backward(proj, dy, dirs, saved, n_lat, n_ctx):
    n = n_lat + n_ctx
    zero_c = jnp.zeros((N_SLAB, STATE_W), F32)
    out = []
    for d, s in enumerate(dirs):
        desc = d == 1
        tag = f"s5b{d}"
        hch_l, hch_c = saved[d]
        gl = _s5_ends(tag + "_lat_ends", dy, n_lat, 0, s["g_lat"], s["ct32"])
        ent_l, g0 = _s5_carry(tag + "_lat_carry", _seg_major(gl), s["a_lat"], zero_c, not desc, True)
        du, db_l, dc_l, da_l = _s5_scan_bwd(tag + "_lat", proj, dy, n_lat, 0, s["b"], s["bt"], s["ct"], s["abar"],
                                            hch_l, _seg_major(ent_l), desc, du_rows=n)
        gc = _s5_ends(tag + "_ctx_ends", dy, n_ctx, n_lat, s["g_ctx"], s["ct32"])
        ent_c, _ = _s5_carry(tag + "_ctx_carry", _seg_major(gc), s["a_ctx"], g0, not desc, True)
        du, db_c, dc_c, da_c = _s5_scan_bwd(tag + "_ctx", proj, dy, n_ctx, n_lat, s["b"], s["bt"], s["ct"], s["abar"],
                                            hch_c, _seg_major(ent_c), desc, du_alias=du, du_rows=n)
        out.append((du, db_l + db_c, dc_l + dc_c, da_l + da_c))
    return out


def _s5_param_grads(raw, bwd):
    dabr, dabi, dbbr, dbbi, dcr, dci = [], [], [], [], [], []
    for _, db, dc, da in bwd:
        da = jnp.sum(da, axis=1)
        dabr.append(da[:, :HALF_W].reshape(SSM_GROUPS, SSM_STATE))
        dabi.append(da[:, HALF_W:].reshape(SSM_GROUPS, SSM_STATE))
        dbd = _slab_diag(db)
        dbbr.append(dbd[0].transpose(1, 0, 2))
        dbbi.append(dbd[1].transpose(1, 0, 2))
        dcd = _slab_diag(dc)
        dcr.append(dcd[0])
        dci.append(-dcd[1])
    cat = lambda xs, ax: jnp.concatenate(xs, axis=ax)
    dar, dai, dld, dbr, dbi = _s5_prep_bwd(*raw, cat(dabr, 0), cat(dabi, 0), cat(dbbr, 1), cat(dbbi, 1))
    shp = (1, 2, SSM_GROUPS, SSM_STATE)
    b_shape = (1, 2, SSM_GROUPS, SSM_STATE, SSM_GROUP)
    return dict(
        ssm_a_re=dar.reshape(shp), ssm_a_im=dai.reshape(shp), ssm_log_dt=dld.reshape(1, 2, SSM_GROUPS),
        ssm_b_re=dbr.transpose(1, 2, 0).reshape(b_shape), ssm_b_im=dbi.transpose(1, 2, 0).reshape(b_shape),
        ssm_c_re=jnp.stack(dcr)[None], ssm_c_im=jnp.stack(dci)[None])


def _example_step(x, ctx, target, mods, w, p):
    t, c = x.shape[0], ctx.shape[0]
    n = t + c
    assert t % c == 0 and c % LANES == 0 and c % (8 * N_SEG) == 0 and t % GRID_W == 0
    tr = _largest_tile(c, 256, 8)
    xall = jnp.concatenate([x, ctx], axis=0)
    g0, g1 = _vec(p["norm_g"][0]), _vec(p["norm_g"][1])
    (shift0, scale0, gate0), (shift1, scale1, gate1) = [tuple(_vec(v) for v in m) for m in mods]

    h0 = _norm_mod_fwd("l0_norm", xall, g0, scale0, shift0, n, tr, t)
    proj0 = _mm("l0_in", h0, w["ssm_w_in"], "nn")
    raw, dirs = _s5_system(p, t, c)
    (y_f, y_r), saved = _s5_forward(proj0, dirs, t, c)
    d_skip = _vec(p["ssm_d"][0])

    def post_a(u, yf, yr, dv):
        y = u * dv + yf + yr
        return y, _gelu(y)

    y0, yg = _rowwise("l0_gelu", post_a, n, tr, [(proj0, 0, D_MODEL), (y_f, 0, D_MODEL), (y_r, 0, D_MODEL)], [d_skip],
                      [(D_MODEL, F32), (D_MODEL, F32)], [])
    tg = _mm("l0_glu", yg, w["ssm_w_glu"], "nn")
    b_glu = _vec(p["ssm_b_glu"][0])

    def post_b(ygt, tt, zt, bv):
        return ygt * _sigmoid(tt + bv) * _silu(zt)

    gz0 = _rowwise("l0_gate", post_b, n, tr, [(yg, 0, D_MODEL), (tg, 0, D_MODEL), (proj0, 1, D_MODEL)], [b_glu],
                   [(D_MODEL, BF16)], [])[0]
    out0 = _mm("l0_out", gz0, w["ssm_w_out"], "nn")

    def res_norm(xt, ot, gv, g1v, sc, sh):
        x1t = xt + gv * ot
        xh, _ = _rms_hat(x1t)
        return x1t, (xh * g1v) * (1.0 + sc) + sh

    x1, h1 = _rowwise("l0_res_l1_norm", res_norm, n, tr, [(xall, 0, D_MODEL), (out0, 0, D_MODEL)],
                      [gate0, g1, scale1, shift1], [(D_MODEL, F32), (D_MODEL, BF16)], [], n_lat=t)
    proj1 = _mm("l1_in", h1, w["attn_w_in"], "nn")
    cos, sin = _rope_tables(t, c)
    qn, kn = p["attn_q_norm"][0], p["attn_k_norm"][0]
    q_h, k_h, v_h = _qk_prep(proj1, qn, kn, cos, sin, n, tr)
    tq = _largest_tile(t, 512, LANES)
    o, lse = _attn_fwd(q_h, k_h, v_h, t, tq, _largest_tile(n, 2816, LANES))
    gz1 = _rowwise("l1_gate", lambda ot, zt: ot * _silu(zt), t, tr, [(o, 0, D_MODEL), (proj1, 1, D_MODEL)], [],
                   [(D_MODEL, BF16)], [])[0]
    out1 = _mm("l1_out", gz1, w["attn_w_out"], "nn")

    gf = _vec(p["final_norm_g"])

    def head(x1t, o1t, tgt, g1v, gfv):
        x2 = x1t + g1v * o1t
        xh, r = _rms_hat(x2)
        e = xh * gfv - tgt
        dyf = e * (1.0 / D_MODEL)
        dx2 = _rms_bwd(xh, r, dyf * gfv)
        return dx2, g1v * dx2, dyf * xh, dx2 * o1t, jnp.sum(e * e, axis=1, keepdims=True)

    gate1_lat = gate1[0:1]
    dx2, dout1, d_gf, d_gate1, sq = _rowwise(
        "head", head, t, tr, [(x1, 0, D_MODEL), (out1, 0, D_MODEL), (target, 0, D_MODEL)], [gate1_lat, gf],
        [(D_MODEL, F32), (D_MODEL, BF16)], [(1, D_MODEL), (1, D_MODEL), (1, 1)])

    d_w_attn_out = _mm("l1_out_dw", gz1, dout1, "tn", out_dtype=BF16)
    dgz1 = _mm("l1_out_dx", dout1, w["attn_w_out"], "nt")

    def gate1_bwd(dgt, ot, zt):
        return dgt * _silu(zt), dgt * ot * _silu_grad(zt)

    do, dz1 = _rowwise("l1_gate_bwd", gate1_bwd, t, tr, [(dgz1, 0, D_MODEL), (o, 0, D_MODEL), (proj1, 1, D_MODEL)], [],
                       [(D_MODEL, BF16), (D_MODEL, F32)], [])
    dq_s, dk, dv = _attn_bwd(q_h, k_h, v_h, do, o, lse, t, tq, _largest_tile(n, 1024, LANES))
    dproj1, d_qn, d_kn = _qk_prep_bwd(proj1, qn, kn, cos, sin, dq_s, dz1, dk, dv, n, t, tr)
    d_w_attn_in = _mm("l1_in_dw", h1, dproj1, "tn", out_dtype=BF16)
    dh1 = _mm("l1_in_dx", dproj1, w["attn_w_in"], "nt")
    dx1, dout0, d_g1, d_scale1, d_shift1, d_gate0 = _norm_mod_bwd("l1_norm_bwd", x1, g1, scale1, dh1, dx2, n, tr, t,
                                                                  prev=(out0, gate0))

    d_w_out = _mm("l0_out_dw", gz0, dout0, "tn", out_dtype=BF16)
    dgz0 = _mm("l0_out_dx", dout0, w["ssm_w_out"], "nt")

    def post_b_bwd(dgt, ygt, tt, zt, bv):
        s = _sigmoid(tt + bv)
        dy2 = dgt * _silu(zt)
        dt = dy2 * ygt * s * (1.0 - s)
        return dgt * (ygt * s) * _silu_grad(zt), dt, dy2 * s, dt

    dz0, dtg, dyg_a, d_b_glu = _rowwise(
        "l0_gate_bwd", post_b_bwd, n, tr, [(dgz0, 0, D_MODEL), (yg, 0, D_MODEL), (tg, 0, D_MODEL), (proj0, 1, D_MODEL)],
        [b_glu], [(D_MODEL, BF16), (D_MODEL, BF16), (D_MODEL, F32)], [(1, D_MODEL)])
    d_w_glu = _mm("l0_glu_dw", yg, dtg, "tn", out_dtype=BF16)
    dyg_b = _mm("l0_glu_dx", dtg, w["ssm_w_glu"], "nt")

    def post_a_bwd(da, db, yt, ut, dv):
        dy = (da + db) * _gelu_grad(yt)
        return dy, dy * dv, dy * ut

    dy0, du_skip, d_d = _rowwise("l0_gelu_bwd", post_a_bwd, n, tr,
                                 [(dyg_a, 0, D_MODEL), (dyg_b, 0, D_MODEL), (y0, 0, D_MODEL), (proj0, 0, D_MODEL)], [d_skip],
                                 [(D_MODEL, F32), (D_MODEL, F32)], [(1, D_MODEL)])
    s5_bwd = _s5_backward(proj0, dy0, dirs, saved, t, c)
    dproj0 = _rowwise("l0_in_grad", lambda a, b, cc, dz: jnp.concatenate([a + b + cc, dz], axis=1), n, tr,
                      [(du_skip, 0, D_MODEL), (s5_bwd[0][0], 0, D_MODEL), (s5_bwd[1][0], 0, D_MODEL), (dz0, 0, D_MODEL)], [],
                      [(2 * D_MODEL, BF16)], [])[0]
    d_w_in = _mm("l0_in_dw", h0, dproj0, "tn", out_dtype=BF16)
    dh0 = _mm("l0_in_dx", dproj0, w["ssm_w_in"], "nt")
    dx0, d_g0, d_scale0, d_shift0 = _norm_mod_bwd("l0_norm_bwd", xall, g0, scale0, dh0, dx1, n, tr, t)

    big = dict(ssm_w_in=d_w_in, ssm_w_glu=d_w_glu, ssm_w_out=d_w_out, attn_w_in=d_w_attn_in, attn_w_out=d_w_attn_out)
    small = dict(
        norm_g=jnp.concatenate([d_g0[0], d_g1[0]], axis=0), ssm_d=d_d[0], ssm_b_glu=d_b_glu[0],
        attn_q_norm=d_qn, attn_k_norm=d_kn, final_norm_g=d_gf[0, 0], **_s5_param_grads(raw, s5_bwd))
    zero_v = jnp.zeros((D_MODEL,), F32)
    d_mod_lat = jnp.stack([jnp.concatenate([d_shift0[0, 0], d_scale0[0, 0], d_gate0[0, 0]]),
                           jnp.concatenate([d_shift1[0, 0], d_scale1[0, 0], d_gate1[0, 0]])])
    d_mod_ctx = jnp.stack([jnp.concatenate([d_shift0[1, 0], d_scale0[1, 0], d_gate0[1, 0]]),
                           jnp.concatenate([d_shift1[1, 0], d_scale1[1, 0], zero_v])])
    return sq[0, 0, 0], dx0[:t], big, small, d_mod_lat, d_mod_ctx


def _adamw(name, w, g, m, v):
    rows, cols = w.shape
    tr = _largest_tile(rows, 256, 8)
    c1 = 1.0 / (1.0 - ADAM_B1 ** ADAM_STEP)
    c2 = 1.0 / (1.0 - ADAM_B2 ** ADAM_STEP)

    def fn(wt, gt, mt, vt):
        mn = ADAM_B1 * mt + (1.0 - ADAM_B1) * gt
        vn = ADAM_B2 * vt + (1.0 - ADAM_B2) * (gt * gt)
        delta = -ADAM_LR * ((mn * c1) / (jnp.sqrt(vn * c2) + ADAM_EPS) + ADAM_WD * wt)
        return delta, mn, vn

    return _rowwise(name, fn, rows, tr, [(a, 0, cols) for a in (w, g, m, v)], [], [(cols, F32)] * 3, [])


BIG = ("ssm_w_in", "ssm_w_glu", "ssm_w_out", "attn_w_in", "attn_w_out")
COL_SHARDED = ("ssm_w_in", "attn_w_in")
WEIGHTS = ("c_ctx", "w_mod", "b_mod", "norm_g", "ssm_w_in", "ssm_a_re", "ssm_a_im", "ssm_log_dt", "ssm_b_re", "ssm_b_im",
           "ssm_c_re", "ssm_c_im", "ssm_d", "ssm_w_glu", "ssm_b_glu", "ssm_w_out", "attn_w_in", "attn_q_norm",
           "attn_k_norm", "attn_w_out", "final_norm_g")
SMALL = tuple(k for k in WEIGHTS if k not in BIG and k != "w_mod")
PACK_W = 1024


def _attn_in_perm(x, inverse):
    a, kv = ATTN_W, 2 * KV_W
    if inverse:
        return jnp.concatenate([x[..., :a], x[..., 2 * a:], x[..., a:2 * a]], axis=-1)
    return jnp.concatenate([x[..., :a], x[..., a + kv:], x[..., a:a + kv]], axis=-1)


def _pack(arrays, dtype, row_unit):
    flat = jnp.concatenate([a.reshape(-1).astype(dtype) for a in arrays])
    rows = -(-flat.shape[0] // PACK_W)
    rows = -(-rows // row_unit) * row_unit
    flat = jnp.concatenate([flat, jnp.zeros((rows * PACK_W - flat.shape[0],), dtype)])
    return flat.reshape(rows, PACK_W)


def _unpack(buf, shapes):
    lead = buf.shape[:-2]
    flat = buf.reshape(lead + (-1,))
    out, off = [], 0
    for shp in shapes:
        size = math.prod(shp)
        out.append(flat[..., off:off + size].reshape(lead + tuple(shp)))
        off += size
    return out


def _half_shape(name, shard_shape):
    r, ccols = shard_shape
    return (r // 2, ccols)


def kernel(x, c, ctx, c_ctx, w_mod, b_mod, norm_g, ssm_w_in, ssm_a_re, ssm_a_im, ssm_log_dt, ssm_b_re, ssm_b_im, ssm_c_re, ssm_c_im, ssm_d, ssm_w_glu, ssm_b_glu, ssm_w_out, attn_w_in, attn_q_norm, attn_k_norm, attn_w_out, final_norm_g, loss_target, m_c_ctx, m_w_mod, m_b_mod, m_norm_g, m_ssm_w_in, m_ssm_a_re, m_ssm_a_im, m_ssm_log_dt, m_ssm_b_re, m_ssm_b_im, m_ssm_c_re, m_ssm_c_im, m_ssm_d, m_ssm_w_glu, m_ssm_b_glu, m_ssm_w_out, m_attn_w_in, m_attn_q_norm, m_attn_k_norm, m_attn_w_out, m_final_norm_g, v_c_ctx, v_w_mod, v_b_mod, v_norm_g, v_ssm_w_in, v_ssm_a_re, v_ssm_a_im, v_ssm_log_dt, v_ssm_b_re, v_ssm_b_im, v_ssm_c_re, v_ssm_c_im, v_ssm_d, v_ssm_w_glu, v_ssm_b_glu, v_ssm_w_out, v_attn_w_in, v_attn_q_norm, v_attn_k_norm, v_attn_w_out, v_final_norm_g):
    args = dict(locals())
    wts = {k: args[k] for k in WEIGHTS}
    mom_m = {k: args["m_" + k] for k in WEIGHTS}
    mom_v = {k: args["v_" + k] for k in WEIGHTS}
    mx, my, mc = lax.axis_index("x"), lax.axis_index("y"), lax.axis_index("c")
    chip = 2 * mx + my
    me = 2 * chip + mc

    halves = []
    for k in BIG:
        sh = wts[k][0]
        hr = sh.shape[0] // 2
        halves.append(lax.dynamic_slice_in_dim(sh, mc * hr, hr, axis=0))
    gathered = _gather_two_level("gather_weights", _pack(halves, BF16, 16))
    parts = _unpack(gathered, [h.shape for h in halves])
    w_full = {}
    for k, pc in zip(BIG, parts):
        hr, cols = pc.shape[1:]
        pc = pc.reshape(N_CHIP, 2, hr, cols)
        if k in COL_SHARDED:
            w_full[k] = pc.transpose(1, 2, 0, 3).reshape(2 * hr, N_CHIP * cols)
        else:
            w_full[k] = pc.reshape(N_CHIP * 2 * hr, cols)
    w_full["attn_w_in"] = _attn_in_perm(w_full["attn_w_in"], False)

    c_blk = jnp.concatenate([c, jnp.zeros((7, D_MODEL), F32)], axis=0)
    c_all = _exchange("gather_c", c_blk, True)[:, 0]
    cond = jnp.concatenate([c_all, c_ctx[None], jnp.zeros((7, D_MODEL), F32)], axis=0)
    s_cond, ds_cond = _rowwise("cond_silu", lambda t: (_silu(t), _silu_grad(t)), 16, 16, [(cond, 0, D_MODEL)], [],
                               [(D_MODEL, F32), (D_MODEL, F32)], [])
    w_mod_b = w_mod.astype(BF16)
    mcols = w_mod.shape[2]
    mod_part = jnp.stack([_mm(f"mod{i}", s_cond, w_mod_b[i], "nn") for i in range(2)])
    mod_g = _exchange("gather_mod", mod_part.reshape(32, mcols), True)
    mod_all = mod_g.reshape(N_CHIP, 2, 2, 16, mcols)[:, 0]
    mod_all = mod_all.transpose(1, 2, 0, 3).reshape(2, 16, N_CHIP * mcols) + b_mod[:, None, :]
    mods = []
    for i in range(2):
        lat = lax.dynamic_slice_in_dim(mod_all[i], me, 1, axis=0)[0]
        both = jnp.stack([lat, mod_all[i, 8]])
        mods.append((both[:, :D_MODEL], both[:, D_MODEL:2 * D_MODEL], both[:, 2 * D_MODEL:]))

    small_p = {k: wts[k] for k in SMALL if k != "c_ctx" and k != "b_mod"}
    sq, grad_x, big_g, small_g, d_mod_lat, d_mod_ctx = _example_step(x[0], ctx[0], loss_target[0], mods, w_full, small_p)
    loss = lax.psum(0.5 / D_MODEL * sq, ("x", "y", "c"))
    big_g["attn_w_in"] = _attn_in_perm(big_g["attn_w_in"], True)

    small_names = [k for k in SMALL if k not in ("c_ctx", "b_mod")]
    small_list = [small_g[k] for k in small_names] + [d_mod_lat, d_mod_ctx]
    small_shapes = [wts[k].shape for k in small_names] + [d_mod_lat.shape, d_mod_ctx.shape]
    packed = _pack(small_list, F32, 8 * N_DEV)
    slice_rows = packed.shape[0] // N_DEV
    slices = _exchange("scatter_small", packed.reshape(N_DEV, slice_rows, PACK_W), False)
    my_sum = _sum_slots("sum_small", slices)
    payload = jnp.concatenate([my_sum, _pack([d_mod_lat], F32, 8)], axis=0)
    sg = _exchange("gather_small", payload, True)
    summed = _unpack(sg[:, :slice_rows].reshape(packed.shape), small_shapes)
    grads = dict(zip(small_names, summed[:-2]))
    d_mod_lat_sum, d_mod_ctx_sum = summed[-2], summed[-1]
    grads["b_mod"] = d_mod_lat_sum + d_mod_ctx_sum
    d_mod_lat_all = _unpack(sg[:, slice_rows:], [d_mod_lat.shape])[0]

    g_w_mod, ds_cc = [], []
    for i in range(2):
        rows9 = jnp.concatenate([d_mod_lat_all[:, i], d_mod_ctx_sum[i][None], jnp.zeros((7, 3 * D_MODEL), F32)], axis=0)
        mine = lax.dynamic_slice_in_dim(rows9, chip * mcols, mcols, axis=1)
        g_w_mod.append(_mm(f"mod{i}_dw", s_cond, mine, "tn"))
        ds_cc.append(_mm(f"mod{i}_dx", mine, w_mod_b[i], "nt")[8])
    grads["w_mod"] = jnp.stack(g_w_mod)
    part = (ds_cc[0] + ds_cc[1]) * jnp.where(mc == 0, 1.0, 0.0)
    part_blk = jnp.concatenate([part[None], jnp.zeros((7, D_MODEL), F32)], axis=0)
    ds_all = _sum_slots("sum_c_ctx", _exchange("gather_c_ctx", part_blk, True))
    grads["c_ctx"] = ds_all[0] * ds_cond[8]

    blocks = []
    for k in BIG:
        g = big_g[k]
        rows, cols = g.shape
        if k in COL_SHARDED:
            blocks.append(g.reshape(2, rows // 2, N_CHIP, cols // N_CHIP).transpose(2, 0, 1, 3).reshape(N_DEV, -1))
        else:
            blocks.append(g.reshape(N_DEV, -1))
    sendbuf = jnp.concatenate(blocks, axis=1).astype(BF16)
    sendbuf = sendbuf.reshape(N_DEV, -1, PACK_W)
    recv = _exchange("scatter_big", sendbuf, False)
    mine = _sum_slots("sum_big", recv)
    both = _exchange("swap_halves", mine, True, sibling_only=True, chunks=8)
    half_shapes = [(wts[k].shape[1] // 2, wts[k].shape[2]) for k in BIG]
    for k, pc in zip(BIG, _unpack(both, half_shapes)):
        grads[k] = pc.reshape(wts[k].shape)

    delta, new_m, new_v = {}, {}, {}
    for k in BIG + ("w_mod",):
        shp = wts[k].shape
        two_d = (-1, shp[-1])
        res = _adamw("adamw_" + k, *[a.reshape(two_d) for a in (wts[k], grads[k], mom_m[k], mom_v[k])])
        delta[k], new_m[k], new_v[k] = [r.reshape(shp) for r in res]
    shapes = [wts[k].shape for k in SMALL]
    packed = [_pack([d[k] for k in SMALL], F32, 8) for d in (wts, grads, mom_m, mom_v)]
    res = _adamw("adamw_small", *packed)
    for dst, buf in zip((delta, new_m, new_v), res):
        for k, a in zip(SMALL, _unpack(buf, shapes)):
            dst[k] = a
    grads = {k: grads[k].reshape(wts[k].shape) for k in WEIGHTS}
    return (loss, grad_x[None], *[grads[k] for k in WEIGHTS], *[delta[k] for k in WEIGHTS],
            *[new_m[k] for k in WEIGHTS], *[new_v[k] for k in WEIGHTS])
```

```python
import functools
import math

import jax
import jax.numpy as jnp
from jax import lax
from jax.experimental import pallas as pl
from jax.experimental.pallas import tpu as pltpu

F32 = jnp.float32
BF16 = jnp.bfloat16

D_MODEL = 1024
NORM_EPS = 1e-6
SSM_GROUPS = 64
SSM_GROUP = 16
SSM_STATE = 64
LANES = 128
SLAB_W = LANES
N_SLAB = D_MODEL // SLAB_W
SLAB_GROUPS = SLAB_W // SSM_GROUP
HALF_W = SLAB_GROUPS * SSM_STATE
STATE_W = 2 * HALF_W
N_SEG = 8
HEAD_DIM = 64
N_Q_HEADS = 16
N_KV_HEADS = 4
KV_REP = N_Q_HEADS // N_KV_HEADS
ATTN_W = N_Q_HEADS * HEAD_DIM
KV_W = N_KV_HEADS * HEAD_DIM
GRID_W = 64
ROPE_THETA = 10000.0
N_DEV = 8
N_CHIP = 4
VMEM_LIMIT_BYTES = 56 * 1024 * 1024

ADAM_LR = 0.001
ADAM_B1 = 0.9
ADAM_B2 = 0.999
ADAM_EPS = 1e-08
ADAM_WD = 0.01
ADAM_STEP = 10


def _params(*sem):
    return pltpu.CompilerParams(dimension_semantics=sem, vmem_limit_bytes=VMEM_LIMIT_BYTES)


def _largest_tile(n, cap, unit):
    if n <= cap:
        return n
    t = (cap // unit) * unit
    while t >= unit:
        if n % t == 0:
            return t
        t -= unit
    raise ValueError(f"no tile for {n} (cap {cap}, unit {unit})")


def _rowwise(name, fn, n_rows, tr, row_ins, vec_ins, row_outs, red_outs, n_lat=None, want_flag=False):
    nt = n_rows // tr
    assert nt * tr == n_rows
    nlt = nt if n_lat is None else n_lat // tr

    def sel(i):
        return jnp.where(i >= nlt, 1, 0)

    arrays, in_specs = [], []
    for spec in row_ins:
        arr, cb, w = spec[:3]
        kind = spec[3] if len(spec) > 3 else None
        m = spec[4] if len(spec) > 4 else None
        if kind == "mod":
            imap = functools.partial(lambda i, cb, m: (i % m, cb), cb=cb, m=m)
        elif kind == "clamp":
            imap = functools.partial(lambda i, cb, m: (jnp.minimum(i, m - 1), cb), cb=cb, m=m)
        else:
            imap = functools.partial(lambda i, cb: (i, cb), cb=cb)
        arrays.append(arr)
        in_specs.append(pl.BlockSpec((tr, w), imap))
    for v in vec_ins:
        s, a, w = v.shape
        imap = (lambda i: (sel(i), 0, 0)) if s == 2 else (lambda i: (0, 0, 0))
        arrays.append(v)
        in_specs.append(pl.BlockSpec((1, a, w), imap))
    out_shapes, out_specs = [], []
    for w, dt in row_outs:
        out_shapes.append(jax.ShapeDtypeStruct((n_rows, w), dt))
        out_specs.append(pl.BlockSpec((tr, w), lambda i: (i, 0)))
    for s, w in red_outs:
        out_shapes.append(jax.ShapeDtypeStruct((s, 1, w), F32))
        imap = (lambda i: (sel(i), 0, 0)) if s == 2 else (lambda i: (0, 0, 0))
        out_specs.append(pl.BlockSpec((1, 1, w), imap))
    n_ri, n_vi, n_ro, n_rd = len(row_ins), len(vec_ins), len(row_outs), len(red_outs)

    def body(*refs):
        i = pl.program_id(0)
        rows = [r[...] for r in refs[:n_ri]]
        vecs = [r[0] for r in refs[n_ri:n_ri + n_vi]]
        outs = refs[n_ri + n_vi:]
        lead = [jnp.where(i < nlt, 1.0, 0.0).astype(F32)] if want_flag else []
        res = fn(*lead, *rows, *vecs)
        if not isinstance(res, (tuple, list)):
            res = (res,)
        assert len(res) == n_ro + n_rd
        for k in range(n_ro):
            outs[k][...] = res[k].astype(outs[k].dtype)
        for k in range(n_rd):
            part = jnp.sum(res[n_ro + k].astype(F32), axis=0, keepdims=True)
            first = i == 0
            if red_outs[k][0] == 2:
                first = jnp.logical_or(first, i == nlt)
            o = outs[n_ro + k]

            @pl.when(first)
            def _():
                o[0] = part

            @pl.when(jnp.logical_not(first))
            def _():
                o[0] = o[0] + part

    res = pl.pallas_call(
        body, grid=(nt,), in_specs=in_specs, out_specs=out_specs, out_shape=out_shapes,
        compiler_params=_params("arbitrary"), name=name)(*arrays)
    return res


def _vec(v):
    v = v.astype(F32)
    if v.ndim == 1:
        v = v[None]
    return v[:, None, :]


def _mm(name, a, b, mode, out_dtype=F32):
    if mode in ("nn", "nt"):
        m, k = a.shape
        n = b.shape[1] if mode == "nn" else b.shape[0]
        tm = _largest_tile(m, 1024, 8)
        tn = _largest_tile(n, 1024, 128)
        contract = (((1,), (0,)), ((), ())) if mode == "nn" else (((1,), (1,)), ((), ()))

        def body(a_ref, b_ref, o_ref):
            o_ref[...] = lax.dot_general(a_ref[...].astype(BF16), b_ref[...].astype(BF16), contract,
                                         preferred_element_type=F32).astype(o_ref.dtype)

        b_spec = pl.BlockSpec((k, tn), lambda i, j: (0, j)) if mode == "nn" else pl.BlockSpec((tn, k), lambda i, j: (j, 0))
        return pl.pallas_call(
            body, grid=(m // tm, n // tn),
            in_specs=[pl.BlockSpec((tm, k), lambda i, j: (i, 0)), b_spec],
            out_specs=pl.BlockSpec((tm, tn), lambda i, j: (i, j)),
            out_shape=jax.ShapeDtypeStruct((m, n), out_dtype),
            compiler_params=_params("parallel", "arbitrary"), name=name)(a, b)
    assert mode == "tn"
    r, k1 = a.shape
    k2 = b.shape[1]
    tr = _largest_tile(r, 1024, 8)
    t2 = _largest_tile(k2, 1024, 128)
    nr = r // tr

    def body(a_ref, b_ref, o_ref, acc_ref):
        part = lax.dot_general(a_ref[...].astype(BF16), b_ref[...].astype(BF16), (((0,), (0,)), ((), ())),
                               preferred_element_type=F32)
        i = pl.program_id(1)

        @pl.when(i == 0)
        def _():
            acc_ref[...] = part

        @pl.when(i > 0)
        def _():
            acc_ref[...] += part

        @pl.when(i == nr - 1)
        def _():
            o_ref[...] = acc_ref[...].astype(o_ref.dtype)

    return pl.pallas_call(
        body, grid=(k2 // t2, nr),
        in_specs=[pl.BlockSpec((tr, k1), lambda j, i: (i, 0)), pl.BlockSpec((tr, t2), lambda j, i: (i, j))],
        out_specs=pl.BlockSpec((k1, t2), lambda j, i: (0, j)),
        out_shape=jax.ShapeDtypeStruct((k1, k2), out_dtype),
        scratch_shapes=[pltpu.VMEM((k1, t2), F32)],
        compiler_params=_params("parallel", "arbitrary"), name=name)(a, b)


def _exchange(name, x, bcast, sibling_only=False):
    rels = [1] if sibling_only else list(range(1, N_DEV))
    n_slot = 2 if sibling_only else N_DEV
    blk = x.shape if bcast else x.shape[1:]

    def body(x_ref, o_ref, send_sems, recv_sems, local_sem):
        mx, my, mc = lax.axis_index("x"), lax.axis_index("y"), lax.axis_index("c")
        me = mc if sibling_only else 4 * mx + 2 * my + mc
        me_dev = 4 * mx + 2 * my + mc
        mine = pltpu.make_async_copy(x_ref if bcast else x_ref.at[me_dev], o_ref.at[me], local_sem)
        mine.start()
        copies = []
        for k, r in enumerate(rels):
            px = 1 - mx if (r >> 2) & 1 else mx
            py = 1 - my if (r >> 1) & 1 else my
            pc = 1 - mc if r & 1 else mc
            src = x_ref if bcast else x_ref.at[4 * px + 2 * py + pc]
            cp = pltpu.make_async_remote_copy(
                src_ref=src, dst_ref=o_ref.at[me], send_sem=send_sems.at[k], recv_sem=recv_sems.at[k],
                device_id=(px, py, pc), device_id_type=pl.DeviceIdType.MESH)
            cp.start()
            copies.append(cp)
        for cp in copies:
            cp.wait()
        mine.wait()

    return pl.pallas_call(
        body, out_shape=jax.ShapeDtypeStruct((n_slot,) + tuple(blk), x.dtype),
        in_specs=[pl.BlockSpec(memory_space=pltpu.VMEM if sibling_only else pl.ANY)],
        out_specs=pl.BlockSpec(memory_space=pl.ANY),
        scratch_shapes=[pltpu.SemaphoreType.DMA((len(rels),)), pltpu.SemaphoreType.DMA((len(rels),)),
                        pltpu.SemaphoreType.DMA],
        name=name)(x)


def _gather_two_level(name, x):
    def body(x_ref, o_ref, send_sems, recv_sems, local_sem):
        mx, my, mc = lax.axis_index("x"), lax.axis_index("y"), lax.axis_index("c")
        me, sibling = (mx, my, mc), (mx, my, 1 - mc)
        chips = [(1 - mx, my), (mx, 1 - my), (1 - mx, 1 - my)]

        def slot(px, py, pc):
            return o_ref.at[4 * px + 2 * py + pc]

        def copy(k, block, to, src=None):
            return pltpu.make_async_remote_copy(
                src_ref=slot(*block) if src is None else src, dst_ref=slot(*block), send_sem=send_sems.at[k],
                recv_sem=recv_sems.at[k], device_id=to, device_id_type=pl.DeviceIdType.MESH)

        mine = pltpu.make_async_copy(x_ref, slot(*me), local_sem)
        mine.start()
        first = [copy(0, me, sibling, src=x_ref)]
        first += [copy(1 + j, me, (*chip, mc), src=x_ref) for j, chip in enumerate(chips)]
        for cp in first:
            cp.start()
        passed = [copy(4 + j, (*chip, mc), sibling) for j, chip in enumerate(chips)]
        for j, chip in enumerate(chips):
            copy(1 + j, (*chip, mc), me).wait_recv()
            passed[j].start()
        copy(0, sibling, me).wait_recv()
        for j, chip in enumerate(chips):
            copy(4 + j, (*chip, 1 - mc), me).wait_recv()
        for cp in first + passed:
            cp.wait_send()
        mine.wait()

    return pl.pallas_call(
        body, out_shape=jax.ShapeDtypeStruct((N_DEV,) + tuple(x.shape), x.dtype),
        in_specs=[pl.BlockSpec(memory_space=pl.ANY)], out_specs=pl.BlockSpec(memory_space=pl.ANY),
        scratch_shapes=[pltpu.SemaphoreType.DMA((N_DEV - 1,)), pltpu.SemaphoreType.DMA((N_DEV - 1,)),
                        pltpu.SemaphoreType.DMA],
        name=name)(x)


def _sum_slots(name, x):
    s, r, w = x.shape
    tr = _largest_tile(r, 256, 8)

    def body(x_ref, o_ref):
        acc = x_ref[0].astype(F32)
        for j in range(1, s):
            acc = acc + x_ref[j].astype(F32)
        o_ref[...] = acc

    return pl.pallas_call(
        body, grid=(r // tr,), in_specs=[pl.BlockSpec((s, tr, w), lambda i: (0, i, 0))],
        out_specs=pl.BlockSpec((tr, w), lambda i: (i, 0)), out_shape=jax.ShapeDtypeStruct((r, w), F32),
        compiler_params=_params("parallel"), name=name)(x)


def _sigmoid(x):
    return 1.0 / (1.0 + jnp.exp(-x))


def _silu(x):
    return x * _sigmoid(x)


def _silu_grad(x):
    s = _sigmoid(x)
    return s * (1.0 + x * (1.0 - s))


_INV_SQRT2 = 1.0 / math.sqrt(2.0)
_INV_SQRT2PI = 1.0 / math.sqrt(2.0 * math.pi)


def _gelu(x):
    return 0.5 * x * (1.0 + lax.erf(x * _INV_SQRT2))


def _gelu_grad(x):
    return 0.5 * (1.0 + lax.erf(x * _INV_SQRT2)) + x * jnp.exp(-0.5 * x * x) * _INV_SQRT2PI


def _rms_hat(x):
    r = lax.rsqrt(jnp.mean(x * x, axis=-1, keepdims=True) + NORM_EPS)
    return x * r, r


def _rms_bwd(xh, r, dxh):
    return r * (dxh - xh * jnp.mean(dxh * xh, axis=-1, keepdims=True))


def _norm_mod_fwd(name, x, g, scale, shift, n_rows, tr, n_lat):
    def fn(xt, gv, sc, sh):
        xh, _ = _rms_hat(xt)
        return (xh * gv) * (1.0 + sc) + sh

    return _rowwise(name, fn, n_rows, tr, [(x, 0, D_MODEL)], [g, scale, shift], [(D_MODEL, BF16)], [], n_lat=n_lat)[0]


def _norm_mod_bwd(name, x, g, scale, dh, dres, n_rows, tr, n_lat, prev=None):
    nlt = n_lat // tr

    def fn(flag, xt, dht, drt, *rest):
        gv, sc = rest[-2:] if prev is None else rest[1:3]
        xh, r = _rms_hat(xt)
        n = xh * gv
        dn = dht * (1.0 + sc)
        dx = _rms_bwd(xh, r, dn * gv) + flag * drt
        if prev is None:
            return dx, dn * xh, dht * n, dht
        return dx, rest[3] * dx, dn * xh, dht * n, dht, dx * rest[0]

    rows = [(x, 0, D_MODEL), (dh, 0, D_MODEL), (dres, 0, D_MODEL, "clamp", nlt)]
    vecs, row_outs, reds = [g, scale], [(D_MODEL, F32)], [(1, D_MODEL), (2, D_MODEL), (2, D_MODEL)]
    if prev is not None:
        rows.append((prev[0], 0, D_MODEL))
        vecs.append(prev[1])
        row_outs.append((D_MODEL, BF16))
        reds.append((2, D_MODEL))
    return _rowwise(name, fn, n_rows, tr, rows, vecs, row_outs, reds, n_lat=n_lat, want_flag=True)


def _s5_prep(a_re, a_im, log_dt, b_re, b_im, seg_lat, seg_ctx):
    def body(ar_ref, ai_ref, ld_ref, br_ref, bi_ref, abr_ref, abi_ref, bbr_ref, bbi_ref, alr_ref, ali_ref, acr_ref,
             aci_ref):
        lr, li = ar_ref[...], ai_ref[...]
        dt = jnp.exp(ld_ref[...])
        ldr, ldi = lr * dt, li * dt
        e = jnp.exp(ldr)
        abr, abi = e * jnp.cos(ldi), e * jnp.sin(ldi)
        abr_ref[...] = abr
        abi_ref[...] = abi
        den = lr * lr + li * li
        nr, ni = abr - 1.0, abi
        qr = (nr * lr + ni * li) / den
        qi = (ni * lr - nr * li) / den
        br, bi = br_ref[...], bi_ref[...]
        bbr_ref[...] = qr[None] * br - qi[None] * bi
        bbi_ref[...] = qr[None] * bi + qi[None] * br
        for seg, r_ref, i_ref in ((seg_lat, alr_ref, ali_ref), (seg_ctx, acr_ref, aci_ref)):
            es = jnp.exp(ldr * float(seg))
            r_ref[...] = es * jnp.cos(ldi * float(seg))
            i_ref[...] = es * jnp.sin(ldi * float(seg))

    sm = jax.ShapeDtypeStruct(a_re.shape, F32)
    big = jax.ShapeDtypeStruct(b_re.shape, F32)
    return pl.pallas_call(body, out_shape=[sm, sm, big, big, sm, sm, sm, sm], name="s5_prep")(
        a_re, a_im, log_dt, b_re, b_im)


def _s5_prep_bwd(a_re, a_im, log_dt, b_re, b_im, dabr, dabi, dbbr, dbbi):
    def body(ar_ref, ai_ref, ld_ref, br_ref, bi_ref, dabr_ref, dabi_ref, dbbr_ref, dbbi_ref,
             dar_ref, dai_ref, dld_ref, dbr_ref, dbi_ref):
        lr, li = ar_ref[...], ai_ref[...]
        dt = jnp.exp(ld_ref[...])
        ldr, ldi = lr * dt, li * dt
        e = jnp.exp(ldr)
        abr, abi = e * jnp.cos(ldi), e * jnp.sin(ldi)
        den = lr * lr + li * li
        nr, ni = abr - 1.0, abi
        qr = (nr * lr + ni * li) / den
        qi = (ni * lr - nr * li) / den
        br, bi = br_ref[...], bi_ref[...]
        gbr, gbi = dbbr_ref[...], dbbi_ref[...]
        dbr_ref[...] = gbr * qr[None] + gbi * qi[None]
        dbi_ref[...] = gbi * qr[None] - gbr * qi[None]
        dqr = jnp.sum(gbr * br + gbi * bi, axis=0)
        dqi = jnp.sum(gbi * br - gbr * bi, axis=0)
        dnr = (dqr * lr - dqi * li) / den
        dni = (dqr * li + dqi * lr) / den
        dlr_q = (dqr * (nr - 2.0 * lr * qr) + dqi * (ni - 2.0 * lr * qi)) / den
        dli_q = (dqr * (ni - 2.0 * li * qr) + dqi * (-nr - 2.0 * li * qi)) / den
        gar = dabr_ref[...] + dnr
        gai = dabi_ref[...] + dni
        dldr = gar * abr + gai * abi
        dldi = gai * abr - gar * abi
        dar_ref[...] = dldr * dt + dlr_q
        dai_ref[...] = dldi * dt + dli_q
        ddt = jnp.sum(dldr * lr + dldi * li, axis=1, keepdims=True)
        dld_ref[...] = ddt * dt

    sm = jax.ShapeDtypeStruct(a_re.shape, F32)
    big = jax.ShapeDtypeStruct(b_re.shape, F32)
    return pl.pallas_call(body, out_shape=[sm, sm, jax.ShapeDtypeStruct(log_dt.shape, F32), big, big],
                          name="s5_prep_bwd")(a_re, a_im, log_dt, b_re, b_im, dabr, dabi, dbbr, dbbi)


def _slab_cols(v):
    return v.reshape(N_SLAB, 1, HALF_W)


def _slab_pair(vr, vi):
    return jnp.concatenate([_slab_cols(vr), _slab_cols(vi)], axis=-1)


def _slab_in_matrix(bbr, bbi):
    eye = jnp.eye(SLAB_GROUPS, dtype=F32)

    def one(b):
        b = b.reshape(N_SLAB, SLAB_GROUPS, SSM_STATE, SSM_GROUP)
        m = jnp.einsum("sgph,gk->sghkp", b, eye)
        return m.reshape(N_SLAB, SLAB_W, HALF_W)

    return jnp.concatenate([one(bbr), one(bbi)], axis=-1)


def _slab_out_matrix(cr, ci):
    eye = jnp.eye(SLAB_GROUPS, dtype=F32)

    def one(c):
        c = c.reshape(N_SLAB, SLAB_GROUPS, SSM_GROUP, SSM_STATE)
        m = jnp.einsum("sghp,gk->skpgh", c, eye)
        return m.reshape(N_SLAB, HALF_W, SLAB_W)

    return jnp.concatenate([one(cr), one(-ci)], axis=1)


def _slab_diag(m):
    m = m.reshape(N_SLAB, SLAB_GROUPS, SSM_GROUP, 2, SLAB_GROUPS, SSM_STATE)
    d = jnp.stack([m[:, g, :, :, g, :] for g in range(SLAB_GROUPS)], axis=1)
    return d.transpose(3, 0, 1, 2, 4).reshape(2, SSM_GROUPS, SSM_GROUP, SSM_STATE)


def _cmul(ar, ai, xr, xi, conj):
    if conj:
        return ar * xr + ai * xi, ar * xi - ai * xr
    return ar * xr - ai * xi, ar * xi + ai * xr


def _s5_pow_table(name, abar, seg, falling, conj):
    assert seg >= 8 and seg & (seg - 1) == 0

    def body(a_ref, o_ref, t_ref):
        ar, ai = a_ref[0, :, :HALF_W], a_ref[0, :, HALF_W:]
        if conj:
            ai = -ai
        rr, ri = [jnp.ones_like(ar)], [jnp.zeros_like(ai)]
        for _ in range(7):
            pr, pi = _cmul(ar, ai, rr[-1], ri[-1], False)
            rr.append(pr)
            ri.append(pi)
        sr, si = _cmul(ar, ai, rr[-1], ri[-1], False)
        if falling:
            rr, ri = rr[::-1], ri[::-1]
        first = slice(seg - 8, seg) if falling else slice(0, 8)
        t_ref[first, :HALF_W] = jnp.concatenate(rr, axis=0)
        t_ref[first, HALF_W:] = jnp.concatenate(ri, axis=0)
        size = 8
        while size < seg:
            src = slice(seg - size, seg) if falling else slice(0, size)
            dst = slice(seg - 2 * size, seg - size) if falling else slice(size, 2 * size)
            pr, pi = _cmul(sr, si, t_ref[src, :HALF_W], t_ref[src, HALF_W:], False)
            t_ref[dst, :HALF_W] = pr
            t_ref[dst, HALF_W:] = pi
            sr, si = _cmul(sr, si, sr, si, False)
            size *= 2
        o_ref[0] = t_ref[...].astype(BF16)

    return pl.pallas_call(
        body, grid=(N_SLAB,), in_specs=[pl.BlockSpec((1, 1, STATE_W), lambda s: (s, 0, 0))],
        out_specs=pl.BlockSpec((1, seg, STATE_W), lambda s: (s, 0, 0)),
        out_shape=jax.ShapeDtypeStruct((N_SLAB, seg, STATE_W), BF16),
        scratch_shapes=[pltpu.VMEM((seg, STATE_W), F32)], compiler_params=_params("parallel"), name=name)(abar)


def _s5_ends(name, x, n_rows, row0, table, m_mat):
    seg = n_rows // N_SEG
    rb = row0 // n_rows
    tn = (((0,), (0,)), ((), ()))

    def body(x_ref, t_ref, m_ref, z_ref):
        mr, mi = m_ref[0, :, :HALF_W], m_ref[0, :, HALF_W:]
        for j in range(N_SEG):
            t = lax.dot_general(x_ref[j * seg:(j + 1) * seg, :].astype(BF16), t_ref[0], tn,
                                preferred_element_type=F32)
            tr_, ti_ = t[:, :HALF_W], t[:, HALF_W:]
            z_ref[0, j:j + 1, :HALF_W] = jnp.sum(mr * tr_ - mi * ti_, axis=0, keepdims=True)
            z_ref[0, j:j + 1, HALF_W:] = jnp.sum(mr * ti_ + mi * tr_, axis=0, keepdims=True)

    return pl.pallas_call(
        body, grid=(N_SLAB,),
        in_specs=[pl.BlockSpec((n_rows, SLAB_W), lambda s: (rb, s)),
                  pl.BlockSpec((1, seg, STATE_W), lambda s: (s, 0, 0)),
                  pl.BlockSpec((1, SLAB_W, STATE_W), lambda s: (s, 0, 0))],
        out_specs=pl.BlockSpec((1, N_SEG, STATE_W), lambda s: (s, 0, 0)),
        out_shape=jax.ShapeDtypeStruct((N_SLAB, N_SEG, STATE_W), F32),
        compiler_params=_params("parallel"), name=name)(x, table, m_mat)


def _s5_carry(name, z, a_seg, init, descending, conj):
    order = list(range(N_SEG - 1, -1, -1)) if descending else list(range(N_SEG))

    def body(z_ref, a_ref, i_ref, e_ref, o_ref):
        ar, ai = a_ref[:, :HALF_W], a_ref[:, HALF_W:]
        cr, ci = i_ref[:, :HALF_W], i_ref[:, HALF_W:]
        for j in order:
            e_ref[j, :, :HALF_W] = cr
            e_ref[j, :, HALF_W:] = ci
            pr, pi = _cmul(ar, ai, cr, ci, conj)
            cr = pr + z_ref[j, :, :HALF_W]
            ci = pi + z_ref[j, :, HALF_W:]
        o_ref[:, :HALF_W] = cr
        o_ref[:, HALF_W:] = ci

    return pl.pallas_call(body, out_shape=[jax.ShapeDtypeStruct(z.shape, F32), jax.ShapeDtypeStruct(init.shape, F32)],
                          name=name)(z, a_seg, init)


def _seg_major(v):
    return jnp.transpose(v, (1, 0, 2))


def _s5_scan(name, u, n_rows, row0, b_mat, c_mat, abar, h_in, descending, y_alias=None, y_rows=None):
    seg = n_rows // N_SEG
    ta = min(32, seg)
    nk = seg // ta
    assert seg * N_SEG == n_rows and nk * ta == seg and row0 % n_rows == 0 and ta % 8 == 0
    rb = row0 // n_rows
    tile = ta * N_SEG

    def body(*refs):
        u_ref, b_ref, c_ref, a_ref, hin_ref = refs[:5]
        y_ref, hch_ref, st_ref, up_ref, h_ref = refs[-5:]
        k = pl.program_id(1)
        kk = nk - 1 - k if descending else k
        a0 = kk * ta

        @pl.when(k == 0)
        def _():
            st_ref[...] = hin_ref[0]

        hch_ref[0, 0] = st_ref[...]
        for al in range(ta):
            up_ref[al * N_SEG:(al + 1) * N_SEG, :] = u_ref[pl.ds(a0 + al, N_SEG, stride=seg), :]
        h_ref[...] = jnp.dot(up_ref[...].astype(BF16), b_ref[0], preferred_element_type=F32)
        ar = jnp.broadcast_to(a_ref[0, :, :HALF_W], (N_SEG, HALF_W))
        ai = jnp.broadcast_to(a_ref[0, :, HALF_W:], (N_SEG, HALF_W))

        def step(i, carry):
            hr, hi = carry
            al = ta - 1 - i if descending else i
            row = pl.multiple_of(al * N_SEG, N_SEG)
            pr, pi = _cmul(ar, ai, hr, hi, False)
            hr = pr + h_ref[pl.ds(row, N_SEG), :HALF_W]
            hi = pi + h_ref[pl.ds(row, N_SEG), HALF_W:]
            h_ref[pl.ds(row, N_SEG), :HALF_W] = hr
            h_ref[pl.ds(row, N_SEG), HALF_W:] = hi
            return hr, hi

        hr, hi = lax.fori_loop(0, ta, step, (st_ref[:, :HALF_W], st_ref[:, HALF_W:]), unroll=True)
        st_ref[:, :HALF_W] = hr
        st_ref[:, HALF_W:] = hi
        yt = jnp.dot(h_ref[...].astype(BF16), c_ref[0], preferred_element_type=F32)
        for al in range(ta):
            y_ref[pl.ds(a0 + al, N_SEG, stride=seg), :] = yt[al * N_SEG:(al + 1) * N_SEG, :]

    u_spec = pl.BlockSpec((n_rows, SLAB_W), lambda s, k: (rb, s))
    b_spec = pl.BlockSpec((1, SLAB_W, STATE_W), lambda s, k: (s, 0, 0))
    c_spec = pl.BlockSpec((1, STATE_W, SLAB_W), lambda s, k: (s, 0, 0))
    a_spec = pl.BlockSpec((1, 1, STATE_W), lambda s, k: (s, 0, 0))
    st_spec = pl.BlockSpec((1, N_SEG, STATE_W), lambda s, k: (s, 0, 0))
    scratch = [pltpu.VMEM((N_SEG, STATE_W), F32), pltpu.VMEM((tile, SLAB_W), F32), pltpu.VMEM((tile, STATE_W), F32)]
    kmap = (lambda s, k: (s, nk - 1 - k, 0, 0)) if descending else (lambda s, k: (s, k, 0, 0))
    out_specs = [u_spec, pl.BlockSpec((1, 1, N_SEG, STATE_W), kmap)]
    out_shape = [jax.ShapeDtypeStruct((y_rows, D_MODEL), F32), jax.ShapeDtypeStruct((N_SLAB, nk, N_SEG, STATE_W), F32)]
    in_specs = [u_spec, b_spec, c_spec, a_spec, st_spec]
    args = [u, b_mat, c_mat, abar, h_in]
    aliases = {}
    if y_alias is not None:
        in_specs.append(pl.BlockSpec(memory_space=pl.ANY))
        args.append(y_alias)
        aliases = {5: 0}
    return pl.pallas_call(
        body, grid=(N_SLAB, nk), in_specs=in_specs, out_specs=out_specs, out_shape=out_shape, scratch_shapes=scratch,
        input_output_aliases=aliases, compiler_params=_params("parallel", "arbitrary"), name=name)(*args)


def _s5_scan_bwd(name, u, dy, n_rows, row0, b_mat, bt_mat, ct_mat, abar, h_chunks, g_in, descending,
                 du_alias=None, du_rows=None):
    seg = n_rows // N_SEG
    ta = min(32, seg)
    nk = seg // ta
    rb = row0 // n_rows
    tile = ta * N_SEG
    g_desc = not descending

    def body(*refs):
        u_ref, dy_ref, b_ref, bt_ref, ct_ref, a_ref, hch_ref, gin_ref = refs[:8]
        du_ref, db_ref, dc_ref, da_ref, st_ref, up_ref, dyp_ref, h_ref, g_ref = refs[-9:]
        k = pl.program_id(1)
        kk = nk - 1 - k if g_desc else k
        a0 = kk * ta
        ar = jnp.broadcast_to(a_ref[0, :, :HALF_W], (N_SEG, HALF_W))
        ai = jnp.broadcast_to(a_ref[0, :, HALF_W:], (N_SEG, HALF_W))

        @pl.when(k == 0)
        def _():
            st_ref[...] = gin_ref[0]

        for al in range(ta):
            dyp_ref[al * N_SEG:(al + 1) * N_SEG, :] = dy_ref[pl.ds(a0 + al, N_SEG, stride=seg), :]
            up_ref[al * N_SEG:(al + 1) * N_SEG, :] = u_ref[pl.ds(a0 + al, N_SEG, stride=seg), :]
        g_ref[...] = jnp.dot(dyp_ref[...].astype(BF16), ct_ref[0], preferred_element_type=F32)
        h_ref[...] = jnp.dot(up_ref[...].astype(BF16), b_ref[0], preferred_element_type=F32)
        h0r, h0i = hch_ref[0, 0, :, :HALF_W], hch_ref[0, 0, :, HALF_W:]

        def hstep(i, carry):
            hr, hi = carry
            al = ta - 1 - i if descending else i
            row = pl.multiple_of(al * N_SEG, N_SEG)
            pr, pi = _cmul(ar, ai, hr, hi, False)
            hr = pr + h_ref[pl.ds(row, N_SEG), :HALF_W]
            hi = pi + h_ref[pl.ds(row, N_SEG), HALF_W:]
            h_ref[pl.ds(row, N_SEG), :HALF_W] = hr
            h_ref[pl.ds(row, N_SEG), HALF_W:] = hi
            return hr, hi

        lax.fori_loop(0, ta, hstep, (h0r, h0i), unroll=True)

        def gstep(i, carry):
            gr, gi = carry
            al = ta - 1 - i if g_desc else i
            row = pl.multiple_of(al * N_SEG, N_SEG)
            pr, pi = _cmul(ar, ai, gr, gi, True)
            gr = pr + g_ref[pl.ds(row, N_SEG), :HALF_W]
            gi = pi + g_ref[pl.ds(row, N_SEG), HALF_W:]
            g_ref[pl.ds(row, N_SEG), :HALF_W] = gr
            g_ref[pl.ds(row, N_SEG), HALF_W:] = gi
            return gr, gi

        gr, gi = lax.fori_loop(0, ta, gstep, (st_ref[:, :HALF_W], st_ref[:, HALF_W:]), unroll=True)
        st_ref[:, :HALF_W] = gr
        st_ref[:, HALF_W:] = gi

        gb = g_ref[...].astype(BF16)
        dut = jnp.dot(gb, bt_ref[0], preferred_element_type=F32)
        for al in range(ta):
            du_ref[pl.ds(a0 + al, N_SEG, stride=seg), :] = dut[al * N_SEG:(al + 1) * N_SEG, :]
        tn = (((0,), (0,)), ((), ()))
        dbp = lax.dot_general(up_ref[...].astype(BF16), gb, tn, preferred_element_type=F32)
        dcp = lax.dot_general(dyp_ref[...].astype(BF16), h_ref[...].astype(BF16), tn, preferred_element_type=F32)
        inner = (ta - 1) * N_SEG
        if descending:
            g_in_r, g_in_i = g_ref[0:inner, :HALF_W], g_ref[0:inner, HALF_W:]
            p_in_r, p_in_i = h_ref[N_SEG:tile, :HALF_W], h_ref[N_SEG:tile, HALF_W:]
            g_ed_r, g_ed_i = g_ref[inner:tile, :HALF_W], g_ref[inner:tile, HALF_W:]
        else:
            g_in_r, g_in_i = g_ref[N_SEG:tile, :HALF_W], g_ref[N_SEG:tile, HALF_W:]
            p_in_r, p_in_i = h_ref[0:inner, :HALF_W], h_ref[0:inner, HALF_W:]
            g_ed_r, g_ed_i = g_ref[0:N_SEG, :HALF_W], g_ref[0:N_SEG, HALF_W:]
        dar = g_ed_r * h0r + g_ed_i * h0i
        dai = g_ed_i * h0r - g_ed_r * h0i
        if ta > 1:
            dar = dar + jnp.sum((g_in_r * p_in_r + g_in_i * p_in_i).reshape(ta - 1, N_SEG, HALF_W), axis=0)
            dai = dai + jnp.sum((g_in_i * p_in_r - g_in_r * p_in_i).reshape(ta - 1, N_SEG, HALF_W), axis=0)

        @pl.when(k == 0)
        def _():
            db_ref[0] = dbp
            dc_ref[0] = dcp
            da_ref[0, :, :HALF_W] = dar
            da_ref[0, :, HALF_W:] = dai

        @pl.when(k > 0)
        def _():
            db_ref[0] += dbp
            dc_ref[0] += dcp
            da_ref[0, :, :HALF_W] += dar
            da_ref[0, :, HALF_W:] += dai

    u_spec = pl.BlockSpec((n_rows, SLAB_W), lambda s, k: (rb, s))
    m_spec = pl.BlockSpec((1, SLAB_W, STATE_W), lambda s, k: (s, 0, 0))
    mt_spec = pl.BlockSpec((1, STATE_W, SLAB_W), lambda s, k: (s, 0, 0))
    a_spec = pl.BlockSpec((1, 1, STATE_W), lambda s, k: (s, 0, 0))
    st_spec = pl.BlockSpec((1, N_SEG, STATE_W), lambda s, k: (s, 0, 0))
    st_shape = jax.ShapeDtypeStruct((N_SLAB, N_SEG, STATE_W), F32)
    kmap = (lambda s, k: (s, nk - 1 - k, 0, 0)) if g_desc else (lambda s, k: (s, k, 0, 0))
    in_specs = [u_spec, u_spec, m_spec, mt_spec, m_spec, a_spec, pl.BlockSpec((1, 1, N_SEG, STATE_W), kmap), st_spec]
    args = [u, dy, b_mat, bt_mat, ct_mat, abar, h_chunks, g_in]
    aliases = {}
    if du_alias is not None:
        in_specs.append(pl.BlockSpec(memory_space=pl.ANY))
        args.append(du_alias)
        aliases = {8: 0}
    acc_shape = jax.ShapeDtypeStruct((N_SLAB, SLAB_W, STATE_W), F32)
    out_specs = [u_spec, m_spec, m_spec, st_spec]
    out_shape = [jax.ShapeDtypeStruct((du_rows, D_MODEL), F32), acc_shape, acc_shape, st_shape]
    scratch = [pltpu.VMEM((N_SEG, STATE_W), F32), pltpu.VMEM((tile, SLAB_W), F32), pltpu.VMEM((tile, SLAB_W), F32),
               pltpu.VMEM((tile, STATE_W), F32), pltpu.VMEM((tile, STATE_W), F32)]
    return pl.pallas_call(
        body, grid=(N_SLAB, nk), in_specs=in_specs, out_specs=out_specs, out_shape=out_shape, scratch_shapes=scratch,
        input_output_aliases=aliases, compiler_params=_params("parallel", "arbitrary"), name=name)(*args)


ROPE_HALF = HEAD_DIM // 4
TABLE_W = 2 * HEAD_DIM
Q_SCALE = 1.0 / math.sqrt(HEAD_DIM)
HEADS_PER_BLOCK = 2 * KV_REP
Q_BLOCK_W = HEADS_PER_BLOCK * HEAD_DIM


def _rope_tables(n_lat, n_ctx):
    rows = n_lat // GRID_W
    freqs = ROPE_THETA ** (-jnp.arange(ROPE_HALF, dtype=F32) / ROPE_HALF)
    ang_r = jnp.arange(rows, dtype=F32)[:, None] * freqs[None]
    ang_c = jnp.arange(GRID_W, dtype=F32)[:, None] * freqs[None]
    by_row = lambda v: jnp.repeat(v, GRID_W, axis=0)
    by_col = lambda v: jnp.tile(v, (rows, 1))
    cos = jnp.concatenate([by_row(jnp.cos(ang_r)), by_row(jnp.cos(ang_r)), by_col(jnp.cos(ang_c)), by_col(jnp.cos(ang_c))] * 2,
                          axis=1)
    sin = jnp.concatenate([by_row(jnp.sin(ang_r)), by_row(jnp.sin(ang_r)), by_col(jnp.sin(ang_c)), by_col(jnp.sin(ang_c))] * 2,
                          axis=1)
    cos = jnp.concatenate([cos, jnp.ones((n_ctx, TABLE_W), F32)], axis=0)
    sin = jnp.concatenate([sin, jnp.zeros((n_ctx, TABLE_W), F32)], axis=0)
    return cos, sin


def _rot_half(v):
    w = v.shape[1]
    ahead = pltpu.roll(v, w - ROPE_HALF, axis=1)
    behind = pltpu.roll(v, ROPE_HALF, axis=1)
    lane = lax.broadcasted_iota(jnp.int32, v.shape, 1)
    return jnp.where((lane % (2 * ROPE_HALF)) < ROPE_HALF, -ahead, behind)


def _head_mean(v, sel, selt):
    m = jnp.dot(v, sel, precision=lax.Precision.HIGH, preferred_element_type=F32) * (1.0 / HEAD_DIM)
    return jnp.dot(m, selt, precision=lax.Precision.HIGH, preferred_element_type=F32)


def _head_selectors(n_heads):
    sel = jnp.repeat(jnp.eye(n_heads, dtype=F32), HEAD_DIM, axis=0)
    return sel[None], sel.T[None]


def _head_norm(x, sel, selt):
    r = lax.rsqrt(_head_mean(x * x, sel, selt) + NORM_EPS)
    return x * r, r


def _qk_prep(proj, qn, kn, cos, sin, n, tr):
    qw, kw = _vec(jnp.tile(qn, N_Q_HEADS)), _vec(jnp.tile(kn, N_KV_HEADS))
    sq, sqt = _head_selectors(N_Q_HEADS)
    sk, skt = _head_selectors(N_KV_HEADS)

    def fn(qr, kvr, ct, st, qwv, kwv, s16, s16t, s4, s4t):
        outs = []
        for x, wv, sel, selt, scale in ((qr, qwv, s16, s16t, Q_SCALE), (kvr[:, :KV_W], kwv, s4, s4t, 1.0)):
            reps = x.shape[1] // TABLE_W
            cw, sw = jnp.tile(ct, (1, reps)), jnp.tile(st, (1, reps))
            xh, _ = _head_norm(x, sel, selt)
            nrm = xh * wv
            outs.append((nrm * cw + _rot_half(nrm) * sw) * scale)
        return outs[0], outs[1], kvr[:, KV_W:]

    return _rowwise("l1_qk_prep", fn, n, tr,
                    [(proj, 0, ATTN_W), (proj, 2 * ATTN_W // (2 * KV_W), 2 * KV_W), (cos, 0, TABLE_W), (sin, 0, TABLE_W)],
                    [qw, kw, sq, sqt, sk, skt], [(ATTN_W, BF16), (KV_W, BF16), (KV_W, BF16)], [])


def _qk_prep_bwd(proj, qn, kn, cos, sin, dq, dz, dk, dv, n, n_lat, tr):
    qw, kw = _vec(jnp.tile(qn, N_Q_HEADS)), _vec(jnp.tile(kn, N_KV_HEADS))
    sq, sqt = _head_selectors(N_Q_HEADS)
    sk, skt = _head_selectors(N_KV_HEADS)
    nlt = n_lat // tr

    def fn(flag, qr, kvr, ct, st, dqt, dzt, dkt, dvt, qwv, kwv, s16, s16t, s4, s4t):
        dxs, dws = [], []
        for x, dy, wv, sel, selt in ((qr, dqt * (flag * Q_SCALE), qwv, s16, s16t), (kvr[:, :KV_W], dkt, kwv, s4, s4t)):
            reps = x.shape[1] // TABLE_W
            cw, sw = jnp.tile(ct, (1, reps)), jnp.tile(st, (1, reps))
            xh, r = _head_norm(x, sel, selt)
            dn = dy * cw - _rot_half(dy * sw)
            dxh = dn * wv
            dxs.append(r * (dxh - xh * _head_mean(dxh * xh, sel, selt)))
            dws.append(dn * xh)
        return jnp.concatenate([dxs[0], dzt * flag, dxs[1], dvt], axis=1), dws[0], dws[1]

    dproj, dqw, dkw = _rowwise(
        "l1_qk_prep_bwd", fn, n, tr,
        [(proj, 0, ATTN_W), (proj, 2 * ATTN_W // (2 * KV_W), 2 * KV_W), (cos, 0, TABLE_W), (sin, 0, TABLE_W),
         (dq, 0, ATTN_W, "clamp", nlt), (dz, 0, ATTN_W, "clamp", nlt), (dk, 0, KV_W), (dv, 0, KV_W)],
        [qw, kw, sq, sqt, sk, skt], [(2 * ATTN_W + 2 * KV_W, BF16)], [(1, ATTN_W), (1, KV_W)], n_lat=n_lat, want_flag=True)
    return dproj, dqw.reshape(N_Q_HEADS, HEAD_DIM).sum(0)[None], dkw.reshape(N_KV_HEADS, HEAD_DIM).sum(0)[None]


NT = (((1,), (1,)), ((), ()))


def _attn_fwd(q, k, v, t, tq, tk):
    n = k.shape[0]
    nkc = n // tk

    ts = _largest_tile(tq, 256, LANES)
    items = [(sub, j) for sub in range(tq // ts) for j in range(HEADS_PER_BLOCK)]

    def body(q_ref, k_ref, v_ref, o_ref, lse_ref, s_ref, m_ref, l_ref, acc_ref):
        def lanes(j):
            g = j // KV_REP
            return slice(j * HEAD_DIM, (j + 1) * HEAD_DIM), slice(g * HEAD_DIM, (g + 1) * HEAD_DIM)

        for idx in range(len(items) + 1):
            nxt = items[idx] if idx < len(items) else None
            cur = items[idx - 1] if idx > 0 else None
            sn, sc = idx % 2, (idx - 1) % 2
            if nxt is not None:
                rows_n = slice(nxt[0] * ts, (nxt[0] + 1) * ts)
                ql_n, kl_n = lanes(nxt[1])
                qv = q_ref[rows_n, ql_n]
                m_ref[sn] = jnp.full((ts, LANES), -jnp.inf, F32)
            if cur is not None:
                rows_c = slice(cur[0] * ts, (cur[0] + 1) * ts)
                ql_c, kl_c = lanes(cur[1])
                m_row = jnp.max(m_ref[sc], axis=-1, keepdims=True)
                l_ref[...] = jnp.zeros(l_ref.shape, F32)
                acc_ref[...] = jnp.zeros(acc_ref.shape, F32)

            def sweep(kc, c):
                off = pl.multiple_of(kc * tk, tk)
                if nxt is not None:
                    s = lax.dot_general(qv, k_ref[pl.ds(off, tk), kl_n], NT, preferred_element_type=F32)
                    s_ref[sn, :, pl.ds(off, tk)] = s
                    m = m_ref[sn]
                    for cb in range(tk // LANES):
                        m = jnp.maximum(m, s[:, cb * LANES:(cb + 1) * LANES])
                    m_ref[sn] = m
                if cur is not None:
                    p = jnp.exp(s_ref[sc, :, pl.ds(off, tk)] - m_row)
                    lsum = l_ref[...]
                    for cb in range(tk // LANES):
                        lsum = lsum + p[:, cb * LANES:(cb + 1) * LANES]
                    l_ref[...] = lsum
                    acc_ref[...] += jnp.dot(p.astype(BF16), v_ref[pl.ds(off, tk), kl_c], preferred_element_type=F32)
                return c

            lax.fori_loop(0, nkc, sweep, 0, unroll=True)
            if cur is not None:
                l_row = jnp.sum(l_ref[...], axis=-1, keepdims=True)
                o_ref[rows_c, ql_c] = acc_ref[...] / l_row
                lse_ref[0, rows_c, cur[1]:cur[1] + 1] = m_row + jnp.log(l_row)

    nb = ATTN_W // Q_BLOCK_W
    kspec = pl.BlockSpec((n, LANES), lambda b, i: (0, b))
    return pl.pallas_call(
        body, grid=(nb, t // tq),
        in_specs=[pl.BlockSpec((tq, Q_BLOCK_W), lambda b, i: (i, b)), kspec, kspec],
        out_specs=[pl.BlockSpec((tq, Q_BLOCK_W), lambda b, i: (i, b)),
                   pl.BlockSpec((1, tq, HEADS_PER_BLOCK), lambda b, i: (b, i, 0))],
        out_shape=[jax.ShapeDtypeStruct((t, ATTN_W), F32), jax.ShapeDtypeStruct((nb, t, HEADS_PER_BLOCK), F32)],
        scratch_shapes=[pltpu.VMEM((2, ts, n), F32), pltpu.VMEM((2, ts, LANES), F32), pltpu.VMEM((ts, LANES), F32),
                        pltpu.VMEM((ts, HEAD_DIM), F32)],
        compiler_params=_params("parallel", "parallel"), name="attn_fwd")(q, k, v)


def _attn_bwd(q, k, v, do, o, lse, t, tq, tk):
    n = k.shape[0]
    nkc = n // tk
    tn = (((0,), (0,)), ((), ()))

    def body(q_ref, k_ref, v_ref, do_ref, o_ref, lse_ref, dq_ref, dk_ref, dv_ref, acc_ref):
        @pl.when(pl.program_id(1) == 0)
        def _():
            dk_ref[...] = jnp.zeros(dk_ref.shape, F32)
            dv_ref[...] = jnp.zeros(dv_ref.shape, F32)

        for j0 in range(0, HEADS_PER_BLOCK, 2):
            kl = slice((j0 // KV_REP) * HEAD_DIM, (j0 // KV_REP + 1) * HEAD_DIM)
            heads = []
            for a in range(2):
                j = j0 + a
                ql = slice(j * HEAD_DIM, (j + 1) * HEAD_DIM)
                qv, dov = q_ref[:, ql], do_ref[:, ql]
                dl_v = jnp.sum(dov.astype(F32) * o_ref[:, ql], axis=-1, keepdims=True)
                heads.append((ql, qv, dov, dl_v, lse_ref[0, :, j:j + 1]))
                acc_ref[a] = jnp.zeros((tq, HEAD_DIM), F32)

            def step(kc, c):
                off = pl.multiple_of(kc * tk, tk)
                kt = k_ref[pl.ds(off, tk), kl]
                vt = v_ref[pl.ds(off, tk), kl]
                dv_part, dk_part = None, None
                for a, (_, qv, dov, dl_v, lse_v) in enumerate(heads):
                    s = lax.dot_general(qv, kt, NT, preferred_element_type=F32)
                    p = jnp.exp(s - lse_v)
                    dp = lax.dot_general(dov, vt, NT, preferred_element_type=F32)
                    ds = (p * (dp - dl_v)).astype(BF16)
                    acc_ref[a] += jnp.dot(ds, kt, preferred_element_type=F32)
                    dvp = lax.dot_general(p.astype(BF16), dov, tn, preferred_element_type=F32)
                    dkp = lax.dot_general(ds, qv, tn, preferred_element_type=F32)
                    dv_part = dvp if dv_part is None else dv_part + dvp
                    dk_part = dkp if dk_part is None else dk_part + dkp
                dv_ref[pl.ds(off, tk), kl] += dv_part
                dk_ref[pl.ds(off, tk), kl] += dk_part
                return c

            lax.fori_loop(0, nkc, step, 0, unroll=2)
            for a, h in enumerate(heads):
                dq_ref[:, h[0]] = acc_ref[a]

    nb = ATTN_W // Q_BLOCK_W
    qspec = pl.BlockSpec((tq, Q_BLOCK_W), lambda b, i: (i, b))
    kspec = pl.BlockSpec((n, LANES), lambda b, i: (0, b))
    cspec = pl.BlockSpec((1, tq, HEADS_PER_BLOCK), lambda b, i: (b, i, 0))
    return pl.pallas_call(
        body, grid=(nb, t // tq), in_specs=[qspec, kspec, kspec, qspec, qspec, cspec], out_specs=[qspec, kspec, kspec],
        out_shape=[jax.ShapeDtypeStruct((t, ATTN_W), F32), jax.ShapeDtypeStruct((n, KV_W), F32),
                   jax.ShapeDtypeStruct((n, KV_W), F32)],
        scratch_shapes=[pltpu.VMEM((2, tq, HEAD_DIM), F32)],
        compiler_params=_params("parallel", "arbitrary"), name="attn_bwd")(q, k, v, do, o, lse)


def _s5_system(p, n_lat, n_ctx):
    two_g = 2 * SSM_GROUPS
    a_re = p["ssm_a_re"].reshape(two_g, SSM_STATE)
    a_im = p["ssm_a_im"].reshape(two_g, SSM_STATE)
    log_dt = p["ssm_log_dt"].reshape(two_g, 1)
    b_re = p["ssm_b_re"].reshape(two_g, SSM_STATE, SSM_GROUP).transpose(2, 0, 1)
    b_im = p["ssm_b_im"].reshape(two_g, SSM_STATE, SSM_GROUP).transpose(2, 0, 1)
    raw = (a_re, a_im, log_dt, b_re, b_im)
    abr, abi, bbr, bbi, alr, ali, acr, aci = _s5_prep(*raw, n_lat // N_SEG, n_ctx // N_SEG)
    dirs = []
    for d in range(2):
        g = slice(d * SSM_GROUPS, (d + 1) * SSM_GROUPS)
        b_mat = _slab_in_matrix(bbr[:, g].transpose(1, 2, 0), bbi[:, g].transpose(1, 2, 0))
        c_mat = _slab_out_matrix(p["ssm_c_re"][0, d], p["ssm_c_im"][0, d])
        abar = _slab_pair(abr[g], abi[g])
        tables = {}
        for part, seg in (("lat", n_lat // N_SEG), ("ctx", n_ctx // N_SEG)):
            tables["h_" + part] = _s5_pow_table(f"s5_pow_h{d}_{part}", abar, seg, d == 0, False)
            tables["g_" + part] = _s5_pow_table(f"s5_pow_g{d}_{part}", abar, seg, d == 1, True)
        dirs.append(dict(
            b=b_mat.astype(BF16), bt=b_mat.transpose(0, 2, 1).astype(BF16), b32=b_mat,
            c=c_mat.astype(BF16), ct=c_mat.transpose(0, 2, 1).astype(BF16), ct32=c_mat.transpose(0, 2, 1),
            abar=abar, a_lat=_slab_pair(alr[g], ali[g])[:, 0], a_ctx=_slab_pair(acr[g], aci[g])[:, 0], **tables))
    return raw, dirs


def _s5_forward(proj, dirs, n_lat, n_ctx):
    n = n_lat + n_ctx
    zero_c = jnp.zeros((N_SLAB, STATE_W), F32)
    ys, saved = [], []
    for d, s in enumerate(dirs):
        desc = d == 1
        tag = f"s5f{d}"
        zc = _s5_ends(tag + "_ctx_ends", proj, n_ctx, n_lat, s["h_ctx"], s["b32"])
        ent_c, h0 = _s5_carry(tag + "_ctx_carry", _seg_major(zc), s["a_ctx"], zero_c, desc, False)
        y, hch_c = _s5_scan(tag + "_ctx", proj, n_ctx, n_lat, s["b"], s["c"], s["abar"], _seg_major(ent_c), desc,
                            y_rows=n)
        zl = _s5_ends(tag + "_lat_ends", proj, n_lat, 0, s["h_lat"], s["b32"])
        ent_l, _ = _s5_carry(tag + "_lat_carry", _seg_major(zl), s["a_lat"], h0, desc, False)
        y, hch_l = _s5_scan(tag + "_lat", proj, n_lat, 0, s["b"], s["c"], s["abar"], _seg_major(ent_l), desc,
                            y_alias=y, y_rows=n)
        ys.append(y)
        saved.append((hch_l, hch_c))
    return ys, saved


def _s5_backward(proj, dy, dirs, saved, n_lat, n_ctx):
    n = n_lat + n_ctx
    zero_c = jnp.zeros((N_SLAB, STATE_W), F32)
    out = []
    for d, s in enumerate(dirs):
        desc = d == 1
        tag = f"s5b{d}"
        hch_l, hch_c = saved[d]
        gl = _s5_ends(tag + "_lat_ends", dy, n_lat, 0, s["g_lat"], s["ct32"])
        ent_l, g0 = _s5_carry(tag + "_lat_carry", _seg_major(gl), s["a_lat"], zero_c, not desc, True)
        du, db_l, dc_l, da_l = _s5_scan_bwd(tag + "_lat", proj, dy, n_lat, 0, s["b"], s["bt"], s["ct"], s["abar"],
                                            hch_l, _seg_major(ent_l), desc, du_rows=n)
        gc = _s5_ends(tag + "_ctx_ends", dy, n_ctx, n_lat, s["g_ctx"], s["ct32"])
        ent_c, _ = _s5_carry(tag + "_ctx_carry", _seg_major(gc), s["a_ctx"], g0, not desc, True)
        du, db_c, dc_c, da_c = _s5_scan_bwd(tag + "_ctx", proj, dy, n_ctx, n_lat, s["b"], s["bt"], s["ct"], s["abar"],
                                            hch_c, _seg_major(ent_c), desc, du_alias=du, du_rows=n)
        out.append((du, db_l + db_c, dc_l + dc_c, da_l + da_c))
    return out


def _s5_param_grads(raw, bwd):
    dabr, dabi, dbbr, dbbi, dcr, dci = [], [], [], [], [], []
    for _, db, dc, da in bwd:
        da = jnp.sum(da, axis=1)
        dabr.append(da[:, :HALF_W].reshape(SSM_GROUPS, SSM_STATE))
        dabi.append(da[:, HALF_W:].reshape(SSM_GROUPS, SSM_STATE))
        dbd = _slab_diag(db)
        dbbr.append(dbd[0].transpose(1, 0, 2))
        dbbi.append(dbd[1].transpose(1, 0, 2))
        dcd = _slab_diag(dc)
        dcr.append(dcd[0])
        dci.append(-dcd[1])
    cat = lambda xs, ax: jnp.concatenate(xs, axis=ax)
    dar, dai, dld, dbr, dbi = _s5_prep_bwd(*raw, cat(dabr, 0), cat(dabi, 0), cat(dbbr, 1), cat(dbbi, 1))
    shp = (1, 2, SSM_GROUPS, SSM_STATE)
    b_shape = (1, 2, SSM_GROUPS, SSM_STATE, SSM_GROUP)
    return dict(
        ssm_a_re=dar.reshape(shp), ssm_a_im=dai.reshape(shp), ssm_log_dt=dld.reshape(1, 2, SSM_GROUPS),
        ssm_b_re=dbr.transpose(1, 2, 0).reshape(b_shape), ssm_b_im=dbi.transpose(1, 2, 0).reshape(b_shape),
        ssm_c_re=jnp.stack(dcr)[None], ssm_c_im=jnp.stack(dci)[None])


def _example_step(x, ctx, target, mods, w, p):
    t, c = x.shape[0], ctx.shape[0]
    n = t + c
    assert t % c == 0 and c % LANES == 0 and c % (8 * N_SEG) == 0 and t % GRID_W == 0
    tr = _largest_tile(c, 256, 8)
    xall = jnp.concatenate([x, ctx], axis=0)
    g0, g1 = _vec(p["norm_g"][0]), _vec(p["norm_g"][1])
    (shift0, scale0, gate0), (shift1, scale1, gate1) = [tuple(_vec(v) for v in m) for m in mods]

    h0 = _norm_mod_fwd("l0_norm", xall, g0, scale0, shift0, n, tr, t)
    proj0 = _mm("l0_in", h0, w["ssm_w_in"], "nn")
    raw, dirs = _s5_system(p, t, c)
    (y_f, y_r), saved = _s5_forward(proj0, dirs, t, c)
    d_skip = _vec(p["ssm_d"][0])

    def post_a(u, yf, yr, dv):
        y = u * dv + yf + yr
        return y, _gelu(y)

    y0, yg = _rowwise("l0_gelu", post_a, n, tr, [(proj0, 0, D_MODEL), (y_f, 0, D_MODEL), (y_r, 0, D_MODEL)], [d_skip],
                      [(D_MODEL, F32), (D_MODEL, F32)], [])
    tg = _mm("l0_glu", yg, w["ssm_w_glu"], "nn")
    b_glu = _vec(p["ssm_b_glu"][0])

    def post_b(ygt, tt, zt, bv):
        return ygt * _sigmoid(tt + bv) * _silu(zt)

    gz0 = _rowwise("l0_gate", post_b, n, tr, [(yg, 0, D_MODEL), (tg, 0, D_MODEL), (proj0, 1, D_MODEL)], [b_glu],
                   [(D_MODEL, BF16)], [])[0]
    out0 = _mm("l0_out", gz0, w["ssm_w_out"], "nn")

    def res_norm(xt, ot, gv, g1v, sc, sh):
        x1t = xt + gv * ot
        xh, _ = _rms_hat(x1t)
        return x1t, (xh * g1v) * (1.0 + sc) + sh

    x1, h1 = _rowwise("l0_res_l1_norm", res_norm, n, tr, [(xall, 0, D_MODEL), (out0, 0, D_MODEL)],
                      [gate0, g1, scale1, shift1], [(D_MODEL, F32), (D_MODEL, BF16)], [], n_lat=t)
    proj1 = _mm("l1_in", h1, w["attn_w_in"], "nn")
    cos, sin = _rope_tables(t, c)
    qn, kn = p["attn_q_norm"][0], p["attn_k_norm"][0]
    q_h, k_h, v_h = _qk_prep(proj1, qn, kn, cos, sin, n, tr)
    tq = _largest_tile(t, 512, LANES)
    o, lse = _attn_fwd(q_h, k_h, v_h, t, tq, _largest_tile(n, 2816, LANES))
    gz1 = _rowwise("l1_gate", lambda ot, zt: ot * _silu(zt), t, tr, [(o, 0, D_MODEL), (proj1, 1, D_MODEL)], [],
                   [(D_MODEL, BF16)], [])[0]
    out1 = _mm("l1_out", gz1, w["attn_w_out"], "nn")

    gf = _vec(p["final_norm_g"])

    def head(x1t, o1t, tgt, g1v, gfv):
        x2 = x1t + g1v * o1t
        xh, r = _rms_hat(x2)
        e = xh * gfv - tgt
        dyf = e * (1.0 / D_MODEL)
        dx2 = _rms_bwd(xh, r, dyf * gfv)
        return dx2, g1v * dx2, dyf * xh, dx2 * o1t, jnp.sum(e * e, axis=1, keepdims=True)

    gate1_lat = gate1[0:1]
    dx2, dout1, d_gf, d_gate1, sq = _rowwise(
        "head", head, t, tr, [(x1, 0, D_MODEL), (out1, 0, D_MODEL), (target, 0, D_MODEL)], [gate1_lat, gf],
        [(D_MODEL, F32), (D_MODEL, BF16)], [(1, D_MODEL), (1, D_MODEL), (1, 1)])

    d_w_attn_out = _mm("l1_out_dw", gz1, dout1, "tn", out_dtype=BF16)
    dgz1 = _mm("l1_out_dx", dout1, w["attn_w_out"], "nt")

    def gate1_bwd(dgt, ot, zt):
        return dgt * _silu(zt), dgt * ot * _silu_grad(zt)

    do, dz1 = _rowwise("l1_gate_bwd", gate1_bwd, t, tr, [(dgz1, 0, D_MODEL), (o, 0, D_MODEL), (proj1, 1, D_MODEL)], [],
                       [(D_MODEL, BF16), (D_MODEL, F32)], [])
    dq_s, dk, dv = _attn_bwd(q_h, k_h, v_h, do, o, lse, t, tq, _largest_tile(n, 1024, LANES))
    dproj1, d_qn, d_kn = _qk_prep_bwd(proj1, qn, kn, cos, sin, dq_s, dz1, dk, dv, n, t, tr)
    d_w_attn_in = _mm("l1_in_dw", h1, dproj1, "tn", out_dtype=BF16)
    dh1 = _mm("l1_in_dx", dproj1, w["attn_w_in"], "nt")
    dx1, dout0, d_g1, d_scale1, d_shift1, d_gate0 = _norm_mod_bwd("l1_norm_bwd", x1, g1, scale1, dh1, dx2, n, tr, t,
                                                                  prev=(out0, gate0))

    d_w_out = _mm("l0_out_dw", gz0, dout0, "tn", out_dtype=BF16)
    dgz0 = _mm("l0_out_dx", dout0, w["ssm_w_out"], "nt")

    def post_b_bwd(dgt, ygt, tt, zt, bv):
        s = _sigmoid(tt + bv)
        dy2 = dgt * _silu(zt)
        dt = dy2 * ygt * s * (1.0 - s)
        return dgt * (ygt * s) * _silu_grad(zt), dt, dy2 * s, dt

    dz0, dtg, dyg_a, d_b_glu = _rowwise(
        "l0_gate_bwd", post_b_bwd, n, tr, [(dgz0, 0, D_MODEL), (yg, 0, D_MODEL), (tg, 0, D_MODEL), (proj0, 1, D_MODEL)],
        [b_glu], [(D_MODEL, BF16), (D_MODEL, BF16), (D_MODEL, F32)], [(1, D_MODEL)])
    d_w_glu = _mm("l0_glu_dw", yg, dtg, "tn", out_dtype=BF16)
    dyg_b = _mm("l0_glu_dx", dtg, w["ssm_w_glu"], "nt")

    def post_a_bwd(da, db, yt, ut, dv):
        dy = (da + db) * _gelu_grad(yt)
        return dy, dy * dv, dy * ut

    dy0, du_skip, d_d = _rowwise("l0_gelu_bwd", post_a_bwd, n, tr,
                                 [(dyg_a, 0, D_MODEL), (dyg_b, 0, D_MODEL), (y0, 0, D_MODEL), (proj0, 0, D_MODEL)], [d_skip],
                                 [(D_MODEL, F32), (D_MODEL, F32)], [(1, D_MODEL)])
    s5_bwd = _s5_backward(proj0, dy0, dirs, saved, t, c)
    dproj0 = _rowwise("l0_in_grad", lambda a, b, cc, dz: jnp.concatenate([a + b + cc, dz], axis=1), n, tr,
                      [(du_skip, 0, D_MODEL), (s5_bwd[0][0], 0, D_MODEL), (s5_bwd[1][0], 0, D_MODEL), (dz0, 0, D_MODEL)], [],
                      [(2 * D_MODEL, BF16)], [])[0]
    d_w_in = _mm("l0_in_dw", h0, dproj0, "tn", out_dtype=BF16)
    dh0 = _mm("l0_in_dx", dproj0, w["ssm_w_in"], "nt")
    dx0, d_g0, d_scale0, d_shift0 = _norm_mod_bwd("l0_norm_bwd", xall, g0, scale0, dh0, dx1, n, tr, t)

    big = dict(ssm_w_in=d_w_in, ssm_w_glu=d_w_glu, ssm_w_out=d_w_out, attn_w_in=d_w_attn_in, attn_w_out=d_w_attn_out)
    small = dict(
        norm_g=jnp.concatenate([d_g0[0], d_g1[0]], axis=0), ssm_d=d_d[0], ssm_b_glu=d_b_glu[0],
        attn_q_norm=d_qn, attn_k_norm=d_kn, final_norm_g=d_gf[0, 0], **_s5_param_grads(raw, s5_bwd))
    zero_v = jnp.zeros((D_MODEL,), F32)
    d_mod_lat = jnp.stack([jnp.concatenate([d_shift0[0, 0], d_scale0[0, 0], d_gate0[0, 0]]),
                           jnp.concatenate([d_shift1[0, 0], d_scale1[0, 0], d_gate1[0, 0]])])
    d_mod_ctx = jnp.stack([jnp.concatenate([d_shift0[1, 0], d_scale0[1, 0], d_gate0[1, 0]]),
                           jnp.concatenate([d_shift1[1, 0], d_scale1[1, 0], zero_v])])
    return sq[0, 0, 0], dx0[:t], big, small, d_mod_lat, d_mod_ctx


def _adamw(name, w, g, m, v):
    rows, cols = w.shape
    tr = _largest_tile(rows, 256, 8)
    c1 = 1.0 / (1.0 - ADAM_B1 ** ADAM_STEP)
    c2 = 1.0 / (1.0 - ADAM_B2 ** ADAM_STEP)

    def fn(wt, gt, mt, vt):
        mn = ADAM_B1 * mt + (1.0 - ADAM_B1) * gt
        vn = ADAM_B2 * vt + (1.0 - ADAM_B2) * (gt * gt)
        delta = -ADAM_LR * ((mn * c1) / (jnp.sqrt(vn * c2) + ADAM_EPS) + ADAM_WD * wt)
        return delta, mn, vn

    return _rowwise(name, fn, rows, tr, [(a, 0, cols) for a in (w, g, m, v)], [], [(cols, F32)] * 3, [])


BIG = ("ssm_w_in", "ssm_w_glu", "ssm_w_out", "attn_w_in", "attn_w_out")
COL_SHARDED = ("ssm_w_in", "attn_w_in")
WEIGHTS = ("c_ctx", "w_mod", "b_mod", "norm_g", "ssm_w_in", "ssm_a_re", "ssm_a_im", "ssm_log_dt", "ssm_b_re", "ssm_b_im",
           "ssm_c_re", "ssm_c_im", "ssm_d", "ssm_w_glu", "ssm_b_glu", "ssm_w_out", "attn_w_in", "attn_q_norm",
           "attn_k_norm", "attn_w_out", "final_norm_g")
SMALL = tuple(k for k in WEIGHTS if k not in BIG and k != "w_mod")
PACK_W = 1024
COND_ROWS = 2 * N_DEV


def _attn_in_perm(x, inverse):
    a, kv = ATTN_W, 2 * KV_W
    if inverse:
        return jnp.concatenate([x[..., :a], x[..., 2 * a:], x[..., a:2 * a]], axis=-1)
    return jnp.concatenate([x[..., :a], x[..., a + kv:], x[..., a:a + kv]], axis=-1)


def _pack(arrays, dtype, row_unit):
    flat = jnp.concatenate([a.reshape(-1).astype(dtype) for a in arrays])
    rows = -(-flat.shape[0] // PACK_W)
    rows = -(-rows // row_unit) * row_unit
    flat = jnp.concatenate([flat, jnp.zeros((rows * PACK_W - flat.shape[0],), dtype)])
    return flat.reshape(rows, PACK_W)


def _unpack(buf, shapes):
    lead = buf.shape[:-2]
    flat = buf.reshape(lead + (-1,))
    out, off = [], 0
    for shp in shapes:
        size = math.prod(shp)
        out.append(flat[..., off:off + size].reshape(lead + tuple(shp)))
        off += size
    return out


def kernel(x, c, ctx, c_ctx, w_mod, b_mod, norm_g, ssm_w_in, ssm_a_re, ssm_a_im, ssm_log_dt, ssm_b_re, ssm_b_im, ssm_c_re, ssm_c_im, ssm_d, ssm_w_glu, ssm_b_glu, ssm_w_out, attn_w_in, attn_q_norm, attn_k_norm, attn_w_out, final_norm_g, loss_target, m_c_ctx, m_w_mod, m_b_mod, m_norm_g, m_ssm_w_in, m_ssm_a_re, m_ssm_a_im, m_ssm_log_dt, m_ssm_b_re, m_ssm_b_im, m_ssm_c_re, m_ssm_c_im, m_ssm_d, m_ssm_w_glu, m_ssm_b_glu, m_ssm_w_out, m_attn_w_in, m_attn_q_norm, m_attn_k_norm, m_attn_w_out, m_final_norm_g, v_c_ctx, v_w_mod, v_b_mod, v_norm_g, v_ssm_w_in, v_ssm_a_re, v_ssm_a_im, v_ssm_log_dt, v_ssm_b_re, v_ssm_b_im, v_ssm_c_re, v_ssm_c_im, v_ssm_d, v_ssm_w_glu, v_ssm_b_glu, v_ssm_w_out, v_attn_w_in, v_attn_q_norm, v_attn_k_norm, v_attn_w_out, v_final_norm_g):
    args = dict(locals())
    wts = {k: args[k] for k in WEIGHTS}
    mom_m = {k: args["m_" + k] for k in WEIGHTS}
    mom_v = {k: args["v_" + k] for k in WEIGHTS}
    mx, my, mc = lax.axis_index("x"), lax.axis_index("y"), lax.axis_index("c")
    chip = 2 * mx + my
    me = 2 * chip + mc

    halves = []
    for k in BIG:
        sh = wts[k][0]
        hr = sh.shape[0] // 2
        halves.append(lax.dynamic_slice_in_dim(sh, mc * hr, hr, axis=0))
    gathered = _gather_two_level("gather_weights", _pack(halves, BF16, 16))
    parts = _unpack(gathered, [h.shape for h in halves])
    w_full = {}
    for k, pc in zip(BIG, parts):
        hr, cols = pc.shape[1:]
        pc = pc.reshape(N_CHIP, 2, hr, cols)
        if k in COL_SHARDED:
            w_full[k] = pc.transpose(1, 2, 0, 3).reshape(2 * hr, N_CHIP * cols)
        else:
            w_full[k] = pc.reshape(N_CHIP * 2 * hr, cols)
    w_full["attn_w_in"] = _attn_in_perm(w_full["attn_w_in"], False)

    c_blk = jnp.concatenate([c, jnp.zeros((N_DEV - 1, D_MODEL), F32)], axis=0)
    c_all = _exchange("gather_c", c_blk, True)[:, 0]
    cond = jnp.concatenate([c_all, c_ctx[None], jnp.zeros((COND_ROWS - N_DEV - 1, D_MODEL), F32)], axis=0)
    s_cond, ds_cond = _rowwise("cond_silu", lambda t: (_silu(t), _silu_grad(t)), COND_ROWS, COND_ROWS, [(cond, 0, D_MODEL)], [],
                               [(D_MODEL, F32), (D_MODEL, F32)], [])
    w_mod_b = w_mod.astype(BF16)
    mcols = w_mod.shape[2]
    mod_part = jnp.stack([_mm(f"mod{i}", s_cond, w_mod_b[i], "nn") for i in range(2)])
    mod_g = _exchange("gather_mod", mod_part.reshape(2 * COND_ROWS, mcols), True)
    mod_all = mod_g.reshape(N_CHIP, 2, 2, COND_ROWS, mcols)[:, 0]
    mod_all = mod_all.transpose(1, 2, 0, 3).reshape(2, COND_ROWS, N_CHIP * mcols) + b_mod[:, None, :]
    mods = []
    for i in range(2):
        lat = lax.dynamic_slice_in_dim(mod_all[i], me, 1, axis=0)[0]
        both = jnp.stack([lat, mod_all[i, N_DEV]])
        mods.append((both[:, :D_MODEL], both[:, D_MODEL:2 * D_MODEL], both[:, 2 * D_MODEL:]))

    small_p = {k: wts[k] for k in SMALL if k != "c_ctx" and k != "b_mod"}
    sq, grad_x, big_g, small_g, d_mod_lat, d_mod_ctx = _example_step(x[0], ctx[0], loss_target[0], mods, w_full, small_p)
    loss = lax.psum(0.5 / D_MODEL * sq, ("x", "y", "c"))
    big_g["attn_w_in"] = _attn_in_perm(big_g["attn_w_in"], True)

    small_names = [k for k in SMALL if k not in ("c_ctx", "b_mod")]
    small_list = [small_g[k] for k in small_names] + [d_mod_lat, d_mod_ctx]
    small_shapes = [wts[k].shape for k in small_names] + [d_mod_lat.shape, d_mod_ctx.shape]
    packed = _pack(small_list, F32, 8 * N_DEV)
    slice_rows = packed.shape[0] // N_DEV
    slices = _exchange("scatter_small", packed.reshape(N_DEV, slice_rows, PACK_W), False)
    my_sum = _sum_slots("sum_small", slices)
    payload = jnp.concatenate([my_sum, _pack([d_mod_lat], F32, 8)], axis=0)
    sg = _exchange("gather_small", payload, True)
    summed = _unpack(sg[:, :slice_rows].reshape(packed.shape), small_shapes)
    grads = dict(zip(small_names, summed[:-2]))
    d_mod_lat_sum, d_mod_ctx_sum = summed[-2], summed[-1]
    grads["b_mod"] = d_mod_lat_sum + d_mod_ctx_sum
    d_mod_lat_all = _unpack(sg[:, slice_rows:], [d_mod_lat.shape])[0]

    g_w_mod, ds_cc = [], []
    for i in range(2):
        rows9 = jnp.concatenate([d_mod_lat_all[:, i], d_mod_ctx_sum[i][None],
                                 jnp.zeros((COND_ROWS - N_DEV - 1, 3 * D_MODEL), F32)], axis=0)
        mine = lax.dynamic_slice_in_dim(rows9, chip * mcols, mcols, axis=1)
        g_w_mod.append(_mm(f"mod{i}_dw", s_cond, mine, "tn"))
        ds_cc.append(_mm(f"mod{i}_dx", mine, w_mod_b[i], "nt")[N_DEV])
    grads["w_mod"] = jnp.stack(g_w_mod)
    part = (ds_cc[0] + ds_cc[1]) * jnp.where(mc == 0, 1.0, 0.0)
    part_blk = jnp.concatenate([part[None], jnp.zeros((N_DEV - 1, D_MODEL), F32)], axis=0)
    ds_all = _sum_slots("sum_c_ctx", _exchange("gather_c_ctx", part_blk, True))
    grads["c_ctx"] = ds_all[0] * ds_cond[N_DEV]

    blocks = []
    for k in BIG:
        g = big_g[k]
        rows, cols = g.shape
        if k in COL_SHARDED:
            blocks.append(g.reshape(2, rows // 2, N_CHIP, cols // N_CHIP).transpose(2, 0, 1, 3).reshape(N_DEV, -1))
        else:
            blocks.append(g.reshape(N_DEV, -1))
    sendbuf = jnp.concatenate(blocks, axis=1).astype(BF16)
    sendbuf = sendbuf.reshape(N_DEV, -1, PACK_W)
    recv = _exchange("scatter_big", sendbuf, False)
    mine = _sum_slots("sum_big", recv)
    both = _exchange("swap_halves", mine, True, sibling_only=True)
    half_shapes = [(wts[k].shape[1] // 2, wts[k].shape[2]) for k in BIG]
    for k, pc in zip(BIG, _unpack(both, half_shapes)):
        grads[k] = pc.reshape(wts[k].shape)

    delta, new_m, new_v = {}, {}, {}
    for k in BIG + ("w_mod",):
        shp = wts[k].shape
        two_d = (-1, shp[-1])
        res = _adamw("adamw_" + k, *[a.reshape(two_d) for a in (wts[k], grads[k], mom_m[k], mom_v[k])])
        delta[k], new_m[k], new_v[k] = [r.reshape(shp) for r in res]
    shapes = [wts[k].shape for k in SMALL]
    packed = [_pack([d[k] for k in SMALL], F32, 8) for d in (wts, grads, mom_m, mom_v)]
    res = _adamw("adamw_small", *packed)
    for dst, buf in zip((delta, new_m, new_v), res):
        for k, a in zip(SMALL, _unpack(buf, shapes)):
            dst[k] = a
    grads = {k: grads[k].reshape(wts[k].shape) for k in WEIGHTS}
    return (loss, grad_x[None], *[grads[k] for k in WEIGHTS], *[delta[k] for k in WEIGHTS],
            *[new_m[k] for k in WEIGHTS], *[new_v[k] for k in WEIGHTS])
```

```python
import functools
import math

import jax
import jax.numpy as jnp
from jax import lax
from jax.experimental import pallas as pl
from jax.experimental.pallas import tpu as pltpu

F32 = jnp.float32
BF16 = jnp.bfloat16

D_MODEL = 1024
NORM_EPS = 1e-6
SSM_GROUPS = 64
SSM_GROUP = 16
SSM_STATE = 64
LANES = 128
SLAB_W = LANES
N_SLAB = D_MODEL // SLAB_W
SLAB_GROUPS = SLAB_W // SSM_GROUP
HALF_W = SLAB_GROUPS * SSM_STATE
STATE_W = 2 * HALF_W
N_SEG = 8
HEAD_DIM = 64
N_Q_HEADS = 16
N_KV_HEADS = 4
KV_REP = N_Q_HEADS // N_KV_HEADS
ATTN_W = N_Q_HEADS * HEAD_DIM
KV_W = N_KV_HEADS * HEAD_DIM
GRID_W = 64
ROPE_THETA = 10000.0
N_DEV = 8
N_CHIP = 4
VMEM_LIMIT_BYTES = 56 * 1024 * 1024

ADAM_LR = 0.001
ADAM_B1 = 0.9
ADAM_B2 = 0.999
ADAM_EPS = 1e-08
ADAM_WD = 0.01
ADAM_STEP = 10


def _params(*sem):
    return pltpu.CompilerParams(dimension_semantics=sem, vmem_limit_bytes=VMEM_LIMIT_BYTES)


def _largest_tile(n, cap, unit):
    if n <= cap:
        return n
    t = (cap // unit) * unit
    while t >= unit:
        if n % t == 0:
            return t
        t -= unit
    raise ValueError(f"no tile for {n} (cap {cap}, unit {unit})")


def _rowwise(name, fn, n_rows, tr, row_ins, vec_ins, row_outs, red_outs, n_lat=None, want_flag=False):
    nt = n_rows // tr
    assert nt * tr == n_rows
    nlt = nt if n_lat is None else n_lat // tr

    def sel(i):
        return jnp.where(i >= nlt, 1, 0)

    arrays, in_specs, pairs = [], [], []
    for spec in row_ins:
        arr, cb, w = spec[:3]
        kind = spec[3] if len(spec) > 3 else None
        m = spec[4] if len(spec) > 4 else None
        if kind == "pair":
            arrays += [arr, m]
            in_specs += [pl.BlockSpec((tr, w), functools.partial(lambda i, cb: (jnp.minimum(i, nlt - 1), cb), cb=cb)),
                         pl.BlockSpec((tr, w), functools.partial(lambda i, cb: (jnp.maximum(i - nlt, 0), cb), cb=cb))]
            pairs.append(len(arrays) - 2)
            continue
        if kind == "mod":
            imap = functools.partial(lambda i, cb, m: (i % m, cb), cb=cb, m=m)
        elif kind == "clamp":
            imap = functools.partial(lambda i, cb, m: (jnp.minimum(i, m - 1), cb), cb=cb, m=m)
        else:
            imap = functools.partial(lambda i, cb: (i, cb), cb=cb)
        arrays.append(arr)
        in_specs.append(pl.BlockSpec((tr, w), imap))
    for v in vec_ins:
        s, a, w = v.shape
        imap = (lambda i: (sel(i), 0, 0)) if s == 2 else (lambda i: (0, 0, 0))
        arrays.append(v)
        in_specs.append(pl.BlockSpec((1, a, w), imap))
    out_shapes, out_specs = [], []
    lat_only = [len(spec) > 2 for spec in row_outs]
    for (w, dt), lat in zip([spec[:2] for spec in row_outs], lat_only):
        out_shapes.append(jax.ShapeDtypeStruct((n_lat if lat else n_rows, w), dt))
        out_specs.append(pl.BlockSpec((tr, w), (lambda i: (jnp.minimum(i, nlt - 1), 0)) if lat else (lambda i: (i, 0))))
    for s, w in red_outs:
        out_shapes.append(jax.ShapeDtypeStruct((s, 1, w), F32))
        imap = (lambda i: (sel(i), 0, 0)) if s == 2 else (lambda i: (0, 0, 0))
        out_specs.append(pl.BlockSpec((1, 1, w), imap))
    n_ri, n_vi, n_ro, n_rd = len(row_ins) + len(pairs), len(vec_ins), len(row_outs), len(red_outs)

    def body(*refs):
        i = pl.program_id(0)
        rows, k = [], 0
        while k < n_ri:
            if k in pairs:
                rows.append(jnp.where(i < nlt, refs[k][...], refs[k + 1][...]))
                k += 2
            else:
                rows.append(refs[k][...])
                k += 1
        vecs = [r[0] for r in refs[n_ri:n_ri + n_vi]]
        outs = refs[n_ri + n_vi:]
        lead = [jnp.where(i < nlt, 1.0, 0.0).astype(F32)] if want_flag else []
        res = fn(*lead, *rows, *vecs)
        if not isinstance(res, (tuple, list)):
            res = (res,)
        assert len(res) == n_ro + n_rd
        for k in range(n_ro):
            if lat_only[k]:
                @pl.when(i < nlt)
                def _(k=k):
                    outs[k][...] = res[k].astype(outs[k].dtype)
            else:
                outs[k][...] = res[k].astype(outs[k].dtype)
        for k in range(n_rd):
            part = jnp.sum(res[n_ro + k].astype(F32), axis=0, keepdims=True)
            first = i == 0
            if red_outs[k][0] == 2:
                first = jnp.logical_or(first, i == nlt)
            o = outs[n_ro + k]

            @pl.when(first)
            def _():
                o[0] = part

            @pl.when(jnp.logical_not(first))
            def _():
                o[0] = o[0] + part

    res = pl.pallas_call(
        body, grid=(nt,), in_specs=in_specs, out_specs=out_specs, out_shape=out_shapes,
        compiler_params=_params("arbitrary"), name=name)(*arrays)
    return res


def _vec(v):
    v = v.astype(F32)
    if v.ndim == 1:
        v = v[None]
    return v[:, None, :]


def _mm(name, a, b, mode, out_dtype=F32):
    if mode in ("nn", "nt"):
        m, k = a.shape
        n = b.shape[1] if mode == "nn" else b.shape[0]
        tm = _largest_tile(m, 1024, 8)
        tn = _largest_tile(n, 1024, 128)
        contract = (((1,), (0,)), ((), ())) if mode == "nn" else (((1,), (1,)), ((), ()))

        def body(a_ref, b_ref, o_ref):
            o_ref[...] = lax.dot_general(a_ref[...].astype(BF16), b_ref[...].astype(BF16), contract,
                                         preferred_element_type=F32).astype(o_ref.dtype)

        b_spec = pl.BlockSpec((k, tn), lambda i, j: (0, j)) if mode == "nn" else pl.BlockSpec((tn, k), lambda i, j: (j, 0))
        return pl.pallas_call(
            body, grid=(m // tm, n // tn),
            in_specs=[pl.BlockSpec((tm, k), lambda i, j: (i, 0)), b_spec],
            out_specs=pl.BlockSpec((tm, tn), lambda i, j: (i, j)),
            out_shape=jax.ShapeDtypeStruct((m, n), out_dtype),
            compiler_params=_params("parallel", "arbitrary"), name=name)(a, b)
    assert mode == "tn"
    r, k1 = a.shape
    k2 = b.shape[1]
    tr = _largest_tile(r, 1024, 8)
    t2 = _largest_tile(k2, 1024, 128)
    nr = r // tr

    def body(a_ref, b_ref, o_ref, acc_ref):
        part = lax.dot_general(a_ref[...].astype(BF16), b_ref[...].astype(BF16), (((0,), (0,)), ((), ())),
                               preferred_element_type=F32)
        i = pl.program_id(1)

        @pl.when(i == 0)
        def _():
            acc_ref[...] = part

        @pl.when(i > 0)
        def _():
            acc_ref[...] += part

        @pl.when(i == nr - 1)
        def _():
            o_ref[...] = acc_ref[...].astype(o_ref.dtype)

    return pl.pallas_call(
        body, grid=(k2 // t2, nr),
        in_specs=[pl.BlockSpec((tr, k1), lambda j, i: (i, 0)), pl.BlockSpec((tr, t2), lambda j, i: (i, j))],
        out_specs=pl.BlockSpec((k1, t2), lambda j, i: (0, j)),
        out_shape=jax.ShapeDtypeStruct((k1, k2), out_dtype),
        scratch_shapes=[pltpu.VMEM((k1, t2), F32)],
        compiler_params=_params("parallel", "arbitrary"), name=name)(a, b)


def _exchange(name, x, bcast, sibling_only=False):
    rels = [1] if sibling_only else list(range(1, N_DEV))
    n_slot = 2 if sibling_only else N_DEV
    blk = x.shape if bcast else x.shape[1:]

    def body(x_ref, o_ref, send_sems, recv_sems, local_sem):
        mx, my, mc = lax.axis_index("x"), lax.axis_index("y"), lax.axis_index("c")
        me = mc if sibling_only else 4 * mx + 2 * my + mc
        me_dev = 4 * mx + 2 * my + mc
        mine = pltpu.make_async_copy(x_ref if bcast else x_ref.at[me_dev], o_ref.at[me], local_sem)
        mine.start()
        copies = []
        for k, r in enumerate(rels):
            px = 1 - mx if (r >> 2) & 1 else mx
            py = 1 - my if (r >> 1) & 1 else my
            pc = 1 - mc if r & 1 else mc
            src = x_ref if bcast else x_ref.at[4 * px + 2 * py + pc]
            cp = pltpu.make_async_remote_copy(
                src_ref=src, dst_ref=o_ref.at[me], send_sem=send_sems.at[k], recv_sem=recv_sems.at[k],
                device_id=(px, py, pc), device_id_type=pl.DeviceIdType.MESH)
            cp.start()
            copies.append(cp)
        for cp in copies:
            cp.wait()
        mine.wait()

    return pl.pallas_call(
        body, out_shape=jax.ShapeDtypeStruct((n_slot,) + tuple(blk), x.dtype),
        in_specs=[pl.BlockSpec(memory_space=pltpu.VMEM if sibling_only else pl.ANY)],
        out_specs=pl.BlockSpec(memory_space=pl.ANY),
        scratch_shapes=[pltpu.SemaphoreType.DMA((len(rels),)), pltpu.SemaphoreType.DMA((len(rels),)),
                        pltpu.SemaphoreType.DMA],
        name=name)(x)


def _gather_two_level(name, x):
    def body(x_ref, o_ref, send_sems, recv_sems, local_sem):
        mx, my, mc = lax.axis_index("x"), lax.axis_index("y"), lax.axis_index("c")
        me, sibling = (mx, my, mc), (mx, my, 1 - mc)
        chips = [(1 - mx, my), (mx, 1 - my), (1 - mx, 1 - my)]

        def slot(px, py, pc):
            return o_ref.at[4 * px + 2 * py + pc]

        def copy(k, block, to, src=None):
            return pltpu.make_async_remote_copy(
                src_ref=slot(*block) if src is None else src, dst_ref=slot(*block), send_sem=send_sems.at[k],
                recv_sem=recv_sems.at[k], device_id=to, device_id_type=pl.DeviceIdType.MESH)

        mine = pltpu.make_async_copy(x_ref, slot(*me), local_sem)
        mine.start()
        first = [copy(0, me, sibling, src=x_ref)]
        first += [copy(1 + j, me, (*chip, mc), src=x_ref) for j, chip in enumerate(chips)]
        for cp in first:
            cp.start()
        passed = [copy(4 + j, (*chip, mc), sibling) for j, chip in enumerate(chips)]
        for j, chip in enumerate(chips):
            copy(1 + j, (*chip, mc), me).wait_recv()
            passed[j].start()
        copy(0, sibling, me).wait_recv()
        for j, chip in enumerate(chips):
            copy(4 + j, (*chip, 1 - mc), me).wait_recv()
        for cp in first + passed:
            cp.wait_send()
        mine.wait()

    return pl.pallas_call(
        body, out_shape=jax.ShapeDtypeStruct((N_DEV,) + tuple(x.shape), x.dtype),
        in_specs=[pl.BlockSpec(memory_space=pl.ANY)], out_specs=pl.BlockSpec(memory_space=pl.ANY),
        scratch_shapes=[pltpu.SemaphoreType.DMA((N_DEV - 1,)), pltpu.SemaphoreType.DMA((N_DEV - 1,)),
                        pltpu.SemaphoreType.DMA],
        name=name)(x)


def _sum_slots(name, x):
    s, r, w = x.shape
    tr = _largest_tile(r, 256, 8)

    def body(x_ref, o_ref):
        acc = x_ref[0].astype(F32)
        for j in range(1, s):
            acc = acc + x_ref[j].astype(F32)
        o_ref[...] = acc

    return pl.pallas_call(
        body, grid=(r // tr,), in_specs=[pl.BlockSpec((s, tr, w), lambda i: (0, i, 0))],
        out_specs=pl.BlockSpec((tr, w), lambda i: (i, 0)), out_shape=jax.ShapeDtypeStruct((r, w), F32),
        compiler_params=_params("parallel"), name=name)(x)


def _sigmoid(x):
    return 1.0 / (1.0 + jnp.exp(-x))


def _silu(x):
    return x * _sigmoid(x)


def _silu_grad(x):
    s = _sigmoid(x)
    return s * (1.0 + x * (1.0 - s))


_INV_SQRT2 = 1.0 / math.sqrt(2.0)
_INV_SQRT2PI = 1.0 / math.sqrt(2.0 * math.pi)


def _gelu(x):
    return 0.5 * x * (1.0 + lax.erf(x * _INV_SQRT2))


def _gelu_grad(x):
    return 0.5 * (1.0 + lax.erf(x * _INV_SQRT2)) + x * jnp.exp(-0.5 * x * x) * _INV_SQRT2PI


def _rms_hat(x):
    r = lax.rsqrt(jnp.mean(x * x, axis=-1, keepdims=True) + NORM_EPS)
    return x * r, r


def _rms_bwd(xh, r, dxh):
    return r * (dxh - xh * jnp.mean(dxh * xh, axis=-1, keepdims=True))


def _stream(x):
    return (x[0], 0, D_MODEL, "pair", x[1]) if isinstance(x, tuple) else (x, 0, D_MODEL)


def _norm_mod_fwd(name, x, g, scale, shift, n_rows, tr, n_lat):
    def fn(xt, gv, sc, sh):
        xh, _ = _rms_hat(xt)
        return (xh * gv) * (1.0 + sc) + sh

    return _rowwise(name, fn, n_rows, tr, [_stream(x)], [g, scale, shift], [(D_MODEL, BF16)], [], n_lat=n_lat)[0]


def _norm_mod_bwd(name, x, g, scale, dh, dres, n_rows, tr, n_lat, prev=None, dx_lat_only=False):
    nlt = n_lat // tr

    def fn(flag, xt, dht, drt, *rest):
        gv, sc = rest[-2:] if prev is None else rest[1:3]
        xh, r = _rms_hat(xt)
        n = xh * gv
        dn = dht * (1.0 + sc)
        dx = _rms_bwd(xh, r, dn * gv) + flag * drt
        if prev is None:
            return dx, dn * xh, dht * n, dht
        return dx, rest[3] * dx, dn * xh, dht * n, dht, dx * rest[0]

    rows = [_stream(x), (dh, 0, D_MODEL), (dres, 0, D_MODEL, "clamp", nlt)]
    row_outs = [(D_MODEL, F32, "lat") if dx_lat_only else (D_MODEL, F32)]
    vecs, reds = [g, scale], [(1, D_MODEL), (2, D_MODEL), (2, D_MODEL)]
    if prev is not None:
        rows.append((prev[0], 0, D_MODEL))
        vecs.append(prev[1])
        row_outs.append((D_MODEL, BF16))
        reds.append((2, D_MODEL))
    return _rowwise(name, fn, n_rows, tr, rows, vecs, row_outs, reds, n_lat=n_lat, want_flag=True)


def _s5_prep(a_re, a_im, log_dt, b_re, b_im, seg_lat, seg_ctx):
    def body(ar_ref, ai_ref, ld_ref, br_ref, bi_ref, abr_ref, abi_ref, bbr_ref, bbi_ref, alr_ref, ali_ref, acr_ref,
             aci_ref):
        lr, li = ar_ref[...], ai_ref[...]
        dt = jnp.exp(ld_ref[...])
        ldr, ldi = lr * dt, li * dt
        e = jnp.exp(ldr)
        abr, abi = e * jnp.cos(ldi), e * jnp.sin(ldi)
        abr_ref[...] = abr
        abi_ref[...] = abi
        den = lr * lr + li * li
        nr, ni = abr - 1.0, abi
        qr = (nr * lr + ni * li) / den
        qi = (ni * lr - nr * li) / den
        br, bi = br_ref[...], bi_ref[...]
        bbr_ref[...] = qr[None] * br - qi[None] * bi
        bbi_ref[...] = qr[None] * bi + qi[None] * br
        for seg, r_ref, i_ref in ((seg_lat, alr_ref, ali_ref), (seg_ctx, acr_ref, aci_ref)):
            es = jnp.exp(ldr * float(seg))
            r_ref[...] = es * jnp.cos(ldi * float(seg))
            i_ref[...] = es * jnp.sin(ldi * float(seg))

    sm = jax.ShapeDtypeStruct(a_re.shape, F32)
    big = jax.ShapeDtypeStruct(b_re.shape, F32)
    return pl.pallas_call(body, out_shape=[sm, sm, big, big, sm, sm, sm, sm], name="s5_prep")(
        a_re, a_im, log_dt, b_re, b_im)


def _s5_prep_bwd(a_re, a_im, log_dt, b_re, b_im, dabr, dabi, dbbr, dbbi):
    def body(ar_ref, ai_ref, ld_ref, br_ref, bi_ref, dabr_ref, dabi_ref, dbbr_ref, dbbi_ref,
             dar_ref, dai_ref, dld_ref, dbr_ref, dbi_ref):
        lr, li = ar_ref[...], ai_ref[...]
        dt = jnp.exp(ld_ref[...])
        ldr, ldi = lr * dt, li * dt
        e = jnp.exp(ldr)
        abr, abi = e * jnp.cos(ldi), e * jnp.sin(ldi)
        den = lr * lr + li * li
        nr, ni = abr - 1.0, abi
        qr = (nr * lr + ni * li) / den
        qi = (ni * lr - nr * li) / den
        br, bi = br_ref[...], bi_ref[...]
        gbr, gbi = dbbr_ref[...], dbbi_ref[...]
        dbr_ref[...] = gbr * qr[None] + gbi * qi[None]
        dbi_ref[...] = gbi * qr[None] - gbr * qi[None]
        dqr = jnp.sum(gbr * br + gbi * bi, axis=0)
        dqi = jnp.sum(gbi * br - gbr * bi, axis=0)
        dnr = (dqr * lr - dqi * li) / den
        dni = (dqr * li + dqi * lr) / den
        dlr_q = (dqr * (nr - 2.0 * lr * qr) + dqi * (ni - 2.0 * lr * qi)) / den
        dli_q = (dqr * (ni - 2.0 * li * qr) + dqi * (-nr - 2.0 * li * qi)) / den
        gar = dabr_ref[...] + dnr
        gai = dabi_ref[...] + dni
        dldr = gar * abr + gai * abi
        dldi = gai * abr - gar * abi
        dar_ref[...] = dldr * dt + dlr_q
        dai_ref[...] = dldi * dt + dli_q
        ddt = jnp.sum(dldr * lr + dldi * li, axis=1, keepdims=True)
        dld_ref[...] = ddt * dt

    sm = jax.ShapeDtypeStruct(a_re.shape, F32)
    big = jax.ShapeDtypeStruct(b_re.shape, F32)
    return pl.pallas_call(body, out_shape=[sm, sm, jax.ShapeDtypeStruct(log_dt.shape, F32), big, big],
                          name="s5_prep_bwd")(a_re, a_im, log_dt, b_re, b_im, dabr, dabi, dbbr, dbbi)


def _slab_cols(v):
    return v.reshape(N_SLAB, 1, HALF_W)


def _slab_pair(vr, vi):
    return jnp.concatenate([_slab_cols(vr), _slab_cols(vi)], axis=-1)


def _slab_in_matrix(bbr, bbi):
    eye = jnp.eye(SLAB_GROUPS, dtype=F32)

    def one(b):
        b = b.reshape(N_SLAB, SLAB_GROUPS, SSM_STATE, SSM_GROUP)
        m = jnp.einsum("sgph,gk->sghkp", b, eye)
        return m.reshape(N_SLAB, SLAB_W, HALF_W)

    return jnp.concatenate([one(bbr), one(bbi)], axis=-1)


def _slab_out_matrix(cr, ci):
    eye = jnp.eye(SLAB_GROUPS, dtype=F32)

    def one(c):
        c = c.reshape(N_SLAB, SLAB_GROUPS, SSM_GROUP, SSM_STATE)
        m = jnp.einsum("sghp,gk->skpgh", c, eye)
        return m.reshape(N_SLAB, HALF_W, SLAB_W)

    return jnp.concatenate([one(cr), one(-ci)], axis=1)


def _slab_diag(m):
    m = m.reshape(N_SLAB, SLAB_GROUPS, SSM_GROUP, 2, SLAB_GROUPS, SSM_STATE)
    d = jnp.stack([m[:, g, :, :, g, :] for g in range(SLAB_GROUPS)], axis=1)
    return d.transpose(3, 0, 1, 2, 4).reshape(2, SSM_GROUPS, SSM_GROUP, SSM_STATE)


def _cmul(ar, ai, xr, xi, conj):
    if conj:
        return ar * xr + ai * xi, ar * xi - ai * xr
    return ar * xr - ai * xi, ar * xi + ai * xr


def _s5_pow_table(name, abar, seg, falling, conj):
    assert seg >= 8 and seg & (seg - 1) == 0

    def body(a_ref, o_ref, t_ref):
        ar, ai = a_ref[0, :, :HALF_W], a_ref[0, :, HALF_W:]
        if conj:
            ai = -ai
        rr, ri = [jnp.ones_like(ar)], [jnp.zeros_like(ai)]
        for _ in range(7):
            pr, pi = _cmul(ar, ai, rr[-1], ri[-1], False)
            rr.append(pr)
            ri.append(pi)
        sr, si = _cmul(ar, ai, rr[-1], ri[-1], False)
        if falling:
            rr, ri = rr[::-1], ri[::-1]
        first = slice(seg - 8, seg) if falling else slice(0, 8)
        t_ref[first, :HALF_W] = jnp.concatenate(rr, axis=0)
        t_ref[first, HALF_W:] = jnp.concatenate(ri, axis=0)
        size = 8
        while size < seg:
            src = slice(seg - size, seg) if falling else slice(0, size)
            dst = slice(seg - 2 * size, seg - size) if falling else slice(size, 2 * size)
            pr, pi = _cmul(sr, si, t_ref[src, :HALF_W], t_ref[src, HALF_W:], False)
            t_ref[dst, :HALF_W] = pr
            t_ref[dst, HALF_W:] = pi
            sr, si = _cmul(sr, si, sr, si, False)
            size *= 2
        o_ref[0] = t_ref[...].astype(BF16)

    return pl.pallas_call(
        body, grid=(N_SLAB,), in_specs=[pl.BlockSpec((1, 1, STATE_W), lambda s: (s, 0, 0))],
        out_specs=pl.BlockSpec((1, seg, STATE_W), lambda s: (s, 0, 0)),
        out_shape=jax.ShapeDtypeStruct((N_SLAB, seg, STATE_W), BF16),
        scratch_shapes=[pltpu.VMEM((seg, STATE_W), F32)], compiler_params=_params("parallel"), name=name)(abar)


def _s5_ends(name, x, n_rows, row0, table, m_mat):
    seg = n_rows // N_SEG
    rb = row0 // n_rows
    tn = (((0,), (0,)), ((), ()))

    def body(x_ref, t_ref, m_ref, z_ref):
        mr, mi = m_ref[0, :, :HALF_W], m_ref[0, :, HALF_W:]
        for j in range(N_SEG):
            t = lax.dot_general(x_ref[j * seg:(j + 1) * seg, :].astype(BF16), t_ref[0], tn,
                                preferred_element_type=F32)
            tr_, ti_ = t[:, :HALF_W], t[:, HALF_W:]
            z_ref[0, j:j + 1, :HALF_W] = jnp.sum(mr * tr_ - mi * ti_, axis=0, keepdims=True)
            z_ref[0, j:j + 1, HALF_W:] = jnp.sum(mr * ti_ + mi * tr_, axis=0, keepdims=True)

    return pl.pallas_call(
        body, grid=(N_SLAB,),
        in_specs=[pl.BlockSpec((n_rows, SLAB_W), lambda s: (rb, s)),
                  pl.BlockSpec((1, seg, STATE_W), lambda s: (s, 0, 0)),
                  pl.BlockSpec((1, SLAB_W, STATE_W), lambda s: (s, 0, 0))],
        out_specs=pl.BlockSpec((1, N_SEG, STATE_W), lambda s: (s, 0, 0)),
        out_shape=jax.ShapeDtypeStruct((N_SLAB, N_SEG, STATE_W), F32),
        compiler_params=_params("parallel"), name=name)(x, table, m_mat)


def _s5_carry(name, z, a_seg, init, descending, conj):
    order = list(range(N_SEG - 1, -1, -1)) if descending else list(range(N_SEG))

    def body(z_ref, a_ref, i_ref, e_ref, o_ref):
        ar, ai = a_ref[:, :HALF_W], a_ref[:, HALF_W:]
        cr, ci = i_ref[:, :HALF_W], i_ref[:, HALF_W:]
        for j in order:
            e_ref[:, j, :HALF_W] = cr
            e_ref[:, j, HALF_W:] = ci
            pr, pi = _cmul(ar, ai, cr, ci, conj)
            cr = pr + z_ref[:, j, :HALF_W]
            ci = pi + z_ref[:, j, HALF_W:]
        o_ref[:, :HALF_W] = cr
        o_ref[:, HALF_W:] = ci

    return pl.pallas_call(body, out_shape=[jax.ShapeDtypeStruct(z.shape, F32), jax.ShapeDtypeStruct(init.shape, F32)],
                          name=name)(z, a_seg, init)


def _s5_scan(name, u, n_rows, row0, b_mat, c_mat, abar, h_in, descending, y_alias=None, y_rows=None):
    seg = n_rows // N_SEG
    ta = min(32, seg)
    nk = seg // ta
    assert seg * N_SEG == n_rows and nk * ta == seg and row0 % n_rows == 0 and ta % 8 == 0
    rb = row0 // n_rows
    tile = ta * N_SEG

    def body(*refs):
        u_ref, b_ref, c_ref, a_ref, hin_ref = refs[:5]
        y_ref, hch_ref, st_ref, up_ref, h_ref = refs[-5:]
        k = pl.program_id(1)
        kk = nk - 1 - k if descending else k
        a0 = kk * ta

        @pl.when(k == 0)
        def _():
            st_ref[...] = hin_ref[0]

        hch_ref[0, 0] = st_ref[...]
        for al in range(ta):
            up_ref[al * N_SEG:(al + 1) * N_SEG, :] = u_ref[pl.ds(a0 + al, N_SEG, stride=seg), :]
        h_ref[...] = jnp.dot(up_ref[...].astype(BF16), b_ref[0], preferred_element_type=F32)
        ar = jnp.broadcast_to(a_ref[0, :, :HALF_W], (N_SEG, HALF_W))
        ai = jnp.broadcast_to(a_ref[0, :, HALF_W:], (N_SEG, HALF_W))

        def step(i, carry):
            hr, hi = carry
            al = ta - 1 - i if descending else i
            row = pl.multiple_of(al * N_SEG, N_SEG)
            pr, pi = _cmul(ar, ai, hr, hi, False)
            hr = pr + h_ref[pl.ds(row, N_SEG), :HALF_W]
            hi = pi + h_ref[pl.ds(row, N_SEG), HALF_W:]
            h_ref[pl.ds(row, N_SEG), :HALF_W] = hr
            h_ref[pl.ds(row, N_SEG), HALF_W:] = hi
            return hr, hi

        hr, hi = lax.fori_loop(0, ta, step, (st_ref[:, :HALF_W], st_ref[:, HALF_W:]), unroll=True)
        st_ref[:, :HALF_W] = hr
        st_ref[:, HALF_W:] = hi
        yt = jnp.dot(h_ref[...].astype(BF16), c_ref[0], preferred_element_type=F32)
        for al in range(ta):
            y_ref[pl.ds(a0 + al, N_SEG, stride=seg), :] = yt[al * N_SEG:(al + 1) * N_SEG, :]

    u_spec = pl.BlockSpec((n_rows, SLAB_W), lambda s, k: (rb, s))
    b_spec = pl.BlockSpec((1, SLAB_W, STATE_W), lambda s, k: (s, 0, 0))
    c_spec = pl.BlockSpec((1, STATE_W, SLAB_W), lambda s, k: (s, 0, 0))
    a_spec = pl.BlockSpec((1, 1, STATE_W), lambda s, k: (s, 0, 0))
    st_spec = pl.BlockSpec((1, N_SEG, STATE_W), lambda s, k: (s, 0, 0))
    scratch = [pltpu.VMEM((N_SEG, STATE_W), F32), pltpu.VMEM((tile, SLAB_W), F32), pltpu.VMEM((tile, STATE_W), F32)]
    kmap = (lambda s, k: (s, nk - 1 - k, 0, 0)) if descending else (lambda s, k: (s, k, 0, 0))
    out_specs = [u_spec, pl.BlockSpec((1, 1, N_SEG, STATE_W), kmap)]
    out_shape = [jax.ShapeDtypeStruct((y_rows, D_MODEL), F32), jax.ShapeDtypeStruct((N_SLAB, nk, N_SEG, STATE_W), F32)]
    in_specs = [u_spec, b_spec, c_spec, a_spec, st_spec]
    args = [u, b_mat, c_mat, abar, h_in]
    aliases = {}
    if y_alias is not None:
        in_specs.append(pl.BlockSpec(memory_space=pl.ANY))
        args.append(y_alias)
        aliases = {5: 0}
    return pl.pallas_call(
        body, grid=(N_SLAB, nk), in_specs=in_specs, out_specs=out_specs, out_shape=out_shape, scratch_shapes=scratch,
        input_output_aliases=aliases, compiler_params=_params("parallel", "arbitrary"), name=name)(*args)


def _s5_scan_bwd(name, u, dy, n_rows, row0, b_mat, bt_mat, ct_mat, abar, h_chunks, g_in, descending,
                 du_alias=None, du_rows=None):
    seg = n_rows // N_SEG
    ta = min(32, seg)
    nk = seg // ta
    rb = row0 // n_rows
    tile = ta * N_SEG
    g_desc = not descending

    def body(*refs):
        u_ref, dy_ref, b_ref, bt_ref, ct_ref, a_ref, hch_ref, gin_ref = refs[:8]
        du_ref, db_ref, dc_ref, da_ref, st_ref, up_ref, dyp_ref, h_ref, g_ref = refs[-9:]
        k = pl.program_id(1)
        kk = nk - 1 - k if g_desc else k
        a0 = kk * ta
        ar = jnp.broadcast_to(a_ref[0, :, :HALF_W], (N_SEG, HALF_W))
        ai = jnp.broadcast_to(a_ref[0, :, HALF_W:], (N_SEG, HALF_W))

        @pl.when(k == 0)
        def _():
            st_ref[...] = gin_ref[0]

        for al in range(ta):
            dyp_ref[al * N_SEG:(al + 1) * N_SEG, :] = dy_ref[pl.ds(a0 + al, N_SEG, stride=seg), :]
            up_ref[al * N_SEG:(al + 1) * N_SEG, :] = u_ref[pl.ds(a0 + al, N_SEG, stride=seg), :]
        g_ref[...] = jnp.dot(dyp_ref[...].astype(BF16), ct_ref[0], preferred_element_type=F32)
        h_ref[...] = jnp.dot(up_ref[...].astype(BF16), b_ref[0], preferred_element_type=F32)
        h0r, h0i = hch_ref[0, 0, :, :HALF_W], hch_ref[0, 0, :, HALF_W:]

        def hstep(i, carry):
            hr, hi = carry
            al = ta - 1 - i if descending else i
            row = pl.multiple_of(al * N_SEG, N_SEG)
            pr, pi = _cmul(ar, ai, hr, hi, False)
            hr = pr + h_ref[pl.ds(row, N_SEG), :HALF_W]
            hi = pi + h_ref[pl.ds(row, N_SEG), HALF_W:]
            h_ref[pl.ds(row, N_SEG), :HALF_W] = hr
            h_ref[pl.ds(row, N_SEG), HALF_W:] = hi
            return hr, hi

        lax.fori_loop(0, ta, hstep, (h0r, h0i), unroll=True)

        def gstep(i, carry):
            gr, gi = carry
            al = ta - 1 - i if g_desc else i
            row = pl.multiple_of(al * N_SEG, N_SEG)
            pr, pi = _cmul(ar, ai, gr, gi, True)
            gr = pr + g_ref[pl.ds(row, N_SEG), :HALF_W]
            gi = pi + g_ref[pl.ds(row, N_SEG), HALF_W:]
            g_ref[pl.ds(row, N_SEG), :HALF_W] = gr
            g_ref[pl.ds(row, N_SEG), HALF_W:] = gi
            return gr, gi

        gr, gi = lax.fori_loop(0, ta, gstep, (st_ref[:, :HALF_W], st_ref[:, HALF_W:]), unroll=True)
        st_ref[:, :HALF_W] = gr
        st_ref[:, HALF_W:] = gi

        gb = g_ref[...].astype(BF16)
        dut = jnp.dot(gb, bt_ref[0], preferred_element_type=F32)
        for al in range(ta):
            du_ref[pl.ds(a0 + al, N_SEG, stride=seg), :] = dut[al * N_SEG:(al + 1) * N_SEG, :]
        tn = (((0,), (0,)), ((), ()))
        dbp = lax.dot_general(up_ref[...].astype(BF16), gb, tn, preferred_element_type=F32)
        dcp = lax.dot_general(dyp_ref[...].astype(BF16), h_ref[...].astype(BF16), tn, preferred_element_type=F32)
        inner = (ta - 1) * N_SEG
        if descending:
            g_in_r, g_in_i = g_ref[0:inner, :HALF_W], g_ref[0:inner, HALF_W:]
            p_in_r, p_in_i = h_ref[N_SEG:tile, :HALF_W], h_ref[N_SEG:tile, HALF_W:]
            g_ed_r, g_ed_i = g_ref[inner:tile, :HALF_W], g_ref[inner:tile, HALF_W:]
        else:
            g_in_r, g_in_i = g_ref[N_SEG:tile, :HALF_W], g_ref[N_SEG:tile, HALF_W:]
            p_in_r, p_in_i = h_ref[0:inner, :HALF_W], h_ref[0:inner, HALF_W:]
            g_ed_r, g_ed_i = g_ref[0:N_SEG, :HALF_W], g_ref[0:N_SEG, HALF_W:]
        dar = g_ed_r * h0r + g_ed_i * h0i
        dai = g_ed_i * h0r - g_ed_r * h0i
        if ta > 1:
            dar = dar + jnp.sum((g_in_r * p_in_r + g_in_i * p_in_i).reshape(ta - 1, N_SEG, HALF_W), axis=0)
            dai = dai + jnp.sum((g_in_i * p_in_r - g_in_r * p_in_i).reshape(ta - 1, N_SEG, HALF_W), axis=0)

        @pl.when(k == 0)
        def _():
            db_ref[0] = dbp
            dc_ref[0] = dcp
            da_ref[0, :, :HALF_W] = dar
            da_ref[0, :, HALF_W:] = dai

        @pl.when(k > 0)
        def _():
            db_ref[0] += dbp
            dc_ref[0] += dcp
            da_ref[0, :, :HALF_W] += dar
            da_ref[0, :, HALF_W:] += dai

    u_spec = pl.BlockSpec((n_rows, SLAB_W), lambda s, k: (rb, s))
    m_spec = pl.BlockSpec((1, SLAB_W, STATE_W), lambda s, k: (s, 0, 0))
    mt_spec = pl.BlockSpec((1, STATE_W, SLAB_W), lambda s, k: (s, 0, 0))
    a_spec = pl.BlockSpec((1, 1, STATE_W), lambda s, k: (s, 0, 0))
    st_spec = pl.BlockSpec((1, N_SEG, STATE_W), lambda s, k: (s, 0, 0))
    st_shape = jax.ShapeDtypeStruct((N_SLAB, N_SEG, STATE_W), F32)
    kmap = (lambda s, k: (s, nk - 1 - k, 0, 0)) if g_desc else (lambda s, k: (s, k, 0, 0))
    in_specs = [u_spec, u_spec, m_spec, mt_spec, m_spec, a_spec, pl.BlockSpec((1, 1, N_SEG, STATE_W), kmap), st_spec]
    args = [u, dy, b_mat, bt_mat, ct_mat, abar, h_chunks, g_in]
    aliases = {}
    if du_alias is not None:
        in_specs.append(pl.BlockSpec(memory_space=pl.ANY))
        args.append(du_alias)
        aliases = {8: 0}
    acc_shape = jax.ShapeDtypeStruct((N_SLAB, SLAB_W, STATE_W), F32)
    out_specs = [u_spec, m_spec, m_spec, st_spec]
    out_shape = [jax.ShapeDtypeStruct((du_rows, D_MODEL), F32), acc_shape, acc_shape, st_shape]
    scratch = [pltpu.VMEM((N_SEG, STATE_W), F32), pltpu.VMEM((tile, SLAB_W), F32), pltpu.VMEM((tile, SLAB_W), F32),
               pltpu.VMEM((tile, STATE_W), F32), pltpu.VMEM((tile, STATE_W), F32)]
    return pl.pallas_call(
        body, grid=(N_SLAB, nk), in_specs=in_specs, out_specs=out_specs, out_shape=out_shape, scratch_shapes=scratch,
        input_output_aliases=aliases, compiler_params=_params("parallel", "arbitrary"), name=name)(*args)


ROPE_HALF = HEAD_DIM // 4
TABLE_W = 2 * HEAD_DIM
Q_SCALE = 1.0 / math.sqrt(HEAD_DIM)
HEADS_PER_BLOCK = 2 * KV_REP
Q_BLOCK_W = HEADS_PER_BLOCK * HEAD_DIM


def _rope_tables(n_lat, n_ctx):
    rows = n_lat // GRID_W
    freqs = ROPE_THETA ** (-jnp.arange(ROPE_HALF, dtype=F32) / ROPE_HALF)
    ang_r = jnp.arange(rows, dtype=F32)[:, None] * freqs[None]
    ang_c = jnp.arange(GRID_W, dtype=F32)[:, None] * freqs[None]
    by_row = lambda v: jnp.repeat(v, GRID_W, axis=0)
    by_col = lambda v: jnp.tile(v, (rows, 1))
    cos = jnp.concatenate([by_row(jnp.cos(ang_r)), by_row(jnp.cos(ang_r)), by_col(jnp.cos(ang_c)), by_col(jnp.cos(ang_c))] * 2,
                          axis=1)
    sin = jnp.concatenate([by_row(jnp.sin(ang_r)), by_row(jnp.sin(ang_r)), by_col(jnp.sin(ang_c)), by_col(jnp.sin(ang_c))] * 2,
                          axis=1)
    cos = jnp.concatenate([cos, jnp.ones((n_ctx, TABLE_W), F32)], axis=0)
    sin = jnp.concatenate([sin, jnp.zeros((n_ctx, TABLE_W), F32)], axis=0)
    return cos, sin


def _rot_half(v):
    w = v.shape[1]
    ahead = pltpu.roll(v, w - ROPE_HALF, axis=1)
    behind = pltpu.roll(v, ROPE_HALF, axis=1)
    lane = lax.broadcasted_iota(jnp.int32, v.shape, 1)
    return jnp.where((lane % (2 * ROPE_HALF)) < ROPE_HALF, -ahead, behind)


def _head_mean(v, sel, selt):
    m = jnp.dot(v, sel, precision=lax.Precision.HIGH, preferred_element_type=F32) * (1.0 / HEAD_DIM)
    return jnp.dot(m, selt, precision=lax.Precision.HIGH, preferred_element_type=F32)


def _head_selectors(n_heads):
    sel = jnp.repeat(jnp.eye(n_heads, dtype=F32), HEAD_DIM, axis=0)
    return sel[None], sel.T[None]


def _head_norm(x, sel, selt):
    r = lax.rsqrt(_head_mean(x * x, sel, selt) + NORM_EPS)
    return x * r, r


def _qk_prep(proj, qn, kn, cos, sin, n, tr):
    qw, kw = _vec(jnp.tile(qn, N_Q_HEADS)), _vec(jnp.tile(kn, N_KV_HEADS))
    sq, sqt = _head_selectors(N_Q_HEADS)
    sk, skt = _head_selectors(N_KV_HEADS)

    def fn(qr, kvr, ct, st, qwv, kwv, s16, s16t, s4, s4t):
        outs = []
        for x, wv, sel, selt, scale in ((qr, qwv, s16, s16t, Q_SCALE), (kvr[:, :KV_W], kwv, s4, s4t, 1.0)):
            reps = x.shape[1] // TABLE_W
            cw, sw = jnp.tile(ct, (1, reps)), jnp.tile(st, (1, reps))
            xh, _ = _head_norm(x, sel, selt)
            nrm = xh * wv
            outs.append((nrm * cw + _rot_half(nrm) * sw) * scale)
        return outs[0], outs[1], kvr[:, KV_W:]

    return _rowwise("l1_qk_prep", fn, n, tr,
                    [(proj, 0, ATTN_W), (proj, 2 * ATTN_W // (2 * KV_W), 2 * KV_W), (cos, 0, TABLE_W), (sin, 0, TABLE_W)],
                    [qw, kw, sq, sqt, sk, skt], [(ATTN_W, BF16), (KV_W, BF16), (KV_W, BF16)], [])


def _qk_prep_bwd(proj, qn, kn, cos, sin, dq, dz, dk, dv, n, n_lat, tr):
    qw, kw = _vec(jnp.tile(qn, N_Q_HEADS)), _vec(jnp.tile(kn, N_KV_HEADS))
    sq, sqt = _head_selectors(N_Q_HEADS)
    sk, skt = _head_selectors(N_KV_HEADS)
    nlt = n_lat // tr

    def fn(flag, qr, kvr, ct, st, dqt, dzt, dkt, dvt, qwv, kwv, s16, s16t, s4, s4t):
        dxs, dws = [], []
        for x, dy, wv, sel, selt in ((qr, dqt * (flag * Q_SCALE), qwv, s16, s16t), (kvr[:, :KV_W], dkt, kwv, s4, s4t)):
            reps = x.shape[1] // TABLE_W
            cw, sw = jnp.tile(ct, (1, reps)), jnp.tile(st, (1, reps))
            xh, r = _head_norm(x, sel, selt)
            dn = dy * cw - _rot_half(dy * sw)
            dxh = dn * wv
            dxs.append(r * (dxh - xh * _head_mean(dxh * xh, sel, selt)))
            dws.append(dn * xh)
        return jnp.concatenate([dxs[0], dzt * flag, dxs[1], dvt], axis=1), dws[0], dws[1]

    dproj, dqw, dkw = _rowwise(
        "l1_qk_prep_bwd", fn, n, tr,
        [(proj, 0, ATTN_W), (proj, 2 * ATTN_W // (2 * KV_W), 2 * KV_W), (cos, 0, TABLE_W), (sin, 0, TABLE_W),
         (dq, 0, ATTN_W, "clamp", nlt), (dz, 0, ATTN_W, "clamp", nlt), (dk, 0, KV_W), (dv, 0, KV_W)],
        [qw, kw, sq, sqt, sk, skt], [(2 * ATTN_W + 2 * KV_W, BF16)], [(1, ATTN_W), (1, KV_W)], n_lat=n_lat, want_flag=True)
    return dproj, dqw.reshape(N_Q_HEADS, HEAD_DIM).sum(0)[None], dkw.reshape(N_KV_HEADS, HEAD_DIM).sum(0)[None]


NT = (((1,), (1,)), ((), ()))


def _attn_fwd(q, k, v, t, tq, tk):
    n = k.shape[0]
    nkc = n // tk

    ts = _largest_tile(tq, 256, LANES)
    items = [(sub, j) for sub in range(tq // ts) for j in range(HEADS_PER_BLOCK)]

    def body(q_ref, k_ref, v_ref, o_ref, lse_ref, s_ref, m_ref, l_ref, acc_ref):
        def lanes(j):
            g = j // KV_REP
            return slice(j * HEAD_DIM, (j + 1) * HEAD_DIM), slice(g * HEAD_DIM, (g + 1) * HEAD_DIM)

        for idx in range(len(items) + 1):
            nxt = items[idx] if idx < len(items) else None
            cur = items[idx - 1] if idx > 0 else None
            sn, sc = idx % 2, (idx - 1) % 2
            if nxt is not None:
                rows_n = slice(nxt[0] * ts, (nxt[0] + 1) * ts)
                ql_n, kl_n = lanes(nxt[1])
                qv = q_ref[rows_n, ql_n]
                m_ref[sn] = jnp.full((ts, LANES), -jnp.inf, F32)
            if cur is not None:
                rows_c = slice(cur[0] * ts, (cur[0] + 1) * ts)
                ql_c, kl_c = lanes(cur[1])
                m_row = jnp.max(m_ref[sc], axis=-1, keepdims=True)
                l_ref[...] = jnp.zeros(l_ref.shape, F32)
                acc_ref[...] = jnp.zeros(acc_ref.shape, F32)

            def sweep(kc, c):
                off = pl.multiple_of(kc * tk, tk)
                if nxt is not None:
                    s = lax.dot_general(qv, k_ref[pl.ds(off, tk), kl_n], NT, preferred_element_type=F32)
                    s_ref[sn, :, pl.ds(off, tk)] = s
                    m = m_ref[sn]
                    for cb in range(tk // LANES):
                        m = jnp.maximum(m, s[:, cb * LANES:(cb + 1) * LANES])
                    m_ref[sn] = m
                if cur is not None:
                    p = jnp.exp(s_ref[sc, :, pl.ds(off, tk)] - m_row)
                    lsum = l_ref[...]
                    for cb in range(tk // LANES):
                        lsum = lsum + p[:, cb * LANES:(cb + 1) * LANES]
                    l_ref[...] = lsum
                    acc_ref[...] += jnp.dot(p.astype(BF16), v_ref[pl.ds(off, tk), kl_c], preferred_element_type=F32)
                return c

            lax.fori_loop(0, nkc, sweep, 0, unroll=True)
            if cur is not None:
                l_row = jnp.sum(l_ref[...], axis=-1, keepdims=True)
                o_ref[rows_c, ql_c] = acc_ref[...] / l_row
                lse_ref[0, rows_c, cur[1]:cur[1] + 1] = m_row + jnp.log(l_row)

    nb = ATTN_W // Q_BLOCK_W
    kspec = pl.BlockSpec((n, LANES), lambda b, i: (0, b))
    return pl.pallas_call(
        body, grid=(nb, t // tq),
        in_specs=[pl.BlockSpec((tq, Q_BLOCK_W), lambda b, i: (i, b)), kspec, kspec],
        out_specs=[pl.BlockSpec((tq, Q_BLOCK_W), lambda b, i: (i, b)),
                   pl.BlockSpec((1, tq, HEADS_PER_BLOCK), lambda b, i: (b, i, 0))],
        out_shape=[jax.ShapeDtypeStruct((t, ATTN_W), F32), jax.ShapeDtypeStruct((nb, t, HEADS_PER_BLOCK), F32)],
        scratch_shapes=[pltpu.VMEM((2, ts, n), F32), pltpu.VMEM((2, ts, LANES), F32), pltpu.VMEM((ts, LANES), F32),
                        pltpu.VMEM((ts, HEAD_DIM), F32)],
        compiler_params=_params("parallel", "parallel"), name="attn_fwd")(q, k, v)


def _attn_bwd(q, k, v, do, o, lse, t, tq, tk):
    n = k.shape[0]
    nkc = n // tk
    tn = (((0,), (0,)), ((), ()))

    def body(q_ref, k_ref, v_ref, do_ref, o_ref, lse_ref, dq_ref, dk_ref, dv_ref, acc_ref):
        @pl.when(pl.program_id(1) == 0)
        def _():
            dk_ref[...] = jnp.zeros(dk_ref.shape, F32)
            dv_ref[...] = jnp.zeros(dv_ref.shape, F32)

        for j0 in range(0, HEADS_PER_BLOCK, 2):
            kl = slice((j0 // KV_REP) * HEAD_DIM, (j0 // KV_REP + 1) * HEAD_DIM)
            heads = []
            for a in range(2):
                j = j0 + a
                ql = slice(j * HEAD_DIM, (j + 1) * HEAD_DIM)
                qv, dov = q_ref[:, ql], do_ref[:, ql]
                dl_v = jnp.sum(dov.astype(F32) * o_ref[:, ql], axis=-1, keepdims=True)
                heads.append((ql, qv, dov, dl_v, lse_ref[0, :, j:j + 1]))
                acc_ref[a] = jnp.zeros((tq, HEAD_DIM), F32)

            def step(kc, c):
                off = pl.multiple_of(kc * tk, tk)
                kt = k_ref[pl.ds(off, tk), kl]
                vt = v_ref[pl.ds(off, tk), kl]
                dv_part, dk_part = None, None
                for a, (_, qv, dov, dl_v, lse_v) in enumerate(heads):
                    s = lax.dot_general(qv, kt, NT, preferred_element_type=F32)
                    p = jnp.exp(s - lse_v)
                    dp = lax.dot_general(dov, vt, NT, preferred_element_type=F32)
                    ds = (p * (dp - dl_v)).astype(BF16)
                    acc_ref[a] += jnp.dot(ds, kt, preferred_element_type=F32)
                    dvp = lax.dot_general(p.astype(BF16), dov, tn, preferred_element_type=F32)
                    dkp = lax.dot_general(ds, qv, tn, preferred_element_type=F32)
                    dv_part = dvp if dv_part is None else dv_part + dvp
                    dk_part = dkp if dk_part is None else dk_part + dkp
                dv_ref[pl.ds(off, tk), kl] += dv_part
                dk_ref[pl.ds(off, tk), kl] += dk_part
                return c

            lax.fori_loop(0, nkc, step, 0, unroll=2)
            for a, h in enumerate(heads):
                dq_ref[:, h[0]] = acc_ref[a]

    nb = ATTN_W // Q_BLOCK_W
    qspec = pl.BlockSpec((tq, Q_BLOCK_W), lambda b, i: (i, b))
    kspec = pl.BlockSpec((n, LANES), lambda b, i: (0, b))
    cspec = pl.BlockSpec((1, tq, HEADS_PER_BLOCK), lambda b, i: (b, i, 0))
    return pl.pallas_call(
        body, grid=(nb, t // tq), in_specs=[qspec, kspec, kspec, qspec, qspec, cspec], out_specs=[qspec, kspec, kspec],
        out_shape=[jax.ShapeDtypeStruct((t, ATTN_W), F32), jax.ShapeDtypeStruct((n, KV_W), F32),
                   jax.ShapeDtypeStruct((n, KV_W), F32)],
        scratch_shapes=[pltpu.VMEM((2, tq, HEAD_DIM), F32)],
        compiler_params=_params("parallel", "arbitrary"), name="attn_bwd")(q, k, v, do, o, lse)


def _s5_system(p, n_lat, n_ctx):
    two_g = 2 * SSM_GROUPS
    a_re = p["ssm_a_re"].reshape(two_g, SSM_STATE)
    a_im = p["ssm_a_im"].reshape(two_g, SSM_STATE)
    log_dt = p["ssm_log_dt"].reshape(two_g, 1)
    b_re = p["ssm_b_re"].reshape(two_g, SSM_STATE, SSM_GROUP).transpose(2, 0, 1)
    b_im = p["ssm_b_im"].reshape(two_g, SSM_STATE, SSM_GROUP).transpose(2, 0, 1)
    raw = (a_re, a_im, log_dt, b_re, b_im)
    abr, abi, bbr, bbi, alr, ali, acr, aci = _s5_prep(*raw, n_lat // N_SEG, n_ctx // N_SEG)
    dirs = []
    for d in range(2):
        g = slice(d * SSM_GROUPS, (d + 1) * SSM_GROUPS)
        b_mat = _slab_in_matrix(bbr[:, g].transpose(1, 2, 0), bbi[:, g].transpose(1, 2, 0))
        c_mat = _slab_out_matrix(p["ssm_c_re"][0, d], p["ssm_c_im"][0, d])
        abar = _slab_pair(abr[g], abi[g])
        tables = {}
        for part, seg in (("lat", n_lat // N_SEG), ("ctx", n_ctx // N_SEG)):
            tables["h_" + part] = _s5_pow_table(f"s5_pow_h{d}_{part}", abar, seg, d == 0, False)
            tables["g_" + part] = _s5_pow_table(f"s5_pow_g{d}_{part}", abar, seg, d == 1, True)
        dirs.append(dict(
            b=b_mat.astype(BF16), bt=b_mat.transpose(0, 2, 1).astype(BF16), b32=b_mat,
            c=c_mat.astype(BF16), ct=c_mat.transpose(0, 2, 1).astype(BF16), ct32=c_mat.transpose(0, 2, 1),
            abar=abar, a_lat=_slab_pair(alr[g], ali[g])[:, 0], a_ctx=_slab_pair(acr[g], aci[g])[:, 0], **tables))
    return raw, dirs


def _s5_forward(proj, dirs, n_lat, n_ctx):
    n = n_lat + n_ctx
    zero_c = jnp.zeros((N_SLAB, STATE_W), F32)
    ys, saved = [], []
    for d, s in enumerate(dirs):
        desc = d == 1
        tag = f"s5f{d}"
        zc = _s5_ends(tag + "_ctx_ends", proj, n_ctx, n_lat, s["h_ctx"], s["b32"])
        ent_c, h0 = _s5_carry(tag + "_ctx_carry", zc, s["a_ctx"], zero_c, desc, False)
        y, hch_c = _s5_scan(tag + "_ctx", proj, n_ctx, n_lat, s["b"], s["c"], s["abar"], ent_c, desc,
                            y_rows=n)
        zl = _s5_ends(tag + "_lat_ends", proj, n_lat, 0, s["h_lat"], s["b32"])
        ent_l, _ = _s5_carry(tag + "_lat_carry", zl, s["a_lat"], h0, desc, False)
        y, hch_l = _s5_scan(tag + "_lat", proj, n_lat, 0, s["b"], s["c"], s["abar"], ent_l, desc,
                            y_alias=y, y_rows=n)
        ys.append(y)
        saved.append((hch_l, hch_c))
    return ys, saved


def _s5_backward(proj, dy, dirs, saved, n_lat, n_ctx):
    n = n_lat + n_ctx
    zero_c = jnp.zeros((N_SLAB, STATE_W), F32)
    out = []
    for d, s in enumerate(dirs):
        desc = d == 1
        tag = f"s5b{d}"
        hch_l, hch_c = saved[d]
        gl = _s5_ends(tag + "_lat_ends", dy, n_lat, 0, s["g_lat"], s["ct32"])
        ent_l, g0 = _s5_carry(tag + "_lat_carry", gl, s["a_lat"], zero_c, not desc, True)
        du, db_l, dc_l, da_l = _s5_scan_bwd(tag + "_lat", proj, dy, n_lat, 0, s["b"], s["bt"], s["ct"], s["abar"],
                                            hch_l, ent_l, desc, du_rows=n)
        gc = _s5_ends(tag + "_ctx_ends", dy, n_ctx, n_lat, s["g_ctx"], s["ct32"])
        ent_c, _ = _s5_carry(tag + "_ctx_carry", gc, s["a_ctx"], g0, not desc, True)
        du, db_c, dc_c, da_c = _s5_scan_bwd(tag + "_ctx", proj, dy, n_ctx, n_lat, s["b"], s["bt"], s["ct"], s["abar"],
                                            hch_c, ent_c, desc, du_alias=du, du_rows=n)
        out.append((du, db_l + db_c, dc_l + dc_c, da_l + da_c))
    return out


def _s5_param_grads(raw, bwd):
    dabr, dabi, dbbr, dbbi, dcr, dci = [], [], [], [], [], []
    for _, db, dc, da in bwd:
        da = jnp.sum(da, axis=1)
        dabr.append(da[:, :HALF_W].reshape(SSM_GROUPS, SSM_STATE))
        dabi.append(da[:, HALF_W:].reshape(SSM_GROUPS, SSM_STATE))
        dbd = _slab_diag(db)
        dbbr.append(dbd[0].transpose(1, 0, 2))
        dbbi.append(dbd[1].transpose(1, 0, 2))
        dcd = _slab_diag(dc)
        dcr.append(dcd[0])
        dci.append(-dcd[1])
    cat = lambda xs, ax: jnp.concatenate(xs, axis=ax)
    dar, dai, dld, dbr, dbi = _s5_prep_bwd(*raw, cat(dabr, 0), cat(dabi, 0), cat(dbbr, 1), cat(dbbi, 1))
    shp = (1, 2, SSM_GROUPS, SSM_STATE)
    b_shape = (1, 2, SSM_GROUPS, SSM_STATE, SSM_GROUP)
    return dict(
        ssm_a_re=dar.reshape(shp), ssm_a_im=dai.reshape(shp), ssm_log_dt=dld.reshape(1, 2, SSM_GROUPS),
        ssm_b_re=dbr.transpose(1, 2, 0).reshape(b_shape), ssm_b_im=dbi.transpose(1, 2, 0).reshape(b_shape),
        ssm_c_re=jnp.stack(dcr)[None], ssm_c_im=jnp.stack(dci)[None])


def _example_step(x, ctx, target, mods, w, p):
    t, c = x.shape[0], ctx.shape[0]
    n = t + c
    assert t % c == 0 and c % LANES == 0 and c % (8 * N_SEG) == 0 and t % GRID_W == 0
    tr = _largest_tile(c, 256, 8)
    xall = (x, ctx)
    g0, g1 = _vec(p["norm_g"][0]), _vec(p["norm_g"][1])
    (shift0, scale0, gate0), (shift1, scale1, gate1) = [tuple(_vec(v) for v in m) for m in mods]

    h0 = _norm_mod_fwd("l0_norm", xall, g0, scale0, shift0, n, tr, t)
    proj0 = _mm("l0_in", h0, w["ssm_w_in"], "nn")
    raw, dirs = _s5_system(p, t, c)
    (y_f, y_r), saved = _s5_forward(proj0, dirs, t, c)
    d_skip = _vec(p["ssm_d"][0])

    def post_a(u, yf, yr, dv):
        y = u * dv + yf + yr
        return y, _gelu(y)

    y0, yg = _rowwise("l0_gelu", post_a, n, tr, [(proj0, 0, D_MODEL), (y_f, 0, D_MODEL), (y_r, 0, D_MODEL)], [d_skip],
                      [(D_MODEL, F32), (D_MODEL, F32)], [])
    tg = _mm("l0_glu", yg, w["ssm_w_glu"], "nn")
    b_glu = _vec(p["ssm_b_glu"][0])

    def post_b(ygt, tt, zt, bv):
        return ygt * _sigmoid(tt + bv) * _silu(zt)

    gz0 = _rowwise("l0_gate", post_b, n, tr, [(yg, 0, D_MODEL), (tg, 0, D_MODEL), (proj0, 1, D_MODEL)], [b_glu],
                   [(D_MODEL, BF16)], [])[0]
    out0 = _mm("l0_out", gz0, w["ssm_w_out"], "nn")

    def res_norm(xt, ot, gv, g1v, sc, sh):
        x1t = xt + gv * ot
        xh, _ = _rms_hat(x1t)
        return x1t, (xh * g1v) * (1.0 + sc) + sh

    x1, h1 = _rowwise("l0_res_l1_norm", res_norm, n, tr, [_stream(xall), (out0, 0, D_MODEL)],
                      [gate0, g1, scale1, shift1], [(D_MODEL, F32), (D_MODEL, BF16)], [], n_lat=t)
    proj1 = _mm("l1_in", h1, w["attn_w_in"], "nn")
    cos, sin = _rope_tables(t, c)
    qn, kn = p["attn_q_norm"][0], p["attn_k_norm"][0]
    q_h, k_h, v_h = _qk_prep(proj1, qn, kn, cos, sin, n, tr)
    tq = _largest_tile(t, 512, LANES)
    o, lse = _attn_fwd(q_h, k_h, v_h, t, tq, _largest_tile(n, 2816, LANES))
    gz1 = _rowwise("l1_gate", lambda ot, zt: ot * _silu(zt), t, tr, [(o, 0, D_MODEL), (proj1, 1, D_MODEL)], [],
                   [(D_MODEL, BF16)], [])[0]
    out1 = _mm("l1_out", gz1, w["attn_w_out"], "nn")

    gf = _vec(p["final_norm_g"])

    def head(x1t, o1t, tgt, g1v, gfv):
        x2 = x1t + g1v * o1t
        xh, r = _rms_hat(x2)
        e = xh * gfv - tgt
        dyf = e * (1.0 / D_MODEL)
        dx2 = _rms_bwd(xh, r, dyf * gfv)
        return dx2, g1v * dx2, dyf * xh, dx2 * o1t, jnp.sum(e * e, axis=1, keepdims=True)

    gate1_lat = gate1[0:1]
    dx2, dout1, d_gf, d_gate1, sq = _rowwise(
        "head", head, t, tr, [(x1, 0, D_MODEL), (out1, 0, D_MODEL), (target, 0, D_MODEL)], [gate1_lat, gf],
        [(D_MODEL, F32), (D_MODEL, BF16)], [(1, D_MODEL), (1, D_MODEL), (1, 1)])

    d_w_attn_out = _mm("l1_out_dw", gz1, dout1, "tn", out_dtype=BF16)
    dgz1 = _mm("l1_out_dx", dout1, w["attn_w_out"], "nt")

    def gate1_bwd(dgt, ot, zt):
        return dgt * _silu(zt), dgt * ot * _silu_grad(zt)

    do, dz1 = _rowwise("l1_gate_bwd", gate1_bwd, t, tr, [(dgz1, 0, D_MODEL), (o, 0, D_MODEL), (proj1, 1, D_MODEL)], [],
                       [(D_MODEL, BF16), (D_MODEL, F32)], [])
    dq_s, dk, dv = _attn_bwd(q_h, k_h, v_h, do, o, lse, t, tq, _largest_tile(n, 1024, LANES))
    dproj1, d_qn, d_kn = _qk_prep_bwd(proj1, qn, kn, cos, sin, dq_s, dz1, dk, dv, n, t, tr)
    d_w_attn_in = _mm("l1_in_dw", h1, dproj1, "tn", out_dtype=BF16)
    dh1 = _mm("l1_in_dx", dproj1, w["attn_w_in"], "nt")
    dx1, dout0, d_g1, d_scale1, d_shift1, d_gate0 = _norm_mod_bwd("l1_norm_bwd", x1, g1, scale1, dh1, dx2, n, tr, t,
                                                                  prev=(out0, gate0))

    d_w_out = _mm("l0_out_dw", gz0, dout0, "tn", out_dtype=BF16)
    dgz0 = _mm("l0_out_dx", dout0, w["ssm_w_out"], "nt")

    def post_b_bwd(dgt, ygt, tt, zt, bv):
        s = _sigmoid(tt + bv)
        dy2 = dgt * _silu(zt)
        dt = dy2 * ygt * s * (1.0 - s)
        return dgt * (ygt * s) * _silu_grad(zt), dt, dy2 * s, dt

    dz0, dtg, dyg_a, d_b_glu = _rowwise(
        "l0_gate_bwd", post_b_bwd, n, tr, [(dgz0, 0, D_MODEL), (yg, 0, D_MODEL), (tg, 0, D_MODEL), (proj0, 1, D_MODEL)],
        [b_glu], [(D_MODEL, BF16), (D_MODEL, BF16), (D_MODEL, F32)], [(1, D_MODEL)])
    d_w_glu = _mm("l0_glu_dw", yg, dtg, "tn", out_dtype=BF16)
    dyg_b = _mm("l0_glu_dx", dtg, w["ssm_w_glu"], "nt")

    def post_a_bwd(da, db, yt, ut, dv):
        dy = (da + db) * _gelu_grad(yt)
        return dy, dy * dv, dy * ut

    dy0, du_skip, d_d = _rowwise("l0_gelu_bwd", post_a_bwd, n, tr,
                                 [(dyg_a, 0, D_MODEL), (dyg_b, 0, D_MODEL), (y0, 0, D_MODEL), (proj0, 0, D_MODEL)], [d_skip],
                                 [(D_MODEL, F32), (D_MODEL, F32)], [(1, D_MODEL)])
    s5_bwd = _s5_backward(proj0, dy0, dirs, saved, t, c)
    dproj0 = _rowwise("l0_in_grad", lambda a, b, cc, dz: jnp.concatenate([a + b + cc, dz], axis=1), n, tr,
                      [(du_skip, 0, D_MODEL), (s5_bwd[0][0], 0, D_MODEL), (s5_bwd[1][0], 0, D_MODEL), (dz0, 0, D_MODEL)], [],
                      [(2 * D_MODEL, BF16)], [])[0]
    d_w_in = _mm("l0_in_dw", h0, dproj0, "tn", out_dtype=BF16)
    dh0 = _mm("l0_in_dx", dproj0, w["ssm_w_in"], "nt")
    dx0, d_g0, d_scale0, d_shift0 = _norm_mod_bwd("l0_norm_bwd", xall, g0, scale0, dh0, dx1, n, tr, t, dx_lat_only=True)

    big = dict(ssm_w_in=d_w_in, ssm_w_glu=d_w_glu, ssm_w_out=d_w_out, attn_w_in=d_w_attn_in, attn_w_out=d_w_attn_out)
    small = dict(
        norm_g=jnp.concatenate([d_g0[0], d_g1[0]], axis=0), ssm_d=d_d[0], ssm_b_glu=d_b_glu[0],
        attn_q_norm=d_qn, attn_k_norm=d_kn, final_norm_g=d_gf[0, 0], **_s5_param_grads(raw, s5_bwd))
    zero_v = jnp.zeros((D_MODEL,), F32)
    d_mod_lat = jnp.stack([jnp.concatenate([d_shift0[0, 0], d_scale0[0, 0], d_gate0[0, 0]]),
                           jnp.concatenate([d_shift1[0, 0], d_scale1[0, 0], d_gate1[0, 0]])])
    d_mod_ctx = jnp.stack([jnp.concatenate([d_shift0[1, 0], d_scale0[1, 0], d_gate0[1, 0]]),
                           jnp.concatenate([d_shift1[1, 0], d_scale1[1, 0], zero_v])])
    return sq[0, 0, 0], dx0, big, small, d_mod_lat, d_mod_ctx


def _adamw(name, w, g, m, v):
    rows, cols = w.shape
    tr = _largest_tile(rows, 256, 8)
    c1 = 1.0 / (1.0 - ADAM_B1 ** ADAM_STEP)
    c2 = 1.0 / (1.0 - ADAM_B2 ** ADAM_STEP)

    def fn(wt, gt, mt, vt):
        mn = ADAM_B1 * mt + (1.0 - ADAM_B1) * gt
        vn = ADAM_B2 * vt + (1.0 - ADAM_B2) * (gt * gt)
        delta = -ADAM_LR * ((mn * c1) / (jnp.sqrt(vn * c2) + ADAM_EPS) + ADAM_WD * wt)
        return delta, mn, vn

    return _rowwise(name, fn, rows, tr, [(a, 0, cols) for a in (w, g, m, v)], [], [(cols, F32)] * 3, [])


BIG = ("ssm_w_in", "ssm_w_glu", "ssm_w_out", "attn_w_in", "attn_w_out")
COL_SHARDED = ("ssm_w_in", "attn_w_in")
WEIGHTS = ("c_ctx", "w_mod", "b_mod", "norm_g", "ssm_w_in", "ssm_a_re", "ssm_a_im", "ssm_log_dt", "ssm_b_re", "ssm_b_im",
           "ssm_c_re", "ssm_c_im", "ssm_d", "ssm_w_glu", "ssm_b_glu", "ssm_w_out", "attn_w_in", "attn_q_norm",
           "attn_k_norm", "attn_w_out", "final_norm_g")
SMALL = tuple(k for k in WEIGHTS if k not in BIG and k != "w_mod")
PACK_W = 1024
COND_ROWS = 2 * N_DEV


def _attn_in_perm(x, inverse):
    a, kv = ATTN_W, 2 * KV_W
    if inverse:
        return jnp.concatenate([x[..., :a], x[..., 2 * a:], x[..., a:2 * a]], axis=-1)
    return jnp.concatenate([x[..., :a], x[..., a + kv:], x[..., a:a + kv]], axis=-1)


def _pack(arrays, dtype, row_unit):
    flat = jnp.concatenate([a.reshape(-1).astype(dtype) for a in arrays])
    rows = -(-flat.shape[0] // PACK_W)
    rows = -(-rows // row_unit) * row_unit
    flat = jnp.concatenate([flat, jnp.zeros((rows * PACK_W - flat.shape[0],), dtype)])
    return flat.reshape(rows, PACK_W)


def _unpack(buf, shapes):
    lead = buf.shape[:-2]
    flat = buf.reshape(lead + (-1,))
    out, off = [], 0
    for shp in shapes:
        size = math.prod(shp)
        out.append(flat[..., off:off + size].reshape(lead + tuple(shp)))
        off += size
    return out


def kernel(x, c, ctx, c_ctx, w_mod, b_mod, norm_g, ssm_w_in, ssm_a_re, ssm_a_im, ssm_log_dt, ssm_b_re, ssm_b_im, ssm_c_re, ssm_c_im, ssm_d, ssm_w_glu, ssm_b_glu, ssm_w_out, attn_w_in, attn_q_norm, attn_k_norm, attn_w_out, final_norm_g, loss_target, m_c_ctx, m_w_mod, m_b_mod, m_norm_g, m_ssm_w_in, m_ssm_a_re, m_ssm_a_im, m_ssm_log_dt, m_ssm_b_re, m_ssm_b_im, m_ssm_c_re, m_ssm_c_im, m_ssm_d, m_ssm_w_glu, m_ssm_b_glu, m_ssm_w_out, m_attn_w_in, m_attn_q_norm, m_attn_k_norm, m_attn_w_out, m_final_norm_g, v_c_ctx, v_w_mod, v_b_mod, v_norm_g, v_ssm_w_in, v_ssm_a_re, v_ssm_a_im, v_ssm_log_dt, v_ssm_b_re, v_ssm_b_im, v_ssm_c_re, v_ssm_c_im, v_ssm_d, v_ssm_w_glu, v_ssm_b_glu, v_ssm_w_out, v_attn_w_in, v_attn_q_norm, v_attn_k_norm, v_attn_w_out, v_final_norm_g):
    args = dict(locals())
    wts = {k: args[k] for k in WEIGHTS}
    mom_m = {k: args["m_" + k] for k in WEIGHTS}
    mom_v = {k: args["v_" + k] for k in WEIGHTS}
    mx, my, mc = lax.axis_index("x"), lax.axis_index("y"), lax.axis_index("c")
    chip = 2 * mx + my
    me = 2 * chip + mc

    halves = []
    for k in BIG:
        sh = wts[k][0]
        hr = sh.shape[0] // 2
        halves.append(lax.dynamic_slice_in_dim(sh, mc * hr, hr, axis=0))
    gathered = _gather_two_level("gather_weights", _pack(halves, BF16, 16))
    parts = _unpack(gathered, [h.shape for h in halves])
    w_full = {}
    for k, pc in zip(BIG, parts):
        hr, cols = pc.shape[1:]
        pc = pc.reshape(N_CHIP, 2, hr, cols)
        if k in COL_SHARDED:
            w_full[k] = pc.transpose(1, 2, 0, 3).reshape(2 * hr, N_CHIP * cols)
        else:
            w_full[k] = pc.reshape(N_CHIP * 2 * hr, cols)
    w_full["attn_w_in"] = _attn_in_perm(w_full["attn_w_in"], False)

    c_blk = jnp.concatenate([c, jnp.zeros((N_DEV - 1, D_MODEL), F32)], axis=0)
    c_all = _exchange("gather_c", c_blk, True)[:, 0]
    cond = jnp.concatenate([c_all, c_ctx[None], jnp.zeros((COND_ROWS - N_DEV - 1, D_MODEL), F32)], axis=0)
    s_cond, ds_cond = _rowwise("cond_silu", lambda t: (_silu(t), _silu_grad(t)), COND_ROWS, COND_ROWS, [(cond, 0, D_MODEL)], [],
                               [(D_MODEL, F32), (D_MODEL, F32)], [])
    w_mod_b = w_mod.astype(BF16)
    mcols = w_mod.shape[2]
    mod_part = jnp.stack([_mm(f"mod{i}", s_cond, w_mod_b[i], "nn") for i in range(2)])
    mod_g = _exchange("gather_mod", mod_part.reshape(2 * COND_ROWS, mcols), True)
    mod_all = mod_g.reshape(N_CHIP, 2, 2, COND_ROWS, mcols)[:, 0]
    mod_all = mod_all.transpose(1, 2, 0, 3).reshape(2, COND_ROWS, N_CHIP * mcols) + b_mod[:, None, :]
    mods = []
    for i in range(2):
        lat = lax.dynamic_slice_in_dim(mod_all[i], me, 1, axis=0)[0]
        both = jnp.stack([lat, mod_all[i, N_DEV]])
        mods.append((both[:, :D_MODEL], both[:, D_MODEL:2 * D_MODEL], both[:, 2 * D_MODEL:]))

    small_p = {k: wts[k] for k in SMALL if k != "c_ctx" and k != "b_mod"}
    sq, grad_x, big_g, small_g, d_mod_lat, d_mod_ctx = _example_step(x[0], ctx[0], loss_target[0], mods, w_full, small_p)
    loss = lax.psum(0.5 / D_MODEL * sq, ("x", "y", "c"))
    big_g["attn_w_in"] = _attn_in_perm(big_g["attn_w_in"], True)

    small_names = [k for k in SMALL if k not in ("c_ctx", "b_mod")]
    small_list = [small_g[k] for k in small_names] + [d_mod_lat, d_mod_ctx]
    small_shapes = [wts[k].shape for k in small_names] + [d_mod_lat.shape, d_mod_ctx.shape]
    packed = _pack(small_list, F32, 8 * N_DEV)
    slice_rows = packed.shape[0] // N_DEV
    slices = _exchange("scatter_small", packed.reshape(N_DEV, slice_rows, PACK_W), False)
    my_sum = _sum_slots("sum_small", slices)
    payload = jnp.concatenate([my_sum, _pack([d_mod_lat], F32, 8)], axis=0)
    sg = _exchange("gather_small", payload, True)
    summed = _unpack(sg[:, :slice_rows].reshape(packed.shape), small_shapes)
    grads = dict(zip(small_names, summed[:-2]))
    d_mod_lat_sum, d_mod_ctx_sum = summed[-2], summed[-1]
    grads["b_mod"] = d_mod_lat_sum + d_mod_ctx_sum
    d_mod_lat_all = _unpack(sg[:, slice_rows:], [d_mod_lat.shape])[0]

    g_w_mod, ds_cc = [], []
    for i in range(2):
        rows9 = jnp.concatenate([d_mod_lat_all[:, i], d_mod_ctx_sum[i][None],
                                 jnp.zeros((COND_ROWS - N_DEV - 1, 3 * D_MODEL), F32)], axis=0)
        mine = lax.dynamic_slice_in_dim(rows9, chip * mcols, mcols, axis=1)
        g_w_mod.append(_mm(f"mod{i}_dw", s_cond, mine, "tn"))
        ds_cc.append(_mm(f"mod{i}_dx", mine, w_mod_b[i], "nt")[N_DEV])
    grads["w_mod"] = jnp.stack(g_w_mod)
    part = (ds_cc[0] + ds_cc[1]) * jnp.where(mc == 0, 1.0, 0.0)
    part_blk = jnp.concatenate([part[None], jnp.zeros((N_DEV - 1, D_MODEL), F32)], axis=0)
    ds_all = _sum_slots("sum_c_ctx", _exchange("gather_c_ctx", part_blk, True))
    grads["c_ctx"] = ds_all[0] * ds_cond[N_DEV]

    blocks = []
    for k in BIG:
        g = big_g[k]
        rows, cols = g.shape
        if k in COL_SHARDED:
            blocks.append(g.reshape(2, rows // 2, N_CHIP, cols // N_CHIP).transpose(2, 0, 1, 3).reshape(N_DEV, -1))
        else:
            blocks.append(g.reshape(N_DEV, -1))
    sendbuf = jnp.concatenate(blocks, axis=1).astype(BF16)
    sendbuf = sendbuf.reshape(N_DEV, -1, PACK_W)
    recv = _exchange("scatter_big", sendbuf, False)
    mine = _sum_slots("sum_big", recv)
    both = _exchange("swap_halves", mine, True, sibling_only=True)
    half_shapes = [(wts[k].shape[1] // 2, wts[k].shape[2]) for k in BIG]
    for k, pc in zip(BIG, _unpack(both, half_shapes)):
        grads[k] = pc.reshape(wts[k].shape)

    delta, new_m, new_v = {}, {}, {}
    for k in BIG + ("w_mod",):
        shp = wts[k].shape
        two_d = (-1, shp[-1])
        res = _adamw("adamw_" + k, *[a.reshape(two_d) for a in (wts[k], grads[k], mom_m[k], mom_v[k])])
        delta[k], new_m[k], new_v[k] = [r.reshape(shp) for r in res]
    shapes = [wts[k].shape for k in SMALL]
    packed = [_pack([d[k] for k in SMALL], F32, 8) for d in (wts, grads, mom_m, mom_v)]
    res = _adamw("adamw_small", *packed)
    for dst, buf in zip((delta, new_m, new_v), res):
        for k, a in zip(SMALL, _unpack(buf, shapes)):
            dst[k] = a
    grads = {k: grads[k].reshape(wts[k].shape) for k in WEIGHTS}
    return (loss, grad_x[None], *[grads[k] for k in WEIGHTS], *[delta[k] for k in WEIGHTS],
            *[new_m[k] for k in WEIGHTS], *[new_v[k] for k in WEIGHTS])
```

```python
import functools
import math

import jax
import jax.numpy as jnp
from jax import lax
from jax.experimental import pallas as pl
from jax.experimental.pallas import tpu as pltpu

F32 = jnp.float32
BF16 = jnp.bfloat16

D_MODEL = 1024
NORM_EPS = 1e-6
SSM_GROUPS = 64
SSM_GROUP = 16
SSM_STATE = 64
LANES = 128
SLAB_W = LANES
N_SLAB = D_MODEL // SLAB_W
SLAB_GROUPS = SLAB_W // SSM_GROUP
HALF_W = SLAB_GROUPS * SSM_STATE
STATE_W = 2 * HALF_W
N_SEG = 8
HEAD_DIM = 64
N_Q_HEADS = 16
N_KV_HEADS = 4
KV_REP = N_Q_HEADS // N_KV_HEADS
ATTN_W = N_Q_HEADS * HEAD_DIM
KV_W = N_KV_HEADS * HEAD_DIM
GRID_W = 64
ROPE_THETA = 10000.0
N_DEV = 8
N_CHIP = 4
VMEM_LIMIT_BYTES = 56 * 1024 * 1024

ADAM_LR = 0.001
ADAM_B1 = 0.9
ADAM_B2 = 0.999
ADAM_EPS = 1e-08
ADAM_WD = 0.01
ADAM_STEP = 10


def _params(*sem):
    return pltpu.CompilerParams(dimension_semantics=sem, vmem_limit_bytes=VMEM_LIMIT_BYTES)


def _largest_tile(n, cap, unit):
    if n <= cap:
        return n
    t = (cap // unit) * unit
    while t >= unit:
        if n % t == 0:
            return t
        t -= unit
    raise ValueError(f"no tile for {n} (cap {cap}, unit {unit})")


def _rowwise(name, fn, n_rows, tr, row_ins, vec_ins, row_outs, red_outs, n_lat=None, want_flag=False):
    nt = n_rows // tr
    assert nt * tr == n_rows
    nlt = nt if n_lat is None else n_lat // tr

    def sel(i):
        return jnp.where(i >= nlt, 1, 0)

    arrays, in_specs, pairs = [], [], []
    for spec in row_ins:
        arr, cb, w = spec[:3]
        kind = spec[3] if len(spec) > 3 else None
        m = spec[4] if len(spec) > 4 else None
        if kind == "pair":
            arrays += [arr, m]
            in_specs += [pl.BlockSpec((tr, w), functools.partial(lambda i, cb: (jnp.minimum(i, nlt - 1), cb), cb=cb)),
                         pl.BlockSpec((tr, w), functools.partial(lambda i, cb: (jnp.maximum(i - nlt, 0), cb), cb=cb))]
            pairs.append(len(arrays) - 2)
            continue
        if kind == "mod":
            imap = functools.partial(lambda i, cb, m: (i % m, cb), cb=cb, m=m)
        elif kind == "clamp":
            imap = functools.partial(lambda i, cb, m: (jnp.minimum(i, m - 1), cb), cb=cb, m=m)
        else:
            imap = functools.partial(lambda i, cb: (i, cb), cb=cb)
        arrays.append(arr)
        in_specs.append(pl.BlockSpec((tr, w), imap))
    for v in vec_ins:
        s, a, w = v.shape
        imap = (lambda i: (sel(i), 0, 0)) if s == 2 else (lambda i: (0, 0, 0))
        arrays.append(v)
        in_specs.append(pl.BlockSpec((1, a, w), imap))
    out_shapes, out_specs = [], []
    lat_only = [len(spec) > 2 for spec in row_outs]
    for (w, dt), lat in zip([spec[:2] for spec in row_outs], lat_only):
        out_shapes.append(jax.ShapeDtypeStruct((n_lat if lat else n_rows, w), dt))
        out_specs.append(pl.BlockSpec((tr, w), (lambda i: (jnp.minimum(i, nlt - 1), 0)) if lat else (lambda i: (i, 0))))
    for s, w in red_outs:
        out_shapes.append(jax.ShapeDtypeStruct((s, 1, w), F32))
        imap = (lambda i: (sel(i), 0, 0)) if s == 2 else (lambda i: (0, 0, 0))
        out_specs.append(pl.BlockSpec((1, 1, w), imap))
    n_ri, n_vi, n_ro, n_rd = len(row_ins) + len(pairs), len(vec_ins), len(row_outs), len(red_outs)

    def body(*refs):
        i = pl.program_id(0)
        rows, k = [], 0
        while k < n_ri:
            if k in pairs:
                rows.append(jnp.where(i < nlt, refs[k][...], refs[k + 1][...]))
                k += 2
            else:
                rows.append(refs[k][...])
                k += 1
        vecs = [r[0] for r in refs[n_ri:n_ri + n_vi]]
        outs = refs[n_ri + n_vi:]
        lead = [jnp.where(i < nlt, 1.0, 0.0).astype(F32)] if want_flag else []
        res = fn(*lead, *rows, *vecs)
        if not isinstance(res, (tuple, list)):
            res = (res,)
        assert len(res) == n_ro + n_rd
        for k in range(n_ro):
            if lat_only[k]:
                @pl.when(i < nlt)
                def _(k=k):
                    outs[k][...] = res[k].astype(outs[k].dtype)
            else:
                outs[k][...] = res[k].astype(outs[k].dtype)
        for k in range(n_rd):
            part = jnp.sum(res[n_ro + k].astype(F32), axis=0, keepdims=True)
            first = i == 0
            if red_outs[k][0] == 2:
                first = jnp.logical_or(first, i == nlt)
            o = outs[n_ro + k]

            @pl.when(first)
            def _():
                o[0] = part

            @pl.when(jnp.logical_not(first))
            def _():
                o[0] = o[0] + part

    res = pl.pallas_call(
        body, grid=(nt,), in_specs=in_specs, out_specs=out_specs, out_shape=out_shapes,
        compiler_params=_params("arbitrary"), name=name)(*arrays)
    return res


def _vec(v):
    v = v.astype(F32)
    if v.ndim == 1:
        v = v[None]
    return v[:, None, :]


def _mm(name, a, b, mode, out_dtype=F32):
    if mode in ("nn", "nt"):
        m, k = a.shape
        n = b.shape[1] if mode == "nn" else b.shape[0]
        tm = _largest_tile(m, 1024, 8)
        tn = _largest_tile(n, 1024, 128)
        contract = (((1,), (0,)), ((), ())) if mode == "nn" else (((1,), (1,)), ((), ()))

        def body(a_ref, b_ref, o_ref):
            o_ref[...] = lax.dot_general(a_ref[...].astype(BF16), b_ref[...].astype(BF16), contract,
                                         preferred_element_type=F32).astype(o_ref.dtype)

        b_spec = pl.BlockSpec((k, tn), lambda i, j: (0, j)) if mode == "nn" else pl.BlockSpec((tn, k), lambda i, j: (j, 0))
        return pl.pallas_call(
            body, grid=(m // tm, n // tn),
            in_specs=[pl.BlockSpec((tm, k), lambda i, j: (i, 0)), b_spec],
            out_specs=pl.BlockSpec((tm, tn), lambda i, j: (i, j)),
            out_shape=jax.ShapeDtypeStruct((m, n), out_dtype),
            compiler_params=_params("parallel", "arbitrary"), name=name)(a, b)
    assert mode == "tn"
    r, k1 = a.shape
    k2 = b.shape[1]
    tr = _largest_tile(r, 1024, 8)
    t2 = _largest_tile(k2, 1024, 128)
    nr = r // tr

    def body(a_ref, b_ref, o_ref, acc_ref):
        part = lax.dot_general(a_ref[...].astype(BF16), b_ref[...].astype(BF16), (((0,), (0,)), ((), ())),
                               preferred_element_type=F32)
        i = pl.program_id(1)

        @pl.when(i == 0)
        def _():
            acc_ref[...] = part

        @pl.when(i > 0)
        def _():
            acc_ref[...] += part

        @pl.when(i == nr - 1)
        def _():
            o_ref[...] = acc_ref[...].astype(o_ref.dtype)

    return pl.pallas_call(
        body, grid=(k2 // t2, nr),
        in_specs=[pl.BlockSpec((tr, k1), lambda j, i: (i, 0)), pl.BlockSpec((tr, t2), lambda j, i: (i, j))],
        out_specs=pl.BlockSpec((k1, t2), lambda j, i: (0, j)),
        out_shape=jax.ShapeDtypeStruct((k1, k2), out_dtype),
        scratch_shapes=[pltpu.VMEM((k1, t2), F32)],
        compiler_params=_params("parallel", "arbitrary"), name=name)(a, b)


def _exchange(name, x, bcast, sibling_only=False):
    rels = [1] if sibling_only else list(range(1, N_DEV))
    n_slot = 2 if sibling_only else N_DEV
    blk = x.shape if bcast else x.shape[1:]

    def body(x_ref, o_ref, send_sems, recv_sems, local_sem):
        mx, my, mc = lax.axis_index("x"), lax.axis_index("y"), lax.axis_index("c")
        me = mc if sibling_only else 4 * mx + 2 * my + mc
        me_dev = 4 * mx + 2 * my + mc
        mine = pltpu.make_async_copy(x_ref if bcast else x_ref.at[me_dev], o_ref.at[me], local_sem)
        mine.start()
        copies = []
        for k, r in enumerate(rels):
            px = 1 - mx if (r >> 2) & 1 else mx
            py = 1 - my if (r >> 1) & 1 else my
            pc = 1 - mc if r & 1 else mc
            src = x_ref if bcast else x_ref.at[4 * px + 2 * py + pc]
            cp = pltpu.make_async_remote_copy(
                src_ref=src, dst_ref=o_ref.at[me], send_sem=send_sems.at[k], recv_sem=recv_sems.at[k],
                device_id=(px, py, pc), device_id_type=pl.DeviceIdType.MESH)
            cp.start()
            copies.append(cp)
        for cp in copies:
            cp.wait()
        mine.wait()

    return pl.pallas_call(
        body, out_shape=jax.ShapeDtypeStruct((n_slot,) + tuple(blk), x.dtype),
        in_specs=[pl.BlockSpec(memory_space=pltpu.VMEM if sibling_only else pl.ANY)],
        out_specs=pl.BlockSpec(memory_space=pl.ANY),
        scratch_shapes=[pltpu.SemaphoreType.DMA((len(rels),)), pltpu.SemaphoreType.DMA((len(rels),)),
                        pltpu.SemaphoreType.DMA],
        name=name)(x)


def _gather_two_level(name, x):
    def body(x_ref, o_ref, send_sems, recv_sems, local_sem):
        mx, my, mc = lax.axis_index("x"), lax.axis_index("y"), lax.axis_index("c")
        me, sibling = (mx, my, mc), (mx, my, 1 - mc)
        chips = [(1 - mx, my), (mx, 1 - my), (1 - mx, 1 - my)]

        def slot(px, py, pc):
            return o_ref.at[4 * px + 2 * py + pc]

        def copy(k, block, to, src=None):
            return pltpu.make_async_remote_copy(
                src_ref=slot(*block) if src is None else src, dst_ref=slot(*block), send_sem=send_sems.at[k],
                recv_sem=recv_sems.at[k], device_id=to, device_id_type=pl.DeviceIdType.MESH)

        mine = pltpu.make_async_copy(x_ref, slot(*me), local_sem)
        mine.start()
        first = [copy(0, me, sibling, src=x_ref)]
        first += [copy(1 + j, me, (*chip, mc), src=x_ref) for j, chip in enumerate(chips)]
        for cp in first:
            cp.start()
        passed = [copy(4 + j, (*chip, mc), sibling) for j, chip in enumerate(chips)]
        for j, chip in enumerate(chips):
            copy(1 + j, (*chip, mc), me).wait_recv()
            passed[j].start()
        copy(0, sibling, me).wait_recv()
        for j, chip in enumerate(chips):
            copy(4 + j, (*chip, 1 - mc), me).wait_recv()
        for cp in first + passed:
            cp.wait_send()
        mine.wait()

    return pl.pallas_call(
        body, out_shape=jax.ShapeDtypeStruct((N_DEV,) + tuple(x.shape), x.dtype),
        in_specs=[pl.BlockSpec(memory_space=pl.ANY)], out_specs=pl.BlockSpec(memory_space=pl.ANY),
        scratch_shapes=[pltpu.SemaphoreType.DMA((N_DEV - 1,)), pltpu.SemaphoreType.DMA((N_DEV - 1,)),
                        pltpu.SemaphoreType.DMA],
        name=name)(x)


def _sum_slots(name, x):
    s, r, w = x.shape
    tr = _largest_tile(r, 256, 8)

    def body(x_ref, o_ref):
        acc = x_ref[0].astype(F32)
        for j in range(1, s):
            acc = acc + x_ref[j].astype(F32)
        o_ref[...] = acc

    return pl.pallas_call(
        body, grid=(r // tr,), in_specs=[pl.BlockSpec((s, tr, w), lambda i: (0, i, 0))],
        out_specs=pl.BlockSpec((tr, w), lambda i: (i, 0)), out_shape=jax.ShapeDtypeStruct((r, w), F32),
        compiler_params=_params("parallel"), name=name)(x)


def _sigmoid(x):
    return 1.0 / (1.0 + jnp.exp(-x))


def _silu(x):
    return x * _sigmoid(x)


def _silu_grad(x):
    s = _sigmoid(x)
    return s * (1.0 + x * (1.0 - s))


_INV_SQRT2 = 1.0 / math.sqrt(2.0)
_INV_SQRT2PI = 1.0 / math.sqrt(2.0 * math.pi)


def _gelu(x):
    return 0.5 * x * (1.0 + lax.erf(x * _INV_SQRT2))


def _gelu_grad(x):
    return 0.5 * (1.0 + lax.erf(x * _INV_SQRT2)) + x * jnp.exp(-0.5 * x * x) * _INV_SQRT2PI


def _rms_hat(x):
    r = lax.rsqrt(jnp.mean(x * x, axis=-1, keepdims=True) + NORM_EPS)
    return x * r, r


def _rms_bwd(xh, r, dxh):
    return r * (dxh - xh * jnp.mean(dxh * xh, axis=-1, keepdims=True))


def _stream(x):
    return (x[0], 0, D_MODEL, "pair", x[1]) if isinstance(x, tuple) else (x, 0, D_MODEL)


def _norm_mod_fwd(name, x, g, scale, shift, n_rows, tr, n_lat):
    def fn(xt, gv, sc, sh):
        xh, _ = _rms_hat(xt)
        return (xh * gv) * (1.0 + sc) + sh

    return _rowwise(name, fn, n_rows, tr, [_stream(x)], [g, scale, shift], [(D_MODEL, BF16)], [], n_lat=n_lat)[0]


def _norm_mod_bwd(name, x, g, scale, dh, dres, n_rows, tr, n_lat, prev=None, dx_lat_only=False):
    nlt = n_lat // tr

    def fn(flag, xt, dht, drt, *rest):
        gv, sc = rest[-2:] if prev is None else rest[1:3]
        xh, r = _rms_hat(xt)
        n = xh * gv
        dn = dht * (1.0 + sc)
        dx = _rms_bwd(xh, r, dn * gv) + flag * drt
        if prev is None:
            return dx, dn * xh, dht * n, dht
        return dx, rest[3] * dx, dn * xh, dht * n, dht, dx * rest[0]

    rows = [_stream(x), (dh, 0, D_MODEL), (dres, 0, D_MODEL, "clamp", nlt)]
    row_outs = [(D_MODEL, F32, "lat") if dx_lat_only else (D_MODEL, F32)]
    vecs, reds = [g, scale], [(1, D_MODEL), (2, D_MODEL), (2, D_MODEL)]
    if prev is not None:
        rows.append((prev[0], 0, D_MODEL))
        vecs.append(prev[1])
        row_outs.append((D_MODEL, BF16))
        reds.append((2, D_MODEL))
    return _rowwise(name, fn, n_rows, tr, rows, vecs, row_outs, reds, n_lat=n_lat, want_flag=True)


def _s5_prep(a_re, a_im, log_dt, b_re, b_im, seg_lat, seg_ctx):
    def body(ar_ref, ai_ref, ld_ref, br_ref, bi_ref, abr_ref, abi_ref, bbr_ref, bbi_ref, alr_ref, ali_ref, acr_ref,
             aci_ref):
        lr, li = ar_ref[...], ai_ref[...]
        dt = jnp.exp(ld_ref[...])
        ldr, ldi = lr * dt, li * dt
        e = jnp.exp(ldr)
        abr, abi = e * jnp.cos(ldi), e * jnp.sin(ldi)
        abr_ref[...] = abr
        abi_ref[...] = abi
        den = lr * lr + li * li
        nr, ni = abr - 1.0, abi
        qr = (nr * lr + ni * li) / den
        qi = (ni * lr - nr * li) / den
        br, bi = br_ref[...], bi_ref[...]
        bbr_ref[...] = qr[None] * br - qi[None] * bi
        bbi_ref[...] = qr[None] * bi + qi[None] * br
        for seg, r_ref, i_ref in ((seg_lat, alr_ref, ali_ref), (seg_ctx, acr_ref, aci_ref)):
            es = jnp.exp(ldr * float(seg))
            r_ref[...] = es * jnp.cos(ldi * float(seg))
            i_ref[...] = es * jnp.sin(ldi * float(seg))

    sm = jax.ShapeDtypeStruct(a_re.shape, F32)
    big = jax.ShapeDtypeStruct(b_re.shape, F32)
    return pl.pallas_call(body, out_shape=[sm, sm, big, big, sm, sm, sm, sm], name="s5_prep")(
        a_re, a_im, log_dt, b_re, b_im)


def _s5_prep_bwd(a_re, a_im, log_dt, b_re, b_im, dabr, dabi, dbbr, dbbi):
    def body(ar_ref, ai_ref, ld_ref, br_ref, bi_ref, dabr_ref, dabi_ref, dbbr_ref, dbbi_ref,
             dar_ref, dai_ref, dld_ref, dbr_ref, dbi_ref):
        lr, li = ar_ref[...], ai_ref[...]
        dt = jnp.exp(ld_ref[...])
        ldr, ldi = lr * dt, li * dt
        e = jnp.exp(ldr)
        abr, abi = e * jnp.cos(ldi), e * jnp.sin(ldi)
        den = lr * lr + li * li
        nr, ni = abr - 1.0, abi
        qr = (nr * lr + ni * li) / den
        qi = (ni * lr - nr * li) / den
        br, bi = br_ref[...], bi_ref[...]
        gbr, gbi = dbbr_ref[...], dbbi_ref[...]
        dbr_ref[...] = gbr * qr[None] + gbi * qi[None]
        dbi_ref[...] = gbi * qr[None] - gbr * qi[None]
        dqr = jnp.sum(gbr * br + gbi * bi, axis=0)
        dqi = jnp.sum(gbi * br - gbr * bi, axis=0)
        dnr = (dqr * lr - dqi * li) / den
        dni = (dqr * li + dqi * lr) / den
        dlr_q = (dqr * (nr - 2.0 * lr * qr) + dqi * (ni - 2.0 * lr * qi)) / den
        dli_q = (dqr * (ni - 2.0 * li * qr) + dqi * (-nr - 2.0 * li * qi)) / den
        gar = dabr_ref[...] + dnr
        gai = dabi_ref[...] + dni
        dldr = gar * abr + gai * abi
        dldi = gai * abr - gar * abi
        dar_ref[...] = dldr * dt + dlr_q
        dai_ref[...] = dldi * dt + dli_q
        ddt = jnp.sum(dldr * lr + dldi * li, axis=1, keepdims=True)
        dld_ref[...] = ddt * dt

    sm = jax.ShapeDtypeStruct(a_re.shape, F32)
    big = jax.ShapeDtypeStruct(b_re.shape, F32)
    return pl.pallas_call(body, out_shape=[sm, sm, jax.ShapeDtypeStruct(log_dt.shape, F32), big, big],
                          name="s5_prep_bwd")(a_re, a_im, log_dt, b_re, b_im, dabr, dabi, dbbr, dbbi)


def _slab_cols(v):
    return v.reshape(N_SLAB, 1, HALF_W)


def _slab_pair(vr, vi):
    return jnp.concatenate([_slab_cols(vr), _slab_cols(vi)], axis=-1)


def _slab_in_matrix(bbr, bbi):
    eye = jnp.eye(SLAB_GROUPS, dtype=F32)

    def one(b):
        b = b.reshape(N_SLAB, SLAB_GROUPS, SSM_STATE, SSM_GROUP)
        m = jnp.einsum("sgph,gk->sghkp", b, eye)
        return m.reshape(N_SLAB, SLAB_W, HALF_W)

    return jnp.concatenate([one(bbr), one(bbi)], axis=-1)


def _slab_out_matrix(cr, ci):
    eye = jnp.eye(SLAB_GROUPS, dtype=F32)

    def one(c):
        c = c.reshape(N_SLAB, SLAB_GROUPS, SSM_GROUP, SSM_STATE)
        m = jnp.einsum("sghp,gk->skpgh", c, eye)
        return m.reshape(N_SLAB, HALF_W, SLAB_W)

    return jnp.concatenate([one(cr), one(-ci)], axis=1)


def _slab_diag(m):
    m = m.reshape(N_SLAB, SLAB_GROUPS, SSM_GROUP, 2, SLAB_GROUPS, SSM_STATE)
    d = jnp.stack([m[:, g, :, :, g, :] for g in range(SLAB_GROUPS)], axis=1)
    return d.transpose(3, 0, 1, 2, 4).reshape(2, SSM_GROUPS, SSM_GROUP, SSM_STATE)


def _cmul(ar, ai, xr, xi, conj):
    if conj:
        return ar * xr + ai * xi, ar * xi - ai * xr
    return ar * xr - ai * xi, ar * xi + ai * xr


def _s5_pow_table(name, abar, seg, falling, conj):
    assert seg >= 8 and seg & (seg - 1) == 0

    def body(a_ref, o_ref, t_ref):
        ar, ai = a_ref[0, :, :HALF_W], a_ref[0, :, HALF_W:]
        if conj:
            ai = -ai
        rr, ri = [jnp.ones_like(ar)], [jnp.zeros_like(ai)]
        for _ in range(7):
            pr, pi = _cmul(ar, ai, rr[-1], ri[-1], False)
            rr.append(pr)
            ri.append(pi)
        sr, si = _cmul(ar, ai, rr[-1], ri[-1], False)
        if falling:
            rr, ri = rr[::-1], ri[::-1]
        first = slice(seg - 8, seg) if falling else slice(0, 8)
        t_ref[first, :HALF_W] = jnp.concatenate(rr, axis=0)
        t_ref[first, HALF_W:] = jnp.concatenate(ri, axis=0)
        size = 8
        while size < seg:
            src = slice(seg - size, seg) if falling else slice(0, size)
            dst = slice(seg - 2 * size, seg - size) if falling else slice(size, 2 * size)
            pr, pi = _cmul(sr, si, t_ref[src, :HALF_W], t_ref[src, HALF_W:], False)
            t_ref[dst, :HALF_W] = pr
            t_ref[dst, HALF_W:] = pi
            sr, si = _cmul(sr, si, sr, si, False)
            size *= 2
        o_ref[0] = t_ref[...].astype(BF16)

    return pl.pallas_call(
        body, grid=(N_SLAB,), in_specs=[pl.BlockSpec((1, 1, STATE_W), lambda s: (s, 0, 0))],
        out_specs=pl.BlockSpec((1, seg, STATE_W), lambda s: (s, 0, 0)),
        out_shape=jax.ShapeDtypeStruct((N_SLAB, seg, STATE_W), BF16),
        scratch_shapes=[pltpu.VMEM((seg, STATE_W), F32)], compiler_params=_params("parallel"), name=name)(abar)


def _s5_ends(name, x, n_rows, row0, table, m_mat):
    seg = n_rows // N_SEG
    rb = row0 // n_rows
    tn = (((0,), (0,)), ((), ()))

    def body(x_ref, t_ref, m_ref, z_ref):
        mr, mi = m_ref[0, :, :HALF_W], m_ref[0, :, HALF_W:]
        for j in range(N_SEG):
            t = lax.dot_general(x_ref[j * seg:(j + 1) * seg, :].astype(BF16), t_ref[0], tn,
                                preferred_element_type=F32)
            tr_, ti_ = t[:, :HALF_W], t[:, HALF_W:]
            z_ref[0, j:j + 1, :HALF_W] = jnp.sum(mr * tr_ - mi * ti_, axis=0, keepdims=True)
            z_ref[0, j:j + 1, HALF_W:] = jnp.sum(mr * ti_ + mi * tr_, axis=0, keepdims=True)

    return pl.pallas_call(
        body, grid=(N_SLAB,),
        in_specs=[pl.BlockSpec((n_rows, SLAB_W), lambda s: (rb, s)),
                  pl.BlockSpec((1, seg, STATE_W), lambda s: (s, 0, 0)),
                  pl.BlockSpec((1, SLAB_W, STATE_W), lambda s: (s, 0, 0))],
        out_specs=pl.BlockSpec((1, N_SEG, STATE_W), lambda s: (s, 0, 0)),
        out_shape=jax.ShapeDtypeStruct((N_SLAB, N_SEG, STATE_W), F32),
        compiler_params=_params("parallel"), name=name)(x, table, m_mat)


def _s5_carry(name, z, a_seg, init, descending, conj):
    order = list(range(N_SEG - 1, -1, -1)) if descending else list(range(N_SEG))

    def body(z_ref, a_ref, i_ref, e_ref, o_ref):
        ar, ai = a_ref[:, :HALF_W], a_ref[:, HALF_W:]
        cr, ci = i_ref[:, :HALF_W], i_ref[:, HALF_W:]
        for j in order:
            e_ref[:, j, :HALF_W] = cr
            e_ref[:, j, HALF_W:] = ci
            pr, pi = _cmul(ar, ai, cr, ci, conj)
            cr = pr + z_ref[:, j, :HALF_W]
            ci = pi + z_ref[:, j, HALF_W:]
        o_ref[:, :HALF_W] = cr
        o_ref[:, HALF_W:] = ci

    return pl.pallas_call(body, out_shape=[jax.ShapeDtypeStruct(z.shape, F32), jax.ShapeDtypeStruct(init.shape, F32)],
                          name=name)(z, a_seg, init)


def _s5_scan(name, u, n_rows, row0, b_mat, c_mat, abar, h_in, descending, y_alias=None, y_rows=None):
    seg = n_rows // N_SEG
    ta = min(32, seg)
    nk = seg // ta
    assert seg * N_SEG == n_rows and nk * ta == seg and row0 % n_rows == 0 and ta % 8 == 0
    rb = row0 // n_rows
    tile = ta * N_SEG

    def body(*refs):
        u_ref, b_ref, c_ref, a_ref, hin_ref = refs[:5]
        y_ref, hch_ref, st_ref, up_ref, h_ref = refs[-5:]
        k = pl.program_id(1)
        kk = nk - 1 - k if descending else k
        a0 = kk * ta

        @pl.when(k == 0)
        def _():
            st_ref[...] = hin_ref[0]

        hch_ref[0, 0] = st_ref[...]
        for al in range(ta):
            up_ref[al * N_SEG:(al + 1) * N_SEG, :] = u_ref[pl.ds(a0 + al, N_SEG, stride=seg), :]
        h_ref[...] = jnp.dot(up_ref[...].astype(BF16), b_ref[0], preferred_element_type=F32)
        ar = jnp.broadcast_to(a_ref[0, :, :HALF_W], (N_SEG, HALF_W))
        ai = jnp.broadcast_to(a_ref[0, :, HALF_W:], (N_SEG, HALF_W))

        def step(i, carry):
            hr, hi = carry
            al = ta - 1 - i if descending else i
            row = pl.multiple_of(al * N_SEG, N_SEG)
            pr, pi = _cmul(ar, ai, hr, hi, False)
            hr = pr + h_ref[pl.ds(row, N_SEG), :HALF_W]
            hi = pi + h_ref[pl.ds(row, N_SEG), HALF_W:]
            h_ref[pl.ds(row, N_SEG), :HALF_W] = hr
            h_ref[pl.ds(row, N_SEG), HALF_W:] = hi
            return hr, hi

        hr, hi = lax.fori_loop(0, ta, step, (st_ref[:, :HALF_W], st_ref[:, HALF_W:]), unroll=True)
        st_ref[:, :HALF_W] = hr
        st_ref[:, HALF_W:] = hi
        yt = jnp.dot(h_ref[...].astype(BF16), c_ref[0], preferred_element_type=F32)
        for al in range(ta):
            y_ref[pl.ds(a0 + al, N_SEG, stride=seg), :] = yt[al * N_SEG:(al + 1) * N_SEG, :]

    u_spec = pl.BlockSpec((n_rows, SLAB_W), lambda s, k: (rb, s))
    b_spec = pl.BlockSpec((1, SLAB_W, STATE_W), lambda s, k: (s, 0, 0))
    c_spec = pl.BlockSpec((1, STATE_W, SLAB_W), lambda s, k: (s, 0, 0))
    a_spec = pl.BlockSpec((1, 1, STATE_W), lambda s, k: (s, 0, 0))
    st_spec = pl.BlockSpec((1, N_SEG, STATE_W), lambda s, k: (s, 0, 0))
    scratch = [pltpu.VMEM((N_SEG, STATE_W), F32), pltpu.VMEM((tile, SLAB_W), F32), pltpu.VMEM((tile, STATE_W), F32)]
    kmap = (lambda s, k: (s, nk - 1 - k, 0, 0)) if descending else (lambda s, k: (s, k, 0, 0))
    out_specs = [u_spec, pl.BlockSpec((1, 1, N_SEG, STATE_W), kmap)]
    out_shape = [jax.ShapeDtypeStruct((y_rows, D_MODEL), F32), jax.ShapeDtypeStruct((N_SLAB, nk, N_SEG, STATE_W), F32)]
    in_specs = [u_spec, b_spec, c_spec, a_spec, st_spec]
    args = [u, b_mat, c_mat, abar, h_in]
    aliases = {}
    if y_alias is not None:
        in_specs.append(pl.BlockSpec(memory_space=pl.ANY))
        args.append(y_alias)
        aliases = {5: 0}
    return pl.pallas_call(
        body, grid=(N_SLAB, nk), in_specs=in_specs, out_specs=out_specs, out_shape=out_shape, scratch_shapes=scratch,
        input_output_aliases=aliases, compiler_params=_params("parallel", "arbitrary"), name=name)(*args)


def _s5_scan_bwd(name, u, dy, n_rows, row0, b_mat, bt_mat, ct_mat, abar, h_chunks, g_in, descending,
                 du_alias=None, du_rows=None):
    seg = n_rows // N_SEG
    ta = min(32, seg)
    nk = seg // ta
    rb = row0 // n_rows
    tile = ta * N_SEG
    g_desc = not descending

    def body(*refs):
        u_ref, dy_ref, b_ref, bt_ref, ct_ref, a_ref, hch_ref, gin_ref = refs[:8]
        du_ref, db_ref, dc_ref, da_ref, st_ref, up_ref, dyp_ref, h_ref, g_ref = refs[-9:]
        k = pl.program_id(1)
        kk = nk - 1 - k if g_desc else k
        a0 = kk * ta
        ar = jnp.broadcast_to(a_ref[0, :, :HALF_W], (N_SEG, HALF_W))
        ai = jnp.broadcast_to(a_ref[0, :, HALF_W:], (N_SEG, HALF_W))

        @pl.when(k == 0)
        def _():
            st_ref[...] = gin_ref[0]

        for al in range(ta):
            dyp_ref[al * N_SEG:(al + 1) * N_SEG, :] = dy_ref[pl.ds(a0 + al, N_SEG, stride=seg), :]
            up_ref[al * N_SEG:(al + 1) * N_SEG, :] = u_ref[pl.ds(a0 + al, N_SEG, stride=seg), :]
        g_ref[...] = jnp.dot(dyp_ref[...].astype(BF16), ct_ref[0], preferred_element_type=F32)
        h_ref[...] = jnp.dot(up_ref[...].astype(BF16), b_ref[0], preferred_element_type=F32)
        h0r, h0i = hch_ref[0, 0, :, :HALF_W], hch_ref[0, 0, :, HALF_W:]

        def hstep(i, carry):
            hr, hi = carry
            al = ta - 1 - i if descending else i
            row = pl.multiple_of(al * N_SEG, N_SEG)
            pr, pi = _cmul(ar, ai, hr, hi, False)
            hr = pr + h_ref[pl.ds(row, N_SEG), :HALF_W]
            hi = pi + h_ref[pl.ds(row, N_SEG), HALF_W:]
            h_ref[pl.ds(row, N_SEG), :HALF_W] = hr
            h_ref[pl.ds(row, N_SEG), HALF_W:] = hi
            return hr, hi

        lax.fori_loop(0, ta, hstep, (h0r, h0i), unroll=True)

        def gstep(i, carry):
            gr, gi = carry
            al = ta - 1 - i if g_desc else i
            row = pl.multiple_of(al * N_SEG, N_SEG)
            pr, pi = _cmul(ar, ai, gr, gi, True)
            gr = pr + g_ref[pl.ds(row, N_SEG), :HALF_W]
            gi = pi + g_ref[pl.ds(row, N_SEG), HALF_W:]
            g_ref[pl.ds(row, N_SEG), :HALF_W] = gr
            g_ref[pl.ds(row, N_SEG), HALF_W:] = gi
            return gr, gi

        gr, gi = lax.fori_loop(0, ta, gstep, (st_ref[:, :HALF_W], st_ref[:, HALF_W:]), unroll=True)
        st_ref[:, :HALF_W] = gr
        st_ref[:, HALF_W:] = gi

        gb = g_ref[...].astype(BF16)
        dut = jnp.dot(gb, bt_ref[0], preferred_element_type=F32)
        for al in range(ta):
            du_ref[pl.ds(a0 + al, N_SEG, stride=seg), :] = dut[al * N_SEG:(al + 1) * N_SEG, :]
        tn = (((0,), (0,)), ((), ()))
        dbp = lax.dot_general(up_ref[...].astype(BF16), gb, tn, preferred_element_type=F32)
        dcp = lax.dot_general(dyp_ref[...].astype(BF16), h_ref[...].astype(BF16), tn, preferred_element_type=F32)
        inner = (ta - 1) * N_SEG
        if descending:
            g_in_r, g_in_i = g_ref[0:inner, :HALF_W], g_ref[0:inner, HALF_W:]
            p_in_r, p_in_i = h_ref[N_SEG:tile, :HALF_W], h_ref[N_SEG:tile, HALF_W:]
            g_ed_r, g_ed_i = g_ref[inner:tile, :HALF_W], g_ref[inner:tile, HALF_W:]
        else:
            g_in_r, g_in_i = g_ref[N_SEG:tile, :HALF_W], g_ref[N_SEG:tile, HALF_W:]
            p_in_r, p_in_i = h_ref[0:inner, :HALF_W], h_ref[0:inner, HALF_W:]
            g_ed_r, g_ed_i = g_ref[0:N_SEG, :HALF_W], g_ref[0:N_SEG, HALF_W:]
        dar = g_ed_r * h0r + g_ed_i * h0i
        dai = g_ed_i * h0r - g_ed_r * h0i
        if ta > 1:
            dar = dar + jnp.sum((g_in_r * p_in_r + g_in_i * p_in_i).reshape(ta - 1, N_SEG, HALF_W), axis=0)
            dai = dai + jnp.sum((g_in_i * p_in_r - g_in_r * p_in_i).reshape(ta - 1, N_SEG, HALF_W), axis=0)

        @pl.when(k == 0)
        def _():
            db_ref[0] = dbp
            dc_ref[0] = dcp
            da_ref[0, :, :HALF_W] = dar
            da_ref[0, :, HALF_W:] = dai

        @pl.when(k > 0)
        def _():
            db_ref[0] += dbp
            dc_ref[0] += dcp
            da_ref[0, :, :HALF_W] += dar
            da_ref[0, :, HALF_W:] += dai

    u_spec = pl.BlockSpec((n_rows, SLAB_W), lambda s, k: (rb, s))
    m_spec = pl.BlockSpec((1, SLAB_W, STATE_W), lambda s, k: (s, 0, 0))
    mt_spec = pl.BlockSpec((1, STATE_W, SLAB_W), lambda s, k: (s, 0, 0))
    a_spec = pl.BlockSpec((1, 1, STATE_W), lambda s, k: (s, 0, 0))
    st_spec = pl.BlockSpec((1, N_SEG, STATE_W), lambda s, k: (s, 0, 0))
    st_shape = jax.ShapeDtypeStruct((N_SLAB, N_SEG, STATE_W), F32)
    kmap = (lambda s, k: (s, nk - 1 - k, 0, 0)) if g_desc else (lambda s, k: (s, k, 0, 0))
    in_specs = [u_spec, u_spec, m_spec, mt_spec, m_spec, a_spec, pl.BlockSpec((1, 1, N_SEG, STATE_W), kmap), st_spec]
    args = [u, dy, b_mat, bt_mat, ct_mat, abar, h_chunks, g_in]
    aliases = {}
    if du_alias is not None:
        in_specs.append(pl.BlockSpec(memory_space=pl.ANY))
        args.append(du_alias)
        aliases = {8: 0}
    acc_shape = jax.ShapeDtypeStruct((N_SLAB, SLAB_W, STATE_W), F32)
    out_specs = [u_spec, m_spec, m_spec, st_spec]
    out_shape = [jax.ShapeDtypeStruct((du_rows, D_MODEL), F32), acc_shape, acc_shape, st_shape]
    scratch = [pltpu.VMEM((N_SEG, STATE_W), F32), pltpu.VMEM((tile, SLAB_W), F32), pltpu.VMEM((tile, SLAB_W), F32),
               pltpu.VMEM((tile, STATE_W), F32), pltpu.VMEM((tile, STATE_W), F32)]
    return pl.pallas_call(
        body, grid=(N_SLAB, nk), in_specs=in_specs, out_specs=out_specs, out_shape=out_shape, scratch_shapes=scratch,
        input_output_aliases=aliases, compiler_params=_params("parallel", "arbitrary"), name=name)(*args)


ROPE_HALF = HEAD_DIM // 4
TABLE_W = 2 * HEAD_DIM
Q_SCALE = 1.0 / math.sqrt(HEAD_DIM)
HEADS_PER_BLOCK = 2 * KV_REP
Q_BLOCK_W = HEADS_PER_BLOCK * HEAD_DIM


def _rope_tables(n_lat, n_ctx):
    rows = n_lat // GRID_W
    freqs = ROPE_THETA ** (-jnp.arange(ROPE_HALF, dtype=F32) / ROPE_HALF)
    ang_r = jnp.arange(rows, dtype=F32)[:, None] * freqs[None]
    ang_c = jnp.arange(GRID_W, dtype=F32)[:, None] * freqs[None]
    by_row = lambda v: jnp.repeat(v, GRID_W, axis=0)
    by_col = lambda v: jnp.tile(v, (rows, 1))
    cos = jnp.concatenate([by_row(jnp.cos(ang_r)), by_row(jnp.cos(ang_r)), by_col(jnp.cos(ang_c)), by_col(jnp.cos(ang_c))] * 2,
                          axis=1)
    sin = jnp.concatenate([by_row(jnp.sin(ang_r)), by_row(jnp.sin(ang_r)), by_col(jnp.sin(ang_c)), by_col(jnp.sin(ang_c))] * 2,
                          axis=1)
    cos = jnp.concatenate([cos, jnp.ones((n_ctx, TABLE_W), F32)], axis=0)
    sin = jnp.concatenate([sin, jnp.zeros((n_ctx, TABLE_W), F32)], axis=0)
    return cos, sin


def _rot_half(v):
    w = v.shape[1]
    ahead = pltpu.roll(v, w - ROPE_HALF, axis=1)
    behind = pltpu.roll(v, ROPE_HALF, axis=1)
    lane = lax.broadcasted_iota(jnp.int32, v.shape, 1)
    return jnp.where((lane % (2 * ROPE_HALF)) < ROPE_HALF, -ahead, behind)


def _head_mean(v, sel, selt):
    m = jnp.dot(v, sel, precision=lax.Precision.HIGH, preferred_element_type=F32) * (1.0 / HEAD_DIM)
    return jnp.dot(m, selt, precision=lax.Precision.HIGH, preferred_element_type=F32)


def _head_selectors(n_heads):
    sel = jnp.repeat(jnp.eye(n_heads, dtype=F32), HEAD_DIM, axis=0)
    return sel[None], sel.T[None]


def _head_norm(x, sel, selt):
    r = lax.rsqrt(_head_mean(x * x, sel, selt) + NORM_EPS)
    return x * r, r


def _qk_prep(proj, qn, kn, cos, sin, n, tr):
    qw, kw = _vec(jnp.tile(qn, N_Q_HEADS)), _vec(jnp.tile(kn, N_KV_HEADS))
    sq, sqt = _head_selectors(N_Q_HEADS)
    sk, skt = _head_selectors(N_KV_HEADS)

    def fn(qr, kvr, ct, st, qwv, kwv, s16, s16t, s4, s4t):
        outs = []
        for x, wv, sel, selt, scale in ((qr, qwv, s16, s16t, Q_SCALE), (kvr[:, :KV_W], kwv, s4, s4t, 1.0)):
            reps = x.shape[1] // TABLE_W
            cw, sw = jnp.tile(ct, (1, reps)), jnp.tile(st, (1, reps))
            xh, _ = _head_norm(x, sel, selt)
            nrm = xh * wv
            outs.append((nrm * cw + _rot_half(nrm) * sw) * scale)
        return outs[0], outs[1], kvr[:, KV_W:]

    return _rowwise("l1_qk_prep", fn, n, tr,
                    [(proj, 0, ATTN_W), (proj, 2 * ATTN_W // (2 * KV_W), 2 * KV_W), (cos, 0, TABLE_W), (sin, 0, TABLE_W)],
                    [qw, kw, sq, sqt, sk, skt], [(ATTN_W, BF16), (KV_W, BF16), (KV_W, BF16)], [])


def _qk_prep_bwd(proj, qn, kn, cos, sin, dq, dz, dk, dv, n, n_lat, tr):
    qw, kw = _vec(jnp.tile(qn, N_Q_HEADS)), _vec(jnp.tile(kn, N_KV_HEADS))
    sq, sqt = _head_selectors(N_Q_HEADS)
    sk, skt = _head_selectors(N_KV_HEADS)
    nlt = n_lat // tr

    def fn(flag, qr, kvr, ct, st, dqt, dzt, dkt, dvt, qwv, kwv, s16, s16t, s4, s4t):
        dxs, dws = [], []
        for x, dy, wv, sel, selt in ((qr, dqt * (flag * Q_SCALE), qwv, s16, s16t), (kvr[:, :KV_W], dkt, kwv, s4, s4t)):
            reps = x.shape[1] // TABLE_W
            cw, sw = jnp.tile(ct, (1, reps)), jnp.tile(st, (1, reps))
            xh, r = _head_norm(x, sel, selt)
            dn = dy * cw - _rot_half(dy * sw)
            dxh = dn * wv
            dxs.append(r * (dxh - xh * _head_mean(dxh * xh, sel, selt)))
            dws.append(dn * xh)
        return jnp.concatenate([dxs[0], dzt * flag, dxs[1], dvt], axis=1), dws[0], dws[1]

    dproj, dqw, dkw = _rowwise(
        "l1_qk_prep_bwd", fn, n, tr,
        [(proj, 0, ATTN_W), (proj, 2 * ATTN_W // (2 * KV_W), 2 * KV_W), (cos, 0, TABLE_W), (sin, 0, TABLE_W),
         (dq, 0, ATTN_W, "clamp", nlt), (dz, 0, ATTN_W, "clamp", nlt), (dk, 0, KV_W), (dv, 0, KV_W)],
        [qw, kw, sq, sqt, sk, skt], [(2 * ATTN_W + 2 * KV_W, BF16)], [(1, ATTN_W), (1, KV_W)], n_lat=n_lat, want_flag=True)
    return dproj, dqw.reshape(N_Q_HEADS, HEAD_DIM).sum(0)[None], dkw.reshape(N_KV_HEADS, HEAD_DIM).sum(0)[None]


NT = (((1,), (1,)), ((), ()))


def _attn_fwd(q, k, v, proj, t, tq, tk):
    n = k.shape[0]
    nkc = n // tk

    ts = _largest_tile(tq, 256, LANES)
    items = [(sub, j) for sub in range(tq // ts) for j in range(HEADS_PER_BLOCK)]

    def body(q_ref, k_ref, v_ref, z_ref, o_ref, lse_ref, gz_ref, s_ref, m_ref, l_ref, acc_ref):
        def lanes(j):
            g = j // KV_REP
            return slice(j * HEAD_DIM, (j + 1) * HEAD_DIM), slice(g * HEAD_DIM, (g + 1) * HEAD_DIM)

        for idx in range(len(items) + 1):
            nxt = items[idx] if idx < len(items) else None
            cur = items[idx - 1] if idx > 0 else None
            sn, sc = idx % 2, (idx - 1) % 2
            if nxt is not None:
                rows_n = slice(nxt[0] * ts, (nxt[0] + 1) * ts)
                ql_n, kl_n = lanes(nxt[1])
                qv = q_ref[rows_n, ql_n]
                m_ref[sn] = jnp.full((ts, LANES), -jnp.inf, F32)
            if cur is not None:
                rows_c = slice(cur[0] * ts, (cur[0] + 1) * ts)
                ql_c, kl_c = lanes(cur[1])
                m_row = jnp.max(m_ref[sc], axis=-1, keepdims=True)
                l_ref[...] = jnp.zeros(l_ref.shape, F32)
                acc_ref[...] = jnp.zeros(acc_ref.shape, F32)

            def sweep(kc, c):
                off = pl.multiple_of(kc * tk, tk)
                if nxt is not None:
                    s = lax.dot_general(qv, k_ref[pl.ds(off, tk), kl_n], NT, preferred_element_type=F32)
                    s_ref[sn, :, pl.ds(off, tk)] = s
                    m = m_ref[sn]
                    for cb in range(tk // LANES):
                        m = jnp.maximum(m, s[:, cb * LANES:(cb + 1) * LANES])
                    m_ref[sn] = m
                if cur is not None:
                    p = jnp.exp(s_ref[sc, :, pl.ds(off, tk)] - m_row)
                    lsum = l_ref[...]
                    for cb in range(tk // LANES):
                        lsum = lsum + p[:, cb * LANES:(cb + 1) * LANES]
                    l_ref[...] = lsum
                    acc_ref[...] += jnp.dot(p.astype(BF16), v_ref[pl.ds(off, tk), kl_c], preferred_element_type=F32)
                return c

            lax.fori_loop(0, nkc, sweep, 0, unroll=True)
            if cur is not None:
                l_row = jnp.sum(l_ref[...], axis=-1, keepdims=True)
                o_head = acc_ref[...] / l_row
                o_ref[rows_c, ql_c] = o_head
                gz_ref[rows_c, ql_c] = (o_head * _silu(z_ref[rows_c, ql_c])).astype(BF16)
                lse_ref[0, rows_c, cur[1]:cur[1] + 1] = m_row + jnp.log(l_row)

    nb = ATTN_W // Q_BLOCK_W
    kspec = pl.BlockSpec((n, LANES), lambda b, i: (0, b))
    return pl.pallas_call(
        body, grid=(nb, t // tq),
        in_specs=[pl.BlockSpec((tq, Q_BLOCK_W), lambda b, i: (i, b)), kspec, kspec,
                  pl.BlockSpec((tq, Q_BLOCK_W), lambda b, i: (i, nb + b))],
        out_specs=[pl.BlockSpec((tq, Q_BLOCK_W), lambda b, i: (i, b)),
                   pl.BlockSpec((1, tq, HEADS_PER_BLOCK), lambda b, i: (b, i, 0)),
                   pl.BlockSpec((tq, Q_BLOCK_W), lambda b, i: (i, b))],
        out_shape=[jax.ShapeDtypeStruct((t, ATTN_W), F32), jax.ShapeDtypeStruct((nb, t, HEADS_PER_BLOCK), F32),
                   jax.ShapeDtypeStruct((t, ATTN_W), BF16)],
        scratch_shapes=[pltpu.VMEM((2, ts, n), F32), pltpu.VMEM((2, ts, LANES), F32), pltpu.VMEM((ts, LANES), F32),
                        pltpu.VMEM((ts, HEAD_DIM), F32)],
        compiler_params=_params("parallel", "parallel"), name="attn_fwd")(q, k, v, proj)


def _attn_bwd(q, k, v, dgz, proj, o, lse, t, tq, tk):
    n = k.shape[0]
    nkc = n // tk
    tn = (((0,), (0,)), ((), ()))

    def body(q_ref, k_ref, v_ref, dgz_ref, z_ref, o_ref, lse_ref, dq_ref, dk_ref, dv_ref, dz_ref, acc_ref):
        @pl.when(pl.program_id(1) == 0)
        def _():
            dk_ref[...] = jnp.zeros(dk_ref.shape, F32)
            dv_ref[...] = jnp.zeros(dv_ref.shape, F32)

        for j0 in range(0, HEADS_PER_BLOCK, 2):
            kl = slice((j0 // KV_REP) * HEAD_DIM, (j0 // KV_REP + 1) * HEAD_DIM)
            heads = []
            for a in range(2):
                j = j0 + a
                ql = slice(j * HEAD_DIM, (j + 1) * HEAD_DIM)
                qv, ov, zv, dgv = q_ref[:, ql], o_ref[:, ql], z_ref[:, ql], dgz_ref[:, ql]
                dov = (dgv * _silu(zv)).astype(BF16)
                dz_ref[:, ql] = dgv * ov * _silu_grad(zv)
                dl_v = jnp.sum(dov.astype(F32) * ov, axis=-1, keepdims=True)
                heads.append((ql, qv, dov, dl_v, lse_ref[0, :, j:j + 1]))
                acc_ref[a] = jnp.zeros((tq, HEAD_DIM), F32)

            def step(kc, c):
                off = pl.multiple_of(kc * tk, tk)
                kt = k_ref[pl.ds(off, tk), kl]
                vt = v_ref[pl.ds(off, tk), kl]
                dv_part, dk_part = None, None
                for a, (_, qv, dov, dl_v, lse_v) in enumerate(heads):
                    s = lax.dot_general(qv, kt, NT, preferred_element_type=F32)
                    p = jnp.exp(s - lse_v)
                    dp = lax.dot_general(dov, vt, NT, preferred_element_type=F32)
                    ds = (p * (dp - dl_v)).astype(BF16)
                    acc_ref[a] += jnp.dot(ds, kt, preferred_element_type=F32)
                    dvp = lax.dot_general(p.astype(BF16), dov, tn, preferred_element_type=F32)
                    dkp = lax.dot_general(ds, qv, tn, preferred_element_type=F32)
                    dv_part = dvp if dv_part is None else dv_part + dvp
                    dk_part = dkp if dk_part is None else dk_part + dkp
                dv_ref[pl.ds(off, tk), kl] += dv_part
                dk_ref[pl.ds(off, tk), kl] += dk_part
                return c

            lax.fori_loop(0, nkc, step, 0, unroll=2)
            for a, h in enumerate(heads):
                dq_ref[:, h[0]] = acc_ref[a]

    nb = ATTN_W // Q_BLOCK_W
    qspec = pl.BlockSpec((tq, Q_BLOCK_W), lambda b, i: (i, b))
    kspec = pl.BlockSpec((n, LANES), lambda b, i: (0, b))
    cspec = pl.BlockSpec((1, tq, HEADS_PER_BLOCK), lambda b, i: (b, i, 0))
    return pl.pallas_call(
        body, grid=(nb, t // tq),
        in_specs=[qspec, kspec, kspec, qspec, pl.BlockSpec((tq, Q_BLOCK_W), lambda b, i: (i, nb + b)), qspec, cspec],
        out_specs=[qspec, kspec, kspec, qspec],
        out_shape=[jax.ShapeDtypeStruct((t, ATTN_W), F32), jax.ShapeDtypeStruct((n, KV_W), F32),
                   jax.ShapeDtypeStruct((n, KV_W), F32), jax.ShapeDtypeStruct((t, ATTN_W), F32)],
        scratch_shapes=[pltpu.VMEM((2, tq, HEAD_DIM), F32)],
        compiler_params=_params("parallel", "arbitrary"), name="attn_bwd")(q, k, v, dgz, proj, o, lse)


def _s5_system(p, n_lat, n_ctx):
    two_g = 2 * SSM_GROUPS
    a_re = p["ssm_a_re"].reshape(two_g, SSM_STATE)
    a_im = p["ssm_a_im"].reshape(two_g, SSM_STATE)
    log_dt = p["ssm_log_dt"].reshape(two_g, 1)
    b_re = p["ssm_b_re"].reshape(two_g, SSM_STATE, SSM_GROUP).transpose(2, 0, 1)
    b_im = p["ssm_b_im"].reshape(two_g, SSM_STATE, SSM_GROUP).transpose(2, 0, 1)
    raw = (a_re, a_im, log_dt, b_re, b_im)
    abr, abi, bbr, bbi, alr, ali, acr, aci = _s5_prep(*raw, n_lat // N_SEG, n_ctx // N_SEG)
    dirs = []
    for d in range(2):
        g = slice(d * SSM_GROUPS, (d + 1) * SSM_GROUPS)
        b_mat = _slab_in_matrix(bbr[:, g].transpose(1, 2, 0), bbi[:, g].transpose(1, 2, 0))
        c_mat = _slab_out_matrix(p["ssm_c_re"][0, d], p["ssm_c_im"][0, d])
        abar = _slab_pair(abr[g], abi[g])
        tables = {}
        for part, seg in (("lat", n_lat // N_SEG), ("ctx", n_ctx // N_SEG)):
            tables["h_" + part] = _s5_pow_table(f"s5_pow_h{d}_{part}", abar, seg, d == 0, False)
            tables["g_" + part] = _s5_pow_table(f"s5_pow_g{d}_{part}", abar, seg, d == 1, True)
        dirs.append(dict(
            b=b_mat.astype(BF16), bt=b_mat.transpose(0, 2, 1).astype(BF16), b32=b_mat,
            c=c_mat.astype(BF16), ct=c_mat.transpose(0, 2, 1).astype(BF16), ct32=c_mat.transpose(0, 2, 1),
            abar=abar, a_lat=_slab_pair(alr[g], ali[g])[:, 0], a_ctx=_slab_pair(acr[g], aci[g])[:, 0], **tables))
    return raw, dirs


def _s5_forward(proj, dirs, n_lat, n_ctx):
    n = n_lat + n_ctx
    zero_c = jnp.zeros((N_SLAB, STATE_W), F32)
    ys, saved = [], []
    for d, s in enumerate(dirs):
        desc = d == 1
        tag = f"s5f{d}"
        zc = _s5_ends(tag + "_ctx_ends", proj, n_ctx, n_lat, s["h_ctx"], s["b32"])
        ent_c, h0 = _s5_carry(tag + "_ctx_carry", zc, s["a_ctx"], zero_c, desc, False)
        y, hch_c = _s5_scan(tag + "_ctx", proj, n_ctx, n_lat, s["b"], s["c"], s["abar"], ent_c, desc,
                            y_rows=n)
        zl = _s5_ends(tag + "_lat_ends", proj, n_lat, 0, s["h_lat"], s["b32"])
        ent_l, _ = _s5_carry(tag + "_lat_carry", zl, s["a_lat"], h0, desc, False)
        y, hch_l = _s5_scan(tag + "_lat", proj, n_lat, 0, s["b"], s["c"], s["abar"], ent_l, desc,
                            y_alias=y, y_rows=n)
        ys.append(y)
        saved.append((hch_l, hch_c))
    return ys, saved


def _s5_backward(proj, dy, dirs, saved, n_lat, n_ctx):
    n = n_lat + n_ctx
    zero_c = jnp.zeros((N_SLAB, STATE_W), F32)
    out = []
    for d, s in enumerate(dirs):
        desc = d == 1
        tag = f"s5b{d}"
        hch_l, hch_c = saved[d]
        gl = _s5_ends(tag + "_lat_ends", dy, n_lat, 0, s["g_lat"], s["ct32"])
        ent_l, g0 = _s5_carry(tag + "_lat_carry", gl, s["a_lat"], zero_c, not desc, True)
        du, db_l, dc_l, da_l = _s5_scan_bwd(tag + "_lat", proj, dy, n_lat, 0, s["b"], s["bt"], s["ct"], s["abar"],
                                            hch_l, ent_l, desc, du_rows=n)
        gc = _s5_ends(tag + "_ctx_ends", dy, n_ctx, n_lat, s["g_ctx"], s["ct32"])
        ent_c, _ = _s5_carry(tag + "_ctx_carry", gc, s["a_ctx"], g0, not desc, True)
        du, db_c, dc_c, da_c = _s5_scan_bwd(tag + "_ctx", proj, dy, n_ctx, n_lat, s["b"], s["bt"], s["ct"], s["abar"],
                                            hch_c, ent_c, desc, du_alias=du, du_rows=n)
        out.append((du, db_l + db_c, dc_l + dc_c, da_l + da_c))
    return out


def _s5_param_grads(raw, bwd):
    dabr, dabi, dbbr, dbbi, dcr, dci = [], [], [], [], [], []
    for _, db, dc, da in bwd:
        da = jnp.sum(da, axis=1)
        dabr.append(da[:, :HALF_W].reshape(SSM_GROUPS, SSM_STATE))
        dabi.append(da[:, HALF_W:].reshape(SSM_GROUPS, SSM_STATE))
        dbd = _slab_diag(db)
        dbbr.append(dbd[0].transpose(1, 0, 2))
        dbbi.append(dbd[1].transpose(1, 0, 2))
        dcd = _slab_diag(dc)
        dcr.append(dcd[0])
        dci.append(-dcd[1])
    cat = lambda xs, ax: jnp.concatenate(xs, axis=ax)
    dar, dai, dld, dbr, dbi = _s5_prep_bwd(*raw, cat(dabr, 0), cat(dabi, 0), cat(dbbr, 1), cat(dbbi, 1))
    shp = (1, 2, SSM_GROUPS, SSM_STATE)
    b_shape = (1, 2, SSM_GROUPS, SSM_STATE, SSM_GROUP)
    return dict(
        ssm_a_re=dar.reshape(shp), ssm_a_im=dai.reshape(shp), ssm_log_dt=dld.reshape(1, 2, SSM_GROUPS),
        ssm_b_re=dbr.transpose(1, 2, 0).reshape(b_shape), ssm_b_im=dbi.transpose(1, 2, 0).reshape(b_shape),
        ssm_c_re=jnp.stack(dcr)[None], ssm_c_im=jnp.stack(dci)[None])


def _example_step(x, ctx, target, mods, w, p):
    t, c = x.shape[0], ctx.shape[0]
    n = t + c
    assert t % c == 0 and c % LANES == 0 and c % (8 * N_SEG) == 0 and t % GRID_W == 0
    tr = _largest_tile(c, 256, 8)
    xall = (x, ctx)
    g0, g1 = _vec(p["norm_g"][0]), _vec(p["norm_g"][1])
    (shift0, scale0, gate0), (shift1, scale1, gate1) = [tuple(_vec(v) for v in m) for m in mods]

    h0 = _norm_mod_fwd("l0_norm", xall, g0, scale0, shift0, n, tr, t)
    proj0 = _mm("l0_in", h0, w["ssm_w_in"], "nn")
    raw, dirs = _s5_system(p, t, c)
    (y_f, y_r), saved = _s5_forward(proj0, dirs, t, c)
    d_skip = _vec(p["ssm_d"][0])

    def post_a(u, yf, yr, dv):
        y = u * dv + yf + yr
        return y, _gelu(y)

    y0, yg = _rowwise("l0_gelu", post_a, n, tr, [(proj0, 0, D_MODEL), (y_f, 0, D_MODEL), (y_r, 0, D_MODEL)], [d_skip],
                      [(D_MODEL, F32), (D_MODEL, F32)], [])
    tg = _mm("l0_glu", yg, w["ssm_w_glu"], "nn")
    b_glu = _vec(p["ssm_b_glu"][0])

    def post_b(ygt, tt, zt, bv):
        return ygt * _sigmoid(tt + bv) * _silu(zt)

    gz0 = _rowwise("l0_gate", post_b, n, tr, [(yg, 0, D_MODEL), (tg, 0, D_MODEL), (proj0, 1, D_MODEL)], [b_glu],
                   [(D_MODEL, BF16)], [])[0]
    out0 = _mm("l0_out", gz0, w["ssm_w_out"], "nn")

    def res_norm(xt, ot, gv, g1v, sc, sh):
        x1t = xt + gv * ot
        xh, _ = _rms_hat(x1t)
        return x1t, (xh * g1v) * (1.0 + sc) + sh

    x1, h1 = _rowwise("l0_res_l1_norm", res_norm, n, tr, [_stream(xall), (out0, 0, D_MODEL)],
                      [gate0, g1, scale1, shift1], [(D_MODEL, F32), (D_MODEL, BF16)], [], n_lat=t)
    proj1 = _mm("l1_in", h1, w["attn_w_in"], "nn")
    cos, sin = _rope_tables(t, c)
    qn, kn = p["attn_q_norm"][0], p["attn_k_norm"][0]
    q_h, k_h, v_h = _qk_prep(proj1, qn, kn, cos, sin, n, tr)
    tq = _largest_tile(t, 512, LANES)
    o, lse, gz1 = _attn_fwd(q_h, k_h, v_h, proj1, t, tq, _largest_tile(n, 2816, LANES))
    out1 = _mm("l1_out", gz1, w["attn_w_out"], "nn")

    gf = _vec(p["final_norm_g"])

    def head(x1t, o1t, tgt, g1v, gfv):
        x2 = x1t + g1v * o1t
        xh, r = _rms_hat(x2)
        e = xh * gfv - tgt
        dyf = e * (1.0 / D_MODEL)
        dx2 = _rms_bwd(xh, r, dyf * gfv)
        return dx2, g1v * dx2, dyf * xh, dx2 * o1t, jnp.sum(e * e, axis=1, keepdims=True)

    gate1_lat = gate1[0:1]
    dx2, dout1, d_gf, d_gate1, sq = _rowwise(
        "head", head, t, tr, [(x1, 0, D_MODEL), (out1, 0, D_MODEL), (target, 0, D_MODEL)], [gate1_lat, gf],
        [(D_MODEL, F32), (D_MODEL, BF16)], [(1, D_MODEL), (1, D_MODEL), (1, 1)])

    d_w_attn_out = _mm("l1_out_dw", gz1, dout1, "tn", out_dtype=BF16)
    dgz1 = _mm("l1_out_dx", dout1, w["attn_w_out"], "nt")
    dq_s, dk, dv, dz1 = _attn_bwd(q_h, k_h, v_h, dgz1, proj1, o, lse, t, tq, _largest_tile(n, 1024, LANES))
    dproj1, d_qn, d_kn = _qk_prep_bwd(proj1, qn, kn, cos, sin, dq_s, dz1, dk, dv, n, t, tr)
    d_w_attn_in = _mm("l1_in_dw", h1, dproj1, "tn", out_dtype=BF16)
    dh1 = _mm("l1_in_dx", dproj1, w["attn_w_in"], "nt")
    dx1, dout0, d_g1, d_scale1, d_shift1, d_gate0 = _norm_mod_bwd("l1_norm_bwd", x1, g1, scale1, dh1, dx2, n, tr, t,
                                                                  prev=(out0, gate0))

    d_w_out = _mm("l0_out_dw", gz0, dout0, "tn", out_dtype=BF16)
    dgz0 = _mm("l0_out_dx", dout0, w["ssm_w_out"], "nt")

    def post_b_bwd(dgt, ygt, tt, zt, bv):
        s = _sigmoid(tt + bv)
        dy2 = dgt * _silu(zt)
        dt = dy2 * ygt * s * (1.0 - s)
        return dgt * (ygt * s) * _silu_grad(zt), dt, dy2 * s, dt

    dz0, dtg, dyg_a, d_b_glu = _rowwise(
        "l0_gate_bwd", post_b_bwd, n, tr, [(dgz0, 0, D_MODEL), (yg, 0, D_MODEL), (tg, 0, D_MODEL), (proj0, 1, D_MODEL)],
        [b_glu], [(D_MODEL, BF16), (D_MODEL, BF16), (D_MODEL, F32)], [(1, D_MODEL)])
    d_w_glu = _mm("l0_glu_dw", yg, dtg, "tn", out_dtype=BF16)
    dyg_b = _mm("l0_glu_dx", dtg, w["ssm_w_glu"], "nt")

    def post_a_bwd(da, db, yt, ut, dv):
        dy = (da + db) * _gelu_grad(yt)
        return dy, dy * dv, dy * ut

    dy0, du_skip, d_d = _rowwise("l0_gelu_bwd", post_a_bwd, n, tr,
                                 [(dyg_a, 0, D_MODEL), (dyg_b, 0, D_MODEL), (y0, 0, D_MODEL), (proj0, 0, D_MODEL)], [d_skip],
                                 [(D_MODEL, F32), (D_MODEL, F32)], [(1, D_MODEL)])
    s5_bwd = _s5_backward(proj0, dy0, dirs, saved, t, c)
    dproj0 = _rowwise("l0_in_grad", lambda a, b, cc, dz: jnp.concatenate([a + b + cc, dz], axis=1), n, tr,
                      [(du_skip, 0, D_MODEL), (s5_bwd[0][0], 0, D_MODEL), (s5_bwd[1][0], 0, D_MODEL), (dz0, 0, D_MODEL)], [],
                      [(2 * D_MODEL, BF16)], [])[0]
    d_w_in = _mm("l0_in_dw", h0, dproj0, "tn", out_dtype=BF16)
    dh0 = _mm("l0_in_dx", dproj0, w["ssm_w_in"], "nt")
    dx0, d_g0, d_scale0, d_shift0 = _norm_mod_bwd("l0_norm_bwd", xall, g0, scale0, dh0, dx1, n, tr, t, dx_lat_only=True)

    big = dict(ssm_w_in=d_w_in, ssm_w_glu=d_w_glu, ssm_w_out=d_w_out, attn_w_in=d_w_attn_in, attn_w_out=d_w_attn_out)
    small = dict(
        norm_g=jnp.concatenate([d_g0[0], d_g1[0]], axis=0), ssm_d=d_d[0], ssm_b_glu=d_b_glu[0],
        attn_q_norm=d_qn, attn_k_norm=d_kn, final_norm_g=d_gf[0, 0], **_s5_param_grads(raw, s5_bwd))
    zero_v = jnp.zeros((D_MODEL,), F32)
    d_mod_lat = jnp.stack([jnp.concatenate([d_shift0[0, 0], d_scale0[0, 0], d_gate0[0, 0]]),
                           jnp.concatenate([d_shift1[0, 0], d_scale1[0, 0], d_gate1[0, 0]])])
    d_mod_ctx = jnp.stack([jnp.concatenate([d_shift0[1, 0], d_scale0[1, 0], d_gate0[1, 0]]),
                           jnp.concatenate([d_shift1[1, 0], d_scale1[1, 0], zero_v])])
    return sq[0, 0, 0], dx0, big, small, d_mod_lat, d_mod_ctx


def _adamw(name, w, g, m, v):
    rows, cols = w.shape
    tr = _largest_tile(rows, 256, 8)
    c1 = 1.0 / (1.0 - ADAM_B1 ** ADAM_STEP)
    c2 = 1.0 / (1.0 - ADAM_B2 ** ADAM_STEP)

    def fn(wt, gt, mt, vt):
        mn = ADAM_B1 * mt + (1.0 - ADAM_B1) * gt
        vn = ADAM_B2 * vt + (1.0 - ADAM_B2) * (gt * gt)
        delta = -ADAM_LR * ((mn * c1) / (jnp.sqrt(vn * c2) + ADAM_EPS) + ADAM_WD * wt)
        return delta, mn, vn

    return _rowwise(name, fn, rows, tr, [(a, 0, cols) for a in (w, g, m, v)], [], [(cols, F32)] * 3, [])


BIG = ("ssm_w_in", "ssm_w_glu", "ssm_w_out", "attn_w_in", "attn_w_out")
COL_SHARDED = ("ssm_w_in", "attn_w_in")
WEIGHTS = ("c_ctx", "w_mod", "b_mod", "norm_g", "ssm_w_in", "ssm_a_re", "ssm_a_im", "ssm_log_dt", "ssm_b_re", "ssm_b_im",
           "ssm_c_re", "ssm_c_im", "ssm_d", "ssm_w_glu", "ssm_b_glu", "ssm_w_out", "attn_w_in", "attn_q_norm",
           "attn_k_norm", "attn_w_out", "final_norm_g")
SMALL = tuple(k for k in WEIGHTS if k not in BIG and k != "w_mod")
PACK_W = 1024
COND_ROWS = 2 * N_DEV


def _attn_in_perm(x, inverse):
    a, kv = ATTN_W, 2 * KV_W
    if inverse:
        return jnp.concatenate([x[..., :a], x[..., 2 * a:], x[..., a:2 * a]], axis=-1)
    return jnp.concatenate([x[..., :a], x[..., a + kv:], x[..., a:a + kv]], axis=-1)


def _pack(arrays, dtype, row_unit):
    flat = jnp.concatenate([a.reshape(-1).astype(dtype) for a in arrays])
    rows = -(-flat.shape[0] // PACK_W)
    rows = -(-rows // row_unit) * row_unit
    flat = jnp.concatenate([flat, jnp.zeros((rows * PACK_W - flat.shape[0],), dtype)])
    return flat.reshape(rows, PACK_W)


def _unpack(buf, shapes):
    lead = buf.shape[:-2]
    flat = buf.reshape(lead + (-1,))
    out, off = [], 0
    for shp in shapes:
        size = math.prod(shp)
        out.append(flat[..., off:off + size].reshape(lead + tuple(shp)))
        off += size
    return out


def kernel(x, c, ctx, c_ctx, w_mod, b_mod, norm_g, ssm_w_in, ssm_a_re, ssm_a_im, ssm_log_dt, ssm_b_re, ssm_b_im, ssm_c_re, ssm_c_im, ssm_d, ssm_w_glu, ssm_b_glu, ssm_w_out, attn_w_in, attn_q_norm, attn_k_norm, attn_w_out, final_norm_g, loss_target, m_c_ctx, m_w_mod, m_b_mod, m_norm_g, m_ssm_w_in, m_ssm_a_re, m_ssm_a_im, m_ssm_log_dt, m_ssm_b_re, m_ssm_b_im, m_ssm_c_re, m_ssm_c_im, m_ssm_d, m_ssm_w_glu, m_ssm_b_glu, m_ssm_w_out, m_attn_w_in, m_attn_q_norm, m_attn_k_norm, m_attn_w_out, m_final_norm_g, v_c_ctx, v_w_mod, v_b_mod, v_norm_g, v_ssm_w_in, v_ssm_a_re, v_ssm_a_im, v_ssm_log_dt, v_ssm_b_re, v_ssm_b_im, v_ssm_c_re, v_ssm_c_im, v_ssm_d, v_ssm_w_glu, v_ssm_b_glu, v_ssm_w_out, v_attn_w_in, v_attn_q_norm, v_attn_k_norm, v_attn_w_out, v_final_norm_g):
    args = dict(locals())
    wts = {k: args[k] for k in WEIGHTS}
    mom_m = {k: args["m_" + k] for k in WEIGHTS}
    mom_v = {k: args["v_" + k] for k in WEIGHTS}
    mx, my, mc = lax.axis_index("x"), lax.axis_index("y"), lax.axis_index("c")
    chip = 2 * mx + my
    me = 2 * chip + mc

    halves = []
    for k in BIG:
        sh = wts[k][0]
        hr = sh.shape[0] // 2
        halves.append(lax.dynamic_slice_in_dim(sh, mc * hr, hr, axis=0))
    gathered = _gather_two_level("gather_weights", _pack(halves, BF16, 16))
    parts = _unpack(gathered, [h.shape for h in halves])
    w_full = {}
    for k, pc in zip(BIG, parts):
        hr, cols = pc.shape[1:]
        pc = pc.reshape(N_CHIP, 2, hr, cols)
        if k in COL_SHARDED:
            w_full[k] = pc.transpose(1, 2, 0, 3).reshape(2 * hr, N_CHIP * cols)
        else:
            w_full[k] = pc.reshape(N_CHIP * 2 * hr, cols)
    w_full["attn_w_in"] = _attn_in_perm(w_full["attn_w_in"], False)

    c_blk = jnp.concatenate([c, jnp.zeros((N_DEV - 1, D_MODEL), F32)], axis=0)
    c_all = _exchange("gather_c", c_blk, True)[:, 0]
    cond = jnp.concatenate([c_all, c_ctx[None], jnp.zeros((COND_ROWS - N_DEV - 1, D_MODEL), F32)], axis=0)
    s_cond, ds_cond = _rowwise("cond_silu", lambda t: (_silu(t), _silu_grad(t)), COND_ROWS, COND_ROWS, [(cond, 0, D_MODEL)], [],
                               [(D_MODEL, F32), (D_MODEL, F32)], [])
    w_mod_b = w_mod.astype(BF16)
    mcols = w_mod.shape[2]
    mod_part = jnp.stack([_mm(f"mod{i}", s_cond, w_mod_b[i], "nn") for i in range(2)])
    mod_g = _exchange("gather_mod", mod_part.reshape(2 * COND_ROWS, mcols), True)
    mod_all = mod_g.reshape(N_CHIP, 2, 2, COND_ROWS, mcols)[:, 0]
    mod_all = mod_all.transpose(1, 2, 0, 3).reshape(2, COND_ROWS, N_CHIP * mcols) + b_mod[:, None, :]
    mods = []
    for i in range(2):
        lat = lax.dynamic_slice_in_dim(mod_all[i], me, 1, axis=0)[0]
        both = jnp.stack([lat, mod_all[i, N_DEV]])
        mods.append((both[:, :D_MODEL], both[:, D_MODEL:2 * D_MODEL], both[:, 2 * D_MODEL:]))

    small_p = {k: wts[k] for k in SMALL if k != "c_ctx" and k != "b_mod"}
    sq, grad_x, big_g, small_g, d_mod_lat, d_mod_ctx = _example_step(x[0], ctx[0], loss_target[0], mods, w_full, small_p)
    loss = lax.psum(0.5 / D_MODEL * sq, ("x", "y", "c"))
    big_g["attn_w_in"] = _attn_in_perm(big_g["attn_w_in"], True)

    small_names = [k for k in SMALL if k not in ("c_ctx", "b_mod")]
    small_list = [small_g[k] for k in small_names] + [d_mod_lat, d_mod_ctx]
    small_shapes = [wts[k].shape for k in small_names] + [d_mod_lat.shape, d_mod_ctx.shape]
    packed = _pack(small_list, F32, 8 * N_DEV)
    slice_rows = packed.shape[0] // N_DEV
    slices = _exchange("scatter_small", packed.reshape(N_DEV, slice_rows, PACK_W), False)
    my_sum = _sum_slots("sum_small", slices)
    payload = jnp.concatenate([my_sum, _pack([d_mod_lat], F32, 8)], axis=0)
    sg = _exchange("gather_small", payload, True)
    summed = _unpack(sg[:, :slice_rows].reshape(packed.shape), small_shapes)
    grads = dict(zip(small_names, summed[:-2]))
    d_mod_lat_sum, d_mod_ctx_sum = summed[-2], summed[-1]
    grads["b_mod"] = d_mod_lat_sum + d_mod_ctx_sum
    d_mod_lat_all = _unpack(sg[:, slice_rows:], [d_mod_lat.shape])[0]

    g_w_mod, ds_cc = [], []
    for i in range(2):
        rows9 = jnp.concatenate([d_mod_lat_all[:, i], d_mod_ctx_sum[i][None],
                                 jnp.zeros((COND_ROWS - N_DEV - 1, 3 * D_MODEL), F32)], axis=0)
        mine = lax.dynamic_slice_in_dim(rows9, chip * mcols, mcols, axis=1)
        g_w_mod.append(_mm(f"mod{i}_dw", s_cond, mine, "tn"))
        ds_cc.append(_mm(f"mod{i}_dx", mine, w_mod_b[i], "nt")[N_DEV])
    grads["w_mod"] = jnp.stack(g_w_mod)
    part = (ds_cc[0] + ds_cc[1]) * jnp.where(mc == 0, 1.0, 0.0)
    part_blk = jnp.concatenate([part[None], jnp.zeros((N_DEV - 1, D_MODEL), F32)], axis=0)
    ds_all = _sum_slots("sum_c_ctx", _exchange("gather_c_ctx", part_blk, True))
    grads["c_ctx"] = ds_all[0] * ds_cond[N_DEV]

    blocks = []
    for k in BIG:
        g = big_g[k]
        rows, cols = g.shape
        if k in COL_SHARDED:
            blocks.append(g.reshape(2, rows // 2, N_CHIP, cols // N_CHIP).transpose(2, 0, 1, 3).reshape(N_DEV, -1))
        else:
            blocks.append(g.reshape(N_DEV, -1))
    sendbuf = jnp.concatenate(blocks, axis=1).astype(BF16)
    sendbuf = sendbuf.reshape(N_DEV, -1, PACK_W)
    recv = _exchange("scatter_big", sendbuf, False)
    mine = _sum_slots("sum_big", recv)
    both = _exchange("swap_halves", mine, True, sibling_only=True)
    half_shapes = [(wts[k].shape[1] // 2, wts[k].shape[2]) for k in BIG]
    for k, pc in zip(BIG, _unpack(both, half_shapes)):
        grads[k] = pc.reshape(wts[k].shape)

    delta, new_m, new_v = {}, {}, {}
    for k in BIG + ("w_mod",):
        shp = wts[k].shape
        two_d = (-1, shp[-1])
        res = _adamw("adamw_" + k, *[a.reshape(two_d) for a in (wts[k], grads[k], mom_m[k], mom_v[k])])
        delta[k], new_m[k], new_v[k] = [r.reshape(shp) for r in res]
    shapes = [wts[k].shape for k in SMALL]
    packed = [_pack([d[k] for k in SMALL], F32, 8) for d in (wts, grads, mom_m, mom_v)]
    res = _adamw("adamw_small", *packed)
    for dst, buf in zip((delta, new_m, new_v), res):
        for k, a in zip(SMALL, _unpack(buf, shapes)):
            dst[k] = a
    grads = {k: grads[k].reshape(wts[k].shape) for k in WEIGHTS}
    return (loss, grad_x[None], *[grads[k] for k in WEIGHTS], *[delta[k] for k in WEIGHTS],
            *[new_m[k] for k in WEIGHTS], *[new_v[k] for k in WEIGHTS])
```

```python
import functools
import math

import jax
import jax.numpy as jnp
from jax import lax
from jax.experimental import pallas as pl
from jax.experimental.pallas import tpu as pltpu

F32 = jnp.float32
BF16 = jnp.bfloat16

D_MODEL = 1024
NORM_EPS = 1e-6
SSM_GROUPS = 64
SSM_GROUP = 16
SSM_STATE = 64
LANES = 128
SLAB_W = LANES
N_SLAB = D_MODEL // SLAB_W
SLAB_GROUPS = SLAB_W // SSM_GROUP
HALF_W = SLAB_GROUPS * SSM_STATE
STATE_W = 2 * HALF_W
N_SEG = 8
HEAD_DIM = 64
N_Q_HEADS = 16
N_KV_HEADS = 4
KV_REP = N_Q_HEADS // N_KV_HEADS
ATTN_W = N_Q_HEADS * HEAD_DIM
KV_W = N_KV_HEADS * HEAD_DIM
GRID_W = 64
ROPE_THETA = 10000.0
N_DEV = 8
N_CHIP = 4
VMEM_LIMIT_BYTES = 56 * 1024 * 1024

ADAM_LR = 0.001
ADAM_B1 = 0.9
ADAM_B2 = 0.999
ADAM_EPS = 1e-08
ADAM_WD = 0.01
ADAM_STEP = 10


def _params(*sem):
    return pltpu.CompilerParams(dimension_semantics=sem, vmem_limit_bytes=VMEM_LIMIT_BYTES)


def _largest_tile(n, cap, unit):
    if n <= cap:
        return n
    t = (cap // unit) * unit
    while t >= unit:
        if n % t == 0:
            return t
        t -= unit
    raise ValueError(f"no tile for {n} (cap {cap}, unit {unit})")


def _rowwise(name, fn, n_rows, tr, row_ins, vec_ins, row_outs, red_outs, n_lat=None, want_flag=False):
    nt = n_rows // tr
    assert nt * tr == n_rows
    nlt = nt if n_lat is None else n_lat // tr

    def sel(i):
        return jnp.where(i >= nlt, 1, 0)

    arrays, in_specs, pairs = [], [], []
    for spec in row_ins:
        arr, cb, w = spec[:3]
        kind = spec[3] if len(spec) > 3 else None
        m = spec[4] if len(spec) > 4 else None
        if kind == "pair":
            arrays += [arr, m]
            in_specs += [pl.BlockSpec((tr, w), functools.partial(lambda i, cb: (jnp.minimum(i, nlt - 1), cb), cb=cb)),
                         pl.BlockSpec((tr, w), functools.partial(lambda i, cb: (jnp.maximum(i - nlt, 0), cb), cb=cb))]
            pairs.append(len(arrays) - 2)
            continue
        if kind == "mod":
            imap = functools.partial(lambda i, cb, m: (i % m, cb), cb=cb, m=m)
        elif kind == "clamp":
            imap = functools.partial(lambda i, cb, m: (jnp.minimum(i, m - 1), cb), cb=cb, m=m)
        else:
            imap = functools.partial(lambda i, cb: (i, cb), cb=cb)
        arrays.append(arr)
        in_specs.append(pl.BlockSpec((tr, w), imap))
    for v in vec_ins:
        s, a, w = v.shape
        imap = (lambda i: (sel(i), 0, 0)) if s == 2 else (lambda i: (0, 0, 0))
        arrays.append(v)
        in_specs.append(pl.BlockSpec((1, a, w), imap))
    out_shapes, out_specs = [], []
    lat_only = [len(spec) > 2 for spec in row_outs]
    for (w, dt), lat in zip([spec[:2] for spec in row_outs], lat_only):
        out_shapes.append(jax.ShapeDtypeStruct((n_lat if lat else n_rows, w), dt))
        out_specs.append(pl.BlockSpec((tr, w), (lambda i: (jnp.minimum(i, nlt - 1), 0)) if lat else (lambda i: (i, 0))))
    for s, w in red_outs:
        out_shapes.append(jax.ShapeDtypeStruct((s, 1, w), F32))
        imap = (lambda i: (sel(i), 0, 0)) if s == 2 else (lambda i: (0, 0, 0))
        out_specs.append(pl.BlockSpec((1, 1, w), imap))
    n_ri, n_vi, n_ro, n_rd = len(row_ins) + len(pairs), len(vec_ins), len(row_outs), len(red_outs)

    def body(*refs):
        i = pl.program_id(0)
        rows, k = [], 0
        while k < n_ri:
            if k in pairs:
                rows.append(jnp.where(i < nlt, refs[k][...], refs[k + 1][...]).astype(F32))
                k += 2
            else:
                rows.append(refs[k][...].astype(F32))
                k += 1
        vecs = [r[0] for r in refs[n_ri:n_ri + n_vi]]
        outs = refs[n_ri + n_vi:]
        lead = [jnp.where(i < nlt, 1.0, 0.0).astype(F32)] if want_flag else []
        res = fn(*lead, *rows, *vecs)
        if not isinstance(res, (tuple, list)):
            res = (res,)
        assert len(res) == n_ro + n_rd
        for k in range(n_ro):
            if lat_only[k]:
                @pl.when(i < nlt)
                def _(k=k):
                    outs[k][...] = res[k].astype(outs[k].dtype)
            else:
                outs[k][...] = res[k].astype(outs[k].dtype)
        for k in range(n_rd):
            part = jnp.sum(res[n_ro + k].astype(F32), axis=0, keepdims=True)
            first = i == 0
            if red_outs[k][0] == 2:
                first = jnp.logical_or(first, i == nlt)
            o = outs[n_ro + k]

            @pl.when(first)
            def _():
                o[0] = part

            @pl.when(jnp.logical_not(first))
            def _():
                o[0] = o[0] + part

    res = pl.pallas_call(
        body, grid=(nt,), in_specs=in_specs, out_specs=out_specs, out_shape=out_shapes,
        compiler_params=_params("arbitrary"), name=name)(*arrays)
    return res


def _vec(v):
    v = v.astype(F32)
    if v.ndim == 1:
        v = v[None]
    return v[:, None, :]


def _mm(name, a, b, mode, out_dtype=F32):
    if mode in ("nn", "nt"):
        m, k = a.shape
        n = b.shape[1] if mode == "nn" else b.shape[0]
        tm = _largest_tile(m, 1024, 8)
        tn = _largest_tile(n, 1024, 128)
        contract = (((1,), (0,)), ((), ())) if mode == "nn" else (((1,), (1,)), ((), ()))

        def body(a_ref, b_ref, o_ref):
            o_ref[...] = lax.dot_general(a_ref[...].astype(BF16), b_ref[...].astype(BF16), contract,
                                         preferred_element_type=F32).astype(o_ref.dtype)

        b_spec = pl.BlockSpec((k, tn), lambda i, j: (0, j)) if mode == "nn" else pl.BlockSpec((tn, k), lambda i, j: (j, 0))
        return pl.pallas_call(
            body, grid=(m // tm, n // tn),
            in_specs=[pl.BlockSpec((tm, k), lambda i, j: (i, 0)), b_spec],
            out_specs=pl.BlockSpec((tm, tn), lambda i, j: (i, j)),
            out_shape=jax.ShapeDtypeStruct((m, n), out_dtype),
            compiler_params=_params("parallel", "arbitrary"), name=name)(a, b)
    assert mode == "tn"
    r, k1 = a.shape
    k2 = b.shape[1]
    tr = _largest_tile(r, 1024, 8)
    t2 = _largest_tile(k2, 1024, 128)
    nr = r // tr

    def body(a_ref, b_ref, o_ref, acc_ref):
        part = lax.dot_general(a_ref[...].astype(BF16), b_ref[...].astype(BF16), (((0,), (0,)), ((), ())),
                               preferred_element_type=F32)
        i = pl.program_id(1)

        @pl.when(i == 0)
        def _():
            acc_ref[...] = part

        @pl.when(i > 0)
        def _():
            acc_ref[...] += part

        @pl.when(i == nr - 1)
        def _():
            o_ref[...] = acc_ref[...].astype(o_ref.dtype)

    return pl.pallas_call(
        body, grid=(k2 // t2, nr),
        in_specs=[pl.BlockSpec((tr, k1), lambda j, i: (i, 0)), pl.BlockSpec((tr, t2), lambda j, i: (i, j))],
        out_specs=pl.BlockSpec((k1, t2), lambda j, i: (0, j)),
        out_shape=jax.ShapeDtypeStruct((k1, k2), out_dtype),
        scratch_shapes=[pltpu.VMEM((k1, t2), F32)],
        compiler_params=_params("parallel", "arbitrary"), name=name)(a, b)


def _exchange(name, x, bcast, sibling_only=False):
    rels = [1] if sibling_only else list(range(1, N_DEV))
    n_slot = 2 if sibling_only else N_DEV
    blk = x.shape if bcast else x.shape[1:]

    def body(x_ref, o_ref, send_sems, recv_sems, local_sem):
        mx, my, mc = lax.axis_index("x"), lax.axis_index("y"), lax.axis_index("c")
        me = mc if sibling_only else 4 * mx + 2 * my + mc
        me_dev = 4 * mx + 2 * my + mc
        mine = pltpu.make_async_copy(x_ref if bcast else x_ref.at[me_dev], o_ref.at[me], local_sem)
        mine.start()
        copies = []
        for k, r in enumerate(rels):
            px = 1 - mx if (r >> 2) & 1 else mx
            py = 1 - my if (r >> 1) & 1 else my
            pc = 1 - mc if r & 1 else mc
            src = x_ref if bcast else x_ref.at[4 * px + 2 * py + pc]
            cp = pltpu.make_async_remote_copy(
                src_ref=src, dst_ref=o_ref.at[me], send_sem=send_sems.at[k], recv_sem=recv_sems.at[k],
                device_id=(px, py, pc), device_id_type=pl.DeviceIdType.MESH)
            cp.start()
            copies.append(cp)
        for cp in copies:
            cp.wait()
        mine.wait()

    return pl.pallas_call(
        body, out_shape=jax.ShapeDtypeStruct((n_slot,) + tuple(blk), x.dtype),
        in_specs=[pl.BlockSpec(memory_space=pltpu.VMEM if sibling_only else pl.ANY)],
        out_specs=pl.BlockSpec(memory_space=pl.ANY),
        scratch_shapes=[pltpu.SemaphoreType.DMA((len(rels),)), pltpu.SemaphoreType.DMA((len(rels),)),
                        pltpu.SemaphoreType.DMA],
        name=name)(x)


def _gather_two_level(name, x):
    def body(x_ref, o_ref, send_sems, recv_sems, local_sem):
        mx, my, mc = lax.axis_index("x"), lax.axis_index("y"), lax.axis_index("c")
        me, sibling = (mx, my, mc), (mx, my, 1 - mc)
        chips = [(1 - mx, my), (mx, 1 - my), (1 - mx, 1 - my)]

        def slot(px, py, pc):
            return o_ref.at[4 * px + 2 * py + pc]

        def copy(k, block, to, src=None):
            return pltpu.make_async_remote_copy(
                src_ref=slot(*block) if src is None else src, dst_ref=slot(*block), send_sem=send_sems.at[k],
                recv_sem=recv_sems.at[k], device_id=to, device_id_type=pl.DeviceIdType.MESH)

        mine = pltpu.make_async_copy(x_ref, slot(*me), local_sem)
        mine.start()
        first = [copy(0, me, sibling, src=x_ref)]
        first += [copy(1 + j, me, (*chip, mc), src=x_ref) for j, chip in enumerate(chips)]
        for cp in first:
            cp.start()
        passed = [copy(4 + j, (*chip, mc), sibling) for j, chip in enumerate(chips)]
        for j, chip in enumerate(chips):
            copy(1 + j, (*chip, mc), me).wait_recv()
            passed[j].start()
        copy(0, sibling, me).wait_recv()
        for j, chip in enumerate(chips):
            copy(4 + j, (*chip, 1 - mc), me).wait_recv()
        for cp in first + passed:
            cp.wait_send()
        mine.wait()

    return pl.pallas_call(
        body, out_shape=jax.ShapeDtypeStruct((N_DEV,) + tuple(x.shape), x.dtype),
        in_specs=[pl.BlockSpec(memory_space=pl.ANY)], out_specs=pl.BlockSpec(memory_space=pl.ANY),
        scratch_shapes=[pltpu.SemaphoreType.DMA((N_DEV - 1,)), pltpu.SemaphoreType.DMA((N_DEV - 1,)),
                        pltpu.SemaphoreType.DMA],
        name=name)(x)


def _sum_slots(name, x):
    s, r, w = x.shape
    tr = _largest_tile(r, 256, 8)

    def body(x_ref, o_ref):
        acc = x_ref[0].astype(F32)
        for j in range(1, s):
            acc = acc + x_ref[j].astype(F32)
        o_ref[...] = acc

    return pl.pallas_call(
        body, grid=(r // tr,), in_specs=[pl.BlockSpec((s, tr, w), lambda i: (0, i, 0))],
        out_specs=pl.BlockSpec((tr, w), lambda i: (i, 0)), out_shape=jax.ShapeDtypeStruct((r, w), F32),
        compiler_params=_params("parallel"), name=name)(x)


def _sigmoid(x):
    return 1.0 / (1.0 + jnp.exp(-x))


def _silu(x):
    return x * _sigmoid(x)


def _silu_grad(x):
    s = _sigmoid(x)
    return s * (1.0 + x * (1.0 - s))


_INV_SQRT2 = 1.0 / math.sqrt(2.0)
_INV_SQRT2PI = 1.0 / math.sqrt(2.0 * math.pi)


def _gelu(x):
    return 0.5 * x * (1.0 + lax.erf(x * _INV_SQRT2))


def _gelu_grad(x):
    return 0.5 * (1.0 + lax.erf(x * _INV_SQRT2)) + x * jnp.exp(-0.5 * x * x) * _INV_SQRT2PI


def _rms_hat(x):
    r = lax.rsqrt(jnp.mean(x * x, axis=-1, keepdims=True) + NORM_EPS)
    return x * r, r


def _rms_bwd(xh, r, dxh):
    return r * (dxh - xh * jnp.mean(dxh * xh, axis=-1, keepdims=True))


def _stream(x):
    return (x[0], 0, D_MODEL, "pair", x[1]) if isinstance(x, tuple) else (x, 0, D_MODEL)


def _norm_mod_fwd(name, x, g, scale, shift, n_rows, tr, n_lat):
    def fn(xt, gv, sc, sh):
        xh, _ = _rms_hat(xt)
        return (xh * gv) * (1.0 + sc) + sh

    return _rowwise(name, fn, n_rows, tr, [_stream(x)], [g, scale, shift], [(D_MODEL, BF16)], [], n_lat=n_lat)[0]


def _norm_mod_bwd(name, x, g, scale, dh, dres, n_rows, tr, n_lat, prev=None, dx_lat_only=False):
    nlt = n_lat // tr

    def fn(flag, xt, dht, drt, *rest):
        gv, sc = rest[-2:] if prev is None else rest[1:3]
        xh, r = _rms_hat(xt)
        n = xh * gv
        dn = dht * (1.0 + sc)
        dx = _rms_bwd(xh, r, dn * gv) + flag * drt
        if prev is None:
            return dx, dn * xh, dht * n, dht
        return dx, rest[3] * dx, dn * xh, dht * n, dht, dx * rest[0]

    rows = [_stream(x), (dh, 0, D_MODEL), (dres, 0, D_MODEL, "clamp", nlt)]
    row_outs = [(D_MODEL, F32, "lat") if dx_lat_only else (D_MODEL, F32)]
    vecs, reds = [g, scale], [(1, D_MODEL), (2, D_MODEL), (2, D_MODEL)]
    if prev is not None:
        rows.append((prev[0], 0, D_MODEL))
        vecs.append(prev[1])
        row_outs.append((D_MODEL, BF16))
        reds.append((2, D_MODEL))
    return _rowwise(name, fn, n_rows, tr, rows, vecs, row_outs, reds, n_lat=n_lat, want_flag=True)


def _s5_prep(a_re, a_im, log_dt, b_re, b_im, seg_lat, seg_ctx):
    def body(ar_ref, ai_ref, ld_ref, br_ref, bi_ref, abr_ref, abi_ref, bbr_ref, bbi_ref, alr_ref, ali_ref, acr_ref,
             aci_ref):
        lr, li = ar_ref[...], ai_ref[...]
        dt = jnp.exp(ld_ref[...])
        ldr, ldi = lr * dt, li * dt
        e = jnp.exp(ldr)
        abr, abi = e * jnp.cos(ldi), e * jnp.sin(ldi)
        abr_ref[...] = abr
        abi_ref[...] = abi
        den = lr * lr + li * li
        nr, ni = abr - 1.0, abi
        qr = (nr * lr + ni * li) / den
        qi = (ni * lr - nr * li) / den
        br, bi = br_ref[...], bi_ref[...]
        bbr_ref[...] = qr[None] * br - qi[None] * bi
        bbi_ref[...] = qr[None] * bi + qi[None] * br
        for seg, r_ref, i_ref in ((seg_lat, alr_ref, ali_ref), (seg_ctx, acr_ref, aci_ref)):
            es = jnp.exp(ldr * float(seg))
            r_ref[...] = es * jnp.cos(ldi * float(seg))
            i_ref[...] = es * jnp.sin(ldi * float(seg))

    sm = jax.ShapeDtypeStruct(a_re.shape, F32)
    big = jax.ShapeDtypeStruct(b_re.shape, F32)
    return pl.pallas_call(body, out_shape=[sm, sm, big, big, sm, sm, sm, sm], name="s5_prep")(
        a_re, a_im, log_dt, b_re, b_im)


def _s5_prep_bwd(a_re, a_im, log_dt, b_re, b_im, dabr, dabi, dbbr, dbbi):
    def body(ar_ref, ai_ref, ld_ref, br_ref, bi_ref, dabr_ref, dabi_ref, dbbr_ref, dbbi_ref,
             dar_ref, dai_ref, dld_ref, dbr_ref, dbi_ref):
        lr, li = ar_ref[...], ai_ref[...]
        dt = jnp.exp(ld_ref[...])
        ldr, ldi = lr * dt, li * dt
        e = jnp.exp(ldr)
        abr, abi = e * jnp.cos(ldi), e * jnp.sin(ldi)
        den = lr * lr + li * li
        nr, ni = abr - 1.0, abi
        qr = (nr * lr + ni * li) / den
        qi = (ni * lr - nr * li) / den
        br, bi = br_ref[...], bi_ref[...]
        gbr, gbi = dbbr_ref[...], dbbi_ref[...]
        dbr_ref[...] = gbr * qr[None] + gbi * qi[None]
        dbi_ref[...] = gbi * qr[None] - gbr * qi[None]
        dqr = jnp.sum(gbr * br + gbi * bi, axis=0)
        dqi = jnp.sum(gbi * br - gbr * bi, axis=0)
        dnr = (dqr * lr - dqi * li) / den
        dni = (dqr * li + dqi * lr) / den
        dlr_q = (dqr * (nr - 2.0 * lr * qr) + dqi * (ni - 2.0 * lr * qi)) / den
        dli_q = (dqr * (ni - 2.0 * li * qr) + dqi * (-nr - 2.0 * li * qi)) / den
        gar = dabr_ref[...] + dnr
        gai = dabi_ref[...] + dni
        dldr = gar * abr + gai * abi
        dldi = gai * abr - gar * abi
        dar_ref[...] = dldr * dt + dlr_q
        dai_ref[...] = dldi * dt + dli_q
        ddt = jnp.sum(dldr * lr + dldi * li, axis=1, keepdims=True)
        dld_ref[...] = ddt * dt

    sm = jax.ShapeDtypeStruct(a_re.shape, F32)
    big = jax.ShapeDtypeStruct(b_re.shape, F32)
    return pl.pallas_call(body, out_shape=[sm, sm, jax.ShapeDtypeStruct(log_dt.shape, F32), big, big],
                          name="s5_prep_bwd")(a_re, a_im, log_dt, b_re, b_im, dabr, dabi, dbbr, dbbi)


def _slab_cols(v):
    return v.reshape(N_SLAB, 1, HALF_W)


def _slab_pair(vr, vi):
    return jnp.concatenate([_slab_cols(vr), _slab_cols(vi)], axis=-1)


def _slab_in_matrix(bbr, bbi):
    eye = jnp.eye(SLAB_GROUPS, dtype=F32)

    def one(b):
        b = b.reshape(N_SLAB, SLAB_GROUPS, SSM_STATE, SSM_GROUP)
        m = jnp.einsum("sgph,gk->sghkp", b, eye)
        return m.reshape(N_SLAB, SLAB_W, HALF_W)

    return jnp.concatenate([one(bbr), one(bbi)], axis=-1)


def _slab_out_matrix(cr, ci):
    eye = jnp.eye(SLAB_GROUPS, dtype=F32)

    def one(c):
        c = c.reshape(N_SLAB, SLAB_GROUPS, SSM_GROUP, SSM_STATE)
        m = jnp.einsum("sghp,gk->skpgh", c, eye)
        return m.reshape(N_SLAB, HALF_W, SLAB_W)

    return jnp.concatenate([one(cr), one(-ci)], axis=1)


def _slab_diag(m):
    m = m.reshape(N_SLAB, SLAB_GROUPS, SSM_GROUP, 2, SLAB_GROUPS, SSM_STATE)
    d = jnp.stack([m[:, g, :, :, g, :] for g in range(SLAB_GROUPS)], axis=1)
    return d.transpose(3, 0, 1, 2, 4).reshape(2, SSM_GROUPS, SSM_GROUP, SSM_STATE)


def _cmul(ar, ai, xr, xi, conj):
    if conj:
        return ar * xr + ai * xi, ar * xi - ai * xr
    return ar * xr - ai * xi, ar * xi + ai * xr


def _s5_pow_table(name, abar, seg, falling, conj):
    assert seg >= 8 and seg & (seg - 1) == 0

    def body(a_ref, o_ref, t_ref):
        ar, ai = a_ref[0, :, :HALF_W], a_ref[0, :, HALF_W:]
        if conj:
            ai = -ai
        rr, ri = [jnp.ones_like(ar)], [jnp.zeros_like(ai)]
        for _ in range(7):
            pr, pi = _cmul(ar, ai, rr[-1], ri[-1], False)
            rr.append(pr)
            ri.append(pi)
        sr, si = _cmul(ar, ai, rr[-1], ri[-1], False)
        if falling:
            rr, ri = rr[::-1], ri[::-1]
        first = slice(seg - 8, seg) if falling else slice(0, 8)
        t_ref[first, :HALF_W] = jnp.concatenate(rr, axis=0)
        t_ref[first, HALF_W:] = jnp.concatenate(ri, axis=0)
        size = 8
        while size < seg:
            src = slice(seg - size, seg) if falling else slice(0, size)
            dst = slice(seg - 2 * size, seg - size) if falling else slice(size, 2 * size)
            pr, pi = _cmul(sr, si, t_ref[src, :HALF_W], t_ref[src, HALF_W:], False)
            t_ref[dst, :HALF_W] = pr
            t_ref[dst, HALF_W:] = pi
            sr, si = _cmul(sr, si, sr, si, False)
            size *= 2
        o_ref[0] = t_ref[...].astype(BF16)

    return pl.pallas_call(
        body, grid=(N_SLAB,), in_specs=[pl.BlockSpec((1, 1, STATE_W), lambda s: (s, 0, 0))],
        out_specs=pl.BlockSpec((1, seg, STATE_W), lambda s: (s, 0, 0)),
        out_shape=jax.ShapeDtypeStruct((N_SLAB, seg, STATE_W), BF16),
        scratch_shapes=[pltpu.VMEM((seg, STATE_W), F32)], compiler_params=_params("parallel"), name=name)(abar)


def _s5_ends(name, x, n_rows, row0, table, m_mat):
    seg = n_rows // N_SEG
    rb = row0 // n_rows
    tn = (((0,), (0,)), ((), ()))

    def body(x_ref, t_ref, m_ref, z_ref):
        mr, mi = m_ref[0, :, :HALF_W], m_ref[0, :, HALF_W:]
        for j in range(N_SEG):
            t = lax.dot_general(x_ref[j * seg:(j + 1) * seg, :].astype(BF16), t_ref[0], tn,
                                preferred_element_type=F32)
            tr_, ti_ = t[:, :HALF_W], t[:, HALF_W:]
            z_ref[0, j:j + 1, :HALF_W] = jnp.sum(mr * tr_ - mi * ti_, axis=0, keepdims=True)
            z_ref[0, j:j + 1, HALF_W:] = jnp.sum(mr * ti_ + mi * tr_, axis=0, keepdims=True)

    return pl.pallas_call(
        body, grid=(N_SLAB,),
        in_specs=[pl.BlockSpec((n_rows, SLAB_W), lambda s: (rb, s)),
                  pl.BlockSpec((1, seg, STATE_W), lambda s: (s, 0, 0)),
                  pl.BlockSpec((1, SLAB_W, STATE_W), lambda s: (s, 0, 0))],
        out_specs=pl.BlockSpec((1, N_SEG, STATE_W), lambda s: (s, 0, 0)),
        out_shape=jax.ShapeDtypeStruct((N_SLAB, N_SEG, STATE_W), F32),
        compiler_params=_params("parallel"), name=name)(x, table, m_mat)


def _s5_carry(name, z, a_seg, init, descending, conj):
    order = list(range(N_SEG - 1, -1, -1)) if descending else list(range(N_SEG))

    def body(z_ref, a_ref, i_ref, e_ref, o_ref):
        ar, ai = a_ref[:, :HALF_W], a_ref[:, HALF_W:]
        cr, ci = i_ref[:, :HALF_W], i_ref[:, HALF_W:]
        for j in order:
            e_ref[:, j, :HALF_W] = cr
            e_ref[:, j, HALF_W:] = ci
            pr, pi = _cmul(ar, ai, cr, ci, conj)
            cr = pr + z_ref[:, j, :HALF_W]
            ci = pi + z_ref[:, j, HALF_W:]
        o_ref[:, :HALF_W] = cr
        o_ref[:, HALF_W:] = ci

    return pl.pallas_call(body, out_shape=[jax.ShapeDtypeStruct(z.shape, F32), jax.ShapeDtypeStruct(init.shape, F32)],
                          name=name)(z, a_seg, init)


def _s5_scan(name, u, n_rows, row0, b_mat, c_mat, abar, h_in, descending, y_alias=None, y_rows=None):
    seg = n_rows // N_SEG
    ta = min(32, seg)
    nk = seg // ta
    assert seg * N_SEG == n_rows and nk * ta == seg and row0 % n_rows == 0 and ta % 8 == 0
    rb = row0 // n_rows
    tile = ta * N_SEG

    def body(*refs):
        u_ref, b_ref, c_ref, a_ref, hin_ref = refs[:5]
        y_ref, hch_ref, st_ref, up_ref, h_ref = refs[-5:]
        k = pl.program_id(1)
        kk = nk - 1 - k if descending else k
        a0 = kk * ta

        @pl.when(k == 0)
        def _():
            st_ref[...] = hin_ref[0]

        hch_ref[0, 0] = st_ref[...]
        for al in range(ta):
            up_ref[al * N_SEG:(al + 1) * N_SEG, :] = u_ref[pl.ds(a0 + al, N_SEG, stride=seg), :]
        h_ref[...] = jnp.dot(up_ref[...].astype(BF16), b_ref[0], preferred_element_type=F32)
        ar = jnp.broadcast_to(a_ref[0, :, :HALF_W], (N_SEG, HALF_W))
        ai = jnp.broadcast_to(a_ref[0, :, HALF_W:], (N_SEG, HALF_W))

        def step(i, carry):
            hr, hi = carry
            al = ta - 1 - i if descending else i
            row = pl.multiple_of(al * N_SEG, N_SEG)
            pr, pi = _cmul(ar, ai, hr, hi, False)
            hr = pr + h_ref[pl.ds(row, N_SEG), :HALF_W]
            hi = pi + h_ref[pl.ds(row, N_SEG), HALF_W:]
            h_ref[pl.ds(row, N_SEG), :HALF_W] = hr
            h_ref[pl.ds(row, N_SEG), HALF_W:] = hi
            return hr, hi

        hr, hi = lax.fori_loop(0, ta, step, (st_ref[:, :HALF_W], st_ref[:, HALF_W:]), unroll=True)
        st_ref[:, :HALF_W] = hr
        st_ref[:, HALF_W:] = hi
        yt = jnp.dot(h_ref[...].astype(BF16), c_ref[0], preferred_element_type=F32)
        for al in range(ta):
            y_ref[pl.ds(a0 + al, N_SEG, stride=seg), :] = yt[al * N_SEG:(al + 1) * N_SEG, :]

    u_spec = pl.BlockSpec((n_rows, SLAB_W), lambda s, k: (rb, s))
    b_spec = pl.BlockSpec((1, SLAB_W, STATE_W), lambda s, k: (s, 0, 0))
    c_spec = pl.BlockSpec((1, STATE_W, SLAB_W), lambda s, k: (s, 0, 0))
    a_spec = pl.BlockSpec((1, 1, STATE_W), lambda s, k: (s, 0, 0))
    st_spec = pl.BlockSpec((1, N_SEG, STATE_W), lambda s, k: (s, 0, 0))
    scratch = [pltpu.VMEM((N_SEG, STATE_W), F32), pltpu.VMEM((tile, SLAB_W), F32), pltpu.VMEM((tile, STATE_W), F32)]
    kmap = (lambda s, k: (s, nk - 1 - k, 0, 0)) if descending else (lambda s, k: (s, k, 0, 0))
    out_specs = [u_spec, pl.BlockSpec((1, 1, N_SEG, STATE_W), kmap)]
    out_shape = [jax.ShapeDtypeStruct((y_rows, D_MODEL), F32), jax.ShapeDtypeStruct((N_SLAB, nk, N_SEG, STATE_W), F32)]
    in_specs = [u_spec, b_spec, c_spec, a_spec, st_spec]
    args = [u, b_mat, c_mat, abar, h_in]
    aliases = {}
    if y_alias is not None:
        in_specs.append(pl.BlockSpec(memory_space=pl.ANY))
        args.append(y_alias)
        aliases = {5: 0}
    return pl.pallas_call(
        body, grid=(N_SLAB, nk), in_specs=in_specs, out_specs=out_specs, out_shape=out_shape, scratch_shapes=scratch,
        input_output_aliases=aliases, compiler_params=_params("parallel", "arbitrary"), name=name)(*args)


def _s5_scan_bwd(name, u, dy, n_rows, row0, b_mat, bt_mat, ct_mat, abar, h_chunks, g_in, descending,
                 du_alias=None, du_rows=None):
    seg = n_rows // N_SEG
    ta = min(32, seg)
    nk = seg // ta
    rb = row0 // n_rows
    tile = ta * N_SEG
    g_desc = not descending

    def body(*refs):
        u_ref, dy_ref, b_ref, bt_ref, ct_ref, a_ref, hch_ref, gin_ref = refs[:8]
        du_ref, db_ref, dc_ref, da_ref, st_ref, up_ref, dyp_ref, h_ref, g_ref = refs[-9:]
        k = pl.program_id(1)
        kk = nk - 1 - k if g_desc else k
        a0 = kk * ta
        ar = jnp.broadcast_to(a_ref[0, :, :HALF_W], (N_SEG, HALF_W))
        ai = jnp.broadcast_to(a_ref[0, :, HALF_W:], (N_SEG, HALF_W))

        @pl.when(k == 0)
        def _():
            st_ref[...] = gin_ref[0]

        for al in range(ta):
            dyp_ref[al * N_SEG:(al + 1) * N_SEG, :] = dy_ref[pl.ds(a0 + al, N_SEG, stride=seg), :]
            up_ref[al * N_SEG:(al + 1) * N_SEG, :] = u_ref[pl.ds(a0 + al, N_SEG, stride=seg), :]
        g_ref[...] = jnp.dot(dyp_ref[...].astype(BF16), ct_ref[0], preferred_element_type=F32)
        h_ref[...] = jnp.dot(up_ref[...].astype(BF16), b_ref[0], preferred_element_type=F32)
        h0r, h0i = hch_ref[0, 0, :, :HALF_W], hch_ref[0, 0, :, HALF_W:]

        def hstep(i, carry):
            hr, hi = carry
            al = ta - 1 - i if descending else i
            row = pl.multiple_of(al * N_SEG, N_SEG)
            pr, pi = _cmul(ar, ai, hr, hi, False)
            hr = pr + h_ref[pl.ds(row, N_SEG), :HALF_W]
            hi = pi + h_ref[pl.ds(row, N_SEG), HALF_W:]
            h_ref[pl.ds(row, N_SEG), :HALF_W] = hr
            h_ref[pl.ds(row, N_SEG), HALF_W:] = hi
            return hr, hi

        lax.fori_loop(0, ta, hstep, (h0r, h0i), unroll=True)

        def gstep(i, carry):
            gr, gi = carry
            al = ta - 1 - i if g_desc else i
            row = pl.multiple_of(al * N_SEG, N_SEG)
            pr, pi = _cmul(ar, ai, gr, gi, True)
            gr = pr + g_ref[pl.ds(row, N_SEG), :HALF_W]
            gi = pi + g_ref[pl.ds(row, N_SEG), HALF_W:]
            g_ref[pl.ds(row, N_SEG), :HALF_W] = gr
            g_ref[pl.ds(row, N_SEG), HALF_W:] = gi
            return gr, gi

        gr, gi = lax.fori_loop(0, ta, gstep, (st_ref[:, :HALF_W], st_ref[:, HALF_W:]), unroll=True)
        st_ref[:, :HALF_W] = gr
        st_ref[:, HALF_W:] = gi

        gb = g_ref[...].astype(BF16)
        dut = jnp.dot(gb, bt_ref[0], preferred_element_type=F32)
        for al in range(ta):
            du_ref[pl.ds(a0 + al, N_SEG, stride=seg), :] = dut[al * N_SEG:(al + 1) * N_SEG, :]
        tn = (((0,), (0,)), ((), ()))
        dbp = lax.dot_general(up_ref[...].astype(BF16), gb, tn, preferred_element_type=F32)
        dcp = lax.dot_general(dyp_ref[...].astype(BF16), h_ref[...].astype(BF16), tn, preferred_element_type=F32)
        inner = (ta - 1) * N_SEG
        if descending:
            g_in_r, g_in_i = g_ref[0:inner, :HALF_W], g_ref[0:inner, HALF_W:]
            p_in_r, p_in_i = h_ref[N_SEG:tile, :HALF_W], h_ref[N_SEG:tile, HALF_W:]
            g_ed_r, g_ed_i = g_ref[inner:tile, :HALF_W], g_ref[inner:tile, HALF_W:]
        else:
            g_in_r, g_in_i = g_ref[N_SEG:tile, :HALF_W], g_ref[N_SEG:tile, HALF_W:]
            p_in_r, p_in_i = h_ref[0:inner, :HALF_W], h_ref[0:inner, HALF_W:]
            g_ed_r, g_ed_i = g_ref[0:N_SEG, :HALF_W], g_ref[0:N_SEG, HALF_W:]
        dar = g_ed_r * h0r + g_ed_i * h0i
        dai = g_ed_i * h0r - g_ed_r * h0i
        if ta > 1:
            dar = dar + jnp.sum((g_in_r * p_in_r + g_in_i * p_in_i).reshape(ta - 1, N_SEG, HALF_W), axis=0)
            dai = dai + jnp.sum((g_in_i * p_in_r - g_in_r * p_in_i).reshape(ta - 1, N_SEG, HALF_W), axis=0)

        @pl.when(k == 0)
        def _():
            db_ref[0] = dbp
            dc_ref[0] = dcp
            da_ref[0, :, :HALF_W] = dar
            da_ref[0, :, HALF_W:] = dai

        @pl.when(k > 0)
        def _():
            db_ref[0] += dbp
            dc_ref[0] += dcp
            da_ref[0, :, :HALF_W] += dar
            da_ref[0, :, HALF_W:] += dai

    u_spec = pl.BlockSpec((n_rows, SLAB_W), lambda s, k: (rb, s))
    m_spec = pl.BlockSpec((1, SLAB_W, STATE_W), lambda s, k: (s, 0, 0))
    mt_spec = pl.BlockSpec((1, STATE_W, SLAB_W), lambda s, k: (s, 0, 0))
    a_spec = pl.BlockSpec((1, 1, STATE_W), lambda s, k: (s, 0, 0))
    st_spec = pl.BlockSpec((1, N_SEG, STATE_W), lambda s, k: (s, 0, 0))
    st_shape = jax.ShapeDtypeStruct((N_SLAB, N_SEG, STATE_W), F32)
    kmap = (lambda s, k: (s, nk - 1 - k, 0, 0)) if g_desc else (lambda s, k: (s, k, 0, 0))
    in_specs = [u_spec, u_spec, m_spec, mt_spec, m_spec, a_spec, pl.BlockSpec((1, 1, N_SEG, STATE_W), kmap), st_spec]
    args = [u, dy, b_mat, bt_mat, ct_mat, abar, h_chunks, g_in]
    aliases = {}
    if du_alias is not None:
        in_specs.append(pl.BlockSpec(memory_space=pl.ANY))
        args.append(du_alias)
        aliases = {8: 0}
    acc_shape = jax.ShapeDtypeStruct((N_SLAB, SLAB_W, STATE_W), F32)
    out_specs = [u_spec, m_spec, m_spec, st_spec]
    out_shape = [jax.ShapeDtypeStruct((du_rows, D_MODEL), F32), acc_shape, acc_shape, st_shape]
    scratch = [pltpu.VMEM((N_SEG, STATE_W), F32), pltpu.VMEM((tile, SLAB_W), F32), pltpu.VMEM((tile, SLAB_W), F32),
               pltpu.VMEM((tile, STATE_W), F32), pltpu.VMEM((tile, STATE_W), F32)]
    return pl.pallas_call(
        body, grid=(N_SLAB, nk), in_specs=in_specs, out_specs=out_specs, out_shape=out_shape, scratch_shapes=scratch,
        input_output_aliases=aliases, compiler_params=_params("parallel", "arbitrary"), name=name)(*args)


ROPE_HALF = HEAD_DIM // 4
TABLE_W = 2 * HEAD_DIM
Q_SCALE = 1.0 / math.sqrt(HEAD_DIM)
HEADS_PER_BLOCK = 2 * KV_REP
Q_BLOCK_W = HEADS_PER_BLOCK * HEAD_DIM


def _rope_tables(n_lat, n_ctx):
    rows = n_lat // GRID_W
    freqs = ROPE_THETA ** (-jnp.arange(ROPE_HALF, dtype=F32) / ROPE_HALF)
    ang_r = jnp.arange(rows, dtype=F32)[:, None] * freqs[None]
    ang_c = jnp.arange(GRID_W, dtype=F32)[:, None] * freqs[None]
    by_row = lambda v: jnp.repeat(v, GRID_W, axis=0)
    by_col = lambda v: jnp.tile(v, (rows, 1))
    cos = jnp.concatenate([by_row(jnp.cos(ang_r)), by_row(jnp.cos(ang_r)), by_col(jnp.cos(ang_c)), by_col(jnp.cos(ang_c))] * 2,
                          axis=1)
    sin = jnp.concatenate([by_row(jnp.sin(ang_r)), by_row(jnp.sin(ang_r)), by_col(jnp.sin(ang_c)), by_col(jnp.sin(ang_c))] * 2,
                          axis=1)
    cos = jnp.concatenate([cos, jnp.ones((n_ctx, TABLE_W), F32)], axis=0)
    sin = jnp.concatenate([sin, jnp.zeros((n_ctx, TABLE_W), F32)], axis=0)
    return cos, sin


def _rot_half(v):
    w = v.shape[1]
    ahead = pltpu.roll(v, w - ROPE_HALF, axis=1)
    behind = pltpu.roll(v, ROPE_HALF, axis=1)
    lane = lax.broadcasted_iota(jnp.int32, v.shape, 1)
    return jnp.where((lane % (2 * ROPE_HALF)) < ROPE_HALF, -ahead, behind)


def _head_mean(v, sel, selt):
    m = jnp.dot(v, sel, precision=lax.Precision.HIGH, preferred_element_type=F32) * (1.0 / HEAD_DIM)
    return jnp.dot(m, selt, precision=lax.Precision.HIGH, preferred_element_type=F32)


def _head_selectors(n_heads):
    sel = jnp.repeat(jnp.eye(n_heads, dtype=F32), HEAD_DIM, axis=0)
    return sel[None], sel.T[None]


def _head_norm(x, sel, selt):
    r = lax.rsqrt(_head_mean(x * x, sel, selt) + NORM_EPS)
    return x * r, r


def _qk_prep(proj, qn, kn, cos, sin, n, tr):
    qw, kw = _vec(jnp.tile(qn, N_Q_HEADS)), _vec(jnp.tile(kn, N_KV_HEADS))
    sq, sqt = _head_selectors(N_Q_HEADS)
    sk, skt = _head_selectors(N_KV_HEADS)

    def fn(qr, kvr, ct, st, qwv, kwv, s16, s16t, s4, s4t):
        outs = []
        for x, wv, sel, selt, scale in ((qr, qwv, s16, s16t, Q_SCALE), (kvr[:, :KV_W], kwv, s4, s4t, 1.0)):
            reps = x.shape[1] // TABLE_W
            cw, sw = jnp.tile(ct, (1, reps)), jnp.tile(st, (1, reps))
            xh, _ = _head_norm(x, sel, selt)
            nrm = xh * wv
            outs.append((nrm * cw + _rot_half(nrm) * sw) * scale)
        return outs[0], outs[1], kvr[:, KV_W:]

    return _rowwise("l1_qk_prep", fn, n, tr,
                    [(proj, 0, ATTN_W), (proj, 2 * ATTN_W // (2 * KV_W), 2 * KV_W), (cos, 0, TABLE_W), (sin, 0, TABLE_W)],
                    [qw, kw, sq, sqt, sk, skt], [(ATTN_W, BF16), (KV_W, BF16), (KV_W, BF16)], [])


def _qk_prep_bwd(proj, qn, kn, cos, sin, dq, dz, dk, dv, n, n_lat, tr):
    qw, kw = _vec(jnp.tile(qn, N_Q_HEADS)), _vec(jnp.tile(kn, N_KV_HEADS))
    sq, sqt = _head_selectors(N_Q_HEADS)
    sk, skt = _head_selectors(N_KV_HEADS)
    nlt = n_lat // tr

    def fn(flag, qr, kvr, ct, st, dqt, dzt, dkt, dvt, qwv, kwv, s16, s16t, s4, s4t):
        dxs, dws = [], []
        for x, dy, wv, sel, selt in ((qr, dqt * (flag * Q_SCALE), qwv, s16, s16t), (kvr[:, :KV_W], dkt, kwv, s4, s4t)):
            reps = x.shape[1] // TABLE_W
            cw, sw = jnp.tile(ct, (1, reps)), jnp.tile(st, (1, reps))
            xh, r = _head_norm(x, sel, selt)
            dn = dy * cw - _rot_half(dy * sw)
            dxh = dn * wv
            dxs.append(r * (dxh - xh * _head_mean(dxh * xh, sel, selt)))
            dws.append(dn * xh)
        return jnp.concatenate([dxs[0], dzt * flag, dxs[1], dvt], axis=1), dws[0], dws[1]

    dproj, dqw, dkw = _rowwise(
        "l1_qk_prep_bwd", fn, n, tr,
        [(proj, 0, ATTN_W), (proj, 2 * ATTN_W // (2 * KV_W), 2 * KV_W), (cos, 0, TABLE_W), (sin, 0, TABLE_W),
         (dq, 0, ATTN_W, "clamp", nlt), (dz, 0, ATTN_W, "clamp", nlt), (dk, 0, KV_W), (dv, 0, KV_W)],
        [qw, kw, sq, sqt, sk, skt], [(2 * ATTN_W + 2 * KV_W, BF16)], [(1, ATTN_W), (1, KV_W)], n_lat=n_lat, want_flag=True)
    return dproj, dqw.reshape(N_Q_HEADS, HEAD_DIM).sum(0)[None], dkw.reshape(N_KV_HEADS, HEAD_DIM).sum(0)[None]


NT = (((1,), (1,)), ((), ()))


def _attn_fwd(q, k, v, proj, t, tq, tk):
    n = k.shape[0]
    nkc = n // tk

    ts = _largest_tile(tq, 256, LANES)
    items = [(sub, j) for sub in range(tq // ts) for j in range(HEADS_PER_BLOCK)]

    def body(q_ref, k_ref, v_ref, z_ref, o_ref, lse_ref, gz_ref, s_ref, m_ref, l_ref, acc_ref):
        def lanes(j):
            g = j // KV_REP
            return slice(j * HEAD_DIM, (j + 1) * HEAD_DIM), slice(g * HEAD_DIM, (g + 1) * HEAD_DIM)

        for idx in range(len(items) + 1):
            nxt = items[idx] if idx < len(items) else None
            cur = items[idx - 1] if idx > 0 else None
            sn, sc = idx % 2, (idx - 1) % 2
            if nxt is not None:
                rows_n = slice(nxt[0] * ts, (nxt[0] + 1) * ts)
                ql_n, kl_n = lanes(nxt[1])
                qv = q_ref[rows_n, ql_n]
                m_ref[sn] = jnp.full((ts, LANES), -jnp.inf, F32)
            if cur is not None:
                rows_c = slice(cur[0] * ts, (cur[0] + 1) * ts)
                ql_c, kl_c = lanes(cur[1])
                m_row = jnp.max(m_ref[sc], axis=-1, keepdims=True)
                l_ref[...] = jnp.zeros(l_ref.shape, F32)
                acc_ref[...] = jnp.zeros(acc_ref.shape, F32)

            def sweep(kc, c):
                off = pl.multiple_of(kc * tk, tk)
                if nxt is not None:
                    s = lax.dot_general(qv, k_ref[pl.ds(off, tk), kl_n], NT, preferred_element_type=F32)
                    s_ref[sn, :, pl.ds(off, tk)] = s
                    m = m_ref[sn]
                    for cb in range(tk // LANES):
                        m = jnp.maximum(m, s[:, cb * LANES:(cb + 1) * LANES])
                    m_ref[sn] = m
                if cur is not None:
                    p = jnp.exp(s_ref[sc, :, pl.ds(off, tk)] - m_row)
                    lsum = l_ref[...]
                    for cb in range(tk // LANES):
                        lsum = lsum + p[:, cb * LANES:(cb + 1) * LANES]
                    l_ref[...] = lsum
                    acc_ref[...] += jnp.dot(p.astype(BF16), v_ref[pl.ds(off, tk), kl_c], preferred_element_type=F32)
                return c

            lax.fori_loop(0, nkc, sweep, 0, unroll=True)
            if cur is not None:
                l_row = jnp.sum(l_ref[...], axis=-1, keepdims=True)
                o_head = acc_ref[...] / l_row
                o_ref[rows_c, ql_c] = o_head
                gz_ref[rows_c, ql_c] = (o_head * _silu(z_ref[rows_c, ql_c].astype(F32))).astype(BF16)
                lse_ref[0, rows_c, cur[1]:cur[1] + 1] = m_row + jnp.log(l_row)

    nb = ATTN_W // Q_BLOCK_W
    kspec = pl.BlockSpec((n, LANES), lambda b, i: (0, b))
    return pl.pallas_call(
        body, grid=(nb, t // tq),
        in_specs=[pl.BlockSpec((tq, Q_BLOCK_W), lambda b, i: (i, b)), kspec, kspec,
                  pl.BlockSpec((tq, Q_BLOCK_W), lambda b, i: (i, nb + b))],
        out_specs=[pl.BlockSpec((tq, Q_BLOCK_W), lambda b, i: (i, b)),
                   pl.BlockSpec((1, tq, HEADS_PER_BLOCK), lambda b, i: (b, i, 0)),
                   pl.BlockSpec((tq, Q_BLOCK_W), lambda b, i: (i, b))],
        out_shape=[jax.ShapeDtypeStruct((t, ATTN_W), F32), jax.ShapeDtypeStruct((nb, t, HEADS_PER_BLOCK), F32),
                   jax.ShapeDtypeStruct((t, ATTN_W), BF16)],
        scratch_shapes=[pltpu.VMEM((2, ts, n), F32), pltpu.VMEM((2, ts, LANES), F32), pltpu.VMEM((ts, LANES), F32),
                        pltpu.VMEM((ts, HEAD_DIM), F32)],
        compiler_params=_params("parallel", "parallel"), name="attn_fwd")(q, k, v, proj)


def _attn_bwd(q, k, v, dgz, proj, o, lse, t, tq, tk):
    n = k.shape[0]
    nkc = n // tk
    tn = (((0,), (0,)), ((), ()))

    def body(q_ref, k_ref, v_ref, dgz_ref, z_ref, o_ref, lse_ref, dq_ref, dk_ref, dv_ref, dz_ref, acc_ref):
        @pl.when(pl.program_id(1) == 0)
        def _():
            dk_ref[...] = jnp.zeros(dk_ref.shape, F32)
            dv_ref[...] = jnp.zeros(dv_ref.shape, F32)

        for j0 in range(0, HEADS_PER_BLOCK, 2):
            kl = slice((j0 // KV_REP) * HEAD_DIM, (j0 // KV_REP + 1) * HEAD_DIM)
            heads = []
            for a in range(2):
                j = j0 + a
                ql = slice(j * HEAD_DIM, (j + 1) * HEAD_DIM)
                qv, ov = q_ref[:, ql], o_ref[:, ql]
                zv, dgv = z_ref[:, ql].astype(F32), dgz_ref[:, ql].astype(F32)
                dov = (dgv * _silu(zv)).astype(BF16)
                dz_ref[:, ql] = (dgv * ov * _silu_grad(zv)).astype(dz_ref.dtype)
                dl_v = jnp.sum(dov.astype(F32) * ov, axis=-1, keepdims=True)
                heads.append((ql, qv, dov, dl_v, lse_ref[0, :, j:j + 1]))
                acc_ref[a] = jnp.zeros((tq, HEAD_DIM), F32)

            def step(kc, c):
                off = pl.multiple_of(kc * tk, tk)
                kt = k_ref[pl.ds(off, tk), kl]
                vt = v_ref[pl.ds(off, tk), kl]
                dv_part, dk_part = None, None
                for a, (_, qv, dov, dl_v, lse_v) in enumerate(heads):
                    s = lax.dot_general(qv, kt, NT, preferred_element_type=F32)
                    p = jnp.exp(s - lse_v)
                    dp = lax.dot_general(dov, vt, NT, preferred_element_type=F32)
                    ds = (p * (dp - dl_v)).astype(BF16)
                    acc_ref[a] += jnp.dot(ds, kt, preferred_element_type=F32)
                    dvp = lax.dot_general(p.astype(BF16), dov, tn, preferred_element_type=F32)
                    dkp = lax.dot_general(ds, qv, tn, preferred_element_type=F32)
                    dv_part = dvp if dv_part is None else dv_part + dvp
                    dk_part = dkp if dk_part is None else dk_part + dkp
                dv_ref[pl.ds(off, tk), kl] += dv_part
                dk_ref[pl.ds(off, tk), kl] += dk_part
                return c

            lax.fori_loop(0, nkc, step, 0, unroll=2)
            for a, h in enumerate(heads):
                dq_ref[:, h[0]] = acc_ref[a].astype(dq_ref.dtype)

    nb = ATTN_W // Q_BLOCK_W
    qspec = pl.BlockSpec((tq, Q_BLOCK_W), lambda b, i: (i, b))
    kspec = pl.BlockSpec((n, LANES), lambda b, i: (0, b))
    cspec = pl.BlockSpec((1, tq, HEADS_PER_BLOCK), lambda b, i: (b, i, 0))
    return pl.pallas_call(
        body, grid=(nb, t // tq),
        in_specs=[qspec, kspec, kspec, qspec, pl.BlockSpec((tq, Q_BLOCK_W), lambda b, i: (i, nb + b)), qspec, cspec],
        out_specs=[qspec, kspec, kspec, qspec],
        out_shape=[jax.ShapeDtypeStruct((t, ATTN_W), BF16), jax.ShapeDtypeStruct((n, KV_W), F32),
                   jax.ShapeDtypeStruct((n, KV_W), F32), jax.ShapeDtypeStruct((t, ATTN_W), BF16)],
        scratch_shapes=[pltpu.VMEM((2, tq, HEAD_DIM), F32)],
        compiler_params=_params("parallel", "arbitrary"), name="attn_bwd")(q, k, v, dgz, proj, o, lse)


def _s5_system(p, n_lat, n_ctx):
    two_g = 2 * SSM_GROUPS
    a_re = p["ssm_a_re"].reshape(two_g, SSM_STATE)
    a_im = p["ssm_a_im"].reshape(two_g, SSM_STATE)
    log_dt = p["ssm_log_dt"].reshape(two_g, 1)
    b_re = p["ssm_b_re"].reshape(two_g, SSM_STATE, SSM_GROUP).transpose(2, 0, 1)
    b_im = p["ssm_b_im"].reshape(two_g, SSM_STATE, SSM_GROUP).transpose(2, 0, 1)
    raw = (a_re, a_im, log_dt, b_re, b_im)
    abr, abi, bbr, bbi, alr, ali, acr, aci = _s5_prep(*raw, n_lat // N_SEG, n_ctx // N_SEG)
    dirs = []
    for d in range(2):
        g = slice(d * SSM_GROUPS, (d + 1) * SSM_GROUPS)
        b_mat = _slab_in_matrix(bbr[:, g].transpose(1, 2, 0), bbi[:, g].transpose(1, 2, 0))
        c_mat = _slab_out_matrix(p["ssm_c_re"][0, d], p["ssm_c_im"][0, d])
        abar = _slab_pair(abr[g], abi[g])
        tables = {}
        for part, seg in (("lat", n_lat // N_SEG), ("ctx", n_ctx // N_SEG)):
            tables["h_" + part] = _s5_pow_table(f"s5_pow_h{d}_{part}", abar, seg, d == 0, False)
            tables["g_" + part] = _s5_pow_table(f"s5_pow_g{d}_{part}", abar, seg, d == 1, True)
        dirs.append(dict(
            b=b_mat.astype(BF16), bt=b_mat.transpose(0, 2, 1).astype(BF16), b32=b_mat,
            c=c_mat.astype(BF16), ct=c_mat.transpose(0, 2, 1).astype(BF16), ct32=c_mat.transpose(0, 2, 1),
            abar=abar, a_lat=_slab_pair(alr[g], ali[g])[:, 0], a_ctx=_slab_pair(acr[g], aci[g])[:, 0], **tables))
    return raw, dirs


def _s5_forward(proj, dirs, n_lat, n_ctx):
    n = n_lat + n_ctx
    zero_c = jnp.zeros((N_SLAB, STATE_W), F32)
    ys, saved = [], []
    for d, s in enumerate(dirs):
        desc = d == 1
        tag = f"s5f{d}"
        zc = _s5_ends(tag + "_ctx_ends", proj, n_ctx, n_lat, s["h_ctx"], s["b32"])
        ent_c, h0 = _s5_carry(tag + "_ctx_carry", zc, s["a_ctx"], zero_c, desc, False)
        y, hch_c = _s5_scan(tag + "_ctx", proj, n_ctx, n_lat, s["b"], s["c"], s["abar"], ent_c, desc,
                            y_rows=n)
        zl = _s5_ends(tag + "_lat_ends", proj, n_lat, 0, s["h_lat"], s["b32"])
        ent_l, _ = _s5_carry(tag + "_lat_carry", zl, s["a_lat"], h0, desc, False)
        y, hch_l = _s5_scan(tag + "_lat", proj, n_lat, 0, s["b"], s["c"], s["abar"], ent_l, desc,
                            y_alias=y, y_rows=n)
        ys.append(y)
        saved.append((hch_l, hch_c))
    return ys, saved


def _s5_backward(proj, dy, dirs, saved, n_lat, n_ctx):
    n = n_lat + n_ctx
    zero_c = jnp.zeros((N_SLAB, STATE_W), F32)
    out = []
    for d, s in enumerate(dirs):
        desc = d == 1
        tag = f"s5b{d}"
        hch_l, hch_c = saved[d]
        gl = _s5_ends(tag + "_lat_ends", dy, n_lat, 0, s["g_lat"], s["ct32"])
        ent_l, g0 = _s5_carry(tag + "_lat_carry", gl, s["a_lat"], zero_c, not desc, True)
        du, db_l, dc_l, da_l = _s5_scan_bwd(tag + "_lat", proj, dy, n_lat, 0, s["b"], s["bt"], s["ct"], s["abar"],
                                            hch_l, ent_l, desc, du_rows=n)
        gc = _s5_ends(tag + "_ctx_ends", dy, n_ctx, n_lat, s["g_ctx"], s["ct32"])
        ent_c, _ = _s5_carry(tag + "_ctx_carry", gc, s["a_ctx"], g0, not desc, True)
        du, db_c, dc_c, da_c = _s5_scan_bwd(tag + "_ctx", proj, dy, n_ctx, n_lat, s["b"], s["bt"], s["ct"], s["abar"],
                                            hch_c, ent_c, desc, du_alias=du, du_rows=n)
        out.append((du, db_l + db_c, dc_l + dc_c, da_l + da_c))
    return out


def _s5_param_grads(raw, bwd):
    dabr, dabi, dbbr, dbbi, dcr, dci = [], [], [], [], [], []
    for _, db, dc, da in bwd:
        da = jnp.sum(da, axis=1)
        dabr.append(da[:, :HALF_W].reshape(SSM_GROUPS, SSM_STATE))
        dabi.append(da[:, HALF_W:].reshape(SSM_GROUPS, SSM_STATE))
        dbd = _slab_diag(db)
        dbbr.append(dbd[0].transpose(1, 0, 2))
        dbbi.append(dbd[1].transpose(1, 0, 2))
        dcd = _slab_diag(dc)
        dcr.append(dcd[0])
        dci.append(-dcd[1])
    cat = lambda xs, ax: jnp.concatenate(xs, axis=ax)
    dar, dai, dld, dbr, dbi = _s5_prep_bwd(*raw, cat(dabr, 0), cat(dabi, 0), cat(dbbr, 1), cat(dbbi, 1))
    shp = (1, 2, SSM_GROUPS, SSM_STATE)
    b_shape = (1, 2, SSM_GROUPS, SSM_STATE, SSM_GROUP)
    return dict(
        ssm_a_re=dar.reshape(shp), ssm_a_im=dai.reshape(shp), ssm_log_dt=dld.reshape(1, 2, SSM_GROUPS),
        ssm_b_re=dbr.transpose(1, 2, 0).reshape(b_shape), ssm_b_im=dbi.transpose(1, 2, 0).reshape(b_shape),
        ssm_c_re=jnp.stack(dcr)[None], ssm_c_im=jnp.stack(dci)[None])


def _example_step(x, ctx, target, mods, w, p):
    t, c = x.shape[0], ctx.shape[0]
    n = t + c
    assert t % c == 0 and c % LANES == 0 and c % (8 * N_SEG) == 0 and t % GRID_W == 0
    tr = _largest_tile(c, 256, 8)
    xall = (x, ctx)
    g0, g1 = _vec(p["norm_g"][0]), _vec(p["norm_g"][1])
    (shift0, scale0, gate0), (shift1, scale1, gate1) = [tuple(_vec(v) for v in m) for m in mods]

    h0 = _norm_mod_fwd("l0_norm", xall, g0, scale0, shift0, n, tr, t)
    proj0 = _mm("l0_in", h0, w["ssm_w_in"], "nn")
    raw, dirs = _s5_system(p, t, c)
    (y_f, y_r), saved = _s5_forward(proj0, dirs, t, c)
    d_skip = _vec(p["ssm_d"][0])

    def post_a(u, yf, yr, dv):
        y = u * dv + yf + yr
        return y, _gelu(y)

    y0, yg = _rowwise("l0_gelu", post_a, n, tr, [(proj0, 0, D_MODEL), (y_f, 0, D_MODEL), (y_r, 0, D_MODEL)], [d_skip],
                      [(D_MODEL, BF16), (D_MODEL, BF16)], [])
    tg = _mm("l0_glu", yg, w["ssm_w_glu"], "nn", out_dtype=BF16)
    b_glu = _vec(p["ssm_b_glu"][0])

    def post_b(ygt, tt, zt, bv):
        return ygt * _sigmoid(tt + bv) * _silu(zt)

    gz0 = _rowwise("l0_gate", post_b, n, tr, [(yg, 0, D_MODEL), (tg, 0, D_MODEL), (proj0, 1, D_MODEL)], [b_glu],
                   [(D_MODEL, BF16)], [])[0]
    out0 = _mm("l0_out", gz0, w["ssm_w_out"], "nn")

    def res_norm(xt, ot, gv, g1v, sc, sh):
        x1t = xt + gv * ot
        xh, _ = _rms_hat(x1t)
        return x1t, (xh * g1v) * (1.0 + sc) + sh

    x1, h1 = _rowwise("l0_res_l1_norm", res_norm, n, tr, [_stream(xall), (out0, 0, D_MODEL)],
                      [gate0, g1, scale1, shift1], [(D_MODEL, F32), (D_MODEL, BF16)], [], n_lat=t)
    proj1 = _mm("l1_in", h1, w["attn_w_in"], "nn", out_dtype=BF16)
    cos, sin = _rope_tables(t, c)
    qn, kn = p["attn_q_norm"][0], p["attn_k_norm"][0]
    q_h, k_h, v_h = _qk_prep(proj1, qn, kn, cos, sin, n, tr)
    tq = _largest_tile(t, 512, LANES)
    o, lse, gz1 = _attn_fwd(q_h, k_h, v_h, proj1, t, tq, _largest_tile(n, 2816, LANES))
    out1 = _mm("l1_out", gz1, w["attn_w_out"], "nn")

    gf = _vec(p["final_norm_g"])

    def head(x1t, o1t, tgt, g1v, gfv):
        x2 = x1t + g1v * o1t
        xh, r = _rms_hat(x2)
        e = xh * gfv - tgt
        dyf = e * (1.0 / D_MODEL)
        dx2 = _rms_bwd(xh, r, dyf * gfv)
        return dx2, g1v * dx2, dyf * xh, dx2 * o1t, jnp.sum(e * e, axis=1, keepdims=True)

    gate1_lat = gate1[0:1]
    dx2, dout1, d_gf, d_gate1, sq = _rowwise(
        "head", head, t, tr, [(x1, 0, D_MODEL), (out1, 0, D_MODEL), (target, 0, D_MODEL)], [gate1_lat, gf],
        [(D_MODEL, F32), (D_MODEL, BF16)], [(1, D_MODEL), (1, D_MODEL), (1, 1)])

    d_w_attn_out = _mm("l1_out_dw", gz1, dout1, "tn", out_dtype=BF16)
    dgz1 = _mm("l1_out_dx", dout1, w["attn_w_out"], "nt", out_dtype=BF16)
    dq_s, dk, dv, dz1 = _attn_bwd(q_h, k_h, v_h, dgz1, proj1, o, lse, t, tq, _largest_tile(n, 1024, LANES))
    dproj1, d_qn, d_kn = _qk_prep_bwd(proj1, qn, kn, cos, sin, dq_s, dz1, dk, dv, n, t, tr)
    d_w_attn_in = _mm("l1_in_dw", h1, dproj1, "tn", out_dtype=BF16)
    dh1 = _mm("l1_in_dx", dproj1, w["attn_w_in"], "nt", out_dtype=BF16)
    dx1, dout0, d_g1, d_scale1, d_shift1, d_gate0 = _norm_mod_bwd("l1_norm_bwd", x1, g1, scale1, dh1, dx2, n, tr, t,
                                                                  prev=(out0, gate0))

    d_w_out = _mm("l0_out_dw", gz0, dout0, "tn", out_dtype=BF16)
    dgz0 = _mm("l0_out_dx", dout0, w["ssm_w_out"], "nt", out_dtype=BF16)

    def post_b_bwd(dgt, ygt, tt, zt, bv):
        s = _sigmoid(tt + bv)
        dy2 = dgt * _silu(zt)
        dt = dy2 * ygt * s * (1.0 - s)
        return dgt * (ygt * s) * _silu_grad(zt), dt, dy2 * s, dt

    dz0, dtg, dyg_a, d_b_glu = _rowwise(
        "l0_gate_bwd", post_b_bwd, n, tr, [(dgz0, 0, D_MODEL), (yg, 0, D_MODEL), (tg, 0, D_MODEL), (proj0, 1, D_MODEL)],
        [b_glu], [(D_MODEL, BF16), (D_MODEL, BF16), (D_MODEL, BF16)], [(1, D_MODEL)])
    d_w_glu = _mm("l0_glu_dw", yg, dtg, "tn", out_dtype=BF16)
    dyg_b = _mm("l0_glu_dx", dtg, w["ssm_w_glu"], "nt", out_dtype=BF16)

    def post_a_bwd(da, db, yt, ut, dv):
        dy = (da + db) * _gelu_grad(yt)
        return dy, dy * dv, dy * ut

    dy0, du_skip, d_d = _rowwise("l0_gelu_bwd", post_a_bwd, n, tr,
                                 [(dyg_a, 0, D_MODEL), (dyg_b, 0, D_MODEL), (y0, 0, D_MODEL), (proj0, 0, D_MODEL)], [d_skip],
                                 [(D_MODEL, F32), (D_MODEL, BF16)], [(1, D_MODEL)])
    s5_bwd = _s5_backward(proj0, dy0, dirs, saved, t, c)
    dproj0 = _rowwise("l0_in_grad", lambda a, b, cc, dz: jnp.concatenate([a + b + cc, dz], axis=1), n, tr,
                      [(du_skip, 0, D_MODEL), (s5_bwd[0][0], 0, D_MODEL), (s5_bwd[1][0], 0, D_MODEL), (dz0, 0, D_MODEL)], [],
                      [(2 * D_MODEL, BF16)], [])[0]
    d_w_in = _mm("l0_in_dw", h0, dproj0, "tn", out_dtype=BF16)
    dh0 = _mm("l0_in_dx", dproj0, w["ssm_w_in"], "nt", out_dtype=BF16)
    dx0, d_g0, d_scale0, d_shift0 = _norm_mod_bwd("l0_norm_bwd", xall, g0, scale0, dh0, dx1, n, tr, t, dx_lat_only=True)

    big = dict(ssm_w_in=d_w_in, ssm_w_glu=d_w_glu, ssm_w_out=d_w_out, attn_w_in=d_w_attn_in, attn_w_out=d_w_attn_out)
    small = dict(
        norm_g=jnp.concatenate([d_g0[0], d_g1[0]], axis=0), ssm_d=d_d[0], ssm_b_glu=d_b_glu[0],
        attn_q_norm=d_qn, attn_k_norm=d_kn, final_norm_g=d_gf[0, 0], **_s5_param_grads(raw, s5_bwd))
    zero_v = jnp.zeros((D_MODEL,), F32)
    d_mod_lat = jnp.stack([jnp.concatenate([d_shift0[0, 0], d_scale0[0, 0], d_gate0[0, 0]]),
                           jnp.concatenate([d_shift1[0, 0], d_scale1[0, 0], d_gate1[0, 0]])])
    d_mod_ctx = jnp.stack([jnp.concatenate([d_shift0[1, 0], d_scale0[1, 0], d_gate0[1, 0]]),
                           jnp.concatenate([d_shift1[1, 0], d_scale1[1, 0], zero_v])])
    return sq[0, 0, 0], dx0, big, small, d_mod_lat, d_mod_ctx


def _adamw(name, w, g, m, v):
    rows, cols = w.shape
    tr = _largest_tile(rows, 256, 8)
    c1 = 1.0 / (1.0 - ADAM_B1 ** ADAM_STEP)
    c2 = 1.0 / (1.0 - ADAM_B2 ** ADAM_STEP)

    def fn(wt, gt, mt, vt):
        mn = ADAM_B1 * mt + (1.0 - ADAM_B1) * gt
        vn = ADAM_B2 * vt + (1.0 - ADAM_B2) * (gt * gt)
        delta = -ADAM_LR * ((mn * c1) / (jnp.sqrt(vn * c2) + ADAM_EPS) + ADAM_WD * wt)
        return delta, mn, vn

    return _rowwise(name, fn, rows, tr, [(a, 0, cols) for a in (w, g, m, v)], [], [(cols, F32)] * 3, [])


BIG = ("ssm_w_in", "ssm_w_glu", "ssm_w_out", "attn_w_in", "attn_w_out")
COL_SHARDED = ("ssm_w_in", "attn_w_in")
WEIGHTS = ("c_ctx", "w_mod", "b_mod", "norm_g", "ssm_w_in", "ssm_a_re", "ssm_a_im", "ssm_log_dt", "ssm_b_re", "ssm_b_im",
           "ssm_c_re", "ssm_c_im", "ssm_d", "ssm_w_glu", "ssm_b_glu", "ssm_w_out", "attn_w_in", "attn_q_norm",
           "attn_k_norm", "attn_w_out", "final_norm_g")
SMALL = tuple(k for k in WEIGHTS if k not in BIG and k != "w_mod")
PACK_W = 1024
COND_ROWS = 2 * N_DEV


def _attn_in_perm(x, inverse):
    a, kv = ATTN_W, 2 * KV_W
    if inverse:
        return jnp.concatenate([x[..., :a], x[..., 2 * a:], x[..., a:2 * a]], axis=-1)
    return jnp.concatenate([x[..., :a], x[..., a + kv:], x[..., a:a + kv]], axis=-1)


def _pack(arrays, dtype, row_unit):
    flat = jnp.concatenate([a.reshape(-1).astype(dtype) for a in arrays])
    rows = -(-flat.shape[0] // PACK_W)
    rows = -(-rows // row_unit) * row_unit
    flat = jnp.concatenate([flat, jnp.zeros((rows * PACK_W - flat.shape[0],), dtype)])
    return flat.reshape(rows, PACK_W)


def _unpack(buf, shapes):
    lead = buf.shape[:-2]
    flat = buf.reshape(lead + (-1,))
    out, off = [], 0
    for shp in shapes:
        size = math.prod(shp)
        out.append(flat[..., off:off + size].reshape(lead + tuple(shp)))
        off += size
    return out


def kernel(x, c, ctx, c_ctx, w_mod, b_mod, norm_g, ssm_w_in, ssm_a_re, ssm_a_im, ssm_log_dt, ssm_b_re, ssm_b_im, ssm_c_re, ssm_c_im, ssm_d, ssm_w_glu, ssm_b_glu, ssm_w_out, attn_w_in, attn_q_norm, attn_k_norm, attn_w_out, final_norm_g, loss_target, m_c_ctx, m_w_mod, m_b_mod, m_norm_g, m_ssm_w_in, m_ssm_a_re, m_ssm_a_im, m_ssm_log_dt, m_ssm_b_re, m_ssm_b_im, m_ssm_c_re, m_ssm_c_im, m_ssm_d, m_ssm_w_glu, m_ssm_b_glu, m_ssm_w_out, m_attn_w_in, m_attn_q_norm, m_attn_k_norm, m_attn_w_out, m_final_norm_g, v_c_ctx, v_w_mod, v_b_mod, v_norm_g, v_ssm_w_in, v_ssm_a_re, v_ssm_a_im, v_ssm_log_dt, v_ssm_b_re, v_ssm_b_im, v_ssm_c_re, v_ssm_c_im, v_ssm_d, v_ssm_w_glu, v_ssm_b_glu, v_ssm_w_out, v_attn_w_in, v_attn_q_norm, v_attn_k_norm, v_attn_w_out, v_final_norm_g):
    args = dict(locals())
    wts = {k: args[k] for k in WEIGHTS}
    mom_m = {k: args["m_" + k] for k in WEIGHTS}
    mom_v = {k: args["v_" + k] for k in WEIGHTS}
    mx, my, mc = lax.axis_index("x"), lax.axis_index("y"), lax.axis_index("c")
    chip = 2 * mx + my
    me = 2 * chip + mc

    halves = []
    for k in BIG:
        sh = wts[k][0]
        hr = sh.shape[0] // 2
        halves.append(lax.dynamic_slice_in_dim(sh, mc * hr, hr, axis=0))
    gathered = _gather_two_level("gather_weights", _pack(halves, BF16, 16))
    parts = _unpack(gathered, [h.shape for h in halves])
    w_full = {}
    for k, pc in zip(BIG, parts):
        hr, cols = pc.shape[1:]
        pc = pc.reshape(N_CHIP, 2, hr, cols)
        if k in COL_SHARDED:
            w_full[k] = pc.transpose(1, 2, 0, 3).reshape(2 * hr, N_CHIP * cols)
        else:
            w_full[k] = pc.reshape(N_CHIP * 2 * hr, cols)
    w_full["attn_w_in"] = _attn_in_perm(w_full["attn_w_in"], False)

    c_blk = jnp.concatenate([c, jnp.zeros((N_DEV - 1, D_MODEL), F32)], axis=0)
    c_all = _exchange("gather_c", c_blk, True)[:, 0]
    cond = jnp.concatenate([c_all, c_ctx[None], jnp.zeros((COND_ROWS - N_DEV - 1, D_MODEL), F32)], axis=0)
    s_cond, ds_cond = _rowwise("cond_silu", lambda t: (_silu(t), _silu_grad(t)), COND_ROWS, COND_ROWS, [(cond, 0, D_MODEL)], [],
                               [(D_MODEL, F32), (D_MODEL, F32)], [])
    w_mod_b = w_mod.astype(BF16)
    mcols = w_mod.shape[2]
    mod_part = jnp.stack([_mm(f"mod{i}", s_cond, w_mod_b[i], "nn") for i in range(2)])
    mod_g = _exchange("gather_mod", mod_part.reshape(2 * COND_ROWS, mcols), True)
    mod_all = mod_g.reshape(N_CHIP, 2, 2, COND_ROWS, mcols)[:, 0]
    mod_all = mod_all.transpose(1, 2, 0, 3).reshape(2, COND_ROWS, N_CHIP * mcols) + b_mod[:, None, :]
    mods = []
    for i in range(2):
        lat = lax.dynamic_slice_in_dim(mod_all[i], me, 1, axis=0)[0]
        both = jnp.stack([lat, mod_all[i, N_DEV]])
        mods.append((both[:, :D_MODEL], both[:, D_MODEL:2 * D_MODEL], both[:, 2 * D_MODEL:]))

    small_p = {k: wts[k] for k in SMALL if k != "c_ctx" and k != "b_mod"}
    sq, grad_x, big_g, small_g, d_mod_lat, d_mod_ctx = _example_step(x[0], ctx[0], loss_target[0], mods, w_full, small_p)
    loss = lax.psum(0.5 / D_MODEL * sq, ("x", "y", "c"))
    big_g["attn_w_in"] = _attn_in_perm(big_g["attn_w_in"], True)

    small_names = [k for k in SMALL if k not in ("c_ctx", "b_mod")]
    small_list = [small_g[k] for k in small_names] + [d_mod_lat, d_mod_ctx]
    small_shapes = [wts[k].shape for k in small_names] + [d_mod_lat.shape, d_mod_ctx.shape]
    packed = _pack(small_list, F32, 8 * N_DEV)
    slice_rows = packed.shape[0] // N_DEV
    slices = _exchange("scatter_small", packed.reshape(N_DEV, slice_rows, PACK_W), False)
    my_sum = _sum_slots("sum_small", slices)
    payload = jnp.concatenate([my_sum, _pack([d_mod_lat], F32, 8)], axis=0)
    sg = _exchange("gather_small", payload, True)
    summed = _unpack(sg[:, :slice_rows].reshape(packed.shape), small_shapes)
    grads = dict(zip(small_names, summed[:-2]))
    d_mod_lat_sum, d_mod_ctx_sum = summed[-2], summed[-1]
    grads["b_mod"] = d_mod_lat_sum + d_mod_ctx_sum
    d_mod_lat_all = _unpack(sg[:, slice_rows:], [d_mod_lat.shape])[0]

    g_w_mod, ds_cc = [], []
    for i in range(2):
        rows9 = jnp.concatenate([d_mod_lat_all[:, i], d_mod_ctx_sum[i][None],
                                 jnp.zeros((COND_ROWS - N_DEV - 1, 3 * D_MODEL), F32)], axis=0)
        mine = lax.dynamic_slice_in_dim(rows9, chip * mcols, mcols, axis=1)
        g_w_mod.append(_mm(f"mod{i}_dw", s_cond, mine, "tn"))
        ds_cc.append(_mm(f"mod{i}_dx", mine, w_mod_b[i], "nt")[N_DEV])
    grads["w_mod"] = jnp.stack(g_w_mod)
    part = (ds_cc[0] + ds_cc[1]) * jnp.where(mc == 0, 1.0, 0.0)
    part_blk = jnp.concatenate([part[None], jnp.zeros((N_DEV - 1, D_MODEL), F32)], axis=0)
    ds_all = _sum_slots("sum_c_ctx", _exchange("gather_c_ctx", part_blk, True))
    grads["c_ctx"] = ds_all[0] * ds_cond[N_DEV]

    blocks = []
    for k in BIG:
        g = big_g[k]
        rows, cols = g.shape
        if k in COL_SHARDED:
            blocks.append(g.reshape(2, rows // 2, N_CHIP, cols // N_CHIP).transpose(2, 0, 1, 3).reshape(N_DEV, -1))
        else:
            blocks.append(g.reshape(N_DEV, -1))
    sendbuf = jnp.concatenate(blocks, axis=1).astype(BF16)
    sendbuf = sendbuf.reshape(N_DEV, -1, PACK_W)
    recv = _exchange("scatter_big", sendbuf, False)
    mine = _sum_slots("sum_big", recv)
    both = _exchange("swap_halves", mine, True, sibling_only=True)
    half_shapes = [(wts[k].shape[1] // 2, wts[k].shape[2]) for k in BIG]
    for k, pc in zip(BIG, _unpack(both, half_shapes)):
        grads[k] = pc.reshape(wts[k].shape)

    delta, new_m, new_v = {}, {}, {}
    for k in BIG + ("w_mod",):
        shp = wts[k].shape
        two_d = (-1, shp[-1])
        res = _adamw("adamw_" + k, *[a.reshape(two_d) for a in (wts[k], grads[k], mom_m[k], mom_v[k])])
        delta[k], new_m[k], new_v[k] = [r.reshape(shp) for r in res]
    shapes = [wts[k].shape for k in SMALL]
    packed = [_pack([d[k] for k in SMALL], F32, 8) for d in (wts, grads, mom_m, mom_v)]
    res = _adamw("adamw_small", *packed)
    for dst, buf in zip((delta, new_m, new_v), res):
        for k, a in zip(SMALL, _unpack(buf, shapes)):
            dst[k] = a
    grads = {k: grads[k].reshape(wts[k].shape) for k in WEIGHTS}
    return (loss, grad_x[None], *[grads[k] for k in WEIGHTS], *[delta[k] for k in WEIGHTS],
            *[new_m[k] for k in WEIGHTS], *[new_v[k] for k in WEIGHTS])
```

```python
import functools
import math

import jax
import jax.numpy as jnp
from jax import lax
from jax.experimental import pallas as pl
from jax.experimental.pallas import tpu as pltpu

F32 = jnp.float32
BF16 = jnp.bfloat16

D_MODEL = 1024
NORM_EPS = 1e-6
SSM_GROUPS = 64
SSM_GROUP = 16
SSM_STATE = 64
LANES = 128
SLAB_W = LANES
N_SLAB = D_MODEL // SLAB_W
SLAB_GROUPS = SLAB_W // SSM_GROUP
HALF_W = SLAB_GROUPS * SSM_STATE
STATE_W = 2 * HALF_W
N_SEG = 8
HEAD_DIM = 64
N_Q_HEADS = 16
N_KV_HEADS = 4
KV_REP = N_Q_HEADS // N_KV_HEADS
ATTN_W = N_Q_HEADS * HEAD_DIM
KV_W = N_KV_HEADS * HEAD_DIM
GRID_W = 64
ROPE_THETA = 10000.0
N_DEV = 8
N_CHIP = 4
VMEM_LIMIT_BYTES = 56 * 1024 * 1024

ADAM_LR = 0.001
ADAM_B1 = 0.9
ADAM_B2 = 0.999
ADAM_EPS = 1e-08
ADAM_WD = 0.01
ADAM_STEP = 10


def _params(*sem):
    return pltpu.CompilerParams(dimension_semantics=sem, vmem_limit_bytes=VMEM_LIMIT_BYTES)


def _largest_tile(n, cap, unit):
    if n <= cap:
        return n
    t = (cap // unit) * unit
    while t >= unit:
        if n % t == 0:
            return t
        t -= unit
    raise ValueError(f"no tile for {n} (cap {cap}, unit {unit})")


def _rowwise(name, fn, n_rows, tr, row_ins, vec_ins, row_outs, red_outs, n_lat=None, want_flag=False):
    nt = n_rows // tr
    assert nt * tr == n_rows
    nlt = nt if n_lat is None else n_lat // tr

    def sel(i):
        return jnp.where(i >= nlt, 1, 0)

    arrays, in_specs, pairs = [], [], []
    for spec in row_ins:
        arr, cb, w = spec[:3]
        kind = spec[3] if len(spec) > 3 else None
        m = spec[4] if len(spec) > 4 else None
        if kind == "pair":
            arrays += [arr, m]
            in_specs += [pl.BlockSpec((tr, w), functools.partial(lambda i, cb: (jnp.minimum(i, nlt - 1), cb), cb=cb)),
                         pl.BlockSpec((tr, w), functools.partial(lambda i, cb: (jnp.maximum(i - nlt, 0), cb), cb=cb))]
            pairs.append(len(arrays) - 2)
            continue
        if kind == "mod":
            imap = functools.partial(lambda i, cb, m: (i % m, cb), cb=cb, m=m)
        elif kind == "clamp":
            imap = functools.partial(lambda i, cb, m: (jnp.minimum(i, m - 1), cb), cb=cb, m=m)
        else:
            imap = functools.partial(lambda i, cb: (i, cb), cb=cb)
        arrays.append(arr)
        in_specs.append(pl.BlockSpec((tr, w), imap))
    for v in vec_ins:
        s, a, w = v.shape
        imap = (lambda i: (sel(i), 0, 0)) if s == 2 else (lambda i: (0, 0, 0))
        arrays.append(v)
        in_specs.append(pl.BlockSpec((1, a, w), imap))
    out_shapes, out_specs = [], []
    lat_only = [len(spec) > 2 for spec in row_outs]
    for (w, dt), lat in zip([spec[:2] for spec in row_outs], lat_only):
        out_shapes.append(jax.ShapeDtypeStruct((n_lat if lat else n_rows, w), dt))
        out_specs.append(pl.BlockSpec((tr, w), (lambda i: (jnp.minimum(i, nlt - 1), 0)) if lat else (lambda i: (i, 0))))
    for s, w in red_outs:
        out_shapes.append(jax.ShapeDtypeStruct((s, 1, w), F32))
        imap = (lambda i: (sel(i), 0, 0)) if s == 2 else (lambda i: (0, 0, 0))
        out_specs.append(pl.BlockSpec((1, 1, w), imap))
    n_ri, n_vi, n_ro, n_rd = len(row_ins) + len(pairs), len(vec_ins), len(row_outs), len(red_outs)

    def body(*refs):
        i = pl.program_id(0)
        rows, k = [], 0
        while k < n_ri:
            if k in pairs:
                rows.append(jnp.where(i < nlt, refs[k][...], refs[k + 1][...]).astype(F32))
                k += 2
            else:
                rows.append(refs[k][...].astype(F32))
                k += 1
        vecs = [r[0] for r in refs[n_ri:n_ri + n_vi]]
        outs = refs[n_ri + n_vi:]
        lead = [jnp.where(i < nlt, 1.0, 0.0).astype(F32)] if want_flag else []
        res = fn(*lead, *rows, *vecs)
        if not isinstance(res, (tuple, list)):
            res = (res,)
        assert len(res) == n_ro + n_rd
        for k in range(n_ro):
            if lat_only[k]:
                @pl.when(i < nlt)
                def _(k=k):
                    outs[k][...] = res[k].astype(outs[k].dtype)
            else:
                outs[k][...] = res[k].astype(outs[k].dtype)
        for k in range(n_rd):
            part = jnp.sum(res[n_ro + k].astype(F32), axis=0, keepdims=True)
            first = i == 0
            if red_outs[k][0] == 2:
                first = jnp.logical_or(first, i == nlt)
            o = outs[n_ro + k]

            @pl.when(first)
            def _():
                o[0] = part

            @pl.when(jnp.logical_not(first))
            def _():
                o[0] = o[0] + part

    res = pl.pallas_call(
        body, grid=(nt,), in_specs=in_specs, out_specs=out_specs, out_shape=out_shapes,
        compiler_params=_params("arbitrary"), name=name)(*arrays)
    return res


def _vec(v):
    v = v.astype(F32)
    if v.ndim == 1:
        v = v[None]
    return v[:, None, :]


def _mm(name, a, b, mode, out_dtype=F32):
    if mode in ("nn", "nt"):
        m, k = a.shape
        n = b.shape[1] if mode == "nn" else b.shape[0]
        tm = _largest_tile(m, 1024, 8)
        tn = _largest_tile(n, 1024, 128)
        contract = (((1,), (0,)), ((), ())) if mode == "nn" else (((1,), (1,)), ((), ()))

        def body(a_ref, b_ref, o_ref):
            o_ref[...] = lax.dot_general(a_ref[...].astype(BF16), b_ref[...].astype(BF16), contract,
                                         preferred_element_type=F32).astype(o_ref.dtype)

        b_spec = pl.BlockSpec((k, tn), lambda i, j: (0, j)) if mode == "nn" else pl.BlockSpec((tn, k), lambda i, j: (j, 0))
        return pl.pallas_call(
            body, grid=(m // tm, n // tn),
            in_specs=[pl.BlockSpec((tm, k), lambda i, j: (i, 0)), b_spec],
            out_specs=pl.BlockSpec((tm, tn), lambda i, j: (i, j)),
            out_shape=jax.ShapeDtypeStruct((m, n), out_dtype),
            compiler_params=_params("parallel", "arbitrary"), name=name)(a, b)
    assert mode == "tn"
    r, k1 = a.shape
    k2 = b.shape[1]
    tr = _largest_tile(r, 1024, 8)
    t2 = _largest_tile(k2, 1024, 128)
    nr = r // tr

    def body(a_ref, b_ref, o_ref, acc_ref):
        part = lax.dot_general(a_ref[...].astype(BF16), b_ref[...].astype(BF16), (((0,), (0,)), ((), ())),
                               preferred_element_type=F32)
        i = pl.program_id(1)

        @pl.when(i == 0)
        def _():
            acc_ref[...] = part

        @pl.when(i > 0)
        def _():
            acc_ref[...] += part

        @pl.when(i == nr - 1)
        def _():
            o_ref[...] = acc_ref[...].astype(o_ref.dtype)

    return pl.pallas_call(
        body, grid=(k2 // t2, nr),
        in_specs=[pl.BlockSpec((tr, k1), lambda j, i: (i, 0)), pl.BlockSpec((tr, t2), lambda j, i: (i, j))],
        out_specs=pl.BlockSpec((k1, t2), lambda j, i: (0, j)),
        out_shape=jax.ShapeDtypeStruct((k1, k2), out_dtype),
        scratch_shapes=[pltpu.VMEM((k1, t2), F32)],
        compiler_params=_params("parallel", "arbitrary"), name=name)(a, b)


def _exchange(name, x, bcast, sibling_only=False):
    rels = [1] if sibling_only else list(range(1, N_DEV))
    n_slot = 2 if sibling_only else N_DEV
    blk = x.shape if bcast else x.shape[1:]

    def body(x_ref, o_ref, send_sems, recv_sems, local_sem):
        mx, my, mc = lax.axis_index("x"), lax.axis_index("y"), lax.axis_index("c")
        me = mc if sibling_only else 4 * mx + 2 * my + mc
        me_dev = 4 * mx + 2 * my + mc
        mine = pltpu.make_async_copy(x_ref if bcast else x_ref.at[me_dev], o_ref.at[me], local_sem)
        mine.start()
        copies = []
        for k, r in enumerate(rels):
            px = 1 - mx if (r >> 2) & 1 else mx
            py = 1 - my if (r >> 1) & 1 else my
            pc = 1 - mc if r & 1 else mc
            src = x_ref if bcast else x_ref.at[4 * px + 2 * py + pc]
            cp = pltpu.make_async_remote_copy(
                src_ref=src, dst_ref=o_ref.at[me], send_sem=send_sems.at[k], recv_sem=recv_sems.at[k],
                device_id=(px, py, pc), device_id_type=pl.DeviceIdType.MESH)
            cp.start()
            copies.append(cp)
        for cp in copies:
            cp.wait()
        mine.wait()

    return pl.pallas_call(
        body, out_shape=jax.ShapeDtypeStruct((n_slot,) + tuple(blk), x.dtype),
        in_specs=[pl.BlockSpec(memory_space=pltpu.VMEM if sibling_only else pl.ANY)],
        out_specs=pl.BlockSpec(memory_space=pl.ANY),
        scratch_shapes=[pltpu.SemaphoreType.DMA((len(rels),)), pltpu.SemaphoreType.DMA((len(rels),)),
                        pltpu.SemaphoreType.DMA],
        name=name)(x)


def _gather_two_level(name, x):
    def body(x_ref, o_ref, send_sems, recv_sems, local_sem):
        mx, my, mc = lax.axis_index("x"), lax.axis_index("y"), lax.axis_index("c")
        me, sibling = (mx, my, mc), (mx, my, 1 - mc)
        chips = [(1 - mx, my), (mx, 1 - my), (1 - mx, 1 - my)]

        def slot(px, py, pc):
            return o_ref.at[4 * px + 2 * py + pc]

        def copy(k, block, to, src=None):
            return pltpu.make_async_remote_copy(
                src_ref=slot(*block) if src is None else src, dst_ref=slot(*block), send_sem=send_sems.at[k],
                recv_sem=recv_sems.at[k], device_id=to, device_id_type=pl.DeviceIdType.MESH)

        mine = pltpu.make_async_copy(x_ref, slot(*me), local_sem)
        mine.start()
        first = [copy(0, me, sibling, src=x_ref)]
        first += [copy(1 + j, me, (*chip, mc), src=x_ref) for j, chip in enumerate(chips)]
        for cp in first:
            cp.start()
        passed = [copy(4 + j, (*chip, mc), sibling) for j, chip in enumerate(chips)]
        for j, chip in enumerate(chips):
            copy(1 + j, (*chip, mc), me).wait_recv()
            passed[j].start()
        copy(0, sibling, me).wait_recv()
        for j, chip in enumerate(chips):
            copy(4 + j, (*chip, 1 - mc), me).wait_recv()
        for cp in first + passed:
            cp.wait_send()
        mine.wait()

    return pl.pallas_call(
        body, out_shape=jax.ShapeDtypeStruct((N_DEV,) + tuple(x.shape), x.dtype),
        in_specs=[pl.BlockSpec(memory_space=pl.ANY)], out_specs=pl.BlockSpec(memory_space=pl.ANY),
        scratch_shapes=[pltpu.SemaphoreType.DMA((N_DEV - 1,)), pltpu.SemaphoreType.DMA((N_DEV - 1,)),
                        pltpu.SemaphoreType.DMA],
        name=name)(x)


def _sum_slots(name, x):
    s, r, w = x.shape
    tr = _largest_tile(r, 256, 8)

    def body(x_ref, o_ref):
        acc = x_ref[0].astype(F32)
        for j in range(1, s):
            acc = acc + x_ref[j].astype(F32)
        o_ref[...] = acc

    return pl.pallas_call(
        body, grid=(r // tr,), in_specs=[pl.BlockSpec((s, tr, w), lambda i: (0, i, 0))],
        out_specs=pl.BlockSpec((tr, w), lambda i: (i, 0)), out_shape=jax.ShapeDtypeStruct((r, w), F32),
        compiler_params=_params("parallel"), name=name)(x)


def _sigmoid(x):
    return 1.0 / (1.0 + jnp.exp(-x))


def _silu(x):
    return x * _sigmoid(x)


def _silu_grad(x):
    s = _sigmoid(x)
    return s * (1.0 + x * (1.0 - s))


_INV_SQRT2 = 1.0 / math.sqrt(2.0)
_INV_SQRT2PI = 1.0 / math.sqrt(2.0 * math.pi)


def _gelu(x):
    return 0.5 * x * (1.0 + lax.erf(x * _INV_SQRT2))


def _gelu_grad(x):
    return 0.5 * (1.0 + lax.erf(x * _INV_SQRT2)) + x * jnp.exp(-0.5 * x * x) * _INV_SQRT2PI


def _rms_hat(x):
    r = lax.rsqrt(jnp.mean(x * x, axis=-1, keepdims=True) + NORM_EPS)
    return x * r, r


def _rms_bwd(xh, r, dxh):
    return r * (dxh - xh * jnp.mean(dxh * xh, axis=-1, keepdims=True))


def _stream(x):
    return (x[0], 0, D_MODEL, "pair", x[1]) if isinstance(x, tuple) else (x, 0, D_MODEL)


def _norm_mod_fwd(name, x, g, scale, shift, n_rows, tr, n_lat):
    def fn(xt, gv, sc, sh):
        xh, _ = _rms_hat(xt)
        return (xh * gv) * (1.0 + sc) + sh

    return _rowwise(name, fn, n_rows, tr, [_stream(x)], [g, scale, shift], [(D_MODEL, BF16)], [], n_lat=n_lat)[0]


def _norm_mod_bwd(name, x, g, scale, dh, dres, n_rows, tr, n_lat, prev=None, dx_lat_only=False):
    nlt = n_lat // tr

    def fn(flag, xt, dht, drt, *rest):
        gv, sc = rest[-2:] if prev is None else rest[1:3]
        xh, r = _rms_hat(xt)
        n = xh * gv
        dn = dht * (1.0 + sc)
        dx = _rms_bwd(xh, r, dn * gv) + flag * drt
        if prev is None:
            return dx, dn * xh, dht * n, dht
        return dx, rest[3] * dx, dn * xh, dht * n, dht, dx * rest[0]

    rows = [_stream(x), (dh, 0, D_MODEL), (dres, 0, D_MODEL, "clamp", nlt)]
    row_outs = [(D_MODEL, F32, "lat") if dx_lat_only else (D_MODEL, F32)]
    vecs, reds = [g, scale], [(1, D_MODEL), (2, D_MODEL), (2, D_MODEL)]
    if prev is not None:
        rows.append((prev[0], 0, D_MODEL))
        vecs.append(prev[1])
        row_outs.append((D_MODEL, BF16))
        reds.append((2, D_MODEL))
    return _rowwise(name, fn, n_rows, tr, rows, vecs, row_outs, reds, n_lat=n_lat, want_flag=True)


def _s5_prep(a_re, a_im, log_dt, b_re, b_im, seg_lat, seg_ctx):
    def body(ar_ref, ai_ref, ld_ref, br_ref, bi_ref, abr_ref, abi_ref, bbr_ref, bbi_ref, alr_ref, ali_ref, acr_ref,
             aci_ref):
        lr, li = ar_ref[...], ai_ref[...]
        dt = jnp.exp(ld_ref[...])
        ldr, ldi = lr * dt, li * dt
        e = jnp.exp(ldr)
        abr, abi = e * jnp.cos(ldi), e * jnp.sin(ldi)
        abr_ref[...] = abr
        abi_ref[...] = abi
        den = lr * lr + li * li
        nr, ni = abr - 1.0, abi
        qr = (nr * lr + ni * li) / den
        qi = (ni * lr - nr * li) / den
        br, bi = br_ref[...], bi_ref[...]
        bbr_ref[...] = qr[None] * br - qi[None] * bi
        bbi_ref[...] = qr[None] * bi + qi[None] * br
        for seg, r_ref, i_ref in ((seg_lat, alr_ref, ali_ref), (seg_ctx, acr_ref, aci_ref)):
            es = jnp.exp(ldr * float(seg))
            r_ref[...] = es * jnp.cos(ldi * float(seg))
            i_ref[...] = es * jnp.sin(ldi * float(seg))

    sm = jax.ShapeDtypeStruct(a_re.shape, F32)
    big = jax.ShapeDtypeStruct(b_re.shape, F32)
    return pl.pallas_call(body, out_shape=[sm, sm, big, big, sm, sm, sm, sm], name="s5_prep")(
        a_re, a_im, log_dt, b_re, b_im)


def _s5_prep_bwd(a_re, a_im, log_dt, b_re, b_im, dabr, dabi, dbbr, dbbi):
    def body(ar_ref, ai_ref, ld_ref, br_ref, bi_ref, dabr_ref, dabi_ref, dbbr_ref, dbbi_ref,
             dar_ref, dai_ref, dld_ref, dbr_ref, dbi_ref):
        lr, li = ar_ref[...], ai_ref[...]
        dt = jnp.exp(ld_ref[...])
        ldr, ldi = lr * dt, li * dt
        e = jnp.exp(ldr)
        abr, abi = e * jnp.cos(ldi), e * jnp.sin(ldi)
        den = lr * lr + li * li
        nr, ni = abr - 1.0, abi
        qr = (nr * lr + ni * li) / den
        qi = (ni * lr - nr * li) / den
        br, bi = br_ref[...], bi_ref[...]
        gbr, gbi = dbbr_ref[...], dbbi_ref[...]
        dbr_ref[...] = gbr * qr[None] + gbi * qi[None]
        dbi_ref[...] = gbi * qr[None] - gbr * qi[None]
        dqr = jnp.sum(gbr * br + gbi * bi, axis=0)
        dqi = jnp.sum(gbi * br - gbr * bi, axis=0)
        dnr = (dqr * lr - dqi * li) / den
        dni = (dqr * li + dqi * lr) / den
        dlr_q = (dqr * (nr - 2.0 * lr * qr) + dqi * (ni - 2.0 * lr * qi)) / den
        dli_q = (dqr * (ni - 2.0 * li * qr) + dqi * (-nr - 2.0 * li * qi)) / den
        gar = dabr_ref[...] + dnr
        gai = dabi_ref[...] + dni
        dldr = gar * abr + gai * abi
        dldi = gai * abr - gar * abi
        dar_ref[...] = dldr * dt + dlr_q
        dai_ref[...] = dldi * dt + dli_q
        ddt = jnp.sum(dldr * lr + dldi * li, axis=1, keepdims=True)
        dld_ref[...] = ddt * dt

    sm = jax.ShapeDtypeStruct(a_re.shape, F32)
    big = jax.ShapeDtypeStruct(b_re.shape, F32)
    return pl.pallas_call(body, out_shape=[sm, sm, jax.ShapeDtypeStruct(log_dt.shape, F32), big, big],
                          name="s5_prep_bwd")(a_re, a_im, log_dt, b_re, b_im, dabr, dabi, dbbr, dbbi)


def _slab_cols(v):
    return v.reshape(N_SLAB, 1, HALF_W)


def _slab_pair(vr, vi):
    return jnp.concatenate([_slab_cols(vr), _slab_cols(vi)], axis=-1)


def _slab_in_matrix(bbr, bbi):
    eye = jnp.eye(SLAB_GROUPS, dtype=F32)

    def one(b):
        b = b.reshape(N_SLAB, SLAB_GROUPS, SSM_STATE, SSM_GROUP)
        m = jnp.einsum("sgph,gk->sghkp", b, eye)
        return m.reshape(N_SLAB, SLAB_W, HALF_W)

    return jnp.concatenate([one(bbr), one(bbi)], axis=-1)


def _slab_out_matrix(cr, ci):
    eye = jnp.eye(SLAB_GROUPS, dtype=F32)

    def one(c):
        c = c.reshape(N_SLAB, SLAB_GROUPS, SSM_GROUP, SSM_STATE)
        m = jnp.einsum("sghp,gk->skpgh", c, eye)
        return m.reshape(N_SLAB, HALF_W, SLAB_W)

    return jnp.concatenate([one(cr), one(-ci)], axis=1)


def _slab_diag(m):
    m = m.reshape(N_SLAB, SLAB_GROUPS, SSM_GROUP, 2, SLAB_GROUPS, SSM_STATE)
    d = jnp.stack([m[:, g, :, :, g, :] for g in range(SLAB_GROUPS)], axis=1)
    return d.transpose(3, 0, 1, 2, 4).reshape(2, SSM_GROUPS, SSM_GROUP, SSM_STATE)


def _cmul(ar, ai, xr, xi, conj):
    if conj:
        return ar * xr + ai * xi, ar * xi - ai * xr
    return ar * xr - ai * xi, ar * xi + ai * xr


def _s5_pow_table(name, abar, seg, falling, conj):
    assert seg >= 8 and seg & (seg - 1) == 0

    def body(a_ref, o_ref, t_ref):
        ar, ai = a_ref[0, :, :HALF_W], a_ref[0, :, HALF_W:]
        if conj:
            ai = -ai
        rr, ri = [jnp.ones_like(ar)], [jnp.zeros_like(ai)]
        for _ in range(7):
            pr, pi = _cmul(ar, ai, rr[-1], ri[-1], False)
            rr.append(pr)
            ri.append(pi)
        sr, si = _cmul(ar, ai, rr[-1], ri[-1], False)
        if falling:
            rr, ri = rr[::-1], ri[::-1]
        first = slice(seg - 8, seg) if falling else slice(0, 8)
        t_ref[first, :HALF_W] = jnp.concatenate(rr, axis=0)
        t_ref[first, HALF_W:] = jnp.concatenate(ri, axis=0)
        size = 8
        while size < seg:
            src = slice(seg - size, seg) if falling else slice(0, size)
            dst = slice(seg - 2 * size, seg - size) if falling else slice(size, 2 * size)
            pr, pi = _cmul(sr, si, t_ref[src, :HALF_W], t_ref[src, HALF_W:], False)
            t_ref[dst, :HALF_W] = pr
            t_ref[dst, HALF_W:] = pi
            sr, si = _cmul(sr, si, sr, si, False)
            size *= 2
        o_ref[0] = t_ref[...].astype(BF16)

    return pl.pallas_call(
        body, grid=(N_SLAB,), in_specs=[pl.BlockSpec((1, 1, STATE_W), lambda s: (s, 0, 0))],
        out_specs=pl.BlockSpec((1, seg, STATE_W), lambda s: (s, 0, 0)),
        out_shape=jax.ShapeDtypeStruct((N_SLAB, seg, STATE_W), BF16),
        scratch_shapes=[pltpu.VMEM((seg, STATE_W), F32)], compiler_params=_params("parallel"), name=name)(abar)


def _s5_ends(name, x, n_rows, row0, table, m_mat):
    seg = n_rows // N_SEG
    rb = row0 // n_rows
    tn = (((0,), (0,)), ((), ()))

    def body(x_ref, t_ref, m_ref, z_ref):
        mr, mi = m_ref[0, :, :HALF_W], m_ref[0, :, HALF_W:]
        for j in range(N_SEG):
            t = lax.dot_general(x_ref[j * seg:(j + 1) * seg, :].astype(BF16), t_ref[0], tn,
                                preferred_element_type=F32)
            tr_, ti_ = t[:, :HALF_W], t[:, HALF_W:]
            z_ref[0, j:j + 1, :HALF_W] = jnp.sum(mr * tr_ - mi * ti_, axis=0, keepdims=True)
            z_ref[0, j:j + 1, HALF_W:] = jnp.sum(mr * ti_ + mi * tr_, axis=0, keepdims=True)

    return pl.pallas_call(
        body, grid=(N_SLAB,),
        in_specs=[pl.BlockSpec((n_rows, SLAB_W), lambda s: (rb, s)),
                  pl.BlockSpec((1, seg, STATE_W), lambda s: (s, 0, 0)),
                  pl.BlockSpec((1, SLAB_W, STATE_W), lambda s: (s, 0, 0))],
        out_specs=pl.BlockSpec((1, N_SEG, STATE_W), lambda s: (s, 0, 0)),
        out_shape=jax.ShapeDtypeStruct((N_SLAB, N_SEG, STATE_W), F32),
        compiler_params=_params("parallel"), name=name)(x, table, m_mat)


def _s5_carry(name, z, a_seg, init, descending, conj):
    order = list(range(N_SEG - 1, -1, -1)) if descending else list(range(N_SEG))

    def body(z_ref, a_ref, i_ref, e_ref, o_ref):
        ar, ai = a_ref[:, :HALF_W], a_ref[:, HALF_W:]
        cr, ci = i_ref[:, :HALF_W], i_ref[:, HALF_W:]
        for j in order:
            e_ref[:, j, :HALF_W] = cr
            e_ref[:, j, HALF_W:] = ci
            pr, pi = _cmul(ar, ai, cr, ci, conj)
            cr = pr + z_ref[:, j, :HALF_W]
            ci = pi + z_ref[:, j, HALF_W:]
        o_ref[:, :HALF_W] = cr
        o_ref[:, HALF_W:] = ci

    return pl.pallas_call(body, out_shape=[jax.ShapeDtypeStruct(z.shape, F32), jax.ShapeDtypeStruct(init.shape, F32)],
                          name=name)(z, a_seg, init)


def _s5_scan(name, u, n_rows, row0, b_mat, c_mat, abar, h_in, descending, y_alias=None, y_rows=None):
    seg = n_rows // N_SEG
    ta = min(32, seg)
    nk = seg // ta
    assert seg * N_SEG == n_rows and nk * ta == seg and row0 % n_rows == 0 and ta % 8 == 0
    rb = row0 // n_rows
    tile = ta * N_SEG

    def body(*refs):
        u_ref, b_ref, c_ref, a_ref, hin_ref = refs[:5]
        y_ref, hch_ref, st_ref, up_ref, h_ref = refs[-5:]
        k = pl.program_id(1)
        kk = nk - 1 - k if descending else k
        a0 = kk * ta

        @pl.when(k == 0)
        def _():
            st_ref[...] = hin_ref[0]

        hch_ref[0, 0] = st_ref[...]
        for al in range(ta):
            up_ref[al * N_SEG:(al + 1) * N_SEG, :] = u_ref[pl.ds(a0 + al, N_SEG, stride=seg), :]
        h_ref[...] = jnp.dot(up_ref[...].astype(BF16), b_ref[0], preferred_element_type=F32)
        ar = jnp.broadcast_to(a_ref[0, :, :HALF_W], (N_SEG, HALF_W))
        ai = jnp.broadcast_to(a_ref[0, :, HALF_W:], (N_SEG, HALF_W))

        def step(i, carry):
            hr, hi = carry
            al = ta - 1 - i if descending else i
            row = pl.multiple_of(al * N_SEG, N_SEG)
            pr, pi = _cmul(ar, ai, hr, hi, False)
            hr = pr + h_ref[pl.ds(row, N_SEG), :HALF_W]
            hi = pi + h_ref[pl.ds(row, N_SEG), HALF_W:]
            h_ref[pl.ds(row, N_SEG), :HALF_W] = hr
            h_ref[pl.ds(row, N_SEG), HALF_W:] = hi
            return hr, hi

        hr, hi = lax.fori_loop(0, ta, step, (st_ref[:, :HALF_W], st_ref[:, HALF_W:]), unroll=True)
        st_ref[:, :HALF_W] = hr
        st_ref[:, HALF_W:] = hi
        yt = jnp.dot(h_ref[...].astype(BF16), c_ref[0], preferred_element_type=F32)
        for al in range(ta):
            y_ref[pl.ds(a0 + al, N_SEG, stride=seg), :] = yt[al * N_SEG:(al + 1) * N_SEG, :]

    u_spec = pl.BlockSpec((n_rows, SLAB_W), lambda s, k: (rb, s))
    b_spec = pl.BlockSpec((1, SLAB_W, STATE_W), lambda s, k: (s, 0, 0))
    c_spec = pl.BlockSpec((1, STATE_W, SLAB_W), lambda s, k: (s, 0, 0))
    a_spec = pl.BlockSpec((1, 1, STATE_W), lambda s, k: (s, 0, 0))
    st_spec = pl.BlockSpec((1, N_SEG, STATE_W), lambda s, k: (s, 0, 0))
    scratch = [pltpu.VMEM((N_SEG, STATE_W), F32), pltpu.VMEM((tile, SLAB_W), F32), pltpu.VMEM((tile, STATE_W), F32)]
    kmap = (lambda s, k: (s, nk - 1 - k, 0, 0)) if descending else (lambda s, k: (s, k, 0, 0))
    out_specs = [u_spec, pl.BlockSpec((1, 1, N_SEG, STATE_W), kmap)]
    out_shape = [jax.ShapeDtypeStruct((y_rows, D_MODEL), F32), jax.ShapeDtypeStruct((N_SLAB, nk, N_SEG, STATE_W), F32)]
    in_specs = [u_spec, b_spec, c_spec, a_spec, st_spec]
    args = [u, b_mat, c_mat, abar, h_in]
    aliases = {}
    if y_alias is not None:
        in_specs.append(pl.BlockSpec(memory_space=pl.ANY))
        args.append(y_alias)
        aliases = {5: 0}
    return pl.pallas_call(
        body, grid=(N_SLAB, nk), in_specs=in_specs, out_specs=out_specs, out_shape=out_shape, scratch_shapes=scratch,
        input_output_aliases=aliases, compiler_params=_params("parallel", "arbitrary"), name=name)(*args)


def _s5_scan_bwd(name, u, dy, n_rows, row0, b_mat, bt_mat, ct_mat, abar, h_chunks, g_in, descending,
                 du_alias=None, du_rows=None):
    seg = n_rows // N_SEG
    ta = min(32, seg)
    nk = seg // ta
    rb = row0 // n_rows
    tile = ta * N_SEG
    g_desc = not descending

    def body(*refs):
        u_ref, dy_ref, b_ref, bt_ref, ct_ref, a_ref, hch_ref, gin_ref = refs[:8]
        du_ref, db_ref, dc_ref, da_ref, st_ref, up_ref, dyp_ref, h_ref, g_ref = refs[-9:]
        k = pl.program_id(1)
        kk = nk - 1 - k if g_desc else k
        a0 = kk * ta
        ar = jnp.broadcast_to(a_ref[0, :, :HALF_W], (N_SEG, HALF_W))
        ai = jnp.broadcast_to(a_ref[0, :, HALF_W:], (N_SEG, HALF_W))

        @pl.when(k == 0)
        def _():
            st_ref[...] = gin_ref[0]

        for al in range(ta):
            dyp_ref[al * N_SEG:(al + 1) * N_SEG, :] = dy_ref[pl.ds(a0 + al, N_SEG, stride=seg), :]
            up_ref[al * N_SEG:(al + 1) * N_SEG, :] = u_ref[pl.ds(a0 + al, N_SEG, stride=seg), :]
        g_ref[...] = jnp.dot(dyp_ref[...].astype(BF16), ct_ref[0], preferred_element_type=F32)
        h_ref[...] = jnp.dot(up_ref[...].astype(BF16), b_ref[0], preferred_element_type=F32)
        h0r, h0i = hch_ref[0, 0, :, :HALF_W], hch_ref[0, 0, :, HALF_W:]

        def hstep(i, carry):
            hr, hi = carry
            al = ta - 1 - i if descending else i
            row = pl.multiple_of(al * N_SEG, N_SEG)
            pr, pi = _cmul(ar, ai, hr, hi, False)
            hr = pr + h_ref[pl.ds(row, N_SEG), :HALF_W]
            hi = pi + h_ref[pl.ds(row, N_SEG), HALF_W:]
            h_ref[pl.ds(row, N_SEG), :HALF_W] = hr
            h_ref[pl.ds(row, N_SEG), HALF_W:] = hi
            return hr, hi

        lax.fori_loop(0, ta, hstep, (h0r, h0i), unroll=True)

        def gstep(i, carry):
            gr, gi = carry
            al = ta - 1 - i if g_desc else i
            row = pl.multiple_of(al * N_SEG, N_SEG)
            pr, pi = _cmul(ar, ai, gr, gi, True)
            gr = pr + g_ref[pl.ds(row, N_SEG), :HALF_W]
            gi = pi + g_ref[pl.ds(row, N_SEG), HALF_W:]
            g_ref[pl.ds(row, N_SEG), :HALF_W] = gr
            g_ref[pl.ds(row, N_SEG), HALF_W:] = gi
            return gr, gi

        gr, gi = lax.fori_loop(0, ta, gstep, (st_ref[:, :HALF_W], st_ref[:, HALF_W:]), unroll=True)
        st_ref[:, :HALF_W] = gr
        st_ref[:, HALF_W:] = gi

        gb = g_ref[...].astype(BF16)
        dut = jnp.dot(gb, bt_ref[0], preferred_element_type=F32)
        for al in range(ta):
            du_ref[pl.ds(a0 + al, N_SEG, stride=seg), :] = dut[al * N_SEG:(al + 1) * N_SEG, :]
        tn = (((0,), (0,)), ((), ()))
        dbp = lax.dot_general(up_ref[...].astype(BF16), gb, tn, preferred_element_type=F32)
        dcp = lax.dot_general(dyp_ref[...].astype(BF16), h_ref[...].astype(BF16), tn, preferred_element_type=F32)
        inner = (ta - 1) * N_SEG
        if descending:
            g_in_r, g_in_i = g_ref[0:inner, :HALF_W], g_ref[0:inner, HALF_W:]
            p_in_r, p_in_i = h_ref[N_SEG:tile, :HALF_W], h_ref[N_SEG:tile, HALF_W:]
            g_ed_r, g_ed_i = g_ref[inner:tile, :HALF_W], g_ref[inner:tile, HALF_W:]
        else:
            g_in_r, g_in_i = g_ref[N_SEG:tile, :HALF_W], g_ref[N_SEG:tile, HALF_W:]
            p_in_r, p_in_i = h_ref[0:inner, :HALF_W], h_ref[0:inner, HALF_W:]
            g_ed_r, g_ed_i = g_ref[0:N_SEG, :HALF_W], g_ref[0:N_SEG, HALF_W:]
        dar = g_ed_r * h0r + g_ed_i * h0i
        dai = g_ed_i * h0r - g_ed_r * h0i
        if ta > 1:
            dar = dar + jnp.sum((g_in_r * p_in_r + g_in_i * p_in_i).reshape(ta - 1, N_SEG, HALF_W), axis=0)
            dai = dai + jnp.sum((g_in_i * p_in_r - g_in_r * p_in_i).reshape(ta - 1, N_SEG, HALF_W), axis=0)

        @pl.when(k == 0)
        def _():
            db_ref[0] = dbp
            dc_ref[0] = dcp
            da_ref[0, :, :HALF_W] = dar
            da_ref[0, :, HALF_W:] = dai

        @pl.when(k > 0)
        def _():
            db_ref[0] += dbp
            dc_ref[0] += dcp
            da_ref[0, :, :HALF_W] += dar
            da_ref[0, :, HALF_W:] += dai

    u_spec = pl.BlockSpec((n_rows, SLAB_W), lambda s, k: (rb, s))
    m_spec = pl.BlockSpec((1, SLAB_W, STATE_W), lambda s, k: (s, 0, 0))
    mt_spec = pl.BlockSpec((1, STATE_W, SLAB_W), lambda s, k: (s, 0, 0))
    a_spec = pl.BlockSpec((1, 1, STATE_W), lambda s, k: (s, 0, 0))
    st_spec = pl.BlockSpec((1, N_SEG, STATE_W), lambda s, k: (s, 0, 0))
    st_shape = jax.ShapeDtypeStruct((N_SLAB, N_SEG, STATE_W), F32)
    kmap = (lambda s, k: (s, nk - 1 - k, 0, 0)) if g_desc else (lambda s, k: (s, k, 0, 0))
    in_specs = [u_spec, u_spec, m_spec, mt_spec, m_spec, a_spec, pl.BlockSpec((1, 1, N_SEG, STATE_W), kmap), st_spec]
    args = [u, dy, b_mat, bt_mat, ct_mat, abar, h_chunks, g_in]
    aliases = {}
    if du_alias is not None:
        in_specs.append(pl.BlockSpec(memory_space=pl.ANY))
        args.append(du_alias)
        aliases = {8: 0}
    acc_shape = jax.ShapeDtypeStruct((N_SLAB, SLAB_W, STATE_W), F32)
    out_specs = [u_spec, m_spec, m_spec, st_spec]
    out_shape = [jax.ShapeDtypeStruct((du_rows, D_MODEL), F32), acc_shape, acc_shape, st_shape]
    scratch = [pltpu.VMEM((N_SEG, STATE_W), F32), pltpu.VMEM((tile, SLAB_W), F32), pltpu.VMEM((tile, SLAB_W), F32),
               pltpu.VMEM((tile, STATE_W), F32), pltpu.VMEM((tile, STATE_W), F32)]
    return pl.pallas_call(
        body, grid=(N_SLAB, nk), in_specs=in_specs, out_specs=out_specs, out_shape=out_shape, scratch_shapes=scratch,
        input_output_aliases=aliases, compiler_params=_params("parallel", "arbitrary"), name=name)(*args)


ROPE_HALF = HEAD_DIM // 4
TABLE_W = 2 * HEAD_DIM
Q_SCALE = 1.0 / math.sqrt(HEAD_DIM)
HEADS_PER_BLOCK = 2 * KV_REP
Q_BLOCK_W = HEADS_PER_BLOCK * HEAD_DIM


def _rope_tables(n_lat, n_ctx):
    rows = n_lat // GRID_W
    freqs = ROPE_THETA ** (-jnp.arange(ROPE_HALF, dtype=F32) / ROPE_HALF)
    ang_r = jnp.arange(rows, dtype=F32)[:, None] * freqs[None]
    ang_c = jnp.arange(GRID_W, dtype=F32)[:, None] * freqs[None]
    by_row = lambda v: jnp.repeat(v, GRID_W, axis=0)
    by_col = lambda v: jnp.tile(v, (rows, 1))
    cos = jnp.concatenate([by_row(jnp.cos(ang_r)), by_row(jnp.cos(ang_r)), by_col(jnp.cos(ang_c)), by_col(jnp.cos(ang_c))] * 2,
                          axis=1)
    sin = jnp.concatenate([by_row(jnp.sin(ang_r)), by_row(jnp.sin(ang_r)), by_col(jnp.sin(ang_c)), by_col(jnp.sin(ang_c))] * 2,
                          axis=1)
    cos = jnp.concatenate([cos, jnp.ones((n_ctx, TABLE_W), F32)], axis=0)
    sin = jnp.concatenate([sin, jnp.zeros((n_ctx, TABLE_W), F32)], axis=0)
    return cos, sin


def _rot_half(v):
    w = v.shape[1]
    ahead = pltpu.roll(v, w - ROPE_HALF, axis=1)
    behind = pltpu.roll(v, ROPE_HALF, axis=1)
    lane = lax.broadcasted_iota(jnp.int32, v.shape, 1)
    return jnp.where((lane % (2 * ROPE_HALF)) < ROPE_HALF, -ahead, behind)


def _head_mean(v, sel, selt):
    m = jnp.dot(v, sel, precision=lax.Precision.HIGH, preferred_element_type=F32) * (1.0 / HEAD_DIM)
    return jnp.dot(m, selt, precision=lax.Precision.HIGH, preferred_element_type=F32)


def _head_selectors(n_heads):
    sel = jnp.repeat(jnp.eye(n_heads, dtype=F32), HEAD_DIM, axis=0)
    return sel[None], sel.T[None]


def _head_norm(x, sel, selt):
    r = lax.rsqrt(_head_mean(x * x, sel, selt) + NORM_EPS)
    return x * r, r


def _qk_prep(proj, qn, kn, cos, sin, n, tr):
    qw, kw = _vec(jnp.tile(qn, N_Q_HEADS)), _vec(jnp.tile(kn, N_KV_HEADS))
    sq, sqt = _head_selectors(N_Q_HEADS)
    sk, skt = _head_selectors(N_KV_HEADS)

    def fn(qr, kvr, ct, st, qwv, kwv, s16, s16t, s4, s4t):
        outs = []
        for x, wv, sel, selt, scale in ((qr, qwv, s16, s16t, Q_SCALE), (kvr[:, :KV_W], kwv, s4, s4t, 1.0)):
            reps = x.shape[1] // TABLE_W
            cw, sw = jnp.tile(ct, (1, reps)), jnp.tile(st, (1, reps))
            xh, _ = _head_norm(x, sel, selt)
            nrm = xh * wv
            outs.append((nrm * cw + _rot_half(nrm) * sw) * scale)
        return outs[0], outs[1], kvr[:, KV_W:]

    return _rowwise("l1_qk_prep", fn, n, tr,
                    [(proj, 0, ATTN_W), (proj, 2 * ATTN_W // (2 * KV_W), 2 * KV_W), (cos, 0, TABLE_W), (sin, 0, TABLE_W)],
                    [qw, kw, sq, sqt, sk, skt], [(ATTN_W, BF16), (KV_W, BF16), (KV_W, BF16)], [])


def _qk_prep_bwd(proj, qn, kn, cos, sin, dq, dz, dk, dv, n, n_lat, tr):
    qw, kw = _vec(jnp.tile(qn, N_Q_HEADS)), _vec(jnp.tile(kn, N_KV_HEADS))
    sq, sqt = _head_selectors(N_Q_HEADS)
    sk, skt = _head_selectors(N_KV_HEADS)
    nlt = n_lat // tr

    def fn(flag, qr, kvr, ct, st, dqt, dzt, dkt, dvt, qwv, kwv, s16, s16t, s4, s4t):
        dxs, dws = [], []
        for x, dy, wv, sel, selt in ((qr, dqt * (flag * Q_SCALE), qwv, s16, s16t), (kvr[:, :KV_W], dkt, kwv, s4, s4t)):
            reps = x.shape[1] // TABLE_W
            cw, sw = jnp.tile(ct, (1, reps)), jnp.tile(st, (1, reps))
            xh, r = _head_norm(x, sel, selt)
            dn = dy * cw - _rot_half(dy * sw)
            dxh = dn * wv
            dxs.append(r * (dxh - xh * _head_mean(dxh * xh, sel, selt)))
            dws.append(dn * xh)
        return jnp.concatenate([dxs[0], dzt * flag, dxs[1], dvt], axis=1), dws[0], dws[1]

    dproj, dqw, dkw = _rowwise(
        "l1_qk_prep_bwd", fn, n, tr,
        [(proj, 0, ATTN_W), (proj, 2 * ATTN_W // (2 * KV_W), 2 * KV_W), (cos, 0, TABLE_W), (sin, 0, TABLE_W),
         (dq, 0, ATTN_W, "clamp", nlt), (dz, 0, ATTN_W, "clamp", nlt), (dk, 0, KV_W), (dv, 0, KV_W)],
        [qw, kw, sq, sqt, sk, skt], [(2 * ATTN_W + 2 * KV_W, BF16)], [(1, ATTN_W), (1, KV_W)], n_lat=n_lat, want_flag=True)
    return dproj, dqw.reshape(N_Q_HEADS, HEAD_DIM).sum(0)[None], dkw.reshape(N_KV_HEADS, HEAD_DIM).sum(0)[None]


NT = (((1,), (1,)), ((), ()))


def _attn_fwd(q, k, v, proj, t, tq, tk):
    n = k.shape[0]
    nkc = n // tk

    ts = _largest_tile(tq, 256, LANES)
    items = [(sub, j) for sub in range(tq // ts) for j in range(HEADS_PER_BLOCK)]

    def body(q_ref, k_ref, v_ref, z_ref, o_ref, lse_ref, gz_ref, s_ref, m_ref, acc_ref):
        def lanes(j):
            g = j // KV_REP
            return slice(j * HEAD_DIM, (j + 1) * HEAD_DIM), slice(g * HEAD_DIM, (g + 1) * HEAD_DIM)

        for idx in range(len(items) + 1):
            nxt = items[idx] if idx < len(items) else None
            cur = items[idx - 1] if idx > 0 else None
            sn, sc = idx % 2, (idx - 1) % 2
            if nxt is not None:
                rows_n = slice(nxt[0] * ts, (nxt[0] + 1) * ts)
                ql_n, kl_n = lanes(nxt[1])
                qv = q_ref[rows_n, ql_n]
                m_ref[sn] = jnp.full((ts, LANES), -jnp.inf, F32)
            if cur is not None:
                rows_c = slice(cur[0] * ts, (cur[0] + 1) * ts)
                ql_c, kl_c = lanes(cur[1])
                m_row = jnp.max(m_ref[sc], axis=-1, keepdims=True)
                acc_ref[...] = jnp.zeros(acc_ref.shape, F32)

            def sweep(kc, c):
                off = pl.multiple_of(kc * tk, tk)
                if nxt is not None:
                    s = lax.dot_general(qv, k_ref[pl.ds(off, tk), kl_n], NT, preferred_element_type=F32)
                    s_ref[sn, :, pl.ds(off, tk)] = s
                    m = m_ref[sn]
                    for cb in range(tk // LANES):
                        m = jnp.maximum(m, s[:, cb * LANES:(cb + 1) * LANES])
                    m_ref[sn] = m
                if cur is not None:
                    p = jnp.exp(s_ref[sc, :, pl.ds(off, tk)] - m_row)
                    v_one = jnp.concatenate([v_ref[pl.ds(off, tk), kl_c], jnp.ones((tk, HEAD_DIM), BF16)], axis=1)
                    acc_ref[...] += jnp.dot(p.astype(BF16), v_one, preferred_element_type=F32)
                return c

            lax.fori_loop(0, nkc, sweep, 0, unroll=True)
            if cur is not None:
                l_row = acc_ref[:, HEAD_DIM:HEAD_DIM + 1]
                o_head = acc_ref[:, :HEAD_DIM] / l_row
                o_ref[rows_c, ql_c] = o_head
                gz_ref[rows_c, ql_c] = (o_head * _silu(z_ref[rows_c, ql_c].astype(F32))).astype(BF16)
                lse_ref[0, rows_c, cur[1]:cur[1] + 1] = m_row + jnp.log(l_row)

    nb = ATTN_W // Q_BLOCK_W
    kspec = pl.BlockSpec((n, LANES), lambda b, i: (0, b))
    return pl.pallas_call(
        body, grid=(nb, t // tq),
        in_specs=[pl.BlockSpec((tq, Q_BLOCK_W), lambda b, i: (i, b)), kspec, kspec,
                  pl.BlockSpec((tq, Q_BLOCK_W), lambda b, i: (i, nb + b))],
        out_specs=[pl.BlockSpec((tq, Q_BLOCK_W), lambda b, i: (i, b)),
                   pl.BlockSpec((1, tq, HEADS_PER_BLOCK), lambda b, i: (b, i, 0)),
                   pl.BlockSpec((tq, Q_BLOCK_W), lambda b, i: (i, b))],
        out_shape=[jax.ShapeDtypeStruct((t, ATTN_W), F32), jax.ShapeDtypeStruct((nb, t, HEADS_PER_BLOCK), F32),
                   jax.ShapeDtypeStruct((t, ATTN_W), BF16)],
        scratch_shapes=[pltpu.VMEM((2, ts, n), F32), pltpu.VMEM((2, ts, LANES), F32), pltpu.VMEM((ts, 2 * HEAD_DIM), F32)],
        compiler_params=_params("parallel", "parallel"), name="attn_fwd")(q, k, v, proj)


def _attn_bwd(q, k, v, dgz, proj, o, lse, t, tq, tk):
    n = k.shape[0]
    nkc = n // tk
    tn = (((0,), (0,)), ((), ()))

    def body(q_ref, k_ref, v_ref, dgz_ref, z_ref, o_ref, lse_ref, dq_ref, dk_ref, dv_ref, dz_ref, acc_ref):
        @pl.when(pl.program_id(1) == 0)
        def _():
            dk_ref[...] = jnp.zeros(dk_ref.shape, F32)
            dv_ref[...] = jnp.zeros(dv_ref.shape, F32)

        for j0 in range(0, HEADS_PER_BLOCK, 2):
            kl = slice((j0 // KV_REP) * HEAD_DIM, (j0 // KV_REP + 1) * HEAD_DIM)
            heads = []
            for a in range(2):
                j = j0 + a
                ql = slice(j * HEAD_DIM, (j + 1) * HEAD_DIM)
                qv, ov = q_ref[:, ql], o_ref[:, ql]
                zv, dgv = z_ref[:, ql].astype(F32), dgz_ref[:, ql].astype(F32)
                dov = (dgv * _silu(zv)).astype(BF16)
                dz_ref[:, ql] = (dgv * ov * _silu_grad(zv)).astype(dz_ref.dtype)
                dl_v = jnp.sum(dov.astype(F32) * ov, axis=-1, keepdims=True)
                heads.append((ql, qv, dov, dl_v, lse_ref[0, :, j:j + 1]))
                acc_ref[a] = jnp.zeros((tq, HEAD_DIM), F32)

            def step(kc, c):
                off = pl.multiple_of(kc * tk, tk)
                kt = k_ref[pl.ds(off, tk), kl]
                vt = v_ref[pl.ds(off, tk), kl]
                dv_part, dk_part = None, None
                for a, (_, qv, dov, dl_v, lse_v) in enumerate(heads):
                    s = lax.dot_general(qv, kt, NT, preferred_element_type=F32)
                    p = jnp.exp(s - lse_v)
                    dp = lax.dot_general(dov, vt, NT, preferred_element_type=F32)
                    ds = (p * (dp - dl_v)).astype(BF16)
                    acc_ref[a] += jnp.dot(ds, kt, preferred_element_type=F32)
                    dvp = lax.dot_general(p.astype(BF16), dov, tn, preferred_element_type=F32)
                    dkp = lax.dot_general(ds, qv, tn, preferred_element_type=F32)
                    dv_part = dvp if dv_part is None else dv_part + dvp
                    dk_part = dkp if dk_part is None else dk_part + dkp
                dv_ref[pl.ds(off, tk), kl] += dv_part
                dk_ref[pl.ds(off, tk), kl] += dk_part
                return c

            lax.fori_loop(0, nkc, step, 0, unroll=2)
            for a, h in enumerate(heads):
                dq_ref[:, h[0]] = acc_ref[a].astype(dq_ref.dtype)

    nb = ATTN_W // Q_BLOCK_W
    qspec = pl.BlockSpec((tq, Q_BLOCK_W), lambda b, i: (i, b))
    kspec = pl.BlockSpec((n, LANES), lambda b, i: (0, b))
    cspec = pl.BlockSpec((1, tq, HEADS_PER_BLOCK), lambda b, i: (b, i, 0))
    return pl.pallas_call(
        body, grid=(nb, t // tq),
        in_specs=[qspec, kspec, kspec, qspec, pl.BlockSpec((tq, Q_BLOCK_W), lambda b, i: (i, nb + b)), qspec, cspec],
        out_specs=[qspec, kspec, kspec, qspec],
        out_shape=[jax.ShapeDtypeStruct((t, ATTN_W), BF16), jax.ShapeDtypeStruct((n, KV_W), F32),
                   jax.ShapeDtypeStruct((n, KV_W), F32), jax.ShapeDtypeStruct((t, ATTN_W), BF16)],
        scratch_shapes=[pltpu.VMEM((2, tq, HEAD_DIM), F32)],
        compiler_params=_params("parallel", "arbitrary"), name="attn_bwd")(q, k, v, dgz, proj, o, lse)


def _s5_system(p, n_lat, n_ctx):
    two_g = 2 * SSM_GROUPS
    a_re = p["ssm_a_re"].reshape(two_g, SSM_STATE)
    a_im = p["ssm_a_im"].reshape(two_g, SSM_STATE)
    log_dt = p["ssm_log_dt"].reshape(two_g, 1)
    b_re = p["ssm_b_re"].reshape(two_g, SSM_STATE, SSM_GROUP).transpose(2, 0, 1)
    b_im = p["ssm_b_im"].reshape(two_g, SSM_STATE, SSM_GROUP).transpose(2, 0, 1)
    raw = (a_re, a_im, log_dt, b_re, b_im)
    abr, abi, bbr, bbi, alr, ali, acr, aci = _s5_prep(*raw, n_lat // N_SEG, n_ctx // N_SEG)
    dirs = []
    for d in range(2):
        g = slice(d * SSM_GROUPS, (d + 1) * SSM_GROUPS)
        b_mat = _slab_in_matrix(bbr[:, g].transpose(1, 2, 0), bbi[:, g].transpose(1, 2, 0))
        c_mat = _slab_out_matrix(p["ssm_c_re"][0, d], p["ssm_c_im"][0, d])
        abar = _slab_pair(abr[g], abi[g])
        tables = {}
        for part, seg in (("lat", n_lat // N_SEG), ("ctx", n_ctx // N_SEG)):
            tables["h_" + part] = _s5_pow_table(f"s5_pow_h{d}_{part}", abar, seg, d == 0, False)
            tables["g_" + part] = _s5_pow_table(f"s5_pow_g{d}_{part}", abar, seg, d == 1, True)
        dirs.append(dict(
            b=b_mat.astype(BF16), bt=b_mat.transpose(0, 2, 1).astype(BF16), b32=b_mat,
            c=c_mat.astype(BF16), ct=c_mat.transpose(0, 2, 1).astype(BF16), ct32=c_mat.transpose(0, 2, 1),
            abar=abar, a_lat=_slab_pair(alr[g], ali[g])[:, 0], a_ctx=_slab_pair(acr[g], aci[g])[:, 0], **tables))
    return raw, dirs


def _s5_forward(proj, dirs, n_lat, n_ctx):
    n = n_lat + n_ctx
    zero_c = jnp.zeros((N_SLAB, STATE_W), F32)
    ys, saved = [], []
    for d, s in enumerate(dirs):
        desc = d == 1
        tag = f"s5f{d}"
        zc = _s5_ends(tag + "_ctx_ends", proj, n_ctx, n_lat, s["h_ctx"], s["b32"])
        ent_c, h0 = _s5_carry(tag + "_ctx_carry", zc, s["a_ctx"], zero_c, desc, False)
        y, hch_c = _s5_scan(tag + "_ctx", proj, n_ctx, n_lat, s["b"], s["c"], s["abar"], ent_c, desc,
                            y_rows=n)
        zl = _s5_ends(tag + "_lat_ends", proj, n_lat, 0, s["h_lat"], s["b32"])
        ent_l, _ = _s5_carry(tag + "_lat_carry", zl, s["a_lat"], h0, desc, False)
        y, hch_l = _s5_scan(tag + "_lat", proj, n_lat, 0, s["b"], s["c"], s["abar"], ent_l, desc,
                            y_alias=y, y_rows=n)
        ys.append(y)
        saved.append((hch_l, hch_c))
    return ys, saved


def _s5_backward(proj, dy, dirs, saved, n_lat, n_ctx):
    n = n_lat + n_ctx
    zero_c = jnp.zeros((N_SLAB, STATE_W), F32)
    out = []
    for d, s in enumerate(dirs):
        desc = d == 1
        tag = f"s5b{d}"
        hch_l, hch_c = saved[d]
        gl = _s5_ends(tag + "_lat_ends", dy, n_lat, 0, s["g_lat"], s["ct32"])
        ent_l, g0 = _s5_carry(tag + "_lat_carry", gl, s["a_lat"], zero_c, not desc, True)
        du, db_l, dc_l, da_l = _s5_scan_bwd(tag + "_lat", proj, dy, n_lat, 0, s["b"], s["bt"], s["ct"], s["abar"],
                                            hch_l, ent_l, desc, du_rows=n)
        gc = _s5_ends(tag + "_ctx_ends", dy, n_ctx, n_lat, s["g_ctx"], s["ct32"])
        ent_c, _ = _s5_carry(tag + "_ctx_carry", gc, s["a_ctx"], g0, not desc, True)
        du, db_c, dc_c, da_c = _s5_scan_bwd(tag + "_ctx", proj, dy, n_ctx, n_lat, s["b"], s["bt"], s["ct"], s["abar"],
                                            hch_c, ent_c, desc, du_alias=du, du_rows=n)
        out.append((du, db_l + db_c, dc_l + dc_c, da_l + da_c))
    return out


def _s5_param_grads(raw, bwd):
    dabr, dabi, dbbr, dbbi, dcr, dci = [], [], [], [], [], []
    for _, db, dc, da in bwd:
        da = jnp.sum(da, axis=1)
        dabr.append(da[:, :HALF_W].reshape(SSM_GROUPS, SSM_STATE))
        dabi.append(da[:, HALF_W:].reshape(SSM_GROUPS, SSM_STATE))
        dbd = _slab_diag(db)
        dbbr.append(dbd[0].transpose(1, 0, 2))
        dbbi.append(dbd[1].transpose(1, 0, 2))
        dcd = _slab_diag(dc)
        dcr.append(dcd[0])
        dci.append(-dcd[1])
    cat = lambda xs, ax: jnp.concatenate(xs, axis=ax)
    dar, dai, dld, dbr, dbi = _s5_prep_bwd(*raw, cat(dabr, 0), cat(dabi, 0), cat(dbbr, 1), cat(dbbi, 1))
    shp = (1, 2, SSM_GROUPS, SSM_STATE)
    b_shape = (1, 2, SSM_GROUPS, SSM_STATE, SSM_GROUP)
    return dict(
        ssm_a_re=dar.reshape(shp), ssm_a_im=dai.reshape(shp), ssm_log_dt=dld.reshape(1, 2, SSM_GROUPS),
        ssm_b_re=dbr.transpose(1, 2, 0).reshape(b_shape), ssm_b_im=dbi.transpose(1, 2, 0).reshape(b_shape),
        ssm_c_re=jnp.stack(dcr)[None], ssm_c_im=jnp.stack(dci)[None])


def _example_step(x, ctx, target, mods, w, p):
    t, c = x.shape[0], ctx.shape[0]
    n = t + c
    assert t % c == 0 and c % LANES == 0 and c % (8 * N_SEG) == 0 and t % GRID_W == 0
    tr = _largest_tile(c, 256, 8)
    xall = (x, ctx)
    g0, g1 = _vec(p["norm_g"][0]), _vec(p["norm_g"][1])
    (shift0, scale0, gate0), (shift1, scale1, gate1) = [tuple(_vec(v) for v in m) for m in mods]

    h0 = _norm_mod_fwd("l0_norm", xall, g0, scale0, shift0, n, tr, t)
    proj0 = _mm("l0_in", h0, w["ssm_w_in"], "nn")
    raw, dirs = _s5_system(p, t, c)
    (y_f, y_r), saved = _s5_forward(proj0, dirs, t, c)
    d_skip = _vec(p["ssm_d"][0])

    def post_a(u, yf, yr, dv):
        y = u * dv + yf + yr
        return y, _gelu(y)

    y0, yg = _rowwise("l0_gelu", post_a, n, tr, [(proj0, 0, D_MODEL), (y_f, 0, D_MODEL), (y_r, 0, D_MODEL)], [d_skip],
                      [(D_MODEL, BF16), (D_MODEL, BF16)], [])
    tg = _mm("l0_glu", yg, w["ssm_w_glu"], "nn", out_dtype=BF16)
    b_glu = _vec(p["ssm_b_glu"][0])

    def post_b(ygt, tt, zt, bv):
        return ygt * _sigmoid(tt + bv) * _silu(zt)

    gz0 = _rowwise("l0_gate", post_b, n, tr, [(yg, 0, D_MODEL), (tg, 0, D_MODEL), (proj0, 1, D_MODEL)], [b_glu],
                   [(D_MODEL, BF16)], [])[0]
    out0 = _mm("l0_out", gz0, w["ssm_w_out"], "nn")

    def res_norm(xt, ot, gv, g1v, sc, sh):
        x1t = xt + gv * ot
        xh, _ = _rms_hat(x1t)
        return x1t, (xh * g1v) * (1.0 + sc) + sh

    x1, h1 = _rowwise("l0_res_l1_norm", res_norm, n, tr, [_stream(xall), (out0, 0, D_MODEL)],
                      [gate0, g1, scale1, shift1], [(D_MODEL, F32), (D_MODEL, BF16)], [], n_lat=t)
    proj1 = _mm("l1_in", h1, w["attn_w_in"], "nn", out_dtype=BF16)
    cos, sin = _rope_tables(t, c)
    qn, kn = p["attn_q_norm"][0], p["attn_k_norm"][0]
    q_h, k_h, v_h = _qk_prep(proj1, qn, kn, cos, sin, n, tr)
    tq = _largest_tile(t, 512, LANES)
    o, lse, gz1 = _attn_fwd(q_h, k_h, v_h, proj1, t, tq, _largest_tile(n, 2816, LANES))
    out1 = _mm("l1_out", gz1, w["attn_w_out"], "nn")

    gf = _vec(p["final_norm_g"])

    def head(x1t, o1t, tgt, g1v, gfv):
        x2 = x1t + g1v * o1t
        xh, r = _rms_hat(x2)
        e = xh * gfv - tgt
        dyf = e * (1.0 / D_MODEL)
        dx2 = _rms_bwd(xh, r, dyf * gfv)
        return dx2, g1v * dx2, dyf * xh, dx2 * o1t, jnp.sum(e * e, axis=1, keepdims=True)

    gate1_lat = gate1[0:1]
    dx2, dout1, d_gf, d_gate1, sq = _rowwise(
        "head", head, t, tr, [(x1, 0, D_MODEL), (out1, 0, D_MODEL), (target, 0, D_MODEL)], [gate1_lat, gf],
        [(D_MODEL, F32), (D_MODEL, BF16)], [(1, D_MODEL), (1, D_MODEL), (1, 1)])

    d_w_attn_out = _mm("l1_out_dw", gz1, dout1, "tn", out_dtype=BF16)
    dgz1 = _mm("l1_out_dx", dout1, w["attn_w_out"], "nt", out_dtype=BF16)
    dq_s, dk, dv, dz1 = _attn_bwd(q_h, k_h, v_h, dgz1, proj1, o, lse, t, tq, _largest_tile(n, 1024, LANES))
    dproj1, d_qn, d_kn = _qk_prep_bwd(proj1, qn, kn, cos, sin, dq_s, dz1, dk, dv, n, t, tr)
    d_w_attn_in = _mm("l1_in_dw", h1, dproj1, "tn", out_dtype=BF16)
    dh1 = _mm("l1_in_dx", dproj1, w["attn_w_in"], "nt", out_dtype=BF16)
    dx1, dout0, d_g1, d_scale1, d_shift1, d_gate0 = _norm_mod_bwd("l1_norm_bwd", x1, g1, scale1, dh1, dx2, n, tr, t,
                                                                  prev=(out0, gate0))

    d_w_out = _mm("l0_out_dw", gz0, dout0, "tn", out_dtype=BF16)
    dgz0 = _mm("l0_out_dx", dout0, w["ssm_w_out"], "nt", out_dtype=BF16)

    def post_b_bwd(dgt, ygt, tt, zt, bv):
        s = _sigmoid(tt + bv)
        dy2 = dgt * _silu(zt)
        dt = dy2 * ygt * s * (1.0 - s)
        return dgt * (ygt * s) * _silu_grad(zt), dt, dy2 * s, dt

    dz0, dtg, dyg_a, d_b_glu = _rowwise(
        "l0_gate_bwd", post_b_bwd, n, tr, [(dgz0, 0, D_MODEL), (yg, 0, D_MODEL), (tg, 0, D_MODEL), (proj0, 1, D_MODEL)],
        [b_glu], [(D_MODEL, BF16), (D_MODEL, BF16), (D_MODEL, BF16)], [(1, D_MODEL)])
    d_w_glu = _mm("l0_glu_dw", yg, dtg, "tn", out_dtype=BF16)
    dyg_b = _mm("l0_glu_dx", dtg, w["ssm_w_glu"], "nt", out_dtype=BF16)

    def post_a_bwd(da, db, yt, ut, dv):
        dy = (da + db) * _gelu_grad(yt)
        return dy, dy * dv, dy * ut

    dy0, du_skip, d_d = _rowwise("l0_gelu_bwd", post_a_bwd, n, tr,
                                 [(dyg_a, 0, D_MODEL), (dyg_b, 0, D_MODEL), (y0, 0, D_MODEL), (proj0, 0, D_MODEL)], [d_skip],
                                 [(D_MODEL, F32), (D_MODEL, BF16)], [(1, D_MODEL)])
    s5_bwd = _s5_backward(proj0, dy0, dirs, saved, t, c)
    dproj0 = _rowwise("l0_in_grad", lambda a, b, cc, dz: jnp.concatenate([a + b + cc, dz], axis=1), n, tr,
                      [(du_skip, 0, D_MODEL), (s5_bwd[0][0], 0, D_MODEL), (s5_bwd[1][0], 0, D_MODEL), (dz0, 0, D_MODEL)], [],
                      [(2 * D_MODEL, BF16)], [])[0]
    d_w_in = _mm("l0_in_dw", h0, dproj0, "tn", out_dtype=BF16)
    dh0 = _mm("l0_in_dx", dproj0, w["ssm_w_in"], "nt", out_dtype=BF16)
    dx0, d_g0, d_scale0, d_shift0 = _norm_mod_bwd("l0_norm_bwd", xall, g0, scale0, dh0, dx1, n, tr, t, dx_lat_only=True)

    big = dict(ssm_w_in=d_w_in, ssm_w_glu=d_w_glu, ssm_w_out=d_w_out, attn_w_in=d_w_attn_in, attn_w_out=d_w_attn_out)
    small = dict(
        norm_g=jnp.concatenate([d_g0[0], d_g1[0]], axis=0), ssm_d=d_d[0], ssm_b_glu=d_b_glu[0],
        attn_q_norm=d_qn, attn_k_norm=d_kn, final_norm_g=d_gf[0, 0], **_s5_param_grads(raw, s5_bwd))
    zero_v = jnp.zeros((D_MODEL,), F32)
    d_mod_lat = jnp.stack([jnp.concatenate([d_shift0[0, 0], d_scale0[0, 0], d_gate0[0, 0]]),
                           jnp.concatenate([d_shift1[0, 0], d_scale1[0, 0], d_gate1[0, 0]])])
    d_mod_ctx = jnp.stack([jnp.concatenate([d_shift0[1, 0], d_scale0[1, 0], d_gate0[1, 0]]),
                           jnp.concatenate([d_shift1[1, 0], d_scale1[1, 0], zero_v])])
    return sq[0, 0, 0], dx0, big, small, d_mod_lat, d_mod_ctx


def _adamw(name, w, g, m, v):
    rows, cols = w.shape
    tr = _largest_tile(rows, 256, 8)
    c1 = 1.0 / (1.0 - ADAM_B1 ** ADAM_STEP)
    c2 = 1.0 / (1.0 - ADAM_B2 ** ADAM_STEP)

    def fn(wt, gt, mt, vt):
        mn = ADAM_B1 * mt + (1.0 - ADAM_B1) * gt
        vn = ADAM_B2 * vt + (1.0 - ADAM_B2) * (gt * gt)
        delta = -ADAM_LR * ((mn * c1) / (jnp.sqrt(vn * c2) + ADAM_EPS) + ADAM_WD * wt)
        return delta, mn, vn

    return _rowwise(name, fn, rows, tr, [(a, 0, cols) for a in (w, g, m, v)], [], [(cols, F32)] * 3, [])


BIG = ("ssm_w_in", "ssm_w_glu", "ssm_w_out", "attn_w_in", "attn_w_out")
COL_SHARDED = ("ssm_w_in", "attn_w_in")
WEIGHTS = ("c_ctx", "w_mod", "b_mod", "norm_g", "ssm_w_in", "ssm_a_re", "ssm_a_im", "ssm_log_dt", "ssm_b_re", "ssm_b_im",
           "ssm_c_re", "ssm_c_im", "ssm_d", "ssm_w_glu", "ssm_b_glu", "ssm_w_out", "attn_w_in", "attn_q_norm",
           "attn_k_norm", "attn_w_out", "final_norm_g")
SMALL = tuple(k for k in WEIGHTS if k not in BIG and k != "w_mod")
PACK_W = 1024
COND_ROWS = 2 * N_DEV


def _attn_in_perm(x, inverse):
    a, kv = ATTN_W, 2 * KV_W
    if inverse:
        return jnp.concatenate([x[..., :a], x[..., 2 * a:], x[..., a:2 * a]], axis=-1)
    return jnp.concatenate([x[..., :a], x[..., a + kv:], x[..., a:a + kv]], axis=-1)


def _pack(arrays, dtype, row_unit):
    flat = jnp.concatenate([a.reshape(-1).astype(dtype) for a in arrays])
    rows = -(-flat.shape[0] // PACK_W)
    rows = -(-rows // row_unit) * row_unit
    flat = jnp.concatenate([flat, jnp.zeros((rows * PACK_W - flat.shape[0],), dtype)])
    return flat.reshape(rows, PACK_W)


def _unpack(buf, shapes):
    lead = buf.shape[:-2]
    flat = buf.reshape(lead + (-1,))
    out, off = [], 0
    for shp in shapes:
        size = math.prod(shp)
        out.append(flat[..., off:off + size].reshape(lead + tuple(shp)))
        off += size
    return out


def kernel(x, c, ctx, c_ctx, w_mod, b_mod, norm_g, ssm_w_in, ssm_a_re, ssm_a_im, ssm_log_dt, ssm_b_re, ssm_b_im, ssm_c_re, ssm_c_im, ssm_d, ssm_w_glu, ssm_b_glu, ssm_w_out, attn_w_in, attn_q_norm, attn_k_norm, attn_w_out, final_norm_g, loss_target, m_c_ctx, m_w_mod, m_b_mod, m_norm_g, m_ssm_w_in, m_ssm_a_re, m_ssm_a_im, m_ssm_log_dt, m_ssm_b_re, m_ssm_b_im, m_ssm_c_re, m_ssm_c_im, m_ssm_d, m_ssm_w_glu, m_ssm_b_glu, m_ssm_w_out, m_attn_w_in, m_attn_q_norm, m_attn_k_norm, m_attn_w_out, m_final_norm_g, v_c_ctx, v_w_mod, v_b_mod, v_norm_g, v_ssm_w_in, v_ssm_a_re, v_ssm_a_im, v_ssm_log_dt, v_ssm_b_re, v_ssm_b_im, v_ssm_c_re, v_ssm_c_im, v_ssm_d, v_ssm_w_glu, v_ssm_b_glu, v_ssm_w_out, v_attn_w_in, v_attn_q_norm, v_attn_k_norm, v_attn_w_out, v_final_norm_g):
    args = dict(locals())
    wts = {k: args[k] for k in WEIGHTS}
    mom_m = {k: args["m_" + k] for k in WEIGHTS}
    mom_v = {k: args["v_" + k] for k in WEIGHTS}
    mx, my, mc = lax.axis_index("x"), lax.axis_index("y"), lax.axis_index("c")
    chip = 2 * mx + my
    me = 2 * chip + mc

    halves = []
    for k in BIG:
        sh = wts[k][0]
        hr = sh.shape[0] // 2
        halves.append(lax.dynamic_slice_in_dim(sh, mc * hr, hr, axis=0))
    gathered = _gather_two_level("gather_weights", _pack(halves, BF16, 16))
    parts = _unpack(gathered, [h.shape for h in halves])
    w_full = {}
    for k, pc in zip(BIG, parts):
        hr, cols = pc.shape[1:]
        pc = pc.reshape(N_CHIP, 2, hr, cols)
        if k in COL_SHARDED:
            w_full[k] = pc.transpose(1, 2, 0, 3).reshape(2 * hr, N_CHIP * cols)
        else:
            w_full[k] = pc.reshape(N_CHIP * 2 * hr, cols)
    w_full["attn_w_in"] = _attn_in_perm(w_full["attn_w_in"], False)

    c_blk = jnp.concatenate([c, jnp.zeros((N_DEV - 1, D_MODEL), F32)], axis=0)
    c_all = _exchange("gather_c", c_blk, True)[:, 0]
    cond = jnp.concatenate([c_all, c_ctx[None], jnp.zeros((COND_ROWS - N_DEV - 1, D_MODEL), F32)], axis=0)
    s_cond, ds_cond = _rowwise("cond_silu", lambda t: (_silu(t), _silu_grad(t)), COND_ROWS, COND_ROWS, [(cond, 0, D_MODEL)], [],
                               [(D_MODEL, F32), (D_MODEL, F32)], [])
    w_mod_b = w_mod.astype(BF16)
    mcols = w_mod.shape[2]
    mod_part = jnp.stack([_mm(f"mod{i}", s_cond, w_mod_b[i], "nn") for i in range(2)])
    mod_g = _exchange("gather_mod", mod_part.reshape(2 * COND_ROWS, mcols), True)
    mod_all = mod_g.reshape(N_CHIP, 2, 2, COND_ROWS, mcols)[:, 0]
    mod_all = mod_all.transpose(1, 2, 0, 3).reshape(2, COND_ROWS, N_CHIP * mcols) + b_mod[:, None, :]
    mods = []
    for i in range(2):
        lat = lax.dynamic_slice_in_dim(mod_all[i], me, 1, axis=0)[0]
        both = jnp.stack([lat, mod_all[i, N_DEV]])
        mods.append((both[:, :D_MODEL], both[:, D_MODEL:2 * D_MODEL], both[:, 2 * D_MODEL:]))

    small_p = {k: wts[k] for k in SMALL if k != "c_ctx" and k != "b_mod"}
    sq, grad_x, big_g, small_g, d_mod_lat, d_mod_ctx = _example_step(x[0], ctx[0], loss_target[0], mods, w_full, small_p)
    loss = lax.psum(0.5 / D_MODEL * sq, ("x", "y", "c"))
    big_g["attn_w_in"] = _attn_in_perm(big_g["attn_w_in"], True)

    small_names = [k for k in SMALL if k not in ("c_ctx", "b_mod")]
    small_list = [small_g[k] for k in small_names] + [d_mod_lat, d_mod_ctx]
    small_shapes = [wts[k].shape for k in small_names] + [d_mod_lat.shape, d_mod_ctx.shape]
    packed = _pack(small_list, F32, 8 * N_DEV)
    slice_rows = packed.shape[0] // N_DEV
    slices = _exchange("scatter_small", packed.reshape(N_DEV, slice_rows, PACK_W), False)
    my_sum = _sum_slots("sum_small", slices)
    payload = jnp.concatenate([my_sum, _pack([d_mod_lat], F32, 8)], axis=0)
    sg = _exchange("gather_small", payload, True)
    summed = _unpack(sg[:, :slice_rows].reshape(packed.shape), small_shapes)
    grads = dict(zip(small_names, summed[:-2]))
    d_mod_lat_sum, d_mod_ctx_sum = summed[-2], summed[-1]
    grads["b_mod"] = d_mod_lat_sum + d_mod_ctx_sum
    d_mod_lat_all = _unpack(sg[:, slice_rows:], [d_mod_lat.shape])[0]

    g_w_mod, ds_cc = [], []
    for i in range(2):
        rows9 = jnp.concatenate([d_mod_lat_all[:, i], d_mod_ctx_sum[i][None],
                                 jnp.zeros((COND_ROWS - N_DEV - 1, 3 * D_MODEL), F32)], axis=0)
        mine = lax.dynamic_slice_in_dim(rows9, chip * mcols, mcols, axis=1)
        g_w_mod.append(_mm(f"mod{i}_dw", s_cond, mine, "tn"))
        ds_cc.append(_mm(f"mod{i}_dx", mine, w_mod_b[i], "nt")[N_DEV])
    grads["w_mod"] = jnp.stack(g_w_mod)
    part = (ds_cc[0] + ds_cc[1]) * jnp.where(mc == 0, 1.0, 0.0)
    part_blk = jnp.concatenate([part[None], jnp.zeros((N_DEV - 1, D_MODEL), F32)], axis=0)
    ds_all = _sum_slots("sum_c_ctx", _exchange("gather_c_ctx", part_blk, True))
    grads["c_ctx"] = ds_all[0] * ds_cond[N_DEV]

    blocks = []
    for k in BIG:
        g = big_g[k]
        rows, cols = g.shape
        if k in COL_SHARDED:
            blocks.append(g.reshape(2, rows // 2, N_CHIP, cols // N_CHIP).transpose(2, 0, 1, 3).reshape(N_DEV, -1))
        else:
            blocks.append(g.reshape(N_DEV, -1))
    sendbuf = jnp.concatenate(blocks, axis=1).astype(BF16)
    sendbuf = sendbuf.reshape(N_DEV, -1, PACK_W)
    recv = _exchange("scatter_big", sendbuf, False)
    mine = _sum_slots("sum_big", recv)
    both = _exchange("swap_halves", mine, True, sibling_only=True)
    half_shapes = [(wts[k].shape[1] // 2, wts[k].shape[2]) for k in BIG]
    for k, pc in zip(BIG, _unpack(both, half_shapes)):
        grads[k] = pc.reshape(wts[k].shape)

    delta, new_m, new_v = {}, {}, {}
    for k in BIG + ("w_mod",):
        shp = wts[k].shape
        two_d = (-1, shp[-1])
        res = _adamw("adamw_" + k, *[a.reshape(two_d) for a in (wts[k], grads[k], mom_m[k], mom_v[k])])
        delta[k], new_m[k], new_v[k] = [r.reshape(shp) for r in res]
    shapes = [wts[k].shape for k in SMALL]
    packed = [_pack([d[k] for k in SMALL], F32, 8) for d in (wts, grads, mom_m, mom_v)]
    res = _adamw("adamw_small", *packed)
    for dst, buf in zip((delta, new_m, new_v), res):
        for k, a in zip(SMALL, _unpack(buf, shapes)):
            dst[k] = a
    grads = {k: grads[k].reshape(wts[k].shape) for k in WEIGHTS}
    return (loss, grad_x[None], *[grads[k] for k in WEIGHTS], *[delta[k] for k in WEIGHTS],
            *[new_m[k] for k in WEIGHTS], *[new_v[k] for k in WEIGHTS])
```

```python
import functools
import math

import jax
import jax.numpy as jnp
from jax import lax
from jax.experimental import pallas as pl
from jax.experimental.pallas import tpu as pltpu

F32 = jnp.float32
BF16 = jnp.bfloat16

D_MODEL = 1024
NORM_EPS = 1e-6
SSM_GROUPS = 64
SSM_GROUP = 16
SSM_STATE = 64
LANES = 128
SLAB_W = LANES
N_SLAB = D_MODEL // SLAB_W
SLAB_GROUPS = SLAB_W // SSM_GROUP
HALF_W = SLAB_GROUPS * SSM_STATE
STATE_W = 2 * HALF_W
N_SEG = 8
HEAD_DIM = 64
N_Q_HEADS = 16
N_KV_HEADS = 4
KV_REP = N_Q_HEADS // N_KV_HEADS
ATTN_W = N_Q_HEADS * HEAD_DIM
KV_W = N_KV_HEADS * HEAD_DIM
GRID_W = 64
ROPE_THETA = 10000.0
N_DEV = 8
N_CHIP = 4
VMEM_LIMIT_BYTES = 56 * 1024 * 1024

ADAM_LR = 0.001
ADAM_B1 = 0.9
ADAM_B2 = 0.999
ADAM_EPS = 1e-08
ADAM_WD = 0.01
ADAM_STEP = 10


def _params(*sem):
    return pltpu.CompilerParams(dimension_semantics=sem, vmem_limit_bytes=VMEM_LIMIT_BYTES)


def _largest_tile(n, cap, unit):
    if n <= cap:
        return n
    t = (cap // unit) * unit
    while t >= unit:
        if n % t == 0:
            return t
        t -= unit
    raise ValueError(f"no tile for {n} (cap {cap}, unit {unit})")


def _rowwise(name, fn, n_rows, tr, row_ins, vec_ins, row_outs, red_outs, n_lat=None, want_flag=False):
    nt = n_rows // tr
    assert nt * tr == n_rows
    nlt = nt if n_lat is None else n_lat // tr

    def sel(i):
        return jnp.where(i >= nlt, 1, 0)

    arrays, in_specs, pairs = [], [], []
    for spec in row_ins:
        arr, cb, w = spec[:3]
        kind = spec[3] if len(spec) > 3 else None
        m = spec[4] if len(spec) > 4 else None
        if kind == "pair":
            arrays += [arr, m]
            in_specs += [pl.BlockSpec((tr, w), functools.partial(lambda i, cb: (jnp.minimum(i, nlt - 1), cb), cb=cb)),
                         pl.BlockSpec((tr, w), functools.partial(lambda i, cb: (jnp.maximum(i - nlt, 0), cb), cb=cb))]
            pairs.append(len(arrays) - 2)
            continue
        if kind == "mod":
            imap = functools.partial(lambda i, cb, m: (i % m, cb), cb=cb, m=m)
        elif kind == "clamp":
            imap = functools.partial(lambda i, cb, m: (jnp.minimum(i, m - 1), cb), cb=cb, m=m)
        else:
            imap = functools.partial(lambda i, cb: (i, cb), cb=cb)
        arrays.append(arr)
        in_specs.append(pl.BlockSpec((tr, w), imap))
    for v in vec_ins:
        s, a, w = v.shape
        imap = (lambda i: (sel(i), 0, 0)) if s == 2 else (lambda i: (0, 0, 0))
        arrays.append(v)
        in_specs.append(pl.BlockSpec((1, a, w), imap))
    out_shapes, out_specs = [], []
    lat_only = [len(spec) > 2 for spec in row_outs]
    for (w, dt), lat in zip([spec[:2] for spec in row_outs], lat_only):
        out_shapes.append(jax.ShapeDtypeStruct((n_lat if lat else n_rows, w), dt))
        out_specs.append(pl.BlockSpec((tr, w), (lambda i: (jnp.minimum(i, nlt - 1), 0)) if lat else (lambda i: (i, 0))))
    for s, w in red_outs:
        out_shapes.append(jax.ShapeDtypeStruct((s, 1, w), F32))
        imap = (lambda i: (sel(i), 0, 0)) if s == 2 else (lambda i: (0, 0, 0))
        out_specs.append(pl.BlockSpec((1, 1, w), imap))
    n_ri, n_vi, n_ro, n_rd = len(row_ins) + len(pairs), len(vec_ins), len(row_outs), len(red_outs)

    def body(*refs):
        i = pl.program_id(0)
        rows, k = [], 0
        while k < n_ri:
            if k in pairs:
                rows.append(jnp.where(i < nlt, refs[k][...], refs[k + 1][...]).astype(F32))
                k += 2
            else:
                rows.append(refs[k][...].astype(F32))
                k += 1
        vecs = [r[0] for r in refs[n_ri:n_ri + n_vi]]
        outs = refs[n_ri + n_vi:]
        lead = [jnp.where(i < nlt, 1.0, 0.0).astype(F32)] if want_flag else []
        res = fn(*lead, *rows, *vecs)
        if not isinstance(res, (tuple, list)):
            res = (res,)
        assert len(res) == n_ro + n_rd
        for k in range(n_ro):
            if lat_only[k]:
                @pl.when(i < nlt)
                def _(k=k):
                    outs[k][...] = res[k].astype(outs[k].dtype)
            else:
                outs[k][...] = res[k].astype(outs[k].dtype)
        for k in range(n_rd):
            part = jnp.sum(res[n_ro + k].astype(F32), axis=0, keepdims=True)
            first = i == 0
            if red_outs[k][0] == 2:
                first = jnp.logical_or(first, i == nlt)
            o = outs[n_ro + k]

            @pl.when(first)
            def _():
                o[0] = part

            @pl.when(jnp.logical_not(first))
            def _():
                o[0] = o[0] + part

    res = pl.pallas_call(
        body, grid=(nt,), in_specs=in_specs, out_specs=out_specs, out_shape=out_shapes,
        compiler_params=_params("arbitrary"), name=name)(*arrays)
    return res


def _vec(v):
    v = v.astype(F32)
    if v.ndim == 1:
        v = v[None]
    return v[:, None, :]


def _mm(name, a, b, mode, out_dtype=F32):
    if mode in ("nn", "nt"):
        m, k = a.shape
        n = b.shape[1] if mode == "nn" else b.shape[0]
        tm = _largest_tile(m, 1024, 8)
        tn = _largest_tile(n, 1024, 128)
        contract = (((1,), (0,)), ((), ())) if mode == "nn" else (((1,), (1,)), ((), ()))

        def body(a_ref, b_ref, o_ref):
            o_ref[...] = lax.dot_general(a_ref[...].astype(BF16), b_ref[...].astype(BF16), contract,
                                         preferred_element_type=F32).astype(o_ref.dtype)

        b_spec = pl.BlockSpec((k, tn), lambda i, j: (0, j)) if mode == "nn" else pl.BlockSpec((tn, k), lambda i, j: (j, 0))
        return pl.pallas_call(
            body, grid=(m // tm, n // tn),
            in_specs=[pl.BlockSpec((tm, k), lambda i, j: (i, 0)), b_spec],
            out_specs=pl.BlockSpec((tm, tn), lambda i, j: (i, j)),
            out_shape=jax.ShapeDtypeStruct((m, n), out_dtype),
            compiler_params=_params("parallel", "arbitrary"), name=name)(a, b)
    assert mode == "tn"
    r, k1 = a.shape
    k2 = b.shape[1]
    tr = _largest_tile(r, 1024, 8)
    t2 = _largest_tile(k2, 1024, 128)
    nr = r // tr

    def body(a_ref, b_ref, o_ref, acc_ref):
        part = lax.dot_general(a_ref[...].astype(BF16), b_ref[...].astype(BF16), (((0,), (0,)), ((), ())),
                               preferred_element_type=F32)
        i = pl.program_id(1)

        @pl.when(i == 0)
        def _():
            acc_ref[...] = part

        @pl.when(i > 0)
        def _():
            acc_ref[...] += part

        @pl.when(i == nr - 1)
        def _():
            o_ref[...] = acc_ref[...].astype(o_ref.dtype)

    return pl.pallas_call(
        body, grid=(k2 // t2, nr),
        in_specs=[pl.BlockSpec((tr, k1), lambda j, i: (i, 0)), pl.BlockSpec((tr, t2), lambda j, i: (i, j))],
        out_specs=pl.BlockSpec((k1, t2), lambda j, i: (0, j)),
        out_shape=jax.ShapeDtypeStruct((k1, k2), out_dtype),
        scratch_shapes=[pltpu.VMEM((k1, t2), F32)],
        compiler_params=_params("parallel", "arbitrary"), name=name)(a, b)


def _exchange(name, x, bcast, sibling_only=False):
    rels = [1] if sibling_only else list(range(1, N_DEV))
    n_slot = 2 if sibling_only else N_DEV
    blk = x.shape if bcast else x.shape[1:]

    def body(x_ref, o_ref, send_sems, recv_sems, local_sem):
        mx, my, mc = lax.axis_index("x"), lax.axis_index("y"), lax.axis_index("c")
        me = mc if sibling_only else 4 * mx + 2 * my + mc
        me_dev = 4 * mx + 2 * my + mc
        mine = pltpu.make_async_copy(x_ref if bcast else x_ref.at[me_dev], o_ref.at[me], local_sem)
        mine.start()
        copies = []
        for k, r in enumerate(rels):
            px = 1 - mx if (r >> 2) & 1 else mx
            py = 1 - my if (r >> 1) & 1 else my
            pc = 1 - mc if r & 1 else mc
            src = x_ref if bcast else x_ref.at[4 * px + 2 * py + pc]
            cp = pltpu.make_async_remote_copy(
                src_ref=src, dst_ref=o_ref.at[me], send_sem=send_sems.at[k], recv_sem=recv_sems.at[k],
                device_id=(px, py, pc), device_id_type=pl.DeviceIdType.MESH)
            cp.start()
            copies.append(cp)
        for cp in copies:
            cp.wait()
        mine.wait()

    return pl.pallas_call(
        body, out_shape=jax.ShapeDtypeStruct((n_slot,) + tuple(blk), x.dtype),
        in_specs=[pl.BlockSpec(memory_space=pltpu.VMEM if sibling_only else pl.ANY)],
        out_specs=pl.BlockSpec(memory_space=pl.ANY),
        scratch_shapes=[pltpu.SemaphoreType.DMA((len(rels),)), pltpu.SemaphoreType.DMA((len(rels),)),
                        pltpu.SemaphoreType.DMA],
        name=name)(x)


def _gather_two_level(name, x):
    def body(x_ref, o_ref, send_sems, recv_sems, local_sem):
        mx, my, mc = lax.axis_index("x"), lax.axis_index("y"), lax.axis_index("c")
        me, sibling = (mx, my, mc), (mx, my, 1 - mc)
        chips = [(1 - mx, my), (mx, 1 - my), (1 - mx, 1 - my)]

        def slot(px, py, pc):
            return o_ref.at[4 * px + 2 * py + pc]

        def copy(k, block, to, src=None):
            return pltpu.make_async_remote_copy(
                src_ref=slot(*block) if src is None else src, dst_ref=slot(*block), send_sem=send_sems.at[k],
                recv_sem=recv_sems.at[k], device_id=to, device_id_type=pl.DeviceIdType.MESH)

        mine = pltpu.make_async_copy(x_ref, slot(*me), local_sem)
        mine.start()
        first = [copy(0, me, sibling, src=x_ref)]
        first += [copy(1 + j, me, (*chip, mc), src=x_ref) for j, chip in enumerate(chips)]
        for cp in first:
            cp.start()
        passed = [copy(4 + j, (*chip, mc), sibling) for j, chip in enumerate(chips)]
        for j, chip in enumerate(chips):
            copy(1 + j, (*chip, mc), me).wait_recv()
            passed[j].start()
        copy(0, sibling, me).wait_recv()
        for j, chip in enumerate(chips):
            copy(4 + j, (*chip, 1 - mc), me).wait_recv()
        for cp in first + passed:
            cp.wait_send()
        mine.wait()

    return pl.pallas_call(
        body, out_shape=jax.ShapeDtypeStruct((N_DEV,) + tuple(x.shape), x.dtype),
        in_specs=[pl.BlockSpec(memory_space=pl.ANY)], out_specs=pl.BlockSpec(memory_space=pl.ANY),
        scratch_shapes=[pltpu.SemaphoreType.DMA((N_DEV - 1,)), pltpu.SemaphoreType.DMA((N_DEV - 1,)),
                        pltpu.SemaphoreType.DMA],
        name=name)(x)


def _sum_slots(name, x):
    s, r, w = x.shape
    tr = _largest_tile(r, 256, 8)

    def body(x_ref, o_ref):
        acc = x_ref[0].astype(F32)
        for j in range(1, s):
            acc = acc + x_ref[j].astype(F32)
        o_ref[...] = acc

    return pl.pallas_call(
        body, grid=(r // tr,), in_specs=[pl.BlockSpec((s, tr, w), lambda i: (0, i, 0))],
        out_specs=pl.BlockSpec((tr, w), lambda i: (i, 0)), out_shape=jax.ShapeDtypeStruct((r, w), F32),
        compiler_params=_params("parallel"), name=name)(x)


def _sigmoid(x):
    return 1.0 / (1.0 + jnp.exp(-x))


def _silu(x):
    return x * _sigmoid(x)


def _silu_grad(x):
    s = _sigmoid(x)
    return s * (1.0 + x * (1.0 - s))


_INV_SQRT2 = 1.0 / math.sqrt(2.0)
_INV_SQRT2PI = 1.0 / math.sqrt(2.0 * math.pi)


def _gelu(x):
    return 0.5 * x * (1.0 + lax.erf(x * _INV_SQRT2))


def _gelu_grad(x):
    return 0.5 * (1.0 + lax.erf(x * _INV_SQRT2)) + x * jnp.exp(-0.5 * x * x) * _INV_SQRT2PI


def _rms_hat(x):
    r = lax.rsqrt(jnp.mean(x * x, axis=-1, keepdims=True) + NORM_EPS)
    return x * r, r


def _rms_bwd(xh, r, dxh):
    return r * (dxh - xh * jnp.mean(dxh * xh, axis=-1, keepdims=True))


def _stream(x):
    return (x[0], 0, D_MODEL, "pair", x[1]) if isinstance(x, tuple) else (x, 0, D_MODEL)


def _norm_mod_fwd(name, x, g, scale, shift, n_rows, tr, n_lat):
    def fn(xt, gv, sc, sh):
        xh, _ = _rms_hat(xt)
        return (xh * gv) * (1.0 + sc) + sh

    return _rowwise(name, fn, n_rows, tr, [_stream(x)], [g, scale, shift], [(D_MODEL, BF16)], [], n_lat=n_lat)[0]


def _norm_mod_bwd(name, x, g, scale, dh, dres, n_rows, tr, n_lat, prev=None, dx_lat_only=False):
    nlt = n_lat // tr

    def fn(flag, xt, dht, drt, *rest):
        gv, sc = rest[-2:] if prev is None else rest[1:3]
        xh, r = _rms_hat(xt)
        n = xh * gv
        dn = dht * (1.0 + sc)
        dx = _rms_bwd(xh, r, dn * gv) + flag * drt
        if prev is None:
            return dx, dn * xh, dht * n, dht
        return dx, rest[3] * dx, dn * xh, dht * n, dht, dx * rest[0]

    rows = [_stream(x), (dh, 0, D_MODEL), (dres, 0, D_MODEL, "clamp", nlt)]
    row_outs = [(D_MODEL, F32, "lat") if dx_lat_only else (D_MODEL, F32)]
    vecs, reds = [g, scale], [(1, D_MODEL), (2, D_MODEL), (2, D_MODEL)]
    if prev is not None:
        rows.append((prev[0], 0, D_MODEL))
        vecs.append(prev[1])
        row_outs.append((D_MODEL, BF16))
        reds.append((2, D_MODEL))
    return _rowwise(name, fn, n_rows, tr, rows, vecs, row_outs, reds, n_lat=n_lat, want_flag=True)


def _s5_prep(a_re, a_im, log_dt, b_re, b_im, seg_lat, seg_ctx):
    def body(ar_ref, ai_ref, ld_ref, br_ref, bi_ref, abr_ref, abi_ref, bbr_ref, bbi_ref, alr_ref, ali_ref, acr_ref,
             aci_ref):
        lr, li = ar_ref[...], ai_ref[...]
        dt = jnp.exp(ld_ref[...])
        ldr, ldi = lr * dt, li * dt
        e = jnp.exp(ldr)
        abr, abi = e * jnp.cos(ldi), e * jnp.sin(ldi)
        abr_ref[...] = abr
        abi_ref[...] = abi
        den = lr * lr + li * li
        nr, ni = abr - 1.0, abi
        qr = (nr * lr + ni * li) / den
        qi = (ni * lr - nr * li) / den
        br, bi = br_ref[...], bi_ref[...]
        bbr_ref[...] = qr[None] * br - qi[None] * bi
        bbi_ref[...] = qr[None] * bi + qi[None] * br
        for seg, r_ref, i_ref in ((seg_lat, alr_ref, ali_ref), (seg_ctx, acr_ref, aci_ref)):
            es = jnp.exp(ldr * float(seg))
            r_ref[...] = es * jnp.cos(ldi * float(seg))
            i_ref[...] = es * jnp.sin(ldi * float(seg))

    sm = jax.ShapeDtypeStruct(a_re.shape, F32)
    big = jax.ShapeDtypeStruct(b_re.shape, F32)
    return pl.pallas_call(body, out_shape=[sm, sm, big, big, sm, sm, sm, sm], name="s5_prep")(
        a_re, a_im, log_dt, b_re, b_im)


def _s5_prep_bwd(a_re, a_im, log_dt, b_re, b_im, dabr, dabi, dbbr, dbbi):
    def body(ar_ref, ai_ref, ld_ref, br_ref, bi_ref, dabr_ref, dabi_ref, dbbr_ref, dbbi_ref,
             dar_ref, dai_ref, dld_ref, dbr_ref, dbi_ref):
        lr, li = ar_ref[...], ai_ref[...]
        dt = jnp.exp(ld_ref[...])
        ldr, ldi = lr * dt, li * dt
        e = jnp.exp(ldr)
        abr, abi = e * jnp.cos(ldi), e * jnp.sin(ldi)
        den = lr * lr + li * li
        nr, ni = abr - 1.0, abi
        qr = (nr * lr + ni * li) / den
        qi = (ni * lr - nr * li) / den
        br, bi = br_ref[...], bi_ref[...]
        gbr, gbi = dbbr_ref[...], dbbi_ref[...]
        dbr_ref[...] = gbr * qr[None] + gbi * qi[None]
        dbi_ref[...] = gbi * qr[None] - gbr * qi[None]
        dqr = jnp.sum(gbr * br + gbi * bi, axis=0)
        dqi = jnp.sum(gbi * br - gbr * bi, axis=0)
        dnr = (dqr * lr - dqi * li) / den
        dni = (dqr * li + dqi * lr) / den
        dlr_q = (dqr * (nr - 2.0 * lr * qr) + dqi * (ni - 2.0 * lr * qi)) / den
        dli_q = (dqr * (ni - 2.0 * li * qr) + dqi * (-nr - 2.0 * li * qi)) / den
        gar = dabr_ref[...] + dnr
        gai = dabi_ref[...] + dni
        dldr = gar * abr + gai * abi
        dldi = gai * abr - gar * abi
        dar_ref[...] = dldr * dt + dlr_q
        dai_ref[...] = dldi * dt + dli_q
        ddt = jnp.sum(dldr * lr + dldi * li, axis=1, keepdims=True)
        dld_ref[...] = ddt * dt

    sm = jax.ShapeDtypeStruct(a_re.shape, F32)
    big = jax.ShapeDtypeStruct(b_re.shape, F32)
    return pl.pallas_call(body, out_shape=[sm, sm, jax.ShapeDtypeStruct(log_dt.shape, F32), big, big],
                          name="s5_prep_bwd")(a_re, a_im, log_dt, b_re, b_im, dabr, dabi, dbbr, dbbi)


def _slab_cols(v):
    return v.reshape(N_SLAB, 1, HALF_W)


def _slab_pair(vr, vi):
    return jnp.concatenate([_slab_cols(vr), _slab_cols(vi)], axis=-1)


def _slab_in_matrix(bbr, bbi):
    eye = jnp.eye(SLAB_GROUPS, dtype=F32)

    def one(b):
        b = b.reshape(N_SLAB, SLAB_GROUPS, SSM_STATE, SSM_GROUP)
        m = jnp.einsum("sgph,gk->sghkp", b, eye)
        return m.reshape(N_SLAB, SLAB_W, HALF_W)

    return jnp.concatenate([one(bbr), one(bbi)], axis=-1)


def _slab_out_matrix(cr, ci):
    eye = jnp.eye(SLAB_GROUPS, dtype=F32)

    def one(c):
        c = c.reshape(N_SLAB, SLAB_GROUPS, SSM_GROUP, SSM_STATE)
        m = jnp.einsum("sghp,gk->skpgh", c, eye)
        return m.reshape(N_SLAB, HALF_W, SLAB_W)

    return jnp.concatenate([one(cr), one(-ci)], axis=1)


def _slab_diag(m):
    m = m.reshape(N_SLAB, SLAB_GROUPS, SSM_GROUP, 2, SLAB_GROUPS, SSM_STATE)
    d = jnp.stack([m[:, g, :, :, g, :] for g in range(SLAB_GROUPS)], axis=1)
    return d.transpose(3, 0, 1, 2, 4).reshape(2, SSM_GROUPS, SSM_GROUP, SSM_STATE)


def _cmul(ar, ai, xr, xi, conj):
    if conj:
        return ar * xr + ai * xi, ar * xi - ai * xr
    return ar * xr - ai * xi, ar * xi + ai * xr


def _s5_pow_table(name, abar, seg, falling, conj):
    assert seg >= 8 and seg & (seg - 1) == 0

    def body(a_ref, o_ref, t_ref):
        ar, ai = a_ref[0, :, :HALF_W], a_ref[0, :, HALF_W:]
        if conj:
            ai = -ai
        rr, ri = [jnp.ones_like(ar)], [jnp.zeros_like(ai)]
        for _ in range(7):
            pr, pi = _cmul(ar, ai, rr[-1], ri[-1], False)
            rr.append(pr)
            ri.append(pi)
        sr, si = _cmul(ar, ai, rr[-1], ri[-1], False)
        if falling:
            rr, ri = rr[::-1], ri[::-1]
        first = slice(seg - 8, seg) if falling else slice(0, 8)
        t_ref[first, :HALF_W] = jnp.concatenate(rr, axis=0)
        t_ref[first, HALF_W:] = jnp.concatenate(ri, axis=0)
        size = 8
        while size < seg:
            src = slice(seg - size, seg) if falling else slice(0, size)
            dst = slice(seg - 2 * size, seg - size) if falling else slice(size, 2 * size)
            pr, pi = _cmul(sr, si, t_ref[src, :HALF_W], t_ref[src, HALF_W:], False)
            t_ref[dst, :HALF_W] = pr
            t_ref[dst, HALF_W:] = pi
            sr, si = _cmul(sr, si, sr, si, False)
            size *= 2
        o_ref[0] = t_ref[...].astype(BF16)

    return pl.pallas_call(
        body, grid=(N_SLAB,), in_specs=[pl.BlockSpec((1, 1, STATE_W), lambda s: (s, 0, 0))],
        out_specs=pl.BlockSpec((1, seg, STATE_W), lambda s: (s, 0, 0)),
        out_shape=jax.ShapeDtypeStruct((N_SLAB, seg, STATE_W), BF16),
        scratch_shapes=[pltpu.VMEM((seg, STATE_W), F32)], compiler_params=_params("parallel"), name=name)(abar)


def _s5_ends(name, x, n_rows, row0, table, m_mat):
    seg = n_rows // N_SEG
    rb = row0 // n_rows
    tn = (((0,), (0,)), ((), ()))

    def body(x_ref, t_ref, m_ref, z_ref):
        mr, mi = m_ref[0, :, :HALF_W], m_ref[0, :, HALF_W:]
        for j in range(N_SEG):
            t = lax.dot_general(x_ref[j * seg:(j + 1) * seg, :].astype(BF16), t_ref[0], tn,
                                preferred_element_type=F32)
            tr_, ti_ = t[:, :HALF_W], t[:, HALF_W:]
            z_ref[0, j:j + 1, :HALF_W] = jnp.sum(mr * tr_ - mi * ti_, axis=0, keepdims=True)
            z_ref[0, j:j + 1, HALF_W:] = jnp.sum(mr * ti_ + mi * tr_, axis=0, keepdims=True)

    return pl.pallas_call(
        body, grid=(N_SLAB,),
        in_specs=[pl.BlockSpec((n_rows, SLAB_W), lambda s: (rb, s)),
                  pl.BlockSpec((1, seg, STATE_W), lambda s: (s, 0, 0)),
                  pl.BlockSpec((1, SLAB_W, STATE_W), lambda s: (s, 0, 0))],
        out_specs=pl.BlockSpec((1, N_SEG, STATE_W), lambda s: (s, 0, 0)),
        out_shape=jax.ShapeDtypeStruct((N_SLAB, N_SEG, STATE_W), F32),
        compiler_params=_params("parallel"), name=name)(x, table, m_mat)


def _s5_carry(name, z, a_seg, init, descending, conj):
    order = list(range(N_SEG - 1, -1, -1)) if descending else list(range(N_SEG))

    def body(z_ref, a_ref, i_ref, e_ref, o_ref):
        ar, ai = a_ref[:, :HALF_W], a_ref[:, HALF_W:]
        cr, ci = i_ref[:, :HALF_W], i_ref[:, HALF_W:]
        for j in order:
            e_ref[:, j, :HALF_W] = cr
            e_ref[:, j, HALF_W:] = ci
            pr, pi = _cmul(ar, ai, cr, ci, conj)
            cr = pr + z_ref[:, j, :HALF_W]
            ci = pi + z_ref[:, j, HALF_W:]
        o_ref[:, :HALF_W] = cr
        o_ref[:, HALF_W:] = ci

    return pl.pallas_call(body, out_shape=[jax.ShapeDtypeStruct(z.shape, F32), jax.ShapeDtypeStruct(init.shape, F32)],
                          name=name)(z, a_seg, init)


def _s5_scan(name, u, n_rows, row0, b_mat, c_mat, abar, h_in, descending, y_alias=None, y_rows=None):
    seg = n_rows // N_SEG
    ta = min(64, seg)
    nk = seg // ta
    assert seg * N_SEG == n_rows and nk * ta == seg and row0 % n_rows == 0 and ta % 8 == 0
    rb = row0 // n_rows
    tile = ta * N_SEG

    def body(*refs):
        u_ref, b_ref, c_ref, a_ref, hin_ref = refs[:5]
        y_ref, hch_ref, st_ref, up_ref, h_ref = refs[-5:]
        k = pl.program_id(1)
        kk = nk - 1 - k if descending else k
        a0 = kk * ta

        @pl.when(k == 0)
        def _():
            st_ref[...] = hin_ref[0]

        hch_ref[0, 0] = st_ref[...]
        for al in range(ta):
            up_ref[al * N_SEG:(al + 1) * N_SEG, :] = u_ref[pl.ds(a0 + al, N_SEG, stride=seg), :]
        h_ref[...] = jnp.dot(up_ref[...].astype(BF16), b_ref[0], preferred_element_type=F32)
        ar = jnp.broadcast_to(a_ref[0, :, :HALF_W], (N_SEG, HALF_W))
        ai = jnp.broadcast_to(a_ref[0, :, HALF_W:], (N_SEG, HALF_W))

        def step(i, carry):
            hr, hi = carry
            al = ta - 1 - i if descending else i
            row = pl.multiple_of(al * N_SEG, N_SEG)
            pr, pi = _cmul(ar, ai, hr, hi, False)
            hr = pr + h_ref[pl.ds(row, N_SEG), :HALF_W]
            hi = pi + h_ref[pl.ds(row, N_SEG), HALF_W:]
            h_ref[pl.ds(row, N_SEG), :HALF_W] = hr
            h_ref[pl.ds(row, N_SEG), HALF_W:] = hi
            return hr, hi

        hr, hi = lax.fori_loop(0, ta, step, (st_ref[:, :HALF_W], st_ref[:, HALF_W:]), unroll=True)
        st_ref[:, :HALF_W] = hr
        st_ref[:, HALF_W:] = hi
        yt = jnp.dot(h_ref[...].astype(BF16), c_ref[0], preferred_element_type=F32)
        for al in range(ta):
            y_ref[pl.ds(a0 + al, N_SEG, stride=seg), :] = yt[al * N_SEG:(al + 1) * N_SEG, :]

    u_spec = pl.BlockSpec((n_rows, SLAB_W), lambda s, k: (rb, s))
    b_spec = pl.BlockSpec((1, SLAB_W, STATE_W), lambda s, k: (s, 0, 0))
    c_spec = pl.BlockSpec((1, STATE_W, SLAB_W), lambda s, k: (s, 0, 0))
    a_spec = pl.BlockSpec((1, 1, STATE_W), lambda s, k: (s, 0, 0))
    st_spec = pl.BlockSpec((1, N_SEG, STATE_W), lambda s, k: (s, 0, 0))
    scratch = [pltpu.VMEM((N_SEG, STATE_W), F32), pltpu.VMEM((tile, SLAB_W), F32), pltpu.VMEM((tile, STATE_W), F32)]
    kmap = (lambda s, k: (s, nk - 1 - k, 0, 0)) if descending else (lambda s, k: (s, k, 0, 0))
    out_specs = [u_spec, pl.BlockSpec((1, 1, N_SEG, STATE_W), kmap)]
    out_shape = [jax.ShapeDtypeStruct((y_rows, D_MODEL), F32), jax.ShapeDtypeStruct((N_SLAB, nk, N_SEG, STATE_W), F32)]
    in_specs = [u_spec, b_spec, c_spec, a_spec, st_spec]
    args = [u, b_mat, c_mat, abar, h_in]
    aliases = {}
    if y_alias is not None:
        in_specs.append(pl.BlockSpec(memory_space=pl.ANY))
        args.append(y_alias)
        aliases = {5: 0}
    return pl.pallas_call(
        body, grid=(N_SLAB, nk), in_specs=in_specs, out_specs=out_specs, out_shape=out_shape, scratch_shapes=scratch,
        input_output_aliases=aliases, compiler_params=_params("parallel", "arbitrary"), name=name)(*args)


def _s5_scan_bwd(name, u, dy, n_rows, row0, b_mat, bt_mat, ct_mat, abar, h_chunks, g_in, descending,
                 du_alias=None, du_rows=None):
    seg = n_rows // N_SEG
    ta = min(64, seg)
    nk = seg // ta
    rb = row0 // n_rows
    tile = ta * N_SEG
    g_desc = not descending

    def body(*refs):
        u_ref, dy_ref, b_ref, bt_ref, ct_ref, a_ref, hch_ref, gin_ref = refs[:8]
        du_ref, db_ref, dc_ref, da_ref, st_ref, up_ref, dyp_ref, h_ref, g_ref = refs[-9:]
        k = pl.program_id(1)
        kk = nk - 1 - k if g_desc else k
        a0 = kk * ta
        ar = jnp.broadcast_to(a_ref[0, :, :HALF_W], (N_SEG, HALF_W))
        ai = jnp.broadcast_to(a_ref[0, :, HALF_W:], (N_SEG, HALF_W))

        @pl.when(k == 0)
        def _():
            st_ref[...] = gin_ref[0]

        for al in range(ta):
            dyp_ref[al * N_SEG:(al + 1) * N_SEG, :] = dy_ref[pl.ds(a0 + al, N_SEG, stride=seg), :]
            up_ref[al * N_SEG:(al + 1) * N_SEG, :] = u_ref[pl.ds(a0 + al, N_SEG, stride=seg), :]
        g_ref[...] = jnp.dot(dyp_ref[...].astype(BF16), ct_ref[0], preferred_element_type=F32)
        h_ref[...] = jnp.dot(up_ref[...].astype(BF16), b_ref[0], preferred_element_type=F32)
        h0r, h0i = hch_ref[0, 0, :, :HALF_W], hch_ref[0, 0, :, HALF_W:]

        def hstep(i, carry):
            hr, hi = carry
            al = ta - 1 - i if descending else i
            row = pl.multiple_of(al * N_SEG, N_SEG)
            pr, pi = _cmul(ar, ai, hr, hi, False)
            hr = pr + h_ref[pl.ds(row, N_SEG), :HALF_W]
            hi = pi + h_ref[pl.ds(row, N_SEG), HALF_W:]
            h_ref[pl.ds(row, N_SEG), :HALF_W] = hr
            h_ref[pl.ds(row, N_SEG), HALF_W:] = hi
            return hr, hi

        lax.fori_loop(0, ta, hstep, (h0r, h0i), unroll=True)

        def gstep(i, carry):
            gr, gi = carry
            al = ta - 1 - i if g_desc else i
            row = pl.multiple_of(al * N_SEG, N_SEG)
            pr, pi = _cmul(ar, ai, gr, gi, True)
            gr = pr + g_ref[pl.ds(row, N_SEG), :HALF_W]
            gi = pi + g_ref[pl.ds(row, N_SEG), HALF_W:]
            g_ref[pl.ds(row, N_SEG), :HALF_W] = gr
            g_ref[pl.ds(row, N_SEG), HALF_W:] = gi
            return gr, gi

        gr, gi = lax.fori_loop(0, ta, gstep, (st_ref[:, :HALF_W], st_ref[:, HALF_W:]), unroll=True)
        st_ref[:, :HALF_W] = gr
        st_ref[:, HALF_W:] = gi

        gb = g_ref[...].astype(BF16)
        dut = jnp.dot(gb, bt_ref[0], preferred_element_type=F32)
        for al in range(ta):
            du_ref[pl.ds(a0 + al, N_SEG, stride=seg), :] = dut[al * N_SEG:(al + 1) * N_SEG, :]
        tn = (((0,), (0,)), ((), ()))
        dbp = lax.dot_general(up_ref[...].astype(BF16), gb, tn, preferred_element_type=F32)
        dcp = lax.dot_general(dyp_ref[...].astype(BF16), h_ref[...].astype(BF16), tn, preferred_element_type=F32)
        inner = (ta - 1) * N_SEG
        if descending:
            g_in_r, g_in_i = g_ref[0:inner, :HALF_W], g_ref[0:inner, HALF_W:]
            p_in_r, p_in_i = h_ref[N_SEG:tile, :HALF_W], h_ref[N_SEG:tile, HALF_W:]
            g_ed_r, g_ed_i = g_ref[inner:tile, :HALF_W], g_ref[inner:tile, HALF_W:]
        else:
            g_in_r, g_in_i = g_ref[N_SEG:tile, :HALF_W], g_ref[N_SEG:tile, HALF_W:]
            p_in_r, p_in_i = h_ref[0:inner, :HALF_W], h_ref[0:inner, HALF_W:]
            g_ed_r, g_ed_i = g_ref[0:N_SEG, :HALF_W], g_ref[0:N_SEG, HALF_W:]
        dar = g_ed_r * h0r + g_ed_i * h0i
        dai = g_ed_i * h0r - g_ed_r * h0i
        if ta > 1:
            dar = dar + jnp.sum((g_in_r * p_in_r + g_in_i * p_in_i).reshape(ta - 1, N_SEG, HALF_W), axis=0)
            dai = dai + jnp.sum((g_in_i * p_in_r - g_in_r * p_in_i).reshape(ta - 1, N_SEG, HALF_W), axis=0)

        @pl.when(k == 0)
        def _():
            db_ref[0] = dbp
            dc_ref[0] = dcp
            da_ref[0, :, :HALF_W] = dar
            da_ref[0, :, HALF_W:] = dai

        @pl.when(k > 0)
        def _():
            db_ref[0] += dbp
            dc_ref[0] += dcp
            da_ref[0, :, :HALF_W] += dar
            da_ref[0, :, HALF_W:] += dai

    u_spec = pl.BlockSpec((n_rows, SLAB_W), lambda s, k: (rb, s))
    m_spec = pl.BlockSpec((1, SLAB_W, STATE_W), lambda s, k: (s, 0, 0))
    mt_spec = pl.BlockSpec((1, STATE_W, SLAB_W), lambda s, k: (s, 0, 0))
    a_spec = pl.BlockSpec((1, 1, STATE_W), lambda s, k: (s, 0, 0))
    st_spec = pl.BlockSpec((1, N_SEG, STATE_W), lambda s, k: (s, 0, 0))
    st_shape = jax.ShapeDtypeStruct((N_SLAB, N_SEG, STATE_W), F32)
    kmap = (lambda s, k: (s, nk - 1 - k, 0, 0)) if g_desc else (lambda s, k: (s, k, 0, 0))
    in_specs = [u_spec, u_spec, m_spec, mt_spec, m_spec, a_spec, pl.BlockSpec((1, 1, N_SEG, STATE_W), kmap), st_spec]
    args = [u, dy, b_mat, bt_mat, ct_mat, abar, h_chunks, g_in]
    aliases = {}
    if du_alias is not None:
        in_specs.append(pl.BlockSpec(memory_space=pl.ANY))
        args.append(du_alias)
        aliases = {8: 0}
    acc_shape = jax.ShapeDtypeStruct((N_SLAB, SLAB_W, STATE_W), F32)
    out_specs = [u_spec, m_spec, m_spec, st_spec]
    out_shape = [jax.ShapeDtypeStruct((du_rows, D_MODEL), F32), acc_shape, acc_shape, st_shape]
    scratch = [pltpu.VMEM((N_SEG, STATE_W), F32), pltpu.VMEM((tile, SLAB_W), F32), pltpu.VMEM((tile, SLAB_W), F32),
               pltpu.VMEM((tile, STATE_W), F32), pltpu.VMEM((tile, STATE_W), F32)]
    return pl.pallas_call(
        body, grid=(N_SLAB, nk), in_specs=in_specs, out_specs=out_specs, out_shape=out_shape, scratch_shapes=scratch,
        input_output_aliases=aliases, compiler_params=_params("parallel", "arbitrary"), name=name)(*args)


ROPE_HALF = HEAD_DIM // 4
TABLE_W = 2 * HEAD_DIM
Q_SCALE = 1.0 / math.sqrt(HEAD_DIM)
HEADS_PER_BLOCK = 2 * KV_REP
Q_BLOCK_W = HEADS_PER_BLOCK * HEAD_DIM


def _rope_tables(n_lat, n_ctx):
    rows = n_lat // GRID_W
    freqs = ROPE_THETA ** (-jnp.arange(ROPE_HALF, dtype=F32) / ROPE_HALF)
    ang_r = jnp.arange(rows, dtype=F32)[:, None] * freqs[None]
    ang_c = jnp.arange(GRID_W, dtype=F32)[:, None] * freqs[None]
    by_row = lambda v: jnp.repeat(v, GRID_W, axis=0)
    by_col = lambda v: jnp.tile(v, (rows, 1))
    cos = jnp.concatenate([by_row(jnp.cos(ang_r)), by_row(jnp.cos(ang_r)), by_col(jnp.cos(ang_c)), by_col(jnp.cos(ang_c))] * 2,
                          axis=1)
    sin = jnp.concatenate([by_row(jnp.sin(ang_r)), by_row(jnp.sin(ang_r)), by_col(jnp.sin(ang_c)), by_col(jnp.sin(ang_c))] * 2,
                          axis=1)
    cos = jnp.concatenate([cos, jnp.ones((n_ctx, TABLE_W), F32)], axis=0)
    sin = jnp.concatenate([sin, jnp.zeros((n_ctx, TABLE_W), F32)], axis=0)
    return cos, sin


def _rot_half(v):
    w = v.shape[1]
    ahead = pltpu.roll(v, w - ROPE_HALF, axis=1)
    behind = pltpu.roll(v, ROPE_HALF, axis=1)
    lane = lax.broadcasted_iota(jnp.int32, v.shape, 1)
    return jnp.where((lane % (2 * ROPE_HALF)) < ROPE_HALF, -ahead, behind)


def _head_mean(v, sel, selt):
    m = jnp.dot(v, sel, precision=lax.Precision.HIGH, preferred_element_type=F32) * (1.0 / HEAD_DIM)
    return jnp.dot(m, selt, precision=lax.Precision.HIGH, preferred_element_type=F32)


def _head_selectors(n_heads):
    sel = jnp.repeat(jnp.eye(n_heads, dtype=F32), HEAD_DIM, axis=0)
    return sel[None], sel.T[None]


def _head_norm(x, sel, selt):
    r = lax.rsqrt(_head_mean(x * x, sel, selt) + NORM_EPS)
    return x * r, r


def _qk_prep(proj, qn, kn, cos, sin, n, tr):
    qw, kw = _vec(jnp.tile(qn, N_Q_HEADS)), _vec(jnp.tile(kn, N_KV_HEADS))
    sq, sqt = _head_selectors(N_Q_HEADS)
    sk, skt = _head_selectors(N_KV_HEADS)

    def fn(qr, kvr, ct, st, qwv, kwv, s16, s16t, s4, s4t):
        outs = []
        for x, wv, sel, selt, scale in ((qr, qwv, s16, s16t, Q_SCALE), (kvr[:, :KV_W], kwv, s4, s4t, 1.0)):
            reps = x.shape[1] // TABLE_W
            cw, sw = jnp.tile(ct, (1, reps)), jnp.tile(st, (1, reps))
            xh, _ = _head_norm(x, sel, selt)
            nrm = xh * wv
            outs.append((nrm * cw + _rot_half(nrm) * sw) * scale)
        return outs[0], outs[1], kvr[:, KV_W:]

    return _rowwise("l1_qk_prep", fn, n, tr,
                    [(proj, 0, ATTN_W), (proj, 2 * ATTN_W // (2 * KV_W), 2 * KV_W), (cos, 0, TABLE_W), (sin, 0, TABLE_W)],
                    [qw, kw, sq, sqt, sk, skt], [(ATTN_W, BF16), (KV_W, BF16), (KV_W, BF16)], [])


def _qk_prep_bwd(proj, qn, kn, cos, sin, dq, dz, dk, dv, n, n_lat, tr):
    qw, kw = _vec(jnp.tile(qn, N_Q_HEADS)), _vec(jnp.tile(kn, N_KV_HEADS))
    sq, sqt = _head_selectors(N_Q_HEADS)
    sk, skt = _head_selectors(N_KV_HEADS)
    nlt = n_lat // tr

    def fn(flag, qr, kvr, ct, st, dqt, dzt, dkt, dvt, qwv, kwv, s16, s16t, s4, s4t):
        dxs, dws = [], []
        for x, dy, wv, sel, selt in ((qr, dqt * (flag * Q_SCALE), qwv, s16, s16t), (kvr[:, :KV_W], dkt, kwv, s4, s4t)):
            reps = x.shape[1] // TABLE_W
            cw, sw = jnp.tile(ct, (1, reps)), jnp.tile(st, (1, reps))
            xh, r = _head_norm(x, sel, selt)
            dn = dy * cw - _rot_half(dy * sw)
            dxh = dn * wv
            dxs.append(r * (dxh - xh * _head_mean(dxh * xh, sel, selt)))
            dws.append(dn * xh)
        return jnp.concatenate([dxs[0], dzt * flag, dxs[1], dvt], axis=1), dws[0], dws[1]

    dproj, dqw, dkw = _rowwise(
        "l1_qk_prep_bwd", fn, n, tr,
        [(proj, 0, ATTN_W), (proj, 2 * ATTN_W // (2 * KV_W), 2 * KV_W), (cos, 0, TABLE_W), (sin, 0, TABLE_W),
         (dq, 0, ATTN_W, "clamp", nlt), (dz, 0, ATTN_W, "clamp", nlt), (dk, 0, KV_W), (dv, 0, KV_W)],
        [qw, kw, sq, sqt, sk, skt], [(2 * ATTN_W + 2 * KV_W, BF16)], [(1, ATTN_W), (1, KV_W)], n_lat=n_lat, want_flag=True)
    return dproj, dqw.reshape(N_Q_HEADS, HEAD_DIM).sum(0)[None], dkw.reshape(N_KV_HEADS, HEAD_DIM).sum(0)[None]


NT = (((1,), (1,)), ((), ()))


def _attn_fwd(q, k, v, proj, t, tq, tk):
    n = k.shape[0]
    nkc = n // tk

    ts = _largest_tile(tq, 256, LANES)
    items = [(sub, j) for sub in range(tq // ts) for j in range(HEADS_PER_BLOCK)]

    def body(q_ref, k_ref, v_ref, z_ref, o_ref, lse_ref, gz_ref, s_ref, m_ref, acc_ref):
        def lanes(j):
            g = j // KV_REP
            return slice(j * HEAD_DIM, (j + 1) * HEAD_DIM), slice(g * HEAD_DIM, (g + 1) * HEAD_DIM)

        for idx in range(len(items) + 1):
            nxt = items[idx] if idx < len(items) else None
            cur = items[idx - 1] if idx > 0 else None
            sn, sc = idx % 2, (idx - 1) % 2
            if nxt is not None:
                rows_n = slice(nxt[0] * ts, (nxt[0] + 1) * ts)
                ql_n, kl_n = lanes(nxt[1])
                qv = q_ref[rows_n, ql_n]
                m_ref[sn] = jnp.full((ts, LANES), -jnp.inf, F32)
            if cur is not None:
                rows_c = slice(cur[0] * ts, (cur[0] + 1) * ts)
                ql_c, kl_c = lanes(cur[1])
                m_row = jnp.max(m_ref[sc], axis=-1, keepdims=True)
                acc_ref[...] = jnp.zeros(acc_ref.shape, F32)

            def sweep(kc, c):
                off = pl.multiple_of(kc * tk, tk)
                if nxt is not None:
                    s = lax.dot_general(qv, k_ref[pl.ds(off, tk), kl_n], NT, preferred_element_type=F32)
                    s_ref[sn, :, pl.ds(off, tk)] = s
                    m = m_ref[sn]
                    for cb in range(tk // LANES):
                        m = jnp.maximum(m, s[:, cb * LANES:(cb + 1) * LANES])
                    m_ref[sn] = m
                if cur is not None:
                    p = jnp.exp(s_ref[sc, :, pl.ds(off, tk)] - m_row)
                    v_one = jnp.concatenate([v_ref[pl.ds(off, tk), kl_c], jnp.ones((tk, HEAD_DIM), BF16)], axis=1)
                    acc_ref[...] += jnp.dot(p.astype(BF16), v_one, preferred_element_type=F32)
                return c

            lax.fori_loop(0, nkc, sweep, 0, unroll=True)
            if cur is not None:
                l_row = acc_ref[:, HEAD_DIM:HEAD_DIM + 1]
                o_head = acc_ref[:, :HEAD_DIM] / l_row
                o_ref[rows_c, ql_c] = o_head
                gz_ref[rows_c, ql_c] = (o_head * _silu(z_ref[rows_c, ql_c].astype(F32))).astype(BF16)
                lse_ref[0, rows_c, cur[1]:cur[1] + 1] = m_row + jnp.log(l_row)

    nb = ATTN_W // Q_BLOCK_W
    kspec = pl.BlockSpec((n, LANES), lambda b, i: (0, b))
    return pl.pallas_call(
        body, grid=(nb, t // tq),
        in_specs=[pl.BlockSpec((tq, Q_BLOCK_W), lambda b, i: (i, b)), kspec, kspec,
                  pl.BlockSpec((tq, Q_BLOCK_W), lambda b, i: (i, nb + b))],
        out_specs=[pl.BlockSpec((tq, Q_BLOCK_W), lambda b, i: (i, b)),
                   pl.BlockSpec((1, tq, HEADS_PER_BLOCK), lambda b, i: (b, i, 0)),
                   pl.BlockSpec((tq, Q_BLOCK_W), lambda b, i: (i, b))],
        out_shape=[jax.ShapeDtypeStruct((t, ATTN_W), F32), jax.ShapeDtypeStruct((nb, t, HEADS_PER_BLOCK), F32),
                   jax.ShapeDtypeStruct((t, ATTN_W), BF16)],
        scratch_shapes=[pltpu.VMEM((2, ts, n), F32), pltpu.VMEM((2, ts, LANES), F32), pltpu.VMEM((ts, 2 * HEAD_DIM), F32)],
        compiler_params=_params("parallel", "parallel"), name="attn_fwd")(q, k, v, proj)


def _attn_bwd(q, k, v, dgz, proj, o, lse, t, tq, tk):
    n = k.shape[0]
    nkc = n // tk
    tn = (((0,), (0,)), ((), ()))

    def body(q_ref, k_ref, v_ref, dgz_ref, z_ref, o_ref, lse_ref, dq_ref, dk_ref, dv_ref, dz_ref, acc_ref):
        @pl.when(pl.program_id(1) == 0)
        def _():
            dk_ref[...] = jnp.zeros(dk_ref.shape, F32)
            dv_ref[...] = jnp.zeros(dv_ref.shape, F32)

        for j0 in range(0, HEADS_PER_BLOCK, 2):
            kl = slice((j0 // KV_REP) * HEAD_DIM, (j0 // KV_REP + 1) * HEAD_DIM)
            heads = []
            for a in range(2):
                j = j0 + a
                ql = slice(j * HEAD_DIM, (j + 1) * HEAD_DIM)
                qv, ov = q_ref[:, ql], o_ref[:, ql]
                zv, dgv = z_ref[:, ql].astype(F32), dgz_ref[:, ql].astype(F32)
                dov = (dgv * _silu(zv)).astype(BF16)
                dz_ref[:, ql] = (dgv * ov * _silu_grad(zv)).astype(dz_ref.dtype)
                dl_v = jnp.sum(dov.astype(F32) * ov, axis=-1, keepdims=True)
                heads.append((ql, qv, dov, dl_v, lse_ref[0, :, j:j + 1]))
                acc_ref[a] = jnp.zeros((tq, HEAD_DIM), F32)

            def step(kc, c):
                off = pl.multiple_of(kc * tk, tk)
                kt = k_ref[pl.ds(off, tk), kl]
                vt = v_ref[pl.ds(off, tk), kl]
                dv_part, dk_part = None, None
                for a, (_, qv, dov, dl_v, lse_v) in enumerate(heads):
                    s = lax.dot_general(qv, kt, NT, preferred_element_type=F32)
                    p = jnp.exp(s - lse_v)
                    dp = lax.dot_general(dov, vt, NT, preferred_element_type=F32)
                    ds = (p * (dp - dl_v)).astype(BF16)
                    acc_ref[a] += jnp.dot(ds, kt, preferred_element_type=F32)
                    dvp = lax.dot_general(p.astype(BF16), dov, tn, preferred_element_type=F32)
                    dkp = lax.dot_general(ds, qv, tn, preferred_element_type=F32)
                    dv_part = dvp if dv_part is None else dv_part + dvp
                    dk_part = dkp if dk_part is None else dk_part + dkp
                dv_ref[pl.ds(off, tk), kl] += dv_part
                dk_ref[pl.ds(off, tk), kl] += dk_part
                return c

            lax.fori_loop(0, nkc, step, 0, unroll=2)
            for a, h in enumerate(heads):
                dq_ref[:, h[0]] = acc_ref[a].astype(dq_ref.dtype)

    nb = ATTN_W // Q_BLOCK_W
    qspec = pl.BlockSpec((tq, Q_BLOCK_W), lambda b, i: (i, b))
    kspec = pl.BlockSpec((n, LANES), lambda b, i: (0, b))
    cspec = pl.BlockSpec((1, tq, HEADS_PER_BLOCK), lambda b, i: (b, i, 0))
    return pl.pallas_call(
        body, grid=(nb, t // tq),
        in_specs=[qspec, kspec, kspec, qspec, pl.BlockSpec((tq, Q_BLOCK_W), lambda b, i: (i, nb + b)), qspec, cspec],
        out_specs=[qspec, kspec, kspec, qspec],
        out_shape=[jax.ShapeDtypeStruct((t, ATTN_W), BF16), jax.ShapeDtypeStruct((n, KV_W), F32),
                   jax.ShapeDtypeStruct((n, KV_W), F32), jax.ShapeDtypeStruct((t, ATTN_W), BF16)],
        scratch_shapes=[pltpu.VMEM((2, tq, HEAD_DIM), F32)],
        compiler_params=_params("parallel", "arbitrary"), name="attn_bwd")(q, k, v, dgz, proj, o, lse)


def _s5_system(p, n_lat, n_ctx):
    two_g = 2 * SSM_GROUPS
    a_re = p["ssm_a_re"].reshape(two_g, SSM_STATE)
    a_im = p["ssm_a_im"].reshape(two_g, SSM_STATE)
    log_dt = p["ssm_log_dt"].reshape(two_g, 1)
    b_re = p["ssm_b_re"].reshape(two_g, SSM_STATE, SSM_GROUP).transpose(2, 0, 1)
    b_im = p["ssm_b_im"].reshape(two_g, SSM_STATE, SSM_GROUP).transpose(2, 0, 1)
    raw = (a_re, a_im, log_dt, b_re, b_im)
    abr, abi, bbr, bbi, alr, ali, acr, aci = _s5_prep(*raw, n_lat // N_SEG, n_ctx // N_SEG)
    dirs = []
    for d in range(2):
        g = slice(d * SSM_GROUPS, (d + 1) * SSM_GROUPS)
        b_mat = _slab_in_matrix(bbr[:, g].transpose(1, 2, 0), bbi[:, g].transpose(1, 2, 0))
        c_mat = _slab_out_matrix(p["ssm_c_re"][0, d], p["ssm_c_im"][0, d])
        abar = _slab_pair(abr[g], abi[g])
        tables = {}
        for part, seg in (("lat", n_lat // N_SEG), ("ctx", n_ctx // N_SEG)):
            tables["h_" + part] = _s5_pow_table(f"s5_pow_h{d}_{part}", abar, seg, d == 0, False)
            tables["g_" + part] = _s5_pow_table(f"s5_pow_g{d}_{part}", abar, seg, d == 1, True)
        dirs.append(dict(
            b=b_mat.astype(BF16), bt=b_mat.transpose(0, 2, 1).astype(BF16), b32=b_mat,
            c=c_mat.astype(BF16), ct=c_mat.transpose(0, 2, 1).astype(BF16), ct32=c_mat.transpose(0, 2, 1),
            abar=abar, a_lat=_slab_pair(alr[g], ali[g])[:, 0], a_ctx=_slab_pair(acr[g], aci[g])[:, 0], **tables))
    return raw, dirs


def _s5_forward(proj, dirs, n_lat, n_ctx):
    n = n_lat + n_ctx
    zero_c = jnp.zeros((N_SLAB, STATE_W), F32)
    ys, saved = [], []
    for d, s in enumerate(dirs):
        desc = d == 1
        tag = f"s5f{d}"
        zc = _s5_ends(tag + "_ctx_ends", proj, n_ctx, n_lat, s["h_ctx"], s["b32"])
        ent_c, h0 = _s5_carry(tag + "_ctx_carry", zc, s["a_ctx"], zero_c, desc, False)
        y, hch_c = _s5_scan(tag + "_ctx", proj, n_ctx, n_lat, s["b"], s["c"], s["abar"], ent_c, desc,
                            y_rows=n)
        zl = _s5_ends(tag + "_lat_ends", proj, n_lat, 0, s["h_lat"], s["b32"])
        ent_l, _ = _s5_carry(tag + "_lat_carry", zl, s["a_lat"], h0, desc, False)
        y, hch_l = _s5_scan(tag + "_lat", proj, n_lat, 0, s["b"], s["c"], s["abar"], ent_l, desc,
                            y_alias=y, y_rows=n)
        ys.append(y)
        saved.append((hch_l, hch_c))
    return ys, saved


def _s5_backward(proj, dy, dirs, saved, n_lat, n_ctx):
    n = n_lat + n_ctx
    zero_c = jnp.zeros((N_SLAB, STATE_W), F32)
    out = []
    for d, s in enumerate(dirs):
        desc = d == 1
        tag = f"s5b{d}"
        hch_l, hch_c = saved[d]
        gl = _s5_ends(tag + "_lat_ends", dy, n_lat, 0, s["g_lat"], s["ct32"])
        ent_l, g0 = _s5_carry(tag + "_lat_carry", gl, s["a_lat"], zero_c, not desc, True)
        du, db_l, dc_l, da_l = _s5_scan_bwd(tag + "_lat", proj, dy, n_lat, 0, s["b"], s["bt"], s["ct"], s["abar"],
                                            hch_l, ent_l, desc, du_rows=n)
        gc = _s5_ends(tag + "_ctx_ends", dy, n_ctx, n_lat, s["g_ctx"], s["ct32"])
        ent_c, _ = _s5_carry(tag + "_ctx_carry", gc, s["a_ctx"], g0, not desc, True)
        du, db_c, dc_c, da_c = _s5_scan_bwd(tag + "_ctx", proj, dy, n_ctx, n_lat, s["b"], s["bt"], s["ct"], s["abar"],
                                            hch_c, ent_c, desc, du_alias=du, du_rows=n)
        out.append((du, db_l + db_c, dc_l + dc_c, da_l + da_c))
    return out


def _s5_param_grads(raw, bwd):
    dabr, dabi, dbbr, dbbi, dcr, dci = [], [], [], [], [], []
    for _, db, dc, da in bwd:
        da = jnp.sum(da, axis=1)
        dabr.append(da[:, :HALF_W].reshape(SSM_GROUPS, SSM_STATE))
        dabi.append(da[:, HALF_W:].reshape(SSM_GROUPS, SSM_STATE))
        dbd = _slab_diag(db)
        dbbr.append(dbd[0].transpose(1, 0, 2))
        dbbi.append(dbd[1].transpose(1, 0, 2))
        dcd = _slab_diag(dc)
        dcr.append(dcd[0])
        dci.append(-dcd[1])
    cat = lambda xs, ax: jnp.concatenate(xs, axis=ax)
    dar, dai, dld, dbr, dbi = _s5_prep_bwd(*raw, cat(dabr, 0), cat(dabi, 0), cat(dbbr, 1), cat(dbbi, 1))
    shp = (1, 2, SSM_GROUPS, SSM_STATE)
    b_shape = (1, 2, SSM_GROUPS, SSM_STATE, SSM_GROUP)
    return dict(
        ssm_a_re=dar.reshape(shp), ssm_a_im=dai.reshape(shp), ssm_log_dt=dld.reshape(1, 2, SSM_GROUPS),
        ssm_b_re=dbr.transpose(1, 2, 0).reshape(b_shape), ssm_b_im=dbi.transpose(1, 2, 0).reshape(b_shape),
        ssm_c_re=jnp.stack(dcr)[None], ssm_c_im=jnp.stack(dci)[None])


def _example_step(x, ctx, target, mods, w, p):
    t, c = x.shape[0], ctx.shape[0]
    n = t + c
    assert t % c == 0 and c % LANES == 0 and c % (8 * N_SEG) == 0 and t % GRID_W == 0
    tr = _largest_tile(c, 256, 8)
    xall = (x, ctx)
    g0, g1 = _vec(p["norm_g"][0]), _vec(p["norm_g"][1])
    (shift0, scale0, gate0), (shift1, scale1, gate1) = [tuple(_vec(v) for v in m) for m in mods]

    h0 = _norm_mod_fwd("l0_norm", xall, g0, scale0, shift0, n, tr, t)
    proj0 = _mm("l0_in", h0, w["ssm_w_in"], "nn")
    raw, dirs = _s5_system(p, t, c)
    (y_f, y_r), saved = _s5_forward(proj0, dirs, t, c)
    d_skip = _vec(p["ssm_d"][0])

    def post_a(u, yf, yr, dv):
        y = u * dv + yf + yr
        return y, _gelu(y)

    y0, yg = _rowwise("l0_gelu", post_a, n, tr, [(proj0, 0, D_MODEL), (y_f, 0, D_MODEL), (y_r, 0, D_MODEL)], [d_skip],
                      [(D_MODEL, BF16), (D_MODEL, BF16)], [])
    tg = _mm("l0_glu", yg, w["ssm_w_glu"], "nn", out_dtype=BF16)
    b_glu = _vec(p["ssm_b_glu"][0])

    def post_b(ygt, tt, zt, bv):
        return ygt * _sigmoid(tt + bv) * _silu(zt)

    gz0 = _rowwise("l0_gate", post_b, n, tr, [(yg, 0, D_MODEL), (tg, 0, D_MODEL), (proj0, 1, D_MODEL)], [b_glu],
                   [(D_MODEL, BF16)], [])[0]
    out0 = _mm("l0_out", gz0, w["ssm_w_out"], "nn")

    def res_norm(xt, ot, gv, g1v, sc, sh):
        x1t = xt + gv * ot
        xh, _ = _rms_hat(x1t)
        return x1t, (xh * g1v) * (1.0 + sc) + sh

    x1, h1 = _rowwise("l0_res_l1_norm", res_norm, n, tr, [_stream(xall), (out0, 0, D_MODEL)],
                      [gate0, g1, scale1, shift1], [(D_MODEL, F32), (D_MODEL, BF16)], [], n_lat=t)
    proj1 = _mm("l1_in", h1, w["attn_w_in"], "nn", out_dtype=BF16)
    cos, sin = _rope_tables(t, c)
    qn, kn = p["attn_q_norm"][0], p["attn_k_norm"][0]
    q_h, k_h, v_h = _qk_prep(proj1, qn, kn, cos, sin, n, tr)
    tq = _largest_tile(t, 512, LANES)
    o, lse, gz1 = _attn_fwd(q_h, k_h, v_h, proj1, t, tq, _largest_tile(n, 2816, LANES))
    out1 = _mm("l1_out", gz1, w["attn_w_out"], "nn")

    gf = _vec(p["final_norm_g"])

    def head(x1t, o1t, tgt, g1v, gfv):
        x2 = x1t + g1v * o1t
        xh, r = _rms_hat(x2)
        e = xh * gfv - tgt
        dyf = e * (1.0 / D_MODEL)
        dx2 = _rms_bwd(xh, r, dyf * gfv)
        return dx2, g1v * dx2, dyf * xh, dx2 * o1t, jnp.sum(e * e, axis=1, keepdims=True)

    gate1_lat = gate1[0:1]
    dx2, dout1, d_gf, d_gate1, sq = _rowwise(
        "head", head, t, tr, [(x1, 0, D_MODEL), (out1, 0, D_MODEL), (target, 0, D_MODEL)], [gate1_lat, gf],
        [(D_MODEL, F32), (D_MODEL, BF16)], [(1, D_MODEL), (1, D_MODEL), (1, 1)])

    d_w_attn_out = _mm("l1_out_dw", gz1, dout1, "tn", out_dtype=BF16)
    dgz1 = _mm("l1_out_dx", dout1, w["attn_w_out"], "nt", out_dtype=BF16)
    dq_s, dk, dv, dz1 = _attn_bwd(q_h, k_h, v_h, dgz1, proj1, o, lse, t, tq, _largest_tile(n, 1024, LANES))
    dproj1, d_qn, d_kn = _qk_prep_bwd(proj1, qn, kn, cos, sin, dq_s, dz1, dk, dv, n, t, tr)
    d_w_attn_in = _mm("l1_in_dw", h1, dproj1, "tn", out_dtype=BF16)
    dh1 = _mm("l1_in_dx", dproj1, w["attn_w_in"], "nt", out_dtype=BF16)
    dx1, dout0, d_g1, d_scale1, d_shift1, d_gate0 = _norm_mod_bwd("l1_norm_bwd", x1, g1, scale1, dh1, dx2, n, tr, t,
                                                                  prev=(out0, gate0))

    d_w_out = _mm("l0_out_dw", gz0, dout0, "tn", out_dtype=BF16)
    dgz0 = _mm("l0_out_dx", dout0, w["ssm_w_out"], "nt", out_dtype=BF16)

    def post_b_bwd(dgt, ygt, tt, zt, bv):
        s = _sigmoid(tt + bv)
        dy2 = dgt * _silu(zt)
        dt = dy2 * ygt * s * (1.0 - s)
        return dgt * (ygt * s) * _silu_grad(zt), dt, dy2 * s, dt

    dz0, dtg, dyg_a, d_b_glu = _rowwise(
        "l0_gate_bwd", post_b_bwd, n, tr, [(dgz0, 0, D_MODEL), (yg, 0, D_MODEL), (tg, 0, D_MODEL), (proj0, 1, D_MODEL)],
        [b_glu], [(D_MODEL, BF16), (D_MODEL, BF16), (D_MODEL, BF16)], [(1, D_MODEL)])
    d_w_glu = _mm("l0_glu_dw", yg, dtg, "tn", out_dtype=BF16)
    dyg_b = _mm("l0_glu_dx", dtg, w["ssm_w_glu"], "nt", out_dtype=BF16)

    def post_a_bwd(da, db, yt, ut, dv):
        dy = (da + db) * _gelu_grad(yt)
        return dy, dy * dv, dy * ut

    dy0, du_skip, d_d = _rowwise("l0_gelu_bwd", post_a_bwd, n, tr,
                                 [(dyg_a, 0, D_MODEL), (dyg_b, 0, D_MODEL), (y0, 0, D_MODEL), (proj0, 0, D_MODEL)], [d_skip],
                                 [(D_MODEL, F32), (D_MODEL, BF16)], [(1, D_MODEL)])
    s5_bwd = _s5_backward(proj0, dy0, dirs, saved, t, c)
    dproj0 = _rowwise("l0_in_grad", lambda a, b, cc, dz: jnp.concatenate([a + b + cc, dz], axis=1), n, tr,
                      [(du_skip, 0, D_MODEL), (s5_bwd[0][0], 0, D_MODEL), (s5_bwd[1][0], 0, D_MODEL), (dz0, 0, D_MODEL)], [],
                      [(2 * D_MODEL, BF16)], [])[0]
    d_w_in = _mm("l0_in_dw", h0, dproj0, "tn", out_dtype=BF16)
    dh0 = _mm("l0_in_dx", dproj0, w["ssm_w_in"], "nt", out_dtype=BF16)
    dx0, d_g0, d_scale0, d_shift0 = _norm_mod_bwd("l0_norm_bwd", xall, g0, scale0, dh0, dx1, n, tr, t, dx_lat_only=True)

    big = dict(ssm_w_in=d_w_in, ssm_w_glu=d_w_glu, ssm_w_out=d_w_out, attn_w_in=d_w_attn_in, attn_w_out=d_w_attn_out)
    small = dict(
        norm_g=jnp.concatenate([d_g0[0], d_g1[0]], axis=0), ssm_d=d_d[0], ssm_b_glu=d_b_glu[0],
        attn_q_norm=d_qn, attn_k_norm=d_kn, final_norm_g=d_gf[0, 0], **_s5_param_grads(raw, s5_bwd))
    zero_v = jnp.zeros((D_MODEL,), F32)
    d_mod_lat = jnp.stack([jnp.concatenate([d_shift0[0, 0], d_scale0[0, 0], d_gate0[0, 0]]),
                           jnp.concatenate([d_shift1[0, 0], d_scale1[0, 0], d_gate1[0, 0]])])
    d_mod_ctx = jnp.stack([jnp.concatenate([d_shift0[1, 0], d_scale0[1, 0], d_gate0[1, 0]]),
                           jnp.concatenate([d_shift1[1, 0], d_scale1[1, 0], zero_v])])
    return sq[0, 0, 0], dx0, big, small, d_mod_lat, d_mod_ctx


def _adamw(name, w, g, m, v):
    rows, cols = w.shape
    tr = _largest_tile(rows, 256, 8)
    c1 = 1.0 / (1.0 - ADAM_B1 ** ADAM_STEP)
    c2 = 1.0 / (1.0 - ADAM_B2 ** ADAM_STEP)

    def fn(wt, gt, mt, vt):
        mn = ADAM_B1 * mt + (1.0 - ADAM_B1) * gt
        vn = ADAM_B2 * vt + (1.0 - ADAM_B2) * (gt * gt)
        delta = -ADAM_LR * ((mn * c1) / (jnp.sqrt(vn * c2) + ADAM_EPS) + ADAM_WD * wt)
        return delta, mn, vn

    return _rowwise(name, fn, rows, tr, [(a, 0, cols) for a in (w, g, m, v)], [], [(cols, F32)] * 3, [])


BIG = ("ssm_w_in", "ssm_w_glu", "ssm_w_out", "attn_w_in", "attn_w_out")
COL_SHARDED = ("ssm_w_in", "attn_w_in")
WEIGHTS = ("c_ctx", "w_mod", "b_mod", "norm_g", "ssm_w_in", "ssm_a_re", "ssm_a_im", "ssm_log_dt", "ssm_b_re", "ssm_b_im",
           "ssm_c_re", "ssm_c_im", "ssm_d", "ssm_w_glu", "ssm_b_glu", "ssm_w_out", "attn_w_in", "attn_q_norm",
           "attn_k_norm", "attn_w_out", "final_norm_g")
SMALL = tuple(k for k in WEIGHTS if k not in BIG and k != "w_mod")
PACK_W = 1024
COND_ROWS = 2 * N_DEV


def _attn_in_perm(x, inverse):
    a, kv = ATTN_W, 2 * KV_W
    if inverse:
        return jnp.concatenate([x[..., :a], x[..., 2 * a:], x[..., a:2 * a]], axis=-1)
    return jnp.concatenate([x[..., :a], x[..., a + kv:], x[..., a:a + kv]], axis=-1)


def _pack(arrays, dtype, row_unit):
    flat = jnp.concatenate([a.reshape(-1).astype(dtype) for a in arrays])
    rows = -(-flat.shape[0] // PACK_W)
    rows = -(-rows // row_unit) * row_unit
    flat = jnp.concatenate([flat, jnp.zeros((rows * PACK_W - flat.shape[0],), dtype)])
    return flat.reshape(rows, PACK_W)


def _unpack(buf, shapes):
    lead = buf.shape[:-2]
    flat = buf.reshape(lead + (-1,))
    out, off = [], 0
    for shp in shapes:
        size = math.prod(shp)
        out.append(flat[..., off:off + size].reshape(lead + tuple(shp)))
        off += size
    return out


def kernel(x, c, ctx, c_ctx, w_mod, b_mod, norm_g, ssm_w_in, ssm_a_re, ssm_a_im, ssm_log_dt, ssm_b_re, ssm_b_im, ssm_c_re, ssm_c_im, ssm_d, ssm_w_glu, ssm_b_glu, ssm_w_out, attn_w_in, attn_q_norm, attn_k_norm, attn_w_out, final_norm_g, loss_target, m_c_ctx, m_w_mod, m_b_mod, m_norm_g, m_ssm_w_in, m_ssm_a_re, m_ssm_a_im, m_ssm_log_dt, m_ssm_b_re, m_ssm_b_im, m_ssm_c_re, m_ssm_c_im, m_ssm_d, m_ssm_w_glu, m_ssm_b_glu, m_ssm_w_out, m_attn_w_in, m_attn_q_norm, m_attn_k_norm, m_attn_w_out, m_final_norm_g, v_c_ctx, v_w_mod, v_b_mod, v_norm_g, v_ssm_w_in, v_ssm_a_re, v_ssm_a_im, v_ssm_log_dt, v_ssm_b_re, v_ssm_b_im, v_ssm_c_re, v_ssm_c_im, v_ssm_d, v_ssm_w_glu, v_ssm_b_glu, v_ssm_w_out, v_attn_w_in, v_attn_q_norm, v_attn_k_norm, v_attn_w_out, v_final_norm_g):
    args = dict(locals())
    wts = {k: args[k] for k in WEIGHTS}
    mom_m = {k: args["m_" + k] for k in WEIGHTS}
    mom_v = {k: args["v_" + k] for k in WEIGHTS}
    mx, my, mc = lax.axis_index("x"), lax.axis_index("y"), lax.axis_index("c")
    chip = 2 * mx + my
    me = 2 * chip + mc

    halves = []
    for k in BIG:
        sh = wts[k][0]
        hr = sh.shape[0] // 2
        halves.append(lax.dynamic_slice_in_dim(sh, mc * hr, hr, axis=0))
    gathered = _gather_two_level("gather_weights", _pack(halves, BF16, 16))
    parts = _unpack(gathered, [h.shape for h in halves])
    w_full = {}
    for k, pc in zip(BIG, parts):
        hr, cols = pc.shape[1:]
        pc = pc.reshape(N_CHIP, 2, hr, cols)
        if k in COL_SHARDED:
            w_full[k] = pc.transpose(1, 2, 0, 3).reshape(2 * hr, N_CHIP * cols)
        else:
            w_full[k] = pc.reshape(N_CHIP * 2 * hr, cols)
    w_full["attn_w_in"] = _attn_in_perm(w_full["attn_w_in"], False)

    c_blk = jnp.concatenate([c, jnp.zeros((N_DEV - 1, D_MODEL), F32)], axis=0)
    c_all = _exchange("gather_c", c_blk, True)[:, 0]
    cond = jnp.concatenate([c_all, c_ctx[None], jnp.zeros((COND_ROWS - N_DEV - 1, D_MODEL), F32)], axis=0)
    s_cond, ds_cond = _rowwise("cond_silu", lambda t: (_silu(t), _silu_grad(t)), COND_ROWS, COND_ROWS, [(cond, 0, D_MODEL)], [],
                               [(D_MODEL, F32), (D_MODEL, F32)], [])
    w_mod_b = w_mod.astype(BF16)
    mcols = w_mod.shape[2]
    mod_part = jnp.stack([_mm(f"mod{i}", s_cond, w_mod_b[i], "nn") for i in range(2)])
    mod_g = _exchange("gather_mod", mod_part.reshape(2 * COND_ROWS, mcols), True)
    mod_all = mod_g.reshape(N_CHIP, 2, 2, COND_ROWS, mcols)[:, 0]
    mod_all = mod_all.transpose(1, 2, 0, 3).reshape(2, COND_ROWS, N_CHIP * mcols) + b_mod[:, None, :]
    mods = []
    for i in range(2):
        lat = lax.dynamic_slice_in_dim(mod_all[i], me, 1, axis=0)[0]
        both = jnp.stack([lat, mod_all[i, N_DEV]])
        mods.append((both[:, :D_MODEL], both[:, D_MODEL:2 * D_MODEL], both[:, 2 * D_MODEL:]))

    small_p = {k: wts[k] for k in SMALL if k != "c_ctx" and k != "b_mod"}
    sq, grad_x, big_g, small_g, d_mod_lat, d_mod_ctx = _example_step(x[0], ctx[0], loss_target[0], mods, w_full, small_p)
    loss = lax.psum(0.5 / D_MODEL * sq, ("x", "y", "c"))
    big_g["attn_w_in"] = _attn_in_perm(big_g["attn_w_in"], True)

    small_names = [k for k in SMALL if k not in ("c_ctx", "b_mod")]
    small_list = [small_g[k] for k in small_names] + [d_mod_lat, d_mod_ctx]
    small_shapes = [wts[k].shape for k in small_names] + [d_mod_lat.shape, d_mod_ctx.shape]
    packed = _pack(small_list, F32, 8 * N_DEV)
    slice_rows = packed.shape[0] // N_DEV
    slices = _exchange("scatter_small", packed.reshape(N_DEV, slice_rows, PACK_W), False)
    my_sum = _sum_slots("sum_small", slices)
    payload = jnp.concatenate([my_sum, _pack([d_mod_lat], F32, 8)], axis=0)
    sg = _exchange("gather_small", payload, True)
    summed = _unpack(sg[:, :slice_rows].reshape(packed.shape), small_shapes)
    grads = dict(zip(small_names, summed[:-2]))
    d_mod_lat_sum, d_mod_ctx_sum = summed[-2], summed[-1]
    grads["b_mod"] = d_mod_lat_sum + d_mod_ctx_sum
    d_mod_lat_all = _unpack(sg[:, slice_rows:], [d_mod_lat.shape])[0]

    g_w_mod, ds_cc = [], []
    for i in range(2):
        rows9 = jnp.concatenate([d_mod_lat_all[:, i], d_mod_ctx_sum[i][None],
                                 jnp.zeros((COND_ROWS - N_DEV - 1, 3 * D_MODEL), F32)], axis=0)
        mine = lax.dynamic_slice_in_dim(rows9, chip * mcols, mcols, axis=1)
        g_w_mod.append(_mm(f"mod{i}_dw", s_cond, mine, "tn"))
        ds_cc.append(_mm(f"mod{i}_dx", mine, w_mod_b[i], "nt")[N_DEV])
    grads["w_mod"] = jnp.stack(g_w_mod)
    part = (ds_cc[0] + ds_cc[1]) * jnp.where(mc == 0, 1.0, 0.0)
    part_blk = jnp.concatenate([part[None], jnp.zeros((N_DEV - 1, D_MODEL), F32)], axis=0)
    ds_all = _sum_slots("sum_c_ctx", _exchange("gather_c_ctx", part_blk, True))
    grads["c_ctx"] = ds_all[0] * ds_cond[N_DEV]

    blocks = []
    for k in BIG:
        g = big_g[k]
        rows, cols = g.shape
        if k in COL_SHARDED:
            blocks.append(g.reshape(2, rows // 2, N_CHIP, cols // N_CHIP).transpose(2, 0, 1, 3).reshape(N_DEV, -1))
        else:
            blocks.append(g.reshape(N_DEV, -1))
    sendbuf = jnp.concatenate(blocks, axis=1).astype(BF16)
    sendbuf = sendbuf.reshape(N_DEV, -1, PACK_W)
    recv = _exchange("scatter_big", sendbuf, False)
    mine = _sum_slots("sum_big", recv)
    both = _exchange("swap_halves", mine, True, sibling_only=True)
    half_shapes = [(wts[k].shape[1] // 2, wts[k].shape[2]) for k in BIG]
    for k, pc in zip(BIG, _unpack(both, half_shapes)):
        grads[k] = pc.reshape(wts[k].shape)

    delta, new_m, new_v = {}, {}, {}
    for k in BIG + ("w_mod",):
        shp = wts[k].shape
        two_d = (-1, shp[-1])
        res = _adamw("adamw_" + k, *[a.reshape(two_d) for a in (wts[k], grads[k], mom_m[k], mom_v[k])])
        delta[k], new_m[k], new_v[k] = [r.reshape(shp) for r in res]
    shapes = [wts[k].shape for k in SMALL]
    packed = [_pack([d[k] for k in SMALL], F32, 8) for d in (wts, grads, mom_m, mom_v)]
    res = _adamw("adamw_small", *packed)
    for dst, buf in zip((delta, new_m, new_v), res):
        for k, a in zip(SMALL, _unpack(buf, shapes)):
            dst[k] = a
    grads = {k: grads[k].reshape(wts[k].shape) for k in WEIGHTS}
    return (loss, grad_x[None], *[grads[k] for k in WEIGHTS], *[delta[k] for k in WEIGHTS],
            *[new_m[k] for k in WEIGHTS], *[new_v[k] for k in WEIGHTS])
```

```python
import functools
import math

import jax
import jax.numpy as jnp
from jax import lax
from jax.experimental import pallas as pl
from jax.experimental.pallas import tpu as pltpu

F32 = jnp.float32
BF16 = jnp.bfloat16

D_MODEL = 1024
NORM_EPS = 1e-6
SSM_GROUPS = 64
SSM_GROUP = 16
SSM_STATE = 64
LANES = 128
SLAB_W = LANES
N_SLAB = D_MODEL // SLAB_W
SLAB_GROUPS = SLAB_W // SSM_GROUP
HALF_W = SLAB_GROUPS * SSM_STATE
STATE_W = 2 * HALF_W
N_SEG = 8
HEAD_DIM = 64
N_Q_HEADS = 16
N_KV_HEADS = 4
KV_REP = N_Q_HEADS // N_KV_HEADS
ATTN_W = N_Q_HEADS * HEAD_DIM
KV_W = N_KV_HEADS * HEAD_DIM
GRID_W = 64
ROPE_THETA = 10000.0
N_DEV = 8
N_CHIP = 4
VMEM_LIMIT_BYTES = 56 * 1024 * 1024

ADAM_LR = 0.001
ADAM_B1 = 0.9
ADAM_B2 = 0.999
ADAM_EPS = 1e-08
ADAM_WD = 0.01
ADAM_STEP = 10


def _params(*sem):
    return pltpu.CompilerParams(dimension_semantics=sem, vmem_limit_bytes=VMEM_LIMIT_BYTES)


def _largest_tile(n, cap, unit):
    if n <= cap:
        return n
    t = (cap // unit) * unit
    while t >= unit:
        if n % t == 0:
            return t
        t -= unit
    raise ValueError(f"no tile for {n} (cap {cap}, unit {unit})")


def _rowwise(name, fn, n_rows, tr, row_ins, vec_ins, row_outs, red_outs, n_lat=None, want_flag=False):
    nt = n_rows // tr
    assert nt * tr == n_rows
    nlt = nt if n_lat is None else n_lat // tr

    def sel(i):
        return jnp.where(i >= nlt, 1, 0)

    arrays, in_specs, pairs = [], [], []
    for spec in row_ins:
        arr, cb, w = spec[:3]
        kind = spec[3] if len(spec) > 3 else None
        m = spec[4] if len(spec) > 4 else None
        if kind == "pair":
            arrays += [arr, m]
            in_specs += [pl.BlockSpec((tr, w), functools.partial(lambda i, cb: (jnp.minimum(i, nlt - 1), cb), cb=cb)),
                         pl.BlockSpec((tr, w), functools.partial(lambda i, cb: (jnp.maximum(i - nlt, 0), cb), cb=cb))]
            pairs.append(len(arrays) - 2)
            continue
        if kind == "mod":
            imap = functools.partial(lambda i, cb, m: (i % m, cb), cb=cb, m=m)
        elif kind == "clamp":
            imap = functools.partial(lambda i, cb, m: (jnp.minimum(i, m - 1), cb), cb=cb, m=m)
        else:
            imap = functools.partial(lambda i, cb: (i, cb), cb=cb)
        arrays.append(arr)
        in_specs.append(pl.BlockSpec((tr, w), imap))
    for v in vec_ins:
        s, a, w = v.shape
        imap = (lambda i: (sel(i), 0, 0)) if s == 2 else (lambda i: (0, 0, 0))
        arrays.append(v)
        in_specs.append(pl.BlockSpec((1, a, w), imap))
    out_shapes, out_specs = [], []
    lat_only = [len(spec) > 2 for spec in row_outs]
    for (w, dt), lat in zip([spec[:2] for spec in row_outs], lat_only):
        out_shapes.append(jax.ShapeDtypeStruct((n_lat if lat else n_rows, w), dt))
        out_specs.append(pl.BlockSpec((tr, w), (lambda i: (jnp.minimum(i, nlt - 1), 0)) if lat else (lambda i: (i, 0))))
    for s, w in red_outs:
        out_shapes.append(jax.ShapeDtypeStruct((s, 1, w), F32))
        imap = (lambda i: (sel(i), 0, 0)) if s == 2 else (lambda i: (0, 0, 0))
        out_specs.append(pl.BlockSpec((1, 1, w), imap))
    n_ri, n_vi, n_ro, n_rd = len(row_ins) + len(pairs), len(vec_ins), len(row_outs), len(red_outs)

    def body(*refs):
        i = pl.program_id(0)
        rows, k = [], 0
        while k < n_ri:
            if k in pairs:
                rows.append(jnp.where(i < nlt, refs[k][...], refs[k + 1][...]).astype(F32))
                k += 2
            else:
                rows.append(refs[k][...].astype(F32))
                k += 1
        vecs = [r[0] for r in refs[n_ri:n_ri + n_vi]]
        outs = refs[n_ri + n_vi:]
        lead = [jnp.where(i < nlt, 1.0, 0.0).astype(F32)] if want_flag else []
        res = fn(*lead, *rows, *vecs)
        if not isinstance(res, (tuple, list)):
            res = (res,)
        assert len(res) == n_ro + n_rd
        for k in range(n_ro):
            if lat_only[k]:
                @pl.when(i < nlt)
                def _(k=k):
                    outs[k][...] = res[k].astype(outs[k].dtype)
            else:
                outs[k][...] = res[k].astype(outs[k].dtype)
        for k in range(n_rd):
            part = jnp.sum(res[n_ro + k].astype(F32), axis=0, keepdims=True)
            first = i == 0
            if red_outs[k][0] == 2:
                first = jnp.logical_or(first, i == nlt)
            o = outs[n_ro + k]

            @pl.when(first)
            def _():
                o[0] = part

            @pl.when(jnp.logical_not(first))
            def _():
                o[0] = o[0] + part

    res = pl.pallas_call(
        body, grid=(nt,), in_specs=in_specs, out_specs=out_specs, out_shape=out_shapes,
        compiler_params=_params("arbitrary"), name=name)(*arrays)
    return res


def _vec(v):
    v = v.astype(F32)
    if v.ndim == 1:
        v = v[None]
    return v[:, None, :]


def _mm(name, a, b, mode, out_dtype=F32):
    if mode in ("nn", "nt"):
        m, k = a.shape
        n = b.shape[1] if mode == "nn" else b.shape[0]
        tm = _largest_tile(m, 1024, 8)
        tn = _largest_tile(n, 1024, 128)
        contract = (((1,), (0,)), ((), ())) if mode == "nn" else (((1,), (1,)), ((), ()))

        def body(a_ref, b_ref, o_ref):
            o_ref[...] = lax.dot_general(a_ref[...].astype(BF16), b_ref[...].astype(BF16), contract,
                                         preferred_element_type=F32).astype(o_ref.dtype)

        b_spec = pl.BlockSpec((k, tn), lambda i, j: (0, j)) if mode == "nn" else pl.BlockSpec((tn, k), lambda i, j: (j, 0))
        return pl.pallas_call(
            body, grid=(m // tm, n // tn),
            in_specs=[pl.BlockSpec((tm, k), lambda i, j: (i, 0)), b_spec],
            out_specs=pl.BlockSpec((tm, tn), lambda i, j: (i, j)),
            out_shape=jax.ShapeDtypeStruct((m, n), out_dtype),
            compiler_params=_params("parallel", "arbitrary"), name=name)(a, b)
    assert mode == "tn"
    r, k1 = a.shape
    k2 = b.shape[1]
    tr = _largest_tile(r, 1024, 8)
    t2 = _largest_tile(k2, 1024, 128)
    nr = r // tr

    def body(a_ref, b_ref, o_ref, acc_ref):
        part = lax.dot_general(a_ref[...].astype(BF16), b_ref[...].astype(BF16), (((0,), (0,)), ((), ())),
                               preferred_element_type=F32)
        i = pl.program_id(1)

        @pl.when(i == 0)
        def _():
            acc_ref[...] = part

        @pl.when(i > 0)
        def _():
            acc_ref[...] += part

        @pl.when(i == nr - 1)
        def _():
            o_ref[...] = acc_ref[...].astype(o_ref.dtype)

    return pl.pallas_call(
        body, grid=(k2 // t2, nr),
        in_specs=[pl.BlockSpec((tr, k1), lambda j, i: (i, 0)), pl.BlockSpec((tr, t2), lambda j, i: (i, j))],
        out_specs=pl.BlockSpec((k1, t2), lambda j, i: (0, j)),
        out_shape=jax.ShapeDtypeStruct((k1, k2), out_dtype),
        scratch_shapes=[pltpu.VMEM((k1, t2), F32)],
        compiler_params=_params("parallel", "arbitrary"), name=name)(a, b)


def _exchange(name, x, bcast, sibling_only=False):
    rels = [1] if sibling_only else list(range(1, N_DEV))
    n_slot = 2 if sibling_only else N_DEV
    blk = x.shape if bcast else x.shape[1:]

    def body(x_ref, o_ref, send_sems, recv_sems, local_sem):
        mx, my, mc = lax.axis_index("x"), lax.axis_index("y"), lax.axis_index("c")
        me = mc if sibling_only else 4 * mx + 2 * my + mc
        me_dev = 4 * mx + 2 * my + mc
        mine = pltpu.make_async_copy(x_ref if bcast else x_ref.at[me_dev], o_ref.at[me], local_sem)
        mine.start()
        copies = []
        for k, r in enumerate(rels):
            px = 1 - mx if (r >> 2) & 1 else mx
            py = 1 - my if (r >> 1) & 1 else my
            pc = 1 - mc if r & 1 else mc
            src = x_ref if bcast else x_ref.at[4 * px + 2 * py + pc]
            cp = pltpu.make_async_remote_copy(
                src_ref=src, dst_ref=o_ref.at[me], send_sem=send_sems.at[k], recv_sem=recv_sems.at[k],
                device_id=(px, py, pc), device_id_type=pl.DeviceIdType.MESH)
            cp.start()
            copies.append(cp)
        for cp in copies:
            cp.wait()
        mine.wait()

    return pl.pallas_call(
        body, out_shape=jax.ShapeDtypeStruct((n_slot,) + tuple(blk), x.dtype),
        in_specs=[pl.BlockSpec(memory_space=pltpu.VMEM if sibling_only else pl.ANY)],
        out_specs=pl.BlockSpec(memory_space=pl.ANY),
        scratch_shapes=[pltpu.SemaphoreType.DMA((len(rels),)), pltpu.SemaphoreType.DMA((len(rels),)),
                        pltpu.SemaphoreType.DMA],
        name=name)(x)


def _gather_two_level(name, x):
    def body(x_ref, o_ref, send_sems, recv_sems, local_sem):
        mx, my, mc = lax.axis_index("x"), lax.axis_index("y"), lax.axis_index("c")
        me, sibling = (mx, my, mc), (mx, my, 1 - mc)
        chips = [(1 - mx, my), (mx, 1 - my), (1 - mx, 1 - my)]

        def slot(px, py, pc):
            return o_ref.at[4 * px + 2 * py + pc]

        def copy(k, block, to, src=None):
            return pltpu.make_async_remote_copy(
                src_ref=slot(*block) if src is None else src, dst_ref=slot(*block), send_sem=send_sems.at[k],
                recv_sem=recv_sems.at[k], device_id=to, device_id_type=pl.DeviceIdType.MESH)

        mine = pltpu.make_async_copy(x_ref, slot(*me), local_sem)
        mine.start()
        first = [copy(0, me, sibling, src=x_ref)]
        first += [copy(1 + j, me, (*chip, mc), src=x_ref) for j, chip in enumerate(chips)]
        for cp in first:
            cp.start()
        passed = [copy(4 + j, (*chip, mc), sibling) for j, chip in enumerate(chips)]
        for j, chip in enumerate(chips):
            copy(1 + j, (*chip, mc), me).wait_recv()
            passed[j].start()
        copy(0, sibling, me).wait_recv()
        for j, chip in enumerate(chips):
            copy(4 + j, (*chip, 1 - mc), me).wait_recv()
        for cp in first + passed:
            cp.wait_send()
        mine.wait()

    return pl.pallas_call(
        body, out_shape=jax.ShapeDtypeStruct((N_DEV,) + tuple(x.shape), x.dtype),
        in_specs=[pl.BlockSpec(memory_space=pl.ANY)], out_specs=pl.BlockSpec(memory_space=pl.ANY),
        scratch_shapes=[pltpu.SemaphoreType.DMA((N_DEV - 1,)), pltpu.SemaphoreType.DMA((N_DEV - 1,)),
                        pltpu.SemaphoreType.DMA],
        name=name)(x)


def _sum_slots(name, x):
    s, r, w = x.shape
    tr = _largest_tile(r, 256, 8)

    def body(x_ref, o_ref):
        acc = x_ref[0].astype(F32)
        for j in range(1, s):
            acc = acc + x_ref[j].astype(F32)
        o_ref[...] = acc

    return pl.pallas_call(
        body, grid=(r // tr,), in_specs=[pl.BlockSpec((s, tr, w), lambda i: (0, i, 0))],
        out_specs=pl.BlockSpec((tr, w), lambda i: (i, 0)), out_shape=jax.ShapeDtypeStruct((r, w), F32),
        compiler_params=_params("parallel"), name=name)(x)


def _sigmoid(x):
    return 1.0 / (1.0 + jnp.exp(-x))


def _silu(x):
    return x * _sigmoid(x)


def _silu_grad(x):
    s = _sigmoid(x)
    return s * (1.0 + x * (1.0 - s))


_INV_SQRT2 = 1.0 / math.sqrt(2.0)
_INV_SQRT2PI = 1.0 / math.sqrt(2.0 * math.pi)


def _gelu(x):
    return 0.5 * x * (1.0 + lax.erf(x * _INV_SQRT2))


def _gelu_grad(x):
    return 0.5 * (1.0 + lax.erf(x * _INV_SQRT2)) + x * jnp.exp(-0.5 * x * x) * _INV_SQRT2PI


def _rms_hat(x):
    r = lax.rsqrt(jnp.mean(x * x, axis=-1, keepdims=True) + NORM_EPS)
    return x * r, r


def _rms_bwd(xh, r, dxh):
    return r * (dxh - xh * jnp.mean(dxh * xh, axis=-1, keepdims=True))


def _stream(x):
    return (x[0], 0, D_MODEL, "pair", x[1]) if isinstance(x, tuple) else (x, 0, D_MODEL)


def _norm_mod_fwd(name, x, g, scale, shift, n_rows, tr, n_lat):
    def fn(xt, gv, sc, sh):
        xh, _ = _rms_hat(xt)
        return (xh * gv) * (1.0 + sc) + sh

    return _rowwise(name, fn, n_rows, tr, [_stream(x)], [g, scale, shift], [(D_MODEL, BF16)], [], n_lat=n_lat)[0]


def _norm_mod_bwd(name, x, g, scale, dh, dres, n_rows, tr, n_lat, prev=None, dx_lat_only=False):
    nlt = n_lat // tr

    def fn(flag, xt, dht, drt, *rest):
        gv, sc = rest[-2:] if prev is None else rest[1:3]
        xh, r = _rms_hat(xt)
        n = xh * gv
        dn = dht * (1.0 + sc)
        dx = _rms_bwd(xh, r, dn * gv) + flag * drt
        if prev is None:
            return dx, dn * xh, dht * n, dht
        return dx, rest[3] * dx, dn * xh, dht * n, dht, dx * rest[0]

    rows = [_stream(x), (dh, 0, D_MODEL), (dres, 0, D_MODEL, "clamp", nlt)]
    row_outs = [(D_MODEL, F32, "lat") if dx_lat_only else (D_MODEL, F32)]
    vecs, reds = [g, scale], [(1, D_MODEL), (2, D_MODEL), (2, D_MODEL)]
    if prev is not None:
        rows.append((prev[0], 0, D_MODEL))
        vecs.append(prev[1])
        row_outs.append((D_MODEL, BF16))
        reds.append((2, D_MODEL))
    return _rowwise(name, fn, n_rows, tr, rows, vecs, row_outs, reds, n_lat=n_lat, want_flag=True)


def _s5_prep(a_re, a_im, log_dt, b_re, b_im, seg_lat, seg_ctx):
    def body(ar_ref, ai_ref, ld_ref, br_ref, bi_ref, abr_ref, abi_ref, bbr_ref, bbi_ref, alr_ref, ali_ref, acr_ref,
             aci_ref):
        lr, li = ar_ref[...], ai_ref[...]
        dt = jnp.exp(ld_ref[...])
        ldr, ldi = lr * dt, li * dt
        e = jnp.exp(ldr)
        abr, abi = e * jnp.cos(ldi), e * jnp.sin(ldi)
        abr_ref[...] = abr
        abi_ref[...] = abi
        den = lr * lr + li * li
        nr, ni = abr - 1.0, abi
        qr = (nr * lr + ni * li) / den
        qi = (ni * lr - nr * li) / den
        br, bi = br_ref[...], bi_ref[...]
        bbr_ref[...] = qr[None] * br - qi[None] * bi
        bbi_ref[...] = qr[None] * bi + qi[None] * br
        for seg, r_ref, i_ref in ((seg_lat, alr_ref, ali_ref), (seg_ctx, acr_ref, aci_ref)):
            es = jnp.exp(ldr * float(seg))
            r_ref[...] = es * jnp.cos(ldi * float(seg))
            i_ref[...] = es * jnp.sin(ldi * float(seg))

    sm = jax.ShapeDtypeStruct(a_re.shape, F32)
    big = jax.ShapeDtypeStruct(b_re.shape, F32)
    return pl.pallas_call(body, out_shape=[sm, sm, big, big, sm, sm, sm, sm], name="s5_prep")(
        a_re, a_im, log_dt, b_re, b_im)


def _s5_prep_bwd(a_re, a_im, log_dt, b_re, b_im, dabr, dabi, dbbr, dbbi):
    def body(ar_ref, ai_ref, ld_ref, br_ref, bi_ref, dabr_ref, dabi_ref, dbbr_ref, dbbi_ref,
             dar_ref, dai_ref, dld_ref, dbr_ref, dbi_ref):
        lr, li = ar_ref[...], ai_ref[...]
        dt = jnp.exp(ld_ref[...])
        ldr, ldi = lr * dt, li * dt
        e = jnp.exp(ldr)
        abr, abi = e * jnp.cos(ldi), e * jnp.sin(ldi)
        den = lr * lr + li * li
        nr, ni = abr - 1.0, abi
        qr = (nr * lr + ni * li) / den
        qi = (ni * lr - nr * li) / den
        br, bi = br_ref[...], bi_ref[...]
        gbr, gbi = dbbr_ref[...], dbbi_ref[...]
        dbr_ref[...] = gbr * qr[None] + gbi * qi[None]
        dbi_ref[...] = gbi * qr[None] - gbr * qi[None]
        dqr = jnp.sum(gbr * br + gbi * bi, axis=0)
        dqi = jnp.sum(gbi * br - gbr * bi, axis=0)
        dnr = (dqr * lr - dqi * li) / den
        dni = (dqr * li + dqi * lr) / den
        dlr_q = (dqr * (nr - 2.0 * lr * qr) + dqi * (ni - 2.0 * lr * qi)) / den
        dli_q = (dqr * (ni - 2.0 * li * qr) + dqi * (-nr - 2.0 * li * qi)) / den
        gar = dabr_ref[...] + dnr
        gai = dabi_ref[...] + dni
        dldr = gar * abr + gai * abi
        dldi = gai * abr - gar * abi
        dar_ref[...] = dldr * dt + dlr_q
        dai_ref[...] = dldi * dt + dli_q
        ddt = jnp.sum(dldr * lr + dldi * li, axis=1, keepdims=True)
        dld_ref[...] = ddt * dt

    sm = jax.ShapeDtypeStruct(a_re.shape, F32)
    big = jax.ShapeDtypeStruct(b_re.shape, F32)
    return pl.pallas_call(body, out_shape=[sm, sm, jax.ShapeDtypeStruct(log_dt.shape, F32), big, big],
                          name="s5_prep_bwd")(a_re, a_im, log_dt, b_re, b_im, dabr, dabi, dbbr, dbbi)


def _slab_cols(v):
    return v.reshape(N_SLAB, 1, HALF_W)


def _slab_pair(vr, vi):
    return jnp.concatenate([_slab_cols(vr), _slab_cols(vi)], axis=-1)


def _slab_in_matrix(bbr, bbi):
    eye = jnp.eye(SLAB_GROUPS, dtype=F32)

    def one(b):
        b = b.reshape(N_SLAB, SLAB_GROUPS, SSM_STATE, SSM_GROUP)
        m = jnp.einsum("sgph,gk->sghkp", b, eye)
        return m.reshape(N_SLAB, SLAB_W, HALF_W)

    return jnp.concatenate([one(bbr), one(bbi)], axis=-1)


def _slab_out_matrix(cr, ci):
    eye = jnp.eye(SLAB_GROUPS, dtype=F32)

    def one(c):
        c = c.reshape(N_SLAB, SLAB_GROUPS, SSM_GROUP, SSM_STATE)
        m = jnp.einsum("sghp,gk->skpgh", c, eye)
        return m.reshape(N_SLAB, HALF_W, SLAB_W)

    return jnp.concatenate([one(cr), one(-ci)], axis=1)


def _slab_diag(m):
    m = m.reshape(N_SLAB, SLAB_GROUPS, SSM_GROUP, 2, SLAB_GROUPS, SSM_STATE)
    d = jnp.stack([m[:, g, :, :, g, :] for g in range(SLAB_GROUPS)], axis=1)
    return d.transpose(3, 0, 1, 2, 4).reshape(2, SSM_GROUPS, SSM_GROUP, SSM_STATE)


def _cmul(ar, ai, xr, xi, conj):
    if conj:
        return ar * xr + ai * xi, ar * xi - ai * xr
    return ar * xr - ai * xi, ar * xi + ai * xr


def _s5_pow_table(name, abar, seg, falling, conj):
    assert seg >= 8 and seg & (seg - 1) == 0

    def body(a_ref, o_ref, t_ref):
        ar, ai = a_ref[0, :, :HALF_W], a_ref[0, :, HALF_W:]
        if conj:
            ai = -ai
        rr, ri = [jnp.ones_like(ar)], [jnp.zeros_like(ai)]
        for _ in range(7):
            pr, pi = _cmul(ar, ai, rr[-1], ri[-1], False)
            rr.append(pr)
            ri.append(pi)
        sr, si = _cmul(ar, ai, rr[-1], ri[-1], False)
        if falling:
            rr, ri = rr[::-1], ri[::-1]
        first = slice(seg - 8, seg) if falling else slice(0, 8)
        t_ref[first, :HALF_W] = jnp.concatenate(rr, axis=0)
        t_ref[first, HALF_W:] = jnp.concatenate(ri, axis=0)
        size = 8
        while size < seg:
            src = slice(seg - size, seg) if falling else slice(0, size)
            dst = slice(seg - 2 * size, seg - size) if falling else slice(size, 2 * size)
            pr, pi = _cmul(sr, si, t_ref[src, :HALF_W], t_ref[src, HALF_W:], False)
            t_ref[dst, :HALF_W] = pr
            t_ref[dst, HALF_W:] = pi
            sr, si = _cmul(sr, si, sr, si, False)
            size *= 2
        o_ref[0] = t_ref[...].astype(BF16)

    return pl.pallas_call(
        body, grid=(N_SLAB,), in_specs=[pl.BlockSpec((1, 1, STATE_W), lambda s: (s, 0, 0))],
        out_specs=pl.BlockSpec((1, seg, STATE_W), lambda s: (s, 0, 0)),
        out_shape=jax.ShapeDtypeStruct((N_SLAB, seg, STATE_W), BF16),
        scratch_shapes=[pltpu.VMEM((seg, STATE_W), F32)], compiler_params=_params("parallel"), name=name)(abar)


def _s5_ends(name, x, n_rows, row0, table, m_mat):
    seg = n_rows // N_SEG
    rb = row0 // n_rows
    tn = (((0,), (0,)), ((), ()))

    def body(x_ref, t_ref, m_ref, z_ref):
        mr, mi = m_ref[0, :, :HALF_W], m_ref[0, :, HALF_W:]
        for j in range(N_SEG):
            t = lax.dot_general(x_ref[j * seg:(j + 1) * seg, :].astype(BF16), t_ref[0], tn,
                                preferred_element_type=F32)
            tr_, ti_ = t[:, :HALF_W], t[:, HALF_W:]
            z_ref[0, j:j + 1, :HALF_W] = jnp.sum(mr * tr_ - mi * ti_, axis=0, keepdims=True)
            z_ref[0, j:j + 1, HALF_W:] = jnp.sum(mr * ti_ + mi * tr_, axis=0, keepdims=True)

    return pl.pallas_call(
        body, grid=(N_SLAB,),
        in_specs=[pl.BlockSpec((n_rows, SLAB_W), lambda s: (rb, s)),
                  pl.BlockSpec((1, seg, STATE_W), lambda s: (s, 0, 0)),
                  pl.BlockSpec((1, SLAB_W, STATE_W), lambda s: (s, 0, 0))],
        out_specs=pl.BlockSpec((1, N_SEG, STATE_W), lambda s: (s, 0, 0)),
        out_shape=jax.ShapeDtypeStruct((N_SLAB, N_SEG, STATE_W), F32),
        compiler_params=_params("parallel"), name=name)(x, table, m_mat)


def _s5_carry(name, z, a_seg, init, descending, conj):
    order = list(range(N_SEG - 1, -1, -1)) if descending else list(range(N_SEG))

    def body(z_ref, a_ref, i_ref, e_ref, o_ref):
        ar, ai = a_ref[:, :HALF_W], a_ref[:, HALF_W:]
        cr, ci = i_ref[:, :HALF_W], i_ref[:, HALF_W:]
        for j in order:
            e_ref[:, j, :HALF_W] = cr
            e_ref[:, j, HALF_W:] = ci
            pr, pi = _cmul(ar, ai, cr, ci, conj)
            cr = pr + z_ref[:, j, :HALF_W]
            ci = pi + z_ref[:, j, HALF_W:]
        o_ref[:, :HALF_W] = cr
        o_ref[:, HALF_W:] = ci

    return pl.pallas_call(body, out_shape=[jax.ShapeDtypeStruct(z.shape, F32), jax.ShapeDtypeStruct(init.shape, F32)],
                          name=name)(z, a_seg, init)


def _s5_scan(name, u, n_rows, row0, b_mat, c_mat, abar, h_in, descending, y_alias=None, y_rows=None):
    seg = n_rows // N_SEG
    ta = min(128, seg)
    nk = seg // ta
    assert seg * N_SEG == n_rows and nk * ta == seg and row0 % n_rows == 0 and ta % 8 == 0
    rb = row0 // n_rows
    tile = ta * N_SEG

    def body(*refs):
        u_ref, b_ref, c_ref, a_ref, hin_ref = refs[:5]
        y_ref, hch_ref, st_ref, up_ref, h_ref = refs[-5:]
        k = pl.program_id(1)
        kk = nk - 1 - k if descending else k
        a0 = kk * ta

        @pl.when(k == 0)
        def _():
            st_ref[...] = hin_ref[0]

        hch_ref[0, 0] = st_ref[...]
        for al in range(ta):
            up_ref[al * N_SEG:(al + 1) * N_SEG, :] = u_ref[pl.ds(a0 + al, N_SEG, stride=seg), :]
        h_ref[...] = jnp.dot(up_ref[...].astype(BF16), b_ref[0], preferred_element_type=F32)
        ar = jnp.broadcast_to(a_ref[0, :, :HALF_W], (N_SEG, HALF_W))
        ai = jnp.broadcast_to(a_ref[0, :, HALF_W:], (N_SEG, HALF_W))

        def step(i, carry):
            hr, hi = carry
            al = ta - 1 - i if descending else i
            row = pl.multiple_of(al * N_SEG, N_SEG)
            pr, pi = _cmul(ar, ai, hr, hi, False)
            hr = pr + h_ref[pl.ds(row, N_SEG), :HALF_W]
            hi = pi + h_ref[pl.ds(row, N_SEG), HALF_W:]
            h_ref[pl.ds(row, N_SEG), :HALF_W] = hr
            h_ref[pl.ds(row, N_SEG), HALF_W:] = hi
            return hr, hi

        hr, hi = lax.fori_loop(0, ta, step, (st_ref[:, :HALF_W], st_ref[:, HALF_W:]), unroll=True)
        st_ref[:, :HALF_W] = hr
        st_ref[:, HALF_W:] = hi
        yt = jnp.dot(h_ref[...].astype(BF16), c_ref[0], preferred_element_type=F32)
        for al in range(ta):
            y_ref[pl.ds(a0 + al, N_SEG, stride=seg), :] = yt[al * N_SEG:(al + 1) * N_SEG, :]

    u_spec = pl.BlockSpec((n_rows, SLAB_W), lambda s, k: (rb, s))
    b_spec = pl.BlockSpec((1, SLAB_W, STATE_W), lambda s, k: (s, 0, 0))
    c_spec = pl.BlockSpec((1, STATE_W, SLAB_W), lambda s, k: (s, 0, 0))
    a_spec = pl.BlockSpec((1, 1, STATE_W), lambda s, k: (s, 0, 0))
    st_spec = pl.BlockSpec((1, N_SEG, STATE_W), lambda s, k: (s, 0, 0))
    scratch = [pltpu.VMEM((N_SEG, STATE_W), F32), pltpu.VMEM((tile, SLAB_W), F32), pltpu.VMEM((tile, STATE_W), F32)]
    kmap = (lambda s, k: (s, nk - 1 - k, 0, 0)) if descending else (lambda s, k: (s, k, 0, 0))
    out_specs = [u_spec, pl.BlockSpec((1, 1, N_SEG, STATE_W), kmap)]
    out_shape = [jax.ShapeDtypeStruct((y_rows, D_MODEL), F32), jax.ShapeDtypeStruct((N_SLAB, nk, N_SEG, STATE_W), F32)]
    in_specs = [u_spec, b_spec, c_spec, a_spec, st_spec]
    args = [u, b_mat, c_mat, abar, h_in]
    aliases = {}
    if y_alias is not None:
        in_specs.append(pl.BlockSpec(memory_space=pl.ANY))
        args.append(y_alias)
        aliases = {5: 0}
    return pl.pallas_call(
        body, grid=(N_SLAB, nk), in_specs=in_specs, out_specs=out_specs, out_shape=out_shape, scratch_shapes=scratch,
        input_output_aliases=aliases, compiler_params=_params("parallel", "arbitrary"), name=name)(*args)


def _s5_scan_bwd(name, u, dy, n_rows, row0, b_mat, bt_mat, ct_mat, abar, h_chunks, g_in, descending,
                 du_alias=None, du_rows=None):
    seg = n_rows // N_SEG
    ta = min(128, seg)
    nk = seg // ta
    rb = row0 // n_rows
    tile = ta * N_SEG
    g_desc = not descending

    def body(*refs):
        u_ref, dy_ref, b_ref, bt_ref, ct_ref, a_ref, hch_ref, gin_ref = refs[:8]
        du_ref, db_ref, dc_ref, da_ref, st_ref, up_ref, dyp_ref, h_ref, g_ref = refs[-9:]
        k = pl.program_id(1)
        kk = nk - 1 - k if g_desc else k
        a0 = kk * ta
        ar = jnp.broadcast_to(a_ref[0, :, :HALF_W], (N_SEG, HALF_W))
        ai = jnp.broadcast_to(a_ref[0, :, HALF_W:], (N_SEG, HALF_W))

        @pl.when(k == 0)
        def _():
            st_ref[...] = gin_ref[0]

        for al in range(ta):
            dyp_ref[al * N_SEG:(al + 1) * N_SEG, :] = dy_ref[pl.ds(a0 + al, N_SEG, stride=seg), :]
            up_ref[al * N_SEG:(al + 1) * N_SEG, :] = u_ref[pl.ds(a0 + al, N_SEG, stride=seg), :]
        g_ref[...] = jnp.dot(dyp_ref[...].astype(BF16), ct_ref[0], preferred_element_type=F32)
        h_ref[...] = jnp.dot(up_ref[...].astype(BF16), b_ref[0], preferred_element_type=F32)
        h0r, h0i = hch_ref[0, 0, :, :HALF_W], hch_ref[0, 0, :, HALF_W:]

        def hstep(i, carry):
            hr, hi = carry
            al = ta - 1 - i if descending else i
            row = pl.multiple_of(al * N_SEG, N_SEG)
            pr, pi = _cmul(ar, ai, hr, hi, False)
            hr = pr + h_ref[pl.ds(row, N_SEG), :HALF_W]
            hi = pi + h_ref[pl.ds(row, N_SEG), HALF_W:]
            h_ref[pl.ds(row, N_SEG), :HALF_W] = hr
            h_ref[pl.ds(row, N_SEG), HALF_W:] = hi
            return hr, hi

        lax.fori_loop(0, ta, hstep, (h0r, h0i), unroll=True)

        def gstep(i, carry):
            gr, gi = carry
            al = ta - 1 - i if g_desc else i
            row = pl.multiple_of(al * N_SEG, N_SEG)
            pr, pi = _cmul(ar, ai, gr, gi, True)
            gr = pr + g_ref[pl.ds(row, N_SEG), :HALF_W]
            gi = pi + g_ref[pl.ds(row, N_SEG), HALF_W:]
            g_ref[pl.ds(row, N_SEG), :HALF_W] = gr
            g_ref[pl.ds(row, N_SEG), HALF_W:] = gi
            return gr, gi

        gr, gi = lax.fori_loop(0, ta, gstep, (st_ref[:, :HALF_W], st_ref[:, HALF_W:]), unroll=True)
        st_ref[:, :HALF_W] = gr
        st_ref[:, HALF_W:] = gi

        gb = g_ref[...].astype(BF16)
        dut = jnp.dot(gb, bt_ref[0], preferred_element_type=F32)
        for al in range(ta):
            du_ref[pl.ds(a0 + al, N_SEG, stride=seg), :] = dut[al * N_SEG:(al + 1) * N_SEG, :]
        tn = (((0,), (0,)), ((), ()))
        dbp = lax.dot_general(up_ref[...].astype(BF16), gb, tn, preferred_element_type=F32)
        dcp = lax.dot_general(dyp_ref[...].astype(BF16), h_ref[...].astype(BF16), tn, preferred_element_type=F32)
        inner = (ta - 1) * N_SEG
        if descending:
            g_in_r, g_in_i = g_ref[0:inner, :HALF_W], g_ref[0:inner, HALF_W:]
            p_in_r, p_in_i = h_ref[N_SEG:tile, :HALF_W], h_ref[N_SEG:tile, HALF_W:]
            g_ed_r, g_ed_i = g_ref[inner:tile, :HALF_W], g_ref[inner:tile, HALF_W:]
        else:
            g_in_r, g_in_i = g_ref[N_SEG:tile, :HALF_W], g_ref[N_SEG:tile, HALF_W:]
            p_in_r, p_in_i = h_ref[0:inner, :HALF_W], h_ref[0:inner, HALF_W:]
            g_ed_r, g_ed_i = g_ref[0:N_SEG, :HALF_W], g_ref[0:N_SEG, HALF_W:]
        dar = g_ed_r * h0r + g_ed_i * h0i
        dai = g_ed_i * h0r - g_ed_r * h0i
        if ta > 1:
            dar = dar + jnp.sum((g_in_r * p_in_r + g_in_i * p_in_i).reshape(ta - 1, N_SEG, HALF_W), axis=0)
            dai = dai + jnp.sum((g_in_i * p_in_r - g_in_r * p_in_i).reshape(ta - 1, N_SEG, HALF_W), axis=0)

        @pl.when(k == 0)
        def _():
            db_ref[0] = dbp
            dc_ref[0] = dcp
            da_ref[0, :, :HALF_W] = dar
            da_ref[0, :, HALF_W:] = dai

        @pl.when(k > 0)
        def _():
            db_ref[0] += dbp
            dc_ref[0] += dcp
            da_ref[0, :, :HALF_W] += dar
            da_ref[0, :, HALF_W:] += dai

    u_spec = pl.BlockSpec((n_rows, SLAB_W), lambda s, k: (rb, s))
    m_spec = pl.BlockSpec((1, SLAB_W, STATE_W), lambda s, k: (s, 0, 0))
    mt_spec = pl.BlockSpec((1, STATE_W, SLAB_W), lambda s, k: (s, 0, 0))
    a_spec = pl.BlockSpec((1, 1, STATE_W), lambda s, k: (s, 0, 0))
    st_spec = pl.BlockSpec((1, N_SEG, STATE_W), lambda s, k: (s, 0, 0))
    st_shape = jax.ShapeDtypeStruct((N_SLAB, N_SEG, STATE_W), F32)
    kmap = (lambda s, k: (s, nk - 1 - k, 0, 0)) if g_desc else (lambda s, k: (s, k, 0, 0))
    in_specs = [u_spec, u_spec, m_spec, mt_spec, m_spec, a_spec, pl.BlockSpec((1, 1, N_SEG, STATE_W), kmap), st_spec]
    args = [u, dy, b_mat, bt_mat, ct_mat, abar, h_chunks, g_in]
    aliases = {}
    if du_alias is not None:
        in_specs.append(pl.BlockSpec(memory_space=pl.ANY))
        args.append(du_alias)
        aliases = {8: 0}
    acc_shape = jax.ShapeDtypeStruct((N_SLAB, SLAB_W, STATE_W), F32)
    out_specs = [u_spec, m_spec, m_spec, st_spec]
    out_shape = [jax.ShapeDtypeStruct((du_rows, D_MODEL), F32), acc_shape, acc_shape, st_shape]
    scratch = [pltpu.VMEM((N_SEG, STATE_W), F32), pltpu.VMEM((tile, SLAB_W), F32), pltpu.VMEM((tile, SLAB_W), F32),
               pltpu.VMEM((tile, STATE_W), F32), pltpu.VMEM((tile, STATE_W), F32)]
    return pl.pallas_call(
        body, grid=(N_SLAB, nk), in_specs=in_specs, out_specs=out_specs, out_shape=out_shape, scratch_shapes=scratch,
        input_output_aliases=aliases, compiler_params=_params("parallel", "arbitrary"), name=name)(*args)


ROPE_HALF = HEAD_DIM // 4
TABLE_W = 2 * HEAD_DIM
Q_SCALE = 1.0 / math.sqrt(HEAD_DIM)
HEADS_PER_BLOCK = 2 * KV_REP
Q_BLOCK_W = HEADS_PER_BLOCK * HEAD_DIM


def _rope_tables(n_lat, n_ctx):
    rows = n_lat // GRID_W
    freqs = ROPE_THETA ** (-jnp.arange(ROPE_HALF, dtype=F32) / ROPE_HALF)
    ang_r = jnp.arange(rows, dtype=F32)[:, None] * freqs[None]
    ang_c = jnp.arange(GRID_W, dtype=F32)[:, None] * freqs[None]
    by_row = lambda v: jnp.repeat(v, GRID_W, axis=0)
    by_col = lambda v: jnp.tile(v, (rows, 1))
    cos = jnp.concatenate([by_row(jnp.cos(ang_r)), by_row(jnp.cos(ang_r)), by_col(jnp.cos(ang_c)), by_col(jnp.cos(ang_c))] * 2,
                          axis=1)
    sin = jnp.concatenate([by_row(jnp.sin(ang_r)), by_row(jnp.sin(ang_r)), by_col(jnp.sin(ang_c)), by_col(jnp.sin(ang_c))] * 2,
                          axis=1)
    cos = jnp.concatenate([cos, jnp.ones((n_ctx, TABLE_W), F32)], axis=0)
    sin = jnp.concatenate([sin, jnp.zeros((n_ctx, TABLE_W), F32)], axis=0)
    return cos, sin


def _rot_half(v):
    w = v.shape[1]
    ahead = pltpu.roll(v, w - ROPE_HALF, axis=1)
    behind = pltpu.roll(v, ROPE_HALF, axis=1)
    lane = lax.broadcasted_iota(jnp.int32, v.shape, 1)
    return jnp.where((lane % (2 * ROPE_HALF)) < ROPE_HALF, -ahead, behind)


def _head_mean(v, sel, selt):
    m = jnp.dot(v, sel, precision=lax.Precision.HIGH, preferred_element_type=F32) * (1.0 / HEAD_DIM)
    return jnp.dot(m, selt, precision=lax.Precision.HIGH, preferred_element_type=F32)


def _head_selectors(n_heads):
    sel = jnp.repeat(jnp.eye(n_heads, dtype=F32), HEAD_DIM, axis=0)
    return sel[None], sel.T[None]


def _head_norm(x, sel, selt):
    r = lax.rsqrt(_head_mean(x * x, sel, selt) + NORM_EPS)
    return x * r, r


def _qk_prep(proj, qn, kn, cos, sin, n, tr):
    qw, kw = _vec(jnp.tile(qn, N_Q_HEADS)), _vec(jnp.tile(kn, N_KV_HEADS))
    sq, sqt = _head_selectors(N_Q_HEADS)
    sk, skt = _head_selectors(N_KV_HEADS)

    def fn(qr, kvr, ct, st, qwv, kwv, s16, s16t, s4, s4t):
        outs = []
        for x, wv, sel, selt, scale in ((qr, qwv, s16, s16t, Q_SCALE), (kvr[:, :KV_W], kwv, s4, s4t, 1.0)):
            reps = x.shape[1] // TABLE_W
            cw, sw = jnp.tile(ct, (1, reps)), jnp.tile(st, (1, reps))
            xh, _ = _head_norm(x, sel, selt)
            nrm = xh * wv
            outs.append((nrm * cw + _rot_half(nrm) * sw) * scale)
        return outs[0], outs[1], kvr[:, KV_W:]

    return _rowwise("l1_qk_prep", fn, n, tr,
                    [(proj, 0, ATTN_W), (proj, 2 * ATTN_W // (2 * KV_W), 2 * KV_W), (cos, 0, TABLE_W), (sin, 0, TABLE_W)],
                    [qw, kw, sq, sqt, sk, skt], [(ATTN_W, BF16), (KV_W, BF16), (KV_W, BF16)], [])


def _qk_prep_bwd(proj, qn, kn, cos, sin, dq, dz, dk, dv, n, n_lat, tr):
    qw, kw = _vec(jnp.tile(qn, N_Q_HEADS)), _vec(jnp.tile(kn, N_KV_HEADS))
    sq, sqt = _head_selectors(N_Q_HEADS)
    sk, skt = _head_selectors(N_KV_HEADS)
    nlt = n_lat // tr

    def fn(flag, qr, kvr, ct, st, dqt, dzt, dkt, dvt, qwv, kwv, s16, s16t, s4, s4t):
        dxs, dws = [], []
        for x, dy, wv, sel, selt in ((qr, dqt * (flag * Q_SCALE), qwv, s16, s16t), (kvr[:, :KV_W], dkt, kwv, s4, s4t)):
            reps = x.shape[1] // TABLE_W
            cw, sw = jnp.tile(ct, (1, reps)), jnp.tile(st, (1, reps))
            xh, r = _head_norm(x, sel, selt)
            dn = dy * cw - _rot_half(dy * sw)
            dxh = dn * wv
            dxs.append(r * (dxh - xh * _head_mean(dxh * xh, sel, selt)))
            dws.append(dn * xh)
        return jnp.concatenate([dxs[0], dzt * flag, dxs[1], dvt], axis=1), dws[0], dws[1]

    dproj, dqw, dkw = _rowwise(
        "l1_qk_prep_bwd", fn, n, tr,
        [(proj, 0, ATTN_W), (proj, 2 * ATTN_W // (2 * KV_W), 2 * KV_W), (cos, 0, TABLE_W), (sin, 0, TABLE_W),
         (dq, 0, ATTN_W, "clamp", nlt), (dz, 0, ATTN_W, "clamp", nlt), (dk, 0, KV_W), (dv, 0, KV_W)],
        [qw, kw, sq, sqt, sk, skt], [(2 * ATTN_W + 2 * KV_W, BF16)], [(1, ATTN_W), (1, KV_W)], n_lat=n_lat, want_flag=True)
    return dproj, dqw.reshape(N_Q_HEADS, HEAD_DIM).sum(0)[None], dkw.reshape(N_KV_HEADS, HEAD_DIM).sum(0)[None]


NT = (((1,), (1,)), ((), ()))


def _attn_fwd(q, k, v, proj, t, tq, tk):
    n = k.shape[0]
    nkc = n // tk

    ts = _largest_tile(tq, 256, LANES)
    items = [(sub, j) for sub in range(tq // ts) for j in range(HEADS_PER_BLOCK)]

    def body(q_ref, k_ref, v_ref, z_ref, o_ref, lse_ref, gz_ref, s_ref, m_ref, acc_ref):
        def lanes(j):
            g = j // KV_REP
            return slice(j * HEAD_DIM, (j + 1) * HEAD_DIM), slice(g * HEAD_DIM, (g + 1) * HEAD_DIM)

        for idx in range(len(items) + 1):
            nxt = items[idx] if idx < len(items) else None
            cur = items[idx - 1] if idx > 0 else None
            sn, sc = idx % 2, (idx - 1) % 2
            if nxt is not None:
                rows_n = slice(nxt[0] * ts, (nxt[0] + 1) * ts)
                ql_n, kl_n = lanes(nxt[1])
                qv = q_ref[rows_n, ql_n]
                m_ref[sn] = jnp.full((ts, LANES), -jnp.inf, F32)
            if cur is not None:
                rows_c = slice(cur[0] * ts, (cur[0] + 1) * ts)
                ql_c, kl_c = lanes(cur[1])
                m_row = jnp.max(m_ref[sc], axis=-1, keepdims=True)
                acc_ref[...] = jnp.zeros(acc_ref.shape, F32)

            def sweep(kc, c):
                off = pl.multiple_of(kc * tk, tk)
                if nxt is not None:
                    s = lax.dot_general(qv, k_ref[pl.ds(off, tk), kl_n], NT, preferred_element_type=F32)
                    s_ref[sn, :, pl.ds(off, tk)] = s
                    m = m_ref[sn]
                    for cb in range(tk // LANES):
                        m = jnp.maximum(m, s[:, cb * LANES:(cb + 1) * LANES])
                    m_ref[sn] = m
                if cur is not None:
                    p = jnp.exp(s_ref[sc, :, pl.ds(off, tk)] - m_row)
                    v_one = jnp.concatenate([v_ref[pl.ds(off, tk), kl_c], jnp.ones((tk, HEAD_DIM), BF16)], axis=1)
                    acc_ref[...] += jnp.dot(p.astype(BF16), v_one, preferred_element_type=F32)
                return c

            lax.fori_loop(0, nkc, sweep, 0, unroll=True)
            if cur is not None:
                l_row = acc_ref[:, HEAD_DIM:HEAD_DIM + 1]
                o_head = acc_ref[:, :HEAD_DIM] / l_row
                o_ref[rows_c, ql_c] = o_head
                gz_ref[rows_c, ql_c] = (o_head * _silu(z_ref[rows_c, ql_c].astype(F32))).astype(BF16)
                lse_ref[0, rows_c, cur[1]:cur[1] + 1] = m_row + jnp.log(l_row)

    nb = ATTN_W // Q_BLOCK_W
    kspec = pl.BlockSpec((n, LANES), lambda b, i: (0, b))
    return pl.pallas_call(
        body, grid=(nb, t // tq),
        in_specs=[pl.BlockSpec((tq, Q_BLOCK_W), lambda b, i: (i, b)), kspec, kspec,
                  pl.BlockSpec((tq, Q_BLOCK_W), lambda b, i: (i, nb + b))],
        out_specs=[pl.BlockSpec((tq, Q_BLOCK_W), lambda b, i: (i, b)),
                   pl.BlockSpec((1, tq, HEADS_PER_BLOCK), lambda b, i: (b, i, 0)),
                   pl.BlockSpec((tq, Q_BLOCK_W), lambda b, i: (i, b))],
        out_shape=[jax.ShapeDtypeStruct((t, ATTN_W), F32), jax.ShapeDtypeStruct((nb, t, HEADS_PER_BLOCK), F32),
                   jax.ShapeDtypeStruct((t, ATTN_W), BF16)],
        scratch_shapes=[pltpu.VMEM((2, ts, n), F32), pltpu.VMEM((2, ts, LANES), F32), pltpu.VMEM((ts, 2 * HEAD_DIM), F32)],
        compiler_params=_params("parallel", "parallel"), name="attn_fwd")(q, k, v, proj)


def _attn_bwd(q, k, v, dgz, proj, o, lse, t, tq, tk):
    n = k.shape[0]
    nkc = n // tk
    tn = (((0,), (0,)), ((), ()))

    def body(q_ref, k_ref, v_ref, dgz_ref, z_ref, o_ref, lse_ref, dq_ref, dk_ref, dv_ref, dz_ref, acc_ref):
        @pl.when(pl.program_id(1) == 0)
        def _():
            dk_ref[...] = jnp.zeros(dk_ref.shape, F32)
            dv_ref[...] = jnp.zeros(dv_ref.shape, F32)

        for j0 in range(0, HEADS_PER_BLOCK, 2):
            kl = slice((j0 // KV_REP) * HEAD_DIM, (j0 // KV_REP + 1) * HEAD_DIM)
            heads = []
            for a in range(2):
                j = j0 + a
                ql = slice(j * HEAD_DIM, (j + 1) * HEAD_DIM)
                qv, ov = q_ref[:, ql], o_ref[:, ql]
                zv, dgv = z_ref[:, ql].astype(F32), dgz_ref[:, ql].astype(F32)
                dov = (dgv * _silu(zv)).astype(BF16)
                dz_ref[:, ql] = (dgv * ov * _silu_grad(zv)).astype(dz_ref.dtype)
                dl_v = jnp.sum(dov.astype(F32) * ov, axis=-1, keepdims=True)
                heads.append((ql, qv, dov, dl_v, lse_ref[0, :, j:j + 1]))
                acc_ref[a] = jnp.zeros((tq, HEAD_DIM), F32)

            def step(kc, c):
                off = pl.multiple_of(kc * tk, tk)
                kt = k_ref[pl.ds(off, tk), kl]
                vt = v_ref[pl.ds(off, tk), kl]
                dv_part, dk_part = None, None
                for a, (_, qv, dov, dl_v, lse_v) in enumerate(heads):
                    s = lax.dot_general(qv, kt, NT, preferred_element_type=F32)
                    p = jnp.exp(s - lse_v)
                    dp = lax.dot_general(dov, vt, NT, preferred_element_type=F32)
                    ds = (p * (dp - dl_v)).astype(BF16)
                    acc_ref[a] += jnp.dot(ds, kt, preferred_element_type=F32)
                    dvp = lax.dot_general(p.astype(BF16), dov, tn, preferred_element_type=F32)
                    dkp = lax.dot_general(ds, qv, tn, preferred_element_type=F32)
                    dv_part = dvp if dv_part is None else dv_part + dvp
                    dk_part = dkp if dk_part is None else dk_part + dkp
                dv_ref[pl.ds(off, tk), kl] += dv_part
                dk_ref[pl.ds(off, tk), kl] += dk_part
                return c

            lax.fori_loop(0, nkc, step, 0, unroll=2)
            for a, h in enumerate(heads):
                dq_ref[:, h[0]] = acc_ref[a].astype(dq_ref.dtype)

    nb = ATTN_W // Q_BLOCK_W
    qspec = pl.BlockSpec((tq, Q_BLOCK_W), lambda b, i: (i, b))
    kspec = pl.BlockSpec((n, LANES), lambda b, i: (0, b))
    cspec = pl.BlockSpec((1, tq, HEADS_PER_BLOCK), lambda b, i: (b, i, 0))
    return pl.pallas_call(
        body, grid=(nb, t // tq),
        in_specs=[qspec, kspec, kspec, qspec, pl.BlockSpec((tq, Q_BLOCK_W), lambda b, i: (i, nb + b)), qspec, cspec],
        out_specs=[qspec, kspec, kspec, qspec],
        out_shape=[jax.ShapeDtypeStruct((t, ATTN_W), BF16), jax.ShapeDtypeStruct((n, KV_W), F32),
                   jax.ShapeDtypeStruct((n, KV_W), F32), jax.ShapeDtypeStruct((t, ATTN_W), BF16)],
        scratch_shapes=[pltpu.VMEM((2, tq, HEAD_DIM), F32)],
        compiler_params=_params("parallel", "arbitrary"), name="attn_bwd")(q, k, v, dgz, proj, o, lse)


def _s5_system(p, n_lat, n_ctx):
    two_g = 2 * SSM_GROUPS
    a_re = p["ssm_a_re"].reshape(two_g, SSM_STATE)
    a_im = p["ssm_a_im"].reshape(two_g, SSM_STATE)
    log_dt = p["ssm_log_dt"].reshape(two_g, 1)
    b_re = p["ssm_b_re"].reshape(two_g, SSM_STATE, SSM_GROUP).transpose(2, 0, 1)
    b_im = p["ssm_b_im"].reshape(two_g, SSM_STATE, SSM_GROUP).transpose(2, 0, 1)
    raw = (a_re, a_im, log_dt, b_re, b_im)
    abr, abi, bbr, bbi, alr, ali, acr, aci = _s5_prep(*raw, n_lat // N_SEG, n_ctx // N_SEG)
    dirs = []
    for d in range(2):
        g = slice(d * SSM_GROUPS, (d + 1) * SSM_GROUPS)
        b_mat = _slab_in_matrix(bbr[:, g].transpose(1, 2, 0), bbi[:, g].transpose(1, 2, 0))
        c_mat = _slab_out_matrix(p["ssm_c_re"][0, d], p["ssm_c_im"][0, d])
        abar = _slab_pair(abr[g], abi[g])
        tables = {}
        for part, seg in (("lat", n_lat // N_SEG), ("ctx", n_ctx // N_SEG)):
            tables["h_" + part] = _s5_pow_table(f"s5_pow_h{d}_{part}", abar, seg, d == 0, False)
            tables["g_" + part] = _s5_pow_table(f"s5_pow_g{d}_{part}", abar, seg, d == 1, True)
        dirs.append(dict(
            b=b_mat.astype(BF16), bt=b_mat.transpose(0, 2, 1).astype(BF16), b32=b_mat,
            c=c_mat.astype(BF16), ct=c_mat.transpose(0, 2, 1).astype(BF16), ct32=c_mat.transpose(0, 2, 1),
            abar=abar, a_lat=_slab_pair(alr[g], ali[g])[:, 0], a_ctx=_slab_pair(acr[g], aci[g])[:, 0], **tables))
    return raw, dirs


def _s5_forward(proj, dirs, n_lat, n_ctx):
    n = n_lat + n_ctx
    zero_c = jnp.zeros((N_SLAB, STATE_W), F32)
    ys, saved = [], []
    for d, s in enumerate(dirs):
        desc = d == 1
        tag = f"s5f{d}"
        zc = _s5_ends(tag + "_ctx_ends", proj, n_ctx, n_lat, s["h_ctx"], s["b32"])
        ent_c, h0 = _s5_carry(tag + "_ctx_carry", zc, s["a_ctx"], zero_c, desc, False)
        y, hch_c = _s5_scan(tag + "_ctx", proj, n_ctx, n_lat, s["b"], s["c"], s["abar"], ent_c, desc,
                            y_rows=n)
        zl = _s5_ends(tag + "_lat_ends", proj, n_lat, 0, s["h_lat"], s["b32"])
        ent_l, _ = _s5_carry(tag + "_lat_carry", zl, s["a_lat"], h0, desc, False)
        y, hch_l = _s5_scan(tag + "_lat", proj, n_lat, 0, s["b"], s["c"], s["abar"], ent_l, desc,
                            y_alias=y, y_rows=n)
        ys.append(y)
        saved.append((hch_l, hch_c))
    return ys, saved


def _s5_backward(proj, dy, dirs, saved, n_lat, n_ctx):
    n = n_lat + n_ctx
    zero_c = jnp.zeros((N_SLAB, STATE_W), F32)
    out = []
    for d, s in enumerate(dirs):
        desc = d == 1
        tag = f"s5b{d}"
        hch_l, hch_c = saved[d]
        gl = _s5_ends(tag + "_lat_ends", dy, n_lat, 0, s["g_lat"], s["ct32"])
        ent_l, g0 = _s5_carry(tag + "_lat_carry", gl, s["a_lat"], zero_c, not desc, True)
        du, db_l, dc_l, da_l = _s5_scan_bwd(tag + "_lat", proj, dy, n_lat, 0, s["b"], s["bt"], s["ct"], s["abar"],
                                            hch_l, ent_l, desc, du_rows=n)
        gc = _s5_ends(tag + "_ctx_ends", dy, n_ctx, n_lat, s["g_ctx"], s["ct32"])
        ent_c, _ = _s5_carry(tag + "_ctx_carry", gc, s["a_ctx"], g0, not desc, True)
        du, db_c, dc_c, da_c = _s5_scan_bwd(tag + "_ctx", proj, dy, n_ctx, n_lat, s["b"], s["bt"], s["ct"], s["abar"],
                                            hch_c, ent_c, desc, du_alias=du, du_rows=n)
        out.append((du, db_l + db_c, dc_l + dc_c, da_l + da_c))
    return out


def _s5_param_grads(raw, bwd):
    dabr, dabi, dbbr, dbbi, dcr, dci = [], [], [], [], [], []
    for _, db, dc, da in bwd:
        da = jnp.sum(da, axis=1)
        dabr.append(da[:, :HALF_W].reshape(SSM_GROUPS, SSM_STATE))
        dabi.append(da[:, HALF_W:].reshape(SSM_GROUPS, SSM_STATE))
        dbd = _slab_diag(db)
        dbbr.append(dbd[0].transpose(1, 0, 2))
        dbbi.append(dbd[1].transpose(1, 0, 2))
        dcd = _slab_diag(dc)
        dcr.append(dcd[0])
        dci.append(-dcd[1])
    cat = lambda xs, ax: jnp.concatenate(xs, axis=ax)
    dar, dai, dld, dbr, dbi = _s5_prep_bwd(*raw, cat(dabr, 0), cat(dabi, 0), cat(dbbr, 1), cat(dbbi, 1))
    shp = (1, 2, SSM_GROUPS, SSM_STATE)
    b_shape = (1, 2, SSM_GROUPS, SSM_STATE, SSM_GROUP)
    return dict(
        ssm_a_re=dar.reshape(shp), ssm_a_im=dai.reshape(shp), ssm_log_dt=dld.reshape(1, 2, SSM_GROUPS),
        ssm_b_re=dbr.transpose(1, 2, 0).reshape(b_shape), ssm_b_im=dbi.transpose(1, 2, 0).reshape(b_shape),
        ssm_c_re=jnp.stack(dcr)[None], ssm_c_im=jnp.stack(dci)[None])


def _example_step(x, ctx, target, mods, w, p):
    t, c = x.shape[0], ctx.shape[0]
    n = t + c
    assert t % c == 0 and c % LANES == 0 and c % (8 * N_SEG) == 0 and t % GRID_W == 0
    tr = _largest_tile(c, 256, 8)
    xall = (x, ctx)
    g0, g1 = _vec(p["norm_g"][0]), _vec(p["norm_g"][1])
    (shift0, scale0, gate0), (shift1, scale1, gate1) = [tuple(_vec(v) for v in m) for m in mods]

    h0 = _norm_mod_fwd("l0_norm", xall, g0, scale0, shift0, n, tr, t)
    proj0 = _mm("l0_in", h0, w["ssm_w_in"], "nn")
    raw, dirs = _s5_system(p, t, c)
    (y_f, y_r), saved = _s5_forward(proj0, dirs, t, c)
    d_skip = _vec(p["ssm_d"][0])

    def post_a(u, yf, yr, dv):
        y = u * dv + yf + yr
        return y, _gelu(y)

    y0, yg = _rowwise("l0_gelu", post_a, n, tr, [(proj0, 0, D_MODEL), (y_f, 0, D_MODEL), (y_r, 0, D_MODEL)], [d_skip],
                      [(D_MODEL, BF16), (D_MODEL, BF16)], [])
    tg = _mm("l0_glu", yg, w["ssm_w_glu"], "nn", out_dtype=BF16)
    b_glu = _vec(p["ssm_b_glu"][0])

    def post_b(ygt, tt, zt, bv):
        return ygt * _sigmoid(tt + bv) * _silu(zt)

    gz0 = _rowwise("l0_gate", post_b, n, tr, [(yg, 0, D_MODEL), (tg, 0, D_MODEL), (proj0, 1, D_MODEL)], [b_glu],
                   [(D_MODEL, BF16)], [])[0]
    out0 = _mm("l0_out", gz0, w["ssm_w_out"], "nn")

    def res_norm(xt, ot, gv, g1v, sc, sh):
        x1t = xt + gv * ot
        xh, _ = _rms_hat(x1t)
        return x1t, (xh * g1v) * (1.0 + sc) + sh

    x1, h1 = _rowwise("l0_res_l1_norm", res_norm, n, tr, [_stream(xall), (out0, 0, D_MODEL)],
                      [gate0, g1, scale1, shift1], [(D_MODEL, F32), (D_MODEL, BF16)], [], n_lat=t)
    proj1 = _mm("l1_in", h1, w["attn_w_in"], "nn", out_dtype=BF16)
    cos, sin = _rope_tables(t, c)
    qn, kn = p["attn_q_norm"][0], p["attn_k_norm"][0]
    q_h, k_h, v_h = _qk_prep(proj1, qn, kn, cos, sin, n, tr)
    tq = _largest_tile(t, 512, LANES)
    o, lse, gz1 = _attn_fwd(q_h, k_h, v_h, proj1, t, tq, _largest_tile(n, 2816, LANES))
    out1 = _mm("l1_out", gz1, w["attn_w_out"], "nn")

    gf = _vec(p["final_norm_g"])

    def head(x1t, o1t, tgt, g1v, gfv):
        x2 = x1t + g1v * o1t
        xh, r = _rms_hat(x2)
        e = xh * gfv - tgt
        dyf = e * (1.0 / D_MODEL)
        dx2 = _rms_bwd(xh, r, dyf * gfv)
        return dx2, g1v * dx2, dyf * xh, dx2 * o1t, jnp.sum(e * e, axis=1, keepdims=True)

    gate1_lat = gate1[0:1]
    dx2, dout1, d_gf, d_gate1, sq = _rowwise(
        "head", head, t, tr, [(x1, 0, D_MODEL), (out1, 0, D_MODEL), (target, 0, D_MODEL)], [gate1_lat, gf],
        [(D_MODEL, F32), (D_MODEL, BF16)], [(1, D_MODEL), (1, D_MODEL), (1, 1)])

    d_w_attn_out = _mm("l1_out_dw", gz1, dout1, "tn", out_dtype=BF16)
    dgz1 = _mm("l1_out_dx", dout1, w["attn_w_out"], "nt", out_dtype=BF16)
    dq_s, dk, dv, dz1 = _attn_bwd(q_h, k_h, v_h, dgz1, proj1, o, lse, t, tq, _largest_tile(n, 1024, LANES))
    dproj1, d_qn, d_kn = _qk_prep_bwd(proj1, qn, kn, cos, sin, dq_s, dz1, dk, dv, n, t, tr)
    d_w_attn_in = _mm("l1_in_dw", h1, dproj1, "tn", out_dtype=BF16)
    dh1 = _mm("l1_in_dx", dproj1, w["attn_w_in"], "nt", out_dtype=BF16)
    dx1, dout0, d_g1, d_scale1, d_shift1, d_gate0 = _norm_mod_bwd("l1_norm_bwd", x1, g1, scale1, dh1, dx2, n, tr, t,
                                                                  prev=(out0, gate0))

    d_w_out = _mm("l0_out_dw", gz0, dout0, "tn", out_dtype=BF16)
    dgz0 = _mm("l0_out_dx", dout0, w["ssm_w_out"], "nt", out_dtype=BF16)

    def post_b_bwd(dgt, ygt, tt, zt, bv):
        s = _sigmoid(tt + bv)
        dy2 = dgt * _silu(zt)
        dt = dy2 * ygt * s * (1.0 - s)
        return dgt * (ygt * s) * _silu_grad(zt), dt, dy2 * s, dt

    dz0, dtg, dyg_a, d_b_glu = _rowwise(
        "l0_gate_bwd", post_b_bwd, n, tr, [(dgz0, 0, D_MODEL), (yg, 0, D_MODEL), (tg, 0, D_MODEL), (proj0, 1, D_MODEL)],
        [b_glu], [(D_MODEL, BF16), (D_MODEL, BF16), (D_MODEL, BF16)], [(1, D_MODEL)])
    d_w_glu = _mm("l0_glu_dw", yg, dtg, "tn", out_dtype=BF16)
    dyg_b = _mm("l0_glu_dx", dtg, w["ssm_w_glu"], "nt", out_dtype=BF16)

    def post_a_bwd(da, db, yt, ut, dv):
        dy = (da + db) * _gelu_grad(yt)
        return dy, dy * dv, dy * ut

    dy0, du_skip, d_d = _rowwise("l0_gelu_bwd", post_a_bwd, n, tr,
                                 [(dyg_a, 0, D_MODEL), (dyg_b, 0, D_MODEL), (y0, 0, D_MODEL), (proj0, 0, D_MODEL)], [d_skip],
                                 [(D_MODEL, F32), (D_MODEL, BF16)], [(1, D_MODEL)])
    s5_bwd = _s5_backward(proj0, dy0, dirs, saved, t, c)
    dproj0 = _rowwise("l0_in_grad", lambda a, b, cc, dz: jnp.concatenate([a + b + cc, dz], axis=1), n, tr,
                      [(du_skip, 0, D_MODEL), (s5_bwd[0][0], 0, D_MODEL), (s5_bwd[1][0], 0, D_MODEL), (dz0, 0, D_MODEL)], [],
                      [(2 * D_MODEL, BF16)], [])[0]
    d_w_in = _mm("l0_in_dw", h0, dproj0, "tn", out_dtype=BF16)
    dh0 = _mm("l0_in_dx", dproj0, w["ssm_w_in"], "nt", out_dtype=BF16)
    dx0, d_g0, d_scale0, d_shift0 = _norm_mod_bwd("l0_norm_bwd", xall, g0, scale0, dh0, dx1, n, tr, t, dx_lat_only=True)

    big = dict(ssm_w_in=d_w_in, ssm_w_glu=d_w_glu, ssm_w_out=d_w_out, attn_w_in=d_w_attn_in, attn_w_out=d_w_attn_out)
    small = dict(
        norm_g=jnp.concatenate([d_g0[0], d_g1[0]], axis=0), ssm_d=d_d[0], ssm_b_glu=d_b_glu[0],
        attn_q_norm=d_qn, attn_k_norm=d_kn, final_norm_g=d_gf[0, 0], **_s5_param_grads(raw, s5_bwd))
    zero_v = jnp.zeros((D_MODEL,), F32)
    d_mod_lat = jnp.stack([jnp.concatenate([d_shift0[0, 0], d_scale0[0, 0], d_gate0[0, 0]]),
                           jnp.concatenate([d_shift1[0, 0], d_scale1[0, 0], d_gate1[0, 0]])])
    d_mod_ctx = jnp.stack([jnp.concatenate([d_shift0[1, 0], d_scale0[1, 0], d_gate0[1, 0]]),
                           jnp.concatenate([d_shift1[1, 0], d_scale1[1, 0], zero_v])])
    return sq[0, 0, 0], dx0, big, small, d_mod_lat, d_mod_ctx


def _adamw(name, w, g, m, v):
    rows, cols = w.shape
    tr = _largest_tile(rows, 256, 8)
    c1 = 1.0 / (1.0 - ADAM_B1 ** ADAM_STEP)
    c2 = 1.0 / (1.0 - ADAM_B2 ** ADAM_STEP)

    def fn(wt, gt, mt, vt):
        mn = ADAM_B1 * mt + (1.0 - ADAM_B1) * gt
        vn = ADAM_B2 * vt + (1.0 - ADAM_B2) * (gt * gt)
        delta = -ADAM_LR * ((mn * c1) / (jnp.sqrt(vn * c2) + ADAM_EPS) + ADAM_WD * wt)
        return delta, mn, vn

    return _rowwise(name, fn, rows, tr, [(a, 0, cols) for a in (w, g, m, v)], [], [(cols, F32)] * 3, [])


BIG = ("ssm_w_in", "ssm_w_glu", "ssm_w_out", "attn_w_in", "attn_w_out")
COL_SHARDED = ("ssm_w_in", "attn_w_in")
WEIGHTS = ("c_ctx", "w_mod", "b_mod", "norm_g", "ssm_w_in", "ssm_a_re", "ssm_a_im", "ssm_log_dt", "ssm_b_re", "ssm_b_im",
           "ssm_c_re", "ssm_c_im", "ssm_d", "ssm_w_glu", "ssm_b_glu", "ssm_w_out", "attn_w_in", "attn_q_norm",
           "attn_k_norm", "attn_w_out", "final_norm_g")
SMALL = tuple(k for k in WEIGHTS if k not in BIG and k != "w_mod")
PACK_W = 1024
COND_ROWS = 2 * N_DEV


def _attn_in_perm(x, inverse):
    a, kv = ATTN_W, 2 * KV_W
    if inverse:
        return jnp.concatenate([x[..., :a], x[..., 2 * a:], x[..., a:2 * a]], axis=-1)
    return jnp.concatenate([x[..., :a], x[..., a + kv:], x[..., a:a + kv]], axis=-1)


def _pack(arrays, dtype, row_unit):
    flat = jnp.concatenate([a.reshape(-1).astype(dtype) for a in arrays])
    rows = -(-flat.shape[0] // PACK_W)
    rows = -(-rows // row_unit) * row_unit
    flat = jnp.concatenate([flat, jnp.zeros((rows * PACK_W - flat.shape[0],), dtype)])
    return flat.reshape(rows, PACK_W)


def _unpack(buf, shapes):
    lead = buf.shape[:-2]
    flat = buf.reshape(lead + (-1,))
    out, off = [], 0
    for shp in shapes:
        size = math.prod(shp)
        out.append(flat[..., off:off + size].reshape(lead + tuple(shp)))
        off += size
    return out


def kernel(x, c, ctx, c_ctx, w_mod, b_mod, norm_g, ssm_w_in, ssm_a_re, ssm_a_im, ssm_log_dt, ssm_b_re, ssm_b_im, ssm_c_re, ssm_c_im, ssm_d, ssm_w_glu, ssm_b_glu, ssm_w_out, attn_w_in, attn_q_norm, attn_k_norm, attn_w_out, final_norm_g, loss_target, m_c_ctx, m_w_mod, m_b_mod, m_norm_g, m_ssm_w_in, m_ssm_a_re, m_ssm_a_im, m_ssm_log_dt, m_ssm_b_re, m_ssm_b_im, m_ssm_c_re, m_ssm_c_im, m_ssm_d, m_ssm_w_glu, m_ssm_b_glu, m_ssm_w_out, m_attn_w_in, m_attn_q_norm, m_attn_k_norm, m_attn_w_out, m_final_norm_g, v_c_ctx, v_w_mod, v_b_mod, v_norm_g, v_ssm_w_in, v_ssm_a_re, v_ssm_a_im, v_ssm_log_dt, v_ssm_b_re, v_ssm_b_im, v_ssm_c_re, v_ssm_c_im, v_ssm_d, v_ssm_w_glu, v_ssm_b_glu, v_ssm_w_out, v_attn_w_in, v_attn_q_norm, v_attn_k_norm, v_attn_w_out, v_final_norm_g):
    args = dict(locals())
    wts = {k: args[k] for k in WEIGHTS}
    mom_m = {k: args["m_" + k] for k in WEIGHTS}
    mom_v = {k: args["v_" + k] for k in WEIGHTS}
    mx, my, mc = lax.axis_index("x"), lax.axis_index("y"), lax.axis_index("c")
    chip = 2 * mx + my
    me = 2 * chip + mc

    halves = []
    for k in BIG:
        sh = wts[k][0]
        hr = sh.shape[0] // 2
        halves.append(lax.dynamic_slice_in_dim(sh, mc * hr, hr, axis=0))
    gathered = _gather_two_level("gather_weights", _pack(halves, BF16, 16))
    parts = _unpack(gathered, [h.shape for h in halves])
    w_full = {}
    for k, pc in zip(BIG, parts):
        hr, cols = pc.shape[1:]
        pc = pc.reshape(N_CHIP, 2, hr, cols)
        if k in COL_SHARDED:
            w_full[k] = pc.transpose(1, 2, 0, 3).reshape(2 * hr, N_CHIP * cols)
        else:
            w_full[k] = pc.reshape(N_CHIP * 2 * hr, cols)
    w_full["attn_w_in"] = _attn_in_perm(w_full["attn_w_in"], False)

    c_blk = jnp.concatenate([c, jnp.zeros((N_DEV - 1, D_MODEL), F32)], axis=0)
    c_all = _exchange("gather_c", c_blk, True)[:, 0]
    cond = jnp.concatenate([c_all, c_ctx[None], jnp.zeros((COND_ROWS - N_DEV - 1, D_MODEL), F32)], axis=0)
    s_cond, ds_cond = _rowwise("cond_silu", lambda t: (_silu(t), _silu_grad(t)), COND_ROWS, COND_ROWS, [(cond, 0, D_MODEL)], [],
                               [(D_MODEL, F32), (D_MODEL, F32)], [])
    w_mod_b = w_mod.astype(BF16)
    mcols = w_mod.shape[2]
    mod_part = jnp.stack([_mm(f"mod{i}", s_cond, w_mod_b[i], "nn") for i in range(2)])
    mod_g = _exchange("gather_mod", mod_part.reshape(2 * COND_ROWS, mcols), True)
    mod_all = mod_g.reshape(N_CHIP, 2, 2, COND_ROWS, mcols)[:, 0]
    mod_all = mod_all.transpose(1, 2, 0, 3).reshape(2, COND_ROWS, N_CHIP * mcols) + b_mod[:, None, :]
    mods = []
    for i in range(2):
        lat = lax.dynamic_slice_in_dim(mod_all[i], me, 1, axis=0)[0]
        both = jnp.stack([lat, mod_all[i, N_DEV]])
        mods.append((both[:, :D_MODEL], both[:, D_MODEL:2 * D_MODEL], both[:, 2 * D_MODEL:]))

    small_p = {k: wts[k] for k in SMALL if k != "c_ctx" and k != "b_mod"}
    sq, grad_x, big_g, small_g, d_mod_lat, d_mod_ctx = _example_step(x[0], ctx[0], loss_target[0], mods, w_full, small_p)
    loss = lax.psum(0.5 / D_MODEL * sq, ("x", "y", "c"))
    big_g["attn_w_in"] = _attn_in_perm(big_g["attn_w_in"], True)

    small_names = [k for k in SMALL if k not in ("c_ctx", "b_mod")]
    small_list = [small_g[k] for k in small_names] + [d_mod_lat, d_mod_ctx]
    small_shapes = [wts[k].shape for k in small_names] + [d_mod_lat.shape, d_mod_ctx.shape]
    packed = _pack(small_list, F32, 8 * N_DEV)
    slice_rows = packed.shape[0] // N_DEV
    slices = _exchange("scatter_small", packed.reshape(N_DEV, slice_rows, PACK_W), False)
    my_sum = _sum_slots("sum_small", slices)
    payload = jnp.concatenate([my_sum, _pack([d_mod_lat], F32, 8)], axis=0)
    sg = _exchange("gather_small", payload, True)
    summed = _unpack(sg[:, :slice_rows].reshape(packed.shape), small_shapes)
    grads = dict(zip(small_names, summed[:-2]))
    d_mod_lat_sum, d_mod_ctx_sum = summed[-2], summed[-1]
    grads["b_mod"] = d_mod_lat_sum + d_mod_ctx_sum
    d_mod_lat_all = _unpack(sg[:, slice_rows:], [d_mod_lat.shape])[0]

    g_w_mod, ds_cc = [], []
    for i in range(2):
        rows9 = jnp.concatenate([d_mod_lat_all[:, i], d_mod_ctx_sum[i][None],
                                 jnp.zeros((COND_ROWS - N_DEV - 1, 3 * D_MODEL), F32)], axis=0)
        mine = lax.dynamic_slice_in_dim(rows9, chip * mcols, mcols, axis=1)
        g_w_mod.append(_mm(f"mod{i}_dw", s_cond, mine, "tn"))
        ds_cc.append(_mm(f"mod{i}_dx", mine, w_mod_b[i], "nt")[N_DEV])
    grads["w_mod"] = jnp.stack(g_w_mod)
    part = (ds_cc[0] + ds_cc[1]) * jnp.where(mc == 0, 1.0, 0.0)
    part_blk = jnp.concatenate([part[None], jnp.zeros((N_DEV - 1, D_MODEL), F32)], axis=0)
    ds_all = _sum_slots("sum_c_ctx", _exchange("gather_c_ctx", part_blk, True))
    grads["c_ctx"] = ds_all[0] * ds_cond[N_DEV]

    blocks = []
    for k in BIG:
        g = big_g[k]
        rows, cols = g.shape
        if k in COL_SHARDED:
            blocks.append(g.reshape(2, rows // 2, N_CHIP, cols // N_CHIP).transpose(2, 0, 1, 3).reshape(N_DEV, -1))
        else:
            blocks.append(g.reshape(N_DEV, -1))
    sendbuf = jnp.concatenate(blocks, axis=1).astype(BF16)
    sendbuf = sendbuf.reshape(N_DEV, -1, PACK_W)
    recv = _exchange("scatter_big", sendbuf, False)
    mine = _sum_slots("sum_big", recv)
    both = _exchange("swap_halves", mine, True, sibling_only=True)
    half_shapes = [(wts[k].shape[1] // 2, wts[k].shape[2]) for k in BIG]
    for k, pc in zip(BIG, _unpack(both, half_shapes)):
        grads[k] = pc.reshape(wts[k].shape)

    delta, new_m, new_v = {}, {}, {}
    for k in BIG + ("w_mod",):
        shp = wts[k].shape
        two_d = (-1, shp[-1])
        res = _adamw("adamw_" + k, *[a.reshape(two_d) for a in (wts[k], grads[k], mom_m[k], mom_v[k])])
        delta[k], new_m[k], new_v[k] = [r.reshape(shp) for r in res]
    shapes = [wts[k].shape for k in SMALL]
    packed = [_pack([d[k] for k in SMALL], F32, 8) for d in (wts, grads, mom_m, mom_v)]
    res = _adamw("adamw_small", *packed)
    for dst, buf in zip((delta, new_m, new_v), res):
        for k, a in zip(SMALL, _unpack(buf, shapes)):
            dst[k] = a
    grads = {k: grads[k].reshape(wts[k].shape) for k in WEIGHTS}
    return (loss, grad_x[None], *[grads[k] for k in WEIGHTS], *[delta[k] for k in WEIGHTS],
            *[new_m[k] for k in WEIGHTS], *[new_v[k] for k in WEIGHTS])
```

```python
import functools
import math

import jax
import jax.numpy as jnp
from jax import lax
from jax.experimental import pallas as pl
from jax.experimental.pallas import tpu as pltpu

F32 = jnp.float32
BF16 = jnp.bfloat16

D_MODEL = 1024
NORM_EPS = 1e-6
SSM_GROUPS = 64
SSM_GROUP = 16
SSM_STATE = 64
LANES = 128
SLAB_W = LANES
N_SLAB = D_MODEL // SLAB_W
SLAB_GROUPS = SLAB_W // SSM_GROUP
HALF_W = SLAB_GROUPS * SSM_STATE
STATE_W = 2 * HALF_W
N_SEG = 8
HEAD_DIM = 64
N_Q_HEADS = 16
N_KV_HEADS = 4
KV_REP = N_Q_HEADS // N_KV_HEADS
ATTN_W = N_Q_HEADS * HEAD_DIM
KV_W = N_KV_HEADS * HEAD_DIM
GRID_W = 64
ROPE_THETA = 10000.0
N_DEV = 8
N_CHIP = 4
VMEM_LIMIT_BYTES = 56 * 1024 * 1024

ADAM_LR = 0.001
ADAM_B1 = 0.9
ADAM_B2 = 0.999
ADAM_EPS = 1e-08
ADAM_WD = 0.01
ADAM_STEP = 10


def _params(*sem):
    return pltpu.CompilerParams(dimension_semantics=sem, vmem_limit_bytes=VMEM_LIMIT_BYTES)


def _largest_tile(n, cap, unit):
    if n <= cap:
        return n
    t = (cap // unit) * unit
    while t >= unit:
        if n % t == 0:
            return t
        t -= unit
    raise ValueError(f"no tile for {n} (cap {cap}, unit {unit})")


def _rowwise(name, fn, n_rows, tr, row_ins, vec_ins, row_outs, red_outs, n_lat=None, want_flag=False):
    nt = n_rows // tr
    assert nt * tr == n_rows
    nlt = nt if n_lat is None else n_lat // tr

    def sel(i):
        return jnp.where(i >= nlt, 1, 0)

    arrays, in_specs, pairs = [], [], []
    for spec in row_ins:
        arr, cb, w = spec[:3]
        kind = spec[3] if len(spec) > 3 else None
        m = spec[4] if len(spec) > 4 else None
        if kind == "pair":
            arrays += [arr, m]
            in_specs += [pl.BlockSpec((tr, w), functools.partial(lambda i, cb: (jnp.minimum(i, nlt - 1), cb), cb=cb)),
                         pl.BlockSpec((tr, w), functools.partial(lambda i, cb: (jnp.maximum(i - nlt, 0), cb), cb=cb))]
            pairs.append(len(arrays) - 2)
            continue
        if kind == "mod":
            imap = functools.partial(lambda i, cb, m: (i % m, cb), cb=cb, m=m)
        elif kind == "clamp":
            imap = functools.partial(lambda i, cb, m: (jnp.minimum(i, m - 1), cb), cb=cb, m=m)
        else:
            imap = functools.partial(lambda i, cb: (i, cb), cb=cb)
        arrays.append(arr)
        in_specs.append(pl.BlockSpec((tr, w), imap))
    for v in vec_ins:
        s, a, w = v.shape
        imap = (lambda i: (sel(i), 0, 0)) if s == 2 else (lambda i: (0, 0, 0))
        arrays.append(v)
        in_specs.append(pl.BlockSpec((1, a, w), imap))
    out_shapes, out_specs = [], []
    lat_only = [len(spec) > 2 for spec in row_outs]
    for (w, dt), lat in zip([spec[:2] for spec in row_outs], lat_only):
        out_shapes.append(jax.ShapeDtypeStruct((n_lat if lat else n_rows, w), dt))
        out_specs.append(pl.BlockSpec((tr, w), (lambda i: (jnp.minimum(i, nlt - 1), 0)) if lat else (lambda i: (i, 0))))
    for s, w in red_outs:
        out_shapes.append(jax.ShapeDtypeStruct((s, 1, w), F32))
        imap = (lambda i: (sel(i), 0, 0)) if s == 2 else (lambda i: (0, 0, 0))
        out_specs.append(pl.BlockSpec((1, 1, w), imap))
    n_ri, n_vi, n_ro, n_rd = len(row_ins) + len(pairs), len(vec_ins), len(row_outs), len(red_outs)

    def body(*refs):
        i = pl.program_id(0)
        rows, k = [], 0
        while k < n_ri:
            if k in pairs:
                rows.append(jnp.where(i < nlt, refs[k][...], refs[k + 1][...]).astype(F32))
                k += 2
            else:
                rows.append(refs[k][...].astype(F32))
                k += 1
        vecs = [r[0] for r in refs[n_ri:n_ri + n_vi]]
        outs = refs[n_ri + n_vi:]
        lead = [jnp.where(i < nlt, 1.0, 0.0).astype(F32)] if want_flag else []
        res = fn(*lead, *rows, *vecs)
        if not isinstance(res, (tuple, list)):
            res = (res,)
        assert len(res) == n_ro + n_rd
        for k in range(n_ro):
            if lat_only[k]:
                @pl.when(i < nlt)
                def _(k=k):
                    outs[k][...] = res[k].astype(outs[k].dtype)
            else:
                outs[k][...] = res[k].astype(outs[k].dtype)
        for k in range(n_rd):
            part = jnp.sum(res[n_ro + k].astype(F32), axis=0, keepdims=True)
            first = i == 0
            if red_outs[k][0] == 2:
                first = jnp.logical_or(first, i == nlt)
            o = outs[n_ro + k]

            @pl.when(first)
            def _():
                o[0] = part

            @pl.when(jnp.logical_not(first))
            def _():
                o[0] = o[0] + part

    res = pl.pallas_call(
        body, grid=(nt,), in_specs=in_specs, out_specs=out_specs, out_shape=out_shapes,
        compiler_params=_params("arbitrary"), name=name)(*arrays)
    return res


def _vec(v):
    v = v.astype(F32)
    if v.ndim == 1:
        v = v[None]
    return v[:, None, :]


def _mm(name, a, b, mode, out_dtype=F32):
    if mode in ("nn", "nt"):
        m, k = a.shape
        n = b.shape[1] if mode == "nn" else b.shape[0]
        tm = _largest_tile(m, 1024, 8)
        tn = _largest_tile(n, 1024, 128)
        contract = (((1,), (0,)), ((), ())) if mode == "nn" else (((1,), (1,)), ((), ()))

        def body(a_ref, b_ref, o_ref):
            o_ref[...] = lax.dot_general(a_ref[...].astype(BF16), b_ref[...].astype(BF16), contract,
                                         preferred_element_type=F32).astype(o_ref.dtype)

        b_spec = pl.BlockSpec((k, tn), lambda i, j: (0, j)) if mode == "nn" else pl.BlockSpec((tn, k), lambda i, j: (j, 0))
        return pl.pallas_call(
            body, grid=(m // tm, n // tn),
            in_specs=[pl.BlockSpec((tm, k), lambda i, j: (i, 0)), b_spec],
            out_specs=pl.BlockSpec((tm, tn), lambda i, j: (i, j)),
            out_shape=jax.ShapeDtypeStruct((m, n), out_dtype),
            compiler_params=_params("parallel", "arbitrary"), name=name)(a, b)
    assert mode == "tn"
    r, k1 = a.shape
    k2 = b.shape[1]
    tr = _largest_tile(r, 1024, 8)
    t2 = _largest_tile(k2, 1024, 128)
    nr = r // tr

    def body(a_ref, b_ref, o_ref, acc_ref):
        part = lax.dot_general(a_ref[...].astype(BF16), b_ref[...].astype(BF16), (((0,), (0,)), ((), ())),
                               preferred_element_type=F32)
        i = pl.program_id(1)

        @pl.when(i == 0)
        def _():
            acc_ref[...] = part

        @pl.when(i > 0)
        def _():
            acc_ref[...] += part

        @pl.when(i == nr - 1)
        def _():
            o_ref[...] = acc_ref[...].astype(o_ref.dtype)

    return pl.pallas_call(
        body, grid=(k2 // t2, nr),
        in_specs=[pl.BlockSpec((tr, k1), lambda j, i: (i, 0)), pl.BlockSpec((tr, t2), lambda j, i: (i, j))],
        out_specs=pl.BlockSpec((k1, t2), lambda j, i: (0, j)),
        out_shape=jax.ShapeDtypeStruct((k1, k2), out_dtype),
        scratch_shapes=[pltpu.VMEM((k1, t2), F32)],
        compiler_params=_params("parallel", "arbitrary"), name=name)(a, b)


def _exchange(name, x, bcast, sibling_only=False):
    rels = [1] if sibling_only else list(range(1, N_DEV))
    n_slot = 2 if sibling_only else N_DEV
    blk = x.shape if bcast else x.shape[1:]

    def body(x_ref, o_ref, send_sems, recv_sems, local_sem):
        mx, my, mc = lax.axis_index("x"), lax.axis_index("y"), lax.axis_index("c")
        me = mc if sibling_only else 4 * mx + 2 * my + mc
        me_dev = 4 * mx + 2 * my + mc
        mine = pltpu.make_async_copy(x_ref if bcast else x_ref.at[me_dev], o_ref.at[me], local_sem)
        mine.start()
        copies = []
        for k, r in enumerate(rels):
            px = 1 - mx if (r >> 2) & 1 else mx
            py = 1 - my if (r >> 1) & 1 else my
            pc = 1 - mc if r & 1 else mc
            src = x_ref if bcast else x_ref.at[4 * px + 2 * py + pc]
            cp = pltpu.make_async_remote_copy(
                src_ref=src, dst_ref=o_ref.at[me], send_sem=send_sems.at[k], recv_sem=recv_sems.at[k],
                device_id=(px, py, pc), device_id_type=pl.DeviceIdType.MESH)
            cp.start()
            copies.append(cp)
        for cp in copies:
            cp.wait()
        mine.wait()

    return pl.pallas_call(
        body, out_shape=jax.ShapeDtypeStruct((n_slot,) + tuple(blk), x.dtype),
        in_specs=[pl.BlockSpec(memory_space=pltpu.VMEM if sibling_only else pl.ANY)],
        out_specs=pl.BlockSpec(memory_space=pl.ANY),
        scratch_shapes=[pltpu.SemaphoreType.DMA((len(rels),)), pltpu.SemaphoreType.DMA((len(rels),)),
                        pltpu.SemaphoreType.DMA],
        name=name)(x)


def _gather_two_level(name, x):
    def body(x_ref, o_ref, send_sems, recv_sems, local_sem):
        mx, my, mc = lax.axis_index("x"), lax.axis_index("y"), lax.axis_index("c")
        me, sibling = (mx, my, mc), (mx, my, 1 - mc)
        chips = [(1 - mx, my), (mx, 1 - my), (1 - mx, 1 - my)]

        def slot(px, py, pc):
            return o_ref.at[4 * px + 2 * py + pc]

        def copy(k, block, to, src=None):
            return pltpu.make_async_remote_copy(
                src_ref=slot(*block) if src is None else src, dst_ref=slot(*block), send_sem=send_sems.at[k],
                recv_sem=recv_sems.at[k], device_id=to, device_id_type=pl.DeviceIdType.MESH)

        mine = pltpu.make_async_copy(x_ref, slot(*me), local_sem)
        mine.start()
        first = [copy(0, me, sibling, src=x_ref)]
        first += [copy(1 + j, me, (*chip, mc), src=x_ref) for j, chip in enumerate(chips)]
        for cp in first:
            cp.start()
        passed = [copy(4 + j, (*chip, mc), sibling) for j, chip in enumerate(chips)]
        for j, chip in enumerate(chips):
            copy(1 + j, (*chip, mc), me).wait_recv()
            passed[j].start()
        copy(0, sibling, me).wait_recv()
        for j, chip in enumerate(chips):
            copy(4 + j, (*chip, 1 - mc), me).wait_recv()
        for cp in first + passed:
            cp.wait_send()
        mine.wait()

    return pl.pallas_call(
        body, out_shape=jax.ShapeDtypeStruct((N_DEV,) + tuple(x.shape), x.dtype),
        in_specs=[pl.BlockSpec(memory_space=pl.ANY)], out_specs=pl.BlockSpec(memory_space=pl.ANY),
        scratch_shapes=[pltpu.SemaphoreType.DMA((N_DEV - 1,)), pltpu.SemaphoreType.DMA((N_DEV - 1,)),
                        pltpu.SemaphoreType.DMA],
        name=name)(x)


def _sum_slots(name, x):
    s, r, w = x.shape
    tr = _largest_tile(r, 256, 8)

    def body(x_ref, o_ref):
        acc = x_ref[0].astype(F32)
        for j in range(1, s):
            acc = acc + x_ref[j].astype(F32)
        o_ref[...] = acc

    return pl.pallas_call(
        body, grid=(r // tr,), in_specs=[pl.BlockSpec((s, tr, w), lambda i: (0, i, 0))],
        out_specs=pl.BlockSpec((tr, w), lambda i: (i, 0)), out_shape=jax.ShapeDtypeStruct((r, w), F32),
        compiler_params=_params("parallel"), name=name)(x)


def _sigmoid(x):
    return 1.0 / (1.0 + jnp.exp(-x))


def _silu(x):
    return x * _sigmoid(x)


def _silu_grad(x):
    s = _sigmoid(x)
    return s * (1.0 + x * (1.0 - s))


_INV_SQRT2 = 1.0 / math.sqrt(2.0)
_INV_SQRT2PI = 1.0 / math.sqrt(2.0 * math.pi)


def _gelu(x):
    return 0.5 * x * (1.0 + lax.erf(x * _INV_SQRT2))


def _gelu_grad(x):
    return 0.5 * (1.0 + lax.erf(x * _INV_SQRT2)) + x * jnp.exp(-0.5 * x * x) * _INV_SQRT2PI


def _rms_hat(x):
    r = lax.rsqrt(jnp.mean(x * x, axis=-1, keepdims=True) + NORM_EPS)
    return x * r, r


def _rms_bwd(xh, r, dxh):
    return r * (dxh - xh * jnp.mean(dxh * xh, axis=-1, keepdims=True))


def _stream(x):
    return (x[0], 0, D_MODEL, "pair", x[1]) if isinstance(x, tuple) else (x, 0, D_MODEL)


def _norm_mod_fwd(name, x, g, scale, shift, n_rows, tr, n_lat):
    def fn(xt, gv, sc, sh):
        xh, _ = _rms_hat(xt)
        return (xh * gv) * (1.0 + sc) + sh

    return _rowwise(name, fn, n_rows, tr, [_stream(x)], [g, scale, shift], [(D_MODEL, BF16)], [], n_lat=n_lat)[0]


def _norm_mod_bwd(name, x, g, scale, dh, dres, n_rows, tr, n_lat, prev=None, dx_lat_only=False):
    nlt = n_lat // tr

    def fn(flag, xt, dht, drt, *rest):
        gv, sc = rest[-2:] if prev is None else rest[1:3]
        xh, r = _rms_hat(xt)
        n = xh * gv
        dn = dht * (1.0 + sc)
        dx = _rms_bwd(xh, r, dn * gv) + flag * drt
        if prev is None:
            return dx, dn * xh, dht * n, dht
        return dx, rest[3] * dx, dn * xh, dht * n, dht, dx * rest[0]

    rows = [_stream(x), (dh, 0, D_MODEL), (dres, 0, D_MODEL, "clamp", nlt)]
    row_outs = [(D_MODEL, F32, "lat") if dx_lat_only else (D_MODEL, F32)]
    vecs, reds = [g, scale], [(1, D_MODEL), (2, D_MODEL), (2, D_MODEL)]
    if prev is not None:
        rows.append((prev[0], 0, D_MODEL))
        vecs.append(prev[1])
        row_outs.append((D_MODEL, BF16))
        reds.append((2, D_MODEL))
    return _rowwise(name, fn, n_rows, tr, rows, vecs, row_outs, reds, n_lat=n_lat, want_flag=True)


def _s5_prep(a_re, a_im, log_dt, b_re, b_im, seg_lat, seg_ctx):
    def body(ar_ref, ai_ref, ld_ref, br_ref, bi_ref, abr_ref, abi_ref, bbr_ref, bbi_ref, alr_ref, ali_ref, acr_ref,
             aci_ref):
        lr, li = ar_ref[...], ai_ref[...]
        dt = jnp.exp(ld_ref[...])
        ldr, ldi = lr * dt, li * dt
        e = jnp.exp(ldr)
        abr, abi = e * jnp.cos(ldi), e * jnp.sin(ldi)
        abr_ref[...] = abr
        abi_ref[...] = abi
        den = lr * lr + li * li
        nr, ni = abr - 1.0, abi
        qr = (nr * lr + ni * li) / den
        qi = (ni * lr - nr * li) / den
        br, bi = br_ref[...], bi_ref[...]
        bbr_ref[...] = qr[None] * br - qi[None] * bi
        bbi_ref[...] = qr[None] * bi + qi[None] * br
        for seg, r_ref, i_ref in ((seg_lat, alr_ref, ali_ref), (seg_ctx, acr_ref, aci_ref)):
            es = jnp.exp(ldr * float(seg))
            r_ref[...] = es * jnp.cos(ldi * float(seg))
            i_ref[...] = es * jnp.sin(ldi * float(seg))

    sm = jax.ShapeDtypeStruct(a_re.shape, F32)
    big = jax.ShapeDtypeStruct(b_re.shape, F32)
    return pl.pallas_call(body, out_shape=[sm, sm, big, big, sm, sm, sm, sm], name="s5_prep")(
        a_re, a_im, log_dt, b_re, b_im)


def _s5_prep_bwd(a_re, a_im, log_dt, b_re, b_im, dabr, dabi, dbbr, dbbi):
    def body(ar_ref, ai_ref, ld_ref, br_ref, bi_ref, dabr_ref, dabi_ref, dbbr_ref, dbbi_ref,
             dar_ref, dai_ref, dld_ref, dbr_ref, dbi_ref):
        lr, li = ar_ref[...], ai_ref[...]
        dt = jnp.exp(ld_ref[...])
        ldr, ldi = lr * dt, li * dt
        e = jnp.exp(ldr)
        abr, abi = e * jnp.cos(ldi), e * jnp.sin(ldi)
        den = lr * lr + li * li
        nr, ni = abr - 1.0, abi
        qr = (nr * lr + ni * li) / den
        qi = (ni * lr - nr * li) / den
        br, bi = br_ref[...], bi_ref[...]
        gbr, gbi = dbbr_ref[...], dbbi_ref[...]
        dbr_ref[...] = gbr * qr[None] + gbi * qi[None]
        dbi_ref[...] = gbi * qr[None] - gbr * qi[None]
        dqr = jnp.sum(gbr * br + gbi * bi, axis=0)
        dqi = jnp.sum(gbi * br - gbr * bi, axis=0)
        dnr = (dqr * lr - dqi * li) / den
        dni = (dqr * li + dqi * lr) / den
        dlr_q = (dqr * (nr - 2.0 * lr * qr) + dqi * (ni - 2.0 * lr * qi)) / den
        dli_q = (dqr * (ni - 2.0 * li * qr) + dqi * (-nr - 2.0 * li * qi)) / den
        gar = dabr_ref[...] + dnr
        gai = dabi_ref[...] + dni
        dldr = gar * abr + gai * abi
        dldi = gai * abr - gar * abi
        dar_ref[...] = dldr * dt + dlr_q
        dai_ref[...] = dldi * dt + dli_q
        ddt = jnp.sum(dldr * lr + dldi * li, axis=1, keepdims=True)
        dld_ref[...] = ddt * dt

    sm = jax.ShapeDtypeStruct(a_re.shape, F32)
    big = jax.ShapeDtypeStruct(b_re.shape, F32)
    return pl.pallas_call(body, out_shape=[sm, sm, jax.ShapeDtypeStruct(log_dt.shape, F32), big, big],
                          name="s5_prep_bwd")(a_re, a_im, log_dt, b_re, b_im, dabr, dabi, dbbr, dbbi)


def _slab_cols(v):
    return v.reshape(N_SLAB, 1, HALF_W)


def _slab_pair(vr, vi):
    return jnp.concatenate([_slab_cols(vr), _slab_cols(vi)], axis=-1)


def _slab_in_matrix(bbr, bbi):
    eye = jnp.eye(SLAB_GROUPS, dtype=F32)

    def one(b):
        b = b.reshape(N_SLAB, SLAB_GROUPS, SSM_STATE, SSM_GROUP)
        m = jnp.einsum("sgph,gk->sghkp", b, eye)
        return m.reshape(N_SLAB, SLAB_W, HALF_W)

    return jnp.concatenate([one(bbr), one(bbi)], axis=-1)


def _slab_out_matrix(cr, ci):
    eye = jnp.eye(SLAB_GROUPS, dtype=F32)

    def one(c):
        c = c.reshape(N_SLAB, SLAB_GROUPS, SSM_GROUP, SSM_STATE)
        m = jnp.einsum("sghp,gk->skpgh", c, eye)
        return m.reshape(N_SLAB, HALF_W, SLAB_W)

    return jnp.concatenate([one(cr), one(-ci)], axis=1)


def _slab_diag(m):
    m = m.reshape(N_SLAB, SLAB_GROUPS, SSM_GROUP, 2, SLAB_GROUPS, SSM_STATE)
    d = jnp.stack([m[:, g, :, :, g, :] for g in range(SLAB_GROUPS)], axis=1)
    return d.transpose(3, 0, 1, 2, 4).reshape(2, SSM_GROUPS, SSM_GROUP, SSM_STATE)


def _cmul(ar, ai, xr, xi, conj):
    if conj:
        return ar * xr + ai * xi, ar * xi - ai * xr
    return ar * xr - ai * xi, ar * xi + ai * xr


def _s5_pow_table(name, abar, seg, falling, conj):
    assert seg >= 8 and seg & (seg - 1) == 0

    def body(a_ref, o_ref, t_ref):
        ar, ai = a_ref[0, :, :HALF_W], a_ref[0, :, HALF_W:]
        if conj:
            ai = -ai
        rr, ri = [jnp.ones_like(ar)], [jnp.zeros_like(ai)]
        for _ in range(7):
            pr, pi = _cmul(ar, ai, rr[-1], ri[-1], False)
            rr.append(pr)
            ri.append(pi)
        sr, si = _cmul(ar, ai, rr[-1], ri[-1], False)
        if falling:
            rr, ri = rr[::-1], ri[::-1]
        first = slice(seg - 8, seg) if falling else slice(0, 8)
        t_ref[first, :HALF_W] = jnp.concatenate(rr, axis=0)
        t_ref[first, HALF_W:] = jnp.concatenate(ri, axis=0)
        size = 8
        while size < seg:
            src = slice(seg - size, seg) if falling else slice(0, size)
            dst = slice(seg - 2 * size, seg - size) if falling else slice(size, 2 * size)
            pr, pi = _cmul(sr, si, t_ref[src, :HALF_W], t_ref[src, HALF_W:], False)
            t_ref[dst, :HALF_W] = pr
            t_ref[dst, HALF_W:] = pi
            sr, si = _cmul(sr, si, sr, si, False)
            size *= 2
        o_ref[0] = t_ref[...].astype(BF16)

    return pl.pallas_call(
        body, grid=(N_SLAB,), in_specs=[pl.BlockSpec((1, 1, STATE_W), lambda s: (s, 0, 0))],
        out_specs=pl.BlockSpec((1, seg, STATE_W), lambda s: (s, 0, 0)),
        out_shape=jax.ShapeDtypeStruct((N_SLAB, seg, STATE_W), BF16),
        scratch_shapes=[pltpu.VMEM((seg, STATE_W), F32)], compiler_params=_params("parallel"), name=name)(abar)


def _s5_ends(name, x, n_rows, row0, table, m_mat):
    seg = n_rows // N_SEG
    rb = row0 // n_rows
    tn = (((0,), (0,)), ((), ()))

    def body(x_ref, t_ref, m_ref, z_ref):
        mr, mi = m_ref[0, :, :HALF_W], m_ref[0, :, HALF_W:]
        for j in range(N_SEG):
            t = lax.dot_general(x_ref[j * seg:(j + 1) * seg, :].astype(BF16), t_ref[0], tn,
                                preferred_element_type=F32)
            tr_, ti_ = t[:, :HALF_W], t[:, HALF_W:]
            z_ref[0, j:j + 1, :HALF_W] = jnp.sum(mr * tr_ - mi * ti_, axis=0, keepdims=True)
            z_ref[0, j:j + 1, HALF_W:] = jnp.sum(mr * ti_ + mi * tr_, axis=0, keepdims=True)

    return pl.pallas_call(
        body, grid=(N_SLAB,),
        in_specs=[pl.BlockSpec((n_rows, SLAB_W), lambda s: (rb, s)),
                  pl.BlockSpec((1, seg, STATE_W), lambda s: (s, 0, 0)),
                  pl.BlockSpec((1, SLAB_W, STATE_W), lambda s: (s, 0, 0))],
        out_specs=pl.BlockSpec((1, N_SEG, STATE_W), lambda s: (s, 0, 0)),
        out_shape=jax.ShapeDtypeStruct((N_SLAB, N_SEG, STATE_W), F32),
        compiler_params=_params("parallel"), name=name)(x, table, m_mat)


def _s5_carry(name, z, a_seg, init, descending, conj):
    order = list(range(N_SEG - 1, -1, -1)) if descending else list(range(N_SEG))

    def body(z_ref, a_ref, i_ref, e_ref, o_ref):
        ar, ai = a_ref[:, :HALF_W], a_ref[:, HALF_W:]
        cr, ci = i_ref[:, :HALF_W], i_ref[:, HALF_W:]
        for j in order:
            e_ref[:, j, :HALF_W] = cr
            e_ref[:, j, HALF_W:] = ci
            pr, pi = _cmul(ar, ai, cr, ci, conj)
            cr = pr + z_ref[:, j, :HALF_W]
            ci = pi + z_ref[:, j, HALF_W:]
        o_ref[:, :HALF_W] = cr
        o_ref[:, HALF_W:] = ci

    return pl.pallas_call(body, out_shape=[jax.ShapeDtypeStruct(z.shape, F32), jax.ShapeDtypeStruct(init.shape, F32)],
                          name=name)(z, a_seg, init)


def _s5_scan(name, u, n_rows, row0, b_mat, c_mat, abar, h_in, descending, y_alias=None, y_rows=None):
    seg = n_rows // N_SEG
    ta = min(256, seg)
    nk = seg // ta
    assert seg * N_SEG == n_rows and nk * ta == seg and row0 % n_rows == 0 and ta % 8 == 0
    rb = row0 // n_rows
    tile = ta * N_SEG

    def body(*refs):
        u_ref, b_ref, c_ref, a_ref, hin_ref = refs[:5]
        y_ref, hch_ref, st_ref, up_ref, h_ref = refs[-5:]
        k = pl.program_id(1)
        kk = nk - 1 - k if descending else k
        a0 = kk * ta

        @pl.when(k == 0)
        def _():
            st_ref[...] = hin_ref[0]

        hch_ref[0, 0] = st_ref[...]
        for al in range(ta):
            up_ref[al * N_SEG:(al + 1) * N_SEG, :] = u_ref[pl.ds(a0 + al, N_SEG, stride=seg), :]
        h_ref[...] = jnp.dot(up_ref[...].astype(BF16), b_ref[0], preferred_element_type=F32)
        ar = jnp.broadcast_to(a_ref[0, :, :HALF_W], (N_SEG, HALF_W))
        ai = jnp.broadcast_to(a_ref[0, :, HALF_W:], (N_SEG, HALF_W))

        def step(i, carry):
            hr, hi = carry
            al = ta - 1 - i if descending else i
            row = pl.multiple_of(al * N_SEG, N_SEG)
            pr, pi = _cmul(ar, ai, hr, hi, False)
            hr = pr + h_ref[pl.ds(row, N_SEG), :HALF_W]
            hi = pi + h_ref[pl.ds(row, N_SEG), HALF_W:]
            h_ref[pl.ds(row, N_SEG), :HALF_W] = hr
            h_ref[pl.ds(row, N_SEG), HALF_W:] = hi
            return hr, hi

        hr, hi = lax.fori_loop(0, ta, step, (st_ref[:, :HALF_W], st_ref[:, HALF_W:]), unroll=True)
        st_ref[:, :HALF_W] = hr
        st_ref[:, HALF_W:] = hi
        yt = jnp.dot(h_ref[...].astype(BF16), c_ref[0], preferred_element_type=F32)
        for al in range(ta):
            y_ref[pl.ds(a0 + al, N_SEG, stride=seg), :] = yt[al * N_SEG:(al + 1) * N_SEG, :]

    u_spec = pl.BlockSpec((n_rows, SLAB_W), lambda s, k: (rb, s))
    b_spec = pl.BlockSpec((1, SLAB_W, STATE_W), lambda s, k: (s, 0, 0))
    c_spec = pl.BlockSpec((1, STATE_W, SLAB_W), lambda s, k: (s, 0, 0))
    a_spec = pl.BlockSpec((1, 1, STATE_W), lambda s, k: (s, 0, 0))
    st_spec = pl.BlockSpec((1, N_SEG, STATE_W), lambda s, k: (s, 0, 0))
    scratch = [pltpu.VMEM((N_SEG, STATE_W), F32), pltpu.VMEM((tile, SLAB_W), F32), pltpu.VMEM((tile, STATE_W), F32)]
    kmap = (lambda s, k: (s, nk - 1 - k, 0, 0)) if descending else (lambda s, k: (s, k, 0, 0))
    out_specs = [u_spec, pl.BlockSpec((1, 1, N_SEG, STATE_W), kmap)]
    out_shape = [jax.ShapeDtypeStruct((y_rows, D_MODEL), F32), jax.ShapeDtypeStruct((N_SLAB, nk, N_SEG, STATE_W), F32)]
    in_specs = [u_spec, b_spec, c_spec, a_spec, st_spec]
    args = [u, b_mat, c_mat, abar, h_in]
    aliases = {}
    if y_alias is not None:
        in_specs.append(pl.BlockSpec(memory_space=pl.ANY))
        args.append(y_alias)
        aliases = {5: 0}
    return pl.pallas_call(
        body, grid=(N_SLAB, nk), in_specs=in_specs, out_specs=out_specs, out_shape=out_shape, scratch_shapes=scratch,
        input_output_aliases=aliases, compiler_params=_params("parallel", "arbitrary"), name=name)(*args)


def _s5_scan_bwd(name, u, dy, n_rows, row0, b_mat, bt_mat, ct_mat, abar, h_chunks, g_in, descending,
                 du_alias=None, du_rows=None):
    seg = n_rows // N_SEG
    ta = min(256, seg)
    nk = seg // ta
    rb = row0 // n_rows
    tile = ta * N_SEG
    g_desc = not descending

    def body(*refs):
        u_ref, dy_ref, b_ref, bt_ref, ct_ref, a_ref, hch_ref, gin_ref = refs[:8]
        du_ref, db_ref, dc_ref, da_ref, st_ref, up_ref, dyp_ref, h_ref, g_ref = refs[-9:]
        k = pl.program_id(1)
        kk = nk - 1 - k if g_desc else k
        a0 = kk * ta
        ar = jnp.broadcast_to(a_ref[0, :, :HALF_W], (N_SEG, HALF_W))
        ai = jnp.broadcast_to(a_ref[0, :, HALF_W:], (N_SEG, HALF_W))

        @pl.when(k == 0)
        def _():
            st_ref[...] = gin_ref[0]

        for al in range(ta):
            dyp_ref[al * N_SEG:(al + 1) * N_SEG, :] = dy_ref[pl.ds(a0 + al, N_SEG, stride=seg), :]
            up_ref[al * N_SEG:(al + 1) * N_SEG, :] = u_ref[pl.ds(a0 + al, N_SEG, stride=seg), :]
        g_ref[...] = jnp.dot(dyp_ref[...].astype(BF16), ct_ref[0], preferred_element_type=F32)
        h_ref[...] = jnp.dot(up_ref[...].astype(BF16), b_ref[0], preferred_element_type=F32)
        h0r, h0i = hch_ref[0, 0, :, :HALF_W], hch_ref[0, 0, :, HALF_W:]

        def hstep(i, carry):
            hr, hi = carry
            al = ta - 1 - i if descending else i
            row = pl.multiple_of(al * N_SEG, N_SEG)
            pr, pi = _cmul(ar, ai, hr, hi, False)
            hr = pr + h_ref[pl.ds(row, N_SEG), :HALF_W]
            hi = pi + h_ref[pl.ds(row, N_SEG), HALF_W:]
            h_ref[pl.ds(row, N_SEG), :HALF_W] = hr
            h_ref[pl.ds(row, N_SEG), HALF_W:] = hi
            return hr, hi

        lax.fori_loop(0, ta, hstep, (h0r, h0i), unroll=True)

        def gstep(i, carry):
            gr, gi = carry
            al = ta - 1 - i if g_desc else i
            row = pl.multiple_of(al * N_SEG, N_SEG)
            pr, pi = _cmul(ar, ai, gr, gi, True)
            gr = pr + g_ref[pl.ds(row, N_SEG), :HALF_W]
            gi = pi + g_ref[pl.ds(row, N_SEG), HALF_W:]
            g_ref[pl.ds(row, N_SEG), :HALF_W] = gr
            g_ref[pl.ds(row, N_SEG), HALF_W:] = gi
            return gr, gi

        gr, gi = lax.fori_loop(0, ta, gstep, (st_ref[:, :HALF_W], st_ref[:, HALF_W:]), unroll=True)
        st_ref[:, :HALF_W] = gr
        st_ref[:, HALF_W:] = gi

        gb = g_ref[...].astype(BF16)
        dut = jnp.dot(gb, bt_ref[0], preferred_element_type=F32)
        for al in range(ta):
            du_ref[pl.ds(a0 + al, N_SEG, stride=seg), :] = dut[al * N_SEG:(al + 1) * N_SEG, :]
        tn = (((0,), (0,)), ((), ()))
        dbp = lax.dot_general(up_ref[...].astype(BF16), gb, tn, preferred_element_type=F32)
        dcp = lax.dot_general(dyp_ref[...].astype(BF16), h_ref[...].astype(BF16), tn, preferred_element_type=F32)
        inner = (ta - 1) * N_SEG
        if descending:
            g_in_r, g_in_i = g_ref[0:inner, :HALF_W], g_ref[0:inner, HALF_W:]
            p_in_r, p_in_i = h_ref[N_SEG:tile, :HALF_W], h_ref[N_SEG:tile, HALF_W:]
            g_ed_r, g_ed_i = g_ref[inner:tile, :HALF_W], g_ref[inner:tile, HALF_W:]
        else:
            g_in_r, g_in_i = g_ref[N_SEG:tile, :HALF_W], g_ref[N_SEG:tile, HALF_W:]
            p_in_r, p_in_i = h_ref[0:inner, :HALF_W], h_ref[0:inner, HALF_W:]
            g_ed_r, g_ed_i = g_ref[0:N_SEG, :HALF_W], g_ref[0:N_SEG, HALF_W:]
        dar = g_ed_r * h0r + g_ed_i * h0i
        dai = g_ed_i * h0r - g_ed_r * h0i
        if ta > 1:
            dar = dar + jnp.sum((g_in_r * p_in_r + g_in_i * p_in_i).reshape(ta - 1, N_SEG, HALF_W), axis=0)
            dai = dai + jnp.sum((g_in_i * p_in_r - g_in_r * p_in_i).reshape(ta - 1, N_SEG, HALF_W), axis=0)

        @pl.when(k == 0)
        def _():
            db_ref[0] = dbp
            dc_ref[0] = dcp
            da_ref[0, :, :HALF_W] = dar
            da_ref[0, :, HALF_W:] = dai

        @pl.when(k > 0)
        def _():
            db_ref[0] += dbp
            dc_ref[0] += dcp
            da_ref[0, :, :HALF_W] += dar
            da_ref[0, :, HALF_W:] += dai

    u_spec = pl.BlockSpec((n_rows, SLAB_W), lambda s, k: (rb, s))
    m_spec = pl.BlockSpec((1, SLAB_W, STATE_W), lambda s, k: (s, 0, 0))
    mt_spec = pl.BlockSpec((1, STATE_W, SLAB_W), lambda s, k: (s, 0, 0))
    a_spec = pl.BlockSpec((1, 1, STATE_W), lambda s, k: (s, 0, 0))
    st_spec = pl.BlockSpec((1, N_SEG, STATE_W), lambda s, k: (s, 0, 0))
    st_shape = jax.ShapeDtypeStruct((N_SLAB, N_SEG, STATE_W), F32)
    kmap = (lambda s, k: (s, nk - 1 - k, 0, 0)) if g_desc else (lambda s, k: (s, k, 0, 0))
    in_specs = [u_spec, u_spec, m_spec, mt_spec, m_spec, a_spec, pl.BlockSpec((1, 1, N_SEG, STATE_W), kmap), st_spec]
    args = [u, dy, b_mat, bt_mat, ct_mat, abar, h_chunks, g_in]
    aliases = {}
    if du_alias is not None:
        in_specs.append(pl.BlockSpec(memory_space=pl.ANY))
        args.append(du_alias)
        aliases = {8: 0}
    acc_shape = jax.ShapeDtypeStruct((N_SLAB, SLAB_W, STATE_W), F32)
    out_specs = [u_spec, m_spec, m_spec, st_spec]
    out_shape = [jax.ShapeDtypeStruct((du_rows, D_MODEL), F32), acc_shape, acc_shape, st_shape]
    scratch = [pltpu.VMEM((N_SEG, STATE_W), F32), pltpu.VMEM((tile, SLAB_W), F32), pltpu.VMEM((tile, SLAB_W), F32),
               pltpu.VMEM((tile, STATE_W), F32), pltpu.VMEM((tile, STATE_W), F32)]
    return pl.pallas_call(
        body, grid=(N_SLAB, nk), in_specs=in_specs, out_specs=out_specs, out_shape=out_shape, scratch_shapes=scratch,
        input_output_aliases=aliases, compiler_params=_params("parallel", "arbitrary"), name=name)(*args)


ROPE_HALF = HEAD_DIM // 4
TABLE_W = 2 * HEAD_DIM
Q_SCALE = 1.0 / math.sqrt(HEAD_DIM)
HEADS_PER_BLOCK = 2 * KV_REP
Q_BLOCK_W = HEADS_PER_BLOCK * HEAD_DIM


def _rope_tables(n_lat, n_ctx):
    rows = n_lat // GRID_W
    freqs = ROPE_THETA ** (-jnp.arange(ROPE_HALF, dtype=F32) / ROPE_HALF)
    ang_r = jnp.arange(rows, dtype=F32)[:, None] * freqs[None]
    ang_c = jnp.arange(GRID_W, dtype=F32)[:, None] * freqs[None]
    by_row = lambda v: jnp.repeat(v, GRID_W, axis=0)
    by_col = lambda v: jnp.tile(v, (rows, 1))
    cos = jnp.concatenate([by_row(jnp.cos(ang_r)), by_row(jnp.cos(ang_r)), by_col(jnp.cos(ang_c)), by_col(jnp.cos(ang_c))] * 2,
                          axis=1)
    sin = jnp.concatenate([by_row(jnp.sin(ang_r)), by_row(jnp.sin(ang_r)), by_col(jnp.sin(ang_c)), by_col(jnp.sin(ang_c))] * 2,
                          axis=1)
    cos = jnp.concatenate([cos, jnp.ones((n_ctx, TABLE_W), F32)], axis=0)
    sin = jnp.concatenate([sin, jnp.zeros((n_ctx, TABLE_W), F32)], axis=0)
    return cos, sin


def _rot_half(v):
    w = v.shape[1]
    ahead = pltpu.roll(v, w - ROPE_HALF, axis=1)
    behind = pltpu.roll(v, ROPE_HALF, axis=1)
    lane = lax.broadcasted_iota(jnp.int32, v.shape, 1)
    return jnp.where((lane % (2 * ROPE_HALF)) < ROPE_HALF, -ahead, behind)


def _head_mean(v, sel, selt):
    m = jnp.dot(v, sel, precision=lax.Precision.HIGH, preferred_element_type=F32) * (1.0 / HEAD_DIM)
    return jnp.dot(m, selt, precision=lax.Precision.HIGH, preferred_element_type=F32)


def _head_selectors(n_heads):
    sel = jnp.repeat(jnp.eye(n_heads, dtype=F32), HEAD_DIM, axis=0)
    return sel[None], sel.T[None]


def _head_norm(x, sel, selt):
    r = lax.rsqrt(_head_mean(x * x, sel, selt) + NORM_EPS)
    return x * r, r


def _qk_prep(proj, qn, kn, cos, sin, n, tr):
    qw, kw = _vec(jnp.tile(qn, N_Q_HEADS)), _vec(jnp.tile(kn, N_KV_HEADS))
    sq, sqt = _head_selectors(N_Q_HEADS)
    sk, skt = _head_selectors(N_KV_HEADS)

    def fn(qr, kvr, ct, st, qwv, kwv, s16, s16t, s4, s4t):
        outs = []
        for x, wv, sel, selt, scale in ((qr, qwv, s16, s16t, Q_SCALE), (kvr[:, :KV_W], kwv, s4, s4t, 1.0)):
            reps = x.shape[1] // TABLE_W
            cw, sw = jnp.tile(ct, (1, reps)), jnp.tile(st, (1, reps))
            xh, _ = _head_norm(x, sel, selt)
            nrm = xh * wv
            outs.append((nrm * cw + _rot_half(nrm) * sw) * scale)
        return outs[0], outs[1], kvr[:, KV_W:]

    return _rowwise("l1_qk_prep", fn, n, tr,
                    [(proj, 0, ATTN_W), (proj, 2 * ATTN_W // (2 * KV_W), 2 * KV_W), (cos, 0, TABLE_W), (sin, 0, TABLE_W)],
                    [qw, kw, sq, sqt, sk, skt], [(ATTN_W, BF16), (KV_W, BF16), (KV_W, BF16)], [])


def _qk_prep_bwd(proj, qn, kn, cos, sin, dq, dz, dk, dv, n, n_lat, tr):
    qw, kw = _vec(jnp.tile(qn, N_Q_HEADS)), _vec(jnp.tile(kn, N_KV_HEADS))
    sq, sqt = _head_selectors(N_Q_HEADS)
    sk, skt = _head_selectors(N_KV_HEADS)
    nlt = n_lat // tr

    def fn(flag, qr, kvr, ct, st, dqt, dzt, dkt, dvt, qwv, kwv, s16, s16t, s4, s4t):
        dxs, dws = [], []
        for x, dy, wv, sel, selt in ((qr, dqt * (flag * Q_SCALE), qwv, s16, s16t), (kvr[:, :KV_W], dkt, kwv, s4, s4t)):
            reps = x.shape[1] // TABLE_W
            cw, sw = jnp.tile(ct, (1, reps)), jnp.tile(st, (1, reps))
            xh, r = _head_norm(x, sel, selt)
            dn = dy * cw - _rot_half(dy * sw)
            dxh = dn * wv
            dxs.append(r * (dxh - xh * _head_mean(dxh * xh, sel, selt)))
            dws.append(dn * xh)
        return jnp.concatenate([dxs[0], dzt * flag, dxs[1], dvt], axis=1), dws[0], dws[1]

    dproj, dqw, dkw = _rowwise(
        "l1_qk_prep_bwd", fn, n, tr,
        [(proj, 0, ATTN_W), (proj, 2 * ATTN_W // (2 * KV_W), 2 * KV_W), (cos, 0, TABLE_W), (sin, 0, TABLE_W),
         (dq, 0, ATTN_W, "clamp", nlt), (dz, 0, ATTN_W, "clamp", nlt), (dk, 0, KV_W), (dv, 0, KV_W)],
        [qw, kw, sq, sqt, sk, skt], [(2 * ATTN_W + 2 * KV_W, BF16)], [(1, ATTN_W), (1, KV_W)], n_lat=n_lat, want_flag=True)
    return dproj, dqw.reshape(N_Q_HEADS, HEAD_DIM).sum(0)[None], dkw.reshape(N_KV_HEADS, HEAD_DIM).sum(0)[None]


NT = (((1,), (1,)), ((), ()))


def _attn_fwd(q, k, v, proj, t, tq, tk):
    n = k.shape[0]
    nkc = n // tk

    ts = _largest_tile(tq, 256, LANES)
    items = [(sub, j) for sub in range(tq // ts) for j in range(HEADS_PER_BLOCK)]

    def body(q_ref, k_ref, v_ref, z_ref, o_ref, lse_ref, gz_ref, s_ref, m_ref, acc_ref):
        def lanes(j):
            g = j // KV_REP
            return slice(j * HEAD_DIM, (j + 1) * HEAD_DIM), slice(g * HEAD_DIM, (g + 1) * HEAD_DIM)

        for idx in range(len(items) + 1):
            nxt = items[idx] if idx < len(items) else None
            cur = items[idx - 1] if idx > 0 else None
            sn, sc = idx % 2, (idx - 1) % 2
            if nxt is not None:
                rows_n = slice(nxt[0] * ts, (nxt[0] + 1) * ts)
                ql_n, kl_n = lanes(nxt[1])
                qv = q_ref[rows_n, ql_n]
                m_ref[sn] = jnp.full((ts, LANES), -jnp.inf, F32)
            if cur is not None:
                rows_c = slice(cur[0] * ts, (cur[0] + 1) * ts)
                ql_c, kl_c = lanes(cur[1])
                m_row = jnp.max(m_ref[sc], axis=-1, keepdims=True)
                acc_ref[...] = jnp.zeros(acc_ref.shape, F32)

            def sweep(kc, c):
                off = pl.multiple_of(kc * tk, tk)
                if nxt is not None:
                    s = lax.dot_general(qv, k_ref[pl.ds(off, tk), kl_n], NT, preferred_element_type=F32)
                    s_ref[sn, :, pl.ds(off, tk)] = s
                    m = m_ref[sn]
                    for cb in range(tk // LANES):
                        m = jnp.maximum(m, s[:, cb * LANES:(cb + 1) * LANES])
                    m_ref[sn] = m
                if cur is not None:
                    p = jnp.exp(s_ref[sc, :, pl.ds(off, tk)] - m_row)
                    v_one = jnp.concatenate([v_ref[pl.ds(off, tk), kl_c], jnp.ones((tk, HEAD_DIM), BF16)], axis=1)
                    acc_ref[...] += jnp.dot(p.astype(BF16), v_one, preferred_element_type=F32)
                return c

            lax.fori_loop(0, nkc, sweep, 0, unroll=True)
            if cur is not None:
                l_row = acc_ref[:, HEAD_DIM:HEAD_DIM + 1]
                o_head = acc_ref[:, :HEAD_DIM] / l_row
                o_ref[rows_c, ql_c] = o_head
                gz_ref[rows_c, ql_c] = (o_head * _silu(z_ref[rows_c, ql_c].astype(F32))).astype(BF16)
                lse_ref[0, rows_c, cur[1]:cur[1] + 1] = m_row + jnp.log(l_row)

    nb = ATTN_W // Q_BLOCK_W
    kspec = pl.BlockSpec((n, LANES), lambda b, i: (0, b))
    return pl.pallas_call(
        body, grid=(nb, t // tq),
        in_specs=[pl.BlockSpec((tq, Q_BLOCK_W), lambda b, i: (i, b)), kspec, kspec,
                  pl.BlockSpec((tq, Q_BLOCK_W), lambda b, i: (i, nb + b))],
        out_specs=[pl.BlockSpec((tq, Q_BLOCK_W), lambda b, i: (i, b)),
                   pl.BlockSpec((1, tq, HEADS_PER_BLOCK), lambda b, i: (b, i, 0)),
                   pl.BlockSpec((tq, Q_BLOCK_W), lambda b, i: (i, b))],
        out_shape=[jax.ShapeDtypeStruct((t, ATTN_W), F32), jax.ShapeDtypeStruct((nb, t, HEADS_PER_BLOCK), F32),
                   jax.ShapeDtypeStruct((t, ATTN_W), BF16)],
        scratch_shapes=[pltpu.VMEM((2, ts, n), F32), pltpu.VMEM((2, ts, LANES), F32), pltpu.VMEM((ts, 2 * HEAD_DIM), F32)],
        compiler_params=_params("parallel", "parallel"), name="attn_fwd")(q, k, v, proj)


def _attn_bwd(q, k, v, dgz, proj, o, lse, t, tq, tk):
    n = k.shape[0]
    nkc = n // tk
    tn = (((0,), (0,)), ((), ()))

    def body(q_ref, k_ref, v_ref, dgz_ref, z_ref, o_ref, lse_ref, dq_ref, dk_ref, dv_ref, dz_ref, acc_ref):
        @pl.when(pl.program_id(1) == 0)
        def _():
            dk_ref[...] = jnp.zeros(dk_ref.shape, F32)
            dv_ref[...] = jnp.zeros(dv_ref.shape, F32)

        for j0 in range(0, HEADS_PER_BLOCK, 2):
            kl = slice((j0 // KV_REP) * HEAD_DIM, (j0 // KV_REP + 1) * HEAD_DIM)
            heads = []
            for a in range(2):
                j = j0 + a
                ql = slice(j * HEAD_DIM, (j + 1) * HEAD_DIM)
                qv, ov = q_ref[:, ql], o_ref[:, ql]
                zv, dgv = z_ref[:, ql].astype(F32), dgz_ref[:, ql].astype(F32)
                dov = (dgv * _silu(zv)).astype(BF16)
                dz_ref[:, ql] = (dgv * ov * _silu_grad(zv)).astype(dz_ref.dtype)
                dl_v = jnp.sum(dov.astype(F32) * ov, axis=-1, keepdims=True)
                heads.append((ql, qv, dov, dl_v, lse_ref[0, :, j:j + 1]))
                acc_ref[a] = jnp.zeros((tq, HEAD_DIM), F32)

            def step(kc, c):
                off = pl.multiple_of(kc * tk, tk)
                kt = k_ref[pl.ds(off, tk), kl]
                vt = v_ref[pl.ds(off, tk), kl]
                dv_part, dk_part = None, None
                for a, (_, qv, dov, dl_v, lse_v) in enumerate(heads):
                    s = lax.dot_general(qv, kt, NT, preferred_element_type=F32)
                    p = jnp.exp(s - lse_v)
                    dp = lax.dot_general(dov, vt, NT, preferred_element_type=F32)
                    ds = (p * (dp - dl_v)).astype(BF16)
                    acc_ref[a] += jnp.dot(ds, kt, preferred_element_type=F32)
                    dvp = lax.dot_general(p.astype(BF16), dov, tn, preferred_element_type=F32)
                    dkp = lax.dot_general(ds, qv, tn, preferred_element_type=F32)
                    dv_part = dvp if dv_part is None else dv_part + dvp
                    dk_part = dkp if dk_part is None else dk_part + dkp
                dv_ref[pl.ds(off, tk), kl] += dv_part
                dk_ref[pl.ds(off, tk), kl] += dk_part
                return c

            lax.fori_loop(0, nkc, step, 0, unroll=2)
            for a, h in enumerate(heads):
                dq_ref[:, h[0]] = acc_ref[a].astype(dq_ref.dtype)

    nb = ATTN_W // Q_BLOCK_W
    qspec = pl.BlockSpec((tq, Q_BLOCK_W), lambda b, i: (i, b))
    kspec = pl.BlockSpec((n, LANES), lambda b, i: (0, b))
    cspec = pl.BlockSpec((1, tq, HEADS_PER_BLOCK), lambda b, i: (b, i, 0))
    return pl.pallas_call(
        body, grid=(nb, t // tq),
        in_specs=[qspec, kspec, kspec, qspec, pl.BlockSpec((tq, Q_BLOCK_W), lambda b, i: (i, nb + b)), qspec, cspec],
        out_specs=[qspec, kspec, kspec, qspec],
        out_shape=[jax.ShapeDtypeStruct((t, ATTN_W), BF16), jax.ShapeDtypeStruct((n, KV_W), F32),
                   jax.ShapeDtypeStruct((n, KV_W), F32), jax.ShapeDtypeStruct((t, ATTN_W), BF16)],
        scratch_shapes=[pltpu.VMEM((2, tq, HEAD_DIM), F32)],
        compiler_params=_params("parallel", "arbitrary"), name="attn_bwd")(q, k, v, dgz, proj, o, lse)


def _s5_system(p, n_lat, n_ctx):
    two_g = 2 * SSM_GROUPS
    a_re = p["ssm_a_re"].reshape(two_g, SSM_STATE)
    a_im = p["ssm_a_im"].reshape(two_g, SSM_STATE)
    log_dt = p["ssm_log_dt"].reshape(two_g, 1)
    b_re = p["ssm_b_re"].reshape(two_g, SSM_STATE, SSM_GROUP).transpose(2, 0, 1)
    b_im = p["ssm_b_im"].reshape(two_g, SSM_STATE, SSM_GROUP).transpose(2, 0, 1)
    raw = (a_re, a_im, log_dt, b_re, b_im)
    abr, abi, bbr, bbi, alr, ali, acr, aci = _s5_prep(*raw, n_lat // N_SEG, n_ctx // N_SEG)
    dirs = []
    for d in range(2):
        g = slice(d * SSM_GROUPS, (d + 1) * SSM_GROUPS)
        b_mat = _slab_in_matrix(bbr[:, g].transpose(1, 2, 0), bbi[:, g].transpose(1, 2, 0))
        c_mat = _slab_out_matrix(p["ssm_c_re"][0, d], p["ssm_c_im"][0, d])
        abar = _slab_pair(abr[g], abi[g])
        tables = {}
        for part, seg in (("lat", n_lat // N_SEG), ("ctx", n_ctx // N_SEG)):
            tables["h_" + part] = _s5_pow_table(f"s5_pow_h{d}_{part}", abar, seg, d == 0, False)
            tables["g_" + part] = _s5_pow_table(f"s5_pow_g{d}_{part}", abar, seg, d == 1, True)
        dirs.append(dict(
            b=b_mat.astype(BF16), bt=b_mat.transpose(0, 2, 1).astype(BF16), b32=b_mat,
            c=c_mat.astype(BF16), ct=c_mat.transpose(0, 2, 1).astype(BF16), ct32=c_mat.transpose(0, 2, 1),
            abar=abar, a_lat=_slab_pair(alr[g], ali[g])[:, 0], a_ctx=_slab_pair(acr[g], aci[g])[:, 0], **tables))
    return raw, dirs


def _s5_forward(proj, dirs, n_lat, n_ctx):
    n = n_lat + n_ctx
    zero_c = jnp.zeros((N_SLAB, STATE_W), F32)
    ys, saved = [], []
    for d, s in enumerate(dirs):
        desc = d == 1
        tag = f"s5f{d}"
        zc = _s5_ends(tag + "_ctx_ends", proj, n_ctx, n_lat, s["h_ctx"], s["b32"])
        ent_c, h0 = _s5_carry(tag + "_ctx_carry", zc, s["a_ctx"], zero_c, desc, False)
        y, hch_c = _s5_scan(tag + "_ctx", proj, n_ctx, n_lat, s["b"], s["c"], s["abar"], ent_c, desc,
                            y_rows=n)
        zl = _s5_ends(tag + "_lat_ends", proj, n_lat, 0, s["h_lat"], s["b32"])
        ent_l, _ = _s5_carry(tag + "_lat_carry", zl, s["a_lat"], h0, desc, False)
        y, hch_l = _s5_scan(tag + "_lat", proj, n_lat, 0, s["b"], s["c"], s["abar"], ent_l, desc,
                            y_alias=y, y_rows=n)
        ys.append(y)
        saved.append((hch_l, hch_c))
    return ys, saved


def _s5_backward(proj, dy, dirs, saved, n_lat, n_ctx):
    n = n_lat + n_ctx
    zero_c = jnp.zeros((N_SLAB, STATE_W), F32)
    out = []
    for d, s in enumerate(dirs):
        desc = d == 1
        tag = f"s5b{d}"
        hch_l, hch_c = saved[d]
        gl = _s5_ends(tag + "_lat_ends", dy, n_lat, 0, s["g_lat"], s["ct32"])
        ent_l, g0 = _s5_carry(tag + "_lat_carry", gl, s["a_lat"], zero_c, not desc, True)
        du, db_l, dc_l, da_l = _s5_scan_bwd(tag + "_lat", proj, dy, n_lat, 0, s["b"], s["bt"], s["ct"], s["abar"],
                                            hch_l, ent_l, desc, du_rows=n)
        gc = _s5_ends(tag + "_ctx_ends", dy, n_ctx, n_lat, s["g_ctx"], s["ct32"])
        ent_c, _ = _s5_carry(tag + "_ctx_carry", gc, s["a_ctx"], g0, not desc, True)
        du, db_c, dc_c, da_c = _s5_scan_bwd(tag + "_ctx", proj, dy, n_ctx, n_lat, s["b"], s["bt"], s["ct"], s["abar"],
                                            hch_c, ent_c, desc, du_alias=du, du_rows=n)
        out.append((du, db_l + db_c, dc_l + dc_c, da_l + da_c))
    return out


def _s5_param_grads(raw, bwd):
    dabr, dabi, dbbr, dbbi, dcr, dci = [], [], [], [], [], []
    for _, db, dc, da in bwd:
        da = jnp.sum(da, axis=1)
        dabr.append(da[:, :HALF_W].reshape(SSM_GROUPS, SSM_STATE))
        dabi.append(da[:, HALF_W:].reshape(SSM_GROUPS, SSM_STATE))
        dbd = _slab_diag(db)
        dbbr.append(dbd[0].transpose(1, 0, 2))
        dbbi.append(dbd[1].transpose(1, 0, 2))
        dcd = _slab_diag(dc)
        dcr.append(dcd[0])
        dci.append(-dcd[1])
    cat = lambda xs, ax: jnp.concatenate(xs, axis=ax)
    dar, dai, dld, dbr, dbi = _s5_prep_bwd(*raw, cat(dabr, 0), cat(dabi, 0), cat(dbbr, 1), cat(dbbi, 1))
    shp = (1, 2, SSM_GROUPS, SSM_STATE)
    b_shape = (1, 2, SSM_GROUPS, SSM_STATE, SSM_GROUP)
    return dict(
        ssm_a_re=dar.reshape(shp), ssm_a_im=dai.reshape(shp), ssm_log_dt=dld.reshape(1, 2, SSM_GROUPS),
        ssm_b_re=dbr.transpose(1, 2, 0).reshape(b_shape), ssm_b_im=dbi.transpose(1, 2, 0).reshape(b_shape),
        ssm_c_re=jnp.stack(dcr)[None], ssm_c_im=jnp.stack(dci)[None])


def _example_step(x, ctx, target, mods, w, p):
    t, c = x.shape[0], ctx.shape[0]
    n = t + c
    assert t % c == 0 and c % LANES == 0 and c % (8 * N_SEG) == 0 and t % GRID_W == 0
    tr = _largest_tile(c, 256, 8)
    xall = (x, ctx)
    g0, g1 = _vec(p["norm_g"][0]), _vec(p["norm_g"][1])
    (shift0, scale0, gate0), (shift1, scale1, gate1) = [tuple(_vec(v) for v in m) for m in mods]

    h0 = _norm_mod_fwd("l0_norm", xall, g0, scale0, shift0, n, tr, t)
    proj0 = _mm("l0_in", h0, w["ssm_w_in"], "nn")
    raw, dirs = _s5_system(p, t, c)
    (y_f, y_r), saved = _s5_forward(proj0, dirs, t, c)
    d_skip = _vec(p["ssm_d"][0])

    def post_a(u, yf, yr, dv):
        y = u * dv + yf + yr
        return y, _gelu(y)

    y0, yg = _rowwise("l0_gelu", post_a, n, tr, [(proj0, 0, D_MODEL), (y_f, 0, D_MODEL), (y_r, 0, D_MODEL)], [d_skip],
                      [(D_MODEL, BF16), (D_MODEL, BF16)], [])
    tg = _mm("l0_glu", yg, w["ssm_w_glu"], "nn", out_dtype=BF16)
    b_glu = _vec(p["ssm_b_glu"][0])

    def post_b(ygt, tt, zt, bv):
        return ygt * _sigmoid(tt + bv) * _silu(zt)

    gz0 = _rowwise("l0_gate", post_b, n, tr, [(yg, 0, D_MODEL), (tg, 0, D_MODEL), (proj0, 1, D_MODEL)], [b_glu],
                   [(D_MODEL, BF16)], [])[0]
    out0 = _mm("l0_out", gz0, w["ssm_w_out"], "nn")

    def res_norm(xt, ot, gv, g1v, sc, sh):
        x1t = xt + gv * ot
        xh, _ = _rms_hat(x1t)
        return x1t, (xh * g1v) * (1.0 + sc) + sh

    x1, h1 = _rowwise("l0_res_l1_norm", res_norm, n, tr, [_stream(xall), (out0, 0, D_MODEL)],
                      [gate0, g1, scale1, shift1], [(D_MODEL, F32), (D_MODEL, BF16)], [], n_lat=t)
    proj1 = _mm("l1_in", h1, w["attn_w_in"], "nn", out_dtype=BF16)
    cos, sin = _rope_tables(t, c)
    qn, kn = p["attn_q_norm"][0], p["attn_k_norm"][0]
    q_h, k_h, v_h = _qk_prep(proj1, qn, kn, cos, sin, n, tr)
    tq = _largest_tile(t, 512, LANES)
    o, lse, gz1 = _attn_fwd(q_h, k_h, v_h, proj1, t, tq, _largest_tile(n, 2816, LANES))
    out1 = _mm("l1_out", gz1, w["attn_w_out"], "nn")

    gf = _vec(p["final_norm_g"])

    def head(x1t, o1t, tgt, g1v, gfv):
        x2 = x1t + g1v * o1t
        xh, r = _rms_hat(x2)
        e = xh * gfv - tgt
        dyf = e * (1.0 / D_MODEL)
        dx2 = _rms_bwd(xh, r, dyf * gfv)
        return dx2, g1v * dx2, dyf * xh, dx2 * o1t, jnp.sum(e * e, axis=1, keepdims=True)

    gate1_lat = gate1[0:1]
    dx2, dout1, d_gf, d_gate1, sq = _rowwise(
        "head", head, t, tr, [(x1, 0, D_MODEL), (out1, 0, D_MODEL), (target, 0, D_MODEL)], [gate1_lat, gf],
        [(D_MODEL, F32), (D_MODEL, BF16)], [(1, D_MODEL), (1, D_MODEL), (1, 1)])

    d_w_attn_out = _mm("l1_out_dw", gz1, dout1, "tn", out_dtype=BF16)
    dgz1 = _mm("l1_out_dx", dout1, w["attn_w_out"], "nt", out_dtype=BF16)
    dq_s, dk, dv, dz1 = _attn_bwd(q_h, k_h, v_h, dgz1, proj1, o, lse, t, tq, _largest_tile(n, 1024, LANES))
    dproj1, d_qn, d_kn = _qk_prep_bwd(proj1, qn, kn, cos, sin, dq_s, dz1, dk, dv, n, t, tr)
    d_w_attn_in = _mm("l1_in_dw", h1, dproj1, "tn", out_dtype=BF16)
    dh1 = _mm("l1_in_dx", dproj1, w["attn_w_in"], "nt", out_dtype=BF16)
    dx1, dout0, d_g1, d_scale1, d_shift1, d_gate0 = _norm_mod_bwd("l1_norm_bwd", x1, g1, scale1, dh1, dx2, n, tr, t,
                                                                  prev=(out0, gate0))

    d_w_out = _mm("l0_out_dw", gz0, dout0, "tn", out_dtype=BF16)
    dgz0 = _mm("l0_out_dx", dout0, w["ssm_w_out"], "nt", out_dtype=BF16)

    def post_b_bwd(dgt, ygt, tt, zt, bv):
        s = _sigmoid(tt + bv)
        dy2 = dgt * _silu(zt)
        dt = dy2 * ygt * s * (1.0 - s)
        return dgt * (ygt * s) * _silu_grad(zt), dt, dy2 * s, dt

    dz0, dtg, dyg_a, d_b_glu = _rowwise(
        "l0_gate_bwd", post_b_bwd, n, tr, [(dgz0, 0, D_MODEL), (yg, 0, D_MODEL), (tg, 0, D_MODEL), (proj0, 1, D_MODEL)],
        [b_glu], [(D_MODEL, BF16), (D_MODEL, BF16), (D_MODEL, BF16)], [(1, D_MODEL)])
    d_w_glu = _mm("l0_glu_dw", yg, dtg, "tn", out_dtype=BF16)
    dyg_b = _mm("l0_glu_dx", dtg, w["ssm_w_glu"], "nt", out_dtype=BF16)

    def post_a_bwd(da, db, yt, ut, dv):
        dy = (da + db) * _gelu_grad(yt)
        return dy, dy * dv, dy * ut

    dy0, du_skip, d_d = _rowwise("l0_gelu_bwd", post_a_bwd, n, tr,
                                 [(dyg_a, 0, D_MODEL), (dyg_b, 0, D_MODEL), (y0, 0, D_MODEL), (proj0, 0, D_MODEL)], [d_skip],
                                 [(D_MODEL, F32), (D_MODEL, BF16)], [(1, D_MODEL)])
    s5_bwd = _s5_backward(proj0, dy0, dirs, saved, t, c)
    dproj0 = _rowwise("l0_in_grad", lambda a, b, cc, dz: jnp.concatenate([a + b + cc, dz], axis=1), n, tr,
                      [(du_skip, 0, D_MODEL), (s5_bwd[0][0], 0, D_MODEL), (s5_bwd[1][0], 0, D_MODEL), (dz0, 0, D_MODEL)], [],
                      [(2 * D_MODEL, BF16)], [])[0]
    d_w_in = _mm("l0_in_dw", h0, dproj0, "tn", out_dtype=BF16)
    dh0 = _mm("l0_in_dx", dproj0, w["ssm_w_in"], "nt", out_dtype=BF16)
    dx0, d_g0, d_scale0, d_shift0 = _norm_mod_bwd("l0_norm_bwd", xall, g0, scale0, dh0, dx1, n, tr, t, dx_lat_only=True)

    big = dict(ssm_w_in=d_w_in, ssm_w_glu=d_w_glu, ssm_w_out=d_w_out, attn_w_in=d_w_attn_in, attn_w_out=d_w_attn_out)
    small = dict(
        norm_g=jnp.concatenate([d_g0[0], d_g1[0]], axis=0), ssm_d=d_d[0], ssm_b_glu=d_b_glu[0],
        attn_q_norm=d_qn, attn_k_norm=d_kn, final_norm_g=d_gf[0, 0], **_s5_param_grads(raw, s5_bwd))
    zero_v = jnp.zeros((D_MODEL,), F32)
    d_mod_lat = jnp.stack([jnp.concatenate([d_shift0[0, 0], d_scale0[0, 0], d_gate0[0, 0]]),
                           jnp.concatenate([d_shift1[0, 0], d_scale1[0, 0], d_gate1[0, 0]])])
    d_mod_ctx = jnp.stack([jnp.concatenate([d_shift0[1, 0], d_scale0[1, 0], d_gate0[1, 0]]),
                           jnp.concatenate([d_shift1[1, 0], d_scale1[1, 0], zero_v])])
    return sq[0, 0, 0], dx0, big, small, d_mod_lat, d_mod_ctx


def _adamw(name, w, g, m, v):
    rows, cols = w.shape
    tr = _largest_tile(rows, 256, 8)
    c1 = 1.0 / (1.0 - ADAM_B1 ** ADAM_STEP)
    c2 = 1.0 / (1.0 - ADAM_B2 ** ADAM_STEP)

    def fn(wt, gt, mt, vt):
        mn = ADAM_B1 * mt + (1.0 - ADAM_B1) * gt
        vn = ADAM_B2 * vt + (1.0 - ADAM_B2) * (gt * gt)
        delta = -ADAM_LR * ((mn * c1) / (jnp.sqrt(vn * c2) + ADAM_EPS) + ADAM_WD * wt)
        return delta, mn, vn

    return _rowwise(name, fn, rows, tr, [(a, 0, cols) for a in (w, g, m, v)], [], [(cols, F32)] * 3, [])


BIG = ("ssm_w_in", "ssm_w_glu", "ssm_w_out", "attn_w_in", "attn_w_out")
COL_SHARDED = ("ssm_w_in", "attn_w_in")
WEIGHTS = ("c_ctx", "w_mod", "b_mod", "norm_g", "ssm_w_in", "ssm_a_re", "ssm_a_im", "ssm_log_dt", "ssm_b_re", "ssm_b_im",
           "ssm_c_re", "ssm_c_im", "ssm_d", "ssm_w_glu", "ssm_b_glu", "ssm_w_out", "attn_w_in", "attn_q_norm",
           "attn_k_norm", "attn_w_out", "final_norm_g")
SMALL = tuple(k for k in WEIGHTS if k not in BIG and k != "w_mod")
PACK_W = 1024
COND_ROWS = 2 * N_DEV


def _attn_in_perm(x, inverse):
    a, kv = ATTN_W, 2 * KV_W
    if inverse:
        return jnp.concatenate([x[..., :a], x[..., 2 * a:], x[..., a:2 * a]], axis=-1)
    return jnp.concatenate([x[..., :a], x[..., a + kv:], x[..., a:a + kv]], axis=-1)


def _pack(arrays, dtype, row_unit):
    flat = jnp.concatenate([a.reshape(-1).astype(dtype) for a in arrays])
    rows = -(-flat.shape[0] // PACK_W)
    rows = -(-rows // row_unit) * row_unit
    flat = jnp.concatenate([flat, jnp.zeros((rows * PACK_W - flat.shape[0],), dtype)])
    return flat.reshape(rows, PACK_W)


def _unpack(buf, shapes):
    lead = buf.shape[:-2]
    flat = buf.reshape(lead + (-1,))
    out, off = [], 0
    for shp in shapes:
        size = math.prod(shp)
        out.append(flat[..., off:off + size].reshape(lead + tuple(shp)))
        off += size
    return out


def kernel(x, c, ctx, c_ctx, w_mod, b_mod, norm_g, ssm_w_in, ssm_a_re, ssm_a_im, ssm_log_dt, ssm_b_re, ssm_b_im, ssm_c_re, ssm_c_im, ssm_d, ssm_w_glu, ssm_b_glu, ssm_w_out, attn_w_in, attn_q_norm, attn_k_norm, attn_w_out, final_norm_g, loss_target, m_c_ctx, m_w_mod, m_b_mod, m_norm_g, m_ssm_w_in, m_ssm_a_re, m_ssm_a_im, m_ssm_log_dt, m_ssm_b_re, m_ssm_b_im, m_ssm_c_re, m_ssm_c_im, m_ssm_d, m_ssm_w_glu, m_ssm_b_glu, m_ssm_w_out, m_attn_w_in, m_attn_q_norm, m_attn_k_norm, m_attn_w_out, m_final_norm_g, v_c_ctx, v_w_mod, v_b_mod, v_norm_g, v_ssm_w_in, v_ssm_a_re, v_ssm_a_im, v_ssm_log_dt, v_ssm_b_re, v_ssm_b_im, v_ssm_c_re, v_ssm_c_im, v_ssm_d, v_ssm_w_glu, v_ssm_b_glu, v_ssm_w_out, v_attn_w_in, v_attn_q_norm, v_attn_k_norm, v_attn_w_out, v_final_norm_g):
    args = dict(locals())
    wts = {k: args[k] for k in WEIGHTS}
    mom_m = {k: args["m_" + k] for k in WEIGHTS}
    mom_v = {k: args["v_" + k] for k in WEIGHTS}
    mx, my, mc = lax.axis_index("x"), lax.axis_index("y"), lax.axis_index("c")
    chip = 2 * mx + my
    me = 2 * chip + mc

    halves = []
    for k in BIG:
        sh = wts[k][0]
        hr = sh.shape[0] // 2
        halves.append(lax.dynamic_slice_in_dim(sh, mc * hr, hr, axis=0))
    gathered = _gather_two_level("gather_weights", _pack(halves, BF16, 16))
    parts = _unpack(gathered, [h.shape for h in halves])
    w_full = {}
    for k, pc in zip(BIG, parts):
        hr, cols = pc.shape[1:]
        pc = pc.reshape(N_CHIP, 2, hr, cols)
        if k in COL_SHARDED:
            w_full[k] = pc.transpose(1, 2, 0, 3).reshape(2 * hr, N_CHIP * cols)
        else:
            w_full[k] = pc.reshape(N_CHIP * 2 * hr, cols)
    w_full["attn_w_in"] = _attn_in_perm(w_full["attn_w_in"], False)

    c_blk = jnp.concatenate([c, jnp.zeros((N_DEV - 1, D_MODEL), F32)], axis=0)
    c_all = _exchange("gather_c", c_blk, True)[:, 0]
    cond = jnp.concatenate([c_all, c_ctx[None], jnp.zeros((COND_ROWS - N_DEV - 1, D_MODEL), F32)], axis=0)
    s_cond, ds_cond = _rowwise("cond_silu", lambda t: (_silu(t), _silu_grad(t)), COND_ROWS, COND_ROWS, [(cond, 0, D_MODEL)], [],
                               [(D_MODEL, F32), (D_MODEL, F32)], [])
    w_mod_b = w_mod.astype(BF16)
    mcols = w_mod.shape[2]
    mod_part = jnp.stack([_mm(f"mod{i}", s_cond, w_mod_b[i], "nn") for i in range(2)])
    mod_g = _exchange("gather_mod", mod_part.reshape(2 * COND_ROWS, mcols), True)
    mod_all = mod_g.reshape(N_CHIP, 2, 2, COND_ROWS, mcols)[:, 0]
    mod_all = mod_all.transpose(1, 2, 0, 3).reshape(2, COND_ROWS, N_CHIP * mcols) + b_mod[:, None, :]
    mods = []
    for i in range(2):
        lat = lax.dynamic_slice_in_dim(mod_all[i], me, 1, axis=0)[0]
        both = jnp.stack([lat, mod_all[i, N_DEV]])
        mods.append((both[:, :D_MODEL], both[:, D_MODEL:2 * D_MODEL], both[:, 2 * D_MODEL:]))

    small_p = {k: wts[k] for k in SMALL if k != "c_ctx" and k != "b_mod"}
    sq, grad_x, big_g, small_g, d_mod_lat, d_mod_ctx = _example_step(x[0], ctx[0], loss_target[0], mods, w_full, small_p)
    loss = lax.psum(0.5 / D_MODEL * sq, ("x", "y", "c"))
    big_g["attn_w_in"] = _attn_in_perm(big_g["attn_w_in"], True)

    small_names = [k for k in SMALL if k not in ("c_ctx", "b_mod")]
    small_list = [small_g[k] for k in small_names] + [d_mod_lat, d_mod_ctx]
    small_shapes = [wts[k].shape for k in small_names] + [d_mod_lat.shape, d_mod_ctx.shape]
    packed = _pack(small_list, F32, 8 * N_DEV)
    slice_rows = packed.shape[0] // N_DEV
    slices = _exchange("scatter_small", packed.reshape(N_DEV, slice_rows, PACK_W), False)
    my_sum = _sum_slots("sum_small", slices)
    payload = jnp.concatenate([my_sum, _pack([d_mod_lat], F32, 8)], axis=0)
    sg = _exchange("gather_small", payload, True)
    summed = _unpack(sg[:, :slice_rows].reshape(packed.shape), small_shapes)
    grads = dict(zip(small_names, summed[:-2]))
    d_mod_lat_sum, d_mod_ctx_sum = summed[-2], summed[-1]
    grads["b_mod"] = d_mod_lat_sum + d_mod_ctx_sum
    d_mod_lat_all = _unpack(sg[:, slice_rows:], [d_mod_lat.shape])[0]

    g_w_mod, ds_cc = [], []
    for i in range(2):
        rows9 = jnp.concatenate([d_mod_lat_all[:, i], d_mod_ctx_sum[i][None],
                                 jnp.zeros((COND_ROWS - N_DEV - 1, 3 * D_MODEL), F32)], axis=0)
        mine = lax.dynamic_slice_in_dim(rows9, chip * mcols, mcols, axis=1)
        g_w_mod.append(_mm(f"mod{i}_dw", s_cond, mine, "tn"))
        ds_cc.append(_mm(f"mod{i}_dx", mine, w_mod_b[i], "nt")[N_DEV])
    grads["w_mod"] = jnp.stack(g_w_mod)
    part = (ds_cc[0] + ds_cc[1]) * jnp.where(mc == 0, 1.0, 0.0)
    part_blk = jnp.concatenate([part[None], jnp.zeros((N_DEV - 1, D_MODEL), F32)], axis=0)
    ds_all = _sum_slots("sum_c_ctx", _exchange("gather_c_ctx", part_blk, True))
    grads["c_ctx"] = ds_all[0] * ds_cond[N_DEV]

    blocks = []
    for k in BIG:
        g = big_g[k]
        rows, cols = g.shape
        if k in COL_SHARDED:
            blocks.append(g.reshape(2, rows // 2, N_CHIP, cols // N_CHIP).transpose(2, 0, 1, 3).reshape(N_DEV, -1))
        else:
            blocks.append(g.reshape(N_DEV, -1))
    sendbuf = jnp.concatenate(blocks, axis=1).astype(BF16)
    sendbuf = sendbuf.reshape(N_DEV, -1, PACK_W)
    recv = _exchange("scatter_big", sendbuf, False)
    mine = _sum_slots("sum_big", recv)
    both = _exchange("swap_halves", mine, True, sibling_only=True)
    half_shapes = [(wts[k].shape[1] // 2, wts[k].shape[2]) for k in BIG]
    for k, pc in zip(BIG, _unpack(both, half_shapes)):
        grads[k] = pc.reshape(wts[k].shape)

    delta, new_m, new_v = {}, {}, {}
    for k in BIG + ("w_mod",):
        shp = wts[k].shape
        two_d = (-1, shp[-1])
        res = _adamw("adamw_" + k, *[a.reshape(two_d) for a in (wts[k], grads[k], mom_m[k], mom_v[k])])
        delta[k], new_m[k], new_v[k] = [r.reshape(shp) for r in res]
    shapes = [wts[k].shape for k in SMALL]
    packed = [_pack([d[k] for k in SMALL], F32, 8) for d in (wts, grads, mom_m, mom_v)]
    res = _adamw("adamw_small", *packed)
    for dst, buf in zip((delta, new_m, new_v), res):
        for k, a in zip(SMALL, _unpack(buf, shapes)):
            dst[k] = a
    grads = {k: grads[k].reshape(wts[k].shape) for k in WEIGHTS}
    return (loss, grad_x[None], *[grads[k] for k in WEIGHTS], *[delta[k] for k in WEIGHTS],
            *[new_m[k] for k in WEIGHTS], *[new_v[k] for k in WEIGHTS])
```

```python
import functools
import math

import jax
import jax.numpy as jnp
from jax import lax
from jax.experimental import pallas as pl
from jax.experimental.pallas import tpu as pltpu

F32 = jnp.float32
BF16 = jnp.bfloat16

D_MODEL = 1024
NORM_EPS = 1e-6
SSM_GROUPS = 64
SSM_GROUP = 16
SSM_STATE = 64
LANES = 128
SLAB_W = LANES
N_SLAB = D_MODEL // SLAB_W
SLAB_GROUPS = SLAB_W // SSM_GROUP
HALF_W = SLAB_GROUPS * SSM_STATE
STATE_W = 2 * HALF_W
N_SEG = 8
HEAD_DIM = 64
N_Q_HEADS = 16
N_KV_HEADS = 4
KV_REP = N_Q_HEADS // N_KV_HEADS
ATTN_W = N_Q_HEADS * HEAD_DIM
KV_W = N_KV_HEADS * HEAD_DIM
GRID_W = 64
ROPE_THETA = 10000.0
N_DEV = 8
N_CHIP = 4
VMEM_LIMIT_BYTES = 56 * 1024 * 1024
STAGE_BYTES = 4 * 1024 * 1024

ADAM_LR = 0.001
ADAM_B1 = 0.9
ADAM_B2 = 0.999
ADAM_EPS = 1e-08
ADAM_WD = 0.01
ADAM_STEP = 10


def _params(*sem):
    return pltpu.CompilerParams(dimension_semantics=sem, vmem_limit_bytes=VMEM_LIMIT_BYTES)


def _largest_tile(n, cap, unit):
    if n <= cap:
        return n
    t = (cap // unit) * unit
    while t >= unit:
        if n % t == 0:
            return t
        t -= unit
    raise ValueError(f"no tile for {n} (cap {cap}, unit {unit})")


def _rowwise(name, fn, n_rows, tr, row_ins, vec_ins, row_outs, red_outs, n_lat=None, want_flag=False):
    nt = n_rows // tr
    assert nt * tr == n_rows
    nlt = nt if n_lat is None else n_lat // tr

    def sel(i):
        return jnp.where(i >= nlt, 1, 0)

    arrays, in_specs, pairs = [], [], []
    for spec in row_ins:
        arr, cb, w = spec[:3]
        kind = spec[3] if len(spec) > 3 else None
        m = spec[4] if len(spec) > 4 else None
        if kind == "pair":
            arrays += [arr, m]
            in_specs += [pl.BlockSpec((tr, w), functools.partial(lambda i, cb: (jnp.minimum(i, nlt - 1), cb), cb=cb)),
                         pl.BlockSpec((tr, w), functools.partial(lambda i, cb: (jnp.maximum(i - nlt, 0), cb), cb=cb))]
            pairs.append(len(arrays) - 2)
            continue
        if kind == "mod":
            imap = functools.partial(lambda i, cb, m: (i % m, cb), cb=cb, m=m)
        elif kind == "clamp":
            imap = functools.partial(lambda i, cb, m: (jnp.minimum(i, m - 1), cb), cb=cb, m=m)
        else:
            imap = functools.partial(lambda i, cb: (i, cb), cb=cb)
        arrays.append(arr)
        in_specs.append(pl.BlockSpec((tr, w), imap))
    for v in vec_ins:
        s, a, w = v.shape
        imap = (lambda i: (sel(i), 0, 0)) if s == 2 else (lambda i: (0, 0, 0))
        arrays.append(v)
        in_specs.append(pl.BlockSpec((1, a, w), imap))
    out_shapes, out_specs = [], []
    lat_only = [len(spec) > 2 for spec in row_outs]
    for (w, dt), lat in zip([spec[:2] for spec in row_outs], lat_only):
        out_shapes.append(jax.ShapeDtypeStruct((n_lat if lat else n_rows, w), dt))
        out_specs.append(pl.BlockSpec((tr, w), (lambda i: (jnp.minimum(i, nlt - 1), 0)) if lat else (lambda i: (i, 0))))
    for s, w in red_outs:
        out_shapes.append(jax.ShapeDtypeStruct((s, 1, w), F32))
        imap = (lambda i: (sel(i), 0, 0)) if s == 2 else (lambda i: (0, 0, 0))
        out_specs.append(pl.BlockSpec((1, 1, w), imap))
    n_ri, n_vi, n_ro, n_rd = len(row_ins) + len(pairs), len(vec_ins), len(row_outs), len(red_outs)

    def body(*refs):
        i = pl.program_id(0)
        rows, k = [], 0
        while k < n_ri:
            if k in pairs:
                rows.append(jnp.where(i < nlt, refs[k][...], refs[k + 1][...]).astype(F32))
                k += 2
            else:
                rows.append(refs[k][...].astype(F32))
                k += 1
        vecs = [r[0] for r in refs[n_ri:n_ri + n_vi]]
        outs = refs[n_ri + n_vi:]
        lead = [jnp.where(i < nlt, 1.0, 0.0).astype(F32)] if want_flag else []
        res = fn(*lead, *rows, *vecs)
        if not isinstance(res, (tuple, list)):
            res = (res,)
        assert len(res) == n_ro + n_rd
        for k in range(n_ro):
            if lat_only[k]:
                @pl.when(i < nlt)
                def _(k=k):
                    outs[k][...] = res[k].astype(outs[k].dtype)
            else:
                outs[k][...] = res[k].astype(outs[k].dtype)
        for k in range(n_rd):
            part = jnp.sum(res[n_ro + k].astype(F32), axis=0, keepdims=True)
            first = i == 0
            if red_outs[k][0] == 2:
                first = jnp.logical_or(first, i == nlt)
            o = outs[n_ro + k]

            @pl.when(first)
            def _():
                o[0] = part

            @pl.when(jnp.logical_not(first))
            def _():
                o[0] = o[0] + part

    res = pl.pallas_call(
        body, grid=(nt,), in_specs=in_specs, out_specs=out_specs, out_shape=out_shapes,
        compiler_params=_params("arbitrary"), name=name)(*arrays)
    return res


def _vec(v):
    v = v.astype(F32)
    if v.ndim == 1:
        v = v[None]
    return v[:, None, :]


def _mm(name, a, b, mode, out_dtype=F32):
    if mode in ("nn", "nt"):
        m, k = a.shape
        n = b.shape[1] if mode == "nn" else b.shape[0]
        tm = _largest_tile(m, 1024, 8)
        tn = _largest_tile(n, 1024, 128)
        contract = (((1,), (0,)), ((), ())) if mode == "nn" else (((1,), (1,)), ((), ()))

        def body(a_ref, b_ref, o_ref):
            o_ref[...] = lax.dot_general(a_ref[...].astype(BF16), b_ref[...].astype(BF16), contract,
                                         preferred_element_type=F32).astype(o_ref.dtype)

        b_spec = pl.BlockSpec((k, tn), lambda i, j: (0, j)) if mode == "nn" else pl.BlockSpec((tn, k), lambda i, j: (j, 0))
        return pl.pallas_call(
            body, grid=(m // tm, n // tn),
            in_specs=[pl.BlockSpec((tm, k), lambda i, j: (i, 0)), b_spec],
            out_specs=pl.BlockSpec((tm, tn), lambda i, j: (i, j)),
            out_shape=jax.ShapeDtypeStruct((m, n), out_dtype),
            compiler_params=_params("parallel", "arbitrary"), name=name)(a, b)
    assert mode == "tn"
    r, k1 = a.shape
    k2 = b.shape[1]
    tr = _largest_tile(r, 1024, 8)
    t2 = _largest_tile(k2, 1024, 128)
    nr = r // tr

    def body(a_ref, b_ref, o_ref, acc_ref):
        part = lax.dot_general(a_ref[...].astype(BF16), b_ref[...].astype(BF16), (((0,), (0,)), ((), ())),
                               preferred_element_type=F32)
        i = pl.program_id(1)

        @pl.when(i == 0)
        def _():
            acc_ref[...] = part

        @pl.when(i > 0)
        def _():
            acc_ref[...] += part

        @pl.when(i == nr - 1)
        def _():
            o_ref[...] = acc_ref[...].astype(o_ref.dtype)

    return pl.pallas_call(
        body, grid=(k2 // t2, nr),
        in_specs=[pl.BlockSpec((tr, k1), lambda j, i: (i, 0)), pl.BlockSpec((tr, t2), lambda j, i: (i, j))],
        out_specs=pl.BlockSpec((k1, t2), lambda j, i: (0, j)),
        out_shape=jax.ShapeDtypeStruct((k1, k2), out_dtype),
        scratch_shapes=[pltpu.VMEM((k1, t2), F32)],
        compiler_params=_params("parallel", "arbitrary"), name=name)(a, b)


def _exchange(name, x, bcast, sibling_only=False):
    rels = [1] if sibling_only else list(range(1, N_DEV))
    n_slot = 2 if sibling_only else N_DEV
    blk = x.shape if bcast else x.shape[1:]

    def body(x_ref, o_ref, send_sems, recv_sems, local_sem):
        mx, my, mc = lax.axis_index("x"), lax.axis_index("y"), lax.axis_index("c")
        me = mc if sibling_only else 4 * mx + 2 * my + mc
        me_dev = 4 * mx + 2 * my + mc
        mine = pltpu.make_async_copy(x_ref if bcast else x_ref.at[me_dev], o_ref.at[me], local_sem)
        mine.start()
        copies = []
        for k, r in enumerate(rels):
            px = 1 - mx if (r >> 2) & 1 else mx
            py = 1 - my if (r >> 1) & 1 else my
            pc = 1 - mc if r & 1 else mc
            src = x_ref if bcast else x_ref.at[4 * px + 2 * py + pc]
            cp = pltpu.make_async_remote_copy(
                src_ref=src, dst_ref=o_ref.at[me], send_sem=send_sems.at[k], recv_sem=recv_sems.at[k],
                device_id=(px, py, pc), device_id_type=pl.DeviceIdType.MESH)
            cp.start()
            copies.append(cp)
        for cp in copies:
            cp.wait()
        mine.wait()

    return pl.pallas_call(
        body, out_shape=jax.ShapeDtypeStruct((n_slot,) + tuple(blk), x.dtype),
        in_specs=[pl.BlockSpec(memory_space=pltpu.VMEM if x.size * x.dtype.itemsize <= STAGE_BYTES else pl.ANY)],
        out_specs=pl.BlockSpec(memory_space=pl.ANY),
        scratch_shapes=[pltpu.SemaphoreType.DMA((len(rels),)), pltpu.SemaphoreType.DMA((len(rels),)),
                        pltpu.SemaphoreType.DMA],
        name=name)(x)


def _gather_two_level(name, x):
    def body(x_ref, o_ref, send_sems, recv_sems, local_sem):
        mx, my, mc = lax.axis_index("x"), lax.axis_index("y"), lax.axis_index("c")
        me, sibling = (mx, my, mc), (mx, my, 1 - mc)
        chips = [(1 - mx, my), (mx, 1 - my), (1 - mx, 1 - my)]

        def slot(px, py, pc):
            return o_ref.at[4 * px + 2 * py + pc]

        def copy(k, block, to, src=None):
            return pltpu.make_async_remote_copy(
                src_ref=slot(*block) if src is None else src, dst_ref=slot(*block), send_sem=send_sems.at[k],
                recv_sem=recv_sems.at[k], device_id=to, device_id_type=pl.DeviceIdType.MESH)

        mine = pltpu.make_async_copy(x_ref, slot(*me), local_sem)
        mine.start()
        first = [copy(0, me, sibling, src=x_ref)]
        first += [copy(1 + j, me, (*chip, mc), src=x_ref) for j, chip in enumerate(chips)]
        for cp in first:
            cp.start()
        passed = [copy(4 + j, (*chip, mc), sibling) for j, chip in enumerate(chips)]
        for j, chip in enumerate(chips):
            copy(1 + j, (*chip, mc), me).wait_recv()
            passed[j].start()
        copy(0, sibling, me).wait_recv()
        for j, chip in enumerate(chips):
            copy(4 + j, (*chip, 1 - mc), me).wait_recv()
        for cp in first + passed:
            cp.wait_send()
        mine.wait()

    return pl.pallas_call(
        body, out_shape=jax.ShapeDtypeStruct((N_DEV,) + tuple(x.shape), x.dtype),
        in_specs=[pl.BlockSpec(memory_space=pl.ANY)], out_specs=pl.BlockSpec(memory_space=pl.ANY),
        scratch_shapes=[pltpu.SemaphoreType.DMA((N_DEV - 1,)), pltpu.SemaphoreType.DMA((N_DEV - 1,)),
                        pltpu.SemaphoreType.DMA],
        name=name)(x)


def _sum_slots(name, x):
    s, r, w = x.shape
    tr = _largest_tile(r, 256, 8)

    def body(x_ref, o_ref):
        acc = x_ref[0].astype(F32)
        for j in range(1, s):
            acc = acc + x_ref[j].astype(F32)
        o_ref[...] = acc

    return pl.pallas_call(
        body, grid=(r // tr,), in_specs=[pl.BlockSpec((s, tr, w), lambda i: (0, i, 0))],
        out_specs=pl.BlockSpec((tr, w), lambda i: (i, 0)), out_shape=jax.ShapeDtypeStruct((r, w), F32),
        compiler_params=_params("parallel"), name=name)(x)


def _sigmoid(x):
    return 1.0 / (1.0 + jnp.exp(-x))


def _silu(x):
    return x * _sigmoid(x)


def _silu_grad(x):
    s = _sigmoid(x)
    return s * (1.0 + x * (1.0 - s))


_INV_SQRT2 = 1.0 / math.sqrt(2.0)
_INV_SQRT2PI = 1.0 / math.sqrt(2.0 * math.pi)


def _gelu(x):
    return 0.5 * x * (1.0 + lax.erf(x * _INV_SQRT2))


def _gelu_grad(x):
    return 0.5 * (1.0 + lax.erf(x * _INV_SQRT2)) + x * jnp.exp(-0.5 * x * x) * _INV_SQRT2PI


def _rms_hat(x):
    r = lax.rsqrt(jnp.mean(x * x, axis=-1, keepdims=True) + NORM_EPS)
    return x * r, r


def _rms_bwd(xh, r, dxh):
    return r * (dxh - xh * jnp.mean(dxh * xh, axis=-1, keepdims=True))


def _stream(x):
    return (x[0], 0, D_MODEL, "pair", x[1]) if isinstance(x, tuple) else (x, 0, D_MODEL)


def _norm_mod_fwd(name, x, g, scale, shift, n_rows, tr, n_lat):
    def fn(xt, gv, sc, sh):
        xh, _ = _rms_hat(xt)
        return (xh * gv) * (1.0 + sc) + sh

    return _rowwise(name, fn, n_rows, tr, [_stream(x)], [g, scale, shift], [(D_MODEL, BF16)], [], n_lat=n_lat)[0]


def _norm_mod_bwd(name, x, g, scale, dh, dres, n_rows, tr, n_lat, prev=None, dx_lat_only=False):
    nlt = n_lat // tr

    def fn(flag, xt, dht, drt, *rest):
        gv, sc = rest[-2:] if prev is None else rest[1:3]
        xh, r = _rms_hat(xt)
        n = xh * gv
        dn = dht * (1.0 + sc)
        dx = _rms_bwd(xh, r, dn * gv) + flag * drt
        if prev is None:
            return dx, dn * xh, dht * n, dht
        return dx, rest[3] * dx, dn * xh, dht * n, dht, dx * rest[0]

    rows = [_stream(x), (dh, 0, D_MODEL), (dres, 0, D_MODEL, "clamp", nlt)]
    row_outs = [(D_MODEL, F32, "lat") if dx_lat_only else (D_MODEL, F32)]
    vecs, reds = [g, scale], [(1, D_MODEL), (2, D_MODEL), (2, D_MODEL)]
    if prev is not None:
        rows.append((prev[0], 0, D_MODEL))
        vecs.append(prev[1])
        row_outs.append((D_MODEL, BF16))
        reds.append((2, D_MODEL))
    return _rowwise(name, fn, n_rows, tr, rows, vecs, row_outs, reds, n_lat=n_lat, want_flag=True)


def _s5_prep(a_re, a_im, log_dt, b_re, b_im, seg_lat, seg_ctx):
    def body(ar_ref, ai_ref, ld_ref, br_ref, bi_ref, abr_ref, abi_ref, bbr_ref, bbi_ref, alr_ref, ali_ref, acr_ref,
             aci_ref):
        lr, li = ar_ref[...], ai_ref[...]
        dt = jnp.exp(ld_ref[...])
        ldr, ldi = lr * dt, li * dt
        e = jnp.exp(ldr)
        abr, abi = e * jnp.cos(ldi), e * jnp.sin(ldi)
        abr_ref[...] = abr
        abi_ref[...] = abi
        den = lr * lr + li * li
        nr, ni = abr - 1.0, abi
        qr = (nr * lr + ni * li) / den
        qi = (ni * lr - nr * li) / den
        br, bi = br_ref[...], bi_ref[...]
        bbr_ref[...] = qr[None] * br - qi[None] * bi
        bbi_ref[...] = qr[None] * bi + qi[None] * br
        for seg, r_ref, i_ref in ((seg_lat, alr_ref, ali_ref), (seg_ctx, acr_ref, aci_ref)):
            es = jnp.exp(ldr * float(seg))
            r_ref[...] = es * jnp.cos(ldi * float(seg))
            i_ref[...] = es * jnp.sin(ldi * float(seg))

    sm = jax.ShapeDtypeStruct(a_re.shape, F32)
    big = jax.ShapeDtypeStruct(b_re.shape, F32)
    return pl.pallas_call(body, out_shape=[sm, sm, big, big, sm, sm, sm, sm], name="s5_prep")(
        a_re, a_im, log_dt, b_re, b_im)


def _s5_prep_bwd(a_re, a_im, log_dt, b_re, b_im, dabr, dabi, dbbr, dbbi):
    def body(ar_ref, ai_ref, ld_ref, br_ref, bi_ref, dabr_ref, dabi_ref, dbbr_ref, dbbi_ref,
             dar_ref, dai_ref, dld_ref, dbr_ref, dbi_ref):
        lr, li = ar_ref[...], ai_ref[...]
        dt = jnp.exp(ld_ref[...])
        ldr, ldi = lr * dt, li * dt
        e = jnp.exp(ldr)
        abr, abi = e * jnp.cos(ldi), e * jnp.sin(ldi)
        den = lr * lr + li * li
        nr, ni = abr - 1.0, abi
        qr = (nr * lr + ni * li) / den
        qi = (ni * lr - nr * li) / den
        br, bi = br_ref[...], bi_ref[...]
        gbr, gbi = dbbr_ref[...], dbbi_ref[...]
        dbr_ref[...] = gbr * qr[None] + gbi * qi[None]
        dbi_ref[...] = gbi * qr[None] - gbr * qi[None]
        dqr = jnp.sum(gbr * br + gbi * bi, axis=0)
        dqi = jnp.sum(gbi * br - gbr * bi, axis=0)
        dnr = (dqr * lr - dqi * li) / den
        dni = (dqr * li + dqi * lr) / den
        dlr_q = (dqr * (nr - 2.0 * lr * qr) + dqi * (ni - 2.0 * lr * qi)) / den
        dli_q = (dqr * (ni - 2.0 * li * qr) + dqi * (-nr - 2.0 * li * qi)) / den
        gar = dabr_ref[...] + dnr
        gai = dabi_ref[...] + dni
        dldr = gar * abr + gai * abi
        dldi = gai * abr - gar * abi
        dar_ref[...] = dldr * dt + dlr_q
        dai_ref[...] = dldi * dt + dli_q
        ddt = jnp.sum(dldr * lr + dldi * li, axis=1, keepdims=True)
        dld_ref[...] = ddt * dt

    sm = jax.ShapeDtypeStruct(a_re.shape, F32)
    big = jax.ShapeDtypeStruct(b_re.shape, F32)
    return pl.pallas_call(body, out_shape=[sm, sm, jax.ShapeDtypeStruct(log_dt.shape, F32), big, big],
                          name="s5_prep_bwd")(a_re, a_im, log_dt, b_re, b_im, dabr, dabi, dbbr, dbbi)


def _slab_cols(v):
    return v.reshape(N_SLAB, 1, HALF_W)


def _slab_pair(vr, vi):
    return jnp.concatenate([_slab_cols(vr), _slab_cols(vi)], axis=-1)


def _slab_in_matrix(bbr, bbi):
    eye = jnp.eye(SLAB_GROUPS, dtype=F32)

    def one(b):
        b = b.reshape(N_SLAB, SLAB_GROUPS, SSM_STATE, SSM_GROUP)
        m = jnp.einsum("sgph,gk->sghkp", b, eye)
        return m.reshape(N_SLAB, SLAB_W, HALF_W)

    return jnp.concatenate([one(bbr), one(bbi)], axis=-1)


def _slab_out_matrix(cr, ci):
    eye = jnp.eye(SLAB_GROUPS, dtype=F32)

    def one(c):
        c = c.reshape(N_SLAB, SLAB_GROUPS, SSM_GROUP, SSM_STATE)
        m = jnp.einsum("sghp,gk->skpgh", c, eye)
        return m.reshape(N_SLAB, HALF_W, SLAB_W)

    return jnp.concatenate([one(cr), one(-ci)], axis=1)


def _slab_diag(m):
    m = m.reshape(N_SLAB, SLAB_GROUPS, SSM_GROUP, 2, SLAB_GROUPS, SSM_STATE)
    d = jnp.stack([m[:, g, :, :, g, :] for g in range(SLAB_GROUPS)], axis=1)
    return d.transpose(3, 0, 1, 2, 4).reshape(2, SSM_GROUPS, SSM_GROUP, SSM_STATE)


def _cmul(ar, ai, xr, xi, conj):
    if conj:
        return ar * xr + ai * xi, ar * xi - ai * xr
    return ar * xr - ai * xi, ar * xi + ai * xr


def _s5_pow_table(name, abar, seg, falling, conj):
    assert seg >= 8 and seg & (seg - 1) == 0

    def body(a_ref, o_ref, t_ref):
        ar, ai = a_ref[0, :, :HALF_W], a_ref[0, :, HALF_W:]
        if conj:
            ai = -ai
        rr, ri = [jnp.ones_like(ar)], [jnp.zeros_like(ai)]
        for _ in range(7):
            pr, pi = _cmul(ar, ai, rr[-1], ri[-1], False)
            rr.append(pr)
            ri.append(pi)
        sr, si = _cmul(ar, ai, rr[-1], ri[-1], False)
        if falling:
            rr, ri = rr[::-1], ri[::-1]
        first = slice(seg - 8, seg) if falling else slice(0, 8)
        t_ref[first, :HALF_W] = jnp.concatenate(rr, axis=0)
        t_ref[first, HALF_W:] = jnp.concatenate(ri, axis=0)
        size = 8
        while size < seg:
            src = slice(seg - size, seg) if falling else slice(0, size)
            dst = slice(seg - 2 * size, seg - size) if falling else slice(size, 2 * size)
            pr, pi = _cmul(sr, si, t_ref[src, :HALF_W], t_ref[src, HALF_W:], False)
            t_ref[dst, :HALF_W] = pr
            t_ref[dst, HALF_W:] = pi
            sr, si = _cmul(sr, si, sr, si, False)
            size *= 2
        o_ref[0] = t_ref[...].astype(BF16)

    return pl.pallas_call(
        body, grid=(N_SLAB,), in_specs=[pl.BlockSpec((1, 1, STATE_W), lambda s: (s, 0, 0))],
        out_specs=pl.BlockSpec((1, seg, STATE_W), lambda s: (s, 0, 0)),
        out_shape=jax.ShapeDtypeStruct((N_SLAB, seg, STATE_W), BF16),
        scratch_shapes=[pltpu.VMEM((seg, STATE_W), F32)], compiler_params=_params("parallel"), name=name)(abar)


def _s5_ends(name, x, n_rows, row0, table, m_mat):
    seg = n_rows // N_SEG
    rb = row0 // n_rows
    tn = (((0,), (0,)), ((), ()))

    def body(x_ref, t_ref, m_ref, z_ref):
        mr, mi = m_ref[0, :, :HALF_W], m_ref[0, :, HALF_W:]
        for j in range(N_SEG):
            t = lax.dot_general(x_ref[j * seg:(j + 1) * seg, :].astype(BF16), t_ref[0], tn,
                                preferred_element_type=F32)
            tr_, ti_ = t[:, :HALF_W], t[:, HALF_W:]
            z_ref[0, j:j + 1, :HALF_W] = jnp.sum(mr * tr_ - mi * ti_, axis=0, keepdims=True)
            z_ref[0, j:j + 1, HALF_W:] = jnp.sum(mr * ti_ + mi * tr_, axis=0, keepdims=True)

    return pl.pallas_call(
        body, grid=(N_SLAB,),
        in_specs=[pl.BlockSpec((n_rows, SLAB_W), lambda s: (rb, s)),
                  pl.BlockSpec((1, seg, STATE_W), lambda s: (s, 0, 0)),
                  pl.BlockSpec((1, SLAB_W, STATE_W), lambda s: (s, 0, 0))],
        out_specs=pl.BlockSpec((1, N_SEG, STATE_W), lambda s: (s, 0, 0)),
        out_shape=jax.ShapeDtypeStruct((N_SLAB, N_SEG, STATE_W), F32),
        compiler_params=_params("parallel"), name=name)(x, table, m_mat)


def _s5_carry(name, z, a_seg, init, descending, conj):
    order = list(range(N_SEG - 1, -1, -1)) if descending else list(range(N_SEG))

    def body(z_ref, a_ref, i_ref, e_ref, o_ref):
        ar, ai = a_ref[:, :HALF_W], a_ref[:, HALF_W:]
        cr, ci = i_ref[:, :HALF_W], i_ref[:, HALF_W:]
        for j in order:
            e_ref[:, j, :HALF_W] = cr
            e_ref[:, j, HALF_W:] = ci
            pr, pi = _cmul(ar, ai, cr, ci, conj)
            cr = pr + z_ref[:, j, :HALF_W]
            ci = pi + z_ref[:, j, HALF_W:]
        o_ref[:, :HALF_W] = cr
        o_ref[:, HALF_W:] = ci

    return pl.pallas_call(body, out_shape=[jax.ShapeDtypeStruct(z.shape, F32), jax.ShapeDtypeStruct(init.shape, F32)],
                          name=name)(z, a_seg, init)


def _s5_scan(name, u, n_rows, row0, b_mat, c_mat, abar, h_in, descending, y_alias=None, y_rows=None):
    seg = n_rows // N_SEG
    ta = min(256, seg)
    nk = seg // ta
    assert seg * N_SEG == n_rows and nk * ta == seg and row0 % n_rows == 0 and ta % 8 == 0
    rb = row0 // n_rows
    tile = ta * N_SEG

    def body(*refs):
        u_ref, b_ref, c_ref, a_ref, hin_ref = refs[:5]
        y_ref, hch_ref, st_ref, up_ref, h_ref = refs[-5:]
        k = pl.program_id(1)
        kk = nk - 1 - k if descending else k
        a0 = kk * ta

        @pl.when(k == 0)
        def _():
            st_ref[...] = hin_ref[0]

        hch_ref[0, 0] = st_ref[...]
        for al in range(ta):
            up_ref[al * N_SEG:(al + 1) * N_SEG, :] = u_ref[pl.ds(a0 + al, N_SEG, stride=seg), :]
        h_ref[...] = jnp.dot(up_ref[...].astype(BF16), b_ref[0], preferred_element_type=F32)
        ar = jnp.broadcast_to(a_ref[0, :, :HALF_W], (N_SEG, HALF_W))
        ai = jnp.broadcast_to(a_ref[0, :, HALF_W:], (N_SEG, HALF_W))

        def step(i, carry):
            hr, hi = carry
            al = ta - 1 - i if descending else i
            row = pl.multiple_of(al * N_SEG, N_SEG)
            pr, pi = _cmul(ar, ai, hr, hi, False)
            hr = pr + h_ref[pl.ds(row, N_SEG), :HALF_W]
            hi = pi + h_ref[pl.ds(row, N_SEG), HALF_W:]
            h_ref[pl.ds(row, N_SEG), :HALF_W] = hr
            h_ref[pl.ds(row, N_SEG), HALF_W:] = hi
            return hr, hi

        hr, hi = lax.fori_loop(0, ta, step, (st_ref[:, :HALF_W], st_ref[:, HALF_W:]), unroll=True)
        st_ref[:, :HALF_W] = hr
        st_ref[:, HALF_W:] = hi
        yt = jnp.dot(h_ref[...].astype(BF16), c_ref[0], preferred_element_type=F32)
        for al in range(ta):
            y_ref[pl.ds(a0 + al, N_SEG, stride=seg), :] = yt[al * N_SEG:(al + 1) * N_SEG, :]

    u_spec = pl.BlockSpec((n_rows, SLAB_W), lambda s, k: (rb, s))
    b_spec = pl.BlockSpec((1, SLAB_W, STATE_W), lambda s, k: (s, 0, 0))
    c_spec = pl.BlockSpec((1, STATE_W, SLAB_W), lambda s, k: (s, 0, 0))
    a_spec = pl.BlockSpec((1, 1, STATE_W), lambda s, k: (s, 0, 0))
    st_spec = pl.BlockSpec((1, N_SEG, STATE_W), lambda s, k: (s, 0, 0))
    scratch = [pltpu.VMEM((N_SEG, STATE_W), F32), pltpu.VMEM((tile, SLAB_W), F32), pltpu.VMEM((tile, STATE_W), F32)]
    kmap = (lambda s, k: (s, nk - 1 - k, 0, 0)) if descending else (lambda s, k: (s, k, 0, 0))
    out_specs = [u_spec, pl.BlockSpec((1, 1, N_SEG, STATE_W), kmap)]
    out_shape = [jax.ShapeDtypeStruct((y_rows, D_MODEL), F32), jax.ShapeDtypeStruct((N_SLAB, nk, N_SEG, STATE_W), F32)]
    in_specs = [u_spec, b_spec, c_spec, a_spec, st_spec]
    args = [u, b_mat, c_mat, abar, h_in]
    aliases = {}
    if y_alias is not None:
        in_specs.append(pl.BlockSpec(memory_space=pl.ANY))
        args.append(y_alias)
        aliases = {5: 0}
    return pl.pallas_call(
        body, grid=(N_SLAB, nk), in_specs=in_specs, out_specs=out_specs, out_shape=out_shape, scratch_shapes=scratch,
        input_output_aliases=aliases, compiler_params=_params("parallel", "arbitrary"), name=name)(*args)


def _s5_scan_bwd(name, u, dy, n_rows, row0, b_mat, bt_mat, ct_mat, abar, h_chunks, g_in, descending,
                 du_alias=None, du_rows=None):
    seg = n_rows // N_SEG
    ta = min(256, seg)
    nk = seg // ta
    rb = row0 // n_rows
    tile = ta * N_SEG
    g_desc = not descending

    def body(*refs):
        u_ref, dy_ref, b_ref, bt_ref, ct_ref, a_ref, hch_ref, gin_ref = refs[:8]
        du_ref, db_ref, dc_ref, da_ref, st_ref, up_ref, dyp_ref, h_ref, g_ref = refs[-9:]
        k = pl.program_id(1)
        kk = nk - 1 - k if g_desc else k
        a0 = kk * ta
        ar = jnp.broadcast_to(a_ref[0, :, :HALF_W], (N_SEG, HALF_W))
        ai = jnp.broadcast_to(a_ref[0, :, HALF_W:], (N_SEG, HALF_W))

        @pl.when(k == 0)
        def _():
            st_ref[...] = gin_ref[0]

        for al in range(ta):
            dyp_ref[al * N_SEG:(al + 1) * N_SEG, :] = dy_ref[pl.ds(a0 + al, N_SEG, stride=seg), :]
            up_ref[al * N_SEG:(al + 1) * N_SEG, :] = u_ref[pl.ds(a0 + al, N_SEG, stride=seg), :]
        g_ref[...] = jnp.dot(dyp_ref[...].astype(BF16), ct_ref[0], preferred_element_type=F32)
        h_ref[...] = jnp.dot(up_ref[...].astype(BF16), b_ref[0], preferred_element_type=F32)
        h0r, h0i = hch_ref[0, 0, :, :HALF_W], hch_ref[0, 0, :, HALF_W:]

        def hstep(i, carry):
            hr, hi = carry
            al = ta - 1 - i if descending else i
            row = pl.multiple_of(al * N_SEG, N_SEG)
            pr, pi = _cmul(ar, ai, hr, hi, False)
            hr = pr + h_ref[pl.ds(row, N_SEG), :HALF_W]
            hi = pi + h_ref[pl.ds(row, N_SEG), HALF_W:]
            h_ref[pl.ds(row, N_SEG), :HALF_W] = hr
            h_ref[pl.ds(row, N_SEG), HALF_W:] = hi
            return hr, hi

        lax.fori_loop(0, ta, hstep, (h0r, h0i), unroll=True)

        def gstep(i, carry):
            gr, gi = carry
            al = ta - 1 - i if g_desc else i
            row = pl.multiple_of(al * N_SEG, N_SEG)
            pr, pi = _cmul(ar, ai, gr, gi, True)
            gr = pr + g_ref[pl.ds(row, N_SEG), :HALF_W]
            gi = pi + g_ref[pl.ds(row, N_SEG), HALF_W:]
            g_ref[pl.ds(row, N_SEG), :HALF_W] = gr
            g_ref[pl.ds(row, N_SEG), HALF_W:] = gi
            return gr, gi

        gr, gi = lax.fori_loop(0, ta, gstep, (st_ref[:, :HALF_W], st_ref[:, HALF_W:]), unroll=True)
        st_ref[:, :HALF_W] = gr
        st_ref[:, HALF_W:] = gi

        gb = g_ref[...].astype(BF16)
        dut = jnp.dot(gb, bt_ref[0], preferred_element_type=F32)
        for al in range(ta):
            du_ref[pl.ds(a0 + al, N_SEG, stride=seg), :] = dut[al * N_SEG:(al + 1) * N_SEG, :]
        tn = (((0,), (0,)), ((), ()))
        dbp = lax.dot_general(up_ref[...].astype(BF16), gb, tn, preferred_element_type=F32)
        dcp = lax.dot_general(dyp_ref[...].astype(BF16), h_ref[...].astype(BF16), tn, preferred_element_type=F32)
        inner = (ta - 1) * N_SEG
        if descending:
            g_in_r, g_in_i = g_ref[0:inner, :HALF_W], g_ref[0:inner, HALF_W:]
            p_in_r, p_in_i = h_ref[N_SEG:tile, :HALF_W], h_ref[N_SEG:tile, HALF_W:]
            g_ed_r, g_ed_i = g_ref[inner:tile, :HALF_W], g_ref[inner:tile, HALF_W:]
        else:
            g_in_r, g_in_i = g_ref[N_SEG:tile, :HALF_W], g_ref[N_SEG:tile, HALF_W:]
            p_in_r, p_in_i = h_ref[0:inner, :HALF_W], h_ref[0:inner, HALF_W:]
            g_ed_r, g_ed_i = g_ref[0:N_SEG, :HALF_W], g_ref[0:N_SEG, HALF_W:]
        dar = g_ed_r * h0r + g_ed_i * h0i
        dai = g_ed_i * h0r - g_ed_r * h0i
        if ta > 1:
            dar = dar + jnp.sum((g_in_r * p_in_r + g_in_i * p_in_i).reshape(ta - 1, N_SEG, HALF_W), axis=0)
            dai = dai + jnp.sum((g_in_i * p_in_r - g_in_r * p_in_i).reshape(ta - 1, N_SEG, HALF_W), axis=0)

        @pl.when(k == 0)
        def _():
            db_ref[0] = dbp
            dc_ref[0] = dcp
            da_ref[0, :, :HALF_W] = dar
            da_ref[0, :, HALF_W:] = dai

        @pl.when(k > 0)
        def _():
            db_ref[0] += dbp
            dc_ref[0] += dcp
            da_ref[0, :, :HALF_W] += dar
            da_ref[0, :, HALF_W:] += dai

    u_spec = pl.BlockSpec((n_rows, SLAB_W), lambda s, k: (rb, s))
    m_spec = pl.BlockSpec((1, SLAB_W, STATE_W), lambda s, k: (s, 0, 0))
    mt_spec = pl.BlockSpec((1, STATE_W, SLAB_W), lambda s, k: (s, 0, 0))
    a_spec = pl.BlockSpec((1, 1, STATE_W), lambda s, k: (s, 0, 0))
    st_spec = pl.BlockSpec((1, N_SEG, STATE_W), lambda s, k: (s, 0, 0))
    st_shape = jax.ShapeDtypeStruct((N_SLAB, N_SEG, STATE_W), F32)
    kmap = (lambda s, k: (s, nk - 1 - k, 0, 0)) if g_desc else (lambda s, k: (s, k, 0, 0))
    in_specs = [u_spec, u_spec, m_spec, mt_spec, m_spec, a_spec, pl.BlockSpec((1, 1, N_SEG, STATE_W), kmap), st_spec]
    args = [u, dy, b_mat, bt_mat, ct_mat, abar, h_chunks, g_in]
    aliases = {}
    if du_alias is not None:
        in_specs.append(pl.BlockSpec(memory_space=pl.ANY))
        args.append(du_alias)
        aliases = {8: 0}
    acc_shape = jax.ShapeDtypeStruct((N_SLAB, SLAB_W, STATE_W), F32)
    out_specs = [u_spec, m_spec, m_spec, st_spec]
    out_shape = [jax.ShapeDtypeStruct((du_rows, D_MODEL), F32), acc_shape, acc_shape, st_shape]
    scratch = [pltpu.VMEM((N_SEG, STATE_W), F32), pltpu.VMEM((tile, SLAB_W), F32), pltpu.VMEM((tile, SLAB_W), F32),
               pltpu.VMEM((tile, STATE_W), F32), pltpu.VMEM((tile, STATE_W), F32)]
    return pl.pallas_call(
        body, grid=(N_SLAB, nk), in_specs=in_specs, out_specs=out_specs, out_shape=out_shape, scratch_shapes=scratch,
        input_output_aliases=aliases, compiler_params=_params("parallel", "arbitrary"), name=name)(*args)


ROPE_HALF = HEAD_DIM // 4
TABLE_W = 2 * HEAD_DIM
Q_SCALE = 1.0 / math.sqrt(HEAD_DIM)
HEADS_PER_BLOCK = 2 * KV_REP
Q_BLOCK_W = HEADS_PER_BLOCK * HEAD_DIM


def _rope_tables(n_lat, n_ctx):
    rows = n_lat // GRID_W
    freqs = ROPE_THETA ** (-jnp.arange(ROPE_HALF, dtype=F32) / ROPE_HALF)
    ang_r = jnp.arange(rows, dtype=F32)[:, None] * freqs[None]
    ang_c = jnp.arange(GRID_W, dtype=F32)[:, None] * freqs[None]
    by_row = lambda v: jnp.repeat(v, GRID_W, axis=0)
    by_col = lambda v: jnp.tile(v, (rows, 1))
    cos = jnp.concatenate([by_row(jnp.cos(ang_r)), by_row(jnp.cos(ang_r)), by_col(jnp.cos(ang_c)), by_col(jnp.cos(ang_c))] * 2,
                          axis=1)
    sin = jnp.concatenate([by_row(jnp.sin(ang_r)), by_row(jnp.sin(ang_r)), by_col(jnp.sin(ang_c)), by_col(jnp.sin(ang_c))] * 2,
                          axis=1)
    cos = jnp.concatenate([cos, jnp.ones((n_ctx, TABLE_W), F32)], axis=0)
    sin = jnp.concatenate([sin, jnp.zeros((n_ctx, TABLE_W), F32)], axis=0)
    return cos, sin


def _rot_half(v):
    w = v.shape[1]
    ahead = pltpu.roll(v, w - ROPE_HALF, axis=1)
    behind = pltpu.roll(v, ROPE_HALF, axis=1)
    lane = lax.broadcasted_iota(jnp.int32, v.shape, 1)
    return jnp.where((lane % (2 * ROPE_HALF)) < ROPE_HALF, -ahead, behind)


def _head_mean(v, sel, selt):
    m = jnp.dot(v, sel, precision=lax.Precision.HIGH, preferred_element_type=F32) * (1.0 / HEAD_DIM)
    return jnp.dot(m, selt, precision=lax.Precision.HIGH, preferred_element_type=F32)


def _head_selectors(n_heads):
    sel = jnp.repeat(jnp.eye(n_heads, dtype=F32), HEAD_DIM, axis=0)
    return sel[None], sel.T[None]


def _head_norm(x, sel, selt):
    r = lax.rsqrt(_head_mean(x * x, sel, selt) + NORM_EPS)
    return x * r, r


def _qk_prep(proj, qn, kn, cos, sin, n, tr):
    qw, kw = _vec(jnp.tile(qn, N_Q_HEADS)), _vec(jnp.tile(kn, N_KV_HEADS))
    sq, sqt = _head_selectors(N_Q_HEADS)
    sk, skt = _head_selectors(N_KV_HEADS)

    def fn(qr, kvr, ct, st, qwv, kwv, s16, s16t, s4, s4t):
        outs = []
        for x, wv, sel, selt, scale in ((qr, qwv, s16, s16t, Q_SCALE), (kvr[:, :KV_W], kwv, s4, s4t, 1.0)):
            reps = x.shape[1] // TABLE_W
            cw, sw = jnp.tile(ct, (1, reps)), jnp.tile(st, (1, reps))
            xh, _ = _head_norm(x, sel, selt)
            nrm = xh * wv
            outs.append((nrm * cw + _rot_half(nrm) * sw) * scale)
        return outs[0], outs[1], kvr[:, KV_W:]

    return _rowwise("l1_qk_prep", fn, n, tr,
                    [(proj, 0, ATTN_W), (proj, 2 * ATTN_W // (2 * KV_W), 2 * KV_W), (cos, 0, TABLE_W), (sin, 0, TABLE_W)],
                    [qw, kw, sq, sqt, sk, skt], [(ATTN_W, BF16), (KV_W, BF16), (KV_W, BF16)], [])


def _qk_prep_bwd(proj, qn, kn, cos, sin, dq, dz, dk, dv, n, n_lat, tr):
    qw, kw = _vec(jnp.tile(qn, N_Q_HEADS)), _vec(jnp.tile(kn, N_KV_HEADS))
    sq, sqt = _head_selectors(N_Q_HEADS)
    sk, skt = _head_selectors(N_KV_HEADS)
    nlt = n_lat // tr

    def fn(flag, qr, kvr, ct, st, dqt, dzt, dkt, dvt, qwv, kwv, s16, s16t, s4, s4t):
        dxs, dws = [], []
        for x, dy, wv, sel, selt in ((qr, dqt * (flag * Q_SCALE), qwv, s16, s16t), (kvr[:, :KV_W], dkt, kwv, s4, s4t)):
            reps = x.shape[1] // TABLE_W
            cw, sw = jnp.tile(ct, (1, reps)), jnp.tile(st, (1, reps))
            xh, r = _head_norm(x, sel, selt)
            dn = dy * cw - _rot_half(dy * sw)
            dxh = dn * wv
            dxs.append(r * (dxh - xh * _head_mean(dxh * xh, sel, selt)))
            dws.append(dn * xh)
        return jnp.concatenate([dxs[0], dzt * flag, dxs[1], dvt], axis=1), dws[0], dws[1]

    dproj, dqw, dkw = _rowwise(
        "l1_qk_prep_bwd", fn, n, tr,
        [(proj, 0, ATTN_W), (proj, 2 * ATTN_W // (2 * KV_W), 2 * KV_W), (cos, 0, TABLE_W), (sin, 0, TABLE_W),
         (dq, 0, ATTN_W, "clamp", nlt), (dz, 0, ATTN_W, "clamp", nlt), (dk, 0, KV_W), (dv, 0, KV_W)],
        [qw, kw, sq, sqt, sk, skt], [(2 * ATTN_W + 2 * KV_W, BF16)], [(1, ATTN_W), (1, KV_W)], n_lat=n_lat, want_flag=True)
    return dproj, dqw.reshape(N_Q_HEADS, HEAD_DIM).sum(0)[None], dkw.reshape(N_KV_HEADS, HEAD_DIM).sum(0)[None]


NT = (((1,), (1,)), ((), ()))


def _attn_fwd(q, k, v, proj, t, tq, tk):
    n = k.shape[0]
    nkc = n // tk

    ts = _largest_tile(tq, 256, LANES)
    items = [(sub, j) for sub in range(tq // ts) for j in range(HEADS_PER_BLOCK)]

    def body(q_ref, k_ref, v_ref, z_ref, o_ref, lse_ref, gz_ref, s_ref, m_ref, acc_ref):
        def lanes(j):
            g = j // KV_REP
            return slice(j * HEAD_DIM, (j + 1) * HEAD_DIM), slice(g * HEAD_DIM, (g + 1) * HEAD_DIM)

        for idx in range(len(items) + 1):
            nxt = items[idx] if idx < len(items) else None
            cur = items[idx - 1] if idx > 0 else None
            sn, sc = idx % 2, (idx - 1) % 2
            if nxt is not None:
                rows_n = slice(nxt[0] * ts, (nxt[0] + 1) * ts)
                ql_n, kl_n = lanes(nxt[1])
                qv = q_ref[rows_n, ql_n]
                m_ref[sn] = jnp.full((ts, LANES), -jnp.inf, F32)
            if cur is not None:
                rows_c = slice(cur[0] * ts, (cur[0] + 1) * ts)
                ql_c, kl_c = lanes(cur[1])
                m_row = jnp.max(m_ref[sc], axis=-1, keepdims=True)
                acc_ref[...] = jnp.zeros(acc_ref.shape, F32)

            def sweep(kc, c):
                off = pl.multiple_of(kc * tk, tk)
                if nxt is not None:
                    s = lax.dot_general(qv, k_ref[pl.ds(off, tk), kl_n], NT, preferred_element_type=F32)
                    s_ref[sn, :, pl.ds(off, tk)] = s
                    m = m_ref[sn]
                    for cb in range(tk // LANES):
                        m = jnp.maximum(m, s[:, cb * LANES:(cb + 1) * LANES])
                    m_ref[sn] = m
                if cur is not None:
                    p = jnp.exp(s_ref[sc, :, pl.ds(off, tk)] - m_row)
                    v_one = jnp.concatenate([v_ref[pl.ds(off, tk), kl_c], jnp.ones((tk, HEAD_DIM), BF16)], axis=1)
                    acc_ref[...] += jnp.dot(p.astype(BF16), v_one, preferred_element_type=F32)
                return c

            lax.fori_loop(0, nkc, sweep, 0, unroll=True)
            if cur is not None:
                l_row = acc_ref[:, HEAD_DIM:HEAD_DIM + 1]
                o_head = acc_ref[:, :HEAD_DIM] / l_row
                o_ref[rows_c, ql_c] = o_head
                gz_ref[rows_c, ql_c] = (o_head * _silu(z_ref[rows_c, ql_c].astype(F32))).astype(BF16)
                lse_ref[0, rows_c, cur[1]:cur[1] + 1] = m_row + jnp.log(l_row)

    nb = ATTN_W // Q_BLOCK_W
    kspec = pl.BlockSpec((n, LANES), lambda b, i: (0, b))
    return pl.pallas_call(
        body, grid=(nb, t // tq),
        in_specs=[pl.BlockSpec((tq, Q_BLOCK_W), lambda b, i: (i, b)), kspec, kspec,
                  pl.BlockSpec((tq, Q_BLOCK_W), lambda b, i: (i, nb + b))],
        out_specs=[pl.BlockSpec((tq, Q_BLOCK_W), lambda b, i: (i, b)),
                   pl.BlockSpec((1, tq, HEADS_PER_BLOCK), lambda b, i: (b, i, 0)),
                   pl.BlockSpec((tq, Q_BLOCK_W), lambda b, i: (i, b))],
        out_shape=[jax.ShapeDtypeStruct((t, ATTN_W), F32), jax.ShapeDtypeStruct((nb, t, HEADS_PER_BLOCK), F32),
                   jax.ShapeDtypeStruct((t, ATTN_W), BF16)],
        scratch_shapes=[pltpu.VMEM((2, ts, n), F32), pltpu.VMEM((2, ts, LANES), F32), pltpu.VMEM((ts, 2 * HEAD_DIM), F32)],
        compiler_params=_params("parallel", "parallel"), name="attn_fwd")(q, k, v, proj)


def _attn_bwd(q, k, v, dgz, proj, o, lse, t, tq, tk):
    n = k.shape[0]
    nkc = n // tk
    tn = (((0,), (0,)), ((), ()))

    def body(q_ref, k_ref, v_ref, dgz_ref, z_ref, o_ref, lse_ref, dq_ref, dk_ref, dv_ref, dz_ref, acc_ref):
        @pl.when(pl.program_id(1) == 0)
        def _():
            dk_ref[...] = jnp.zeros(dk_ref.shape, F32)
            dv_ref[...] = jnp.zeros(dv_ref.shape, F32)

        for j0 in range(0, HEADS_PER_BLOCK, 2):
            kl = slice((j0 // KV_REP) * HEAD_DIM, (j0 // KV_REP + 1) * HEAD_DIM)
            heads = []
            for a in range(2):
                j = j0 + a
                ql = slice(j * HEAD_DIM, (j + 1) * HEAD_DIM)
                qv, ov = q_ref[:, ql], o_ref[:, ql]
                zv, dgv = z_ref[:, ql].astype(F32), dgz_ref[:, ql].astype(F32)
                dov = (dgv * _silu(zv)).astype(BF16)
                dz_ref[:, ql] = (dgv * ov * _silu_grad(zv)).astype(dz_ref.dtype)
                dl_v = jnp.sum(dov.astype(F32) * ov, axis=-1, keepdims=True)
                heads.append((ql, qv, dov, dl_v, lse_ref[0, :, j:j + 1]))
                acc_ref[a] = jnp.zeros((tq, HEAD_DIM), F32)

            def step(kc, c):
                off = pl.multiple_of(kc * tk, tk)
                kt = k_ref[pl.ds(off, tk), kl]
                vt = v_ref[pl.ds(off, tk), kl]
                dv_part, dk_part = None, None
                for a, (_, qv, dov, dl_v, lse_v) in enumerate(heads):
                    s = lax.dot_general(qv, kt, NT, preferred_element_type=F32)
                    p = jnp.exp(s - lse_v)
                    dp = lax.dot_general(dov, vt, NT, preferred_element_type=F32)
                    ds = (p * (dp - dl_v)).astype(BF16)
                    acc_ref[a] += jnp.dot(ds, kt, preferred_element_type=F32)
                    dvp = lax.dot_general(p.astype(BF16), dov, tn, preferred_element_type=F32)
                    dkp = lax.dot_general(ds, qv, tn, preferred_element_type=F32)
                    dv_part = dvp if dv_part is None else dv_part + dvp
                    dk_part = dkp if dk_part is None else dk_part + dkp
                dv_ref[pl.ds(off, tk), kl] += dv_part
                dk_ref[pl.ds(off, tk), kl] += dk_part
                return c

            lax.fori_loop(0, nkc, step, 0, unroll=2)
            for a, h in enumerate(heads):
                dq_ref[:, h[0]] = acc_ref[a].astype(dq_ref.dtype)

    nb = ATTN_W // Q_BLOCK_W
    qspec = pl.BlockSpec((tq, Q_BLOCK_W), lambda b, i: (i, b))
    kspec = pl.BlockSpec((n, LANES), lambda b, i: (0, b))
    cspec = pl.BlockSpec((1, tq, HEADS_PER_BLOCK), lambda b, i: (b, i, 0))
    return pl.pallas_call(
        body, grid=(nb, t // tq),
        in_specs=[qspec, kspec, kspec, qspec, pl.BlockSpec((tq, Q_BLOCK_W), lambda b, i: (i, nb + b)), qspec, cspec],
        out_specs=[qspec, kspec, kspec, qspec],
        out_shape=[jax.ShapeDtypeStruct((t, ATTN_W), BF16), jax.ShapeDtypeStruct((n, KV_W), F32),
                   jax.ShapeDtypeStruct((n, KV_W), F32), jax.ShapeDtypeStruct((t, ATTN_W), BF16)],
        scratch_shapes=[pltpu.VMEM((2, tq, HEAD_DIM), F32)],
        compiler_params=_params("parallel", "arbitrary"), name="attn_bwd")(q, k, v, dgz, proj, o, lse)


def _s5_system(p, n_lat, n_ctx):
    two_g = 2 * SSM_GROUPS
    a_re = p["ssm_a_re"].reshape(two_g, SSM_STATE)
    a_im = p["ssm_a_im"].reshape(two_g, SSM_STATE)
    log_dt = p["ssm_log_dt"].reshape(two_g, 1)
    b_re = p["ssm_b_re"].reshape(two_g, SSM_STATE, SSM_GROUP).transpose(2, 0, 1)
    b_im = p["ssm_b_im"].reshape(two_g, SSM_STATE, SSM_GROUP).transpose(2, 0, 1)
    raw = (a_re, a_im, log_dt, b_re, b_im)
    abr, abi, bbr, bbi, alr, ali, acr, aci = _s5_prep(*raw, n_lat // N_SEG, n_ctx // N_SEG)
    dirs = []
    for d in range(2):
        g = slice(d * SSM_GROUPS, (d + 1) * SSM_GROUPS)
        b_mat = _slab_in_matrix(bbr[:, g].transpose(1, 2, 0), bbi[:, g].transpose(1, 2, 0))
        c_mat = _slab_out_matrix(p["ssm_c_re"][0, d], p["ssm_c_im"][0, d])
        abar = _slab_pair(abr[g], abi[g])
        tables = {}
        for part, seg in (("lat", n_lat // N_SEG), ("ctx", n_ctx // N_SEG)):
            tables["h_" + part] = _s5_pow_table(f"s5_pow_h{d}_{part}", abar, seg, d == 0, False)
            tables["g_" + part] = _s5_pow_table(f"s5_pow_g{d}_{part}", abar, seg, d == 1, True)
        dirs.append(dict(
            b=b_mat.astype(BF16), bt=b_mat.transpose(0, 2, 1).astype(BF16), b32=b_mat,
            c=c_mat.astype(BF16), ct=c_mat.transpose(0, 2, 1).astype(BF16), ct32=c_mat.transpose(0, 2, 1),
            abar=abar, a_lat=_slab_pair(alr[g], ali[g])[:, 0], a_ctx=_slab_pair(acr[g], aci[g])[:, 0], **tables))
    return raw, dirs


def _s5_forward(proj, dirs, n_lat, n_ctx):
    n = n_lat + n_ctx
    zero_c = jnp.zeros((N_SLAB, STATE_W), F32)
    ys, saved = [], []
    for d, s in enumerate(dirs):
        desc = d == 1
        tag = f"s5f{d}"
        zc = _s5_ends(tag + "_ctx_ends", proj, n_ctx, n_lat, s["h_ctx"], s["b32"])
        ent_c, h0 = _s5_carry(tag + "_ctx_carry", zc, s["a_ctx"], zero_c, desc, False)
        y, hch_c = _s5_scan(tag + "_ctx", proj, n_ctx, n_lat, s["b"], s["c"], s["abar"], ent_c, desc,
                            y_rows=n)
        zl = _s5_ends(tag + "_lat_ends", proj, n_lat, 0, s["h_lat"], s["b32"])
        ent_l, _ = _s5_carry(tag + "_lat_carry", zl, s["a_lat"], h0, desc, False)
        y, hch_l = _s5_scan(tag + "_lat", proj, n_lat, 0, s["b"], s["c"], s["abar"], ent_l, desc,
                            y_alias=y, y_rows=n)
        ys.append(y)
        saved.append((hch_l, hch_c))
    return ys, saved


def _s5_backward(proj, dy, dirs, saved, n_lat, n_ctx):
    n = n_lat + n_ctx
    zero_c = jnp.zeros((N_SLAB, STATE_W), F32)
    out = []
    for d, s in enumerate(dirs):
        desc = d == 1
        tag = f"s5b{d}"
        hch_l, hch_c = saved[d]
        gl = _s5_ends(tag + "_lat_ends", dy, n_lat, 0, s["g_lat"], s["ct32"])
        ent_l, g0 = _s5_carry(tag + "_lat_carry", gl, s["a_lat"], zero_c, not desc, True)
        du, db_l, dc_l, da_l = _s5_scan_bwd(tag + "_lat", proj, dy, n_lat, 0, s["b"], s["bt"], s["ct"], s["abar"],
                                            hch_l, ent_l, desc, du_rows=n)
        gc = _s5_ends(tag + "_ctx_ends", dy, n_ctx, n_lat, s["g_ctx"], s["ct32"])
        ent_c, _ = _s5_carry(tag + "_ctx_carry", gc, s["a_ctx"], g0, not desc, True)
        du, db_c, dc_c, da_c = _s5_scan_bwd(tag + "_ctx", proj, dy, n_ctx, n_lat, s["b"], s["bt"], s["ct"], s["abar"],
                                            hch_c, ent_c, desc, du_alias=du, du_rows=n)
        out.append((du, db_l + db_c, dc_l + dc_c, da_l + da_c))
    return out


def _s5_param_grads(raw, bwd):
    dabr, dabi, dbbr, dbbi, dcr, dci = [], [], [], [], [], []
    for _, db, dc, da in bwd:
        da = jnp.sum(da, axis=1)
        dabr.append(da[:, :HALF_W].reshape(SSM_GROUPS, SSM_STATE))
        dabi.append(da[:, HALF_W:].reshape(SSM_GROUPS, SSM_STATE))
        dbd = _slab_diag(db)
        dbbr.append(dbd[0].transpose(1, 0, 2))
        dbbi.append(dbd[1].transpose(1, 0, 2))
        dcd = _slab_diag(dc)
        dcr.append(dcd[0])
        dci.append(-dcd[1])
    cat = lambda xs, ax: jnp.concatenate(xs, axis=ax)
    dar, dai, dld, dbr, dbi = _s5_prep_bwd(*raw, cat(dabr, 0), cat(dabi, 0), cat(dbbr, 1), cat(dbbi, 1))
    shp = (1, 2, SSM_GROUPS, SSM_STATE)
    b_shape = (1, 2, SSM_GROUPS, SSM_STATE, SSM_GROUP)
    return dict(
        ssm_a_re=dar.reshape(shp), ssm_a_im=dai.reshape(shp), ssm_log_dt=dld.reshape(1, 2, SSM_GROUPS),
        ssm_b_re=dbr.transpose(1, 2, 0).reshape(b_shape), ssm_b_im=dbi.transpose(1, 2, 0).reshape(b_shape),
        ssm_c_re=jnp.stack(dcr)[None], ssm_c_im=jnp.stack(dci)[None])


def _example_step(x, ctx, target, mods, w, p):
    t, c = x.shape[0], ctx.shape[0]
    n = t + c
    assert t % c == 0 and c % LANES == 0 and c % (8 * N_SEG) == 0 and t % GRID_W == 0
    tr = _largest_tile(c, 256, 8)
    xall = (x, ctx)
    g0, g1 = _vec(p["norm_g"][0]), _vec(p["norm_g"][1])
    (shift0, scale0, gate0), (shift1, scale1, gate1) = [tuple(_vec(v) for v in m) for m in mods]

    h0 = _norm_mod_fwd("l0_norm", xall, g0, scale0, shift0, n, tr, t)
    proj0 = _mm("l0_in", h0, w["ssm_w_in"], "nn")
    raw, dirs = _s5_system(p, t, c)
    (y_f, y_r), saved = _s5_forward(proj0, dirs, t, c)
    d_skip = _vec(p["ssm_d"][0])

    def post_a(u, yf, yr, dv):
        y = u * dv + yf + yr
        return y, _gelu(y)

    y0, yg = _rowwise("l0_gelu", post_a, n, tr, [(proj0, 0, D_MODEL), (y_f, 0, D_MODEL), (y_r, 0, D_MODEL)], [d_skip],
                      [(D_MODEL, BF16), (D_MODEL, BF16)], [])
    tg = _mm("l0_glu", yg, w["ssm_w_glu"], "nn", out_dtype=BF16)
    b_glu = _vec(p["ssm_b_glu"][0])

    def post_b(ygt, tt, zt, bv):
        return ygt * _sigmoid(tt + bv) * _silu(zt)

    gz0 = _rowwise("l0_gate", post_b, n, tr, [(yg, 0, D_MODEL), (tg, 0, D_MODEL), (proj0, 1, D_MODEL)], [b_glu],
                   [(D_MODEL, BF16)], [])[0]
    out0 = _mm("l0_out", gz0, w["ssm_w_out"], "nn")

    def res_norm(xt, ot, gv, g1v, sc, sh):
        x1t = xt + gv * ot
        xh, _ = _rms_hat(x1t)
        return x1t, (xh * g1v) * (1.0 + sc) + sh

    x1, h1 = _rowwise("l0_res_l1_norm", res_norm, n, tr, [_stream(xall), (out0, 0, D_MODEL)],
                      [gate0, g1, scale1, shift1], [(D_MODEL, F32), (D_MODEL, BF16)], [], n_lat=t)
    proj1 = _mm("l1_in", h1, w["attn_w_in"], "nn", out_dtype=BF16)
    cos, sin = _rope_tables(t, c)
    qn, kn = p["attn_q_norm"][0], p["attn_k_norm"][0]
    q_h, k_h, v_h = _qk_prep(proj1, qn, kn, cos, sin, n, tr)
    tq = _largest_tile(t, 512, LANES)
    o, lse, gz1 = _attn_fwd(q_h, k_h, v_h, proj1, t, tq, _largest_tile(n, 2816, LANES))
    out1 = _mm("l1_out", gz1, w["attn_w_out"], "nn")

    gf = _vec(p["final_norm_g"])

    def head(x1t, o1t, tgt, g1v, gfv):
        x2 = x1t + g1v * o1t
        xh, r = _rms_hat(x2)
        e = xh * gfv - tgt
        dyf = e * (1.0 / D_MODEL)
        dx2 = _rms_bwd(xh, r, dyf * gfv)
        return dx2, g1v * dx2, dyf * xh, dx2 * o1t, jnp.sum(e * e, axis=1, keepdims=True)

    gate1_lat = gate1[0:1]
    dx2, dout1, d_gf, d_gate1, sq = _rowwise(
        "head", head, t, tr, [(x1, 0, D_MODEL), (out1, 0, D_MODEL), (target, 0, D_MODEL)], [gate1_lat, gf],
        [(D_MODEL, F32), (D_MODEL, BF16)], [(1, D_MODEL), (1, D_MODEL), (1, 1)])

    d_w_attn_out = _mm("l1_out_dw", gz1, dout1, "tn", out_dtype=BF16)
    dgz1 = _mm("l1_out_dx", dout1, w["attn_w_out"], "nt", out_dtype=BF16)
    dq_s, dk, dv, dz1 = _attn_bwd(q_h, k_h, v_h, dgz1, proj1, o, lse, t, tq, _largest_tile(n, 1024, LANES))
    dproj1, d_qn, d_kn = _qk_prep_bwd(proj1, qn, kn, cos, sin, dq_s, dz1, dk, dv, n, t, tr)
    d_w_attn_in = _mm("l1_in_dw", h1, dproj1, "tn", out_dtype=BF16)
    dh1 = _mm("l1_in_dx", dproj1, w["attn_w_in"], "nt", out_dtype=BF16)
    dx1, dout0, d_g1, d_scale1, d_shift1, d_gate0 = _norm_mod_bwd("l1_norm_bwd", x1, g1, scale1, dh1, dx2, n, tr, t,
                                                                  prev=(out0, gate0))

    d_w_out = _mm("l0_out_dw", gz0, dout0, "tn", out_dtype=BF16)
    dgz0 = _mm("l0_out_dx", dout0, w["ssm_w_out"], "nt", out_dtype=BF16)

    def post_b_bwd(dgt, ygt, tt, zt, bv):
        s = _sigmoid(tt + bv)
        dy2 = dgt * _silu(zt)
        dt = dy2 * ygt * s * (1.0 - s)
        return dgt * (ygt * s) * _silu_grad(zt), dt, dy2 * s, dt

    dz0, dtg, dyg_a, d_b_glu = _rowwise(
        "l0_gate_bwd", post_b_bwd, n, tr, [(dgz0, 0, D_MODEL), (yg, 0, D_MODEL), (tg, 0, D_MODEL), (proj0, 1, D_MODEL)],
        [b_glu], [(D_MODEL, BF16), (D_MODEL, BF16), (D_MODEL, BF16)], [(1, D_MODEL)])
    d_w_glu = _mm("l0_glu_dw", yg, dtg, "tn", out_dtype=BF16)
    dyg_b = _mm("l0_glu_dx", dtg, w["ssm_w_glu"], "nt", out_dtype=BF16)

    def post_a_bwd(da, db, yt, ut, dv):
        dy = (da + db) * _gelu_grad(yt)
        return dy, dy * dv, dy * ut

    dy0, du_skip, d_d = _rowwise("l0_gelu_bwd", post_a_bwd, n, tr,
                                 [(dyg_a, 0, D_MODEL), (dyg_b, 0, D_MODEL), (y0, 0, D_MODEL), (proj0, 0, D_MODEL)], [d_skip],
                                 [(D_MODEL, F32), (D_MODEL, BF16)], [(1, D_MODEL)])
    s5_bwd = _s5_backward(proj0, dy0, dirs, saved, t, c)
    dproj0 = _rowwise("l0_in_grad", lambda a, b, cc, dz: jnp.concatenate([a + b + cc, dz], axis=1), n, tr,
                      [(du_skip, 0, D_MODEL), (s5_bwd[0][0], 0, D_MODEL), (s5_bwd[1][0], 0, D_MODEL), (dz0, 0, D_MODEL)], [],
                      [(2 * D_MODEL, BF16)], [])[0]
    d_w_in = _mm("l0_in_dw", h0, dproj0, "tn", out_dtype=BF16)
    dh0 = _mm("l0_in_dx", dproj0, w["ssm_w_in"], "nt", out_dtype=BF16)
    dx0, d_g0, d_scale0, d_shift0 = _norm_mod_bwd("l0_norm_bwd", xall, g0, scale0, dh0, dx1, n, tr, t, dx_lat_only=True)

    big = dict(ssm_w_in=d_w_in, ssm_w_glu=d_w_glu, ssm_w_out=d_w_out, attn_w_in=d_w_attn_in, attn_w_out=d_w_attn_out)
    small = dict(
        norm_g=jnp.concatenate([d_g0[0], d_g1[0]], axis=0), ssm_d=d_d[0], ssm_b_glu=d_b_glu[0],
        attn_q_norm=d_qn, attn_k_norm=d_kn, final_norm_g=d_gf[0, 0], **_s5_param_grads(raw, s5_bwd))
    zero_v = jnp.zeros((D_MODEL,), F32)
    d_mod_lat = jnp.stack([jnp.concatenate([d_shift0[0, 0], d_scale0[0, 0], d_gate0[0, 0]]),
                           jnp.concatenate([d_shift1[0, 0], d_scale1[0, 0], d_gate1[0, 0]])])
    d_mod_ctx = jnp.stack([jnp.concatenate([d_shift0[1, 0], d_scale0[1, 0], d_gate0[1, 0]]),
                           jnp.concatenate([d_shift1[1, 0], d_scale1[1, 0], zero_v])])
    return sq[0, 0, 0], dx0, big, small, d_mod_lat, d_mod_ctx


def _adamw(name, w, g, m, v):
    rows, cols = w.shape
    tr = _largest_tile(rows, 256, 8)
    c1 = 1.0 / (1.0 - ADAM_B1 ** ADAM_STEP)
    c2 = 1.0 / (1.0 - ADAM_B2 ** ADAM_STEP)

    def fn(wt, gt, mt, vt):
        mn = ADAM_B1 * mt + (1.0 - ADAM_B1) * gt
        vn = ADAM_B2 * vt + (1.0 - ADAM_B2) * (gt * gt)
        delta = -ADAM_LR * ((mn * c1) / (jnp.sqrt(vn * c2) + ADAM_EPS) + ADAM_WD * wt)
        return delta, mn, vn

    return _rowwise(name, fn, rows, tr, [(a, 0, cols) for a in (w, g, m, v)], [], [(cols, F32)] * 3, [])


BIG = ("ssm_w_in", "ssm_w_glu", "ssm_w_out", "attn_w_in", "attn_w_out")
COL_SHARDED = ("ssm_w_in", "attn_w_in")
WEIGHTS = ("c_ctx", "w_mod", "b_mod", "norm_g", "ssm_w_in", "ssm_a_re", "ssm_a_im", "ssm_log_dt", "ssm_b_re", "ssm_b_im",
           "ssm_c_re", "ssm_c_im", "ssm_d", "ssm_w_glu", "ssm_b_glu", "ssm_w_out", "attn_w_in", "attn_q_norm",
           "attn_k_norm", "attn_w_out", "final_norm_g")
SMALL = tuple(k for k in WEIGHTS if k not in BIG and k != "w_mod")
PACK_W = 1024
COND_ROWS = 2 * N_DEV


def _attn_in_perm(x, inverse):
    a, kv = ATTN_W, 2 * KV_W
    if inverse:
        return jnp.concatenate([x[..., :a], x[..., 2 * a:], x[..., a:2 * a]], axis=-1)
    return jnp.concatenate([x[..., :a], x[..., a + kv:], x[..., a:a + kv]], axis=-1)


def _pack(arrays, dtype, row_unit):
    flat = jnp.concatenate([a.reshape(-1).astype(dtype) for a in arrays])
    rows = -(-flat.shape[0] // PACK_W)
    rows = -(-rows // row_unit) * row_unit
    flat = jnp.concatenate([flat, jnp.zeros((rows * PACK_W - flat.shape[0],), dtype)])
    return flat.reshape(rows, PACK_W)


def _unpack(buf, shapes):
    lead = buf.shape[:-2]
    flat = buf.reshape(lead + (-1,))
    out, off = [], 0
    for shp in shapes:
        size = math.prod(shp)
        out.append(flat[..., off:off + size].reshape(lead + tuple(shp)))
        off += size
    return out


def kernel(x, c, ctx, c_ctx, w_mod, b_mod, norm_g, ssm_w_in, ssm_a_re, ssm_a_im, ssm_log_dt, ssm_b_re, ssm_b_im, ssm_c_re, ssm_c_im, ssm_d, ssm_w_glu, ssm_b_glu, ssm_w_out, attn_w_in, attn_q_norm, attn_k_norm, attn_w_out, final_norm_g, loss_target, m_c_ctx, m_w_mod, m_b_mod, m_norm_g, m_ssm_w_in, m_ssm_a_re, m_ssm_a_im, m_ssm_log_dt, m_ssm_b_re, m_ssm_b_im, m_ssm_c_re, m_ssm_c_im, m_ssm_d, m_ssm_w_glu, m_ssm_b_glu, m_ssm_w_out, m_attn_w_in, m_attn_q_norm, m_attn_k_norm, m_attn_w_out, m_final_norm_g, v_c_ctx, v_w_mod, v_b_mod, v_norm_g, v_ssm_w_in, v_ssm_a_re, v_ssm_a_im, v_ssm_log_dt, v_ssm_b_re, v_ssm_b_im, v_ssm_c_re, v_ssm_c_im, v_ssm_d, v_ssm_w_glu, v_ssm_b_glu, v_ssm_w_out, v_attn_w_in, v_attn_q_norm, v_attn_k_norm, v_attn_w_out, v_final_norm_g):
    args = dict(locals())
    wts = {k: args[k] for k in WEIGHTS}
    mom_m = {k: args["m_" + k] for k in WEIGHTS}
    mom_v = {k: args["v_" + k] for k in WEIGHTS}
    mx, my, mc = lax.axis_index("x"), lax.axis_index("y"), lax.axis_index("c")
    chip = 2 * mx + my
    me = 2 * chip + mc

    halves = []
    for k in BIG:
        sh = wts[k][0]
        hr = sh.shape[0] // 2
        halves.append(lax.dynamic_slice_in_dim(sh, mc * hr, hr, axis=0))
    gathered = _gather_two_level("gather_weights", _pack(halves, BF16, 16))
    parts = _unpack(gathered, [h.shape for h in halves])
    w_full = {}
    for k, pc in zip(BIG, parts):
        hr, cols = pc.shape[1:]
        pc = pc.reshape(N_CHIP, 2, hr, cols)
        if k in COL_SHARDED:
            w_full[k] = pc.transpose(1, 2, 0, 3).reshape(2 * hr, N_CHIP * cols)
        else:
            w_full[k] = pc.reshape(N_CHIP * 2 * hr, cols)
    w_full["attn_w_in"] = _attn_in_perm(w_full["attn_w_in"], False)

    c_blk = jnp.concatenate([c, jnp.zeros((N_DEV - 1, D_MODEL), F32)], axis=0)
    c_all = _exchange("gather_c", c_blk, True)[:, 0]
    cond = jnp.concatenate([c_all, c_ctx[None], jnp.zeros((COND_ROWS - N_DEV - 1, D_MODEL), F32)], axis=0)
    s_cond, ds_cond = _rowwise("cond_silu", lambda t: (_silu(t), _silu_grad(t)), COND_ROWS, COND_ROWS, [(cond, 0, D_MODEL)], [],
                               [(D_MODEL, F32), (D_MODEL, F32)], [])
    w_mod_b = w_mod.astype(BF16)
    mcols = w_mod.shape[2]
    mod_part = jnp.stack([_mm(f"mod{i}", s_cond, w_mod_b[i], "nn") for i in range(2)])
    mod_g = _exchange("gather_mod", mod_part.reshape(2 * COND_ROWS, mcols), True)
    mod_all = mod_g.reshape(N_CHIP, 2, 2, COND_ROWS, mcols)[:, 0]
    mod_all = mod_all.transpose(1, 2, 0, 3).reshape(2, COND_ROWS, N_CHIP * mcols) + b_mod[:, None, :]
    mods = []
    for i in range(2):
        lat = lax.dynamic_slice_in_dim(mod_all[i], me, 1, axis=0)[0]
        both = jnp.stack([lat, mod_all[i, N_DEV]])
        mods.append((both[:, :D_MODEL], both[:, D_MODEL:2 * D_MODEL], both[:, 2 * D_MODEL:]))

    small_p = {k: wts[k] for k in SMALL if k != "c_ctx" and k != "b_mod"}
    sq, grad_x, big_g, small_g, d_mod_lat, d_mod_ctx = _example_step(x[0], ctx[0], loss_target[0], mods, w_full, small_p)
    loss = lax.psum(0.5 / D_MODEL * sq, ("x", "y", "c"))
    big_g["attn_w_in"] = _attn_in_perm(big_g["attn_w_in"], True)

    small_names = [k for k in SMALL if k not in ("c_ctx", "b_mod")]
    small_list = [small_g[k] for k in small_names] + [d_mod_lat, d_mod_ctx]
    small_shapes = [wts[k].shape for k in small_names] + [d_mod_lat.shape, d_mod_ctx.shape]
    packed = _pack(small_list, F32, 8 * N_DEV)
    slice_rows = packed.shape[0] // N_DEV
    slices = _exchange("scatter_small", packed.reshape(N_DEV, slice_rows, PACK_W), False)
    my_sum = _sum_slots("sum_small", slices)
    payload = jnp.concatenate([my_sum, _pack([d_mod_lat], F32, 8)], axis=0)
    sg = _exchange("gather_small", payload, True)
    summed = _unpack(sg[:, :slice_rows].reshape(packed.shape), small_shapes)
    grads = dict(zip(small_names, summed[:-2]))
    d_mod_lat_sum, d_mod_ctx_sum = summed[-2], summed[-1]
    grads["b_mod"] = d_mod_lat_sum + d_mod_ctx_sum
    d_mod_lat_all = _unpack(sg[:, slice_rows:], [d_mod_lat.shape])[0]

    g_w_mod, ds_cc = [], []
    for i in range(2):
        rows9 = jnp.concatenate([d_mod_lat_all[:, i], d_mod_ctx_sum[i][None],
                                 jnp.zeros((COND_ROWS - N_DEV - 1, 3 * D_MODEL), F32)], axis=0)
        mine = lax.dynamic_slice_in_dim(rows9, chip * mcols, mcols, axis=1)
        g_w_mod.append(_mm(f"mod{i}_dw", s_cond, mine, "tn"))
        ds_cc.append(_mm(f"mod{i}_dx", mine, w_mod_b[i], "nt")[N_DEV])
    grads["w_mod"] = jnp.stack(g_w_mod)
    part = (ds_cc[0] + ds_cc[1]) * jnp.where(mc == 0, 1.0, 0.0)
    part_blk = jnp.concatenate([part[None], jnp.zeros((N_DEV - 1, D_MODEL), F32)], axis=0)
    ds_all = _sum_slots("sum_c_ctx", _exchange("gather_c_ctx", part_blk, True))
    grads["c_ctx"] = ds_all[0] * ds_cond[N_DEV]

    blocks = []
    for k in BIG:
        g = big_g[k]
        rows, cols = g.shape
        if k in COL_SHARDED:
            blocks.append(g.reshape(2, rows // 2, N_CHIP, cols // N_CHIP).transpose(2, 0, 1, 3).reshape(N_DEV, -1))
        else:
            blocks.append(g.reshape(N_DEV, -1))
    sendbuf = jnp.concatenate(blocks, axis=1).astype(BF16)
    sendbuf = sendbuf.reshape(N_DEV, -1, PACK_W)
    recv = _exchange("scatter_big", sendbuf, False)
    mine = _sum_slots("sum_big", recv)
    both = _exchange("swap_halves", mine, True, sibling_only=True)
    half_shapes = [(wts[k].shape[1] // 2, wts[k].shape[2]) for k in BIG]
    for k, pc in zip(BIG, _unpack(both, half_shapes)):
        grads[k] = pc.reshape(wts[k].shape)

    delta, new_m, new_v = {}, {}, {}
    for k in BIG + ("w_mod",):
        shp = wts[k].shape
        two_d = (-1, shp[-1])
        res = _adamw("adamw_" + k, *[a.reshape(two_d) for a in (wts[k], grads[k], mom_m[k], mom_v[k])])
        delta[k], new_m[k], new_v[k] = [r.reshape(shp) for r in res]
    shapes = [wts[k].shape for k in SMALL]
    packed = [_pack([d[k] for k in SMALL], F32, 8) for d in (wts, grads, mom_m, mom_v)]
    res = _adamw("adamw_small", *packed)
    for dst, buf in zip((delta, new_m, new_v), res):
        for k, a in zip(SMALL, _unpack(buf, shapes)):
            dst[k] = a
    grads = {k: grads[k].reshape(wts[k].shape) for k in WEIGHTS}
    return (loss, grad_x[None], *[grads[k] for k in WEIGHTS], *[delta[k] for k in WEIGHTS],
            *[new_m[k] for k in WEIGHTS], *[new_v[k] for k in WEIGHTS])
```
